```python
import jax, jax.numpy as jnp
from jax import lax
import numpy as np

D_MODEL = 1024
BATCH = 8
SEQ = 2048
DEPTH = 2

N_HEADS = 16
HEAD_DIM = D_MODEL // N_HEADS
N_MIXERS = 2
GRID_W = 64
NA_ROWS = 8
NA_COLS = 16
DIL_GROUPS = ((128, 1), (512, 4), (2048, 16))
N_GROUPS = len(DIL_GROUPS)
BAND_BLOCK = 128
D_FF = -(-8 * D_MODEL // (3 * 256)) * 256
RMS_EPS = 1e-6
NEG_INF = -1e30
N_A_LAYERS = (DEPTH + 1) // 2
N_B_LAYERS = DEPTH // 2

kernel_name = "hybrid_natten_dilated_encoder"


def rms_norm(x, g):
    xf = x.astype(jnp.float32)
    y = xf * lax.rsqrt(jnp.mean(xf * xf, axis=-1, keepdims=True) + RMS_EPS)
    return (y * g.astype(jnp.float32)).astype(x.dtype)


def alibi_slopes(n):
    return 2.0 ** (-8.0 * jnp.arange(1, n + 1, dtype=jnp.float32) / n)


def swiglu(x, w_gate, w_up, w_down):
    return (jax.nn.silu(x @ w_gate) * (x @ w_up)) @ w_down


def neighbourhood_attention(x, w_qkv, w_o, rpb):
    b, s, _ = x.shape
    rows = s // GRID_W
    kh = min(NA_ROWS, rows)
    qkv = (x @ w_qkv).reshape(b, rows, GRID_W, 3, N_HEADS, HEAD_DIM)
    q, k, v = (jnp.transpose(qkv[:, :, :, i], (0, 3, 1, 2, 4)) for i in range(3))
    q = q * HEAD_DIM ** -0.5
    col = jnp.arange(GRID_W)
    col_start = jnp.clip(col - NA_COLS // 2, 0, GRID_W - NA_COLS)
    col_mask = (col[None, :] >= col_start[:, None]) & (col[None, :] < col_start[:, None] + NA_COLS)
    col_idx = jnp.clip(col[None, :] - col[:, None] + NA_COLS - 1, 0, 2 * NA_COLS - 2)
    rpb_cols = rpb.astype(jnp.float32)[:, :, col_idx]

    def row_block(i):
        rs = jnp.clip(i - kh // 2, 0, rows - kh)
        qi = lax.dynamic_index_in_dim(q, i, axis=2, keepdims=False)
        kr = lax.dynamic_slice_in_dim(k, rs, kh, axis=2)
        vr = lax.dynamic_slice_in_dim(v, rs, kh, axis=2)
        bias = lax.dynamic_slice_in_dim(rpb_cols, rs - i + NA_ROWS - 1, kh, axis=1)
        sc = jnp.einsum('bhqd,bhrkd->bhqrk', qi, kr).astype(jnp.float32)
        sc = sc + jnp.transpose(bias, (0, 2, 1, 3))[None]
        sc = jnp.where(col_mask[:, None, :], sc, NEG_INF)
        p = jax.nn.softmax(sc.reshape(b, N_HEADS, GRID_W, kh * GRID_W), axis=-1).reshape(sc.shape)
        return jnp.einsum('bhqrk,bhrkd->bhqd', p.astype(vr.dtype), vr)

    o = lax.map(row_block, jnp.arange(rows))
    o = jnp.transpose(o, (1, 0, 3, 2, 4)).reshape(b, s, D_MODEL)
    return o @ w_o


def banded_attention(q, k, v, radius, slope_dist):
    n, h, l, dh = q.shape
    nb = -(-l // BAND_BLOCK)
    lp = nb * BAND_BLOCK
    qb = jnp.pad(q, ((0, 0), (0, 0), (0, lp - l), (0, 0))).reshape(n, h, nb, BAND_BLOCK, dh)

    def key_blocks(t):
        tp = jnp.pad(t, ((0, 0), (0, 0), (radius, lp - l + BAND_BLOCK - radius), (0, 0)))
        tp = tp.reshape(n, h, nb + 1, BAND_BLOCK, dh)
        return jnp.concatenate([tp[:, :, :-1], tp[:, :, 1:]], axis=3)

    kb, vb = key_blocks(k), key_blocks(v)
    qi = jnp.arange(lp).reshape(nb, BAND_BLOCK)
    kj = jnp.arange(nb)[:, None] * BAND_BLOCK - radius + jnp.arange(2 * BAND_BLOCK)[None, :]
    dist = jnp.abs(qi[:, :, None] - kj[:, None, :])
    valid = (dist <= radius) & (kj[:, None, :] >= 0) & (kj[:, None, :] < l)
    sc = jnp.einsum('nhbqd,nhbkd->nhbqk', qb, kb).astype(jnp.float32)
    sc = sc - slope_dist[None, :, None, None, None] * dist.astype(jnp.float32)
    sc = jnp.where(valid, sc, NEG_INF)
    lse = jax.nn.logsumexp(sc, axis=-1)
    p = jnp.exp(sc - lse[..., None])
    o = jnp.einsum('nhbqk,nhbkd->nhbqd', p.astype(vb.dtype), vb)
    return o.reshape(n, h, lp, dh)[:, :, :l], lse.reshape(n, h, lp)[:, :, :l]


def dilated_attention(x, w_qkv, w_o):
    b, s, _ = x.shape
    qkv = (x @ w_qkv).reshape(b, s, N_GROUPS, 3, N_HEADS, HEAD_DIM)
    slopes = alibi_slopes(N_HEADS)
    outs, lses = [], []
    for g, (window, dil) in enumerate(DIL_GROUPS):
        radius = window // (2 * dil)
        l = s // dil

        def to_sub(t):
            t = jnp.transpose(t.reshape(b, l, dil, N_HEADS, HEAD_DIM), (0, 2, 3, 1, 4))
            return t.reshape(b * dil, N_HEADS, l, HEAD_DIM)

        q, k, v = (to_sub(qkv[:, :, g, i]) for i in range(3))
        o, lse = banded_attention(q * HEAD_DIM ** -0.5, k, v, radius, slopes * dil)
        o = jnp.transpose(o.reshape(b, dil, N_HEADS, l, HEAD_DIM), (0, 3, 1, 2, 4)).reshape(b, s, N_HEADS, HEAD_DIM)
        lse = jnp.transpose(lse.reshape(b, dil, N_HEADS, l), (0, 3, 1, 2)).reshape(b, s, N_HEADS)
        outs.append(o)
        lses.append(lse)
    alpha = jax.nn.softmax(jnp.stack(lses, axis=0), axis=0)
    o = jnp.einsum('gbsh,gbshd->bshd', alpha, jnp.stack(outs, axis=0).astype(jnp.float32)).astype(x.dtype)
    return o.reshape(b, s, D_MODEL) @ w_o


def _fwd_setup_inputs(seed: int = 0) -> dict:
    key = jax.random.key(seed)
    ks = jax.random.split(key, 14)
    d = D_MODEL

    def w(k, shape, fan_in):
        return jax.random.normal(k, shape, jnp.float32) * fan_in ** -0.5

    def gain(k):
        return 1.0 + 0.02 * jax.random.normal(k, (DEPTH, d), jnp.float32)

    return {
        "x": jax.random.normal(ks[0], (BATCH, SEQ, d), jnp.float32),
        "norm_mix_pre": gain(ks[1]),
        "norm_mix_post": gain(ks[2]),
        "norm_ffn_pre": gain(ks[3]),
        "norm_ffn_post": gain(ks[4]),
        "na_w_qkv": w(ks[5], (N_A_LAYERS, d, 3 * d), d),
        "na_w_o": w(ks[6], (N_A_LAYERS, d, d), d),
        "na_rpb": 0.5 * jax.random.normal(ks[7], (N_A_LAYERS, N_HEADS, 2 * NA_ROWS - 1, 2 * NA_COLS - 1), jnp.float32),
        "dil_w_qkv": w(ks[8], (N_B_LAYERS, d, N_GROUPS * 3 * d), d),
        "dil_w_o": w(ks[9], (N_B_LAYERS, d, d), d),
        "ffn_w_gate": w(ks[10], (DEPTH, d, D_FF), d),
        "ffn_w_up": w(ks[11], (DEPTH, d, D_FF), d),
        "ffn_w_down": w(ks[12], (DEPTH, D_FF, d), D_FF),
    }


def _fwd_reference(x, norm_mix_pre, norm_mix_post, norm_ffn_pre, norm_ffn_post, na_w_qkv, na_w_o, na_rpb,
              dil_w_qkv, dil_w_o, ffn_w_gate, ffn_w_up, ffn_w_down):
    for layer in range(DEPTH):
        j = layer // N_MIXERS
        h = rms_norm(x, norm_mix_pre[layer])
        if layer % N_MIXERS == 0:
            h = neighbourhood_attention(h, na_w_qkv[j], na_w_o[j], na_rpb[j])
        else:
            h = dilated_attention(h, dil_w_qkv[j], dil_w_o[j])
        x = x + rms_norm(h, norm_mix_post[layer])
        h = rms_norm(x, norm_ffn_pre[layer])
        x = x + rms_norm(swiglu(h, ffn_w_gate[layer], ffn_w_up[layer], ffn_w_down[layer]), norm_ffn_post[layer])
    return x


import jax as _jax
import jax.numpy as _jnp

TWIN_FORMAT = 'train_step'
FWD_PARAMS = ['x', 'norm_mix_pre', 'norm_mix_post', 'norm_ffn_pre', 'norm_ffn_post', 'na_w_qkv', 'na_w_o', 'na_rpb', 'dil_w_qkv', 'dil_w_o', 'ffn_w_gate', 'ffn_w_up', 'ffn_w_down']
TWIN_WEIGHTS = ['norm_mix_pre', 'norm_mix_post', 'norm_ffn_pre', 'norm_ffn_post', 'na_w_qkv', 'na_w_o', 'na_rpb', 'dil_w_qkv', 'dil_w_o', 'ffn_w_gate', 'ffn_w_up', 'ffn_w_down']
TWIN_DIFF_INPUT = 'x'
TWIN_INPUTS = ['x', 'norm_mix_pre', 'norm_mix_post', 'norm_ffn_pre', 'norm_ffn_post', 'na_w_qkv', 'na_w_o', 'na_rpb', 'dil_w_qkv', 'dil_w_o', 'ffn_w_gate', 'ffn_w_up', 'ffn_w_down', 'loss_target', 'm_norm_mix_pre', 'm_norm_mix_post', 'm_norm_ffn_pre', 'm_norm_ffn_post', 'm_na_w_qkv', 'm_na_w_o', 'm_na_rpb', 'm_dil_w_qkv', 'm_dil_w_o', 'm_ffn_w_gate', 'm_ffn_w_up', 'm_ffn_w_down', 'v_norm_mix_pre', 'v_norm_mix_post', 'v_norm_ffn_pre', 'v_norm_ffn_post', 'v_na_w_qkv', 'v_na_w_o', 'v_na_rpb', 'v_dil_w_qkv', 'v_dil_w_o', 'v_ffn_w_gate', 'v_ffn_w_up', 'v_ffn_w_down']
TWIN_OUTPUTS = ['loss', 'grad_x', 'grad_norm_mix_pre', 'grad_norm_mix_post', 'grad_norm_ffn_pre', 'grad_norm_ffn_post', 'grad_na_w_qkv', 'grad_na_w_o', 'grad_na_rpb', 'grad_dil_w_qkv', 'grad_dil_w_o', 'grad_ffn_w_gate', 'grad_ffn_w_up', 'grad_ffn_w_down', 'delta_norm_mix_pre', 'delta_norm_mix_post', 'delta_norm_ffn_pre', 'delta_norm_ffn_post', 'delta_na_w_qkv', 'delta_na_w_o', 'delta_na_rpb', 'delta_dil_w_qkv', 'delta_dil_w_o', 'delta_ffn_w_gate', 'delta_ffn_w_up', 'delta_ffn_w_down', 'new_m_norm_mix_pre', 'new_m_norm_mix_post', 'new_m_norm_ffn_pre', 'new_m_norm_ffn_post', 'new_m_na_w_qkv', 'new_m_na_w_o', 'new_m_na_rpb', 'new_m_dil_w_qkv', 'new_m_dil_w_o', 'new_m_ffn_w_gate', 'new_m_ffn_w_up', 'new_m_ffn_w_down', 'new_v_norm_mix_pre', 'new_v_norm_mix_post', 'new_v_norm_ffn_pre', 'new_v_norm_ffn_post', 'new_v_na_w_qkv', 'new_v_na_w_o', 'new_v_na_rpb', 'new_v_dil_w_qkv', 'new_v_dil_w_o', 'new_v_ffn_w_gate', 'new_v_ffn_w_up', 'new_v_ffn_w_down']
TWIN_LEAF_KINDS = {'loss': 'loss', 'grad_x': 'grad_x', 'grad_norm_mix_pre': 'grad_w', 'grad_norm_mix_post': 'grad_w', 'grad_norm_ffn_pre': 'grad_w', 'grad_norm_ffn_post': 'grad_w', 'grad_na_w_qkv': 'grad_w', 'grad_na_w_o': 'grad_w', 'grad_na_rpb': 'grad_w', 'grad_dil_w_qkv': 'grad_w', 'grad_dil_w_o': 'grad_w', 'grad_ffn_w_gate': 'grad_w', 'grad_ffn_w_up': 'grad_w', 'grad_ffn_w_down': 'grad_w', 'delta_norm_mix_pre': 'delta_w', 'delta_norm_mix_post': 'delta_w', 'delta_norm_ffn_pre': 'delta_w', 'delta_norm_ffn_post': 'delta_w', 'delta_na_w_qkv': 'delta_w', 'delta_na_w_o': 'delta_w', 'delta_na_rpb': 'delta_w', 'delta_dil_w_qkv': 'delta_w', 'delta_dil_w_o': 'delta_w', 'delta_ffn_w_gate': 'delta_w', 'delta_ffn_w_up': 'delta_w', 'delta_ffn_w_down': 'delta_w', 'new_m_norm_mix_pre': 'new_m', 'new_m_norm_mix_post': 'new_m', 'new_m_norm_ffn_pre': 'new_m', 'new_m_norm_ffn_post': 'new_m', 'new_m_na_w_qkv': 'new_m', 'new_m_na_w_o': 'new_m', 'new_m_na_rpb': 'new_m', 'new_m_dil_w_qkv': 'new_m', 'new_m_dil_w_o': 'new_m', 'new_m_ffn_w_gate': 'new_m', 'new_m_ffn_w_up': 'new_m', 'new_m_ffn_w_down': 'new_m', 'new_v_norm_mix_pre': 'new_v', 'new_v_norm_mix_post': 'new_v', 'new_v_norm_ffn_pre': 'new_v', 'new_v_norm_ffn_post': 'new_v', 'new_v_na_w_qkv': 'new_v', 'new_v_na_w_o': 'new_v', 'new_v_na_rpb': 'new_v', 'new_v_dil_w_qkv': 'new_v', 'new_v_dil_w_o': 'new_v', 'new_v_ffn_w_gate': 'new_v', 'new_v_ffn_w_up': 'new_v', 'new_v_ffn_w_down': 'new_v'}


def _forward(args):
    return _fwd_reference(*[args[k] for k in FWD_PARAMS])


def _output_shape():
    out = _jax.eval_shape(lambda: _forward(_fwd_setup_inputs(0)))
    return out.shape, out.dtype

N_MICROBATCH = 1
ADAM_LR = 0.001
ADAM_B1 = 0.9
ADAM_B2 = 0.999
ADAM_EPS = 1e-08
ADAM_WD = 0.01
ADAM_STEP = 10
PER_EXAMPLE_BATCH_AXIS = {'x': 0, 'loss_target': 0}
SHARED_INPUTS = []
_WEIGHT_DTYPES = {'norm_mix_pre': _jnp.float32, 'norm_mix_post': _jnp.float32, 'norm_ffn_pre': _jnp.float32, 'norm_ffn_post': _jnp.float32, 'na_w_qkv': _jnp.float32, 'na_w_o': _jnp.float32, 'na_rpb': _jnp.float32, 'dil_w_qkv': _jnp.float32, 'dil_w_o': _jnp.float32, 'ffn_w_gate': _jnp.float32, 'ffn_w_up': _jnp.float32, 'ffn_w_down': _jnp.float32}
MOMENT_SCALE = {'norm_mix_pre': 1.161728e+00, 'norm_mix_post': 1.569533e+01, 'norm_ffn_pre': 6.437807e-01, 'norm_ffn_post': 1.601538e+01, 'na_w_qkv': 8.066882e-01, 'na_w_o': 1.102646e+00, 'na_rpb': 2.125676e-01, 'dil_w_qkv': 2.935735e-01, 'dil_w_o': 8.141375e-01, 'ffn_w_gate': 2.557400e-01, 'ffn_w_up': 3.153192e-01, 'ffn_w_down': 5.294231e-01}


def _to_microbatches(a, axis):
    t = _jnp.moveaxis(a, axis, 0)
    t = t.reshape((N_MICROBATCH, t.shape[0] // N_MICROBATCH) + t.shape[1:])
    return _jnp.moveaxis(t, 1, axis + 1)


def setup_inputs(seed: int = 0) -> dict:
    inp = _fwd_setup_inputs(seed)
    key = _jax.random.fold_in(_jax.random.key(seed), 7919)
    shape, _ = _output_shape()
    out = dict(inp)
    out["loss_target"] = _jax.random.normal(_jax.random.fold_in(key, 0), shape, _jnp.float32)
    for i, name in enumerate(TWIN_WEIGHTS):
        w = inp[name].astype(_jnp.float32)
        if MOMENT_SCALE is None:
            s = _jnp.sqrt(_jnp.mean(_jnp.square(w)) + 1e-30)
        else:
            s = MOMENT_SCALE[name]
        km, kv = _jax.random.split(_jax.random.fold_in(key, i + 1))
        out[name] = w
        out["m_" + name] = s * _jax.random.normal(km, w.shape, _jnp.float32)
        out["v_" + name] = (s * s) * _jax.random.uniform(kv, w.shape, _jnp.float32, 0.5, 1.5)
    if N_MICROBATCH > 1:
        for name, axis in PER_EXAMPLE_BATCH_AXIS.items():
            out[name] = _to_microbatches(out[name], axis)
    return {'x': out['x'], 'norm_mix_pre': out['norm_mix_pre'], 'norm_mix_post': out['norm_mix_post'], 'norm_ffn_pre': out['norm_ffn_pre'], 'norm_ffn_post': out['norm_ffn_post'], 'na_w_qkv': out['na_w_qkv'], 'na_w_o': out['na_w_o'], 'na_rpb': out['na_rpb'], 'dil_w_qkv': out['dil_w_qkv'], 'dil_w_o': out['dil_w_o'], 'ffn_w_gate': out['ffn_w_gate'], 'ffn_w_up': out['ffn_w_up'], 'ffn_w_down': out['ffn_w_down'], 'loss_target': out['loss_target'], 'm_norm_mix_pre': out['m_norm_mix_pre'], 'm_norm_mix_post': out['m_norm_mix_post'], 'm_norm_ffn_pre': out['m_norm_ffn_pre'], 'm_norm_ffn_post': out['m_norm_ffn_post'], 'm_na_w_qkv': out['m_na_w_qkv'], 'm_na_w_o': out['m_na_w_o'], 'm_na_rpb': out['m_na_rpb'], 'm_dil_w_qkv': out['m_dil_w_qkv'], 'm_dil_w_o': out['m_dil_w_o'], 'm_ffn_w_gate': out['m_ffn_w_gate'], 'm_ffn_w_up': out['m_ffn_w_up'], 'm_ffn_w_down': out['m_ffn_w_down'], 'v_norm_mix_pre': out['v_norm_mix_pre'], 'v_norm_mix_post': out['v_norm_mix_post'], 'v_norm_ffn_pre': out['v_norm_ffn_pre'], 'v_norm_ffn_post': out['v_norm_ffn_post'], 'v_na_w_qkv': out['v_na_w_qkv'], 'v_na_w_o': out['v_na_w_o'], 'v_na_rpb': out['v_na_rpb'], 'v_dil_w_qkv': out['v_dil_w_qkv'], 'v_dil_w_o': out['v_dil_w_o'], 'v_ffn_w_gate': out['v_ffn_w_gate'], 'v_ffn_w_up': out['v_ffn_w_up'], 'v_ffn_w_down': out['v_ffn_w_down']}


def _loss(weights, diff, rest, loss_target):
    with _jax.named_scope("forward"):
        args = {**rest, TWIN_DIFF_INPUT: diff, **{k: w.astype(_WEIGHT_DTYPES[k]) for k, w in weights.items()}}
        y = _forward(args)
    with _jax.named_scope("loss_head"):
        err = _jnp.square(y.astype(_jnp.float32) - loss_target)
        return 0.5 * _jnp.sum(_jnp.mean(err, axis=-1)) if err.ndim else 0.5 * err


def _adamw(w, g, m, v):
    m = ADAM_B1 * m + (1.0 - ADAM_B1) * g
    v = ADAM_B2 * v + (1.0 - ADAM_B2) * _jnp.square(g)
    m_hat = m / (1.0 - ADAM_B1 ** ADAM_STEP)
    v_hat = v / (1.0 - ADAM_B2 ** ADAM_STEP)
    delta = -ADAM_LR * (m_hat / (_jnp.sqrt(v_hat) + ADAM_EPS) + ADAM_WD * w)
    return delta, m, v


def reference(x, norm_mix_pre, norm_mix_post, norm_ffn_pre, norm_ffn_post, na_w_qkv, na_w_o, na_rpb, dil_w_qkv, dil_w_o, ffn_w_gate, ffn_w_up, ffn_w_down, loss_target, m_norm_mix_pre, m_norm_mix_post, m_norm_ffn_pre, m_norm_ffn_post, m_na_w_qkv, m_na_w_o, m_na_rpb, m_dil_w_qkv, m_dil_w_o, m_ffn_w_gate, m_ffn_w_up, m_ffn_w_down, v_norm_mix_pre, v_norm_mix_post, v_norm_ffn_pre, v_norm_ffn_post, v_na_w_qkv, v_na_w_o, v_na_rpb, v_dil_w_qkv, v_dil_w_o, v_ffn_w_gate, v_ffn_w_up, v_ffn_w_down):
    given = dict(x=x, norm_mix_pre=norm_mix_pre, norm_mix_post=norm_mix_post, norm_ffn_pre=norm_ffn_pre, norm_ffn_post=norm_ffn_post, na_w_qkv=na_w_qkv, na_w_o=na_w_o, na_rpb=na_rpb, dil_w_qkv=dil_w_qkv, dil_w_o=dil_w_o, ffn_w_gate=ffn_w_gate, ffn_w_up=ffn_w_up, ffn_w_down=ffn_w_down, loss_target=loss_target, m_norm_mix_pre=m_norm_mix_pre, m_norm_mix_post=m_norm_mix_post, m_norm_ffn_pre=m_norm_ffn_pre, m_norm_ffn_post=m_norm_ffn_post, m_na_w_qkv=m_na_w_qkv, m_na_w_o=m_na_w_o, m_na_rpb=m_na_rpb, m_dil_w_qkv=m_dil_w_qkv, m_dil_w_o=m_dil_w_o, m_ffn_w_gate=m_ffn_w_gate, m_ffn_w_up=m_ffn_w_up, m_ffn_w_down=m_ffn_w_down, v_norm_mix_pre=v_norm_mix_pre, v_norm_mix_post=v_norm_mix_post, v_norm_ffn_pre=v_norm_ffn_pre, v_norm_ffn_post=v_norm_ffn_post, v_na_w_qkv=v_na_w_qkv, v_na_w_o=v_na_w_o, v_na_rpb=v_na_rpb, v_dil_w_qkv=v_dil_w_qkv, v_dil_w_o=v_dil_w_o, v_ffn_w_gate=v_ffn_w_gate, v_ffn_w_up=v_ffn_w_up, v_ffn_w_down=v_ffn_w_down)
    weights = {n: given[n] for n in TWIN_WEIGHTS}
    shared = {n: given[n] for n in SHARED_INPUTS}
    per_example = {n: given[n] for n in ['x']}
    grad_fn = _jax.value_and_grad(_loss, argnums=(0, 1))

    def one_microbatch(ex, loss_target):
        ex = dict(ex)
        diff = ex.pop(TWIN_DIFF_INPUT)
        return grad_fn(weights, diff, {**shared, **ex}, loss_target)

    if N_MICROBATCH == 1:
        loss, (grad_w, grad_x) = one_microbatch(per_example, given["loss_target"])
    else:
        def body(carry, xs):
            loss_sum, grad_sum = carry
            l_k, (gw_k, gx_k) = one_microbatch(xs[0], xs[1])
            with _jax.named_scope("update"):
                return (loss_sum + l_k, _jax.tree.map(_jnp.add, grad_sum, gw_k)), gx_k

        init = (_jnp.zeros((), _jnp.float32), _jax.tree.map(_jnp.zeros_like, weights))
        (loss, grad_w), grad_x = _jax.lax.scan(body, init, (per_example, given["loss_target"]))
    with _jax.named_scope("update"):
        delta_w, new_m, new_v = {}, {}, {}
        for n in TWIN_WEIGHTS:
            delta_w[n], new_m[n], new_v[n] = _adamw(weights[n], grad_w[n], given["m_" + n], given["v_" + n])
    return (loss, grad_x, *[grad_w[n] for n in TWIN_WEIGHTS], *[delta_w[n] for n in TWIN_WEIGHTS],
            *[new_m[n] for n in TWIN_WEIGHTS], *[new_v[n] for n in TWIN_WEIGHTS])
```

```python
import functools

import jax
import jax.numpy as jnp
from jax import lax
from jax.experimental import pallas as pl
from jax.experimental.pallas import tpu as pltpu

F32 = jnp.float32
BF16 = jnp.bfloat16
MESH = pl.DeviceIdType.MESH
ANY = pl.BlockSpec(memory_space=pl.ANY)

N_DEV = 8
SEQ = 2048
D_MODEL = 1024
N_HEADS = 16
HEAD_DIM = 64
GRID_W = 64
NA_ROWS = 8
SEQ_ROWS = SEQ // GRID_W
DIL_GROUPS = ((128, 1), (512, 4), (2048, 16))
BAND = 128
RADIUS = 64
FF_SHARD = 352
FF_PAD = 384
RMS_EPS = 1e-6
NEG_INF = -1e30
Q_SCALE = HEAD_DIM ** -0.5

ADAM_LR = 0.001
ADAM_B1 = 0.9
ADAM_B2 = 0.999
ADAM_EPS = 1e-08
ADAM_WD = 0.01
ADAM_STEP = 10

VMEM_LIMIT = 56 * 1024 * 1024
TM = 512

NN = (((1,), (0,)), ((), ()))
NT = (((1,), (1,)), ((), ()))
TN = (((0,), (0,)), ((), ()))


def _params():
    return pltpu.CompilerParams(vmem_limit_bytes=VMEM_LIMIT)


def _matmul(name, a, b, *, grid, a_spec, b_spec, o_spec, out_shape, dims, acc_shape):
    nk = grid[-1]
    kaxis = len(grid) - 1

    def body(a_ref, b_ref, o_ref, acc_ref):
        part = lax.dot_general(a_ref[...].astype(BF16), b_ref[...].astype(BF16), dims, preferred_element_type=F32)
        if nk == 1:
            o_ref[...] = part.astype(o_ref.dtype)
        else:
            k = pl.program_id(kaxis)

            @pl.when(k == 0)
            def _():
                acc_ref[...] = part

            @pl.when(k > 0)
            def _():
                acc_ref[...] += part

            @pl.when(k == nk - 1)
            def _():
                o_ref[...] = acc_ref[...].astype(o_ref.dtype)

    return pl.pallas_call(
        body, out_shape=out_shape, grid=grid, in_specs=[a_spec, b_spec], out_specs=o_spec,
        scratch_shapes=[pltpu.VMEM(acc_shape, F32)], name=name, compiler_params=_params())(a, b)


def _rms_fwd(name, x, g, res=None, out_dtype=F32):
    n_tiles = SEQ // TM
    has_res = res is not None

    def body(*refs):
        x_ref, g_ref = refs[0], refs[1]
        o_ref = refs[-1]
        xv = x_ref[...]
        r = lax.rsqrt(jnp.mean(xv * xv, axis=-1, keepdims=True) + RMS_EPS)
        y = xv * r * g_ref[...]
        if has_res:
            y = refs[2][...] + y
        o_ref[...] = y.astype(o_ref.dtype)

    tile = pl.BlockSpec((TM, D_MODEL), lambda i: (i, 0))
    gspec = pl.BlockSpec((1, D_MODEL), lambda i: (0, 0))
    ins = [x, g] + ([res] if has_res else [])
    specs = [tile, gspec] + ([tile] if has_res else [])
    return pl.pallas_call(
        body, out_shape=jax.ShapeDtypeStruct((SEQ, D_MODEL), out_dtype), grid=(n_tiles,), in_specs=specs,
        out_specs=tile, name=name, compiler_params=_params())(*ins)


def _rms_bwd(name, x, g, dys, res=None, out_dtype=F32):
    n_tiles = SEQ // TM
    n_dy = len(dys)
    has_res = res is not None

    def body(*refs):
        x_ref, g_ref = refs[0], refs[1]
        dy_refs = refs[2:2 + n_dy]
        res_ref = refs[2 + n_dy] if has_res else None
        dx_ref, dg_ref, acc_ref = refs[-3], refs[-2], refs[-1]
        i = pl.program_id(0)
        xv = x_ref[...]
        r = lax.rsqrt(jnp.mean(xv * xv, axis=-1, keepdims=True) + RMS_EPS)
        xn = xv * r
        dy = dy_refs[0][...].astype(F32)
        for extra in dy_refs[1:]:
            dy = dy + extra[...].astype(F32)
        dyg = dy * g_ref[...]
        dx = r * (dyg - xn * jnp.mean(dyg * xn, axis=-1, keepdims=True))
        if has_res:
            dx = res_ref[...] + dx
        dx_ref[...] = dx.astype(dx_ref.dtype)
        part = jnp.sum((dy * xn).reshape(TM // 8, 8, D_MODEL), axis=0)

        @pl.when(i == 0)
        def _():
            acc_ref[...] = part

        @pl.when(i > 0)
        def _():
            acc_ref[...] += part

        @pl.when(i == n_tiles - 1)
        def _():
            dg_ref[...] = jnp.broadcast_to(jnp.sum(acc_ref[...], axis=0, keepdims=True), (8, D_MODEL))

    tile = pl.BlockSpec((TM, D_MODEL), lambda i: (i, 0))
    gspec = pl.BlockSpec((1, D_MODEL), lambda i: (0, 0))
    ins = [x, g] + list(dys) + ([res] if has_res else [])
    specs = [tile, gspec] + [tile] * n_dy + ([tile] if has_res else [])
    dx, dg = pl.pallas_call(
        body, out_shape=(jax.ShapeDtypeStruct((SEQ, D_MODEL), out_dtype), jax.ShapeDtypeStruct((8, D_MODEL), F32)),
        grid=(n_tiles,), in_specs=specs,
        out_specs=(tile, pl.BlockSpec((8, D_MODEL), lambda i: (0, 0))),
        scratch_shapes=[pltpu.VMEM((8, D_MODEL), F32)], name=name, compiler_params=_params())(*ins)
    return dx, dg[0:1]


def _loss_head(name, y, target):
    n_tiles = SEQ // TM

    def body(y_ref, t_ref, dy_ref, loss_ref, acc_ref):
        i = pl.program_id(0)
        diff = y_ref[...] - t_ref[...]
        dy_ref[...] = diff * (1.0 / D_MODEL)
        part = jnp.sum((diff * diff).reshape(TM // 8, 8, D_MODEL), axis=0)

        @pl.when(i == 0)
        def _():
            acc_ref[...] = part

        @pl.when(i > 0)
        def _():
            acc_ref[...] += part

        @pl.when(i == n_tiles - 1)
        def _():
            loss_ref[...] = jnp.full((8, 128), jnp.sum(acc_ref[...]) * (0.5 / D_MODEL), F32)

    tile = pl.BlockSpec((TM, D_MODEL), lambda i: (i, 0))
    dy, loss = pl.pallas_call(
        body, out_shape=(jax.ShapeDtypeStruct((SEQ, D_MODEL), F32), jax.ShapeDtypeStruct((8, 128), F32)),
        grid=(n_tiles,), in_specs=[tile, tile], out_specs=(tile, pl.BlockSpec((8, 128), lambda i: (0, 0))),
        scratch_shapes=[pltpu.VMEM((8, D_MODEL), F32)], name=name, compiler_params=_params())(y, target)
    return dy, loss[0, 0]


def _row_index(shape):
    return lax.broadcasted_iota(jnp.int32, shape, 0)


def _lane_index(shape):
    return lax.broadcasted_iota(jnp.int32, shape, len(shape) - 1)


def _skew_rows(t, direction):
    row = _row_index(t.shape)
    for bit in range(6):
        step = 1 << bit
        shift = step if direction > 0 else 128 - step
        t = jnp.where((row & step) != 0, pltpu.roll(t, shift, 1), t)
    return t


def _rpb_table(rpb_pad):
    def body(r_ref, t_ref):
        lane = _lane_index((GRID_W, 128))
        tiles = []
        for dr in range(2 * NA_ROWS - 1):
            v = pltpu.roll(r_ref[dr:dr + 1, :], 128 - 15, 1)
            t = _skew_rows(jnp.broadcast_to(v, (GRID_W, 128)), +1)
            tiles.append(jnp.where(lane < GRID_W, t, 0.0))
        for p in range(2 * NA_ROWS - 2):
            t_ref[p] = tiles[p] + pltpu.roll(tiles[p + 1], GRID_W, 1)
        t_ref[14] = jnp.zeros((GRID_W, 128), F32)
        t_ref[15] = jnp.zeros((GRID_W, 128), F32)

    return pl.pallas_call(
        body, out_shape=jax.ShapeDtypeStruct((N_HEADS, 16, GRID_W, 128), F32), grid=(N_HEADS,),
        in_specs=[pl.BlockSpec((None, 16, 128), lambda h: (h, 0, 0))],
        out_specs=pl.BlockSpec((None, 16, GRID_W, 128), lambda h: (h, 0, 0, 0)),
        name="rpb_table", compiler_params=_params())(rpb_pad)


def _rpb_grad(gp):
    def body(g_ref, o_ref):
        lane = _lane_index((GRID_W, 128))
        rows = [jnp.zeros((1, 128), F32) for _ in range(16)]
        for p in range(2 * NA_ROWS - 2):
            t = g_ref[p]
            halves = (jnp.where(lane < GRID_W, t, 0.0), pltpu.roll(jnp.where(lane >= GRID_W, t, 0.0), GRID_W, 1))
            for j, half in enumerate(halves):
                diag = jnp.sum(_skew_rows(half, -1), axis=0, keepdims=True)
                rows[p + j] = rows[p + j] + pltpu.roll(diag, 15, 1)
        o_ref[...] = jnp.concatenate(rows, axis=0)

    return pl.pallas_call(
        body, out_shape=jax.ShapeDtypeStruct((N_HEADS, 16, 128), F32), grid=(N_HEADS,),
        in_specs=[pl.BlockSpec((None, 16, GRID_W, 128), lambda h: (h, 0, 0, 0))],
        out_specs=pl.BlockSpec((None, 16, 128), lambda h: (h, 0, 0)),
        name="rpb_grad", compiler_params=_params())(gp)


NA_KEYS = NA_ROWS * GRID_W


def _na_window(i):
    first_row = jnp.clip(i - NA_ROWS // 2, 0, SEQ_ROWS - NA_ROWS)
    return pl.multiple_of(first_row * GRID_W, GRID_W), first_row - i + NA_ROWS - 1


def _na_valid():
    q = _row_index((GRID_W, NA_KEYS))
    k = _lane_index((GRID_W, NA_KEYS)) & (GRID_W - 1)
    first_col = jnp.clip(q - 8, 0, GRID_W - 16)
    return (k >= first_col) & (k < first_col + 16)


def _head_masks():
    lane = _lane_index((1, 128))
    return (lane < HEAD_DIM, lane >= HEAD_DIM)


def _na_scores(q2, kw, tp_ref, a, dr0, mask, valid):
    s = lax.dot_general(jnp.where(mask, q2, jnp.zeros_like(q2)), kw, NT, preferred_element_type=F32)
    bias = jnp.concatenate([tp_ref[a, pl.ds(dr0 + 2 * c, 1), :, :].reshape(GRID_W, 128) for c in range(4)], axis=1)
    return jnp.where(valid, s + bias, NEG_INF)


def _na_specs():
    q_spec = pl.BlockSpec((GRID_W, 128), lambda hp, i: (i, hp))
    k_spec = pl.BlockSpec((SEQ, 128), lambda hp, i: (0, 8 + hp))
    v_spec = pl.BlockSpec((SEQ, 128), lambda hp, i: (0, 16 + hp))
    tp_spec = pl.BlockSpec((2, 16, GRID_W, 128), lambda hp, i: (hp, 0, 0, 0))
    return q_spec, k_spec, v_spec, tp_spec


def _na_fwd(qkv, table):
    def body(q_ref, k_ref, v_ref, tp_ref, o_ref, lse_ref):
        start, dr0 = _na_window(pl.program_id(1))
        kw = k_ref[pl.ds(start, NA_KEYS), :]
        vw = v_ref[pl.ds(start, NA_KEYS), :]
        q2 = q_ref[...] * Q_SCALE
        valid = _na_valid()
        masks = _head_masks()
        outs, lses = [], []
        for a in range(2):
            s = _na_scores(q2, kw, tp_ref, a, dr0, masks[a], valid)
            m = jnp.max(s, axis=-1, keepdims=True)
            p = jnp.exp(s - m)
            denom = jnp.sum(p, axis=-1, keepdims=True)
            outs.append(jnp.dot(p.astype(BF16), vw, preferred_element_type=F32) / denom)
            lses.append(m + jnp.log(denom))
        o_ref[...] = jnp.where(masks[0], outs[0], outs[1]).astype(o_ref.dtype)
        lse_ref[...] = jnp.where(masks[0], lses[0], lses[1])

    q_spec, k_spec, v_spec, tp_spec = _na_specs()
    return pl.pallas_call(
        body, out_shape=(jax.ShapeDtypeStruct((SEQ, D_MODEL), BF16), jax.ShapeDtypeStruct((SEQ, D_MODEL), F32)),
        grid=(N_HEADS // 2, SEQ_ROWS), in_specs=[q_spec, k_spec, v_spec, tp_spec],
        out_specs=(q_spec, q_spec), name="na_fwd", compiler_params=_params())(qkv, qkv, qkv, table)


def _na_bwd(qkv, table, d_out, lse):
    def body(q_ref, k_ref, v_ref, tp_ref, do_ref, lse_ref, dqkv_ref, gp_ref, dk_acc, dv_acc):
        i = pl.program_id(1)
        start, dr0 = _na_window(i)

        @pl.when(i == 0)
        def _():
            dk_acc[...] = jnp.zeros_like(dk_acc)
            dv_acc[...] = jnp.zeros_like(dv_acc)
            gp_ref[...] = jnp.zeros_like(gp_ref)

        kw = k_ref[pl.ds(start, NA_KEYS), :]
        vw = v_ref[pl.ds(start, NA_KEYS), :]
        q2 = q_ref[...] * Q_SCALE
        do2 = do_ref[...]
        lse2 = lse_ref[...]
        valid = _na_valid()
        masks = _head_masks()
        dqs, dks, dvs = [], [], []
        for a in range(2):
            s = _na_scores(q2, kw, tp_ref, a, dr0, masks[a], valid)
            p = jnp.exp(s - lse2[:, a * HEAD_DIM:a * HEAD_DIM + 1])
            dp = lax.dot_general(jnp.where(masks[a], do2, jnp.zeros_like(do2)), vw, NT, preferred_element_type=F32)
            ds = p * (dp - jnp.sum(p * dp, axis=-1, keepdims=True))
            for c in range(4):
                gp_ref[a, pl.ds(dr0 + 2 * c, 1), :, :] += ds[:, 128 * c:128 * (c + 1)].reshape(1, GRID_W, 128)
            dsb = ds.astype(BF16)
            dqs.append(jnp.dot(dsb, kw, preferred_element_type=F32))
            dks.append(lax.dot_general(dsb, q2, TN, preferred_element_type=F32))
            dvs.append(lax.dot_general(p.astype(BF16), do2, TN, preferred_element_type=F32))
        dq = jnp.where(masks[0], dqs[0], dqs[1]) * Q_SCALE
        dqkv_ref[0, pl.ds(pl.multiple_of(i * GRID_W, GRID_W), GRID_W), :] = dq.astype(dqkv_ref.dtype)
        dk_acc[pl.ds(start, NA_KEYS), :] += jnp.where(masks[0], dks[0], dks[1])
        dv_acc[pl.ds(start, NA_KEYS), :] += jnp.where(masks[0], dvs[0], dvs[1])

        @pl.when(i == SEQ_ROWS - 1)
        def _():
            dqkv_ref[1] = dk_acc[...].astype(dqkv_ref.dtype)
            dqkv_ref[2] = dv_acc[...].astype(dqkv_ref.dtype)

    q_spec, k_spec, v_spec, tp_spec = _na_specs()
    return pl.pallas_call(
        body,
        out_shape=(jax.ShapeDtypeStruct((3, SEQ, D_MODEL), BF16), jax.ShapeDtypeStruct((N_HEADS, 16, GRID_W, 128), F32)),
        grid=(N_HEADS // 2, SEQ_ROWS), in_specs=[q_spec, k_spec, v_spec, tp_spec, q_spec, q_spec],
        out_specs=(pl.BlockSpec((3, SEQ, 128), lambda hp, i: (0, 0, hp)), tp_spec),
        scratch_shapes=[pltpu.VMEM((SEQ, 128), F32), pltpu.VMEM((SEQ, 128), F32)],
        name="na_bwd", compiler_params=_params())(qkv, qkv, qkv, table, d_out, lse)


def _dil_geometry(group):
    dil = DIL_GROUPS[group][1]
    sub_len = SEQ // dil
    return dil, sub_len, sub_len // BAND, min(2 * BAND, sub_len)


def _dil_window(b, sub_len, n_keys):
    if n_keys == sub_len:
        return 0
    return pl.multiple_of(jnp.clip(b * BAND - RADIUS, 0, sub_len - n_keys), RADIUS)


def _dil_distance(b, start, n_keys):
    qpos = b * BAND + _row_index((BAND, n_keys))
    kpos = start + _lane_index((BAND, n_keys))
    dist = jnp.abs(qpos - kpos)
    return dist.astype(F32), dist <= RADIUS


def _dil_scores(q2, kw, slope, dist, valid, mask):
    s = lax.dot_general(jnp.where(mask, q2, jnp.zeros_like(q2)), kw, NT, preferred_element_type=F32)
    return jnp.where(valid, s - slope * dist, NEG_INF)


def _dil_specs(group):
    dil, sub_len, nb, _ = _dil_geometry(group)
    col = group * 24
    q_spec = pl.BlockSpec((BAND, 128), lambda n, hp, b: (n * nb + b, col + hp))
    k_spec = pl.BlockSpec((sub_len, 128), lambda n, hp, b: (n, col + 8 + hp))
    v_spec = pl.BlockSpec((sub_len, 128), lambda n, hp, b: (n, col + 16 + hp))
    tile = pl.BlockSpec((BAND, 128), lambda n, hp, b: (n * nb + b, hp))
    smem = pl.BlockSpec(memory_space=pltpu.SMEM)
    return (dil, N_HEADS // 2, nb), q_spec, k_spec, v_spec, tile, smem


def _dil_fwd(group, qkv, slopes):
    _, sub_len, _, n_keys = _dil_geometry(group)

    def body(q_ref, k_ref, v_ref, slope_ref, o_ref, lse_ref):
        hp, b = pl.program_id(1), pl.program_id(2)
        start = _dil_window(b, sub_len, n_keys)
        kw = k_ref[pl.ds(start, n_keys), :]
        vw = v_ref[pl.ds(start, n_keys), :]
        q2 = q_ref[...] * Q_SCALE
        dist, valid = _dil_distance(b, start, n_keys)
        masks = _head_masks()
        outs, lses = [], []
        for a in range(2):
            s = _dil_scores(q2, kw, slope_ref[2 * hp + a], dist, valid, masks[a])
            m = jnp.max(s, axis=-1, keepdims=True)
            p = jnp.exp(s - m)
            denom = jnp.sum(p, axis=-1, keepdims=True)
            outs.append(jnp.dot(p.astype(BF16), vw, preferred_element_type=F32) / denom)
            lses.append(m + jnp.log(denom))
        o_ref[...] = jnp.where(masks[0], outs[0], outs[1])
        lse_ref[...] = jnp.where(masks[0], lses[0], lses[1])

    grid, q_spec, k_spec, v_spec, tile, smem = _dil_specs(group)
    return pl.pallas_call(
        body, out_shape=(jax.ShapeDtypeStruct((SEQ, D_MODEL), F32), jax.ShapeDtypeStruct((SEQ, D_MODEL), F32)),
        grid=grid, in_specs=[q_spec, k_spec, v_spec, smem], out_specs=(tile, tile),
        name=f"dil_fwd_{group}", compiler_params=_params())(qkv, qkv, qkv, slopes)


def _dil_merge(outs, lses):
    def body(o0, o1, o2, l0, l1, l2, o_ref, lse_ref):
        ls = [l0[...], l1[...], l2[...]]
        m = jnp.maximum(jnp.maximum(ls[0], ls[1]), ls[2])
        es = [jnp.exp(v - m) for v in ls]
        total = es[0] + es[1] + es[2]
        o_ref[...] = (es[0] * o0[...] + es[1] * o1[...] + es[2] * o2[...]) / total
        lse_ref[...] = m + jnp.log(total)

    tile = pl.BlockSpec((TM, D_MODEL), lambda i: (i, 0))
    return pl.pallas_call(
        body, out_shape=(jax.ShapeDtypeStruct((SEQ, D_MODEL), F32), jax.ShapeDtypeStruct((SEQ, D_MODEL), F32)),
        grid=(SEQ // TM,), in_specs=[tile] * 6, out_specs=(tile, tile), name="dil_merge",
        compiler_params=_params())(*outs, *lses)


def _dil_bwd(group, qkv, slopes, d_out, out, lse_group, lse_total):
    _, sub_len, nb, n_keys = _dil_geometry(group)

    def body(q_ref, k_ref, v_ref, slope_ref, do_ref, o_ref, lg_ref, lt_ref, dqkv_ref, dk_acc, dv_acc):
        hp, b = pl.program_id(1), pl.program_id(2)
        start = _dil_window(b, sub_len, n_keys)

        @pl.when(b == 0)
        def _():
            dk_acc[...] = jnp.zeros_like(dk_acc)
            dv_acc[...] = jnp.zeros_like(dv_acc)

        kw = k_ref[pl.ds(start, n_keys), :]
        vw = v_ref[pl.ds(start, n_keys), :]
        q2 = q_ref[...] * Q_SCALE
        dist, valid = _dil_distance(b, start, n_keys)
        masks = _head_masks()
        lse2 = lg_ref[...]
        weight = jnp.exp(lse2 - lt_ref[...])
        do2 = do_ref[...]
        dog = (weight * do2).astype(BF16)
        do_o = do2 * o_ref[...]
        dqs, dks, dvs = [], [], []
        for a in range(2):
            col = slice(a * HEAD_DIM, a * HEAD_DIM + 1)
            s = _dil_scores(q2, kw, slope_ref[2 * hp + a], dist, valid, masks[a])
            p = jnp.exp(s - lse2[:, col])
            delta = weight[:, col] * jnp.sum(jnp.where(masks[a], do_o, 0.0), axis=-1, keepdims=True)
            dp = lax.dot_general(jnp.where(masks[a], dog, jnp.zeros_like(dog)), vw, NT, preferred_element_type=F32)
            dsb = (p * (dp - delta)).astype(BF16)
            dqs.append(jnp.dot(dsb, kw, preferred_element_type=F32))
            dks.append(lax.dot_general(dsb, q2, TN, preferred_element_type=F32))
            dvs.append(lax.dot_general(p.astype(BF16), dog, TN, preferred_element_type=F32))
        dq = jnp.where(masks[0], dqs[0], dqs[1]) * Q_SCALE
        dqkv_ref[0, pl.ds(pl.multiple_of(b * BAND, BAND), BAND), :] = dq.astype(dqkv_ref.dtype)
        dk_acc[pl.ds(start, n_keys), :] += jnp.where(masks[0], dks[0], dks[1])
        dv_acc[pl.ds(start, n_keys), :] += jnp.where(masks[0], dvs[0], dvs[1])

        @pl.when(b == nb - 1)
        def _():
            dqkv_ref[1] = dk_acc[...].astype(dqkv_ref.dtype)
            dqkv_ref[2] = dv_acc[...].astype(dqkv_ref.dtype)

    grid, q_spec, k_spec, v_spec, tile, smem = _dil_specs(group)
    return pl.pallas_call(
        body, out_shape=jax.ShapeDtypeStruct((3, SEQ, D_MODEL), BF16), grid=grid,
        in_specs=[q_spec, k_spec, v_spec, smem, tile, tile, tile, tile],
        out_specs=pl.BlockSpec((3, sub_len, 128), lambda n, hp, b: (0, n, hp)),
        scratch_shapes=[pltpu.VMEM((sub_len, 128), F32), pltpu.VMEM((sub_len, 128), F32)],
        name=f"dil_bwd_{group}", compiler_params=_params())(qkv, qkv, qkv, slopes, d_out, out, lse_group, lse_total)


def _sort_rows(t, dil):
    if dil == 1:
        return t
    return t.reshape(SEQ // dil, dil, t.shape[-1]).transpose(1, 0, 2).reshape(SEQ, t.shape[-1])


def _unsort_rows(t, dil):
    if dil == 1:
        return t
    return t.reshape(dil, SEQ // dil, t.shape[-1]).transpose(1, 0, 2).reshape(SEQ, t.shape[-1])


def _ffn_up(name, h, w_gu, layer):
    def body(h_ref, wg_ref, wu_ref, act_ref, hg_ref, hu_ref):
        hv = h_ref[...]
        hg = jnp.dot(hv, wg_ref[...], preferred_element_type=F32)
        hu = jnp.dot(hv, wu_ref[...], preferred_element_type=F32)
        act_ref[...] = (hg * jax.nn.sigmoid(hg) * hu).astype(act_ref.dtype)
        hg_ref[...] = hg.astype(hg_ref.dtype)
        hu_ref[...] = hu.astype(hu_ref.dtype)

    out = pl.BlockSpec((None, TM, FF_PAD), lambda d, i: (d, i, 0))
    shape = jax.ShapeDtypeStruct((N_DEV, SEQ, FF_PAD), BF16)
    return pl.pallas_call(
        body, out_shape=(shape, shape, shape), grid=(N_DEV, SEQ // TM),
        in_specs=[pl.BlockSpec((TM, D_MODEL), lambda d, i: (i, 0)),
                  pl.BlockSpec((None, None, None, D_MODEL, FF_PAD), lambda d, i: (d, layer, 0, 0, 0)),
                  pl.BlockSpec((None, None, None, D_MODEL, FF_PAD), lambda d, i: (d, layer, 1, 0, 0))],
        out_specs=(out, out, out), name=name, compiler_params=_params())(h, w_gu, w_gu)


def _ffn_bwd_act(name, d_f, w_down, hg, hu, layer):
    def body(df_ref, wd_ref, hg_ref, hu_ref, dgu_ref):
        dact = lax.dot_general(df_ref[...], wd_ref[...], NT, preferred_element_type=F32)
        hgv = hg_ref[...].astype(F32)
        sig = jax.nn.sigmoid(hgv)
        dgu_ref[0] = (dact * hu_ref[...].astype(F32) * (sig * (1.0 + hgv * (1.0 - sig)))).astype(dgu_ref.dtype)
        dgu_ref[1] = (dact * hgv * sig).astype(dgu_ref.dtype)

    tile = pl.BlockSpec((None, TM, FF_PAD), lambda d, i: (d, i, 0))
    return pl.pallas_call(
        body, out_shape=jax.ShapeDtypeStruct((2, N_DEV, SEQ, FF_PAD), BF16), grid=(N_DEV, SEQ // TM),
        in_specs=[pl.BlockSpec((TM, D_MODEL), lambda d, i: (i, 0)),
                  pl.BlockSpec((None, None, FF_PAD, D_MODEL), lambda d, i: (d, layer, 0, 0)), tile, tile],
        out_specs=pl.BlockSpec((2, None, TM, FF_PAD), lambda d, i: (0, d, i, 0)),
        name=name, compiler_params=_params())(d_f, w_down, hg, hu)


def _position():
    return lax.axis_index("x"), lax.axis_index("y"), lax.axis_index("c")


def _flat(p):
    return 4 * p[0] + 2 * p[1] + p[2]


def _all_gather(name, locals_, out_shapes, windows):
    n = len(locals_)

    def body(*refs):
        src_refs, out_refs = refs[:n], refs[n:2 * n]
        send_sems, recv_sems, local_sems = refs[2 * n:]
        x, y, c = _position()
        me, sibling = (x, y, c), (x, y, 1 - c)
        chips = [(1 - x, y), (x, 1 - y), (1 - x, 1 - y)]

        def copy(t, k, block, to, from_local=False):
            dst = windows[t](out_refs[t], _flat(block))
            return pltpu.make_async_remote_copy(
                src_ref=src_refs[t] if from_local else dst, dst_ref=dst, send_sem=send_sems.at[t, k],
                recv_sem=recv_sems.at[t, k], device_id=to, device_id_type=MESH)

        mine = [pltpu.make_async_copy(src_refs[t], windows[t](out_refs[t], _flat(me)), local_sems.at[t]) for t in range(n)]
        sends = []
        for t in range(n):
            mine[t].start()
            sends.append(copy(t, 0, me, sibling, True))
            sends += [copy(t, 1 + j, me, (*chip, c), True) for j, chip in enumerate(chips)]
        for cp in sends:
            cp.start()
        for t in range(n):
            for j, chip in enumerate(chips):
                copy(t, 1 + j, (*chip, c), me).wait_recv()
                passed = copy(t, 4 + j, (*chip, c), sibling)
                passed.start()
                sends.append(passed)
        for t in range(n):
            copy(t, 0, sibling, me).wait_recv()
            for j, chip in enumerate(chips):
                copy(t, 4 + j, (*chip, 1 - c), me).wait_recv()
        for cp in sends:
            cp.wait_send()
        for cp in mine:
            cp.wait()

    return pl.pallas_call(
        body, out_shape=tuple(out_shapes), in_specs=[ANY] * n, out_specs=tuple([ANY] * n),
        scratch_shapes=[pltpu.SemaphoreType.DMA((n, 7)), pltpu.SemaphoreType.DMA((n, 7)), pltpu.SemaphoreType.DMA((n,))],
        name=name)(*locals_)


def _exchange(name, grads, sources, land_shapes, slots):
    n = len(grads)
    n_land = len(land_shapes)

    def body(*refs):
        g_refs, land_refs = refs[:n], refs[n:n + n_land]
        send_sems, recv_sems, local_sems = refs[n + n_land:]
        x, y, c = _position()
        me = (x, y, c)

        def peer(k):
            return (1 - x if k & 4 else x, 1 - y if k & 2 else y, 1 - c if k & 1 else c)

        def copy(t, k, sender, to):
            return pltpu.make_async_remote_copy(
                src_ref=sources[t](g_refs[t], _flat(to)), dst_ref=slots[t](land_refs, _flat(sender)),
                send_sem=send_sems.at[t, k - 1], recv_sem=recv_sems.at[t, k - 1], device_id=to, device_id_type=MESH)

        mine = [pltpu.make_async_copy(sources[t](g_refs[t], _flat(me)), slots[t](land_refs, _flat(me)), local_sems.at[t])
                for t in range(n)]
        sends = [copy(t, k, me, peer(k)) for t in range(n) for k in range(1, N_DEV)]
        for cp in mine + sends:
            cp.start()
        for t in range(n):
            for k in range(1, N_DEV):
                copy(t, k, peer(k), me).wait_recv()
        for cp in sends:
            cp.wait_send()
        for cp in mine:
            cp.wait()

    return pl.pallas_call(
        body, out_shape=tuple(land_shapes), in_specs=[ANY] * n, out_specs=tuple([ANY] * n_land),
        scratch_shapes=[pltpu.SemaphoreType.DMA((n, 7)), pltpu.SemaphoreType.DMA((n, 7)), pltpu.SemaphoreType.DMA((n,))],
        name=name)(*grads)


def _adamw(name, landing, w, m, v, *, grid, land_spec, p_spec):
    def body(land_ref, w_ref, m_ref, v_ref, g_ref, delta_ref, m_out, v_out):
        ncols = w_ref.shape[-1]
        g = land_ref[0]
        for s in range(1, N_DEV):
            g = g + land_ref[s]
        g = g[:, :ncols]
        m_new = ADAM_B1 * m_ref[...] + (1.0 - ADAM_B1) * g
        v_new = ADAM_B2 * v_ref[...] + (1.0 - ADAM_B2) * jnp.square(g)
        m_hat = m_new / (1.0 - ADAM_B1 ** ADAM_STEP)
        v_hat = v_new / (1.0 - ADAM_B2 ** ADAM_STEP)
        g_ref[...] = g
        delta_ref[...] = -ADAM_LR * (m_hat / (jnp.sqrt(v_hat) + ADAM_EPS) + ADAM_WD * w_ref[...])
        m_out[...] = m_new
        v_out[...] = v_new

    shape = jax.ShapeDtypeStruct(w.shape, F32)
    return pl.pallas_call(
        body, out_shape=(shape,) * 4, grid=grid, in_specs=[land_spec, p_spec, p_spec, p_spec],
        out_specs=(p_spec,) * 4, name=name, compiler_params=_params())(landing, w, m, v)


def _row(p, layer):
    return p[layer][None, :]


def _square(name, a, b, dims, out_dtype, a_tokens=True):
    if dims == TN:
        return _matmul(name, a, b, grid=(2, SEQ // TM), a_spec=pl.BlockSpec((TM, 512), lambda i, k: (k, i)),
                       b_spec=pl.BlockSpec((TM, D_MODEL), lambda i, k: (k, 0)),
                       o_spec=pl.BlockSpec((512, D_MODEL), lambda i, k: (i, 0)),
                       out_shape=jax.ShapeDtypeStruct((D_MODEL, D_MODEL), out_dtype), dims=TN, acc_shape=(512, D_MODEL))
    return _matmul(name, a, b, grid=(SEQ // TM, 1), a_spec=pl.BlockSpec((TM, D_MODEL), lambda i, k: (i, 0)),
                   b_spec=pl.BlockSpec((D_MODEL, D_MODEL), lambda i, k: (0, 0)),
                   o_spec=pl.BlockSpec((TM, D_MODEL), lambda i, k: (i, 0)),
                   out_shape=jax.ShapeDtypeStruct((SEQ, D_MODEL), out_dtype), dims=dims, acc_shape=(8, 128))


def _qkv_fwd(name, hs, w, n_chunks):
    return _matmul(name, hs, w, grid=(n_chunks, SEQ // TM, 1),
                   a_spec=pl.BlockSpec((None, TM, D_MODEL), lambda j, i, k: (j // 3, i, 0)),
                   b_spec=pl.BlockSpec((D_MODEL, D_MODEL), lambda j, i, k: (0, j)),
                   o_spec=pl.BlockSpec((TM, D_MODEL), lambda j, i, k: (i, j)),
                   out_shape=jax.ShapeDtypeStruct((SEQ, n_chunks * D_MODEL), BF16), dims=NN, acc_shape=(8, 128))


def _qkv_bwd(name, hs, w, dqkv, n_chunks):
    d_w = _matmul(name + "_dw", hs, dqkv, grid=(n_chunks, 2, SEQ // TM),
                  a_spec=pl.BlockSpec((None, TM, 512), lambda j, i, k: (j // 3, k, i)),
                  b_spec=pl.BlockSpec((None, TM, D_MODEL), lambda j, i, k: (j, k, 0)),
                  o_spec=pl.BlockSpec((512, D_MODEL), lambda j, i, k: (i, j)),
                  out_shape=jax.ShapeDtypeStruct((D_MODEL, n_chunks * D_MODEL), F32), dims=TN, acc_shape=(512, D_MODEL))
    d_h = _matmul(name + "_dh", dqkv, w, grid=(n_chunks // 3, SEQ // TM, 3),
                  a_spec=pl.BlockSpec((None, TM, D_MODEL), lambda g, i, k: (3 * g + k, i, 0)),
                  b_spec=pl.BlockSpec((D_MODEL, D_MODEL), lambda g, i, k: (0, 3 * g + k)),
                  o_spec=pl.BlockSpec((None, TM, D_MODEL), lambda g, i, k: (g, i, 0)),
                  out_shape=jax.ShapeDtypeStruct((n_chunks // 3, SEQ, D_MODEL), F32), dims=NT, acc_shape=(TM, D_MODEL))
    return d_w, d_h


def _local_step(x, target, norms, rpb, w_na, wo_na, w_dil, wo_dil, w_gu, w_down):
    mix_pre, mix_post, ffn_pre, ffn_post = norms
    slopes = 2.0 ** (-8.0 * jnp.arange(1, N_HEADS + 1, dtype=F32) / N_HEADS)
    rpb_pad = jnp.pad(rpb, ((0, 0), (0, 1), (0, 128 - 31)))
    table = _rpb_table(rpb_pad)
    saved = []

    for layer in range(2):
        tag = f"l{layer}"
        h = _rms_fwd(tag + "_norm_mix", x, _row(mix_pre, layer), out_dtype=BF16)
        if layer == 0:
            qkv = _qkv_fwd(tag + "_qkv", h[None], w_na, 3)
            o, lse = _na_fwd(qkv, table)
            mixer = (h, qkv, o, lse)
            a = _square(tag + "_proj", o, wo_na, NN, F32)
        else:
            hs = jnp.stack([_sort_rows(h, dil) for _, dil in DIL_GROUPS])
            qkv = _qkv_fwd(tag + "_qkv", hs, w_dil, 9)
            outs, lses, lses_sorted = [], [], []
            for g, (_, dil) in enumerate(DIL_GROUPS):
                o_g, lse_g = _dil_fwd(g, qkv, slopes * dil)
                outs.append(_unsort_rows(o_g, dil))
                lses.append(_unsort_rows(lse_g, dil))
                lses_sorted.append(lse_g)
            o, lse_total = _dil_merge(outs, lses)
            mixer = (hs, qkv, o, lses_sorted, lse_total)
            a = _square(tag + "_proj", o, wo_dil, NN, F32)
        x1 = _rms_fwd(tag + "_post_mix", a, _row(mix_post, layer), res=x)
        h2 = _rms_fwd(tag + "_norm_ffn", x1, _row(ffn_pre, layer), out_dtype=BF16)
        act, hg, hu = _ffn_up(tag + "_ffn_up", h2, w_gu, layer)
        f = _matmul(tag + "_ffn_down", act, w_down, grid=(SEQ // TM, N_DEV),
                    a_spec=pl.BlockSpec((None, TM, FF_PAD), lambda i, k: (k, i, 0)),
                    b_spec=pl.BlockSpec((None, None, FF_PAD, D_MODEL), lambda i, k: (k, layer, 0, 0)),
                    o_spec=pl.BlockSpec((TM, D_MODEL), lambda i, k: (i, 0)),
                    out_shape=jax.ShapeDtypeStruct((SEQ, D_MODEL), F32), dims=NN, acc_shape=(TM, D_MODEL))
        x2 = _rms_fwd(tag + "_post_ffn", f, _row(ffn_post, layer), res=x1)
        saved.append((x, mixer, a, x1, h2, act, hg, hu, f))
        x = x2

    dx, loss = _loss_head("loss_head", x, target)
    grads = {}
    d_norm = {k: [None, None] for k in ("mix_pre", "mix_post", "ffn_pre", "ffn_post")}

    for layer in (1, 0):
        tag = f"b{layer}"
        x0, mixer, a, x1, h2, act, hg, hu, f = saved[layer]
        d_f, d_norm["ffn_post"][layer] = _rms_bwd(tag + "_post_ffn", f, _row(ffn_post, layer), [dx], out_dtype=BF16)
        dgu = _ffn_bwd_act(tag + "_ffn_act", d_f, w_down, hg, hu, layer)
        grads[f"down{layer}"] = _matmul(
            tag + "_ffn_ddown", act, d_f, grid=(N_DEV, SEQ // TM),
            a_spec=pl.BlockSpec((None, TM, FF_PAD), lambda d, k: (d, k, 0)),
            b_spec=pl.BlockSpec((TM, D_MODEL), lambda d, k: (k, 0)),
            o_spec=pl.BlockSpec((None, FF_PAD, D_MODEL), lambda d, k: (d, 0, 0)),
            out_shape=jax.ShapeDtypeStruct((N_DEV, FF_PAD, D_MODEL), F32), dims=TN, acc_shape=(FF_PAD, D_MODEL))
        grads[f"gu{layer}"] = _matmul(
            tag + "_ffn_dgu", h2, dgu, grid=(2, N_DEV, SEQ // TM),
            a_spec=pl.BlockSpec((TM, D_MODEL), lambda t, d, k: (k, 0)),
            b_spec=pl.BlockSpec((None, None, TM, FF_PAD), lambda t, d, k: (t, d, k, 0)),
            o_spec=pl.BlockSpec((None, None, D_MODEL, FF_PAD), lambda t, d, k: (d, t, 0, 0)),
            out_shape=jax.ShapeDtypeStruct((N_DEV, 2, D_MODEL, FF_PAD), F32), dims=TN, acc_shape=(D_MODEL, FF_PAD))
        d_h2 = _matmul(
            tag + "_ffn_dh", dgu, w_gu, grid=(SEQ // TM, 2 * N_DEV),
            a_spec=pl.BlockSpec((None, None, TM, FF_PAD), lambda i, k: (k // N_DEV, k % N_DEV, i, 0)),
            b_spec=pl.BlockSpec((None, None, None, D_MODEL, FF_PAD), lambda i, k: (k % N_DEV, layer, k // N_DEV, 0, 0)),
            o_spec=pl.BlockSpec((TM, D_MODEL), lambda i, k: (i, 0)),
            out_shape=jax.ShapeDtypeStruct((SEQ, D_MODEL), F32), dims=NT, acc_shape=(TM, D_MODEL))
        dx1, d_norm["ffn_pre"][layer] = _rms_bwd(tag + "_norm_ffn", x1, _row(ffn_pre, layer), [d_h2], res=dx)
        d_a, d_norm["mix_post"][layer] = _rms_bwd(tag + "_post_mix", a, _row(mix_post, layer), [dx1], out_dtype=BF16)
        if layer == 0:
            h, qkv, o, lse = mixer
            d_o = _square(tag + "_proj_do", d_a, wo_na, NT, BF16)
            grads["wo_na"] = _square(tag + "_proj_dw", o, d_a, TN, F32)
            dqkv, gp = _na_bwd(qkv, table, d_o, lse)
            grads["rpb"] = _rpb_grad(gp)[:, :15, :31]
            grads["w_na"], d_h = _qkv_bwd(tag + "_qkv", h[None], w_na, dqkv, 3)
            d_hs = [d_h[0]]
        else:
            hs, qkv, o, lses, lse_total = mixer
            d_o = _square(tag + "_proj_do", d_a, wo_dil, NT, F32)
            grads["wo_dil"] = _square(tag + "_proj_dw", o, d_a, TN, F32)
            parts = []
            for g, (_, dil) in enumerate(DIL_GROUPS):
                parts.append(_dil_bwd(g, qkv, slopes * dil, _sort_rows(d_o, dil), _sort_rows(o, dil), lses[g],
                                      _sort_rows(lse_total, dil)))
            dqkv = jnp.concatenate(parts, axis=0)
            grads["w_dil"], d_h = _qkv_bwd(tag + "_qkv", hs, w_dil, dqkv, 9)
            d_hs = [_unsort_rows(d_h[g], dil) for g, (_, dil) in enumerate(DIL_GROUPS)]
        dx, d_norm["mix_pre"][layer] = _rms_bwd(tag + "_norm_mix", x0, _row(mix_pre, layer), d_hs, res=dx1)

    d_gains = [jnp.concatenate(d_norm[k], axis=0) for k in ("mix_pre", "mix_post", "ffn_pre", "ffn_post")]
    return loss, dx, d_gains, grads


RPB_SIZE = N_HEADS * 15 * 31


def _pack_small(gains, rpb):
    top = jnp.concatenate(gains, axis=0).reshape(64, 128)
    bottom = jnp.pad(rpb.reshape(-1), (0, 64 * 128 - RPB_SIZE)).reshape(64, 128)
    return jnp.concatenate([top, bottom], axis=0)


def _unpack_small(p):
    gains = p[:64].reshape(4, 2, D_MODEL)
    rpb = p[64:].reshape(-1)[:RPB_SIZE].reshape(1, N_HEADS, 15, 31)
    return [gains[i] for i in range(4)], rpb


def kernel(x, norm_mix_pre, norm_mix_post, norm_ffn_pre, norm_ffn_post, na_w_qkv, na_w_o, na_rpb, dil_w_qkv, dil_w_o, ffn_w_gate, ffn_w_up, ffn_w_down, loss_target, m_norm_mix_pre, m_norm_mix_post, m_norm_ffn_pre, m_norm_ffn_post, m_na_w_qkv, m_na_w_o, m_na_rpb, m_dil_w_qkv, m_dil_w_o, m_ffn_w_gate, m_ffn_w_up, m_ffn_w_down, v_norm_mix_pre, v_norm_mix_post, v_norm_ffn_pre, v_norm_ffn_post, v_na_w_qkv, v_na_w_o, v_na_rpb, v_dil_w_qkv, v_dil_w_o, v_ffn_w_gate, v_ffn_w_up, v_ffn_w_down):
    na_cols, dil_cols, o_rows = 3 * D_MODEL // N_DEV, 9 * D_MODEL // N_DEV, D_MODEL // N_DEV
    ff_pad = FF_PAD - FF_SHARD

    gu_local = jnp.pad(jnp.stack([ffn_w_gate, ffn_w_up], axis=1).astype(BF16), ((0, 0), (0, 0), (0, 0), (0, ff_pad)))
    down_local = jnp.pad(ffn_w_down.astype(BF16), ((0, 0), (0, ff_pad), (0, 0)))
    locals_ = [na_w_qkv[0].astype(BF16), na_w_o[0].astype(BF16), dil_w_qkv[0].astype(BF16), dil_w_o[0].astype(BF16),
               gu_local, down_local]

    def cols(width):
        return lambda ref, d: ref.at[:, pl.ds(pl.multiple_of(d * width, 128), width)]

    def lead(ref, d):
        return ref.at[d]

    gathered = _all_gather(
        "gather_weights", locals_,
        [jax.ShapeDtypeStruct((D_MODEL, 3 * D_MODEL), BF16), jax.ShapeDtypeStruct((N_DEV, o_rows, D_MODEL), BF16),
         jax.ShapeDtypeStruct((D_MODEL, 9 * D_MODEL), BF16), jax.ShapeDtypeStruct((N_DEV, o_rows, D_MODEL), BF16),
         jax.ShapeDtypeStruct((N_DEV, 2, 2, D_MODEL, FF_PAD), BF16), jax.ShapeDtypeStruct((N_DEV, 2, FF_PAD, D_MODEL), BF16)],
        [cols(na_cols), lead, cols(dil_cols), lead, lead, lead])
    w_na, wo_na, w_dil, wo_dil, w_gu, w_down = gathered
    wo_na = wo_na.reshape(D_MODEL, D_MODEL)
    wo_dil = wo_dil.reshape(D_MODEL, D_MODEL)

    norms = (norm_mix_pre, norm_mix_post, norm_ffn_pre, norm_ffn_post)
    loss, grad_x, d_gains, grads = _local_step(x[0], loss_target[0], norms, na_rpb[0], w_na, wo_na, w_dil, wo_dil, w_gu, w_down)
    loss = lax.psum(loss, ("x", "y", "c"))

    sources = [cols(na_cols), lead, cols(dil_cols), lead, lead, lead, lead, lead]
    slots = [lambda L, s: L[0].at[s], lambda L, s: L[1].at[s], lambda L, s: L[2].at[s], lambda L, s: L[3].at[s],
             lambda L, s: L[4].at[s, 0], lambda L, s: L[4].at[s, 1], lambda L, s: L[5].at[s, 0], lambda L, s: L[5].at[s, 1]]
    landing = _exchange(
        "exchange_grads",
        [grads["w_na"], grads["wo_na"].reshape(N_DEV, o_rows, D_MODEL), grads["w_dil"],
         grads["wo_dil"].reshape(N_DEV, o_rows, D_MODEL), grads["gu0"], grads["gu1"], grads["down0"], grads["down1"]],
        sources,
        [jax.ShapeDtypeStruct((N_DEV, D_MODEL, na_cols), F32), jax.ShapeDtypeStruct((N_DEV, o_rows, D_MODEL), F32),
         jax.ShapeDtypeStruct((N_DEV, D_MODEL, dil_cols), F32), jax.ShapeDtypeStruct((N_DEV, o_rows, D_MODEL), F32),
         jax.ShapeDtypeStruct((N_DEV, 2, 2, D_MODEL, FF_PAD), F32), jax.ShapeDtypeStruct((N_DEV, 2, FF_PAD, D_MODEL), F32)],
        slots)
    small = _all_gather("gather_small", [_pack_small(d_gains, grads["rpb"])], [jax.ShapeDtypeStruct((N_DEV, 128, 128), F32)],
                        [lead])[0]

    def whole(rows, ncols):
        return dict(grid=(1,), land_spec=pl.BlockSpec((N_DEV, rows, ncols), lambda i: (0, 0, 0)),
                    p_spec=pl.BlockSpec((None, rows, ncols), lambda i: (0, 0, 0)))

    def by_rows(tile, n_tiles, ncols):
        return dict(grid=(n_tiles,), land_spec=pl.BlockSpec((N_DEV, tile, ncols), lambda i: (0, i, 0)),
                    p_spec=pl.BlockSpec((None, tile, ncols), lambda i: (0, i, 0)))

    def gate_up(t):
        return dict(grid=(2, 8), land_spec=pl.BlockSpec((N_DEV, None, None, 128, FF_PAD), lambda l, r: (0, l, t, r, 0)),
                    p_spec=pl.BlockSpec((None, 128, FF_SHARD), lambda l, r: (l, r, 0)))

    down = dict(grid=(2, 4), land_spec=pl.BlockSpec((N_DEV, None, 88, D_MODEL), lambda l, r: (0, l, r, 0)),
                p_spec=pl.BlockSpec((None, 88, D_MODEL), lambda l, r: (l, r, 0)))
    updates = {
        "na_w_qkv": _adamw("adamw_na_qkv", landing[0], na_w_qkv, m_na_w_qkv, v_na_w_qkv, **by_rows(256, 4, na_cols)),
        "na_w_o": _adamw("adamw_na_o", landing[1], na_w_o, m_na_w_o, v_na_w_o, **whole(o_rows, D_MODEL)),
        "dil_w_qkv": _adamw("adamw_dil_qkv", landing[2], dil_w_qkv, m_dil_w_qkv, v_dil_w_qkv, **by_rows(128, 8, dil_cols)),
        "dil_w_o": _adamw("adamw_dil_o", landing[3], dil_w_o, m_dil_w_o, v_dil_w_o, **whole(o_rows, D_MODEL)),
        "ffn_w_gate": _adamw("adamw_gate", landing[4], ffn_w_gate, m_ffn_w_gate, v_ffn_w_gate, **gate_up(0)),
        "ffn_w_up": _adamw("adamw_up", landing[4], ffn_w_up, m_ffn_w_up, v_ffn_w_up, **gate_up(1)),
        "ffn_w_down": _adamw("adamw_down", landing[5], ffn_w_down, m_ffn_w_down, v_ffn_w_down, **down),
    }
    gains = [norm_mix_pre, norm_mix_post, norm_ffn_pre, norm_ffn_post]
    m_gains = [m_norm_mix_pre, m_norm_mix_post, m_norm_ffn_pre, m_norm_ffn_post]
    v_gains = [v_norm_mix_pre, v_norm_mix_post, v_norm_ffn_pre, v_norm_ffn_post]
    packed = _adamw("adamw_small", small, _pack_small(gains, na_rpb)[None], _pack_small(m_gains, m_na_rpb)[None],
                    _pack_small(v_gains, v_na_rpb)[None], **whole(128, 128))
    small_out = [_unpack_small(p[0]) for p in packed]

    order = ["na_w_qkv", "na_w_o", "na_rpb", "dil_w_qkv", "dil_w_o", "ffn_w_gate", "ffn_w_up", "ffn_w_down"]
    result = [loss, grad_x[None]]
    for kind in range(4):
        gains_k, rpb_k = small_out[kind]
        result += gains_k
        result += [rpb_k if name == "na_rpb" else updates[name][kind] for name in order]
    return tuple(result)
```

```python
import functools

import jax
import jax.numpy as jnp
from jax import lax
from jax.experimental import pallas as pl
from jax.experimental.pallas import tpu as pltpu

F32 = jnp.float32
BF16 = jnp.bfloat16
MESH = pl.DeviceIdType.MESH
ANY = pl.BlockSpec(memory_space=pl.ANY)
HBM = pl.BlockSpec(memory_space=pltpu.HBM)
SEM = pl.BlockSpec(memory_space=pltpu.SEMAPHORE)
EFFECT = pltpu.SideEffectType.DATAFLOW_SIDE_EFFECTING

N_DEV = 8
SEQ = 2048
D_MODEL = 1024
N_HEADS = 16
HEAD_DIM = 64
GRID_W = 64
NA_ROWS = 8
SEQ_ROWS = SEQ // GRID_W
DIL_GROUPS = ((128, 1), (512, 4), (2048, 16))
BAND = 128
RADIUS = 64
FF_SHARD = 352
FF_PAD = 384
RMS_EPS = 1e-6
NEG_INF = -1e30
Q_SCALE = HEAD_DIM ** -0.5

ADAM_LR = 0.001
ADAM_B1 = 0.9
ADAM_B2 = 0.999
ADAM_EPS = 1e-08
ADAM_WD = 0.01
ADAM_STEP = 10

VMEM_LIMIT = 56 * 1024 * 1024
TM = 512

NN = (((1,), (0,)), ((), ()))
NT = (((1,), (1,)), ((), ()))
TN = (((0,), (0,)), ((), ()))


def _params():
    return pltpu.CompilerParams(vmem_limit_bytes=VMEM_LIMIT)


def _matmul(name, a, b, *, grid, a_spec, b_spec, o_spec, out_shape, dims, acc_shape, deps=()):
    nk = grid[-1]
    kaxis = len(grid) - 1

    def body(a_ref, b_ref, *rest):
        o_ref, acc_ref = rest[-2], rest[-1]
        part = lax.dot_general(a_ref[...].astype(BF16), b_ref[...].astype(BF16), dims, preferred_element_type=F32)
        if nk == 1:
            o_ref[...] = part.astype(o_ref.dtype)
        else:
            k = pl.program_id(kaxis)

            @pl.when(k == 0)
            def _():
                acc_ref[...] = part

            @pl.when(k > 0)
            def _():
                acc_ref[...] += part

            @pl.when(k == nk - 1)
            def _():
                o_ref[...] = acc_ref[...].astype(o_ref.dtype)

    return pl.pallas_call(
        body, out_shape=out_shape, grid=grid, in_specs=[a_spec, b_spec] + [ANY] * len(deps), out_specs=o_spec,
        scratch_shapes=[pltpu.VMEM(acc_shape, F32)], name=name, compiler_params=_params())(a, b, *deps)


def _rms_fwd(name, x, g, res=None, out_dtype=F32, deps=()):
    n_tiles = SEQ // TM
    has_res = res is not None

    def body(*refs):
        x_ref, g_ref = refs[0], refs[1]
        o_ref = refs[-1]
        xv = x_ref[...]
        r = lax.rsqrt(jnp.mean(xv * xv, axis=-1, keepdims=True) + RMS_EPS)
        y = xv * r * g_ref[...]
        if has_res:
            y = refs[2][...] + y
        o_ref[...] = y.astype(o_ref.dtype)

    tile = pl.BlockSpec((TM, D_MODEL), lambda i: (i, 0))
    gspec = pl.BlockSpec((1, D_MODEL), lambda i: (0, 0))
    ins = [x, g] + ([res] if has_res else []) + list(deps)
    specs = [tile, gspec] + ([tile] if has_res else []) + [ANY] * len(deps)
    return pl.pallas_call(
        body, out_shape=jax.ShapeDtypeStruct((SEQ, D_MODEL), out_dtype), grid=(n_tiles,), in_specs=specs,
        out_specs=tile, name=name, compiler_params=_params())(*ins)


def _rms_bwd(name, x, g, dys, res=None, out_dtype=F32):
    n_tiles = SEQ // TM
    n_dy = len(dys)
    has_res = res is not None

    def body(*refs):
        x_ref, g_ref = refs[0], refs[1]
        dy_refs = refs[2:2 + n_dy]
        res_ref = refs[2 + n_dy] if has_res else None
        dx_ref, dg_ref, acc_ref = refs[-3], refs[-2], refs[-1]
        i = pl.program_id(0)
        xv = x_ref[...]
        r = lax.rsqrt(jnp.mean(xv * xv, axis=-1, keepdims=True) + RMS_EPS)
        xn = xv * r
        dy = dy_refs[0][...].astype(F32)
        for extra in dy_refs[1:]:
            dy = dy + extra[...].astype(F32)
        dyg = dy * g_ref[...]
        dx = r * (dyg - xn * jnp.mean(dyg * xn, axis=-1, keepdims=True))
        if has_res:
            dx = res_ref[...] + dx
        dx_ref[...] = dx.astype(dx_ref.dtype)
        part = jnp.sum((dy * xn).reshape(TM // 8, 8, D_MODEL), axis=0)

        @pl.when(i == 0)
        def _():
            acc_ref[...] = part

        @pl.when(i > 0)
        def _():
            acc_ref[...] += part

        @pl.when(i == n_tiles - 1)
        def _():
            dg_ref[...] = jnp.broadcast_to(jnp.sum(acc_ref[...], axis=0, keepdims=True), (8, D_MODEL))

    tile = pl.BlockSpec((TM, D_MODEL), lambda i: (i, 0))
    gspec = pl.BlockSpec((1, D_MODEL), lambda i: (0, 0))
    ins = [x, g] + list(dys) + ([res] if has_res else [])
    specs = [tile, gspec] + [tile] * n_dy + ([tile] if has_res else [])
    dx, dg = pl.pallas_call(
        body, out_shape=(jax.ShapeDtypeStruct((SEQ, D_MODEL), out_dtype), jax.ShapeDtypeStruct((8, D_MODEL), F32)),
        grid=(n_tiles,), in_specs=specs,
        out_specs=(tile, pl.BlockSpec((8, D_MODEL), lambda i: (0, 0))),
        scratch_shapes=[pltpu.VMEM((8, D_MODEL), F32)], name=name, compiler_params=_params())(*ins)
    return dx, dg[0:1]


def _loss_head(name, y, target):
    n_tiles = SEQ // TM

    def body(y_ref, t_ref, dy_ref, loss_ref, acc_ref):
        i = pl.program_id(0)
        diff = y_ref[...] - t_ref[...]
        dy_ref[...] = diff * (1.0 / D_MODEL)
        part = jnp.sum((diff * diff).reshape(TM // 8, 8, D_MODEL), axis=0)

        @pl.when(i == 0)
        def _():
            acc_ref[...] = part

        @pl.when(i > 0)
        def _():
            acc_ref[...] += part

        @pl.when(i == n_tiles - 1)
        def _():
            loss_ref[...] = jnp.full((8, 128), jnp.sum(acc_ref[...]) * (0.5 / D_MODEL), F32)

    tile = pl.BlockSpec((TM, D_MODEL), lambda i: (i, 0))
    dy, loss = pl.pallas_call(
        body, out_shape=(jax.ShapeDtypeStruct((SEQ, D_MODEL), F32), jax.ShapeDtypeStruct((8, 128), F32)),
        grid=(n_tiles,), in_specs=[tile, tile], out_specs=(tile, pl.BlockSpec((8, 128), lambda i: (0, 0))),
        scratch_shapes=[pltpu.VMEM((8, D_MODEL), F32)], name=name, compiler_params=_params())(y, target)
    return dy, loss[0, 0]


def _row_index(shape):
    return lax.broadcasted_iota(jnp.int32, shape, 0)


def _lane_index(shape):
    return lax.broadcasted_iota(jnp.int32, shape, len(shape) - 1)


def _skew_rows(t, direction):
    row = _row_index(t.shape)
    for bit in range(6):
        step = 1 << bit
        shift = step if direction > 0 else 128 - step
        t = jnp.where((row & step) != 0, pltpu.roll(t, shift, 1), t)
    return t


def _rpb_table(rpb_pad):
    def body(r_ref, t_ref):
        lane = _lane_index((GRID_W, 128))
        tiles = []
        for dr in range(2 * NA_ROWS - 1):
            v = pltpu.roll(r_ref[dr:dr + 1, :], 128 - 15, 1)
            t = _skew_rows(jnp.broadcast_to(v, (GRID_W, 128)), +1)
            tiles.append(jnp.where(lane < GRID_W, t, 0.0))
        for p in range(2 * NA_ROWS - 2):
            t_ref[p] = tiles[p] + pltpu.roll(tiles[p + 1], GRID_W, 1)
        t_ref[14] = jnp.zeros((GRID_W, 128), F32)
        t_ref[15] = jnp.zeros((GRID_W, 128), F32)

    return pl.pallas_call(
        body, out_shape=jax.ShapeDtypeStruct((N_HEADS, 16, GRID_W, 128), F32), grid=(N_HEADS,),
        in_specs=[pl.BlockSpec((None, 16, 128), lambda h: (h, 0, 0))],
        out_specs=pl.BlockSpec((None, 16, GRID_W, 128), lambda h: (h, 0, 0, 0)),
        name="rpb_table", compiler_params=_params())(rpb_pad)


def _rpb_grad(gp):
    def body(g_ref, o_ref):
        lane = _lane_index((GRID_W, 128))
        rows = [jnp.zeros((1, 128), F32) for _ in range(16)]
        for p in range(2 * NA_ROWS - 2):
            t = g_ref[p]
            halves = (jnp.where(lane < GRID_W, t, 0.0), pltpu.roll(jnp.where(lane >= GRID_W, t, 0.0), GRID_W, 1))
            for j, half in enumerate(halves):
                diag = jnp.sum(_skew_rows(half, -1), axis=0, keepdims=True)
                rows[p + j] = rows[p + j] + pltpu.roll(diag, 15, 1)
        o_ref[...] = jnp.concatenate(rows, axis=0)

    return pl.pallas_call(
        body, out_shape=jax.ShapeDtypeStruct((N_HEADS, 16, 128), F32), grid=(N_HEADS,),
        in_specs=[pl.BlockSpec((None, 16, GRID_W, 128), lambda h: (h, 0, 0, 0))],
        out_specs=pl.BlockSpec((None, 16, 128), lambda h: (h, 0, 0)),
        name="rpb_grad", compiler_params=_params())(gp)


NA_KEYS = NA_ROWS * GRID_W


def _na_window(i):
    first_row = jnp.clip(i - NA_ROWS // 2, 0, SEQ_ROWS - NA_ROWS)
    return pl.multiple_of(first_row * GRID_W, GRID_W), first_row - i + NA_ROWS - 1


def _na_valid():
    q = _row_index((GRID_W, NA_KEYS))
    k = _lane_index((GRID_W, NA_KEYS)) & (GRID_W - 1)
    first_col = jnp.clip(q - 8, 0, GRID_W - 16)
    return (k >= first_col) & (k < first_col + 16)


def _head_masks():
    lane = _lane_index((1, 128))
    return (lane < HEAD_DIM, lane >= HEAD_DIM)


def _na_scores(q2, kw, tp_ref, a, dr0, mask, valid):
    s = lax.dot_general(jnp.where(mask, q2, jnp.zeros_like(q2)), kw, NT, preferred_element_type=F32)
    bias = jnp.concatenate([tp_ref[a, pl.ds(dr0 + 2 * c, 1), :, :].reshape(GRID_W, 128) for c in range(4)], axis=1)
    return jnp.where(valid, s + bias, NEG_INF)


def _na_specs():
    q_spec = pl.BlockSpec((GRID_W, 128), lambda hp, i: (i, hp))
    k_spec = pl.BlockSpec((SEQ, 128), lambda hp, i: (0, 8 + hp))
    v_spec = pl.BlockSpec((SEQ, 128), lambda hp, i: (0, 16 + hp))
    tp_spec = pl.BlockSpec((2, 16, GRID_W, 128), lambda hp, i: (hp, 0, 0, 0))
    return q_spec, k_spec, v_spec, tp_spec


def _na_fwd(qkv, table):
    def body(q_ref, k_ref, v_ref, tp_ref, o_ref, lse_ref):
        start, dr0 = _na_window(pl.program_id(1))
        kw = k_ref[pl.ds(start, NA_KEYS), :]
        vw = v_ref[pl.ds(start, NA_KEYS), :]
        q2 = q_ref[...] * Q_SCALE
        valid = _na_valid()
        masks = _head_masks()
        outs, lses = [], []
        for a in range(2):
            s = _na_scores(q2, kw, tp_ref, a, dr0, masks[a], valid)
            m = jnp.max(s, axis=-1, keepdims=True)
            p = jnp.exp(s - m)
            denom = jnp.sum(p, axis=-1, keepdims=True)
            outs.append(jnp.dot(p.astype(BF16), vw, preferred_element_type=F32) / denom)
            lses.append(m + jnp.log(denom))
        o_ref[...] = jnp.where(masks[0], outs[0], outs[1]).astype(o_ref.dtype)
        lse_ref[...] = jnp.where(masks[0], lses[0], lses[1])

    q_spec, k_spec, v_spec, tp_spec = _na_specs()
    return pl.pallas_call(
        body, out_shape=(jax.ShapeDtypeStruct((SEQ, D_MODEL), BF16), jax.ShapeDtypeStruct((SEQ, D_MODEL), F32)),
        grid=(N_HEADS // 2, SEQ_ROWS), in_specs=[q_spec, k_spec, v_spec, tp_spec],
        out_specs=(q_spec, q_spec), name="na_fwd", compiler_params=_params())(qkv, qkv, qkv, table)


def _na_bwd(qkv, table, d_out, lse):
    def body(q_ref, k_ref, v_ref, tp_ref, do_ref, lse_ref, dqkv_ref, gp_ref, dk_acc, dv_acc):
        i = pl.program_id(1)
        start, dr0 = _na_window(i)

        @pl.when(i == 0)
        def _():
            dk_acc[...] = jnp.zeros_like(dk_acc)
            dv_acc[...] = jnp.zeros_like(dv_acc)
            gp_ref[...] = jnp.zeros_like(gp_ref)

        kw = k_ref[pl.ds(start, NA_KEYS), :]
        vw = v_ref[pl.ds(start, NA_KEYS), :]
        q2 = q_ref[...] * Q_SCALE
        do2 = do_ref[...]
        lse2 = lse_ref[...]
        valid = _na_valid()
        masks = _head_masks()
        dqs, dks, dvs = [], [], []
        for a in range(2):
            s = _na_scores(q2, kw, tp_ref, a, dr0, masks[a], valid)
            p = jnp.exp(s - lse2[:, a * HEAD_DIM:a * HEAD_DIM + 1])
            dp = lax.dot_general(jnp.where(masks[a], do2, jnp.zeros_like(do2)), vw, NT, preferred_element_type=F32)
            ds = p * (dp - jnp.sum(p * dp, axis=-1, keepdims=True))
            for c in range(4):
                gp_ref[a, pl.ds(dr0 + 2 * c, 1), :, :] += ds[:, 128 * c:128 * (c + 1)].reshape(1, GRID_W, 128)
            dsb = ds.astype(BF16)
            dqs.append(jnp.dot(dsb, kw, preferred_element_type=F32))
            dks.append(lax.dot_general(dsb, q2, TN, preferred_element_type=F32))
            dvs.append(lax.dot_general(p.astype(BF16), do2, TN, preferred_element_type=F32))
        dq = jnp.where(masks[0], dqs[0], dqs[1]) * Q_SCALE
        dqkv_ref[0, pl.ds(pl.multiple_of(i * GRID_W, GRID_W), GRID_W), :] = dq.astype(dqkv_ref.dtype)
        dk_acc[pl.ds(start, NA_KEYS), :] += jnp.where(masks[0], dks[0], dks[1])
        dv_acc[pl.ds(start, NA_KEYS), :] += jnp.where(masks[0], dvs[0], dvs[1])

        @pl.when(i == SEQ_ROWS - 1)
        def _():
            dqkv_ref[1] = dk_acc[...].astype(dqkv_ref.dtype)
            dqkv_ref[2] = dv_acc[...].astype(dqkv_ref.dtype)

    q_spec, k_spec, v_spec, tp_spec = _na_specs()
    return pl.pallas_call(
        body,
        out_shape=(jax.ShapeDtypeStruct((3, SEQ, D_MODEL), BF16), jax.ShapeDtypeStruct((N_HEADS, 16, GRID_W, 128), F32)),
        grid=(N_HEADS // 2, SEQ_ROWS), in_specs=[q_spec, k_spec, v_spec, tp_spec, q_spec, q_spec],
        out_specs=(pl.BlockSpec((3, SEQ, 128), lambda hp, i: (0, 0, hp)), tp_spec),
        scratch_shapes=[pltpu.VMEM((SEQ, 128), F32), pltpu.VMEM((SEQ, 128), F32)],
        name="na_bwd", compiler_params=_params())(qkv, qkv, qkv, table, d_out, lse)


def _dil_geometry(group):
    dil = DIL_GROUPS[group][1]
    sub_len = SEQ // dil
    return dil, sub_len, sub_len // BAND, min(2 * BAND, sub_len)


def _dil_window(b, sub_len, n_keys):
    if n_keys == sub_len:
        return 0
    return pl.multiple_of(jnp.clip(b * BAND - RADIUS, 0, sub_len - n_keys), RADIUS)


def _dil_distance(b, start, n_keys):
    qpos = b * BAND + _row_index((BAND, n_keys))
    kpos = start + _lane_index((BAND, n_keys))
    dist = jnp.abs(qpos - kpos)
    return dist.astype(F32), dist <= RADIUS


def _dil_scores(q2, kw, slope, dist, valid, mask):
    s = lax.dot_general(jnp.where(mask, q2, jnp.zeros_like(q2)), kw, NT, preferred_element_type=F32)
    return jnp.where(valid, s - slope * dist, NEG_INF)


def _dil_specs(group):
    dil, sub_len, nb, _ = _dil_geometry(group)
    col = group * 24
    q_spec = pl.BlockSpec((BAND, 128), lambda n, hp, b: (n * nb + b, col + hp))
    k_spec = pl.BlockSpec((sub_len, 128), lambda n, hp, b: (n, col + 8 + hp))
    v_spec = pl.BlockSpec((sub_len, 128), lambda n, hp, b: (n, col + 16 + hp))
    tile = pl.BlockSpec((BAND, 128), lambda n, hp, b: (n * nb + b, hp))
    smem = pl.BlockSpec(memory_space=pltpu.SMEM)
    return (dil, N_HEADS // 2, nb), q_spec, k_spec, v_spec, tile, smem


def _dil_fwd(group, qkv, slopes):
    _, sub_len, _, n_keys = _dil_geometry(group)

    def body(q_ref, k_ref, v_ref, slope_ref, o_ref, lse_ref):
        hp, b = pl.program_id(1), pl.program_id(2)
        start = _dil_window(b, sub_len, n_keys)
        kw = k_ref[pl.ds(start, n_keys), :]
        vw = v_ref[pl.ds(start, n_keys), :]
        q2 = q_ref[...] * Q_SCALE
        dist, valid = _dil_distance(b, start, n_keys)
        masks = _head_masks()
        outs, lses = [], []
        for a in range(2):
            s = _dil_scores(q2, kw, slope_ref[2 * hp + a], dist, valid, masks[a])
            m = jnp.max(s, axis=-1, keepdims=True)
            p = jnp.exp(s - m)
            denom = jnp.sum(p, axis=-1, keepdims=True)
            outs.append(jnp.dot(p.astype(BF16), vw, preferred_element_type=F32) / denom)
            lses.append(m + jnp.log(denom))
        o_ref[...] = jnp.where(masks[0], outs[0], outs[1])
        lse_ref[...] = jnp.where(masks[0], lses[0], lses[1])

    grid, q_spec, k_spec, v_spec, tile, smem = _dil_specs(group)
    return pl.pallas_call(
        body, out_shape=(jax.ShapeDtypeStruct((SEQ, D_MODEL), F32), jax.ShapeDtypeStruct((SEQ, D_MODEL), F32)),
        grid=grid, in_specs=[q_spec, k_spec, v_spec, smem], out_specs=(tile, tile),
        name=f"dil_fwd_{group}", compiler_params=_params())(qkv, qkv, qkv, slopes)


def _dil_merge(outs, lses):
    def body(o0, o1, o2, l0, l1, l2, o_ref, lse_ref):
        ls = [l0[...], l1[...], l2[...]]
        m = jnp.maximum(jnp.maximum(ls[0], ls[1]), ls[2])
        es = [jnp.exp(v - m) for v in ls]
        total = es[0] + es[1] + es[2]
        o_ref[...] = (es[0] * o0[...] + es[1] * o1[...] + es[2] * o2[...]) / total
        lse_ref[...] = m + jnp.log(total)

    tile = pl.BlockSpec((TM, D_MODEL), lambda i: (i, 0))
    return pl.pallas_call(
        body, out_shape=(jax.ShapeDtypeStruct((SEQ, D_MODEL), F32), jax.ShapeDtypeStruct((SEQ, D_MODEL), F32)),
        grid=(SEQ // TM,), in_specs=[tile] * 6, out_specs=(tile, tile), name="dil_merge",
        compiler_params=_params())(*outs, *lses)


def _dil_bwd(group, qkv, slopes, d_out, out, lse_group, lse_total):
    _, sub_len, nb, n_keys = _dil_geometry(group)

    def body(q_ref, k_ref, v_ref, slope_ref, do_ref, o_ref, lg_ref, lt_ref, dqkv_ref, dk_acc, dv_acc):
        hp, b = pl.program_id(1), pl.program_id(2)
        start = _dil_window(b, sub_len, n_keys)

        @pl.when(b == 0)
        def _():
            dk_acc[...] = jnp.zeros_like(dk_acc)
            dv_acc[...] = jnp.zeros_like(dv_acc)

        kw = k_ref[pl.ds(start, n_keys), :]
        vw = v_ref[pl.ds(start, n_keys), :]
        q2 = q_ref[...] * Q_SCALE
        dist, valid = _dil_distance(b, start, n_keys)
        masks = _head_masks()
        lse2 = lg_ref[...]
        weight = jnp.exp(lse2 - lt_ref[...])
        do2 = do_ref[...]
        dog = (weight * do2).astype(BF16)
        do_o = do2 * o_ref[...]
        dqs, dks, dvs = [], [], []
        for a in range(2):
            col = slice(a * HEAD_DIM, a * HEAD_DIM + 1)
            s = _dil_scores(q2, kw, slope_ref[2 * hp + a], dist, valid, masks[a])
            p = jnp.exp(s - lse2[:, col])
            delta = weight[:, col] * jnp.sum(jnp.where(masks[a], do_o, 0.0), axis=-1, keepdims=True)
            dp = lax.dot_general(jnp.where(masks[a], dog, jnp.zeros_like(dog)), vw, NT, preferred_element_type=F32)
            dsb = (p * (dp - delta)).astype(BF16)
            dqs.append(jnp.dot(dsb, kw, preferred_element_type=F32))
            dks.append(lax.dot_general(dsb, q2, TN, preferred_element_type=F32))
            dvs.append(lax.dot_general(p.astype(BF16), dog, TN, preferred_element_type=F32))
        dq = jnp.where(masks[0], dqs[0], dqs[1]) * Q_SCALE
        dqkv_ref[0, pl.ds(pl.multiple_of(b * BAND, BAND), BAND), :] = dq.astype(dqkv_ref.dtype)
        dk_acc[pl.ds(start, n_keys), :] += jnp.where(masks[0], dks[0], dks[1])
        dv_acc[pl.ds(start, n_keys), :] += jnp.where(masks[0], dvs[0], dvs[1])

        @pl.when(b == nb - 1)
        def _():
            dqkv_ref[1] = dk_acc[...].astype(dqkv_ref.dtype)
            dqkv_ref[2] = dv_acc[...].astype(dqkv_ref.dtype)

    grid, q_spec, k_spec, v_spec, tile, smem = _dil_specs(group)
    return pl.pallas_call(
        body, out_shape=jax.ShapeDtypeStruct((3, SEQ, D_MODEL), BF16), grid=grid,
        in_specs=[q_spec, k_spec, v_spec, smem, tile, tile, tile, tile],
        out_specs=pl.BlockSpec((3, sub_len, 128), lambda n, hp, b: (0, n, hp)),
        scratch_shapes=[pltpu.VMEM((sub_len, 128), F32), pltpu.VMEM((sub_len, 128), F32)],
        name=f"dil_bwd_{group}", compiler_params=_params())(qkv, qkv, qkv, slopes, d_out, out, lse_group, lse_total)


def _sort_rows(t, dil):
    if dil == 1:
        return t
    return t.reshape(SEQ // dil, dil, t.shape[-1]).transpose(1, 0, 2).reshape(SEQ, t.shape[-1])


def _unsort_rows(t, dil):
    if dil == 1:
        return t
    return t.reshape(dil, SEQ // dil, t.shape[-1]).transpose(1, 0, 2).reshape(SEQ, t.shape[-1])


def _ffn_up(name, h, w_gu):
    def body(h_ref, wg_ref, wu_ref, act_ref, hg_ref, hu_ref):
        hv = h_ref[...]
        hg = jnp.dot(hv, wg_ref[...], preferred_element_type=F32)
        hu = jnp.dot(hv, wu_ref[...], preferred_element_type=F32)
        act_ref[...] = (hg * jax.nn.sigmoid(hg) * hu).astype(act_ref.dtype)
        hg_ref[...] = hg.astype(hg_ref.dtype)
        hu_ref[...] = hu.astype(hu_ref.dtype)

    out = pl.BlockSpec((None, TM, FF_PAD), lambda d, i: (d, i, 0))
    shape = jax.ShapeDtypeStruct((N_DEV, SEQ, FF_PAD), BF16)
    return pl.pallas_call(
        body, out_shape=(shape, shape, shape), grid=(N_DEV, SEQ // TM),
        in_specs=[pl.BlockSpec((TM, D_MODEL), lambda d, i: (i, 0)),
                  pl.BlockSpec((None, None, D_MODEL, FF_PAD), lambda d, i: (d, 0, 0, 0)),
                  pl.BlockSpec((None, None, D_MODEL, FF_PAD), lambda d, i: (d, 1, 0, 0))],
        out_specs=(out, out, out), name=name, compiler_params=_params())(h, w_gu, w_gu)


def _ffn_bwd_act(name, d_f, w_down, hg, hu):
    def body(df_ref, wd_ref, hg_ref, hu_ref, dgu_ref):
        dact = lax.dot_general(df_ref[...], wd_ref[...], NT, preferred_element_type=F32)
        hgv = hg_ref[...].astype(F32)
        sig = jax.nn.sigmoid(hgv)
        dgu_ref[0] = (dact * hu_ref[...].astype(F32) * (sig * (1.0 + hgv * (1.0 - sig)))).astype(dgu_ref.dtype)
        dgu_ref[1] = (dact * hgv * sig).astype(dgu_ref.dtype)

    tile = pl.BlockSpec((None, TM, FF_PAD), lambda d, i: (d, i, 0))
    return pl.pallas_call(
        body, out_shape=jax.ShapeDtypeStruct((2, N_DEV, SEQ, FF_PAD), BF16), grid=(N_DEV, SEQ // TM),
        in_specs=[pl.BlockSpec((TM, D_MODEL), lambda d, i: (i, 0)),
                  pl.BlockSpec((None, FF_PAD, D_MODEL), lambda d, i: (d, 0, 0)), tile, tile],
        out_specs=pl.BlockSpec((2, None, TM, FF_PAD), lambda d, i: (0, d, i, 0)),
        name=name, compiler_params=_params())(d_f, w_down, hg, hu)


def _position():
    return lax.axis_index("x"), lax.axis_index("y"), lax.axis_index("c")


def _flat(p):
    return 4 * p[0] + 2 * p[1] + p[2]


def _peer(me, k):
    x, y, c = me
    return (1 - x if k & 4 else x, 1 - y if k & 2 else y, 1 - c if k & 1 else c)


def _columns(width):
    return lambda ref, d: ref.at[:, pl.ds(pl.multiple_of(d * width, 128), width)]


def _leading(ref, d):
    return ref.at[d]


def _whole(ref, d):
    return ref


def _place_own(name, srcs, land_shapes, sets):
    n_src, n_land = len(srcs), len(land_shapes)

    def body(*refs):
        src_refs, land_refs, sems = refs[:n_src], refs[n_src:n_src + n_land], refs[-1]
        me = _flat(_position())
        copies = [pltpu.make_async_copy(src_of(src_refs[si], me), dst_of(land_refs[li], me), sems.at[t])
                  for t, (si, src_of, li, dst_of) in enumerate(sets)]
        for cp in copies:
            cp.start()
        for cp in copies:
            cp.wait()

    return pl.pallas_call(
        body, out_shape=tuple(land_shapes), in_specs=[ANY] * n_src, out_specs=tuple([ANY] * n_land),
        scratch_shapes=[pltpu.SemaphoreType.DMA((len(sets),))], name=name)(*srcs)


def _remote_copies(sets, src_refs, land_refs, send_sems, recv_sems, outgoing):
    me = _position()
    copies = []
    for t, (si, src_of, li, dst_of) in enumerate(sets):
        for k in range(1, N_DEV):
            other = _peer(me, k)
            sender = me if outgoing else other
            copies.append(pltpu.make_async_remote_copy(
                src_ref=src_of(src_refs[si], _flat(other)), dst_ref=dst_of(land_refs[li], _flat(sender)),
                send_sem=send_sems.at[(N_DEV - 1) * t + k - 1], recv_sem=recv_sems.at[(N_DEV - 1) * t + k - 1],
                device_id=other, device_id_type=MESH))
    return copies


def _send_start(name, srcs, lands, sets_by_group):
    n_src, n_land, n_groups = len(srcs), len(lands), len(sets_by_group)

    def body(*refs):
        src_refs, land_refs = refs[:n_src], refs[n_src:n_src + n_land]
        outs = refs[n_src + n_land:]
        for g, sets in enumerate(sets_by_group):
            for cp in _remote_copies(sets, src_refs, land_refs, outs[2 * g], outs[2 * g + 1], True):
                cp.start()
        outs[-1][...] = jnp.zeros_like(outs[-1])

    sem_shapes = []
    for sets in sets_by_group:
        sem_shapes += [pltpu.SemaphoreType.DMA((len(sets) * (N_DEV - 1),))] * 2
    thru = [pltpu.HBM(a.shape, a.dtype) for a in list(srcs) + list(lands)]
    n_sem = len(sem_shapes)
    result = pl.pallas_call(
        body, out_shape=tuple(sem_shapes + thru + [jax.ShapeDtypeStruct((8, 128), F32)]),
        in_specs=[HBM] * (n_src + n_land),
        out_specs=tuple([SEM] * n_sem + [HBM] * (n_src + n_land) + [pl.BlockSpec(memory_space=pltpu.VMEM)]),
        input_output_aliases={i: n_sem + i for i in range(n_src + n_land)},
        compiler_params=pltpu.CompilerParams(has_side_effects=EFFECT), name=name,
    )(*[pltpu.with_memory_space_constraint(a, pltpu.HBM) for a in list(srcs) + list(lands)])
    sems = [(result[2 * g], result[2 * g + 1]) for g in range(n_groups)]
    return sems, list(result[n_sem:n_sem + n_src]), list(result[n_sem + n_src:n_sem + n_src + n_land]), result[-1]


def _send_wait(name, sems, srcs, lands, sets, after):
    n_src, n_land = len(srcs), len(lands)

    def body(*refs):
        src_refs, land_refs = refs[:n_src], refs[n_src:n_src + n_land]
        send_sems, recv_sems = refs[n_src + n_land], refs[n_src + n_land + 1]
        for cp in _remote_copies(sets, src_refs, land_refs, send_sems, recv_sems, True):
            cp.wait_send()
        for cp in _remote_copies(sets, src_refs, land_refs, send_sems, recv_sems, False):
            cp.wait_recv()

    thru = [pltpu.HBM(a.shape, a.dtype) for a in list(srcs) + list(lands)]
    result = pl.pallas_call(
        body, out_shape=tuple(thru), in_specs=[HBM] * (n_src + n_land) + [SEM, SEM] + [ANY] * len(after),
        out_specs=tuple([HBM] * (n_src + n_land)), input_output_aliases={i: i for i in range(n_src + n_land)},
        compiler_params=pltpu.CompilerParams(has_side_effects=EFFECT), name=name,
    )(*srcs, *lands, sems[0], sems[1], *after)
    return list(result[n_src:])


def _all_gather(name, locals_, out_shapes, windows):
    n = len(locals_)

    def body(*refs):
        src_refs, out_refs = refs[:n], refs[n:2 * n]
        send_sems, recv_sems, local_sems = refs[2 * n:]
        x, y, c = _position()
        me, sibling = (x, y, c), (x, y, 1 - c)
        chips = [(1 - x, y), (x, 1 - y), (1 - x, 1 - y)]

        def copy(t, k, block, to, from_local=False):
            dst = windows[t](out_refs[t], _flat(block))
            return pltpu.make_async_remote_copy(
                src_ref=src_refs[t] if from_local else dst, dst_ref=dst, send_sem=send_sems.at[t, k],
                recv_sem=recv_sems.at[t, k], device_id=to, device_id_type=MESH)

        mine = [pltpu.make_async_copy(src_refs[t], windows[t](out_refs[t], _flat(me)), local_sems.at[t]) for t in range(n)]
        sends = []
        for t in range(n):
            mine[t].start()
            sends.append(copy(t, 0, me, sibling, True))
            sends += [copy(t, 1 + j, me, (*chip, c), True) for j, chip in enumerate(chips)]
        for cp in sends:
            cp.start()
        for t in range(n):
            for j, chip in enumerate(chips):
                copy(t, 1 + j, (*chip, c), me).wait_recv()
                passed = copy(t, 4 + j, (*chip, c), sibling)
                passed.start()
                sends.append(passed)
        for t in range(n):
            copy(t, 0, sibling, me).wait_recv()
            for j, chip in enumerate(chips):
                copy(t, 4 + j, (*chip, 1 - c), me).wait_recv()
        for cp in sends:
            cp.wait_send()
        for cp in mine:
            cp.wait()

    return pl.pallas_call(
        body, out_shape=tuple(out_shapes), in_specs=[ANY] * n, out_specs=tuple([ANY] * n),
        scratch_shapes=[pltpu.SemaphoreType.DMA((n, 7)), pltpu.SemaphoreType.DMA((n, 7)), pltpu.SemaphoreType.DMA((n,))],
        name=name)(*locals_)


def _adamw(name, lands, w, m, v, *, grid, land_specs, p_spec):
    n_land = len(lands)

    def body(*refs):
        land_refs = refs[:n_land]
        w_ref, m_ref, v_ref, g_ref, delta_ref, m_out, v_out = refs[n_land:]
        ncols = w_ref.shape[-1]
        sums = []
        for land_ref in land_refs:
            g = land_ref[0].astype(F32)
            for s in range(1, N_DEV):
                g = g + land_ref[s].astype(F32)
            sums.append(g[:, :ncols])
        g = sums[0] if n_land == 1 else jnp.where(pl.program_id(0) == 0, sums[0], sums[1])
        m_new = ADAM_B1 * m_ref[...] + (1.0 - ADAM_B1) * g
        v_new = ADAM_B2 * v_ref[...] + (1.0 - ADAM_B2) * jnp.square(g)
        m_hat = m_new / (1.0 - ADAM_B1 ** ADAM_STEP)
        v_hat = v_new / (1.0 - ADAM_B2 ** ADAM_STEP)
        g_ref[...] = g
        delta_ref[...] = -ADAM_LR * (m_hat / (jnp.sqrt(v_hat) + ADAM_EPS) + ADAM_WD * w_ref[...])
        m_out[...] = m_new
        v_out[...] = v_new

    shape = jax.ShapeDtypeStruct(w.shape, F32)
    return pl.pallas_call(
        body, out_shape=(shape,) * 4, grid=grid, in_specs=list(land_specs) + [p_spec, p_spec, p_spec],
        out_specs=(p_spec,) * 4, name=name, compiler_params=_params())(*lands, w, m, v)


def _row(p, layer):
    return p[layer][None, :]


def _square(name, a, b, dims, out_dtype, deps=()):
    if dims == TN:
        return _matmul(name, a, b, grid=(2, SEQ // TM), a_spec=pl.BlockSpec((TM, 512), lambda i, k: (k, i)),
                       b_spec=pl.BlockSpec((TM, D_MODEL), lambda i, k: (k, 0)),
                       o_spec=pl.BlockSpec((512, D_MODEL), lambda i, k: (i, 0)),
                       out_shape=jax.ShapeDtypeStruct((D_MODEL, D_MODEL), out_dtype), dims=TN, acc_shape=(512, D_MODEL),
                       deps=deps)
    return _matmul(name, a, b, grid=(SEQ // TM, 1), a_spec=pl.BlockSpec((TM, D_MODEL), lambda i, k: (i, 0)),
                   b_spec=pl.BlockSpec((D_MODEL, D_MODEL), lambda i, k: (0, 0)),
                   o_spec=pl.BlockSpec((TM, D_MODEL), lambda i, k: (i, 0)),
                   out_shape=jax.ShapeDtypeStruct((SEQ, D_MODEL), out_dtype), dims=dims, acc_shape=(8, 128), deps=deps)


def _qkv_fwd(name, hs, w, n_chunks):
    return _matmul(name, hs, w, grid=(n_chunks, SEQ // TM, 1),
                   a_spec=pl.BlockSpec((None, TM, D_MODEL), lambda j, i, k: (j // 3, i, 0)),
                   b_spec=pl.BlockSpec((D_MODEL, D_MODEL), lambda j, i, k: (0, j)),
                   o_spec=pl.BlockSpec((TM, D_MODEL), lambda j, i, k: (i, j)),
                   out_shape=jax.ShapeDtypeStruct((SEQ, n_chunks * D_MODEL), BF16), dims=NN, acc_shape=(8, 128))


def _qkv_dw(name, hs, dqkv, n_chunks):
    return _matmul(name, hs, dqkv, grid=(n_chunks, 2, SEQ // TM),
                   a_spec=pl.BlockSpec((None, TM, 512), lambda j, i, k: (j // 3, k, i)),
                   b_spec=pl.BlockSpec((None, TM, D_MODEL), lambda j, i, k: (j, k, 0)),
                   o_spec=pl.BlockSpec((512, D_MODEL), lambda j, i, k: (i, j)),
                   out_shape=jax.ShapeDtypeStruct((D_MODEL, n_chunks * D_MODEL), BF16), dims=TN, acc_shape=(512, D_MODEL))


def _qkv_dh(name, dqkv, w, n_chunks, deps):
    return _matmul(name, dqkv, w, grid=(n_chunks // 3, SEQ // TM, 3),
                   a_spec=pl.BlockSpec((None, TM, D_MODEL), lambda g, i, k: (3 * g + k, i, 0)),
                   b_spec=pl.BlockSpec((D_MODEL, D_MODEL), lambda g, i, k: (0, 3 * g + k)),
                   o_spec=pl.BlockSpec((None, TM, D_MODEL), lambda g, i, k: (g, i, 0)),
                   out_shape=jax.ShapeDtypeStruct((n_chunks // 3, SEQ, D_MODEL), F32), dims=NT, acc_shape=(TM, D_MODEL),
                   deps=deps)


def _local_step(x, target, norms, rpb, fetch, emit, deps):
    mix_pre, mix_post, ffn_pre, ffn_post = norms
    slopes = 2.0 ** (-8.0 * jnp.arange(1, N_HEADS + 1, dtype=F32) / N_HEADS)
    rpb_pad = jnp.pad(rpb, ((0, 0), (0, 1), (0, 128 - 31)))
    saved = []

    for layer in range(2):
        tag = f"l{layer}"
        h = _rms_fwd(tag + "_norm_mix", x, _row(mix_pre, layer), out_dtype=BF16, deps=deps if layer == 0 else ())
        if layer == 0:
            w_qkv, w_o = fetch("na", [h])
            table = _rpb_table(rpb_pad)
            qkv = _qkv_fwd(tag + "_qkv", h[None], w_qkv, 3)
            o, lse = _na_fwd(qkv, table)
            mixer = (h, qkv, o, lse, table)
        else:
            w_qkv, w_o = fetch("dil", [h])
            hs = jnp.stack([_sort_rows(h, dil) for _, dil in DIL_GROUPS])
            qkv = _qkv_fwd(tag + "_qkv", hs, w_qkv, 9)
            outs, lses, lses_sorted = [], [], []
            for g, (_, dil) in enumerate(DIL_GROUPS):
                o_g, lse_g = _dil_fwd(g, qkv, slopes * dil)
                outs.append(_unsort_rows(o_g, dil))
                lses.append(_unsort_rows(lse_g, dil))
                lses_sorted.append(lse_g)
            o, lse_total = _dil_merge(outs, lses)
            mixer = (hs, qkv, o, lses_sorted, lse_total)
        a = _square(tag + "_proj", o, w_o, NN, F32)
        x1 = _rms_fwd(tag + "_post_mix", a, _row(mix_post, layer), res=x)
        h2 = _rms_fwd(tag + "_norm_ffn", x1, _row(ffn_pre, layer), out_dtype=BF16)
        w_gu, w_down = fetch(f"ffn{layer}", [h2])
        act, hg, hu = _ffn_up(tag + "_ffn_up", h2, w_gu)
        f = _matmul(tag + "_ffn_down", act, w_down, grid=(SEQ // TM, N_DEV),
                    a_spec=pl.BlockSpec((None, TM, FF_PAD), lambda i, k: (k, i, 0)),
                    b_spec=pl.BlockSpec((None, FF_PAD, D_MODEL), lambda i, k: (k, 0, 0)),
                    o_spec=pl.BlockSpec((TM, D_MODEL), lambda i, k: (i, 0)),
                    out_shape=jax.ShapeDtypeStruct((SEQ, D_MODEL), F32), dims=NN, acc_shape=(TM, D_MODEL))
        x2 = _rms_fwd(tag + "_post_ffn", f, _row(ffn_post, layer), res=x1)
        saved.append((x, mixer, a, x1, h2, act, hg, hu, f, w_qkv, w_o, w_gu, w_down))
        x = x2

    dx, loss = _loss_head("loss_head", x, target)
    d_norm = {k: [None, None] for k in ("mix_pre", "mix_post", "ffn_pre", "ffn_post")}
    d_rpb = None

    for layer in (1, 0):
        tag = f"b{layer}"
        x0, mixer, a, x1, h2, act, hg, hu, f, w_qkv, w_o, w_gu, w_down = saved[layer]
        d_f, d_norm["ffn_post"][layer] = _rms_bwd(tag + "_post_ffn", f, _row(ffn_post, layer), [dx], out_dtype=BF16)
        dgu = _ffn_bwd_act(tag + "_ffn_act", d_f, w_down, hg, hu)
        d_down = _matmul(
            tag + "_ffn_ddown", act, d_f, grid=(N_DEV, SEQ // TM),
            a_spec=pl.BlockSpec((None, TM, FF_PAD), lambda d, k: (d, k, 0)),
            b_spec=pl.BlockSpec((TM, D_MODEL), lambda d, k: (k, 0)),
            o_spec=pl.BlockSpec((None, FF_PAD, D_MODEL), lambda d, k: (d, 0, 0)),
            out_shape=jax.ShapeDtypeStruct((N_DEV, FF_PAD, D_MODEL), BF16), dims=TN, acc_shape=(FF_PAD, D_MODEL))
        d_gu = _matmul(
            tag + "_ffn_dgu", h2, dgu, grid=(2, N_DEV, SEQ // TM),
            a_spec=pl.BlockSpec((TM, D_MODEL), lambda t, d, k: (k, 0)),
            b_spec=pl.BlockSpec((None, None, TM, FF_PAD), lambda t, d, k: (t, d, k, 0)),
            o_spec=pl.BlockSpec((None, None, D_MODEL, FF_PAD), lambda t, d, k: (d, t, 0, 0)),
            out_shape=jax.ShapeDtypeStruct((N_DEV, 2, D_MODEL, FF_PAD), BF16), dims=TN, acc_shape=(D_MODEL, FF_PAD))
        sent = emit(f"ffn{layer}", [d_gu, d_down])
        d_h2 = _matmul(
            tag + "_ffn_dh", dgu, w_gu, grid=(SEQ // TM, 2 * N_DEV),
            a_spec=pl.BlockSpec((None, None, TM, FF_PAD), lambda i, k: (k // N_DEV, k % N_DEV, i, 0)),
            b_spec=pl.BlockSpec((None, None, D_MODEL, FF_PAD), lambda i, k: (k % N_DEV, k // N_DEV, 0, 0)),
            o_spec=pl.BlockSpec((TM, D_MODEL), lambda i, k: (i, 0)),
            out_shape=jax.ShapeDtypeStruct((SEQ, D_MODEL), F32), dims=NT, acc_shape=(TM, D_MODEL), deps=sent)
        dx1, d_norm["ffn_pre"][layer] = _rms_bwd(tag + "_norm_ffn", x1, _row(ffn_pre, layer), [d_h2], res=dx)
        d_a, d_norm["mix_post"][layer] = _rms_bwd(tag + "_post_mix", a, _row(mix_post, layer), [dx1], out_dtype=BF16)
        d_wo = _square(tag + "_proj_dw", mixer[2], d_a, TN, BF16)
        if layer == 0:
            h, qkv, o, lse, table = mixer
            d_o = _square(tag + "_proj_do", d_a, w_o, NT, BF16)
            dqkv, gp = _na_bwd(qkv, table, d_o, lse)
            d_rpb = _rpb_grad(gp)[:, :15, :31]
            sent = emit("na", [_qkv_dw(tag + "_qkv_dw", h[None], dqkv, 3), d_wo])
            d_h = _qkv_dh(tag + "_qkv_dh", dqkv, w_qkv, 3, sent)
            d_hs = [d_h[0]]
        else:
            hs, qkv, o, lses, lse_total = mixer
            d_o = _square(tag + "_proj_do", d_a, w_o, NT, F32)
            parts = []
            for g, (_, dil) in enumerate(DIL_GROUPS):
                parts.append(_dil_bwd(g, qkv, slopes * dil, _sort_rows(d_o, dil), _sort_rows(o, dil), lses[g],
                                      _sort_rows(lse_total, dil)))
            dqkv = jnp.concatenate(parts, axis=0)
            sent = emit("dil", [_qkv_dw(tag + "_qkv_dw", hs, dqkv, 9), d_wo])
            d_h = _qkv_dh(tag + "_qkv_dh", dqkv, w_qkv, 9, sent)
            d_hs = [_unsort_rows(d_h[g], dil) for g, (_, dil) in enumerate(DIL_GROUPS)]
        dx, d_norm["mix_pre"][layer] = _rms_bwd(tag + "_norm_mix", x0, _row(mix_pre, layer), d_hs, res=dx1)

    d_gains = [jnp.concatenate(d_norm[k], axis=0) for k in ("mix_pre", "mix_post", "ffn_pre", "ffn_post")]
    return loss, dx, d_gains, d_rpb


RPB_SIZE = N_HEADS * 15 * 31


def _pack_small(gains, rpb):
    top = jnp.concatenate(gains, axis=0).reshape(64, 128)
    bottom = jnp.pad(rpb.reshape(-1), (0, 64 * 128 - RPB_SIZE)).reshape(64, 128)
    return jnp.concatenate([top, bottom], axis=0)


def _unpack_small(p):
    gains = p[:64].reshape(4, 2, D_MODEL)
    rpb = p[64:].reshape(-1)[:RPB_SIZE].reshape(1, N_HEADS, 15, 31)
    return [gains[i] for i in range(4)], rpb


GROUPS = ("na", "ffn0", "dil", "ffn1")


def kernel(x, norm_mix_pre, norm_mix_post, norm_ffn_pre, norm_ffn_post, na_w_qkv, na_w_o, na_rpb, dil_w_qkv, dil_w_o, ffn_w_gate, ffn_w_up, ffn_w_down, loss_target, m_norm_mix_pre, m_norm_mix_post, m_norm_ffn_pre, m_norm_ffn_post, m_na_w_qkv, m_na_w_o, m_na_rpb, m_dil_w_qkv, m_dil_w_o, m_ffn_w_gate, m_ffn_w_up, m_ffn_w_down, v_norm_mix_pre, v_norm_mix_post, v_norm_ffn_pre, v_norm_ffn_post, v_na_w_qkv, v_na_w_o, v_na_rpb, v_dil_w_qkv, v_dil_w_o, v_ffn_w_gate, v_ffn_w_up, v_ffn_w_down):
    na_cols, dil_cols, o_rows = 3 * D_MODEL // N_DEV, 9 * D_MODEL // N_DEV, D_MODEL // N_DEV
    ff_pad = FF_PAD - FF_SHARD

    full = {
        "na": [((D_MODEL, 3 * D_MODEL), _columns(na_cols)), ((N_DEV, o_rows, D_MODEL), _leading)],
        "dil": [((D_MODEL, 9 * D_MODEL), _columns(dil_cols)), ((N_DEV, o_rows, D_MODEL), _leading)],
        "ffn0": [((N_DEV, 2, D_MODEL, FF_PAD), _leading), ((N_DEV, FF_PAD, D_MODEL), _leading)],
        "ffn1": [((N_DEV, 2, D_MODEL, FF_PAD), _leading), ((N_DEV, FF_PAD, D_MODEL), _leading)],
    }
    block = {
        "na": [(D_MODEL, na_cols), (o_rows, D_MODEL)], "dil": [(D_MODEL, dil_cols), (o_rows, D_MODEL)],
        "ffn0": [(2, D_MODEL, FF_PAD), (FF_PAD, D_MODEL)], "ffn1": [(2, D_MODEL, FF_PAD), (FF_PAD, D_MODEL)],
    }

    gu = jnp.pad(jnp.stack([ffn_w_gate, ffn_w_up], axis=1).astype(BF16), ((0, 0), (0, 0), (0, 0), (0, ff_pad)))
    down = jnp.pad(ffn_w_down.astype(BF16), ((0, 0), (0, ff_pad), (0, 0)))
    shards = {"na": [na_w_qkv[0].astype(BF16), na_w_o[0].astype(BF16)],
              "dil": [dil_w_qkv[0].astype(BF16), dil_w_o[0].astype(BF16)],
              "ffn0": [gu[0], down[0]], "ffn1": [gu[1], down[1]]}
    srcs = [s for g in GROUPS for s in shards[g]]
    gather_sets = {g: [(2 * gi + t, _whole, 2 * gi + t, full[g][t][1]) for t in range(2)] for gi, g in enumerate(GROUPS)}
    land_shapes = [jax.ShapeDtypeStruct(full[g][t][0], BF16) for g in GROUPS for t in range(2)]
    lands = _place_own("gather_own", srcs, land_shapes, [s for g in GROUPS for s in gather_sets[g]])
    sems, srcs, lands, token = _send_start("gather_start", srcs, lands, [gather_sets[g] for g in GROUPS])

    def fetch(group, after):
        gi = GROUPS.index(group)
        local_sets = [(t, _whole, t, full[group][t][1]) for t in range(2)]
        qkv, o = _send_wait(f"gather_wait_{group}", sems[gi], srcs[2 * gi:2 * gi + 2], lands[2 * gi:2 * gi + 2],
                            local_sets, after)
        return (qkv, o.reshape(D_MODEL, D_MODEL)) if group in ("na", "dil") else (qkv, o)

    def grad_source(group, t):
        return _columns(block[group][0][1]) if (group in ("na", "dil") and t == 0) else _leading

    in_flight = {}

    def emit(group, grads):
        if group in ("na", "dil"):
            grads = [grads[0], grads[1].reshape(N_DEV, o_rows, D_MODEL)]
        sets = [(t, grad_source(group, t), t, _leading) for t in range(2)]
        shapes = [jax.ShapeDtypeStruct((N_DEV,) + block[group][t], BF16) for t in range(2)]
        landing = _place_own(f"exchange_own_{group}", grads, shapes, sets)
        sems_g, grads, landing, tok = _send_start(f"exchange_start_{group}", grads, landing, [sets])
        in_flight[group] = (sems_g[0], grads, landing, sets)
        return [tok]

    norms = (norm_mix_pre, norm_mix_post, norm_ffn_pre, norm_ffn_post)
    loss, grad_x, d_gains, d_rpb = _local_step(x[0], loss_target[0], norms, na_rpb[0], fetch, emit, [token])
    loss = lax.psum(loss, ("x", "y", "c"))
    small = _all_gather("gather_small", [_pack_small(d_gains, d_rpb)], [jax.ShapeDtypeStruct((N_DEV, 128, 128), F32)],
                        [_leading])[0]

    landed = {}
    for group in ("ffn1", "dil", "ffn0", "na"):
        sems_g, grads, landing, sets = in_flight[group]
        landed[group] = _send_wait(f"exchange_wait_{group}", sems_g, grads, landing, sets, [grad_x])

    def one(rows, tile, ncols):
        n_tiles = rows // tile
        return dict(grid=(n_tiles,), land_specs=[pl.BlockSpec((N_DEV, tile, ncols), lambda i: (0, i, 0))],
                    p_spec=pl.BlockSpec((None, tile, ncols), lambda i: (0, i, 0)))

    def layered(land_block, land_index, p_block, n_tiles):
        first = pl.BlockSpec(land_block, lambda l, r: land_index(jnp.where(l == 0, r, n_tiles - 1)))
        second = pl.BlockSpec(land_block, lambda l, r: land_index(jnp.where(l == 0, 0, r)))
        return dict(grid=(2, n_tiles), land_specs=[first, second], p_spec=pl.BlockSpec(p_block, lambda l, r: (l, r, 0)))

    gu_lands = [landed["ffn0"][0], landed["ffn1"][0]]
    down_lands = [landed["ffn0"][1], landed["ffn1"][1]]
    updates = {
        "na_w_qkv": _adamw("adamw_na_qkv", [landed["na"][0]], na_w_qkv, m_na_w_qkv, v_na_w_qkv, **one(D_MODEL, 256, na_cols)),
        "na_w_o": _adamw("adamw_na_o", [landed["na"][1]], na_w_o, m_na_w_o, v_na_w_o, **one(o_rows, o_rows, D_MODEL)),
        "dil_w_qkv": _adamw("adamw_dil_qkv", [landed["dil"][0]], dil_w_qkv, m_dil_w_qkv, v_dil_w_qkv,
                            **one(D_MODEL, 128, dil_cols)),
        "dil_w_o": _adamw("adamw_dil_o", [landed["dil"][1]], dil_w_o, m_dil_w_o, v_dil_w_o, **one(o_rows, o_rows, D_MODEL)),
        "ffn_w_gate": _adamw("adamw_gate", gu_lands, ffn_w_gate, m_ffn_w_gate, v_ffn_w_gate,
                             **layered((N_DEV, None, 128, FF_PAD), lambda r: (0, 0, r, 0), (None, 128, FF_SHARD), 8)),
        "ffn_w_up": _adamw("adamw_up", gu_lands, ffn_w_up, m_ffn_w_up, v_ffn_w_up,
                           **layered((N_DEV, None, 128, FF_PAD), lambda r: (0, 1, r, 0), (None, 128, FF_SHARD), 8)),
        "ffn_w_down": _adamw("adamw_down", down_lands, ffn_w_down, m_ffn_w_down, v_ffn_w_down,
                             **layered((N_DEV, 176, D_MODEL), lambda r: (0, r, 0), (None, 176, D_MODEL), 2)),
    }
    gains = [norm_mix_pre, norm_mix_post, norm_ffn_pre, norm_ffn_post]
    m_gains = [m_norm_mix_pre, m_norm_mix_post, m_norm_ffn_pre, m_norm_ffn_post]
    v_gains = [v_norm_mix_pre, v_norm_mix_post, v_norm_ffn_pre, v_norm_ffn_post]
    packed = _adamw("adamw_small", [small], _pack_small(gains, na_rpb)[None], _pack_small(m_gains, m_na_rpb)[None],
                    _pack_small(v_gains, v_na_rpb)[None], **one(128, 128, 128))
    small_out = [_unpack_small(p[0]) for p in packed]

    order = ["na_w_qkv", "na_w_o", "na_rpb", "dil_w_qkv", "dil_w_o", "ffn_w_gate", "ffn_w_up", "ffn_w_down"]
    result = [loss, grad_x[None]]
    for kind in range(4):
        gains_k, rpb_k = small_out[kind]
        result += gains_k
        result += [rpb_k if name == "na_rpb" else updates[name][kind] for name in order]
    return tuple(result)
```

```python
import functools

import jax
import jax.numpy as jnp
from jax import lax
from jax.experimental import pallas as pl
from jax.experimental.pallas import tpu as pltpu

F32 = jnp.float32
BF16 = jnp.bfloat16
MESH = pl.DeviceIdType.MESH
ANY = pl.BlockSpec(memory_space=pl.ANY)
HBM = pl.BlockSpec(memory_space=pltpu.HBM)
SEM = pl.BlockSpec(memory_space=pltpu.SEMAPHORE)
EFFECT = pltpu.SideEffectType.DATAFLOW_SIDE_EFFECTING

N_DEV = 8
SEQ = 2048
D_MODEL = 1024
N_HEADS = 16
HEAD_DIM = 64
GRID_W = 64
NA_ROWS = 8
SEQ_ROWS = SEQ // GRID_W
DIL_GROUPS = ((128, 1), (512, 4), (2048, 16))
BAND = 128
RADIUS = 64
FF_SHARD = 352
FF_PAD = 384
RMS_EPS = 1e-6
NEG_INF = -1e30
Q_SCALE = HEAD_DIM ** -0.5

ADAM_LR = 0.001
ADAM_B1 = 0.9
ADAM_B2 = 0.999
ADAM_EPS = 1e-08
ADAM_WD = 0.01
ADAM_STEP = 10

VMEM_LIMIT = 56 * 1024 * 1024
TM = 512

NN = (((1,), (0,)), ((), ()))
NT = (((1,), (1,)), ((), ()))
TN = (((0,), (0,)), ((), ()))


def _params():
    return pltpu.CompilerParams(vmem_limit_bytes=VMEM_LIMIT)


def _matmul(name, a, b, *, grid, a_spec, b_spec, o_spec, out_shape, dims, acc_shape, deps=()):
    nk = grid[-1]
    kaxis = len(grid) - 1

    def body(a_ref, b_ref, *rest):
        o_ref, acc_ref = rest[-2], rest[-1]
        part = lax.dot_general(a_ref[...].astype(BF16), b_ref[...].astype(BF16), dims, preferred_element_type=F32)
        if nk == 1:
            o_ref[...] = part.astype(o_ref.dtype)
        else:
            k = pl.program_id(kaxis)

            @pl.when(k == 0)
            def _():
                acc_ref[...] = part

            @pl.when(k > 0)
            def _():
                acc_ref[...] += part

            @pl.when(k == nk - 1)
            def _():
                o_ref[...] = acc_ref[...].astype(o_ref.dtype)

    return pl.pallas_call(
        body, out_shape=out_shape, grid=grid, in_specs=[a_spec, b_spec] + [ANY] * len(deps), out_specs=o_spec,
        scratch_shapes=[pltpu.VMEM(acc_shape, F32)], name=name, compiler_params=_params())(a, b, *deps)


def _rms_fwd(name, x, g, res=None, out_dtype=F32, deps=()):
    n_tiles = SEQ // TM
    has_res = res is not None

    def body(*refs):
        x_ref, g_ref = refs[0], refs[1]
        o_ref = refs[-1]
        xv = x_ref[...]
        r = lax.rsqrt(jnp.mean(xv * xv, axis=-1, keepdims=True) + RMS_EPS)
        y = xv * r * g_ref[...]
        if has_res:
            y = refs[2][...] + y
        o_ref[...] = y.astype(o_ref.dtype)

    tile = pl.BlockSpec((TM, D_MODEL), lambda i: (i, 0))
    gspec = pl.BlockSpec((1, D_MODEL), lambda i: (0, 0))
    ins = [x, g] + ([res] if has_res else []) + list(deps)
    specs = [tile, gspec] + ([tile] if has_res else []) + [ANY] * len(deps)
    return pl.pallas_call(
        body, out_shape=jax.ShapeDtypeStruct((SEQ, D_MODEL), out_dtype), grid=(n_tiles,), in_specs=specs,
        out_specs=tile, name=name, compiler_params=_params())(*ins)


def _rms_bwd(name, x, g, dys, res=None, out_dtype=F32):
    n_tiles = SEQ // TM
    n_dy = len(dys)
    has_res = res is not None

    def body(*refs):
        x_ref, g_ref = refs[0], refs[1]
        dy_refs = refs[2:2 + n_dy]
        res_ref = refs[2 + n_dy] if has_res else None
        dx_ref, dg_ref, acc_ref = refs[-3], refs[-2], refs[-1]
        i = pl.program_id(0)
        xv = x_ref[...]
        r = lax.rsqrt(jnp.mean(xv * xv, axis=-1, keepdims=True) + RMS_EPS)
        xn = xv * r
        dy = dy_refs[0][...].astype(F32)
        for extra in dy_refs[1:]:
            dy = dy + extra[...].astype(F32)
        dyg = dy * g_ref[...]
        dx = r * (dyg - xn * jnp.mean(dyg * xn, axis=-1, keepdims=True))
        if has_res:
            dx = res_ref[...] + dx
        dx_ref[...] = dx.astype(dx_ref.dtype)
        part = jnp.sum((dy * xn).reshape(TM // 8, 8, D_MODEL), axis=0)

        @pl.when(i == 0)
        def _():
            acc_ref[...] = part

        @pl.when(i > 0)
        def _():
            acc_ref[...] += part

        @pl.when(i == n_tiles - 1)
        def _():
            dg_ref[...] = jnp.broadcast_to(jnp.sum(acc_ref[...], axis=0, keepdims=True), (8, D_MODEL))

    tile = pl.BlockSpec((TM, D_MODEL), lambda i: (i, 0))
    gspec = pl.BlockSpec((1, D_MODEL), lambda i: (0, 0))
    ins = [x, g] + list(dys) + ([res] if has_res else [])
    specs = [tile, gspec] + [tile] * n_dy + ([tile] if has_res else [])
    dx, dg = pl.pallas_call(
        body, out_shape=(jax.ShapeDtypeStruct((SEQ, D_MODEL), out_dtype), jax.ShapeDtypeStruct((8, D_MODEL), F32)),
        grid=(n_tiles,), in_specs=specs,
        out_specs=(tile, pl.BlockSpec((8, D_MODEL), lambda i: (0, 0))),
        scratch_shapes=[pltpu.VMEM((8, D_MODEL), F32)], name=name, compiler_params=_params())(*ins)
    return dx, dg[0:1]


def _loss_head(name, y, target):
    n_tiles = SEQ // TM

    def body(y_ref, t_ref, dy_ref, loss_ref, acc_ref):
        i = pl.program_id(0)
        diff = y_ref[...] - t_ref[...]
        dy_ref[...] = diff * (1.0 / D_MODEL)
        part = jnp.sum((diff * diff).reshape(TM // 8, 8, D_MODEL), axis=0)

        @pl.when(i == 0)
        def _():
            acc_ref[...] = part

        @pl.when(i > 0)
        def _():
            acc_ref[...] += part

        @pl.when(i == n_tiles - 1)
        def _():
            loss_ref[...] = jnp.full((8, 128), jnp.sum(acc_ref[...]) * (0.5 / D_MODEL), F32)

    tile = pl.BlockSpec((TM, D_MODEL), lambda i: (i, 0))
    dy, loss = pl.pallas_call(
        body, out_shape=(jax.ShapeDtypeStruct((SEQ, D_MODEL), F32), jax.ShapeDtypeStruct((8, 128), F32)),
        grid=(n_tiles,), in_specs=[tile, tile], out_specs=(tile, pl.BlockSpec((8, 128), lambda i: (0, 0))),
        scratch_shapes=[pltpu.VMEM((8, D_MODEL), F32)], name=name, compiler_params=_params())(y, target)
    return dy, loss[0, 0]


def _row_index(shape):
    return lax.broadcasted_iota(jnp.int32, shape, 0)


def _lane_index(shape):
    return lax.broadcasted_iota(jnp.int32, shape, len(shape) - 1)


def _skew_rows(t, direction):
    q = _row_index(t.shape) & (GRID_W - 1)
    for bit in range(6):
        step = 1 << bit
        shift = step if direction > 0 else 128 - step
        t = jnp.where((q & step) != 0, pltpu.roll(t, shift, 1), t)
    return t


def _rpb_table(rpb_pad):
    rows = 16 * GRID_W

    def body(r_ref, t_ref):
        lane = _lane_index((rows, 128))
        v = pltpu.roll(r_ref[...], 128 - 15, 1)
        t = _skew_rows(jnp.broadcast_to(v[:, None, :], (16, GRID_W, 128)).reshape(rows, 128), +1)
        t = jnp.where(lane < GRID_W, t, 0.0)
        below = jnp.concatenate([t[GRID_W:], jnp.zeros((GRID_W, 128), F32)], axis=0)
        t_ref[...] = (t + pltpu.roll(below, GRID_W, 1)).reshape(16, GRID_W, 128)

    return pl.pallas_call(
        body, out_shape=jax.ShapeDtypeStruct((N_HEADS, 16, GRID_W, 128), F32), grid=(N_HEADS,),
        in_specs=[pl.BlockSpec((None, 16, 128), lambda h: (h, 0, 0))],
        out_specs=pl.BlockSpec((None, 16, GRID_W, 128), lambda h: (h, 0, 0, 0)),
        name="rpb_table", compiler_params=_params())(rpb_pad)


def _rpb_grad(gp):
    rows = 16 * GRID_W

    def body(g_ref, o_ref):
        lane = _lane_index((rows, 128))
        g = g_ref[...].reshape(rows, 128)
        low = jnp.where(lane < GRID_W, g, 0.0)
        high = pltpu.roll(jnp.where(lane >= GRID_W, g, 0.0), GRID_W, 1)
        above = jnp.concatenate([jnp.zeros((GRID_W, 128), F32), high[:rows - GRID_W]], axis=0)
        diag = jnp.sum(_skew_rows(low + above, -1).reshape(16, GRID_W, 128), axis=1)
        o_ref[...] = pltpu.roll(diag, 15, 1)

    return pl.pallas_call(
        body, out_shape=jax.ShapeDtypeStruct((N_HEADS, 16, 128), F32), grid=(N_HEADS,),
        in_specs=[pl.BlockSpec((None, 16, GRID_W, 128), lambda h: (h, 0, 0, 0))],
        out_specs=pl.BlockSpec((None, 16, 128), lambda h: (h, 0, 0)),
        name="rpb_grad", compiler_params=_params())(gp)


NA_KEYS = NA_ROWS * GRID_W


def _na_window(i):
    first_row = jnp.clip(i - NA_ROWS // 2, 0, SEQ_ROWS - NA_ROWS)
    return pl.multiple_of(first_row * GRID_W, GRID_W), first_row - i + NA_ROWS - 1


NA_STEP = 4


def _na_valid():
    q = _row_index((2 * GRID_W, NA_KEYS)) & (GRID_W - 1)
    k = _lane_index((2 * GRID_W, NA_KEYS)) & (GRID_W - 1)
    first_col = jnp.clip(q - 8, 0, GRID_W - 16)
    return (k >= first_col) & (k < first_col + 16)


def _head_masks():
    lane = _lane_index((1, 128))
    return (lane < HEAD_DIM, lane >= HEAD_DIM)


def _stack_heads(t, masks):
    zero = jnp.zeros_like(t)
    return jnp.concatenate([jnp.where(masks[0], t, zero), jnp.where(masks[1], t, zero)], axis=0)


def _unstack_heads(t, masks):
    n = t.shape[0] // 2
    return jnp.where(masks[0], t[:n], t[n:])


def _stack_columns(t):
    return jnp.concatenate([t[:, 0:1], t[:, HEAD_DIM:HEAD_DIM + 1]], axis=0)


def _na_scores(qs, kw, tp_ref, dr0, valid):
    s = lax.dot_general(qs, kw, NT, preferred_element_type=F32)
    bias = jnp.concatenate(
        [jnp.concatenate([tp_ref[a, pl.ds(dr0 + 2 * c, 1), :, :].reshape(GRID_W, 128) for c in range(4)], axis=1)
         for a in range(2)], axis=0)
    return jnp.where(valid, s + bias, NEG_INF)


def _na_specs():
    q_spec = pl.BlockSpec((NA_STEP * GRID_W, 128), lambda hp, i: (i, hp))
    k_spec = pl.BlockSpec((SEQ, 128), lambda hp, i: (0, 8 + hp))
    v_spec = pl.BlockSpec((SEQ, 128), lambda hp, i: (0, 16 + hp))
    tp_spec = pl.BlockSpec((2, 16, GRID_W, 128), lambda hp, i: (hp, 0, 0, 0))
    return q_spec, k_spec, v_spec, tp_spec


def _na_fwd(qkv, table):
    def body(q_ref, k_ref, v_ref, tp_ref, o_ref, lse_ref):
        valid = _na_valid()
        masks = _head_masks()
        for r in range(NA_STEP):
            rows = slice(r * GRID_W, (r + 1) * GRID_W)
            start, dr0 = _na_window(pl.program_id(1) * NA_STEP + r)
            kw = k_ref[pl.ds(start, NA_KEYS), :]
            vw = v_ref[pl.ds(start, NA_KEYS), :]
            s = _na_scores(_stack_heads(q_ref[rows, :] * Q_SCALE, masks), kw, tp_ref, dr0, valid)
            m = jnp.max(s, axis=-1, keepdims=True)
            p = jnp.exp(s - m)
            denom = jnp.sum(p, axis=-1, keepdims=True)
            out = jnp.dot(p.astype(BF16), vw, preferred_element_type=F32) / denom
            o_ref[rows, :] = _unstack_heads(out, masks).astype(o_ref.dtype)
            lse_ref[rows, :] = _unstack_heads(jnp.broadcast_to(m + jnp.log(denom), (2 * GRID_W, 128)), masks)

    q_spec, k_spec, v_spec, tp_spec = _na_specs()
    return pl.pallas_call(
        body, out_shape=(jax.ShapeDtypeStruct((SEQ, D_MODEL), BF16), jax.ShapeDtypeStruct((SEQ, D_MODEL), F32)),
        grid=(N_HEADS // 2, SEQ_ROWS // NA_STEP), in_specs=[q_spec, k_spec, v_spec, tp_spec],
        out_specs=(q_spec, q_spec), name="na_fwd", compiler_params=_params())(qkv, qkv, qkv, table)


def _na_bwd(qkv, table, d_out, lse):
    def body(q_ref, k_ref, v_ref, tp_ref, do_ref, lse_ref, dqkv_ref, gp_ref, dk_acc, dv_acc):
        step = pl.program_id(1)

        @pl.when(step == 0)
        def _():
            dk_acc[...] = jnp.zeros_like(dk_acc)
            dv_acc[...] = jnp.zeros_like(dv_acc)
            gp_ref[...] = jnp.zeros_like(gp_ref)

        valid = _na_valid()
        masks = _head_masks()
        for r in range(NA_STEP):
            rows = slice(r * GRID_W, (r + 1) * GRID_W)
            i = step * NA_STEP + r
            start, dr0 = _na_window(i)
            kw = k_ref[pl.ds(start, NA_KEYS), :]
            vw = v_ref[pl.ds(start, NA_KEYS), :]
            qs = _stack_heads(q_ref[rows, :] * Q_SCALE, masks)
            dos = _stack_heads(do_ref[rows, :], masks)
            p = jnp.exp(_na_scores(qs, kw, tp_ref, dr0, valid) - _stack_columns(lse_ref[rows, :]))
            dp = lax.dot_general(dos, vw, NT, preferred_element_type=F32)
            ds = p * (dp - jnp.sum(p * dp, axis=-1, keepdims=True))
            for a in range(2):
                for c in range(4):
                    gp_ref[a, pl.ds(dr0 + 2 * c, 1), :, :] += (
                        ds[a * GRID_W:(a + 1) * GRID_W, 128 * c:128 * (c + 1)].reshape(1, GRID_W, 128))
            dsb = ds.astype(BF16)
            dq = _unstack_heads(jnp.dot(dsb, kw, preferred_element_type=F32), masks) * Q_SCALE
            dqkv_ref[0, pl.ds(pl.multiple_of(i * GRID_W, GRID_W), GRID_W), :] = dq.astype(dqkv_ref.dtype)
            dk_acc[pl.ds(start, NA_KEYS), :] += lax.dot_general(dsb, qs, TN, preferred_element_type=F32)
            dv_acc[pl.ds(start, NA_KEYS), :] += lax.dot_general(p.astype(BF16), dos, TN, preferred_element_type=F32)

        @pl.when(step == SEQ_ROWS // NA_STEP - 1)
        def _():
            dqkv_ref[1] = dk_acc[...].astype(dqkv_ref.dtype)
            dqkv_ref[2] = dv_acc[...].astype(dqkv_ref.dtype)

    q_spec, k_spec, v_spec, tp_spec = _na_specs()
    return pl.pallas_call(
        body,
        out_shape=(jax.ShapeDtypeStruct((3, SEQ, D_MODEL), BF16), jax.ShapeDtypeStruct((N_HEADS, 16, GRID_W, 128), F32)),
        grid=(N_HEADS // 2, SEQ_ROWS // NA_STEP), in_specs=[q_spec, k_spec, v_spec, tp_spec, q_spec, q_spec],
        out_specs=(pl.BlockSpec((3, SEQ, 128), lambda hp, i: (0, 0, hp)), tp_spec),
        scratch_shapes=[pltpu.VMEM((SEQ, 128), F32), pltpu.VMEM((SEQ, 128), F32)],
        name="na_bwd", compiler_params=_params())(qkv, qkv, qkv, table, d_out, lse)


def _dil_geometry(group):
    dil = DIL_GROUPS[group][1]
    sub_len = SEQ // dil
    per_step = min(2, sub_len // BAND)
    return dil, sub_len, sub_len // BAND // per_step, per_step, min(2 * BAND, sub_len)


def _dil_window(b, sub_len, n_keys):
    if n_keys == sub_len:
        return 0
    return pl.multiple_of(jnp.clip(b * BAND - RADIUS, 0, sub_len - n_keys), RADIUS)


def _dil_bias(b, start, n_keys, slope_ref, hp):
    row = _row_index((2 * BAND, n_keys))
    qpos = b * BAND + (row & (BAND - 1))
    kpos = start + _lane_index((2 * BAND, n_keys))
    dist = jnp.abs(qpos - kpos)
    slope = jnp.where(row < BAND, slope_ref[2 * hp], slope_ref[2 * hp + 1])
    return slope * dist.astype(F32), dist <= RADIUS


def _dil_scores(qs, kw, penalty, valid):
    return jnp.where(valid, lax.dot_general(qs, kw, NT, preferred_element_type=F32) - penalty, NEG_INF)


def _dil_specs(group):
    dil, sub_len, steps, per_step, _ = _dil_geometry(group)
    col = group * 24
    rows = per_step * BAND
    q_spec = pl.BlockSpec((rows, 128), lambda n, hp, b: (n * steps + b, col + hp))
    k_spec = pl.BlockSpec((sub_len, 128), lambda n, hp, b: (n, col + 8 + hp))
    v_spec = pl.BlockSpec((sub_len, 128), lambda n, hp, b: (n, col + 16 + hp))
    tile = pl.BlockSpec((rows, 128), lambda n, hp, b: (n * steps + b, hp))
    smem = pl.BlockSpec(memory_space=pltpu.SMEM)
    return (dil, N_HEADS // 2, steps), q_spec, k_spec, v_spec, tile, smem


def _dil_fwd(group, qkv, slopes):
    _, sub_len, _, per_step, n_keys = _dil_geometry(group)

    def body(q_ref, k_ref, v_ref, slope_ref, o_ref, lse_ref):
        hp = pl.program_id(1)
        masks = _head_masks()
        for r in range(per_step):
            rows = slice(r * BAND, (r + 1) * BAND)
            b = pl.program_id(2) * per_step + r
            start = _dil_window(b, sub_len, n_keys)
            kw = k_ref[pl.ds(start, n_keys), :]
            vw = v_ref[pl.ds(start, n_keys), :]
            penalty, valid = _dil_bias(b, start, n_keys, slope_ref, hp)
            s = _dil_scores(_stack_heads(q_ref[rows, :] * Q_SCALE, masks), kw, penalty, valid)
            m = jnp.max(s, axis=-1, keepdims=True)
            p = jnp.exp(s - m)
            denom = jnp.sum(p, axis=-1, keepdims=True)
            out = jnp.dot(p.astype(BF16), vw, preferred_element_type=F32) / denom
            o_ref[rows, :] = _unstack_heads(out, masks)
            lse_ref[rows, :] = _unstack_heads(jnp.broadcast_to(m + jnp.log(denom), (2 * BAND, 128)), masks)

    grid, q_spec, k_spec, v_spec, tile, smem = _dil_specs(group)
    return pl.pallas_call(
        body, out_shape=(jax.ShapeDtypeStruct((SEQ, D_MODEL), F32), jax.ShapeDtypeStruct((SEQ, D_MODEL), F32)),
        grid=grid, in_specs=[q_spec, k_spec, v_spec, smem], out_specs=(tile, tile),
        name=f"dil_fwd_{group}", compiler_params=_params())(qkv, qkv, qkv, slopes)


def _dil_merge(outs, lses):
    def body(o0, o1, o2, l0, l1, l2, o_ref, lse_ref):
        ls = [l0[...], l1[...], l2[...]]
        m = jnp.maximum(jnp.maximum(ls[0], ls[1]), ls[2])
        es = [jnp.exp(v - m) for v in ls]
        total = es[0] + es[1] + es[2]
        o_ref[...] = (es[0] * o0[...] + es[1] * o1[...] + es[2] * o2[...]) / total
        lse_ref[...] = m + jnp.log(total)

    tile = pl.BlockSpec((TM, D_MODEL), lambda i: (i, 0))
    return pl.pallas_call(
        body, out_shape=(jax.ShapeDtypeStruct((SEQ, D_MODEL), F32), jax.ShapeDtypeStruct((SEQ, D_MODEL), F32)),
        grid=(SEQ // TM,), in_specs=[tile] * 6, out_specs=(tile, tile), name="dil_merge",
        compiler_params=_params())(*outs, *lses)


def _dil_bwd(group, qkv, slopes, d_out, out, lse_group, lse_total):
    _, sub_len, steps, per_step, n_keys = _dil_geometry(group)

    def body(q_ref, k_ref, v_ref, slope_ref, do_ref, o_ref, lg_ref, lt_ref, dqkv_ref, dk_acc, dv_acc):
        hp, step = pl.program_id(1), pl.program_id(2)

        @pl.when(step == 0)
        def _():
            dk_acc[...] = jnp.zeros_like(dk_acc)
            dv_acc[...] = jnp.zeros_like(dv_acc)

        masks = _head_masks()
        for r in range(per_step):
            rows = slice(r * BAND, (r + 1) * BAND)
            b = step * per_step + r
            start = _dil_window(b, sub_len, n_keys)
            kw = k_ref[pl.ds(start, n_keys), :]
            vw = v_ref[pl.ds(start, n_keys), :]
            penalty, valid = _dil_bias(b, start, n_keys, slope_ref, hp)
            qs = _stack_heads(q_ref[rows, :] * Q_SCALE, masks)
            lse2 = lg_ref[rows, :]
            weight = jnp.exp(lse2 - lt_ref[rows, :])
            do2 = do_ref[rows, :]
            dogs = _stack_heads((weight * do2).astype(BF16), masks)
            delta = _stack_columns(weight) * jnp.sum(_stack_heads(do2 * o_ref[rows, :], masks), axis=-1, keepdims=True)
            p = jnp.exp(_dil_scores(qs, kw, penalty, valid) - _stack_columns(lse2))
            dp = lax.dot_general(dogs, vw, NT, preferred_element_type=F32)
            dsb = (p * (dp - delta)).astype(BF16)
            dq = _unstack_heads(jnp.dot(dsb, kw, preferred_element_type=F32), masks) * Q_SCALE
            dqkv_ref[0, pl.ds(pl.multiple_of(b * BAND, BAND), BAND), :] = dq.astype(dqkv_ref.dtype)
            dk_acc[pl.ds(start, n_keys), :] += lax.dot_general(dsb, qs, TN, preferred_element_type=F32)
            dv_acc[pl.ds(start, n_keys), :] += lax.dot_general(p.astype(BF16), dogs, TN, preferred_element_type=F32)

        @pl.when(step == steps - 1)
        def _():
            dqkv_ref[1] = dk_acc[...].astype(dqkv_ref.dtype)
            dqkv_ref[2] = dv_acc[...].astype(dqkv_ref.dtype)

    grid, q_spec, k_spec, v_spec, tile, smem = _dil_specs(group)
    return pl.pallas_call(
        body, out_shape=jax.ShapeDtypeStruct((3, SEQ, D_MODEL), BF16), grid=grid,
        in_specs=[q_spec, k_spec, v_spec, smem, tile, tile, tile, tile],
        out_specs=pl.BlockSpec((3, sub_len, 128), lambda n, hp, b: (0, n, hp)),
        scratch_shapes=[pltpu.VMEM((sub_len, 128), F32), pltpu.VMEM((sub_len, 128), F32)],
        name=f"dil_bwd_{group}", compiler_params=_params())(qkv, qkv, qkv, slopes, d_out, out, lse_group, lse_total)


def _sort_rows(t, dil):
    if dil == 1:
        return t
    return t.reshape(SEQ // dil, dil, t.shape[-1]).transpose(1, 0, 2).reshape(SEQ, t.shape[-1])


def _unsort_rows(t, dil):
    if dil == 1:
        return t
    return t.reshape(dil, SEQ // dil, t.shape[-1]).transpose(1, 0, 2).reshape(SEQ, t.shape[-1])


def _ffn_up(name, h, w_gu):
    def body(h_ref, wg_ref, wu_ref, act_ref, hg_ref, hu_ref):
        hv = h_ref[...]
        hg = jnp.dot(hv, wg_ref[...], preferred_element_type=F32)
        hu = jnp.dot(hv, wu_ref[...], preferred_element_type=F32)
        act_ref[...] = (hg * jax.nn.sigmoid(hg) * hu).astype(act_ref.dtype)
        hg_ref[...] = hg.astype(hg_ref.dtype)
        hu_ref[...] = hu.astype(hu_ref.dtype)

    out = pl.BlockSpec((None, TM, FF_PAD), lambda d, i: (d, i, 0))
    shape = jax.ShapeDtypeStruct((N_DEV, SEQ, FF_PAD), BF16)
    return pl.pallas_call(
        body, out_shape=(shape, shape, shape), grid=(N_DEV, SEQ // TM),
        in_specs=[pl.BlockSpec((TM, D_MODEL), lambda d, i: (i, 0)),
                  pl.BlockSpec((None, None, D_MODEL, FF_PAD), lambda d, i: (d, 0, 0, 0)),
                  pl.BlockSpec((None, None, D_MODEL, FF_PAD), lambda d, i: (d, 1, 0, 0))],
        out_specs=(out, out, out), name=name, compiler_params=_params())(h, w_gu, w_gu)


def _ffn_bwd_act(name, d_f, w_down, hg, hu):
    def body(df_ref, wd_ref, hg_ref, hu_ref, dgu_ref):
        dact = lax.dot_general(df_ref[...], wd_ref[...], NT, preferred_element_type=F32)
        hgv = hg_ref[...].astype(F32)
        sig = jax.nn.sigmoid(hgv)
        dgu_ref[0] = (dact * hu_ref[...].astype(F32) * (sig * (1.0 + hgv * (1.0 - sig)))).astype(dgu_ref.dtype)
        dgu_ref[1] = (dact * hgv * sig).astype(dgu_ref.dtype)

    tile = pl.BlockSpec((None, TM, FF_PAD), lambda d, i: (d, i, 0))
    return pl.pallas_call(
        body, out_shape=jax.ShapeDtypeStruct((2, N_DEV, SEQ, FF_PAD), BF16), grid=(N_DEV, SEQ // TM),
        in_specs=[pl.BlockSpec((TM, D_MODEL), lambda d, i: (i, 0)),
                  pl.BlockSpec((None, FF_PAD, D_MODEL), lambda d, i: (d, 0, 0)), tile, tile],
        out_specs=pl.BlockSpec((2, None, TM, FF_PAD), lambda d, i: (0, d, i, 0)),
        name=name, compiler_params=_params())(d_f, w_down, hg, hu)


def _position():
    return lax.axis_index("x"), lax.axis_index("y"), lax.axis_index("c")


def _flat(p):
    return 4 * p[0] + 2 * p[1] + p[2]


def _peer(me, k):
    x, y, c = me
    return (1 - x if k & 4 else x, 1 - y if k & 2 else y, 1 - c if k & 1 else c)


def _columns(width):
    return lambda ref, d: ref.at[:, pl.ds(pl.multiple_of(d * width, 128), width)]


def _leading(ref, d):
    return ref.at[d]


def _whole(ref, d):
    return ref


def _by_sender(window):
    return lambda ref, sender, k: window(ref, sender)


def _by_distance(ref, sender, k):
    return ref.at[k - 1]


def _place_own(name, me, srcs, land_shapes, in_specs, out_specs, grid):
    n = len(srcs)

    def body(me_ref, *refs):
        for src_ref, dst_ref in zip(refs[:n], refs[n:]):
            dst_ref[...] = src_ref[...]

    return pl.pallas_call(
        body, out_shape=tuple(land_shapes),
        grid_spec=pltpu.PrefetchScalarGridSpec(num_scalar_prefetch=1, grid=grid, in_specs=in_specs, out_specs=tuple(out_specs)),
        name=name, compiler_params=_params())(me, *srcs)


def _remote_copies(sets, src_refs, land_refs, send_sems, recv_sems, outgoing):
    me = _position()
    copies = []
    for t, (si, src_of, li, dst_of) in enumerate(sets):
        for k in range(1, N_DEV):
            other = _peer(me, k)
            sender = me if outgoing else other
            copies.append(pltpu.make_async_remote_copy(
                src_ref=src_of(src_refs[si], _flat(other)), dst_ref=dst_of(land_refs[li], _flat(sender), k),
                send_sem=send_sems.at[(N_DEV - 1) * t + k - 1], recv_sem=recv_sems.at[(N_DEV - 1) * t + k - 1],
                device_id=other, device_id_type=MESH))
    return copies


def _send_start(name, srcs, lands, sets_by_group):
    n_src, n_land, n_groups = len(srcs), len(lands), len(sets_by_group)

    def body(*refs):
        src_refs, land_refs = refs[:n_src], refs[n_src:n_src + n_land]
        outs = refs[n_src + n_land:]
        for g, sets in enumerate(sets_by_group):
            for cp in _remote_copies(sets, src_refs, land_refs, outs[2 * g], outs[2 * g + 1], True):
                cp.start()
        outs[-1][...] = jnp.zeros_like(outs[-1])

    sem_shapes = []
    for sets in sets_by_group:
        sem_shapes += [pltpu.SemaphoreType.DMA((len(sets) * (N_DEV - 1),))] * 2
    thru = [pltpu.HBM(a.shape, a.dtype) for a in list(srcs) + list(lands)]
    n_sem = len(sem_shapes)
    result = pl.pallas_call(
        body, out_shape=tuple(sem_shapes + thru + [jax.ShapeDtypeStruct((8, 128), F32)]),
        in_specs=[HBM] * (n_src + n_land),
        out_specs=tuple([SEM] * n_sem + [HBM] * (n_src + n_land) + [pl.BlockSpec(memory_space=pltpu.VMEM)]),
        input_output_aliases={i: n_sem + i for i in range(n_src + n_land)},
        compiler_params=pltpu.CompilerParams(has_side_effects=EFFECT), name=name,
    )(*[pltpu.with_memory_space_constraint(a, pltpu.HBM) for a in list(srcs) + list(lands)])
    sems = [(result[2 * g], result[2 * g + 1]) for g in range(n_groups)]
    return sems, list(result[n_sem:n_sem + n_src]), list(result[n_sem + n_src:n_sem + n_src + n_land]), result[-1]


def _send_wait(name, sems, srcs, lands, sets, after):
    n_src, n_land = len(srcs), len(lands)

    def body(*refs):
        src_refs, land_refs = refs[:n_src], refs[n_src:n_src + n_land]
        send_sems, recv_sems = refs[n_src + n_land], refs[n_src + n_land + 1]
        for cp in _remote_copies(sets, src_refs, land_refs, send_sems, recv_sems, True):
            cp.wait_send()
        for cp in _remote_copies(sets, src_refs, land_refs, send_sems, recv_sems, False):
            cp.wait_recv()

    thru = [pltpu.HBM(a.shape, a.dtype) for a in list(srcs) + list(lands)]
    result = pl.pallas_call(
        body, out_shape=tuple(thru), in_specs=[HBM] * (n_src + n_land) + [SEM, SEM] + [ANY] * len(after),
        out_specs=tuple([HBM] * (n_src + n_land)), input_output_aliases={i: i for i in range(n_src + n_land)},
        compiler_params=pltpu.CompilerParams(has_side_effects=EFFECT), name=name,
    )(*srcs, *lands, sems[0], sems[1], *after)
    return list(result[:n_src]), list(result[n_src:])


def _all_gather(name, locals_, out_shapes, windows):
    n = len(locals_)

    def body(*refs):
        src_refs, out_refs = refs[:n], refs[n:2 * n]
        send_sems, recv_sems, local_sems = refs[2 * n:]
        x, y, c = _position()
        me, sibling = (x, y, c), (x, y, 1 - c)
        chips = [(1 - x, y), (x, 1 - y), (1 - x, 1 - y)]

        def copy(t, k, block, to, from_local=False):
            dst = windows[t](out_refs[t], _flat(block))
            return pltpu.make_async_remote_copy(
                src_ref=src_refs[t] if from_local else dst, dst_ref=dst, send_sem=send_sems.at[t, k],
                recv_sem=recv_sems.at[t, k], device_id=to, device_id_type=MESH)

        mine = [pltpu.make_async_copy(src_refs[t], windows[t](out_refs[t], _flat(me)), local_sems.at[t]) for t in range(n)]
        sends = []
        for t in range(n):
            mine[t].start()
            sends.append(copy(t, 0, me, sibling, True))
            sends += [copy(t, 1 + j, me, (*chip, c), True) for j, chip in enumerate(chips)]
        for cp in sends:
            cp.start()
        for t in range(n):
            for j, chip in enumerate(chips):
                copy(t, 1 + j, (*chip, c), me).wait_recv()
                passed = copy(t, 4 + j, (*chip, c), sibling)
                passed.start()
                sends.append(passed)
        for t in range(n):
            copy(t, 0, sibling, me).wait_recv()
            for j, chip in enumerate(chips):
                copy(t, 4 + j, (*chip, 1 - c), me).wait_recv()
        for cp in sends:
            cp.wait_send()
        for cp in mine:
            cp.wait()

    return pl.pallas_call(
        body, out_shape=tuple(out_shapes), in_specs=[ANY] * n, out_specs=tuple([ANY] * n),
        scratch_shapes=[pltpu.SemaphoreType.DMA((n, 7)), pltpu.SemaphoreType.DMA((n, 7)), pltpu.SemaphoreType.DMA((n,))],
        name=name)(*locals_)


def _adamw(name, me, lands, owns, w, m, v, *, grid, land_specs, own_specs, p_spec):
    n_land = len(lands)

    def body(me_ref, *refs):
        land_refs, own_refs = refs[:n_land], refs[n_land:n_land + len(owns)]
        w_ref, m_ref, v_ref, g_ref, delta_ref, m_out, v_out = refs[n_land + len(owns):]
        ncols = w_ref.shape[-1]
        sums = []
        for i, land_ref in enumerate(land_refs):
            g = own_refs[i][...].astype(F32) if owns else land_ref[0].astype(F32)
            for s in range(0 if owns else 1, land_ref.shape[0]):
                g = g + land_ref[s].astype(F32)
            sums.append(g[:, :ncols])
        g = sums[0] if n_land == 1 else jnp.where(pl.program_id(0) == 0, sums[0], sums[1])
        m_new = ADAM_B1 * m_ref[...] + (1.0 - ADAM_B1) * g
        v_new = ADAM_B2 * v_ref[...] + (1.0 - ADAM_B2) * jnp.square(g)
        m_hat = m_new / (1.0 - ADAM_B1 ** ADAM_STEP)
        v_hat = v_new / (1.0 - ADAM_B2 ** ADAM_STEP)
        g_ref[...] = g
        delta_ref[...] = -ADAM_LR * (m_hat / (jnp.sqrt(v_hat) + ADAM_EPS) + ADAM_WD * w_ref[...])
        m_out[...] = m_new
        v_out[...] = v_new

    shape = jax.ShapeDtypeStruct(w.shape, F32)
    return pl.pallas_call(
        body, out_shape=(shape,) * 4,
        grid_spec=pltpu.PrefetchScalarGridSpec(
            num_scalar_prefetch=1, grid=grid, in_specs=list(land_specs) + list(own_specs) + [p_spec, p_spec, p_spec],
            out_specs=(p_spec,) * 4),
        name=name, compiler_params=_params())(me, *lands, *owns, w, m, v)


def _row(p, layer):
    return p[layer][None, :]


def _square(name, a, b, dims, out_dtype, deps=()):
    if dims == TN:
        return _matmul(name, a, b, grid=(2, SEQ // TM), a_spec=pl.BlockSpec((TM, 512), lambda i, k: (k, i)),
                       b_spec=pl.BlockSpec((TM, D_MODEL), lambda i, k: (k, 0)),
                       o_spec=pl.BlockSpec((512, D_MODEL), lambda i, k: (i, 0)),
                       out_shape=jax.ShapeDtypeStruct((D_MODEL, D_MODEL), out_dtype), dims=TN, acc_shape=(512, D_MODEL),
                       deps=deps)
    return _matmul(name, a, b, grid=(SEQ // TM, 1), a_spec=pl.BlockSpec((TM, D_MODEL), lambda i, k: (i, 0)),
                   b_spec=pl.BlockSpec((D_MODEL, D_MODEL), lambda i, k: (0, 0)),
                   o_spec=pl.BlockSpec((TM, D_MODEL), lambda i, k: (i, 0)),
                   out_shape=jax.ShapeDtypeStruct((SEQ, D_MODEL), out_dtype), dims=dims, acc_shape=(8, 128), deps=deps)


def _qkv_fwd(name, hs, w, n_chunks):
    return _matmul(name, hs, w, grid=(n_chunks, SEQ // TM, 1),
                   a_spec=pl.BlockSpec((None, TM, D_MODEL), lambda j, i, k: (j // 3, i, 0)),
                   b_spec=pl.BlockSpec((D_MODEL, D_MODEL), lambda j, i, k: (0, j)),
                   o_spec=pl.BlockSpec((TM, D_MODEL), lambda j, i, k: (i, j)),
                   out_shape=jax.ShapeDtypeStruct((SEQ, n_chunks * D_MODEL), BF16), dims=NN, acc_shape=(8, 128))


def _qkv_dw(name, hs, dqkv, n_chunks):
    return _matmul(name, hs, dqkv, grid=(n_chunks, 2, SEQ // TM),
                   a_spec=pl.BlockSpec((None, TM, 512), lambda j, i, k: (j // 3, k, i)),
                   b_spec=pl.BlockSpec((None, TM, D_MODEL), lambda j, i, k: (j, k, 0)),
                   o_spec=pl.BlockSpec((512, D_MODEL), lambda j, i, k: (i, j)),
                   out_shape=jax.ShapeDtypeStruct((D_MODEL, n_chunks * D_MODEL), BF16), dims=TN, acc_shape=(512, D_MODEL))


def _qkv_dh(name, dqkv, w, n_chunks, deps):
    return _matmul(name, dqkv, w, grid=(n_chunks // 3, SEQ // TM, 3),
                   a_spec=pl.BlockSpec((None, TM, D_MODEL), lambda g, i, k: (3 * g + k, i, 0)),
                   b_spec=pl.BlockSpec((D_MODEL, D_MODEL), lambda g, i, k: (0, 3 * g + k)),
                   o_spec=pl.BlockSpec((None, TM, D_MODEL), lambda g, i, k: (g, i, 0)),
                   out_shape=jax.ShapeDtypeStruct((n_chunks // 3, SEQ, D_MODEL), F32), dims=NT, acc_shape=(TM, D_MODEL),
                   deps=deps)


def _local_step(x, target, norms, rpb, fetch, emit, deps):
    mix_pre, mix_post, ffn_pre, ffn_post = norms
    slopes = 2.0 ** (-8.0 * jnp.arange(1, N_HEADS + 1, dtype=F32) / N_HEADS)
    rpb_pad = jnp.pad(rpb, ((0, 0), (0, 1), (0, 128 - 31)))
    saved = []

    for layer in range(2):
        tag = f"l{layer}"
        h = _rms_fwd(tag + "_norm_mix", x, _row(mix_pre, layer), out_dtype=BF16, deps=deps if layer == 0 else ())
        if layer == 0:
            w_qkv, w_o = fetch("na", [h])
            table = _rpb_table(rpb_pad)
            qkv = _qkv_fwd(tag + "_qkv", h[None], w_qkv, 3)
            o, lse = _na_fwd(qkv, table)
            mixer = (h, qkv, o, lse, table)
        else:
            w_qkv, w_o = fetch("dil", [h])
            hs = jnp.stack([_sort_rows(h, dil) for _, dil in DIL_GROUPS])
            qkv = _qkv_fwd(tag + "_qkv", hs, w_qkv, 9)
            outs, lses, lses_sorted = [], [], []
            for g, (_, dil) in enumerate(DIL_GROUPS):
                o_g, lse_g = _dil_fwd(g, qkv, slopes * dil)
                outs.append(_unsort_rows(o_g, dil))
                lses.append(_unsort_rows(lse_g, dil))
                lses_sorted.append(lse_g)
            o, lse_total = _dil_merge(outs, lses)
            mixer = (hs, qkv, o, lses_sorted, lse_total)
        a = _square(tag + "_proj", o, w_o, NN, F32)
        x1 = _rms_fwd(tag + "_post_mix", a, _row(mix_post, layer), res=x)
        h2 = _rms_fwd(tag + "_norm_ffn", x1, _row(ffn_pre, layer), out_dtype=BF16)
        w_gu, w_down = fetch(f"ffn{layer}", [h2])
        act, hg, hu = _ffn_up(tag + "_ffn_up", h2, w_gu)
        f = _matmul(tag + "_ffn_down", act, w_down, grid=(SEQ // TM, N_DEV),
                    a_spec=pl.BlockSpec((None, TM, FF_PAD), lambda i, k: (k, i, 0)),
                    b_spec=pl.BlockSpec((None, FF_PAD, D_MODEL), lambda i, k: (k, 0, 0)),
                    o_spec=pl.BlockSpec((TM, D_MODEL), lambda i, k: (i, 0)),
                    out_shape=jax.ShapeDtypeStruct((SEQ, D_MODEL), F32), dims=NN, acc_shape=(TM, D_MODEL))
        x2 = _rms_fwd(tag + "_post_ffn", f, _row(ffn_post, layer), res=x1)
        saved.append((x, mixer, a, x1, h2, act, hg, hu, f, w_qkv, w_o, w_gu, w_down))
        x = x2

    dx, loss = _loss_head("loss_head", x, target)
    d_norm = {k: [None, None] for k in ("mix_pre", "mix_post", "ffn_pre", "ffn_post")}
    d_rpb = None

    for layer in (1, 0):
        tag = f"b{layer}"
        x0, mixer, a, x1, h2, act, hg, hu, f, w_qkv, w_o, w_gu, w_down = saved[layer]
        d_f, d_norm["ffn_post"][layer] = _rms_bwd(tag + "_post_ffn", f, _row(ffn_post, layer), [dx], out_dtype=BF16)
        dgu = _ffn_bwd_act(tag + "_ffn_act", d_f, w_down, hg, hu)
        d_down = _matmul(
            tag + "_ffn_ddown", act, d_f, grid=(N_DEV, SEQ // TM),
            a_spec=pl.BlockSpec((None, TM, FF_PAD), lambda d, k: (d, k, 0)),
            b_spec=pl.BlockSpec((TM, D_MODEL), lambda d, k: (k, 0)),
            o_spec=pl.BlockSpec((None, FF_PAD, D_MODEL), lambda d, k: (d, 0, 0)),
            out_shape=jax.ShapeDtypeStruct((N_DEV, FF_PAD, D_MODEL), BF16), dims=TN, acc_shape=(FF_PAD, D_MODEL))
        d_gu = _matmul(
            tag + "_ffn_dgu", h2, dgu, grid=(2, N_DEV, SEQ // TM),
            a_spec=pl.BlockSpec((TM, D_MODEL), lambda t, d, k: (k, 0)),
            b_spec=pl.BlockSpec((None, None, TM, FF_PAD), lambda t, d, k: (t, d, k, 0)),
            o_spec=pl.BlockSpec((None, None, D_MODEL, FF_PAD), lambda t, d, k: (d, t, 0, 0)),
            out_shape=jax.ShapeDtypeStruct((N_DEV, 2, D_MODEL, FF_PAD), BF16), dims=TN, acc_shape=(D_MODEL, FF_PAD))
        sent = emit(f"ffn{layer}", [d_gu, d_down])
        d_h2 = _matmul(
            tag + "_ffn_dh", dgu, w_gu, grid=(SEQ // TM, 2 * N_DEV),
            a_spec=pl.BlockSpec((None, None, TM, FF_PAD), lambda i, k: (k // N_DEV, k % N_DEV, i, 0)),
            b_spec=pl.BlockSpec((None, None, D_MODEL, FF_PAD), lambda i, k: (k % N_DEV, k // N_DEV, 0, 0)),
            o_spec=pl.BlockSpec((TM, D_MODEL), lambda i, k: (i, 0)),
            out_shape=jax.ShapeDtypeStruct((SEQ, D_MODEL), F32), dims=NT, acc_shape=(TM, D_MODEL), deps=sent)
        dx1, d_norm["ffn_pre"][layer] = _rms_bwd(tag + "_norm_ffn", x1, _row(ffn_pre, layer), [d_h2], res=dx)
        d_a, d_norm["mix_post"][layer] = _rms_bwd(tag + "_post_mix", a, _row(mix_post, layer), [dx1], out_dtype=BF16)
        d_wo = _square(tag + "_proj_dw", mixer[2], d_a, TN, BF16)
        if layer == 0:
            h, qkv, o, lse, table = mixer
            d_o = _square(tag + "_proj_do", d_a, w_o, NT, BF16)
            dqkv, gp = _na_bwd(qkv, table, d_o, lse)
            d_rpb = _rpb_grad(gp)[:, :15, :31]
            sent = emit("na", [_qkv_dw(tag + "_qkv_dw", h[None], dqkv, 3), d_wo])
            d_h = _qkv_dh(tag + "_qkv_dh", dqkv, w_qkv, 3, sent)
            d_hs = [d_h[0]]
        else:
            hs, qkv, o, lses, lse_total = mixer
            d_o = _square(tag + "_proj_do", d_a, w_o, NT, F32)
            parts = []
            for g, (_, dil) in enumerate(DIL_GROUPS):
                parts.append(_dil_bwd(g, qkv, slopes * dil, _sort_rows(d_o, dil), _sort_rows(o, dil), lses[g],
                                      _sort_rows(lse_total, dil)))
            dqkv = jnp.concatenate(parts, axis=0)
            sent = emit("dil", [_qkv_dw(tag + "_qkv_dw", hs, dqkv, 9), d_wo])
            d_h = _qkv_dh(tag + "_qkv_dh", dqkv, w_qkv, 9, sent)
            d_hs = [_unsort_rows(d_h[g], dil) for g, (_, dil) in enumerate(DIL_GROUPS)]
        dx, d_norm["mix_pre"][layer] = _rms_bwd(tag + "_norm_mix", x0, _row(mix_pre, layer), d_hs, res=dx1)

    d_gains = [jnp.concatenate(d_norm[k], axis=0) for k in ("mix_pre", "mix_post", "ffn_pre", "ffn_post")]
    return loss, dx, d_gains, d_rpb


RPB_SIZE = N_HEADS * 15 * 31


def _pack_small(gains, rpb):
    top = jnp.concatenate(gains, axis=0).reshape(64, 128)
    bottom = jnp.pad(rpb.reshape(-1), (0, 64 * 128 - RPB_SIZE)).reshape(64, 128)
    return jnp.concatenate([top, bottom], axis=0)


def _unpack_small(p):
    gains = p[:64].reshape(4, 2, D_MODEL)
    rpb = p[64:].reshape(-1)[:RPB_SIZE].reshape(1, N_HEADS, 15, 31)
    return [gains[i] for i in range(4)], rpb


GROUPS = ("na", "ffn0", "dil", "ffn1")


def kernel(x, norm_mix_pre, norm_mix_post, norm_ffn_pre, norm_ffn_post, na_w_qkv, na_w_o, na_rpb, dil_w_qkv, dil_w_o, ffn_w_gate, ffn_w_up, ffn_w_down, loss_target, m_norm_mix_pre, m_norm_mix_post, m_norm_ffn_pre, m_norm_ffn_post, m_na_w_qkv, m_na_w_o, m_na_rpb, m_dil_w_qkv, m_dil_w_o, m_ffn_w_gate, m_ffn_w_up, m_ffn_w_down, v_norm_mix_pre, v_norm_mix_post, v_norm_ffn_pre, v_norm_ffn_post, v_na_w_qkv, v_na_w_o, v_na_rpb, v_dil_w_qkv, v_dil_w_o, v_ffn_w_gate, v_ffn_w_up, v_ffn_w_down):
    na_cols, dil_cols, o_rows = 3 * D_MODEL // N_DEV, 9 * D_MODEL // N_DEV, D_MODEL // N_DEV
    ff_pad = FF_PAD - FF_SHARD
    me = (4 * lax.axis_index("x") + 2 * lax.axis_index("y") + lax.axis_index("c")).astype(jnp.int32).reshape(1)

    full = {
        "na": [((D_MODEL, 3 * D_MODEL), _columns(na_cols)), ((N_DEV, o_rows, D_MODEL), _leading)],
        "dil": [((D_MODEL, 9 * D_MODEL), _columns(dil_cols)), ((N_DEV, o_rows, D_MODEL), _leading)],
        "ffn0": [((N_DEV, 2, D_MODEL, FF_PAD), _leading), ((N_DEV, FF_PAD, D_MODEL), _leading)],
        "ffn1": [((N_DEV, 2, D_MODEL, FF_PAD), _leading), ((N_DEV, FF_PAD, D_MODEL), _leading)],
    }
    block = {
        "na": [(D_MODEL, na_cols), (o_rows, D_MODEL)], "dil": [(D_MODEL, dil_cols), (o_rows, D_MODEL)],
        "ffn0": [(2, D_MODEL, FF_PAD), (FF_PAD, D_MODEL)], "ffn1": [(2, D_MODEL, FF_PAD), (FF_PAD, D_MODEL)],
    }

    gu = jnp.pad(jnp.stack([ffn_w_gate, ffn_w_up], axis=1).astype(BF16), ((0, 0), (0, 0), (0, 0), (0, ff_pad)))
    down = jnp.pad(ffn_w_down.astype(BF16), ((0, 0), (0, ff_pad), (0, 0)))
    shards = {"na": [na_w_qkv[0].astype(BF16), na_w_o[0].astype(BF16)],
              "dil": [dil_w_qkv[0].astype(BF16), dil_w_o[0].astype(BF16)],
              "ffn0": [gu[0], down[0]], "ffn1": [gu[1], down[1]]}
    srcs = [s for g in GROUPS for s in shards[g]]
    gather_sets = {g: [(2 * gi + t, _whole, 2 * gi + t, _by_sender(full[g][t][1])) for t in range(2)]
                   for gi, g in enumerate(GROUPS)}
    land_shapes = [jax.ShapeDtypeStruct(full[g][t][0], BF16) for g in GROUPS for t in range(2)]

    def own_qkv(width):
        return pl.BlockSpec((256, width), lambda i, me: (i, 0)), pl.BlockSpec((256, width), lambda i, me: (i, me[0]))

    own_o = (pl.BlockSpec((32, D_MODEL), lambda i, me: (i, 0)), pl.BlockSpec((None, 32, D_MODEL), lambda i, me: (me[0], i, 0)))
    own_gu = (pl.BlockSpec((2, 256, FF_PAD), lambda i, me: (0, i, 0)),
              pl.BlockSpec((None, 2, 256, FF_PAD), lambda i, me: (me[0], 0, i, 0)))
    own_down = (pl.BlockSpec((96, D_MODEL), lambda i, me: (i, 0)), pl.BlockSpec((None, 96, D_MODEL), lambda i, me: (me[0], i, 0)))
    own = {"na": [own_qkv(na_cols), own_o], "dil": [own_qkv(dil_cols), own_o],
           "ffn0": [own_gu, own_down], "ffn1": [own_gu, own_down]}
    lands = _place_own("gather_own", me, srcs, land_shapes, [own[g][t][0] for g in GROUPS for t in range(2)],
                       [own[g][t][1] for g in GROUPS for t in range(2)], (4,))
    sems, srcs, lands, token = _send_start("gather_start", srcs, lands, [gather_sets[g] for g in GROUPS])

    def fetch(group, after):
        gi = GROUPS.index(group)
        local_sets = [(t, _whole, t, _by_sender(full[group][t][1])) for t in range(2)]
        _, (qkv, o) = _send_wait(f"gather_wait_{group}", sems[gi], srcs[2 * gi:2 * gi + 2], lands[2 * gi:2 * gi + 2],
                                 local_sets, after)
        return (qkv, o.reshape(D_MODEL, D_MODEL)) if group in ("na", "dil") else (qkv, o)

    def grad_source(group, t):
        return _columns(block[group][0][1]) if (group in ("na", "dil") and t == 0) else _leading

    in_flight = {}

    def emit(group, grads):
        if group in ("na", "dil"):
            grads = [grads[0], grads[1].reshape(N_DEV, o_rows, D_MODEL)]
        sets = [(t, grad_source(group, t), t, _by_distance) for t in range(2)]
        landing = [lax.empty((N_DEV - 1,) + block[group][t], BF16) for t in range(2)]
        sems_g, grads, landing, tok = _send_start(f"exchange_start_{group}", grads, landing, [sets])
        in_flight[group] = (sems_g[0], grads, landing, sets)
        return [tok]

    norms = (norm_mix_pre, norm_mix_post, norm_ffn_pre, norm_ffn_post)
    loss, grad_x, d_gains, d_rpb = _local_step(x[0], loss_target[0], norms, na_rpb[0], fetch, emit, [token])
    loss = lax.psum(loss, ("x", "y", "c"))
    small = _all_gather("gather_small", [_pack_small(d_gains, d_rpb)], [jax.ShapeDtypeStruct((N_DEV, 128, 128), F32)],
                        [_leading])[0]

    landed, sent = {}, {}
    for group in ("ffn1", "dil", "ffn0", "na"):
        sems_g, grads, landing, sets = in_flight[group]
        sent[group], landed[group] = _send_wait(f"exchange_wait_{group}", sems_g, grads, landing, sets, [grad_x])

    def one(rows, tile, ncols, columns):
        own = (pl.BlockSpec((tile, ncols), lambda i, me: (i, me[0])) if columns
               else pl.BlockSpec((None, tile, ncols), lambda i, me: (me[0], i, 0)))
        return dict(grid=(rows // tile,), land_specs=[pl.BlockSpec((N_DEV - 1, tile, ncols), lambda i, me: (0, i, 0))],
                    own_specs=[own], p_spec=pl.BlockSpec((None, tile, ncols), lambda i, me: (0, i, 0)))

    def layered(block_shape, index, p_block, n_tiles):
        def specs(lead_size, lead):
            shape = (lead_size,) + block_shape
            return [pl.BlockSpec(shape, lambda l, r, me: index(lead(me), jnp.where(l == 0, r, n_tiles - 1))),
                    pl.BlockSpec(shape, lambda l, r, me: index(lead(me), jnp.where(l == 0, 0, r)))]
        return dict(grid=(2, n_tiles), land_specs=specs(N_DEV - 1, lambda me: 0), own_specs=specs(None, lambda me: me[0]),
                    p_spec=pl.BlockSpec(p_block, lambda l, r, me: (l, r, 0)))

    gu_lands, gu_owns = [landed["ffn0"][0], landed["ffn1"][0]], [sent["ffn0"][0], sent["ffn1"][0]]
    down_lands, down_owns = [landed["ffn0"][1], landed["ffn1"][1]], [sent["ffn0"][1], sent["ffn1"][1]]
    updates = {
        "na_w_qkv": _adamw("adamw_na_qkv", me, [landed["na"][0]], [sent["na"][0]], na_w_qkv, m_na_w_qkv, v_na_w_qkv,
                           **one(D_MODEL, 256, na_cols, True)),
        "na_w_o": _adamw("adamw_na_o", me, [landed["na"][1]], [sent["na"][1]], na_w_o, m_na_w_o, v_na_w_o,
                         **one(o_rows, o_rows, D_MODEL, False)),
        "dil_w_qkv": _adamw("adamw_dil_qkv", me, [landed["dil"][0]], [sent["dil"][0]], dil_w_qkv, m_dil_w_qkv, v_dil_w_qkv,
                            **one(D_MODEL, 128, dil_cols, True)),
        "dil_w_o": _adamw("adamw_dil_o", me, [landed["dil"][1]], [sent["dil"][1]], dil_w_o, m_dil_w_o, v_dil_w_o,
                          **one(o_rows, o_rows, D_MODEL, False)),
        "ffn_w_gate": _adamw("adamw_gate", me, gu_lands, gu_owns, ffn_w_gate, m_ffn_w_gate, v_ffn_w_gate,
                             **layered((None, 128, FF_PAD), lambda lead, r: (lead, 0, r, 0), (None, 128, FF_SHARD), 8)),
        "ffn_w_up": _adamw("adamw_up", me, gu_lands, gu_owns, ffn_w_up, m_ffn_w_up, v_ffn_w_up,
                           **layered((None, 128, FF_PAD), lambda lead, r: (lead, 1, r, 0), (None, 128, FF_SHARD), 8)),
        "ffn_w_down": _adamw("adamw_down", me, down_lands, down_owns, ffn_w_down, m_ffn_w_down, v_ffn_w_down,
                             **layered((176, D_MODEL), lambda lead, r: (lead, r, 0), (None, 176, D_MODEL), 2)),
    }
    gains = [norm_mix_pre, norm_mix_post, norm_ffn_pre, norm_ffn_post]
    m_gains = [m_norm_mix_pre, m_norm_mix_post, m_norm_ffn_pre, m_norm_ffn_post]
    v_gains = [v_norm_mix_pre, v_norm_mix_post, v_norm_ffn_pre, v_norm_ffn_post]
    packed = _adamw("adamw_small", me, [small], (), _pack_small(gains, na_rpb)[None], _pack_small(m_gains, m_na_rpb)[None],
                    _pack_small(v_gains, v_na_rpb)[None], grid=(1,),
                    land_specs=[pl.BlockSpec((N_DEV, 128, 128), lambda i, me: (0, 0, 0))], own_specs=[],
                    p_spec=pl.BlockSpec((None, 128, 128), lambda i, me: (0, 0, 0)))
    small_out = [_unpack_small(p[0]) for p in packed]

    order = ["na_w_qkv", "na_w_o", "na_rpb", "dil_w_qkv", "dil_w_o", "ffn_w_gate", "ffn_w_up", "ffn_w_down"]
    result = [loss, grad_x[None]]
    for kind in range(4):
        gains_k, rpb_k = small_out[kind]
        result += gains_k
        result += [rpb_k if name == "na_rpb" else updates[name][kind] for name in order]
    return tuple(result)
```

```python
import functools

import jax
import jax.numpy as jnp
from jax import lax
from jax.experimental import pallas as pl
from jax.experimental.pallas import tpu as pltpu

F32 = jnp.float32
BF16 = jnp.bfloat16
MESH = pl.DeviceIdType.MESH
ANY = pl.BlockSpec(memory_space=pl.ANY)
HBM = pl.BlockSpec(memory_space=pltpu.HBM)
SEM = pl.BlockSpec(memory_space=pltpu.SEMAPHORE)
EFFECT = pltpu.SideEffectType.DATAFLOW_SIDE_EFFECTING

N_DEV = 8
SEQ = 2048
D_MODEL = 1024
N_HEADS = 16
HEAD_DIM = 64
GRID_W = 64
NA_ROWS = 8
SEQ_ROWS = SEQ // GRID_W
DIL_GROUPS = ((128, 1), (512, 4), (2048, 16))
BAND = 128
RADIUS = 64
FF_SHARD = 352
FF_PAD = 384
RMS_EPS = 1e-6
NEG_INF = -1e30
Q_SCALE = HEAD_DIM ** -0.5

ADAM_LR = 0.001
ADAM_B1 = 0.9
ADAM_B2 = 0.999
ADAM_EPS = 1e-08
ADAM_WD = 0.01
ADAM_STEP = 10

VMEM_LIMIT = 56 * 1024 * 1024
TM = 512

NN = (((1,), (0,)), ((), ()))
NT = (((1,), (1,)), ((), ()))
TN = (((0,), (0,)), ((), ()))


def _params():
    return pltpu.CompilerParams(vmem_limit_bytes=VMEM_LIMIT)


def _matmul(name, a, b, *, grid, a_spec, b_spec, o_spec, out_shape, dims, acc_shape, deps=(), inner=1):
    nk = grid[-1]
    kaxis = len(grid) - 1

    def body(a_ref, b_ref, *rest):
        o_ref, acc_ref = rest[-2], rest[-1]
        if inner == 1:
            part = lax.dot_general(a_ref[...].astype(BF16), b_ref[...].astype(BF16), dims, preferred_element_type=F32)
        elif len(b_ref.shape) == 2:
            a_all = jnp.concatenate([a_ref[j].astype(BF16) for j in range(inner)], axis=1)
            part = lax.dot_general(a_all, b_ref[...].astype(BF16), dims, preferred_element_type=F32)
        else:
            part = sum(lax.dot_general(a_ref[j].astype(BF16), b_ref[j].astype(BF16), dims, preferred_element_type=F32)
                       for j in range(inner))
        if nk == 1:
            o_ref[...] = part.astype(o_ref.dtype)
        else:
            k = pl.program_id(kaxis)

            @pl.when(k == 0)
            def _():
                acc_ref[...] = part

            @pl.when(k > 0)
            def _():
                acc_ref[...] += part

            @pl.when(k == nk - 1)
            def _():
                o_ref[...] = acc_ref[...].astype(o_ref.dtype)

    return pl.pallas_call(
        body, out_shape=out_shape, grid=grid, in_specs=[a_spec, b_spec] + [ANY] * len(deps), out_specs=o_spec,
        scratch_shapes=[pltpu.VMEM(acc_shape, F32)], name=name, compiler_params=_params())(a, b, *deps)


def _rms_fwd(name, x, g, res=None, out_dtype=F32, deps=()):
    n_tiles = SEQ // TM
    has_res = res is not None

    def body(*refs):
        x_ref, g_ref = refs[0], refs[1]
        o_ref = refs[-1]
        xv = x_ref[...]
        r = lax.rsqrt(jnp.mean(xv * xv, axis=-1, keepdims=True) + RMS_EPS)
        y = xv * r * g_ref[...]
        if has_res:
            y = refs[2][...] + y
        o_ref[...] = y.astype(o_ref.dtype)

    tile = pl.BlockSpec((TM, D_MODEL), lambda i: (i, 0))
    gspec = pl.BlockSpec((1, D_MODEL), lambda i: (0, 0))
    ins = [x, g] + ([res] if has_res else []) + list(deps)
    specs = [tile, gspec] + ([tile] if has_res else []) + [ANY] * len(deps)
    return pl.pallas_call(
        body, out_shape=jax.ShapeDtypeStruct((SEQ, D_MODEL), out_dtype), grid=(n_tiles,), in_specs=specs,
        out_specs=tile, name=name, compiler_params=_params())(*ins)


def _rms_bwd(name, x, g, dys, res=None, out_dtype=F32):
    n_tiles = SEQ // TM
    n_dy = len(dys)
    has_res = res is not None

    def body(*refs):
        x_ref, g_ref = refs[0], refs[1]
        dy_refs = refs[2:2 + n_dy]
        res_ref = refs[2 + n_dy] if has_res else None
        dx_ref, dg_ref, acc_ref = refs[-3], refs[-2], refs[-1]
        i = pl.program_id(0)
        xv = x_ref[...]
        r = lax.rsqrt(jnp.mean(xv * xv, axis=-1, keepdims=True) + RMS_EPS)
        xn = xv * r
        dy = dy_refs[0][...].astype(F32)
        for extra in dy_refs[1:]:
            dy = dy + extra[...].astype(F32)
        dyg = dy * g_ref[...]
        dx = r * (dyg - xn * jnp.mean(dyg * xn, axis=-1, keepdims=True))
        if has_res:
            dx = res_ref[...] + dx
        dx_ref[...] = dx.astype(dx_ref.dtype)
        part = jnp.sum((dy * xn).reshape(TM // 8, 8, D_MODEL), axis=0)

        @pl.when(i == 0)
        def _():
            acc_ref[...] = part

        @pl.when(i > 0)
        def _():
            acc_ref[...] += part

        @pl.when(i == n_tiles - 1)
        def _():
            dg_ref[...] = jnp.broadcast_to(jnp.sum(acc_ref[...], axis=0, keepdims=True), (8, D_MODEL))

    tile = pl.BlockSpec((TM, D_MODEL), lambda i: (i, 0))
    gspec = pl.BlockSpec((1, D_MODEL), lambda i: (0, 0))
    ins = [x, g] + list(dys) + ([res] if has_res else [])
    specs = [tile, gspec] + [tile] * n_dy + ([tile] if has_res else [])
    dx, dg = pl.pallas_call(
        body, out_shape=(jax.ShapeDtypeStruct((SEQ, D_MODEL), out_dtype), jax.ShapeDtypeStruct((8, D_MODEL), F32)),
        grid=(n_tiles,), in_specs=specs,
        out_specs=(tile, pl.BlockSpec((8, D_MODEL), lambda i: (0, 0))),
        scratch_shapes=[pltpu.VMEM((8, D_MODEL), F32)], name=name, compiler_params=_params())(*ins)
    return dx, dg[0:1]


def _loss_head(name, y, target):
    n_tiles = SEQ // TM

    def body(y_ref, t_ref, dy_ref, loss_ref, acc_ref):
        i = pl.program_id(0)
        diff = y_ref[...] - t_ref[...]
        dy_ref[...] = diff * (1.0 / D_MODEL)
        part = jnp.sum((diff * diff).reshape(TM // 8, 8, D_MODEL), axis=0)

        @pl.when(i == 0)
        def _():
            acc_ref[...] = part

        @pl.when(i > 0)
        def _():
            acc_ref[...] += part

        @pl.when(i == n_tiles - 1)
        def _():
            loss_ref[...] = jnp.full((8, 128), jnp.sum(acc_ref[...]) * (0.5 / D_MODEL), F32)

    tile = pl.BlockSpec((TM, D_MODEL), lambda i: (i, 0))
    dy, loss = pl.pallas_call(
        body, out_shape=(jax.ShapeDtypeStruct((SEQ, D_MODEL), F32), jax.ShapeDtypeStruct((8, 128), F32)),
        grid=(n_tiles,), in_specs=[tile, tile], out_specs=(tile, pl.BlockSpec((8, 128), lambda i: (0, 0))),
        scratch_shapes=[pltpu.VMEM((8, D_MODEL), F32)], name=name, compiler_params=_params())(y, target)
    return dy, loss[0, 0]


def _row_index(shape):
    return lax.broadcasted_iota(jnp.int32, shape, 0)


def _lane_index(shape):
    return lax.broadcasted_iota(jnp.int32, shape, len(shape) - 1)


def _skew_rows(t, direction):
    q = _row_index(t.shape) & (GRID_W - 1)
    for bit in range(6):
        step = 1 << bit
        shift = step if direction > 0 else 128 - step
        t = jnp.where((q & step) != 0, pltpu.roll(t, shift, 1), t)
    return t


def _rpb_table(rpb_pad):
    rows = 16 * GRID_W

    def body(r_ref, t_ref):
        lane = _lane_index((rows, 128))
        v = pltpu.roll(r_ref[...], 128 - 15, 1)
        t = _skew_rows(jnp.broadcast_to(v[:, None, :], (16, GRID_W, 128)).reshape(rows, 128), +1)
        t = jnp.where(lane < GRID_W, t, 0.0)
        below = jnp.concatenate([t[GRID_W:], jnp.zeros((GRID_W, 128), F32)], axis=0)
        t_ref[...] = (t + pltpu.roll(below, GRID_W, 1)).reshape(16, GRID_W, 128)

    return pl.pallas_call(
        body, out_shape=jax.ShapeDtypeStruct((N_HEADS, 16, GRID_W, 128), F32), grid=(N_HEADS,),
        in_specs=[pl.BlockSpec((None, 16, 128), lambda h: (h, 0, 0))],
        out_specs=pl.BlockSpec((None, 16, GRID_W, 128), lambda h: (h, 0, 0, 0)),
        name="rpb_table", compiler_params=_params())(rpb_pad)


def _rpb_grad(gp):
    rows = 16 * GRID_W

    def body(g_ref, o_ref):
        lane = _lane_index((rows, 128))
        g = g_ref[...].reshape(rows, 128)
        low = jnp.where(lane < GRID_W, g, 0.0)
        high = pltpu.roll(jnp.where(lane >= GRID_W, g, 0.0), GRID_W, 1)
        above = jnp.concatenate([jnp.zeros((GRID_W, 128), F32), high[:rows - GRID_W]], axis=0)
        diag = jnp.sum(_skew_rows(low + above, -1).reshape(16, GRID_W, 128), axis=1)
        o_ref[...] = pltpu.roll(diag, 15, 1)

    return pl.pallas_call(
        body, out_shape=jax.ShapeDtypeStruct((N_HEADS, 16, 128), F32), grid=(N_HEADS,),
        in_specs=[pl.BlockSpec((None, 16, GRID_W, 128), lambda h: (h, 0, 0, 0))],
        out_specs=pl.BlockSpec((None, 16, 128), lambda h: (h, 0, 0)),
        name="rpb_grad", compiler_params=_params())(gp)


NA_KEYS = NA_ROWS * GRID_W


def _na_window(i):
    first_row = jnp.clip(i - NA_ROWS // 2, 0, SEQ_ROWS - NA_ROWS)
    return pl.multiple_of(first_row * GRID_W, GRID_W), first_row - i + NA_ROWS - 1


NA_STEP = 4


def _na_valid():
    q = _row_index((2 * GRID_W, NA_KEYS)) & (GRID_W - 1)
    k = _lane_index((2 * GRID_W, NA_KEYS)) & (GRID_W - 1)
    first_col = jnp.clip(q - 8, 0, GRID_W - 16)
    return (k >= first_col) & (k < first_col + 16)


def _head_masks():
    lane = _lane_index((1, 128))
    return (lane < HEAD_DIM, lane >= HEAD_DIM)


def _stack_heads(t, masks):
    zero = jnp.zeros_like(t)
    return jnp.concatenate([jnp.where(masks[0], t, zero), jnp.where(masks[1], t, zero)], axis=0)


def _unstack_heads(t, masks):
    n = t.shape[0] // 2
    return jnp.where(masks[0], t[:n], t[n:])


def _stack_columns(t):
    return jnp.concatenate([t[:, 0:1], t[:, HEAD_DIM:HEAD_DIM + 1]], axis=0)


def _na_scores(qs, kw, tp_ref, dr0, valid):
    s = lax.dot_general(qs, kw, NT, preferred_element_type=F32)
    bias = jnp.concatenate(
        [jnp.concatenate([tp_ref[a, pl.ds(dr0 + 2 * c, 1), :, :].reshape(GRID_W, 128) for c in range(4)], axis=1)
         for a in range(2)], axis=0)
    return jnp.where(valid, s + bias, NEG_INF)


def _na_specs():
    q_spec = pl.BlockSpec((NA_STEP * GRID_W, 128), lambda hp, i: (i, hp))
    k_spec = pl.BlockSpec((SEQ, 128), lambda hp, i: (0, 8 + hp))
    v_spec = pl.BlockSpec((SEQ, 128), lambda hp, i: (0, 16 + hp))
    tp_spec = pl.BlockSpec((2, 16, GRID_W, 128), lambda hp, i: (hp, 0, 0, 0))
    return q_spec, k_spec, v_spec, tp_spec


def _na_fwd(qkv, table):
    def body(q_ref, k_ref, v_ref, tp_ref, o_ref, lse_ref):
        valid = _na_valid()
        masks = _head_masks()
        for r in range(NA_STEP):
            rows = slice(r * GRID_W, (r + 1) * GRID_W)
            start, dr0 = _na_window(pl.program_id(1) * NA_STEP + r)
            kw = k_ref[pl.ds(start, NA_KEYS), :]
            vw = v_ref[pl.ds(start, NA_KEYS), :]
            s = _na_scores(_stack_heads(q_ref[rows, :] * Q_SCALE, masks), kw, tp_ref, dr0, valid)
            m = jnp.max(s, axis=-1, keepdims=True)
            p = jnp.exp(s - m)
            denom = jnp.sum(p, axis=-1, keepdims=True)
            out = jnp.dot(p.astype(BF16), vw, preferred_element_type=F32) / denom
            o_ref[rows, :] = _unstack_heads(out, masks).astype(o_ref.dtype)
            lse_ref[rows, :] = _unstack_heads(jnp.broadcast_to(m + jnp.log(denom), (2 * GRID_W, 128)), masks)

    q_spec, k_spec, v_spec, tp_spec = _na_specs()
    return pl.pallas_call(
        body, out_shape=(jax.ShapeDtypeStruct((SEQ, D_MODEL), BF16), jax.ShapeDtypeStruct((SEQ, D_MODEL), F32)),
        grid=(N_HEADS // 2, SEQ_ROWS // NA_STEP), in_specs=[q_spec, k_spec, v_spec, tp_spec],
        out_specs=(q_spec, q_spec), name="na_fwd", compiler_params=_params())(qkv, qkv, qkv, table)


def _na_bwd(qkv, table, d_out, lse):
    def body(q_ref, k_ref, v_ref, tp_ref, do_ref, lse_ref, dqkv_ref, gp_ref, dk_acc, dv_acc):
        step = pl.program_id(1)

        @pl.when(step == 0)
        def _():
            dk_acc[...] = jnp.zeros_like(dk_acc)
            dv_acc[...] = jnp.zeros_like(dv_acc)
            gp_ref[...] = jnp.zeros_like(gp_ref)

        valid = _na_valid()
        masks = _head_masks()
        for r in range(NA_STEP):
            rows = slice(r * GRID_W, (r + 1) * GRID_W)
            i = step * NA_STEP + r
            start, dr0 = _na_window(i)
            kw = k_ref[pl.ds(start, NA_KEYS), :]
            vw = v_ref[pl.ds(start, NA_KEYS), :]
            qs = _stack_heads(q_ref[rows, :] * Q_SCALE, masks)
            dos = _stack_heads(do_ref[rows, :], masks)
            p = jnp.exp(_na_scores(qs, kw, tp_ref, dr0, valid) - _stack_columns(lse_ref[rows, :]))
            dp = lax.dot_general(dos, vw, NT, preferred_element_type=F32)
            ds = p * (dp - jnp.sum(p * dp, axis=-1, keepdims=True))
            for a in range(2):
                for c in range(4):
                    gp_ref[a, pl.ds(dr0 + 2 * c, 1), :, :] += (
                        ds[a * GRID_W:(a + 1) * GRID_W, 128 * c:128 * (c + 1)].reshape(1, GRID_W, 128))
            dsb = ds.astype(BF16)
            dq = _unstack_heads(jnp.dot(dsb, kw, preferred_element_type=F32), masks) * Q_SCALE
            dqkv_ref[0, pl.ds(pl.multiple_of(i * GRID_W, GRID_W), GRID_W), :] = dq.astype(dqkv_ref.dtype)
            dk_acc[pl.ds(start, NA_KEYS), :] += lax.dot_general(dsb, qs, TN, preferred_element_type=F32)
            dv_acc[pl.ds(start, NA_KEYS), :] += lax.dot_general(p.astype(BF16), dos, TN, preferred_element_type=F32)

        @pl.when(step == SEQ_ROWS // NA_STEP - 1)
        def _():
            dqkv_ref[1] = dk_acc[...].astype(dqkv_ref.dtype)
            dqkv_ref[2] = dv_acc[...].astype(dqkv_ref.dtype)

    q_spec, k_spec, v_spec, tp_spec = _na_specs()
    return pl.pallas_call(
        body,
        out_shape=(jax.ShapeDtypeStruct((3, SEQ, D_MODEL), BF16), jax.ShapeDtypeStruct((N_HEADS, 16, GRID_W, 128), F32)),
        grid=(N_HEADS // 2, SEQ_ROWS // NA_STEP), in_specs=[q_spec, k_spec, v_spec, tp_spec, q_spec, q_spec],
        out_specs=(pl.BlockSpec((3, SEQ, 128), lambda hp, i: (0, 0, hp)), tp_spec),
        scratch_shapes=[pltpu.VMEM((SEQ, 128), F32), pltpu.VMEM((SEQ, 128), F32)],
        name="na_bwd", compiler_params=_params())(qkv, qkv, qkv, table, d_out, lse)


DIL_STEP = 4


def _dil_geometry(group):
    dil = DIL_GROUPS[group][1]
    sub_len = SEQ // dil
    blocks = sub_len // BAND
    return dil, sub_len, max(blocks // DIL_STEP, 1), max(DIL_STEP // blocks, 1), min(2 * BAND, sub_len)


def _dil_block(step, r, sub_len, subs):
    per_sub = DIL_STEP // subs
    return (r // per_sub) * sub_len, step * per_sub + r % per_sub


def _dil_window(b, sub_len, n_keys):
    if n_keys == sub_len:
        return 0
    return pl.multiple_of(jnp.clip(b * BAND - RADIUS, 0, sub_len - n_keys), RADIUS)


def _dil_bias(b, start, n_keys, slope_ref, hp):
    row = _row_index((2 * BAND, n_keys))
    qpos = b * BAND + (row & (BAND - 1))
    kpos = start + _lane_index((2 * BAND, n_keys))
    dist = jnp.abs(qpos - kpos)
    slope = jnp.where(row < BAND, slope_ref[2 * hp], slope_ref[2 * hp + 1])
    return slope * dist.astype(F32), dist <= RADIUS


def _dil_scores(qs, kw, penalty, valid):
    return jnp.where(valid, lax.dot_general(qs, kw, NT, preferred_element_type=F32) - penalty, NEG_INF)


def _dil_specs(group):
    dil, sub_len, steps, subs, _ = _dil_geometry(group)
    col = group * 24
    rows = DIL_STEP * BAND
    q_spec = pl.BlockSpec((rows, 128), lambda n, hp, b: (n * steps + b, col + hp))
    k_spec = pl.BlockSpec((subs * sub_len, 128), lambda n, hp, b: (n, col + 8 + hp))
    v_spec = pl.BlockSpec((subs * sub_len, 128), lambda n, hp, b: (n, col + 16 + hp))
    tile = pl.BlockSpec((rows, 128), lambda n, hp, b: (n * steps + b, hp))
    smem = pl.BlockSpec(memory_space=pltpu.SMEM)
    return (dil // subs, N_HEADS // 2, steps), q_spec, k_spec, v_spec, tile, smem


def _dil_fwd(group, qkv, slopes):
    _, sub_len, _, subs, n_keys = _dil_geometry(group)

    def body(q_ref, k_ref, v_ref, slope_ref, o_ref, lse_ref):
        hp = pl.program_id(1)
        masks = _head_masks()
        for r in range(DIL_STEP):
            rows = slice(r * BAND, (r + 1) * BAND)
            base, b = _dil_block(pl.program_id(2), r, sub_len, subs)
            start = _dil_window(b, sub_len, n_keys)
            kw = k_ref[pl.ds(base + start, n_keys), :]
            vw = v_ref[pl.ds(base + start, n_keys), :]
            penalty, valid = _dil_bias(b, start, n_keys, slope_ref, hp)
            s = _dil_scores(_stack_heads(q_ref[rows, :] * Q_SCALE, masks), kw, penalty, valid)
            m = jnp.max(s, axis=-1, keepdims=True)
            p = jnp.exp(s - m)
            denom = jnp.sum(p, axis=-1, keepdims=True)
            out = jnp.dot(p.astype(BF16), vw, preferred_element_type=F32) / denom
            o_ref[rows, :] = _unstack_heads(out, masks)
            lse_ref[rows, :] = _unstack_heads(jnp.broadcast_to(m + jnp.log(denom), (2 * BAND, 128)), masks)

    grid, q_spec, k_spec, v_spec, tile, smem = _dil_specs(group)
    return pl.pallas_call(
        body, out_shape=(jax.ShapeDtypeStruct((SEQ, D_MODEL), F32), jax.ShapeDtypeStruct((SEQ, D_MODEL), F32)),
        grid=grid, in_specs=[q_spec, k_spec, v_spec, smem], out_specs=(tile, tile),
        name=f"dil_fwd_{group}", compiler_params=_params())(qkv, qkv, qkv, slopes)


def _dil_merge(outs, lses):
    def body(o0, o1, o2, l0, l1, l2, o_ref, lse_ref):
        ls = [l0[...], l1[...], l2[...]]
        m = jnp.maximum(jnp.maximum(ls[0], ls[1]), ls[2])
        es = [jnp.exp(v - m) for v in ls]
        total = es[0] + es[1] + es[2]
        o_ref[...] = (es[0] * o0[...] + es[1] * o1[...] + es[2] * o2[...]) / total
        lse_ref[...] = m + jnp.log(total)

    tile = pl.BlockSpec((TM, D_MODEL), lambda i: (i, 0))
    return pl.pallas_call(
        body, out_shape=(jax.ShapeDtypeStruct((SEQ, D_MODEL), F32), jax.ShapeDtypeStruct((SEQ, D_MODEL), F32)),
        grid=(SEQ // TM,), in_specs=[tile] * 6, out_specs=(tile, tile), name="dil_merge",
        compiler_params=_params())(*outs, *lses)


def _dil_bwd(group, qkv, slopes, d_out, out, lse_group, lse_total):
    _, sub_len, steps, subs, n_keys = _dil_geometry(group)

    def body(q_ref, k_ref, v_ref, slope_ref, do_ref, o_ref, lg_ref, lt_ref, dqkv_ref, dk_acc, dv_acc):
        hp, step = pl.program_id(1), pl.program_id(2)

        @pl.when(step == 0)
        def _():
            dk_acc[...] = jnp.zeros_like(dk_acc)
            dv_acc[...] = jnp.zeros_like(dv_acc)

        masks = _head_masks()
        for r in range(DIL_STEP):
            rows = slice(r * BAND, (r + 1) * BAND)
            base, b = _dil_block(step, r, sub_len, subs)
            start = _dil_window(b, sub_len, n_keys)
            keys = pl.ds(base + start, n_keys)
            kw = k_ref[keys, :]
            vw = v_ref[keys, :]
            penalty, valid = _dil_bias(b, start, n_keys, slope_ref, hp)
            qs = _stack_heads(q_ref[rows, :] * Q_SCALE, masks)
            lse2 = lg_ref[rows, :]
            weight = jnp.exp(lse2 - lt_ref[rows, :])
            do2 = do_ref[rows, :]
            dogs = _stack_heads((weight * do2).astype(BF16), masks)
            delta = _stack_columns(weight) * jnp.sum(_stack_heads(do2 * o_ref[rows, :], masks), axis=-1, keepdims=True)
            p = jnp.exp(_dil_scores(qs, kw, penalty, valid) - _stack_columns(lse2))
            dp = lax.dot_general(dogs, vw, NT, preferred_element_type=F32)
            dsb = (p * (dp - delta)).astype(BF16)
            dq = _unstack_heads(jnp.dot(dsb, kw, preferred_element_type=F32), masks) * Q_SCALE
            dqkv_ref[0, pl.ds(pl.multiple_of(base + b * BAND, BAND), BAND), :] = dq.astype(dqkv_ref.dtype)
            dk_acc[keys, :] += lax.dot_general(dsb, qs, TN, preferred_element_type=F32)
            dv_acc[keys, :] += lax.dot_general(p.astype(BF16), dogs, TN, preferred_element_type=F32)

        @pl.when(step == steps - 1)
        def _():
            dqkv_ref[1] = dk_acc[...].astype(dqkv_ref.dtype)
            dqkv_ref[2] = dv_acc[...].astype(dqkv_ref.dtype)

    grid, q_spec, k_spec, v_spec, tile, smem = _dil_specs(group)
    return pl.pallas_call(
        body, out_shape=jax.ShapeDtypeStruct((3, SEQ, D_MODEL), BF16), grid=grid,
        in_specs=[q_spec, k_spec, v_spec, smem, tile, tile, tile, tile],
        out_specs=pl.BlockSpec((3, subs * sub_len, 128), lambda n, hp, b: (0, n, hp)),
        scratch_shapes=[pltpu.VMEM((subs * sub_len, 128), F32), pltpu.VMEM((subs * sub_len, 128), F32)],
        name=f"dil_bwd_{group}", compiler_params=_params())(qkv, qkv, qkv, slopes, d_out, out, lse_group, lse_total)


def _sort_rows(t, dil):
    if dil == 1:
        return t
    return t.reshape(SEQ // dil, dil, t.shape[-1]).transpose(1, 0, 2).reshape(SEQ, t.shape[-1])


def _unsort_rows(t, dil):
    if dil == 1:
        return t
    return t.reshape(dil, SEQ // dil, t.shape[-1]).transpose(1, 0, 2).reshape(SEQ, t.shape[-1])


def _ffn_up(name, h, w_gu):
    def body(h_ref, wg_ref, wu_ref, act_ref, hg_ref, hu_ref):
        hv = h_ref[...]
        hg = jnp.dot(hv, wg_ref[...], preferred_element_type=F32)
        hu = jnp.dot(hv, wu_ref[...], preferred_element_type=F32)
        act_ref[...] = (hg * jax.nn.sigmoid(hg) * hu).astype(act_ref.dtype)
        hg_ref[...] = hg.astype(hg_ref.dtype)
        hu_ref[...] = hu.astype(hu_ref.dtype)

    out = pl.BlockSpec((None, TM, FF_PAD), lambda d, i: (d, i, 0))
    shape = jax.ShapeDtypeStruct((N_DEV, SEQ, FF_PAD), BF16)
    return pl.pallas_call(
        body, out_shape=(shape, shape, shape), grid=(N_DEV, SEQ // TM),
        in_specs=[pl.BlockSpec((TM, D_MODEL), lambda d, i: (i, 0)),
                  pl.BlockSpec((None, None, D_MODEL, FF_PAD), lambda d, i: (d, 0, 0, 0)),
                  pl.BlockSpec((None, None, D_MODEL, FF_PAD), lambda d, i: (d, 1, 0, 0))],
        out_specs=(out, out, out), name=name, compiler_params=_params())(h, w_gu, w_gu)


def _ffn_bwd_act(name, d_f, w_down, hg, hu):
    def body(df_ref, wd_ref, hg_ref, hu_ref, dgu_ref):
        dact = lax.dot_general(df_ref[...], wd_ref[...], NT, preferred_element_type=F32)
        hgv = hg_ref[...].astype(F32)
        sig = jax.nn.sigmoid(hgv)
        dgu_ref[0] = (dact * hu_ref[...].astype(F32) * (sig * (1.0 + hgv * (1.0 - sig)))).astype(dgu_ref.dtype)
        dgu_ref[1] = (dact * hgv * sig).astype(dgu_ref.dtype)

    tile = pl.BlockSpec((None, TM, FF_PAD), lambda d, i: (d, i, 0))
    return pl.pallas_call(
        body, out_shape=jax.ShapeDtypeStruct((2, N_DEV, SEQ, FF_PAD), BF16), grid=(N_DEV, SEQ // TM),
        in_specs=[pl.BlockSpec((TM, D_MODEL), lambda d, i: (i, 0)),
                  pl.BlockSpec((None, FF_PAD, D_MODEL), lambda d, i: (d, 0, 0)), tile, tile],
        out_specs=pl.BlockSpec((2, None, TM, FF_PAD), lambda d, i: (0, d, i, 0)),
        name=name, compiler_params=_params())(d_f, w_down, hg, hu)


def _position():
    return lax.axis_index("x"), lax.axis_index("y"), lax.axis_index("c")


def _flat(p):
    return 4 * p[0] + 2 * p[1] + p[2]


def _peer(me, k):
    x, y, c = me
    return (1 - x if k & 4 else x, 1 - y if k & 2 else y, 1 - c if k & 1 else c)


def _columns(width):
    return lambda ref, d: ref.at[:, pl.ds(pl.multiple_of(d * width, 128), width)]


def _leading(ref, d):
    return ref.at[d]


def _whole(ref, d):
    return ref


def _by_sender(window):
    return lambda ref, sender, k: window(ref, sender)


def _by_distance(ref, sender, k):
    return ref.at[k - 1]


def _place_own(name, me, srcs, land_shapes, in_specs, out_specs, grid):
    n = len(srcs)

    def body(me_ref, *refs):
        for src_ref, dst_ref in zip(refs[:n], refs[n:]):
            dst_ref[...] = src_ref[...]

    return pl.pallas_call(
        body, out_shape=tuple(land_shapes),
        grid_spec=pltpu.PrefetchScalarGridSpec(num_scalar_prefetch=1, grid=grid, in_specs=in_specs, out_specs=tuple(out_specs)),
        name=name, compiler_params=_params())(me, *srcs)


def _remote_copies(sets, src_refs, land_refs, send_sems, recv_sems, outgoing):
    me = _position()
    copies = []
    for t, (si, src_of, li, dst_of) in enumerate(sets):
        for k in range(1, N_DEV):
            other = _peer(me, k)
            sender = me if outgoing else other
            copies.append(pltpu.make_async_remote_copy(
                src_ref=src_of(src_refs[si], _flat(other)), dst_ref=dst_of(land_refs[li], _flat(sender), k),
                send_sem=send_sems.at[(N_DEV - 1) * t + k - 1], recv_sem=recv_sems.at[(N_DEV - 1) * t + k - 1],
                device_id=other, device_id_type=MESH))
    return copies


def _send_start(name, srcs, lands, sets_by_group):
    n_src, n_land, n_groups = len(srcs), len(lands), len(sets_by_group)

    def body(*refs):
        src_refs, land_refs = refs[:n_src], refs[n_src:n_src + n_land]
        outs = refs[n_src + n_land:]
        for g, sets in enumerate(sets_by_group):
            for cp in _remote_copies(sets, src_refs, land_refs, outs[2 * g], outs[2 * g + 1], True):
                cp.start()
        outs[-1][...] = jnp.zeros_like(outs[-1])

    sem_shapes = []
    for sets in sets_by_group:
        sem_shapes += [pltpu.SemaphoreType.DMA((len(sets) * (N_DEV - 1),))] * 2
    thru = [pltpu.HBM(a.shape, a.dtype) for a in list(srcs) + list(lands)]
    n_sem = len(sem_shapes)
    result = pl.pallas_call(
        body, out_shape=tuple(sem_shapes + thru + [jax.ShapeDtypeStruct((8, 128), F32)]),
        in_specs=[HBM] * (n_src + n_land),
        out_specs=tuple([SEM] * n_sem + [HBM] * (n_src + n_land) + [pl.BlockSpec(memory_space=pltpu.VMEM)]),
        input_output_aliases={i: n_sem + i for i in range(n_src + n_land)},
        compiler_params=pltpu.CompilerParams(has_side_effects=EFFECT), name=name,
    )(*[pltpu.with_memory_space_constraint(a, pltpu.HBM) for a in list(srcs) + list(lands)])
    sems = [(result[2 * g], result[2 * g + 1]) for g in range(n_groups)]
    return sems, list(result[n_sem:n_sem + n_src]), list(result[n_sem + n_src:n_sem + n_src + n_land]), result[-1]


def _send_wait(name, sems, srcs, lands, sets, after):
    n_src, n_land = len(srcs), len(lands)

    def body(*refs):
        src_refs, land_refs = refs[:n_src], refs[n_src:n_src + n_land]
        send_sems, recv_sems = refs[n_src + n_land], refs[n_src + n_land + 1]
        for cp in _remote_copies(sets, src_refs, land_refs, send_sems, recv_sems, True):
            cp.wait_send()
        for cp in _remote_copies(sets, src_refs, land_refs, send_sems, recv_sems, False):
            cp.wait_recv()

    thru = [pltpu.HBM(a.shape, a.dtype) for a in list(srcs) + list(lands)]
    result = pl.pallas_call(
        body, out_shape=tuple(thru), in_specs=[HBM] * (n_src + n_land) + [SEM, SEM] + [ANY] * len(after),
        out_specs=tuple([HBM] * (n_src + n_land)), input_output_aliases={i: i for i in range(n_src + n_land)},
        compiler_params=pltpu.CompilerParams(has_side_effects=EFFECT), name=name,
    )(*srcs, *lands, sems[0], sems[1], *after)
    return list(result[:n_src]), list(result[n_src:])


def _all_gather(name, locals_, out_shapes, windows):
    n = len(locals_)

    def body(*refs):
        src_refs, out_refs = refs[:n], refs[n:2 * n]
        send_sems, recv_sems, local_sems = refs[2 * n:]
        x, y, c = _position()
        me, sibling = (x, y, c), (x, y, 1 - c)
        chips = [(1 - x, y), (x, 1 - y), (1 - x, 1 - y)]

        def copy(t, k, block, to, from_local=False):
            dst = windows[t](out_refs[t], _flat(block))
            return pltpu.make_async_remote_copy(
                src_ref=src_refs[t] if from_local else dst, dst_ref=dst, send_sem=send_sems.at[t, k],
                recv_sem=recv_sems.at[t, k], device_id=to, device_id_type=MESH)

        mine = [pltpu.make_async_copy(src_refs[t], windows[t](out_refs[t], _flat(me)), local_sems.at[t]) for t in range(n)]
        sends = []
        for t in range(n):
            mine[t].start()
            sends.append(copy(t, 0, me, sibling, True))
            sends += [copy(t, 1 + j, me, (*chip, c), True) for j, chip in enumerate(chips)]
        for cp in sends:
            cp.start()
        for t in range(n):
            for j, chip in enumerate(chips):
                copy(t, 1 + j, (*chip, c), me).wait_recv()
                passed = copy(t, 4 + j, (*chip, c), sibling)
                passed.start()
                sends.append(passed)
        for t in range(n):
            copy(t, 0, sibling, me).wait_recv()
            for j, chip in enumerate(chips):
                copy(t, 4 + j, (*chip, 1 - c), me).wait_recv()
        for cp in sends:
            cp.wait_send()
        for cp in mine:
            cp.wait()

    return pl.pallas_call(
        body, out_shape=tuple(out_shapes), in_specs=[ANY] * n, out_specs=tuple([ANY] * n),
        scratch_shapes=[pltpu.SemaphoreType.DMA((n, 7)), pltpu.SemaphoreType.DMA((n, 7)), pltpu.SemaphoreType.DMA((n,))],
        name=name)(*locals_)


def _adamw(name, me, lands, owns, w, m, v, *, grid, land_specs, own_specs, p_spec):
    n_land = len(lands)

    def body(me_ref, *refs):
        land_refs, own_refs = refs[:n_land], refs[n_land:n_land + len(owns)]
        w_ref, m_ref, v_ref, g_ref, delta_ref, m_out, v_out = refs[n_land + len(owns):]
        ncols = w_ref.shape[-1]
        sums = []
        for i, land_ref in enumerate(land_refs):
            g = own_refs[i][...].astype(F32) if owns else land_ref[0].astype(F32)
            for s in range(0 if owns else 1, land_ref.shape[0]):
                g = g + land_ref[s].astype(F32)
            sums.append(g[:, :ncols])
        g = sums[0] if n_land == 1 else jnp.where(pl.program_id(0) == 0, sums[0], sums[1])
        m_new = ADAM_B1 * m_ref[...] + (1.0 - ADAM_B1) * g
        v_new = ADAM_B2 * v_ref[...] + (1.0 - ADAM_B2) * jnp.square(g)
        m_hat = m_new / (1.0 - ADAM_B1 ** ADAM_STEP)
        v_hat = v_new / (1.0 - ADAM_B2 ** ADAM_STEP)
        g_ref[...] = g
        delta_ref[...] = -ADAM_LR * (m_hat / (jnp.sqrt(v_hat) + ADAM_EPS) + ADAM_WD * w_ref[...])
        m_out[...] = m_new
        v_out[...] = v_new

    shape = jax.ShapeDtypeStruct(w.shape, F32)
    return pl.pallas_call(
        body, out_shape=(shape,) * 4,
        grid_spec=pltpu.PrefetchScalarGridSpec(
            num_scalar_prefetch=1, grid=grid, in_specs=list(land_specs) + list(own_specs) + [p_spec, p_spec, p_spec],
            out_specs=(p_spec,) * 4),
        name=name, compiler_params=_params())(me, *lands, *owns, w, m, v)


def _row(p, layer):
    return p[layer][None, :]


def _square(name, a, b, dims, out_dtype, deps=()):
    if a.shape == (D_MODEL, SEQ):
        return _matmul(name, a, b, grid=(2, 1), a_spec=pl.BlockSpec((512, SEQ), lambda i, k: (i, 0)),
                       b_spec=pl.BlockSpec((SEQ, D_MODEL), lambda i, k: (0, 0)),
                       o_spec=pl.BlockSpec((512, D_MODEL), lambda i, k: (i, 0)),
                       out_shape=jax.ShapeDtypeStruct((D_MODEL, D_MODEL), out_dtype), dims=NN, acc_shape=(8, 128),
                       deps=deps)
    return _matmul(name, a, b, grid=(SEQ // TM, 1), a_spec=pl.BlockSpec((TM, D_MODEL), lambda i, k: (i, 0)),
                   b_spec=pl.BlockSpec((D_MODEL, D_MODEL), lambda i, k: (0, 0)),
                   o_spec=pl.BlockSpec((TM, D_MODEL), lambda i, k: (i, 0)),
                   out_shape=jax.ShapeDtypeStruct((SEQ, D_MODEL), out_dtype), dims=dims, acc_shape=(8, 128), deps=deps)


def _qkv_fwd(name, hs, w, n_chunks):
    return _matmul(name, hs, w, grid=(n_chunks, SEQ // TM, 1),
                   a_spec=pl.BlockSpec((None, TM, D_MODEL), lambda j, i, k: (j // 3, i, 0)),
                   b_spec=pl.BlockSpec((D_MODEL, D_MODEL), lambda j, i, k: (0, j)),
                   o_spec=pl.BlockSpec((TM, D_MODEL), lambda j, i, k: (i, j)),
                   out_shape=jax.ShapeDtypeStruct((SEQ, n_chunks * D_MODEL), BF16), dims=NN, acc_shape=(8, 128))


def _qkv_dw(name, hs_t, dqkv, n_chunks):
    return _matmul(name, hs_t, dqkv, grid=(n_chunks, 2, 1),
                   a_spec=pl.BlockSpec((None, 512, SEQ), lambda j, i, k: (j // 3, i, 0)),
                   b_spec=pl.BlockSpec((None, SEQ, D_MODEL), lambda j, i, k: (j, 0, 0)),
                   o_spec=pl.BlockSpec((512, D_MODEL), lambda j, i, k: (i, j)),
                   out_shape=jax.ShapeDtypeStruct((D_MODEL, n_chunks * D_MODEL), BF16), dims=NN, acc_shape=(8, 128))


def _qkv_dh(name, dqkv, w, n_chunks, deps):
    return _matmul(name, dqkv, w, grid=(n_chunks // 3, SEQ // TM, 1),
                   a_spec=pl.BlockSpec((3, TM, D_MODEL), lambda g, i, k: (g, i, 0)),
                   b_spec=pl.BlockSpec((D_MODEL, 3 * D_MODEL), lambda g, i, k: (0, g)),
                   o_spec=pl.BlockSpec((None, TM, D_MODEL), lambda g, i, k: (g, i, 0)),
                   out_shape=jax.ShapeDtypeStruct((n_chunks // 3, SEQ, D_MODEL), F32), dims=NT, acc_shape=(8, 128),
                   deps=deps, inner=3)


def _local_step(x, target, norms, rpb, fetch, emit, deps):
    mix_pre, mix_post, ffn_pre, ffn_post = norms
    slopes = 2.0 ** (-8.0 * jnp.arange(1, N_HEADS + 1, dtype=F32) / N_HEADS)
    rpb_pad = jnp.pad(rpb, ((0, 0), (0, 1), (0, 128 - 31)))
    saved = []

    for layer in range(2):
        tag = f"l{layer}"
        h = _rms_fwd(tag + "_norm_mix", x, _row(mix_pre, layer), out_dtype=BF16, deps=deps if layer == 0 else ())
        if layer == 0:
            w_qkv, w_o = fetch("na", [h])
            table = _rpb_table(rpb_pad)
            qkv = _qkv_fwd(tag + "_qkv", h[None], w_qkv, 3)
            o, lse = _na_fwd(qkv, table)
            mixer = (h, qkv, o, lse, table)
        else:
            w_qkv, w_o = fetch("dil", [h])
            hs = jnp.stack([_sort_rows(h, dil) for _, dil in DIL_GROUPS])
            qkv = _qkv_fwd(tag + "_qkv", hs, w_qkv, 9)
            outs, lses, lses_sorted = [], [], []
            for g, (_, dil) in enumerate(DIL_GROUPS):
                o_g, lse_g = _dil_fwd(g, qkv, slopes * dil)
                outs.append(_unsort_rows(o_g, dil))
                lses.append(_unsort_rows(lse_g, dil))
                lses_sorted.append(lse_g)
            o, lse_total = _dil_merge(outs, lses)
            mixer = (hs, qkv, o, lses_sorted, lse_total)
        a = _square(tag + "_proj", o, w_o, NN, F32)
        x1 = _rms_fwd(tag + "_post_mix", a, _row(mix_post, layer), res=x)
        h2 = _rms_fwd(tag + "_norm_ffn", x1, _row(ffn_pre, layer), out_dtype=BF16)
        w_gu, w_down = fetch(f"ffn{layer}", [h2])
        act, hg, hu = _ffn_up(tag + "_ffn_up", h2, w_gu)
        f = _matmul(tag + "_ffn_down", act, w_down, grid=(SEQ // TM, N_DEV // 4),
                    a_spec=pl.BlockSpec((4, TM, FF_PAD), lambda i, k: (k, i, 0)),
                    b_spec=pl.BlockSpec((4, FF_PAD, D_MODEL), lambda i, k: (k, 0, 0)),
                    o_spec=pl.BlockSpec((TM, D_MODEL), lambda i, k: (i, 0)),
                    out_shape=jax.ShapeDtypeStruct((SEQ, D_MODEL), F32), dims=NN, acc_shape=(TM, D_MODEL), inner=4)
        x2 = _rms_fwd(tag + "_post_ffn", f, _row(ffn_post, layer), res=x1)
        transposed = (jnp.swapaxes(mixer[0], -1, -2), mixer[2].astype(BF16).T, h2.T, jnp.swapaxes(act, 1, 2))
        saved.append((x, mixer, a, x1, transposed, hg, hu, f, w_qkv, w_o, w_gu, w_down))
        x = x2

    dx, loss = _loss_head("loss_head", x, target)
    d_norm = {k: [None, None] for k in ("mix_pre", "mix_post", "ffn_pre", "ffn_post")}
    d_rpb = None

    for layer in (1, 0):
        tag = f"b{layer}"
        x0, mixer, a, x1, (h_t, o_t, h2_t, act_t), hg, hu, f, w_qkv, w_o, w_gu, w_down = saved[layer]
        d_f, d_norm["ffn_post"][layer] = _rms_bwd(tag + "_post_ffn", f, _row(ffn_post, layer), [dx], out_dtype=BF16)
        dgu = _ffn_bwd_act(tag + "_ffn_act", d_f, w_down, hg, hu)
        d_down = _matmul(
            tag + "_ffn_ddown", act_t, d_f, grid=(N_DEV, 1),
            a_spec=pl.BlockSpec((None, FF_PAD, SEQ), lambda d, k: (d, 0, 0)),
            b_spec=pl.BlockSpec((SEQ, D_MODEL), lambda d, k: (0, 0)),
            o_spec=pl.BlockSpec((None, FF_PAD, D_MODEL), lambda d, k: (d, 0, 0)),
            out_shape=jax.ShapeDtypeStruct((N_DEV, FF_PAD, D_MODEL), BF16), dims=NN, acc_shape=(8, 128))
        d_gu = _matmul(
            tag + "_ffn_dgu", h2_t, dgu, grid=(2, N_DEV, 1),
            a_spec=pl.BlockSpec((D_MODEL, SEQ), lambda t, d, k: (0, 0)),
            b_spec=pl.BlockSpec((None, None, SEQ, FF_PAD), lambda t, d, k: (t, d, 0, 0)),
            o_spec=pl.BlockSpec((None, None, D_MODEL, FF_PAD), lambda t, d, k: (d, t, 0, 0)),
            out_shape=jax.ShapeDtypeStruct((N_DEV, 2, D_MODEL, FF_PAD), BF16), dims=NN, acc_shape=(8, 128))
        sent = emit(f"ffn{layer}", [d_gu, d_down])
        d_h2 = _matmul(
            tag + "_ffn_dh", dgu, w_gu, grid=(SEQ // TM, 4),
            a_spec=pl.BlockSpec((None, 4, TM, FF_PAD), lambda i, k: (k // 2, k % 2, i, 0)),
            b_spec=pl.BlockSpec((4, None, D_MODEL, FF_PAD), lambda i, k: (k % 2, k // 2, 0, 0)),
            o_spec=pl.BlockSpec((TM, D_MODEL), lambda i, k: (i, 0)),
            out_shape=jax.ShapeDtypeStruct((SEQ, D_MODEL), F32), dims=NT, acc_shape=(TM, D_MODEL), deps=sent, inner=4)
        dx1, d_norm["ffn_pre"][layer] = _rms_bwd(tag + "_norm_ffn", x1, _row(ffn_pre, layer), [d_h2], res=dx)
        d_a, d_norm["mix_post"][layer] = _rms_bwd(tag + "_post_mix", a, _row(mix_post, layer), [dx1], out_dtype=BF16)
        d_wo = _square(tag + "_proj_dw", o_t, d_a, NN, BF16)
        if layer == 0:
            h, qkv, o, lse, table = mixer
            d_o = _square(tag + "_proj_do", d_a, w_o, NT, BF16)
            dqkv, gp = _na_bwd(qkv, table, d_o, lse)
            d_rpb = _rpb_grad(gp)[:, :15, :31]
            sent = emit("na", [_qkv_dw(tag + "_qkv_dw", h_t[None], dqkv, 3), d_wo])
            d_h = _qkv_dh(tag + "_qkv_dh", dqkv, w_qkv, 3, sent)
            d_hs = [d_h[0]]
        else:
            hs, qkv, o, lses, lse_total = mixer
            d_o = _square(tag + "_proj_do", d_a, w_o, NT, F32)
            parts = []
            for g, (_, dil) in enumerate(DIL_GROUPS):
                parts.append(_dil_bwd(g, qkv, slopes * dil, _sort_rows(d_o, dil), _sort_rows(o, dil), lses[g],
                                      _sort_rows(lse_total, dil)))
            dqkv = jnp.concatenate(parts, axis=0)
            sent = emit("dil", [_qkv_dw(tag + "_qkv_dw", h_t, dqkv, 9), d_wo])
            d_h = _qkv_dh(tag + "_qkv_dh", dqkv, w_qkv, 9, sent)
            d_hs = [_unsort_rows(d_h[g], dil) for g, (_, dil) in enumerate(DIL_GROUPS)]
        dx, d_norm["mix_pre"][layer] = _rms_bwd(tag + "_norm_mix", x0, _row(mix_pre, layer), d_hs, res=dx1)

    d_gains = [jnp.concatenate(d_norm[k], axis=0) for k in ("mix_pre", "mix_post", "ffn_pre", "ffn_post")]
    return loss, dx, d_gains, d_rpb


RPB_SIZE = N_HEADS * 15 * 31


def _pack_small(gains, rpb, last=None):
    top = jnp.concatenate(gains, axis=0).reshape(64, 128)
    bottom = jnp.pad(rpb.reshape(-1), (0, 64 * 128 - RPB_SIZE))
    if last is not None:
        bottom = bottom + jnp.pad(last.reshape(1), (64 * 128 - 1, 0))
    return jnp.concatenate([top, bottom.reshape(64, 128)], axis=0)


def _unpack_small(p):
    gains = p[:64].reshape(4, 2, D_MODEL)
    rpb = p[64:].reshape(-1)[:RPB_SIZE].reshape(1, N_HEADS, 15, 31)
    return [gains[i] for i in range(4)], rpb


GROUPS = ("na", "ffn0", "dil", "ffn1")


def kernel(x, norm_mix_pre, norm_mix_post, norm_ffn_pre, norm_ffn_post, na_w_qkv, na_w_o, na_rpb, dil_w_qkv, dil_w_o, ffn_w_gate, ffn_w_up, ffn_w_down, loss_target, m_norm_mix_pre, m_norm_mix_post, m_norm_ffn_pre, m_norm_ffn_post, m_na_w_qkv, m_na_w_o, m_na_rpb, m_dil_w_qkv, m_dil_w_o, m_ffn_w_gate, m_ffn_w_up, m_ffn_w_down, v_norm_mix_pre, v_norm_mix_post, v_norm_ffn_pre, v_norm_ffn_post, v_na_w_qkv, v_na_w_o, v_na_rpb, v_dil_w_qkv, v_dil_w_o, v_ffn_w_gate, v_ffn_w_up, v_ffn_w_down):
    na_cols, dil_cols, o_rows = 3 * D_MODEL // N_DEV, 9 * D_MODEL // N_DEV, D_MODEL // N_DEV
    ff_pad = FF_PAD - FF_SHARD
    me = (4 * lax.axis_index("x") + 2 * lax.axis_index("y") + lax.axis_index("c")).astype(jnp.int32).reshape(1)

    full = {
        "na": [((D_MODEL, 3 * D_MODEL), _columns(na_cols)), ((N_DEV, o_rows, D_MODEL), _leading)],
        "dil": [((D_MODEL, 9 * D_MODEL), _columns(dil_cols)), ((N_DEV, o_rows, D_MODEL), _leading)],
        "ffn0": [((N_DEV, 2, D_MODEL, FF_PAD), _leading), ((N_DEV, FF_PAD, D_MODEL), _leading)],
        "ffn1": [((N_DEV, 2, D_MODEL, FF_PAD), _leading), ((N_DEV, FF_PAD, D_MODEL), _leading)],
    }
    block = {
        "na": [(D_MODEL, na_cols), (o_rows, D_MODEL)], "dil": [(D_MODEL, dil_cols), (o_rows, D_MODEL)],
        "ffn0": [(2, D_MODEL, FF_PAD), (FF_PAD, D_MODEL)], "ffn1": [(2, D_MODEL, FF_PAD), (FF_PAD, D_MODEL)],
    }

    gu = jnp.pad(jnp.stack([ffn_w_gate, ffn_w_up], axis=1).astype(BF16), ((0, 0), (0, 0), (0, 0), (0, ff_pad)))
    down = jnp.pad(ffn_w_down.astype(BF16), ((0, 0), (0, ff_pad), (0, 0)))
    shards = {"na": [na_w_qkv[0].astype(BF16), na_w_o[0].astype(BF16)],
              "dil": [dil_w_qkv[0].astype(BF16), dil_w_o[0].astype(BF16)],
              "ffn0": [gu[0], down[0]], "ffn1": [gu[1], down[1]]}
    srcs = [s for g in GROUPS for s in shards[g]]
    gather_sets = {g: [(2 * gi + t, _whole, 2 * gi + t, _by_sender(full[g][t][1])) for t in range(2)]
                   for gi, g in enumerate(GROUPS)}
    land_shapes = [jax.ShapeDtypeStruct(full[g][t][0], BF16) for g in GROUPS for t in range(2)]

    def own_qkv(width):
        return pl.BlockSpec((256, width), lambda i, me: (i, 0)), pl.BlockSpec((256, width), lambda i, me: (i, me[0]))

    own_o = (pl.BlockSpec((32, D_MODEL), lambda i, me: (i, 0)), pl.BlockSpec((None, 32, D_MODEL), lambda i, me: (me[0], i, 0)))
    own_gu = (pl.BlockSpec((2, 256, FF_PAD), lambda i, me: (0, i, 0)),
              pl.BlockSpec((None, 2, 256, FF_PAD), lambda i, me: (me[0], 0, i, 0)))
    own_down = (pl.BlockSpec((96, D_MODEL), lambda i, me: (i, 0)), pl.BlockSpec((None, 96, D_MODEL), lambda i, me: (me[0], i, 0)))
    own = {"na": [own_qkv(na_cols), own_o], "dil": [own_qkv(dil_cols), own_o],
           "ffn0": [own_gu, own_down], "ffn1": [own_gu, own_down]}
    lands = _place_own("gather_own", me, srcs, land_shapes, [own[g][t][0] for g in GROUPS for t in range(2)],
                       [own[g][t][1] for g in GROUPS for t in range(2)], (4,))
    sems, srcs, lands, token = _send_start("gather_start", srcs, lands, [gather_sets[g] for g in GROUPS])

    def fetch(group, after):
        gi = GROUPS.index(group)
        local_sets = [(t, _whole, t, _by_sender(full[group][t][1])) for t in range(2)]
        _, (qkv, o) = _send_wait(f"gather_wait_{group}", sems[gi], srcs[2 * gi:2 * gi + 2], lands[2 * gi:2 * gi + 2],
                                 local_sets, after)
        return (qkv, o.reshape(D_MODEL, D_MODEL)) if group in ("na", "dil") else (qkv, o)

    def grad_source(group, t):
        return _columns(block[group][0][1]) if (group in ("na", "dil") and t == 0) else _leading

    in_flight = {}

    def emit(group, grads):
        if group in ("na", "dil"):
            grads = [grads[0], grads[1].reshape(N_DEV, o_rows, D_MODEL)]
        sets = [(t, grad_source(group, t), t, _by_distance) for t in range(2)]
        landing = [lax.empty((N_DEV - 1,) + block[group][t], BF16) for t in range(2)]
        sems_g, grads, landing, tok = _send_start(f"exchange_start_{group}", grads, landing, [sets])
        in_flight[group] = (sems_g[0], grads, landing, sets)
        return [tok]

    norms = (norm_mix_pre, norm_mix_post, norm_ffn_pre, norm_ffn_post)
    loss, grad_x, d_gains, d_rpb = _local_step(x[0], loss_target[0], norms, na_rpb[0], fetch, emit, [token])
    small = _all_gather("gather_small", [_pack_small(d_gains, d_rpb, loss)], [jax.ShapeDtypeStruct((N_DEV, 128, 128), F32)],
                        [_leading])[0]

    landed, sent = {}, {}
    for group in ("ffn1", "dil", "ffn0", "na"):
        sems_g, grads, landing, sets = in_flight[group]
        sent[group], landed[group] = _send_wait(f"exchange_wait_{group}", sems_g, grads, landing, sets, [grad_x])

    def one(rows, tile, ncols, columns):
        own = (pl.BlockSpec((tile, ncols), lambda i, me: (i, me[0])) if columns
               else pl.BlockSpec((None, tile, ncols), lambda i, me: (me[0], i, 0)))
        return dict(grid=(rows // tile,), land_specs=[pl.BlockSpec((N_DEV - 1, tile, ncols), lambda i, me: (0, i, 0))],
                    own_specs=[own], p_spec=pl.BlockSpec((None, tile, ncols), lambda i, me: (0, i, 0)))

    def layered(block_shape, index, p_block, n_tiles):
        def specs(lead_size, lead):
            shape = (lead_size,) + block_shape
            return [pl.BlockSpec(shape, lambda l, r, me: index(lead(me), jnp.where(l == 0, r, n_tiles - 1))),
                    pl.BlockSpec(shape, lambda l, r, me: index(lead(me), jnp.where(l == 0, 0, r)))]
        return dict(grid=(2, n_tiles), land_specs=specs(N_DEV - 1, lambda me: 0), own_specs=specs(None, lambda me: me[0]),
                    p_spec=pl.BlockSpec(p_block, lambda l, r, me: (l, r, 0)))

    gu_lands, gu_owns = [landed["ffn0"][0], landed["ffn1"][0]], [sent["ffn0"][0], sent["ffn1"][0]]
    down_lands, down_owns = [landed["ffn0"][1], landed["ffn1"][1]], [sent["ffn0"][1], sent["ffn1"][1]]
    updates = {
        "na_w_qkv": _adamw("adamw_na_qkv", me, [landed["na"][0]], [sent["na"][0]], na_w_qkv, m_na_w_qkv, v_na_w_qkv,
                           **one(D_MODEL, 256, na_cols, True)),
        "na_w_o": _adamw("adamw_na_o", me, [landed["na"][1]], [sent["na"][1]], na_w_o, m_na_w_o, v_na_w_o,
                         **one(o_rows, o_rows, D_MODEL, False)),
        "dil_w_qkv": _adamw("adamw_dil_qkv", me, [landed["dil"][0]], [sent["dil"][0]], dil_w_qkv, m_dil_w_qkv, v_dil_w_qkv,
                            **one(D_MODEL, 128, dil_cols, True)),
        "dil_w_o": _adamw("adamw_dil_o", me, [landed["dil"][1]], [sent["dil"][1]], dil_w_o, m_dil_w_o, v_dil_w_o,
                          **one(o_rows, o_rows, D_MODEL, False)),
        "ffn_w_gate": _adamw("adamw_gate", me, gu_lands, gu_owns, ffn_w_gate, m_ffn_w_gate, v_ffn_w_gate,
                             **layered((None, 128, FF_PAD), lambda lead, r: (lead, 0, r, 0), (None, 128, FF_SHARD), 8)),
        "ffn_w_up": _adamw("adamw_up", me, gu_lands, gu_owns, ffn_w_up, m_ffn_w_up, v_ffn_w_up,
                           **layered((None, 128, FF_PAD), lambda lead, r: (lead, 1, r, 0), (None, 128, FF_SHARD), 8)),
        "ffn_w_down": _adamw("adamw_down", me, down_lands, down_owns, ffn_w_down, m_ffn_w_down, v_ffn_w_down,
                             **layered((176, D_MODEL), lambda lead, r: (lead, r, 0), (None, 176, D_MODEL), 2)),
    }
    gains = [norm_mix_pre, norm_mix_post, norm_ffn_pre, norm_ffn_post]
    m_gains = [m_norm_mix_pre, m_norm_mix_post, m_norm_ffn_pre, m_norm_ffn_post]
    v_gains = [v_norm_mix_pre, v_norm_mix_post, v_norm_ffn_pre, v_norm_ffn_post]
    packed = _adamw("adamw_small", me, [small], (), _pack_small(gains, na_rpb)[None], _pack_small(m_gains, m_na_rpb)[None],
                    _pack_small(v_gains, v_na_rpb)[None], grid=(1,),
                    land_specs=[pl.BlockSpec((N_DEV, 128, 128), lambda i, me: (0, 0, 0))], own_specs=[],
                    p_spec=pl.BlockSpec((None, 128, 128), lambda i, me: (0, 0, 0)))
    small_out = [_unpack_small(p[0]) for p in packed]

    order = ["na_w_qkv", "na_w_o", "na_rpb", "dil_w_qkv", "dil_w_o", "ffn_w_gate", "ffn_w_up", "ffn_w_down"]
    result = [packed[0][0, 127, 127], grad_x[None]]
    for kind in range(4):
        gains_k, rpb_k = small_out[kind]
        result += gains_k
        result += [rpb_k if name == "na_rpb" else updates[name][kind] for name in order]
    return tuple(result)
```

```python
import functools

import jax
import jax.numpy as jnp
from jax import lax
from jax.experimental import pallas as pl
from jax.experimental.pallas import tpu as pltpu

F32 = jnp.float32
BF16 = jnp.bfloat16
MESH = pl.DeviceIdType.MESH
ANY = pl.BlockSpec(memory_space=pl.ANY)
HBM = pl.BlockSpec(memory_space=pltpu.HBM)
SEM = pl.BlockSpec(memory_space=pltpu.SEMAPHORE)
EFFECT = pltpu.SideEffectType.DATAFLOW_SIDE_EFFECTING

N_DEV = 8
SEQ = 2048
D_MODEL = 1024
N_HEADS = 16
HEAD_DIM = 64
GRID_W = 64
NA_ROWS = 8
SEQ_ROWS = SEQ // GRID_W
DIL_GROUPS = ((128, 1), (512, 4), (2048, 16))
BAND = 128
RADIUS = 64
FF_SHARD = 352
FF_PAD = 384
RMS_EPS = 1e-6
NEG_INF = -1e30
Q_SCALE = HEAD_DIM ** -0.5

ADAM_LR = 0.001
ADAM_B1 = 0.9
ADAM_B2 = 0.999
ADAM_EPS = 1e-08
ADAM_WD = 0.01
ADAM_STEP = 10

VMEM_LIMIT = 56 * 1024 * 1024
TM = 512

NN = (((1,), (0,)), ((), ()))
NT = (((1,), (1,)), ((), ()))
TN = (((0,), (0,)), ((), ()))


def _params():
    return pltpu.CompilerParams(vmem_limit_bytes=VMEM_LIMIT)


def _matmul(name, a, b, *, grid, a_spec, b_spec, o_spec, out_shape, dims, acc_shape, deps=(), inner=1):
    nk = grid[-1]
    kaxis = len(grid) - 1

    def body(a_ref, b_ref, *rest):
        o_ref, acc_ref = rest[-2], rest[-1]
        if inner == 1:
            part = lax.dot_general(a_ref[...].astype(BF16), b_ref[...].astype(BF16), dims, preferred_element_type=F32)
        elif len(b_ref.shape) == 2:
            a_all = jnp.concatenate([a_ref[j].astype(BF16) for j in range(inner)], axis=1)
            part = lax.dot_general(a_all, b_ref[...].astype(BF16), dims, preferred_element_type=F32)
        else:
            part = sum(lax.dot_general(a_ref[j].astype(BF16), b_ref[j].astype(BF16), dims, preferred_element_type=F32)
                       for j in range(inner))
        if nk == 1:
            o_ref[...] = part.astype(o_ref.dtype)
        else:
            k = pl.program_id(kaxis)

            @pl.when(k == 0)
            def _():
                acc_ref[...] = part

            @pl.when(k > 0)
            def _():
                acc_ref[...] += part

            @pl.when(k == nk - 1)
            def _():
                o_ref[...] = acc_ref[...].astype(o_ref.dtype)

    return pl.pallas_call(
        body, out_shape=out_shape, grid=grid, in_specs=[a_spec, b_spec] + [ANY] * len(deps), out_specs=o_spec,
        scratch_shapes=[pltpu.VMEM(acc_shape, F32)], name=name, compiler_params=_params())(a, b, *deps)


def _rms_fwd(name, x, g, res=None, out_dtype=F32, deps=()):
    n_tiles = SEQ // TM
    has_res = res is not None

    def body(*refs):
        x_ref, g_ref = refs[0], refs[1]
        o_ref = refs[-1]
        xv = x_ref[...]
        r = lax.rsqrt(jnp.mean(xv * xv, axis=-1, keepdims=True) + RMS_EPS)
        y = xv * r * g_ref[...]
        if has_res:
            y = refs[2][...] + y
        o_ref[...] = y.astype(o_ref.dtype)

    tile = pl.BlockSpec((TM, D_MODEL), lambda i: (i, 0))
    gspec = pl.BlockSpec((1, D_MODEL), lambda i: (0, 0))
    ins = [x, g] + ([res] if has_res else []) + list(deps)
    specs = [tile, gspec] + ([tile] if has_res else []) + [ANY] * len(deps)
    return pl.pallas_call(
        body, out_shape=jax.ShapeDtypeStruct((SEQ, D_MODEL), out_dtype), grid=(n_tiles,), in_specs=specs,
        out_specs=tile, name=name, compiler_params=_params())(*ins)


def _rms_bwd(name, x, g, dys, res=None, out_dtype=F32):
    n_tiles = SEQ // TM
    n_dy = len(dys)
    has_res = res is not None

    def body(*refs):
        x_ref, g_ref = refs[0], refs[1]
        dy_refs = refs[2:2 + n_dy]
        res_ref = refs[2 + n_dy] if has_res else None
        dx_ref, dg_ref, acc_ref = refs[-3], refs[-2], refs[-1]
        i = pl.program_id(0)
        xv = x_ref[...]
        r = lax.rsqrt(jnp.mean(xv * xv, axis=-1, keepdims=True) + RMS_EPS)
        xn = xv * r
        dy = dy_refs[0][...].astype(F32)
        for extra in dy_refs[1:]:
            dy = dy + extra[...].astype(F32)
        dyg = dy * g_ref[...]
        dx = r * (dyg - xn * jnp.mean(dyg * xn, axis=-1, keepdims=True))
        if has_res:
            dx = res_ref[...] + dx
        dx_ref[...] = dx.astype(dx_ref.dtype)
        part = jnp.sum((dy * xn).reshape(TM // 8, 8, D_MODEL), axis=0)

        @pl.when(i == 0)
        def _():
            acc_ref[...] = part

        @pl.when(i > 0)
        def _():
            acc_ref[...] += part

        @pl.when(i == n_tiles - 1)
        def _():
            dg_ref[...] = jnp.broadcast_to(jnp.sum(acc_ref[...], axis=0, keepdims=True), (8, D_MODEL))

    tile = pl.BlockSpec((TM, D_MODEL), lambda i: (i, 0))
    gspec = pl.BlockSpec((1, D_MODEL), lambda i: (0, 0))
    ins = [x, g] + list(dys) + ([res] if has_res else [])
    specs = [tile, gspec] + [tile] * n_dy + ([tile] if has_res else [])
    dx, dg = pl.pallas_call(
        body, out_shape=(jax.ShapeDtypeStruct((SEQ, D_MODEL), out_dtype), jax.ShapeDtypeStruct((8, D_MODEL), F32)),
        grid=(n_tiles,), in_specs=specs,
        out_specs=(tile, pl.BlockSpec((8, D_MODEL), lambda i: (0, 0))),
        scratch_shapes=[pltpu.VMEM((8, D_MODEL), F32)], name=name, compiler_params=_params())(*ins)
    return dx, dg[0:1]


def _loss_head(name, y, target):
    n_tiles = SEQ // TM

    def body(y_ref, t_ref, dy_ref, loss_ref, acc_ref):
        i = pl.program_id(0)
        diff = y_ref[...] - t_ref[...]
        dy_ref[...] = diff * (1.0 / D_MODEL)
        part = jnp.sum((diff * diff).reshape(TM // 8, 8, D_MODEL), axis=0)

        @pl.when(i == 0)
        def _():
            acc_ref[...] = part

        @pl.when(i > 0)
        def _():
            acc_ref[...] += part

        @pl.when(i == n_tiles - 1)
        def _():
            loss_ref[...] = jnp.full((8, 128), jnp.sum(acc_ref[...]) * (0.5 / D_MODEL), F32)

    tile = pl.BlockSpec((TM, D_MODEL), lambda i: (i, 0))
    dy, loss = pl.pallas_call(
        body, out_shape=(jax.ShapeDtypeStruct((SEQ, D_MODEL), F32), jax.ShapeDtypeStruct((8, 128), F32)),
        grid=(n_tiles,), in_specs=[tile, tile], out_specs=(tile, pl.BlockSpec((8, 128), lambda i: (0, 0))),
        scratch_shapes=[pltpu.VMEM((8, D_MODEL), F32)], name=name, compiler_params=_params())(y, target)
    return dy, loss[0, 0]


def _row_index(shape):
    return lax.broadcasted_iota(jnp.int32, shape, 0)


def _lane_index(shape):
    return lax.broadcasted_iota(jnp.int32, shape, len(shape) - 1)


def _skew_rows(t, direction):
    q = _row_index(t.shape) & (GRID_W - 1)
    for bit in range(6):
        step = 1 << bit
        shift = step if direction > 0 else 128 - step
        t = jnp.where((q & step) != 0, pltpu.roll(t, shift, 1), t)
    return t


def _rpb_table(rpb_pad):
    rows = 16 * GRID_W

    def body(r_ref, t_ref):
        lane = _lane_index((rows, 128))
        v = pltpu.roll(r_ref[...], 128 - 15, 1)
        t = _skew_rows(jnp.broadcast_to(v[:, None, :], (16, GRID_W, 128)).reshape(rows, 128), +1)
        t = jnp.where(lane < GRID_W, t, 0.0)
        below = jnp.concatenate([t[GRID_W:], jnp.zeros((GRID_W, 128), F32)], axis=0)
        t_ref[...] = (t + pltpu.roll(below, GRID_W, 1)).reshape(16, GRID_W, 128)

    return pl.pallas_call(
        body, out_shape=jax.ShapeDtypeStruct((N_HEADS, 16, GRID_W, 128), F32), grid=(N_HEADS,),
        in_specs=[pl.BlockSpec((None, 16, 128), lambda h: (h, 0, 0))],
        out_specs=pl.BlockSpec((None, 16, GRID_W, 128), lambda h: (h, 0, 0, 0)),
        name="rpb_table", compiler_params=_params())(rpb_pad)


def _rpb_grad(gp):
    rows = 16 * GRID_W

    def body(g_ref, o_ref):
        lane = _lane_index((rows, 128))
        g = g_ref[...].reshape(rows, 128)
        low = jnp.where(lane < GRID_W, g, 0.0)
        high = pltpu.roll(jnp.where(lane >= GRID_W, g, 0.0), GRID_W, 1)
        above = jnp.concatenate([jnp.zeros((GRID_W, 128), F32), high[:rows - GRID_W]], axis=0)
        diag = jnp.sum(_skew_rows(low + above, -1).reshape(16, GRID_W, 128), axis=1)
        o_ref[...] = pltpu.roll(diag, 15, 1)

    return pl.pallas_call(
        body, out_shape=jax.ShapeDtypeStruct((N_HEADS, 16, 128), F32), grid=(N_HEADS,),
        in_specs=[pl.BlockSpec((None, 16, GRID_W, 128), lambda h: (h, 0, 0, 0))],
        out_specs=pl.BlockSpec((None, 16, 128), lambda h: (h, 0, 0)),
        name="rpb_grad", compiler_params=_params())(gp)


NA_KEYS = NA_ROWS * GRID_W


def _na_window(i):
    first_row = jnp.clip(i - NA_ROWS // 2, 0, SEQ_ROWS - NA_ROWS)
    return pl.multiple_of(first_row * GRID_W, GRID_W), first_row - i + NA_ROWS - 1


NA_STEP = 4


def _na_valid():
    q = _row_index((2 * GRID_W, NA_KEYS)) & (GRID_W - 1)
    k = _lane_index((2 * GRID_W, NA_KEYS)) & (GRID_W - 1)
    first_col = jnp.clip(q - 8, 0, GRID_W - 16)
    return (k >= first_col) & (k < first_col + 16)


def _head_masks():
    lane = _lane_index((1, 128))
    return (lane < HEAD_DIM, lane >= HEAD_DIM)


def _stack_heads(t, masks):
    zero = jnp.zeros_like(t)
    return jnp.concatenate([jnp.where(masks[0], t, zero), jnp.where(masks[1], t, zero)], axis=0)


def _unstack_heads(t, masks):
    n = t.shape[0] // 2
    return jnp.where(masks[0], t[:n], t[n:])


def _stack_columns(t):
    return jnp.concatenate([t[:, 0:1], t[:, HEAD_DIM:HEAD_DIM + 1]], axis=0)


def _na_scores(qs, kw, tp_ref, dr0, valid):
    s = lax.dot_general(qs, kw, NT, preferred_element_type=F32)
    bias = jnp.concatenate(
        [jnp.concatenate([tp_ref[a, pl.ds(dr0 + 2 * c, 1), :, :].reshape(GRID_W, 128) for c in range(4)], axis=1)
         for a in range(2)], axis=0)
    return jnp.where(valid, s + bias, NEG_INF)


def _na_specs():
    q_spec = pl.BlockSpec((NA_STEP * GRID_W, 128), lambda hp, i: (i, hp))
    k_spec = pl.BlockSpec((SEQ, 128), lambda hp, i: (0, 8 + hp))
    v_spec = pl.BlockSpec((SEQ, 128), lambda hp, i: (0, 16 + hp))
    tp_spec = pl.BlockSpec((2, 16, GRID_W, 128), lambda hp, i: (hp, 0, 0, 0))
    return q_spec, k_spec, v_spec, tp_spec


def _na_fwd(qkv, table):
    def body(q_ref, k_ref, v_ref, tp_ref, o_ref, lse_ref):
        valid = _na_valid()
        masks = _head_masks()
        for r in range(NA_STEP):
            rows = slice(r * GRID_W, (r + 1) * GRID_W)
            start, dr0 = _na_window(pl.program_id(1) * NA_STEP + r)
            kw = k_ref[pl.ds(start, NA_KEYS), :]
            vw = v_ref[pl.ds(start, NA_KEYS), :]
            s = _na_scores(_stack_heads(q_ref[rows, :] * Q_SCALE, masks), kw, tp_ref, dr0, valid)
            m = jnp.max(s, axis=-1, keepdims=True)
            p = jnp.exp(s - m)
            denom = jnp.sum(p, axis=-1, keepdims=True)
            out = jnp.dot(p.astype(BF16), vw, preferred_element_type=F32) / denom
            o_ref[rows, :] = _unstack_heads(out, masks).astype(o_ref.dtype)
            lse_ref[rows, :] = _unstack_heads(jnp.broadcast_to(m + jnp.log(denom), (2 * GRID_W, 128)), masks)

    q_spec, k_spec, v_spec, tp_spec = _na_specs()
    return pl.pallas_call(
        body, out_shape=(jax.ShapeDtypeStruct((SEQ, D_MODEL), BF16), jax.ShapeDtypeStruct((SEQ, D_MODEL), F32)),
        grid=(N_HEADS // 2, SEQ_ROWS // NA_STEP), in_specs=[q_spec, k_spec, v_spec, tp_spec],
        out_specs=(q_spec, q_spec), name="na_fwd", compiler_params=_params())(qkv, qkv, qkv, table)


def _na_bwd(qkv, table, d_out, lse):
    def body(q_ref, k_ref, v_ref, tp_ref, do_ref, lse_ref, dqkv_ref, gp_ref, dk_acc, dv_acc):
        step = pl.program_id(1)

        @pl.when(step == 0)
        def _():
            dk_acc[...] = jnp.zeros_like(dk_acc)
            dv_acc[...] = jnp.zeros_like(dv_acc)
            gp_ref[...] = jnp.zeros_like(gp_ref)

        valid = _na_valid()
        masks = _head_masks()
        for r in range(NA_STEP):
            rows = slice(r * GRID_W, (r + 1) * GRID_W)
            i = step * NA_STEP + r
            start, dr0 = _na_window(i)
            kw = k_ref[pl.ds(start, NA_KEYS), :]
            vw = v_ref[pl.ds(start, NA_KEYS), :]
            qs = _stack_heads(q_ref[rows, :] * Q_SCALE, masks)
            dos = _stack_heads(do_ref[rows, :], masks)
            p = jnp.exp(_na_scores(qs, kw, tp_ref, dr0, valid) - _stack_columns(lse_ref[rows, :]))
            dp = lax.dot_general(dos, vw, NT, preferred_element_type=F32)
            ds = p * (dp - jnp.sum(p * dp, axis=-1, keepdims=True))
            for a in range(2):
                for c in range(4):
                    gp_ref[a, pl.ds(dr0 + 2 * c, 1), :, :] += (
                        ds[a * GRID_W:(a + 1) * GRID_W, 128 * c:128 * (c + 1)].reshape(1, GRID_W, 128))
            dsb = ds.astype(BF16)
            dq = _unstack_heads(jnp.dot(dsb, kw, preferred_element_type=F32), masks) * Q_SCALE
            dqkv_ref[0, pl.ds(pl.multiple_of(i * GRID_W, GRID_W), GRID_W), :] = dq.astype(dqkv_ref.dtype)
            dk_acc[pl.ds(start, NA_KEYS), :] += lax.dot_general(dsb, qs, TN, preferred_element_type=F32)
            dv_acc[pl.ds(start, NA_KEYS), :] += lax.dot_general(p.astype(BF16), dos, TN, preferred_element_type=F32)

        @pl.when(step == SEQ_ROWS // NA_STEP - 1)
        def _():
            dqkv_ref[1] = dk_acc[...].astype(dqkv_ref.dtype)
            dqkv_ref[2] = dv_acc[...].astype(dqkv_ref.dtype)

    q_spec, k_spec, v_spec, tp_spec = _na_specs()
    return pl.pallas_call(
        body,
        out_shape=(jax.ShapeDtypeStruct((3, SEQ, D_MODEL), BF16), jax.ShapeDtypeStruct((N_HEADS, 16, GRID_W, 128), F32)),
        grid=(N_HEADS // 2, SEQ_ROWS // NA_STEP), in_specs=[q_spec, k_spec, v_spec, tp_spec, q_spec, q_spec],
        out_specs=(pl.BlockSpec((3, SEQ, 128), lambda hp, i: (0, 0, hp)), tp_spec),
        scratch_shapes=[pltpu.VMEM((SEQ, 128), F32), pltpu.VMEM((SEQ, 128), F32)],
        name="na_bwd", compiler_params=_params())(qkv, qkv, qkv, table, d_out, lse)


DIL_STEP = 4


def _dil_geometry(group):
    dil = DIL_GROUPS[group][1]
    sub_len = SEQ // dil
    blocks = sub_len // BAND
    return dil, sub_len, max(blocks // DIL_STEP, 1), max(DIL_STEP // blocks, 1), min(2 * BAND, sub_len)


def _dil_block(step, r, sub_len, subs):
    per_sub = DIL_STEP // subs
    return (r // per_sub) * sub_len, step * per_sub + r % per_sub


def _dil_window(b, sub_len, n_keys):
    if n_keys == sub_len:
        return 0
    return pl.multiple_of(jnp.clip(b * BAND - RADIUS, 0, sub_len - n_keys), RADIUS)


def _dil_bias(b, start, n_keys, slope_ref, hp):
    row = _row_index((2 * BAND, n_keys))
    qpos = b * BAND + (row & (BAND - 1))
    kpos = start + _lane_index((2 * BAND, n_keys))
    dist = jnp.abs(qpos - kpos)
    slope = jnp.where(row < BAND, slope_ref[2 * hp], slope_ref[2 * hp + 1])
    return slope * dist.astype(F32), dist <= RADIUS


def _dil_scores(qs, kw, penalty, valid):
    return jnp.where(valid, lax.dot_general(qs, kw, NT, preferred_element_type=F32) - penalty, NEG_INF)


def _dil_specs(group):
    dil, sub_len, steps, subs, _ = _dil_geometry(group)
    col = group * 24
    rows = DIL_STEP * BAND
    q_spec = pl.BlockSpec((rows, 128), lambda n, hp, b: (n * steps + b, col + hp))
    k_spec = pl.BlockSpec((subs * sub_len, 128), lambda n, hp, b: (n, col + 8 + hp))
    v_spec = pl.BlockSpec((subs * sub_len, 128), lambda n, hp, b: (n, col + 16 + hp))
    tile = pl.BlockSpec((rows, 128), lambda n, hp, b: (n * steps + b, hp))
    smem = pl.BlockSpec(memory_space=pltpu.SMEM)
    return (dil // subs, N_HEADS // 2, steps), q_spec, k_spec, v_spec, tile, smem


def _dil_fwd(group, qkv, slopes):
    _, sub_len, _, subs, n_keys = _dil_geometry(group)

    def body(q_ref, k_ref, v_ref, slope_ref, o_ref, lse_ref):
        hp = pl.program_id(1)
        masks = _head_masks()
        for r in range(DIL_STEP):
            rows = slice(r * BAND, (r + 1) * BAND)
            base, b = _dil_block(pl.program_id(2), r, sub_len, subs)
            start = _dil_window(b, sub_len, n_keys)
            kw = k_ref[pl.ds(base + start, n_keys), :]
            vw = v_ref[pl.ds(base + start, n_keys), :]
            penalty, valid = _dil_bias(b, start, n_keys, slope_ref, hp)
            s = _dil_scores(_stack_heads(q_ref[rows, :] * Q_SCALE, masks), kw, penalty, valid)
            m = jnp.max(s, axis=-1, keepdims=True)
            p = jnp.exp(s - m)
            denom = jnp.sum(p, axis=-1, keepdims=True)
            out = jnp.dot(p.astype(BF16), vw, preferred_element_type=F32) / denom
            o_ref[rows, :] = _unstack_heads(out, masks)
            lse_ref[rows, :] = _unstack_heads(jnp.broadcast_to(m + jnp.log(denom), (2 * BAND, 128)), masks)

    grid, q_spec, k_spec, v_spec, tile, smem = _dil_specs(group)
    return pl.pallas_call(
        body, out_shape=(jax.ShapeDtypeStruct((SEQ, D_MODEL), F32), jax.ShapeDtypeStruct((SEQ, D_MODEL), F32)),
        grid=grid, in_specs=[q_spec, k_spec, v_spec, smem], out_specs=(tile, tile),
        name=f"dil_fwd_{group}", compiler_params=_params())(qkv, qkv, qkv, slopes)


def _dil_merge(outs, lses):
    def body(o0, o1, o2, l0, l1, l2, o_ref, lse_ref):
        ls = [l0[...], l1[...], l2[...]]
        m = jnp.maximum(jnp.maximum(ls[0], ls[1]), ls[2])
        es = [jnp.exp(v - m) for v in ls]
        total = es[0] + es[1] + es[2]
        o_ref[...] = (es[0] * o0[...] + es[1] * o1[...] + es[2] * o2[...]) / total
        lse_ref[...] = m + jnp.log(total)

    tile = pl.BlockSpec((TM, D_MODEL), lambda i: (i, 0))
    return pl.pallas_call(
        body, out_shape=(jax.ShapeDtypeStruct((SEQ, D_MODEL), F32), jax.ShapeDtypeStruct((SEQ, D_MODEL), F32)),
        grid=(SEQ // TM,), in_specs=[tile] * 6, out_specs=(tile, tile), name="dil_merge",
        compiler_params=_params())(*outs, *lses)


def _dil_bwd(group, qkv, slopes, d_out, out, lse_group, lse_total):
    _, sub_len, steps, subs, n_keys = _dil_geometry(group)

    def body(q_ref, k_ref, v_ref, slope_ref, do_ref, o_ref, lg_ref, lt_ref, dqkv_ref, dk_acc, dv_acc):
        hp, step = pl.program_id(1), pl.program_id(2)

        @pl.when(step == 0)
        def _():
            dk_acc[...] = jnp.zeros_like(dk_acc)
            dv_acc[...] = jnp.zeros_like(dv_acc)

        masks = _head_masks()
        for r in range(DIL_STEP):
            rows = slice(r * BAND, (r + 1) * BAND)
            base, b = _dil_block(step, r, sub_len, subs)
            start = _dil_window(b, sub_len, n_keys)
            keys = pl.ds(base + start, n_keys)
            kw = k_ref[keys, :]
            vw = v_ref[keys, :]
            penalty, valid = _dil_bias(b, start, n_keys, slope_ref, hp)
            qs = _stack_heads(q_ref[rows, :] * Q_SCALE, masks)
            lse2 = lg_ref[rows, :]
            weight = jnp.exp(lse2 - lt_ref[rows, :])
            do2 = do_ref[rows, :]
            dogs = _stack_heads((weight * do2).astype(BF16), masks)
            delta = _stack_columns(weight) * jnp.sum(_stack_heads(do2 * o_ref[rows, :], masks), axis=-1, keepdims=True)
            p = jnp.exp(_dil_scores(qs, kw, penalty, valid) - _stack_columns(lse2))
            dp = lax.dot_general(dogs, vw, NT, preferred_element_type=F32)
            dsb = (p * (dp - delta)).astype(BF16)
            dq = _unstack_heads(jnp.dot(dsb, kw, preferred_element_type=F32), masks) * Q_SCALE
            dqkv_ref[0, pl.ds(pl.multiple_of(base + b * BAND, BAND), BAND), :] = dq.astype(dqkv_ref.dtype)
            dk_acc[keys, :] += lax.dot_general(dsb, qs, TN, preferred_element_type=F32)
            dv_acc[keys, :] += lax.dot_general(p.astype(BF16), dogs, TN, preferred_element_type=F32)

        @pl.when(step == steps - 1)
        def _():
            dqkv_ref[1] = dk_acc[...].astype(dqkv_ref.dtype)
            dqkv_ref[2] = dv_acc[...].astype(dqkv_ref.dtype)

    grid, q_spec, k_spec, v_spec, tile, smem = _dil_specs(group)
    return pl.pallas_call(
        body, out_shape=jax.ShapeDtypeStruct((3, SEQ, D_MODEL), BF16), grid=grid,
        in_specs=[q_spec, k_spec, v_spec, smem, tile, tile, tile, tile],
        out_specs=pl.BlockSpec((3, subs * sub_len, 128), lambda n, hp, b: (0, n, hp)),
        scratch_shapes=[pltpu.VMEM((subs * sub_len, 128), F32), pltpu.VMEM((subs * sub_len, 128), F32)],
        name=f"dil_bwd_{group}", compiler_params=_params())(qkv, qkv, qkv, slopes, d_out, out, lse_group, lse_total)


def _sort_rows(t, dil):
    if dil == 1:
        return t
    return t.reshape(SEQ // dil, dil, t.shape[-1]).transpose(1, 0, 2).reshape(SEQ, t.shape[-1])


def _unsort_rows(t, dil):
    if dil == 1:
        return t
    return t.reshape(dil, SEQ // dil, t.shape[-1]).transpose(1, 0, 2).reshape(SEQ, t.shape[-1])


def _ffn_up(name, h, w_gu):
    def body(h_ref, wg_ref, wu_ref, act_ref, hg_ref, hu_ref):
        hv = h_ref[...]
        hg = jnp.dot(hv, wg_ref[...], preferred_element_type=F32)
        hu = jnp.dot(hv, wu_ref[...], preferred_element_type=F32)
        act_ref[...] = (hg * jax.nn.sigmoid(hg) * hu).astype(act_ref.dtype)
        hg_ref[...] = hg.astype(hg_ref.dtype)
        hu_ref[...] = hu.astype(hu_ref.dtype)

    out = pl.BlockSpec((None, TM, FF_PAD), lambda d, i: (d, i, 0))
    shape = jax.ShapeDtypeStruct((N_DEV, SEQ, FF_PAD), BF16)
    return pl.pallas_call(
        body, out_shape=(shape, shape, shape), grid=(N_DEV, SEQ // TM),
        in_specs=[pl.BlockSpec((TM, D_MODEL), lambda d, i: (i, 0)),
                  pl.BlockSpec((None, None, D_MODEL, FF_PAD), lambda d, i: (d, 0, 0, 0)),
                  pl.BlockSpec((None, None, D_MODEL, FF_PAD), lambda d, i: (d, 1, 0, 0))],
        out_specs=(out, out, out), name=name, compiler_params=_params())(h, w_gu, w_gu)


def _ffn_bwd_act(name, d_f, w_down, hg, hu):
    def body(df_ref, wd_ref, hg_ref, hu_ref, dgu_ref):
        dact = lax.dot_general(df_ref[...], wd_ref[...], NT, preferred_element_type=F32)
        hgv = hg_ref[...].astype(F32)
        sig = jax.nn.sigmoid(hgv)
        dgu_ref[0] = (dact * hu_ref[...].astype(F32) * (sig * (1.0 + hgv * (1.0 - sig)))).astype(dgu_ref.dtype)
        dgu_ref[1] = (dact * hgv * sig).astype(dgu_ref.dtype)

    tile = pl.BlockSpec((None, TM, FF_PAD), lambda d, i: (d, i, 0))
    return pl.pallas_call(
        body, out_shape=jax.ShapeDtypeStruct((2, N_DEV, SEQ, FF_PAD), BF16), grid=(N_DEV, SEQ // TM),
        in_specs=[pl.BlockSpec((TM, D_MODEL), lambda d, i: (i, 0)),
                  pl.BlockSpec((None, FF_PAD, D_MODEL), lambda d, i: (d, 0, 0)), tile, tile],
        out_specs=pl.BlockSpec((2, None, TM, FF_PAD), lambda d, i: (0, d, i, 0)),
        name=name, compiler_params=_params())(d_f, w_down, hg, hu)


def _position():
    return lax.axis_index("x"), lax.axis_index("y"), lax.axis_index("c")


def _flat(p):
    return 4 * p[0] + 2 * p[1] + p[2]


def _peer(me, k):
    x, y, c = me
    return (1 - x if k & 4 else x, 1 - y if k & 2 else y, 1 - c if k & 1 else c)


def _columns(width):
    return lambda ref, d: ref.at[:, pl.ds(pl.multiple_of(d * width, 128), width)]


def _leading(ref, d):
    return ref.at[d]


def _whole(ref, d):
    return ref


def _by_sender(window):
    return lambda ref, sender, k: window(ref, sender)


def _by_distance(ref, sender, k):
    return ref.at[k - 1]


def _prep_weights(me, na_qkv, na_o, dil_qkv, dil_o, gate, up, down, land_shapes):
    na_cols, dil_cols = na_qkv.shape[-1], dil_qkv.shape[-1]
    o_rows = na_o.shape[1]
    tiles = 4
    rows, rows_o = D_MODEL // tiles, o_rows // tiles

    def body(me_ref, naq, nao, dq, do_, g0, u0, d0, g1, u1, d1, *outs):
        def put(t, index, value):
            outs[t][index] = value
            outs[8 + t][index] = value

        put(0, ..., naq[...].astype(BF16))
        put(1, ..., nao[...].astype(BF16))
        put(4, ..., dq[...].astype(BF16))
        put(5, ..., do_[...].astype(BF16))
        for t, (g, u, d) in ((2, (g0, u0, d0)), (6, (g1, u1, d1))):
            for j, part in enumerate((g, u)):
                put(t, (j, slice(None), slice(0, FF_SHARD)), part[...].astype(BF16))
                put(t, (j, slice(None), slice(FF_SHARD, FF_PAD)), jnp.zeros((rows, FF_PAD - FF_SHARD), BF16))
            put(t + 1, (slice(0, FF_SHARD), slice(None)), d[...].astype(BF16))
            put(t + 1, (slice(FF_SHARD, FF_PAD), slice(None)), jnp.zeros((FF_PAD - FF_SHARD, D_MODEL), BF16))

    def tiled(width):
        return pl.BlockSpec((None, rows, width), lambda i, me: (0, i, 0))

    def layer(l, width):
        return pl.BlockSpec((None, rows, width), lambda i, me: (l, i, 0))

    def whole_layer(l):
        return pl.BlockSpec((None, FF_SHARD, D_MODEL), lambda i, me: (l, 0, 0))

    in_specs = [tiled(na_cols), pl.BlockSpec((None, rows_o, D_MODEL), lambda i, me: (0, i, 0)), tiled(dil_cols),
                pl.BlockSpec((None, rows_o, D_MODEL), lambda i, me: (0, i, 0)),
                layer(0, FF_SHARD), layer(0, FF_SHARD), whole_layer(0), layer(1, FF_SHARD), layer(1, FF_SHARD), whole_layer(1)]
    o_shard = pl.BlockSpec((rows_o, D_MODEL), lambda i, me: (i, 0))
    o_land = pl.BlockSpec((None, rows_o, D_MODEL), lambda i, me: (me[0], i, 0))
    gu_shard = pl.BlockSpec((2, rows, FF_PAD), lambda i, me: (0, i, 0))
    gu_land = pl.BlockSpec((None, 2, rows, FF_PAD), lambda i, me: (me[0], 0, i, 0))
    down_shard = pl.BlockSpec((FF_PAD, D_MODEL), lambda i, me: (0, 0))
    down_land = pl.BlockSpec((None, FF_PAD, D_MODEL), lambda i, me: (me[0], 0, 0))

    def qkv_shard(width):
        return pl.BlockSpec((rows, width), lambda i, me: (i, 0))

    def qkv_land(width):
        return pl.BlockSpec((rows, width), lambda i, me: (i, me[0]))

    shard_specs = [qkv_shard(na_cols), o_shard, gu_shard, down_shard, qkv_shard(dil_cols), o_shard, gu_shard, down_shard]
    land_specs = [qkv_land(na_cols), o_land, gu_land, down_land, qkv_land(dil_cols), o_land, gu_land, down_land]
    shard_shapes = [jax.ShapeDtypeStruct(s, BF16) for s in
                    ((D_MODEL, na_cols), (o_rows, D_MODEL), (2, D_MODEL, FF_PAD), (FF_PAD, D_MODEL),
                     (D_MODEL, dil_cols), (o_rows, D_MODEL), (2, D_MODEL, FF_PAD), (FF_PAD, D_MODEL))]
    result = pl.pallas_call(
        body, out_shape=tuple(shard_shapes + list(land_shapes)),
        grid_spec=pltpu.PrefetchScalarGridSpec(num_scalar_prefetch=1, grid=(tiles,), in_specs=in_specs,
                                               out_specs=tuple(shard_specs + land_specs)),
        name="prep_weights", compiler_params=_params())(me, na_qkv, na_o, dil_qkv, dil_o, gate, up, down, gate, up, down)
    return list(result[:8]), list(result[8:])


def _remote_copies(sets, src_refs, land_refs, send_sems, recv_sems, outgoing):
    me = _position()
    copies = []
    for t, (si, src_of, li, dst_of) in enumerate(sets):
        for k in range(1, N_DEV):
            other = _peer(me, k)
            sender = me if outgoing else other
            copies.append(pltpu.make_async_remote_copy(
                src_ref=src_of(src_refs[si], _flat(other)), dst_ref=dst_of(land_refs[li], _flat(sender), k),
                send_sem=send_sems.at[(N_DEV - 1) * t + k - 1], recv_sem=recv_sems.at[(N_DEV - 1) * t + k - 1],
                device_id=other, device_id_type=MESH))
    return copies


def _send_start(name, srcs, lands, sets_by_group):
    n_src, n_land, n_groups = len(srcs), len(lands), len(sets_by_group)

    def body(*refs):
        src_refs, land_refs = refs[:n_src], refs[n_src:n_src + n_land]
        outs = refs[n_src + n_land:]
        for g, sets in enumerate(sets_by_group):
            for cp in _remote_copies(sets, src_refs, land_refs, outs[2 * g], outs[2 * g + 1], True):
                cp.start()
        outs[-1][...] = jnp.zeros_like(outs[-1])

    sem_shapes = []
    for sets in sets_by_group:
        sem_shapes += [pltpu.SemaphoreType.DMA((len(sets) * (N_DEV - 1),))] * 2
    thru = [pltpu.HBM(a.shape, a.dtype) for a in list(srcs) + list(lands)]
    n_sem = len(sem_shapes)
    result = pl.pallas_call(
        body, out_shape=tuple(sem_shapes + thru + [jax.ShapeDtypeStruct((8, 128), F32)]),
        in_specs=[HBM] * (n_src + n_land),
        out_specs=tuple([SEM] * n_sem + [HBM] * (n_src + n_land) + [pl.BlockSpec(memory_space=pltpu.VMEM)]),
        input_output_aliases={i: n_sem + i for i in range(n_src + n_land)},
        compiler_params=pltpu.CompilerParams(has_side_effects=EFFECT), name=name,
    )(*[pltpu.with_memory_space_constraint(a, pltpu.HBM) for a in list(srcs) + list(lands)])
    sems = [(result[2 * g], result[2 * g + 1]) for g in range(n_groups)]
    return sems, list(result[n_sem:n_sem + n_src]), list(result[n_sem + n_src:n_sem + n_src + n_land]), result[-1]


def _send_wait(name, sems, srcs, lands, sets, after):
    n_src, n_land = len(srcs), len(lands)

    def body(*refs):
        src_refs, land_refs = refs[:n_src], refs[n_src:n_src + n_land]
        send_sems, recv_sems = refs[n_src + n_land], refs[n_src + n_land + 1]
        for cp in _remote_copies(sets, src_refs, land_refs, send_sems, recv_sems, True):
            cp.wait_send()
        for cp in _remote_copies(sets, src_refs, land_refs, send_sems, recv_sems, False):
            cp.wait_recv()

    thru = [pltpu.HBM(a.shape, a.dtype) for a in list(srcs) + list(lands)]
    result = pl.pallas_call(
        body, out_shape=tuple(thru), in_specs=[HBM] * (n_src + n_land) + [SEM, SEM] + [ANY] * len(after),
        out_specs=tuple([HBM] * (n_src + n_land)), input_output_aliases={i: i for i in range(n_src + n_land)},
        compiler_params=pltpu.CompilerParams(has_side_effects=EFFECT), name=name,
    )(*srcs, *lands, sems[0], sems[1], *after)
    return list(result[:n_src]), list(result[n_src:])


DIRECT = (1, 2, 4, 6)
PASSED = DIRECT[1:]


def _hbm_passthrough(body, name, arrays, n_sem_in, sem_out_shapes, extra):
    n, n_out = len(arrays), len(sem_out_shapes)
    return pl.pallas_call(
        body, out_shape=tuple(list(sem_out_shapes) + [pltpu.HBM(a.shape, a.dtype) for a in arrays]),
        in_specs=[HBM] * n + [SEM] * n_sem_in + [ANY] * len(extra), out_specs=tuple([SEM] * n_out + [HBM] * n),
        input_output_aliases={i: n_out + i for i in range(n)},
        compiler_params=pltpu.CompilerParams(has_side_effects=EFFECT), name=name)


def _shard_copy(src_ref, land_ref, window, block, to, send_sem, recv_sem, from_shard):
    dst = window(land_ref, _flat(block))
    return pltpu.make_async_remote_copy(src_ref=src_ref if from_shard else dst, dst_ref=dst, send_sem=send_sem,
                                        recv_sem=recv_sem, device_id=to, device_id_type=MESH)


def _gather_start(name, shards, lands, windows, group_sizes):
    n = len(shards)

    def body(*refs):
        shard_refs, land_refs, outs = refs[:n], refs[n:2 * n], refs[2 * n:]
        me = _position()
        t = 0
        for g, size in enumerate(group_sizes):
            for local in range(size):
                for j, k in enumerate(DIRECT):
                    i = len(DIRECT) * local + j
                    _shard_copy(shard_refs[t], land_refs[t], windows[t], me, _peer(me, k), outs[2 * g].at[i],
                                outs[2 * g + 1].at[i], True).start()
                t += 1

    sem_shapes = [pltpu.SemaphoreType.DMA((len(DIRECT) * size,)) for size in group_sizes for _ in range(2)]
    arrays = [pltpu.with_memory_space_constraint(a, pltpu.HBM) for a in list(shards) + list(lands)]
    result = _hbm_passthrough(body, name, arrays, 0, sem_shapes, ())(*arrays)
    n_sem = len(sem_shapes)
    sems = [(result[2 * g], result[2 * g + 1]) for g in range(len(group_sizes))]
    return sems, list(result[n_sem:n_sem + n]), list(result[n_sem + n:])


def _gather_pass_on(name, sems, shards, lands, windows, after):
    n = len(shards)

    def body(*refs):
        shard_refs, land_refs = refs[:n], refs[n:2 * n]
        recv_sems = refs[2 * n + 1]
        pass_send, pass_recv = refs[2 * n + 2 + len(after)], refs[2 * n + 3 + len(after)]
        me = _position()
        sibling = _peer(me, 1)
        for t in range(n):
            for j, k in enumerate(PASSED):
                sender = _peer(me, k)
                arrived = len(DIRECT) * t + 1 + j
                _shard_copy(shard_refs[t], land_refs[t], windows[t], sender, me, refs[2 * n].at[arrived], recv_sems.at[arrived],
                            True).wait_recv()
                i = len(PASSED) * t + j
                _shard_copy(shard_refs[t], land_refs[t], windows[t], sender, sibling, pass_send.at[i], pass_recv.at[i],
                            False).start()

    sem_shapes = [pltpu.SemaphoreType.DMA((len(PASSED) * n,))] * 2
    result = _hbm_passthrough(body, name, list(shards) + list(lands), 2, sem_shapes, after)(
        *shards, *lands, sems[0], sems[1], *after)
    return (result[0], result[1]), list(result[2:2 + n]), list(result[2 + n:])


def _gather_wait(name, sems, pass_sems, shards, lands, windows, after):
    n = len(shards)

    def body(*refs):
        shard_refs, land_refs = refs[:n], refs[n:2 * n]
        send_sems, recv_sems, pass_send, pass_recv = refs[2 * n:2 * n + 4]
        me = _position()
        sibling = _peer(me, 1)
        for t in range(n):
            for j, k in enumerate(DIRECT):
                i = len(DIRECT) * t + j
                _shard_copy(shard_refs[t], land_refs[t], windows[t], me, _peer(me, k), send_sems.at[i], recv_sems.at[i],
                            True).wait_send()
            _shard_copy(shard_refs[t], land_refs[t], windows[t], sibling, me, send_sems.at[len(DIRECT) * t],
                        recv_sems.at[len(DIRECT) * t], True).wait_recv()
            for j, k in enumerate(PASSED):
                i = len(PASSED) * t + j
                _shard_copy(shard_refs[t], land_refs[t], windows[t], _peer(me, k), sibling, pass_send.at[i], pass_recv.at[i],
                            False).wait_send()
                _shard_copy(shard_refs[t], land_refs[t], windows[t], _peer(sibling, k), me, pass_send.at[i], pass_recv.at[i],
                            False).wait_recv()

    result = _hbm_passthrough(body, name, list(shards) + list(lands), 4, [], after)(
        *shards, *lands, sems[0], sems[1], pass_sems[0], pass_sems[1], *after)
    return list(result[n:])


def _all_gather(name, locals_, out_shapes, windows, deps=()):
    n = len(locals_)

    def body(*refs):
        src_refs, out_refs = refs[:n], refs[n + len(deps):2 * n + len(deps)]
        send_sems, recv_sems, local_sems = refs[2 * n + len(deps):]
        x, y, c = _position()
        me, sibling = (x, y, c), (x, y, 1 - c)
        chips = [(1 - x, y), (x, 1 - y), (1 - x, 1 - y)]

        def copy(t, k, block, to, from_local=False):
            dst = windows[t](out_refs[t], _flat(block))
            return pltpu.make_async_remote_copy(
                src_ref=src_refs[t] if from_local else dst, dst_ref=dst, send_sem=send_sems.at[t, k],
                recv_sem=recv_sems.at[t, k], device_id=to, device_id_type=MESH)

        mine = [pltpu.make_async_copy(src_refs[t], windows[t](out_refs[t], _flat(me)), local_sems.at[t]) for t in range(n)]
        sends = []
        for t in range(n):
            mine[t].start()
            sends.append(copy(t, 0, me, sibling, True))
            sends += [copy(t, 1 + j, me, (*chip, c), True) for j, chip in enumerate(chips)]
        for cp in sends:
            cp.start()
        for t in range(n):
            for j, chip in enumerate(chips):
                copy(t, 1 + j, (*chip, c), me).wait_recv()
                passed = copy(t, 4 + j, (*chip, c), sibling)
                passed.start()
                sends.append(passed)
        for t in range(n):
            copy(t, 0, sibling, me).wait_recv()
            for j, chip in enumerate(chips):
                copy(t, 4 + j, (*chip, 1 - c), me).wait_recv()
        for cp in sends:
            cp.wait_send()
        for cp in mine:
            cp.wait()

    return pl.pallas_call(
        body, out_shape=tuple(out_shapes), in_specs=[ANY] * (n + len(deps)), out_specs=tuple([ANY] * n),
        scratch_shapes=[pltpu.SemaphoreType.DMA((n, 7)), pltpu.SemaphoreType.DMA((n, 7)), pltpu.SemaphoreType.DMA((n,))],
        name=name)(*locals_, *deps)


def _adamw(name, me, lands, owns, w, m, v, *, grid, land_specs, own_specs, p_spec):
    n_land = len(lands)

    def body(me_ref, *refs):
        land_refs, own_refs = refs[:n_land], refs[n_land:n_land + len(owns)]
        w_ref, m_ref, v_ref, g_ref, delta_ref, m_out, v_out = refs[n_land + len(owns):]
        ncols = w_ref.shape[-1]
        sums = []
        for i, land_ref in enumerate(land_refs):
            g = own_refs[i][...].astype(F32) if owns else land_ref[0].astype(F32)
            for s in range(0 if owns else 1, land_ref.shape[0]):
                g = g + land_ref[s].astype(F32)
            sums.append(g[:, :ncols])
        g = sums[0] if n_land == 1 else jnp.where(pl.program_id(0) == 0, sums[0], sums[1])
        m_new = ADAM_B1 * m_ref[...] + (1.0 - ADAM_B1) * g
        v_new = ADAM_B2 * v_ref[...] + (1.0 - ADAM_B2) * jnp.square(g)
        m_hat = m_new / (1.0 - ADAM_B1 ** ADAM_STEP)
        v_hat = v_new / (1.0 - ADAM_B2 ** ADAM_STEP)
        g_ref[...] = g
        delta_ref[...] = -ADAM_LR * (m_hat / (jnp.sqrt(v_hat) + ADAM_EPS) + ADAM_WD * w_ref[...])
        m_out[...] = m_new
        v_out[...] = v_new

    shape = jax.ShapeDtypeStruct(w.shape, F32)
    return pl.pallas_call(
        body, out_shape=(shape,) * 4,
        grid_spec=pltpu.PrefetchScalarGridSpec(
            num_scalar_prefetch=1, grid=grid, in_specs=list(land_specs) + list(own_specs) + [p_spec, p_spec, p_spec],
            out_specs=(p_spec,) * 4),
        name=name, compiler_params=_params())(me, *lands, *owns, w, m, v)


def _row(p, layer):
    return p[layer][None, :]


def _square(name, a, b, dims, out_dtype, deps=()):
    if a.shape == (D_MODEL, SEQ):
        return _matmul(name, a, b, grid=(2, 1), a_spec=pl.BlockSpec((512, SEQ), lambda i, k: (i, 0)),
                       b_spec=pl.BlockSpec((SEQ, D_MODEL), lambda i, k: (0, 0)),
                       o_spec=pl.BlockSpec((512, D_MODEL), lambda i, k: (i, 0)),
                       out_shape=jax.ShapeDtypeStruct((D_MODEL, D_MODEL), out_dtype), dims=NN, acc_shape=(8, 128),
                       deps=deps)
    return _matmul(name, a, b, grid=(SEQ // TM, 1), a_spec=pl.BlockSpec((TM, D_MODEL), lambda i, k: (i, 0)),
                   b_spec=pl.BlockSpec((D_MODEL, D_MODEL), lambda i, k: (0, 0)),
                   o_spec=pl.BlockSpec((TM, D_MODEL), lambda i, k: (i, 0)),
                   out_shape=jax.ShapeDtypeStruct((SEQ, D_MODEL), out_dtype), dims=dims, acc_shape=(8, 128), deps=deps)


def _qkv_fwd(name, hs, w, n_chunks):
    return _matmul(name, hs, w, grid=(n_chunks, SEQ // TM, 1),
                   a_spec=pl.BlockSpec((None, TM, D_MODEL), lambda j, i, k: (j // 3, i, 0)),
                   b_spec=pl.BlockSpec((D_MODEL, D_MODEL), lambda j, i, k: (0, j)),
                   o_spec=pl.BlockSpec((TM, D_MODEL), lambda j, i, k: (i, j)),
                   out_shape=jax.ShapeDtypeStruct((SEQ, n_chunks * D_MODEL), BF16), dims=NN, acc_shape=(8, 128))


def _qkv_dw(name, hs_t, dqkv, n_chunks):
    return _matmul(name, hs_t, dqkv, grid=(n_chunks, 2, 1),
                   a_spec=pl.BlockSpec((None, 512, SEQ), lambda j, i, k: (j // 3, i, 0)),
                   b_spec=pl.BlockSpec((None, SEQ, D_MODEL), lambda j, i, k: (j, 0, 0)),
                   o_spec=pl.BlockSpec((512, D_MODEL), lambda j, i, k: (i, j)),
                   out_shape=jax.ShapeDtypeStruct((D_MODEL, n_chunks * D_MODEL), BF16), dims=NN, acc_shape=(8, 128))


def _qkv_dh(name, dqkv, w, n_chunks, deps):
    return _matmul(name, dqkv, w, grid=(n_chunks // 3, SEQ // TM, 1),
                   a_spec=pl.BlockSpec((3, TM, D_MODEL), lambda g, i, k: (g, i, 0)),
                   b_spec=pl.BlockSpec((D_MODEL, 3 * D_MODEL), lambda g, i, k: (0, g)),
                   o_spec=pl.BlockSpec((None, TM, D_MODEL), lambda g, i, k: (g, i, 0)),
                   out_shape=jax.ShapeDtypeStruct((n_chunks // 3, SEQ, D_MODEL), F32), dims=NT, acc_shape=(8, 128),
                   deps=deps, inner=3)


def _local_step(x, target, norms, rpb, fetch, emit, deps):
    mix_pre, mix_post, ffn_pre, ffn_post = norms
    slopes = 2.0 ** (-8.0 * jnp.arange(1, N_HEADS + 1, dtype=F32) / N_HEADS)
    rpb_pad = jnp.pad(rpb, ((0, 0), (0, 1), (0, 128 - 31)))
    saved = []

    for layer in range(2):
        tag = f"l{layer}"
        h = _rms_fwd(tag + "_norm_mix", x, _row(mix_pre, layer), out_dtype=BF16, deps=deps if layer == 0 else ())
        if layer == 0:
            w_qkv, w_o = fetch("na", deps, [h])
            table = _rpb_table(rpb_pad)
            qkv = _qkv_fwd(tag + "_qkv", h[None], w_qkv, 3)
            o, lse = _na_fwd(qkv, table)
            mixer = (h, qkv, o, lse, table)
        else:
            w_qkv, w_o = fetch("dil", [x], [h])
            hs = jnp.stack([_sort_rows(h, dil) for _, dil in DIL_GROUPS])
            qkv = _qkv_fwd(tag + "_qkv", hs, w_qkv, 9)
            outs, lses, lses_sorted = [], [], []
            for g, (_, dil) in enumerate(DIL_GROUPS):
                o_g, lse_g = _dil_fwd(g, qkv, slopes * dil)
                outs.append(_unsort_rows(o_g, dil))
                lses.append(_unsort_rows(lse_g, dil))
                lses_sorted.append(lse_g)
            o, lse_total = _dil_merge(outs, lses)
            mixer = (hs, qkv, o, lses_sorted, lse_total)
        a = _square(tag + "_proj", o, w_o, NN, F32)
        x1 = _rms_fwd(tag + "_post_mix", a, _row(mix_post, layer), res=x)
        h2 = _rms_fwd(tag + "_norm_ffn", x1, _row(ffn_pre, layer), out_dtype=BF16)
        w_gu, w_down = fetch(f"ffn{layer}", [a], [h2])
        act, hg, hu = _ffn_up(tag + "_ffn_up", h2, w_gu)
        f = _matmul(tag + "_ffn_down", act, w_down, grid=(SEQ // TM, N_DEV // 4),
                    a_spec=pl.BlockSpec((4, TM, FF_PAD), lambda i, k: (k, i, 0)),
                    b_spec=pl.BlockSpec((4, FF_PAD, D_MODEL), lambda i, k: (k, 0, 0)),
                    o_spec=pl.BlockSpec((TM, D_MODEL), lambda i, k: (i, 0)),
                    out_shape=jax.ShapeDtypeStruct((SEQ, D_MODEL), F32), dims=NN, acc_shape=(TM, D_MODEL), inner=4)
        x2 = _rms_fwd(tag + "_post_ffn", f, _row(ffn_post, layer), res=x1)
        transposed = (jnp.swapaxes(mixer[0], -1, -2), mixer[2].astype(BF16).T, h2.T, jnp.swapaxes(act, 1, 2))
        saved.append((x, mixer, a, x1, transposed, hg, hu, f, w_qkv, w_o, w_gu, w_down))
        x = x2

    dx, loss = _loss_head("loss_head", x, target)
    d_norm = {k: [None, None] for k in ("mix_pre", "mix_post", "ffn_pre", "ffn_post")}
    d_rpb = None

    for layer in (1, 0):
        tag = f"b{layer}"
        x0, mixer, a, x1, (h_t, o_t, h2_t, act_t), hg, hu, f, w_qkv, w_o, w_gu, w_down = saved[layer]
        d_f, d_norm["ffn_post"][layer] = _rms_bwd(tag + "_post_ffn", f, _row(ffn_post, layer), [dx], out_dtype=BF16)
        dgu = _ffn_bwd_act(tag + "_ffn_act", d_f, w_down, hg, hu)
        d_down = _matmul(
            tag + "_ffn_ddown", act_t, d_f, grid=(N_DEV, 1),
            a_spec=pl.BlockSpec((None, FF_PAD, SEQ), lambda d, k: (d, 0, 0)),
            b_spec=pl.BlockSpec((SEQ, D_MODEL), lambda d, k: (0, 0)),
            o_spec=pl.BlockSpec((None, FF_PAD, D_MODEL), lambda d, k: (d, 0, 0)),
            out_shape=jax.ShapeDtypeStruct((N_DEV, FF_PAD, D_MODEL), BF16), dims=NN, acc_shape=(8, 128))
        d_gu = _matmul(
            tag + "_ffn_dgu", h2_t, dgu, grid=(2, N_DEV, 1),
            a_spec=pl.BlockSpec((D_MODEL, SEQ), lambda t, d, k: (0, 0)),
            b_spec=pl.BlockSpec((None, None, SEQ, FF_PAD), lambda t, d, k: (t, d, 0, 0)),
            o_spec=pl.BlockSpec((None, None, D_MODEL, FF_PAD), lambda t, d, k: (d, t, 0, 0)),
            out_shape=jax.ShapeDtypeStruct((N_DEV, 2, D_MODEL, FF_PAD), BF16), dims=NN, acc_shape=(8, 128))
        sent = emit(f"ffn{layer}", [d_gu, d_down])
        d_h2 = _matmul(
            tag + "_ffn_dh", dgu, w_gu, grid=(SEQ // TM, 4),
            a_spec=pl.BlockSpec((None, 4, TM, FF_PAD), lambda i, k: (k // 2, k % 2, i, 0)),
            b_spec=pl.BlockSpec((4, None, D_MODEL, FF_PAD), lambda i, k: (k % 2, k // 2, 0, 0)),
            o_spec=pl.BlockSpec((TM, D_MODEL), lambda i, k: (i, 0)),
            out_shape=jax.ShapeDtypeStruct((SEQ, D_MODEL), F32), dims=NT, acc_shape=(TM, D_MODEL), deps=sent, inner=4)
        dx1, d_norm["ffn_pre"][layer] = _rms_bwd(tag + "_norm_ffn", x1, _row(ffn_pre, layer), [d_h2], res=dx)
        d_a, d_norm["mix_post"][layer] = _rms_bwd(tag + "_post_mix", a, _row(mix_post, layer), [dx1], out_dtype=BF16)
        d_wo = _square(tag + "_proj_dw", o_t, d_a, NN, BF16)
        if layer == 0:
            h, qkv, o, lse, table = mixer
            d_o = _square(tag + "_proj_do", d_a, w_o, NT, BF16)
            dqkv, gp = _na_bwd(qkv, table, d_o, lse)
            d_rpb = _rpb_grad(gp)[:, :15, :31]
            sent = emit("na", [_qkv_dw(tag + "_qkv_dw", h_t[None], dqkv, 3), d_wo])
            d_h = _qkv_dh(tag + "_qkv_dh", dqkv, w_qkv, 3, sent)
            d_hs = [d_h[0]]
        else:
            hs, qkv, o, lses, lse_total = mixer
            d_o = _square(tag + "_proj_do", d_a, w_o, NT, F32)
            parts = []
            for g, (_, dil) in enumerate(DIL_GROUPS):
                parts.append(_dil_bwd(g, qkv, slopes * dil, _sort_rows(d_o, dil), _sort_rows(o, dil), lses[g],
                                      _sort_rows(lse_total, dil)))
            dqkv = jnp.concatenate(parts, axis=0)
            sent = emit("dil", [_qkv_dw(tag + "_qkv_dw", h_t, dqkv, 9), d_wo])
            d_h = _qkv_dh(tag + "_qkv_dh", dqkv, w_qkv, 9, sent)
            d_hs = [_unsort_rows(d_h[g], dil) for g, (_, dil) in enumerate(DIL_GROUPS)]
        dx, d_norm["mix_pre"][layer] = _rms_bwd(tag + "_norm_mix", x0, _row(mix_pre, layer), d_hs, res=dx1)

    d_gains = [jnp.concatenate(d_norm[k], axis=0) for k in ("mix_pre", "mix_post", "ffn_pre", "ffn_post")]
    return loss, dx, d_gains, d_rpb


RPB_SIZE = N_HEADS * 15 * 31


def _pack_small(gains, rpb, last=None):
    top = jnp.concatenate(gains, axis=0).reshape(64, 128)
    bottom = jnp.pad(rpb.reshape(-1), (0, 64 * 128 - RPB_SIZE))
    if last is not None:
        bottom = bottom + jnp.pad(last.reshape(1), (64 * 128 - 1, 0))
    return jnp.concatenate([top, bottom.reshape(64, 128)], axis=0)


def _unpack_small(p):
    gains = p[:64].reshape(4, 2, D_MODEL)
    rpb = p[64:].reshape(-1)[:RPB_SIZE].reshape(1, N_HEADS, 15, 31)
    return [gains[i] for i in range(4)], rpb


GROUPS = ("na", "ffn0", "dil", "ffn1")


def kernel(x, norm_mix_pre, norm_mix_post, norm_ffn_pre, norm_ffn_post, na_w_qkv, na_w_o, na_rpb, dil_w_qkv, dil_w_o, ffn_w_gate, ffn_w_up, ffn_w_down, loss_target, m_norm_mix_pre, m_norm_mix_post, m_norm_ffn_pre, m_norm_ffn_post, m_na_w_qkv, m_na_w_o, m_na_rpb, m_dil_w_qkv, m_dil_w_o, m_ffn_w_gate, m_ffn_w_up, m_ffn_w_down, v_norm_mix_pre, v_norm_mix_post, v_norm_ffn_pre, v_norm_ffn_post, v_na_w_qkv, v_na_w_o, v_na_rpb, v_dil_w_qkv, v_dil_w_o, v_ffn_w_gate, v_ffn_w_up, v_ffn_w_down):
    na_cols, dil_cols, o_rows = 3 * D_MODEL // N_DEV, 9 * D_MODEL // N_DEV, D_MODEL // N_DEV
    ff_pad = FF_PAD - FF_SHARD
    me = (4 * lax.axis_index("x") + 2 * lax.axis_index("y") + lax.axis_index("c")).astype(jnp.int32).reshape(1)

    full = {
        "na": [((D_MODEL, 3 * D_MODEL), _columns(na_cols)), ((N_DEV, o_rows, D_MODEL), _leading)],
        "dil": [((D_MODEL, 9 * D_MODEL), _columns(dil_cols)), ((N_DEV, o_rows, D_MODEL), _leading)],
        "ffn0": [((N_DEV, 2, D_MODEL, FF_PAD), _leading), ((N_DEV, FF_PAD, D_MODEL), _leading)],
        "ffn1": [((N_DEV, 2, D_MODEL, FF_PAD), _leading), ((N_DEV, FF_PAD, D_MODEL), _leading)],
    }
    block = {
        "na": [(D_MODEL, na_cols), (o_rows, D_MODEL)], "dil": [(D_MODEL, dil_cols), (o_rows, D_MODEL)],
        "ffn0": [(2, D_MODEL, FF_PAD), (FF_PAD, D_MODEL)], "ffn1": [(2, D_MODEL, FF_PAD), (FF_PAD, D_MODEL)],
    }

    land_shapes = [jax.ShapeDtypeStruct(full[g][t][0], BF16) for g in GROUPS for t in range(2)]
    windows = [full[g][t][1] for g in GROUPS for t in range(2)]
    shards, lands = _prep_weights(me, na_w_qkv, na_w_o, dil_w_qkv, dil_w_o, ffn_w_gate, ffn_w_up, ffn_w_down, land_shapes)
    sems, shards, lands = _gather_start("gather_start", shards, lands, windows, [2] * len(GROUPS))

    def fetch(group, early, late):
        gi = GROUPS.index(group)
        mine = slice(2 * gi, 2 * gi + 2)
        pass_sems, shards_g, lands_g = _gather_pass_on(f"gather_pass_{group}", sems[gi], shards[mine], lands[mine],
                                                       windows[mine], early)
        qkv, o = _gather_wait(f"gather_wait_{group}", sems[gi], pass_sems, shards_g, lands_g, windows[mine], late)
        return (qkv, o.reshape(D_MODEL, D_MODEL)) if group in ("na", "dil") else (qkv, o)

    def grad_source(group, t):
        return _columns(block[group][0][1]) if (group in ("na", "dil") and t == 0) else _leading

    in_flight = {}

    def emit(group, grads):
        if group in ("na", "dil"):
            grads = [grads[0], grads[1].reshape(N_DEV, o_rows, D_MODEL)]
        sets = [(t, grad_source(group, t), t, _by_distance) for t in range(2)]
        landing = [lax.empty((N_DEV - 1,) + block[group][t], BF16) for t in range(2)]
        sems_g, grads, landing, tok = _send_start(f"exchange_start_{group}", grads, landing, [sets])
        in_flight[group] = (sems_g[0], grads, landing, sets)
        return [tok]

    norms = (norm_mix_pre, norm_mix_post, norm_ffn_pre, norm_ffn_post)
    loss, grad_x, d_gains, d_rpb = _local_step(x[0], loss_target[0], norms, na_rpb[0], fetch, emit, [shards[0]])

    landed, sent = {}, {}

    def wait_for(group, after):
        sems_g, grads, landing, sets = in_flight[group]
        sent[group], landed[group] = _send_wait(f"exchange_wait_{group}", sems_g, grads, landing, sets, after)

    for group in ("ffn1", "dil", "ffn0"):
        wait_for(group, [grad_x])

    def one(rows, tile, ncols, columns):
        own = (pl.BlockSpec((tile, ncols), lambda i, me: (i, me[0])) if columns
               else pl.BlockSpec((None, tile, ncols), lambda i, me: (me[0], i, 0)))
        return dict(grid=(rows // tile,), land_specs=[pl.BlockSpec((N_DEV - 1, tile, ncols), lambda i, me: (0, i, 0))],
                    own_specs=[own], p_spec=pl.BlockSpec((None, tile, ncols), lambda i, me: (0, i, 0)))

    def layered(block_shape, index, p_block, n_tiles):
        def specs(lead_size, lead):
            shape = (lead_size,) + block_shape
            return [pl.BlockSpec(shape, lambda l, r, me: index(lead(me), jnp.where(l == 0, r, n_tiles - 1))),
                    pl.BlockSpec(shape, lambda l, r, me: index(lead(me), jnp.where(l == 0, 0, r)))]
        return dict(grid=(2, n_tiles), land_specs=specs(N_DEV - 1, lambda me: 0), own_specs=specs(None, lambda me: me[0]),
                    p_spec=pl.BlockSpec(p_block, lambda l, r, me: (l, r, 0)))

    gu_lands, gu_owns = [landed["ffn0"][0], landed["ffn1"][0]], [sent["ffn0"][0], sent["ffn1"][0]]
    down_lands, down_owns = [landed["ffn0"][1], landed["ffn1"][1]], [sent["ffn0"][1], sent["ffn1"][1]]
    updates = {
        "dil_w_qkv": _adamw("adamw_dil_qkv", me, [landed["dil"][0]], [sent["dil"][0]], dil_w_qkv, m_dil_w_qkv, v_dil_w_qkv,
                            **one(D_MODEL, 128, dil_cols, True)),
        "dil_w_o": _adamw("adamw_dil_o", me, [landed["dil"][1]], [sent["dil"][1]], dil_w_o, m_dil_w_o, v_dil_w_o,
                          **one(o_rows, o_rows, D_MODEL, False)),
        "ffn_w_gate": _adamw("adamw_gate", me, gu_lands, gu_owns, ffn_w_gate, m_ffn_w_gate, v_ffn_w_gate,
                             **layered((None, 128, FF_PAD), lambda lead, r: (lead, 0, r, 0), (None, 128, FF_SHARD), 8)),
        "ffn_w_up": _adamw("adamw_up", me, gu_lands, gu_owns, ffn_w_up, m_ffn_w_up, v_ffn_w_up,
                           **layered((None, 128, FF_PAD), lambda lead, r: (lead, 1, r, 0), (None, 128, FF_SHARD), 8)),
        "ffn_w_down": _adamw("adamw_down", me, down_lands, down_owns, ffn_w_down, m_ffn_w_down, v_ffn_w_down,
                             **layered((176, D_MODEL), lambda lead, r: (lead, r, 0), (None, 176, D_MODEL), 2)),
    }
    done = [u[0] for u in updates.values()]
    small = _all_gather("gather_small", [_pack_small(d_gains, d_rpb, loss)], [jax.ShapeDtypeStruct((N_DEV, 128, 128), F32)],
                        [_leading], deps=done)[0]
    wait_for("na", [small])
    updates["na_w_qkv"] = _adamw("adamw_na_qkv", me, [landed["na"][0]], [sent["na"][0]], na_w_qkv, m_na_w_qkv, v_na_w_qkv,
                                 **one(D_MODEL, 256, na_cols, True))
    updates["na_w_o"] = _adamw("adamw_na_o", me, [landed["na"][1]], [sent["na"][1]], na_w_o, m_na_w_o, v_na_w_o,
                               **one(o_rows, o_rows, D_MODEL, False))
    gains = [norm_mix_pre, norm_mix_post, norm_ffn_pre, norm_ffn_post]
    m_gains = [m_norm_mix_pre, m_norm_mix_post, m_norm_ffn_pre, m_norm_ffn_post]
    v_gains = [v_norm_mix_pre, v_norm_mix_post, v_norm_ffn_pre, v_norm_ffn_post]
    packed = _adamw("adamw_small", me, [small], (), _pack_small(gains, na_rpb)[None], _pack_small(m_gains, m_na_rpb)[None],
                    _pack_small(v_gains, v_na_rpb)[None], grid=(1,),
                    land_specs=[pl.BlockSpec((N_DEV, 128, 128), lambda i, me: (0, 0, 0))], own_specs=[],
                    p_spec=pl.BlockSpec((None, 128, 128), lambda i, me: (0, 0, 0)))
    small_out = [_unpack_small(p[0]) for p in packed]

    order = ["na_w_qkv", "na_w_o", "na_rpb", "dil_w_qkv", "dil_w_o", "ffn_w_gate", "ffn_w_up", "ffn_w_down"]
    result = [packed[0][0, 127, 127], grad_x[None]]
    for kind in range(4):
        gains_k, rpb_k = small_out[kind]
        result += gains_k
        result += [rpb_k if name == "na_rpb" else updates[name][kind] for name in order]
    return tuple(result)
```

```python
import functools

import jax
import jax.numpy as jnp
from jax import lax
from jax.experimental import pallas as pl
from jax.experimental.pallas import tpu as pltpu

F32 = jnp.float32
BF16 = jnp.bfloat16
MESH = pl.DeviceIdType.MESH
ANY = pl.BlockSpec(memory_space=pl.ANY)
HBM = pl.BlockSpec(memory_space=pltpu.HBM)
SEM = pl.BlockSpec(memory_space=pltpu.SEMAPHORE)
EFFECT = pltpu.SideEffectType.DATAFLOW_SIDE_EFFECTING

N_DEV = 8
SEQ = 2048
D_MODEL = 1024
N_HEADS = 16
HEAD_DIM = 64
GRID_W = 64
NA_ROWS = 8
SEQ_ROWS = SEQ // GRID_W
DIL_GROUPS = ((128, 1), (512, 4), (2048, 16))
BAND = 128
RADIUS = 64
FF_SHARD = 352
FF_PAD = 384
RMS_EPS = 1e-6
NEG_INF = -1e30
Q_SCALE = HEAD_DIM ** -0.5

ADAM_LR = 0.001
ADAM_B1 = 0.9
ADAM_B2 = 0.999
ADAM_EPS = 1e-08
ADAM_WD = 0.01
ADAM_STEP = 10

VMEM_LIMIT = 56 * 1024 * 1024
TM = 512

NN = (((1,), (0,)), ((), ()))
NT = (((1,), (1,)), ((), ()))
TN = (((0,), (0,)), ((), ()))


def _params():
    return pltpu.CompilerParams(vmem_limit_bytes=VMEM_LIMIT)


def _matmul(name, a, b, *, grid, a_spec, b_spec, o_spec, out_shape, dims, acc_shape, deps=(), inner=1):
    nk = grid[-1]
    kaxis = len(grid) - 1

    def body(a_ref, b_ref, *rest):
        o_ref, acc_ref = rest[-2], rest[-1]
        if inner == 1:
            part = lax.dot_general(a_ref[...].astype(BF16), b_ref[...].astype(BF16), dims, preferred_element_type=F32)
        elif len(b_ref.shape) == 2:
            a_all = jnp.concatenate([a_ref[j].astype(BF16) for j in range(inner)], axis=1)
            part = lax.dot_general(a_all, b_ref[...].astype(BF16), dims, preferred_element_type=F32)
        else:
            part = sum(lax.dot_general(a_ref[j].astype(BF16), b_ref[j].astype(BF16), dims, preferred_element_type=F32)
                       for j in range(inner))
        if nk == 1:
            o_ref[...] = part.astype(o_ref.dtype)
        else:
            k = pl.program_id(kaxis)

            @pl.when(k == 0)
            def _():
                acc_ref[...] = part

            @pl.when(k > 0)
            def _():
                acc_ref[...] += part

            @pl.when(k == nk - 1)
            def _():
                o_ref[...] = acc_ref[...].astype(o_ref.dtype)

    return pl.pallas_call(
        body, out_shape=out_shape, grid=grid, in_specs=[a_spec, b_spec] + [ANY] * len(deps), out_specs=o_spec,
        scratch_shapes=[pltpu.VMEM(acc_shape, F32)], name=name, compiler_params=_params())(a, b, *deps)


SORTED = tuple(d for _, d in DIL_GROUPS if d > 1)
LANE_CHUNKS = D_MODEL // 128


def _sort_scratch(tm=TM):
    return pltpu.VMEM((LANE_CHUNKS, tm, 128), F32)


def _sorted_view(t, dil):
    return t.reshape(dil, SEQ // dil, D_MODEL)


def _sorted_spec(dil, lead=(), tm=TM):
    return pl.BlockSpec((None,) * len(lead) + (dil, tm // dil, D_MODEL), lambda i: tuple(lead) + (0, i, 0))


def _sort_tile(scratch, value, dil, out_ref):
    tm = value.shape[0]
    for c in range(LANE_CHUNKS):
        scratch[c] = value[:, 128 * c:128 * (c + 1)]
    for r in range(dil):
        rows = [scratch.at[c][pl.ds(r, tm // dil, stride=dil), :] for c in range(LANE_CHUNKS)]
        out_ref[r] = jnp.concatenate(rows, axis=1).astype(out_ref.dtype)


def _unsort_tile(scratch, in_ref, dil):
    for r in range(dil):
        value = in_ref[r].astype(F32)
        for c in range(LANE_CHUNKS):
            scratch.at[c][pl.ds(r, value.shape[0], stride=dil), :] = value[:, 128 * c:128 * (c + 1)]
    return jnp.concatenate([scratch[c] for c in range(LANE_CHUNKS)], axis=1)


def _rms_fwd(name, x, g, res=None, out_dtype=F32, deps=(), sorted_too=False):
    n_tiles = SEQ // TM
    has_res = res is not None
    n_in = 2 + has_res + len(deps)

    def body(*refs):
        x_ref, g_ref = refs[0], refs[1]
        xv = x_ref[...]
        r = lax.rsqrt(jnp.mean(xv * xv, axis=-1, keepdims=True) + RMS_EPS)
        y = xv * r * g_ref[...]
        if has_res:
            y = refs[2][...] + y
        refs[n_in][...] = y.astype(out_dtype)
        if sorted_too:
            for j, dil in enumerate(SORTED):
                _sort_tile(refs[-1], y, dil, refs[n_in + 1 + j])

    tile = pl.BlockSpec((TM, D_MODEL), lambda i: (i, 0))
    gspec = pl.BlockSpec((1, D_MODEL), lambda i: (0, 0))
    ins = [x, g] + ([res] if has_res else []) + list(deps)
    specs = [tile, gspec] + ([tile] if has_res else []) + [ANY] * len(deps)
    shapes, out_specs = [jax.ShapeDtypeStruct((SEQ, D_MODEL), out_dtype)], [tile]
    if sorted_too:
        shapes += [jax.ShapeDtypeStruct((dil, SEQ // dil, D_MODEL), out_dtype) for dil in SORTED]
        out_specs += [_sorted_spec(dil) for dil in SORTED]
    result = pl.pallas_call(
        body, out_shape=tuple(shapes), grid=(n_tiles,), in_specs=specs, out_specs=tuple(out_specs),
        scratch_shapes=[_sort_scratch()] if sorted_too else [], name=name, compiler_params=_params())(*ins)
    return result if sorted_too else result[0]


def _rms_bwd(name, x, g, dys, res=None, out_dtype=F32, groups=None):
    n_tiles = SEQ // TM
    n_dy = len(dys) if groups is None else 1 + len(SORTED)
    has_res = res is not None

    def body(*refs):
        x_ref, g_ref = refs[0], refs[1]
        dy_refs = refs[2:2 + n_dy]
        res_ref = refs[2 + n_dy] if has_res else None
        dx_ref, dg_ref, acc_ref = refs[2 + n_dy + has_res:5 + n_dy + has_res]
        i = pl.program_id(0)
        xv = x_ref[...]
        r = lax.rsqrt(jnp.mean(xv * xv, axis=-1, keepdims=True) + RMS_EPS)
        xn = xv * r
        dy = dy_refs[0][...].astype(F32)
        for j, extra in enumerate(dy_refs[1:]):
            dy = dy + (extra[...].astype(F32) if groups is None else _unsort_tile(refs[-1], extra, SORTED[j]))
        dyg = dy * g_ref[...]
        dx = r * (dyg - xn * jnp.mean(dyg * xn, axis=-1, keepdims=True))
        if has_res:
            dx = res_ref[...] + dx
        dx_ref[...] = dx.astype(dx_ref.dtype)
        part = jnp.sum((dy * xn).reshape(TM // 8, 8, D_MODEL), axis=0)

        @pl.when(i == 0)
        def _():
            acc_ref[...] = part

        @pl.when(i > 0)
        def _():
            acc_ref[...] += part

        @pl.when(i == n_tiles - 1)
        def _():
            dg_ref[...] = jnp.broadcast_to(jnp.sum(acc_ref[...], axis=0, keepdims=True), (8, D_MODEL))

    tile = pl.BlockSpec((TM, D_MODEL), lambda i: (i, 0))
    gspec = pl.BlockSpec((1, D_MODEL), lambda i: (0, 0))
    if groups is None:
        dy_ins, dy_specs = list(dys), [tile] * n_dy
    else:
        dy_ins = [groups] + [groups.reshape(n_dy, dil, SEQ // dil, D_MODEL) for dil in SORTED]
        dy_specs = [pl.BlockSpec((None, TM, D_MODEL), lambda i: (0, i, 0))]
        dy_specs += [_sorted_spec(dil, lead=(1 + j,)) for j, dil in enumerate(SORTED)]
    ins = [x, g] + dy_ins + ([res] if has_res else [])
    specs = [tile, gspec] + dy_specs + ([tile] if has_res else [])
    dx, dg = pl.pallas_call(
        body, out_shape=(jax.ShapeDtypeStruct((SEQ, D_MODEL), out_dtype), jax.ShapeDtypeStruct((8, D_MODEL), F32)),
        grid=(n_tiles,), in_specs=specs,
        out_specs=(tile, pl.BlockSpec((8, D_MODEL), lambda i: (0, 0))),
        scratch_shapes=[pltpu.VMEM((8, D_MODEL), F32)] + ([_sort_scratch()] if groups is not None else []),
        name=name, compiler_params=_params())(*ins)
    return dx, dg[0:1]


def _loss_head(name, y, target):
    n_tiles = SEQ // TM

    def body(y_ref, t_ref, dy_ref, loss_ref, acc_ref):
        i = pl.program_id(0)
        diff = y_ref[...] - t_ref[...]
        dy_ref[...] = diff * (1.0 / D_MODEL)
        part = jnp.sum((diff * diff).reshape(TM // 8, 8, D_MODEL), axis=0)

        @pl.when(i == 0)
        def _():
            acc_ref[...] = part

        @pl.when(i > 0)
        def _():
            acc_ref[...] += part

        @pl.when(i == n_tiles - 1)
        def _():
            loss_ref[...] = jnp.full((8, 128), jnp.sum(acc_ref[...]) * (0.5 / D_MODEL), F32)

    tile = pl.BlockSpec((TM, D_MODEL), lambda i: (i, 0))
    dy, loss = pl.pallas_call(
        body, out_shape=(jax.ShapeDtypeStruct((SEQ, D_MODEL), F32), jax.ShapeDtypeStruct((8, 128), F32)),
        grid=(n_tiles,), in_specs=[tile, tile], out_specs=(tile, pl.BlockSpec((8, 128), lambda i: (0, 0))),
        scratch_shapes=[pltpu.VMEM((8, D_MODEL), F32)], name=name, compiler_params=_params())(y, target)
    return dy, loss[0, 0]


def _row_index(shape):
    return lax.broadcasted_iota(jnp.int32, shape, 0)


def _lane_index(shape):
    return lax.broadcasted_iota(jnp.int32, shape, len(shape) - 1)


def _skew_rows(t, direction):
    q = _row_index(t.shape) & (GRID_W - 1)
    for bit in range(6):
        step = 1 << bit
        shift = step if direction > 0 else 128 - step
        t = jnp.where((q & step) != 0, pltpu.roll(t, shift, 1), t)
    return t


def _rpb_table(rpb_pad):
    rows = 16 * GRID_W

    def body(r_ref, t_ref):
        lane = _lane_index((rows, 128))
        v = pltpu.roll(r_ref[...], 128 - 15, 1)
        t = _skew_rows(jnp.broadcast_to(v[:, None, :], (16, GRID_W, 128)).reshape(rows, 128), +1)
        t = jnp.where(lane < GRID_W, t, 0.0)
        below = jnp.concatenate([t[GRID_W:], jnp.zeros((GRID_W, 128), F32)], axis=0)
        t_ref[...] = (t + pltpu.roll(below, GRID_W, 1)).reshape(16, GRID_W, 128)

    return pl.pallas_call(
        body, out_shape=jax.ShapeDtypeStruct((N_HEADS, 16, GRID_W, 128), F32), grid=(N_HEADS,),
        in_specs=[pl.BlockSpec((None, 16, 128), lambda h: (h, 0, 0))],
        out_specs=pl.BlockSpec((None, 16, GRID_W, 128), lambda h: (h, 0, 0, 0)),
        name="rpb_table", compiler_params=_params())(rpb_pad)


def _rpb_grad(gp):
    rows = 16 * GRID_W

    def body(g_ref, o_ref):
        lane = _lane_index((rows, 128))
        g = g_ref[...].reshape(rows, 128)
        low = jnp.where(lane < GRID_W, g, 0.0)
        high = pltpu.roll(jnp.where(lane >= GRID_W, g, 0.0), GRID_W, 1)
        above = jnp.concatenate([jnp.zeros((GRID_W, 128), F32), high[:rows - GRID_W]], axis=0)
        diag = jnp.sum(_skew_rows(low + above, -1).reshape(16, GRID_W, 128), axis=1)
        o_ref[...] = pltpu.roll(diag, 15, 1)

    return pl.pallas_call(
        body, out_shape=jax.ShapeDtypeStruct((N_HEADS, 16, 128), F32), grid=(N_HEADS,),
        in_specs=[pl.BlockSpec((None, 16, GRID_W, 128), lambda h: (h, 0, 0, 0))],
        out_specs=pl.BlockSpec((None, 16, 128), lambda h: (h, 0, 0)),
        name="rpb_grad", compiler_params=_params())(gp)


NA_KEYS = NA_ROWS * GRID_W


def _na_window(i):
    first_row = jnp.clip(i - NA_ROWS // 2, 0, SEQ_ROWS - NA_ROWS)
    return pl.multiple_of(first_row * GRID_W, GRID_W), first_row - i + NA_ROWS - 1


NA_STEP = 4


def _na_valid():
    q = _row_index((2 * GRID_W, NA_KEYS)) & (GRID_W - 1)
    k = _lane_index((2 * GRID_W, NA_KEYS)) & (GRID_W - 1)
    first_col = jnp.clip(q - 8, 0, GRID_W - 16)
    return (k >= first_col) & (k < first_col + 16)


def _head_masks():
    lane = _lane_index((1, 128))
    return (lane < HEAD_DIM, lane >= HEAD_DIM)


def _stack_heads(t, masks):
    zero = jnp.zeros_like(t)
    return jnp.concatenate([jnp.where(masks[0], t, zero), jnp.where(masks[1], t, zero)], axis=0)


def _unstack_heads(t, masks):
    n = t.shape[0] // 2
    return jnp.where(masks[0], t[:n], t[n:])


def _stack_columns(t):
    return jnp.concatenate([t[:, 0:1], t[:, HEAD_DIM:HEAD_DIM + 1]], axis=0)


def _na_scores(qs, kw, tp_ref, dr0, valid):
    s = lax.dot_general(qs, kw, NT, preferred_element_type=F32)
    bias = jnp.concatenate(
        [jnp.concatenate([tp_ref[a, pl.ds(dr0 + 2 * c, 1), :, :].reshape(GRID_W, 128) for c in range(4)], axis=1)
         for a in range(2)], axis=0)
    return jnp.where(valid, s + bias, NEG_INF)


def _na_specs():
    q_spec = pl.BlockSpec((NA_STEP * GRID_W, 128), lambda hp, i: (i, hp))
    k_spec = pl.BlockSpec((SEQ, 128), lambda hp, i: (0, 8 + hp))
    v_spec = pl.BlockSpec((SEQ, 128), lambda hp, i: (0, 16 + hp))
    tp_spec = pl.BlockSpec((2, 16, GRID_W, 128), lambda hp, i: (hp, 0, 0, 0))
    return q_spec, k_spec, v_spec, tp_spec


def _na_fwd(qkv, table):
    def body(q_ref, k_ref, v_ref, tp_ref, o_ref, lse_ref):
        valid = _na_valid()
        masks = _head_masks()
        for r in range(NA_STEP):
            rows = slice(r * GRID_W, (r + 1) * GRID_W)
            start, dr0 = _na_window(pl.program_id(1) * NA_STEP + r)
            kw = k_ref[pl.ds(start, NA_KEYS), :]
            vw = v_ref[pl.ds(start, NA_KEYS), :]
            s = _na_scores(_stack_heads(q_ref[rows, :] * Q_SCALE, masks), kw, tp_ref, dr0, valid)
            m = jnp.max(s, axis=-1, keepdims=True)
            p = jnp.exp(s - m)
            denom = jnp.sum(p, axis=-1, keepdims=True)
            out = jnp.dot(p.astype(BF16), vw, preferred_element_type=F32) / denom
            o_ref[rows, :] = _unstack_heads(out, masks).astype(o_ref.dtype)
            lse_ref[rows, :] = _unstack_heads(jnp.broadcast_to(m + jnp.log(denom), (2 * GRID_W, 128)), masks)

    q_spec, k_spec, v_spec, tp_spec = _na_specs()
    return pl.pallas_call(
        body, out_shape=(jax.ShapeDtypeStruct((SEQ, D_MODEL), BF16), jax.ShapeDtypeStruct((SEQ, D_MODEL), F32)),
        grid=(N_HEADS // 2, SEQ_ROWS // NA_STEP), in_specs=[q_spec, k_spec, v_spec, tp_spec],
        out_specs=(q_spec, q_spec), name="na_fwd", compiler_params=_params())(qkv, qkv, qkv, table)


def _na_bwd(qkv, table, d_out, lse):
    def body(q_ref, k_ref, v_ref, tp_ref, do_ref, lse_ref, dqkv_ref, gp_ref, dk_acc, dv_acc):
        step = pl.program_id(1)

        @pl.when(step == 0)
        def _():
            dk_acc[...] = jnp.zeros_like(dk_acc)
            dv_acc[...] = jnp.zeros_like(dv_acc)
            gp_ref[...] = jnp.zeros_like(gp_ref)

        valid = _na_valid()
        masks = _head_masks()
        for r in range(NA_STEP):
            rows = slice(r * GRID_W, (r + 1) * GRID_W)
            i = step * NA_STEP + r
            start, dr0 = _na_window(i)
            kw = k_ref[pl.ds(start, NA_KEYS), :]
            vw = v_ref[pl.ds(start, NA_KEYS), :]
            qs = _stack_heads(q_ref[rows, :] * Q_SCALE, masks)
            dos = _stack_heads(do_ref[rows, :], masks)
            p = jnp.exp(_na_scores(qs, kw, tp_ref, dr0, valid) - _stack_columns(lse_ref[rows, :]))
            dp = lax.dot_general(dos, vw, NT, preferred_element_type=F32)
            ds = p * (dp - jnp.sum(p * dp, axis=-1, keepdims=True))
            for a in range(2):
                for c in range(4):
                    gp_ref[a, pl.ds(dr0 + 2 * c, 1), :, :] += (
                        ds[a * GRID_W:(a + 1) * GRID_W, 128 * c:128 * (c + 1)].reshape(1, GRID_W, 128))
            dsb = ds.astype(BF16)
            dq = _unstack_heads(jnp.dot(dsb, kw, preferred_element_type=F32), masks) * Q_SCALE
            dqkv_ref[0, pl.ds(pl.multiple_of(i * GRID_W, GRID_W), GRID_W), :] = dq.astype(dqkv_ref.dtype)
            dk_acc[pl.ds(start, NA_KEYS), :] += lax.dot_general(dsb, qs, TN, preferred_element_type=F32)
            dv_acc[pl.ds(start, NA_KEYS), :] += lax.dot_general(p.astype(BF16), dos, TN, preferred_element_type=F32)

        @pl.when(step == SEQ_ROWS // NA_STEP - 1)
        def _():
            dqkv_ref[1] = dk_acc[...].astype(dqkv_ref.dtype)
            dqkv_ref[2] = dv_acc[...].astype(dqkv_ref.dtype)

    q_spec, k_spec, v_spec, tp_spec = _na_specs()
    return pl.pallas_call(
        body,
        out_shape=(jax.ShapeDtypeStruct((3, SEQ, D_MODEL), BF16), jax.ShapeDtypeStruct((N_HEADS, 16, GRID_W, 128), F32)),
        grid=(N_HEADS // 2, SEQ_ROWS // NA_STEP), in_specs=[q_spec, k_spec, v_spec, tp_spec, q_spec, q_spec],
        out_specs=(pl.BlockSpec((3, SEQ, 128), lambda hp, i: (0, 0, hp)), tp_spec),
        scratch_shapes=[pltpu.VMEM((SEQ, 128), F32), pltpu.VMEM((SEQ, 128), F32)],
        name="na_bwd", compiler_params=_params())(qkv, qkv, qkv, table, d_out, lse)


DIL_STEP = 4


def _dil_geometry(group):
    dil = DIL_GROUPS[group][1]
    sub_len = SEQ // dil
    blocks = sub_len // BAND
    return dil, sub_len, max(blocks // DIL_STEP, 1), max(DIL_STEP // blocks, 1), min(2 * BAND, sub_len)


def _dil_block(step, r, sub_len, subs):
    per_sub = DIL_STEP // subs
    return (r // per_sub) * sub_len, step * per_sub + r % per_sub


def _dil_window(b, sub_len, n_keys):
    if n_keys == sub_len:
        return 0
    return pl.multiple_of(jnp.clip(b * BAND - RADIUS, 0, sub_len - n_keys), RADIUS)


def _dil_bias(b, start, n_keys, slope_ref, hp):
    row = _row_index((2 * BAND, n_keys))
    qpos = b * BAND + (row & (BAND - 1))
    kpos = start + _lane_index((2 * BAND, n_keys))
    dist = jnp.abs(qpos - kpos)
    slope = jnp.where(row < BAND, slope_ref[2 * hp], slope_ref[2 * hp + 1])
    return slope * dist.astype(F32), dist <= RADIUS


def _dil_scores(qs, kw, penalty, valid):
    return jnp.where(valid, lax.dot_general(qs, kw, NT, preferred_element_type=F32) - penalty, NEG_INF)


def _dil_specs(group):
    dil, sub_len, steps, subs, _ = _dil_geometry(group)
    col = group * 24
    rows = DIL_STEP * BAND
    q_spec = pl.BlockSpec((rows, 128), lambda n, hp, b: (n * steps + b, col + hp))
    k_spec = pl.BlockSpec((subs * sub_len, 128), lambda n, hp, b: (n, col + 8 + hp))
    v_spec = pl.BlockSpec((subs * sub_len, 128), lambda n, hp, b: (n, col + 16 + hp))
    tile = pl.BlockSpec((rows, 128), lambda n, hp, b: (n * steps + b, hp))
    smem = pl.BlockSpec(memory_space=pltpu.SMEM)
    return (dil // subs, N_HEADS // 2, steps), q_spec, k_spec, v_spec, tile, smem


def _dil_fwd(group, qkv, slopes):
    _, sub_len, _, subs, n_keys = _dil_geometry(group)

    def body(q_ref, k_ref, v_ref, slope_ref, o_ref, lse_ref):
        hp = pl.program_id(1)
        masks = _head_masks()
        for r in range(DIL_STEP):
            rows = slice(r * BAND, (r + 1) * BAND)
            base, b = _dil_block(pl.program_id(2), r, sub_len, subs)
            start = _dil_window(b, sub_len, n_keys)
            kw = k_ref[pl.ds(base + start, n_keys), :]
            vw = v_ref[pl.ds(base + start, n_keys), :]
            penalty, valid = _dil_bias(b, start, n_keys, slope_ref, hp)
            s = _dil_scores(_stack_heads(q_ref[rows, :] * Q_SCALE, masks), kw, penalty, valid)
            m = jnp.max(s, axis=-1, keepdims=True)
            p = jnp.exp(s - m)
            denom = jnp.sum(p, axis=-1, keepdims=True)
            out = jnp.dot(p.astype(BF16), vw, preferred_element_type=F32) / denom
            o_ref[rows, :] = _unstack_heads(out, masks)
            lse_ref[rows, :] = _unstack_heads(jnp.broadcast_to(m + jnp.log(denom), (2 * BAND, 128)), masks)

    grid, q_spec, k_spec, v_spec, tile, smem = _dil_specs(group)
    return pl.pallas_call(
        body, out_shape=(jax.ShapeDtypeStruct((SEQ, D_MODEL), F32), jax.ShapeDtypeStruct((SEQ, D_MODEL), F32)),
        grid=grid, in_specs=[q_spec, k_spec, v_spec, smem], out_specs=(tile, tile),
        name=f"dil_fwd_{group}", compiler_params=_params())(qkv, qkv, qkv, slopes)


def _dil_merge(outs, lses):
    n_sorted = len(SORTED)

    def body(*refs):
        o_refs, l_refs = refs[:3], refs[3:6]
        out_refs, lse_refs, scratch = refs[6:7 + n_sorted], refs[7 + n_sorted:8 + 2 * n_sorted], refs[-1]
        os_ = [o_refs[0][...]] + [_unsort_tile(scratch, o_refs[1 + j], dil) for j, dil in enumerate(SORTED)]
        ls = [l_refs[0][...]] + [_unsort_tile(scratch, l_refs[1 + j], dil) for j, dil in enumerate(SORTED)]
        m = jnp.maximum(jnp.maximum(ls[0], ls[1]), ls[2])
        es = [jnp.exp(v - m) for v in ls]
        total = es[0] + es[1] + es[2]
        merged = (es[0] * os_[0] + es[1] * os_[1] + es[2] * os_[2]) / total
        lse = m + jnp.log(total)
        out_refs[0][...] = merged
        lse_refs[0][...] = lse
        for j, dil in enumerate(SORTED):
            _sort_tile(scratch, merged, dil, out_refs[1 + j])
            _sort_tile(scratch, lse, dil, lse_refs[1 + j])

    tm = 256
    tile = pl.BlockSpec((tm, D_MODEL), lambda i: (i, 0))
    specs = [tile] + [_sorted_spec(dil, tm=tm) for dil in SORTED]
    shapes = [jax.ShapeDtypeStruct((SEQ, D_MODEL), F32)] + [jax.ShapeDtypeStruct((dil, SEQ // dil, D_MODEL), F32) for dil in SORTED]
    views = lambda ts: [ts[0]] + [_sorted_view(t, dil) for t, dil in zip(ts[1:], SORTED)]
    result = pl.pallas_call(
        body, out_shape=tuple(shapes * 2), grid=(SEQ // tm,), in_specs=specs * 2, out_specs=tuple(specs * 2),
        scratch_shapes=[_sort_scratch(tm)], name="dil_merge", compiler_params=_params())(*views(outs), *views(lses))
    flat = [t.reshape(SEQ, D_MODEL) for t in result]
    return flat[:1 + n_sorted], flat[1 + n_sorted:]


def _dil_bwd(group, qkv, slopes, d_out, out, lse_group, lse_total, into):
    _, sub_len, steps, subs, n_keys = _dil_geometry(group)

    def body(q_ref, k_ref, v_ref, slope_ref, do_ref, o_ref, lg_ref, lt_ref, into_ref, dqkv_ref, dk_acc, dv_acc):
        hp, step = pl.program_id(1), pl.program_id(2)

        @pl.when(step == 0)
        def _():
            dk_acc[...] = jnp.zeros_like(dk_acc)
            dv_acc[...] = jnp.zeros_like(dv_acc)

        masks = _head_masks()
        for r in range(DIL_STEP):
            rows = slice(r * BAND, (r + 1) * BAND)
            base, b = _dil_block(step, r, sub_len, subs)
            start = _dil_window(b, sub_len, n_keys)
            keys = pl.ds(base + start, n_keys)
            kw = k_ref[keys, :]
            vw = v_ref[keys, :]
            penalty, valid = _dil_bias(b, start, n_keys, slope_ref, hp)
            qs = _stack_heads(q_ref[rows, :] * Q_SCALE, masks)
            lse2 = lg_ref[rows, :]
            weight = jnp.exp(lse2 - lt_ref[rows, :])
            do2 = do_ref[rows, :]
            dogs = _stack_heads((weight * do2).astype(BF16), masks)
            delta = _stack_columns(weight) * jnp.sum(_stack_heads(do2 * o_ref[rows, :], masks), axis=-1, keepdims=True)
            p = jnp.exp(_dil_scores(qs, kw, penalty, valid) - _stack_columns(lse2))
            dp = lax.dot_general(dogs, vw, NT, preferred_element_type=F32)
            dsb = (p * (dp - delta)).astype(BF16)
            dq = _unstack_heads(jnp.dot(dsb, kw, preferred_element_type=F32), masks) * Q_SCALE
            dqkv_ref[0, pl.ds(pl.multiple_of(base + b * BAND, BAND), BAND), :] = dq.astype(dqkv_ref.dtype)
            dk_acc[keys, :] += lax.dot_general(dsb, qs, TN, preferred_element_type=F32)
            dv_acc[keys, :] += lax.dot_general(p.astype(BF16), dogs, TN, preferred_element_type=F32)

        @pl.when(step == steps - 1)
        def _():
            dqkv_ref[1] = dk_acc[...].astype(dqkv_ref.dtype)
            dqkv_ref[2] = dv_acc[...].astype(dqkv_ref.dtype)

    grid, q_spec, k_spec, v_spec, tile, smem = _dil_specs(group)
    return pl.pallas_call(
        body, out_shape=jax.ShapeDtypeStruct(into.shape, into.dtype), grid=grid,
        in_specs=[q_spec, k_spec, v_spec, smem, tile, tile, tile, tile, ANY],
        out_specs=pl.BlockSpec((3, subs * sub_len, 128), lambda n, hp, b: (group, n, hp)),
        scratch_shapes=[pltpu.VMEM((subs * sub_len, 128), F32), pltpu.VMEM((subs * sub_len, 128), F32)],
        input_output_aliases={8: 0}, name=f"dil_bwd_{group}", compiler_params=_params(),
    )(qkv, qkv, qkv, slopes, d_out, out, lse_group, lse_total, into)


def _ffn_up(name, h, w_gu):
    def body(h_ref, wg_ref, wu_ref, act_ref, hg_ref, hu_ref, act_t_ref):
        hv = h_ref[...]
        hg = jnp.dot(hv, wg_ref[...], preferred_element_type=F32)
        hu = jnp.dot(hv, wu_ref[...], preferred_element_type=F32)
        act = hg * jax.nn.sigmoid(hg) * hu
        act_ref[...] = act.astype(act_ref.dtype)
        act_t_ref[...] = act.T.astype(act_t_ref.dtype)
        hg_ref[...] = hg.astype(hg_ref.dtype)
        hu_ref[...] = hu.astype(hu_ref.dtype)

    out = pl.BlockSpec((None, TM, FF_PAD), lambda d, i: (d, i, 0))
    shape = jax.ShapeDtypeStruct((N_DEV, SEQ, FF_PAD), BF16)
    return pl.pallas_call(
        body, out_shape=(shape, shape, shape, jax.ShapeDtypeStruct((N_DEV, FF_PAD, SEQ), BF16)), grid=(N_DEV, SEQ // TM),
        in_specs=[pl.BlockSpec((TM, D_MODEL), lambda d, i: (i, 0)),
                  pl.BlockSpec((None, None, D_MODEL, FF_PAD), lambda d, i: (d, 0, 0, 0)),
                  pl.BlockSpec((None, None, D_MODEL, FF_PAD), lambda d, i: (d, 1, 0, 0))],
        out_specs=(out, out, out, pl.BlockSpec((None, FF_PAD, TM), lambda d, i: (d, 0, i))),
        name=name, compiler_params=_params())(h, w_gu, w_gu)


def _ffn_bwd_act(name, d_f, w_down, hg, hu):
    def body(df_ref, wd_ref, hg_ref, hu_ref, dgu_ref):
        dact = lax.dot_general(df_ref[...], wd_ref[...], NT, preferred_element_type=F32)
        hgv = hg_ref[...].astype(F32)
        sig = jax.nn.sigmoid(hgv)
        dgu_ref[0] = (dact * hu_ref[...].astype(F32) * (sig * (1.0 + hgv * (1.0 - sig)))).astype(dgu_ref.dtype)
        dgu_ref[1] = (dact * hgv * sig).astype(dgu_ref.dtype)

    tile = pl.BlockSpec((None, TM, FF_PAD), lambda d, i: (d, i, 0))
    return pl.pallas_call(
        body, out_shape=jax.ShapeDtypeStruct((2, N_DEV, SEQ, FF_PAD), BF16), grid=(N_DEV, SEQ // TM),
        in_specs=[pl.BlockSpec((TM, D_MODEL), lambda d, i: (i, 0)),
                  pl.BlockSpec((None, FF_PAD, D_MODEL), lambda d, i: (d, 0, 0)), tile, tile],
        out_specs=pl.BlockSpec((2, None, TM, FF_PAD), lambda d, i: (0, d, i, 0)),
        name=name, compiler_params=_params())(d_f, w_down, hg, hu)


def _position():
    return lax.axis_index("x"), lax.axis_index("y"), lax.axis_index("c")


def _flat(p):
    return 4 * p[0] + 2 * p[1] + p[2]


def _peer(me, k):
    x, y, c = me
    return (1 - x if k & 4 else x, 1 - y if k & 2 else y, 1 - c if k & 1 else c)


def _columns(width):
    return lambda ref, d: ref.at[:, pl.ds(pl.multiple_of(d * width, 128), width)]


def _leading(ref, d):
    return ref.at[d]


def _whole(ref, d):
    return ref


def _by_sender(window):
    return lambda ref, sender, k: window(ref, sender)


def _by_distance(ref, sender, k):
    return ref.at[k - 1]


def _prep_weights(me, na_qkv, na_o, dil_qkv, dil_o, gate, up, down, land_shapes):
    na_cols, dil_cols = na_qkv.shape[-1], dil_qkv.shape[-1]
    o_rows = na_o.shape[1]
    tiles = 4
    rows, rows_o = D_MODEL // tiles, o_rows // tiles

    def body(me_ref, naq, nao, dq, do_, g0, u0, d0, g1, u1, d1, *outs):
        def put(t, index, value):
            outs[t][index] = value
            outs[8 + t][index] = value

        put(0, ..., naq[...].astype(BF16))
        put(1, ..., nao[...].astype(BF16))
        put(4, ..., dq[...].astype(BF16))
        put(5, ..., do_[...].astype(BF16))
        for t, (g, u, d) in ((2, (g0, u0, d0)), (6, (g1, u1, d1))):
            for j, part in enumerate((g, u)):
                put(t, (j, slice(None), slice(0, FF_SHARD)), part[...].astype(BF16))
                put(t, (j, slice(None), slice(FF_SHARD, FF_PAD)), jnp.zeros((rows, FF_PAD - FF_SHARD), BF16))
            put(t + 1, (slice(0, FF_SHARD), slice(None)), d[...].astype(BF16))
            put(t + 1, (slice(FF_SHARD, FF_PAD), slice(None)), jnp.zeros((FF_PAD - FF_SHARD, D_MODEL), BF16))

    def tiled(width):
        return pl.BlockSpec((None, rows, width), lambda i, me: (0, i, 0))

    def layer(l, width):
        return pl.BlockSpec((None, rows, width), lambda i, me: (l, i, 0))

    def whole_layer(l):
        return pl.BlockSpec((None, FF_SHARD, D_MODEL), lambda i, me: (l, 0, 0))

    in_specs = [tiled(na_cols), pl.BlockSpec((None, rows_o, D_MODEL), lambda i, me: (0, i, 0)), tiled(dil_cols),
                pl.BlockSpec((None, rows_o, D_MODEL), lambda i, me: (0, i, 0)),
                layer(0, FF_SHARD), layer(0, FF_SHARD), whole_layer(0), layer(1, FF_SHARD), layer(1, FF_SHARD), whole_layer(1)]
    o_shard = pl.BlockSpec((rows_o, D_MODEL), lambda i, me: (i, 0))
    o_land = pl.BlockSpec((None, rows_o, D_MODEL), lambda i, me: (me[0], i, 0))
    gu_shard = pl.BlockSpec((2, rows, FF_PAD), lambda i, me: (0, i, 0))
    gu_land = pl.BlockSpec((None, 2, rows, FF_PAD), lambda i, me: (me[0], 0, i, 0))
    down_shard = pl.BlockSpec((FF_PAD, D_MODEL), lambda i, me: (0, 0))
    down_land = pl.BlockSpec((None, FF_PAD, D_MODEL), lambda i, me: (me[0], 0, 0))

    def qkv_shard(width):
        return pl.BlockSpec((rows, width), lambda i, me: (i, 0))

    def qkv_land(width):
        return pl.BlockSpec((rows, width), lambda i, me: (i, me[0]))

    shard_specs = [qkv_shard(na_cols), o_shard, gu_shard, down_shard, qkv_shard(dil_cols), o_shard, gu_shard, down_shard]
    land_specs = [qkv_land(na_cols), o_land, gu_land, down_land, qkv_land(dil_cols), o_land, gu_land, down_land]
    shard_shapes = [jax.ShapeDtypeStruct(s, BF16) for s in
                    ((D_MODEL, na_cols), (o_rows, D_MODEL), (2, D_MODEL, FF_PAD), (FF_PAD, D_MODEL),
                     (D_MODEL, dil_cols), (o_rows, D_MODEL), (2, D_MODEL, FF_PAD), (FF_PAD, D_MODEL))]
    result = pl.pallas_call(
        body, out_shape=tuple(shard_shapes + list(land_shapes)),
        grid_spec=pltpu.PrefetchScalarGridSpec(num_scalar_prefetch=1, grid=(tiles,), in_specs=in_specs,
                                               out_specs=tuple(shard_specs + land_specs)),
        name="prep_weights", compiler_params=_params())(me, na_qkv, na_o, dil_qkv, dil_o, gate, up, down, gate, up, down)
    return list(result[:8]), list(result[8:])


def _remote_copies(sets, src_refs, land_refs, send_sems, recv_sems, outgoing):
    me = _position()
    copies = []
    for t, (si, src_of, li, dst_of) in enumerate(sets):
        for k in range(1, N_DEV):
            other = _peer(me, k)
            sender = me if outgoing else other
            copies.append(pltpu.make_async_remote_copy(
                src_ref=src_of(src_refs[si], _flat(other)), dst_ref=dst_of(land_refs[li], _flat(sender), k),
                send_sem=send_sems.at[(N_DEV - 1) * t + k - 1], recv_sem=recv_sems.at[(N_DEV - 1) * t + k - 1],
                device_id=other, device_id_type=MESH))
    return copies


def _send_start(name, srcs, lands, sets_by_group):
    n_src, n_land, n_groups = len(srcs), len(lands), len(sets_by_group)

    def body(*refs):
        src_refs, land_refs = refs[:n_src], refs[n_src:n_src + n_land]
        outs = refs[n_src + n_land:]
        for g, sets in enumerate(sets_by_group):
            for cp in _remote_copies(sets, src_refs, land_refs, outs[2 * g], outs[2 * g + 1], True):
                cp.start()
        outs[-1][...] = jnp.zeros_like(outs[-1])

    sem_shapes = []
    for sets in sets_by_group:
        sem_shapes += [pltpu.SemaphoreType.DMA((len(sets) * (N_DEV - 1),))] * 2
    thru = [pltpu.HBM(a.shape, a.dtype) for a in list(srcs) + list(lands)]
    n_sem = len(sem_shapes)
    result = pl.pallas_call(
        body, out_shape=tuple(sem_shapes + thru + [jax.ShapeDtypeStruct((8, 128), F32)]),
        in_specs=[HBM] * (n_src + n_land),
        out_specs=tuple([SEM] * n_sem + [HBM] * (n_src + n_land) + [pl.BlockSpec(memory_space=pltpu.VMEM)]),
        input_output_aliases={i: n_sem + i for i in range(n_src + n_land)},
        compiler_params=pltpu.CompilerParams(has_side_effects=EFFECT), name=name,
    )(*[pltpu.with_memory_space_constraint(a, pltpu.HBM) for a in list(srcs) + list(lands)])
    sems = [(result[2 * g], result[2 * g + 1]) for g in range(n_groups)]
    return sems, list(result[n_sem:n_sem + n_src]), list(result[n_sem + n_src:n_sem + n_src + n_land]), result[-1]


def _send_wait(name, sems, srcs, lands, sets, after):
    n_src, n_land = len(srcs), len(lands)

    def body(*refs):
        src_refs, land_refs = refs[:n_src], refs[n_src:n_src + n_land]
        send_sems, recv_sems = refs[n_src + n_land], refs[n_src + n_land + 1]
        for cp in _remote_copies(sets, src_refs, land_refs, send_sems, recv_sems, True):
            cp.wait_send()
        for cp in _remote_copies(sets, src_refs, land_refs, send_sems, recv_sems, False):
            cp.wait_recv()

    thru = [pltpu.HBM(a.shape, a.dtype) for a in list(srcs) + list(lands)]
    result = pl.pallas_call(
        body, out_shape=tuple(thru), in_specs=[HBM] * (n_src + n_land) + [SEM, SEM] + [ANY] * len(after),
        out_specs=tuple([HBM] * (n_src + n_land)), input_output_aliases={i: i for i in range(n_src + n_land)},
        compiler_params=pltpu.CompilerParams(has_side_effects=EFFECT), name=name,
    )(*srcs, *lands, sems[0], sems[1], *after)
    return list(result[:n_src]), list(result[n_src:])


DIRECT = (1, 2, 4, 6)
PASSED = DIRECT[1:]


def _hbm_passthrough(body, name, arrays, n_sem_in, sem_out_shapes, extra):
    n, n_out = len(arrays), len(sem_out_shapes)
    return pl.pallas_call(
        body, out_shape=tuple(list(sem_out_shapes) + [pltpu.HBM(a.shape, a.dtype) for a in arrays]),
        in_specs=[HBM] * n + [SEM] * n_sem_in + [ANY] * len(extra), out_specs=tuple([SEM] * n_out + [HBM] * n),
        input_output_aliases={i: n_out + i for i in range(n)},
        compiler_params=pltpu.CompilerParams(has_side_effects=EFFECT), name=name)


def _shard_copy(src_ref, land_ref, window, block, to, send_sem, recv_sem, from_shard):
    dst = window(land_ref, _flat(block))
    return pltpu.make_async_remote_copy(src_ref=src_ref if from_shard else dst, dst_ref=dst, send_sem=send_sem,
                                        recv_sem=recv_sem, device_id=to, device_id_type=MESH)


def _gather_start(name, shards, lands, windows, group_sizes):
    n = len(shards)

    def body(*refs):
        shard_refs, land_refs, outs = refs[:n], refs[n:2 * n], refs[2 * n:]
        me = _position()
        t = 0
        for g, size in enumerate(group_sizes):
            for local in range(size):
                for j, k in enumerate(DIRECT):
                    i = len(DIRECT) * local + j
                    _shard_copy(shard_refs[t], land_refs[t], windows[t], me, _peer(me, k), outs[2 * g].at[i],
                                outs[2 * g + 1].at[i], True).start()
                t += 1

    sem_shapes = [pltpu.SemaphoreType.DMA((len(DIRECT) * size,)) for size in group_sizes for _ in range(2)]
    arrays = [pltpu.with_memory_space_constraint(a, pltpu.HBM) for a in list(shards) + list(lands)]
    result = _hbm_passthrough(body, name, arrays, 0, sem_shapes, ())(*arrays)
    n_sem = len(sem_shapes)
    sems = [(result[2 * g], result[2 * g + 1]) for g in range(len(group_sizes))]
    return sems, list(result[n_sem:n_sem + n]), list(result[n_sem + n:])


def _gather_pass_on(name, sems, shards, lands, windows, after):
    n = len(shards)

    def body(*refs):
        shard_refs, land_refs = refs[:n], refs[n:2 * n]
        recv_sems = refs[2 * n + 1]
        pass_send, pass_recv = refs[2 * n + 2 + len(after)], refs[2 * n + 3 + len(after)]
        me = _position()
        sibling = _peer(me, 1)
        for t in range(n):
            for j, k in enumerate(PASSED):
                sender = _peer(me, k)
                arrived = len(DIRECT) * t + 1 + j
                _shard_copy(shard_refs[t], land_refs[t], windows[t], sender, me, refs[2 * n].at[arrived], recv_sems.at[arrived],
                            True).wait_recv()
                i = len(PASSED) * t + j
                _shard_copy(shard_refs[t], land_refs[t], windows[t], sender, sibling, pass_send.at[i], pass_recv.at[i],
                            False).start()

    sem_shapes = [pltpu.SemaphoreType.DMA((len(PASSED) * n,))] * 2
    result = _hbm_passthrough(body, name, list(shards) + list(lands), 2, sem_shapes, after)(
        *shards, *lands, sems[0], sems[1], *after)
    return (result[0], result[1]), list(result[2:2 + n]), list(result[2 + n:])


def _gather_wait(name, sems, pass_sems, shards, lands, windows, after):
    n = len(shards)

    def body(*refs):
        shard_refs, land_refs = refs[:n], refs[n:2 * n]
        send_sems, recv_sems, pass_send, pass_recv = refs[2 * n:2 * n + 4]
        me = _position()
        sibling = _peer(me, 1)
        for t in range(n):
            for j, k in enumerate(DIRECT):
                i = len(DIRECT) * t + j
                _shard_copy(shard_refs[t], land_refs[t], windows[t], me, _peer(me, k), send_sems.at[i], recv_sems.at[i],
                            True).wait_send()
            _shard_copy(shard_refs[t], land_refs[t], windows[t], sibling, me, send_sems.at[len(DIRECT) * t],
                        recv_sems.at[len(DIRECT) * t], True).wait_recv()
            for j, k in enumerate(PASSED):
                i = len(PASSED) * t + j
                _shard_copy(shard_refs[t], land_refs[t], windows[t], _peer(me, k), sibling, pass_send.at[i], pass_recv.at[i],
                            False).wait_send()
                _shard_copy(shard_refs[t], land_refs[t], windows[t], _peer(sibling, k), me, pass_send.at[i], pass_recv.at[i],
                            False).wait_recv()

    result = _hbm_passthrough(body, name, list(shards) + list(lands), 4, [], after)(
        *shards, *lands, sems[0], sems[1], pass_sems[0], pass_sems[1], *after)
    return list(result[n:])


def _all_gather(name, locals_, out_shapes, windows, deps=()):
    n = len(locals_)

    def body(*refs):
        src_refs, out_refs = refs[:n], refs[n + len(deps):2 * n + len(deps)]
        send_sems, recv_sems, local_sems = refs[2 * n + len(deps):]
        x, y, c = _position()
        me, sibling = (x, y, c), (x, y, 1 - c)
        chips = [(1 - x, y), (x, 1 - y), (1 - x, 1 - y)]

        def copy(t, k, block, to, from_local=False):
            dst = windows[t](out_refs[t], _flat(block))
            return pltpu.make_async_remote_copy(
                src_ref=src_refs[t] if from_local else dst, dst_ref=dst, send_sem=send_sems.at[t, k],
                recv_sem=recv_sems.at[t, k], device_id=to, device_id_type=MESH)

        mine = [pltpu.make_async_copy(src_refs[t], windows[t](out_refs[t], _flat(me)), local_sems.at[t]) for t in range(n)]
        sends = []
        for t in range(n):
            mine[t].start()
            sends.append(copy(t, 0, me, sibling, True))
            sends += [copy(t, 1 + j, me, (*chip, c), True) for j, chip in enumerate(chips)]
        for cp in sends:
            cp.start()
        for t in range(n):
            for j, chip in enumerate(chips):
                copy(t, 1 + j, (*chip, c), me).wait_recv()
                passed = copy(t, 4 + j, (*chip, c), sibling)
                passed.start()
                sends.append(passed)
        for t in range(n):
            copy(t, 0, sibling, me).wait_recv()
            for j, chip in enumerate(chips):
                copy(t, 4 + j, (*chip, 1 - c), me).wait_recv()
        for cp in sends:
            cp.wait_send()
        for cp in mine:
            cp.wait()

    return pl.pallas_call(
        body, out_shape=tuple(out_shapes), in_specs=[ANY] * (n + len(deps)), out_specs=tuple([ANY] * n),
        scratch_shapes=[pltpu.SemaphoreType.DMA((n, 7)), pltpu.SemaphoreType.DMA((n, 7)), pltpu.SemaphoreType.DMA((n,))],
        name=name)(*locals_, *deps)


def _adamw(name, me, lands, owns, w, m, v, *, grid, land_specs, own_specs, p_spec):
    n_land = len(lands)

    def body(me_ref, *refs):
        land_refs, own_refs = refs[:n_land], refs[n_land:n_land + len(owns)]
        w_ref, m_ref, v_ref, g_ref, delta_ref, m_out, v_out = refs[n_land + len(owns):]
        ncols = w_ref.shape[-1]
        sums = []
        for i, land_ref in enumerate(land_refs):
            g = own_refs[i][...].astype(F32) if owns else land_ref[0].astype(F32)
            for s in range(0 if owns else 1, land_ref.shape[0]):
                g = g + land_ref[s].astype(F32)
            sums.append(g[:, :ncols])
        g = sums[0] if n_land == 1 else jnp.where(pl.program_id(0) == 0, sums[0], sums[1])
        m_new = ADAM_B1 * m_ref[...] + (1.0 - ADAM_B1) * g
        v_new = ADAM_B2 * v_ref[...] + (1.0 - ADAM_B2) * jnp.square(g)
        m_hat = m_new / (1.0 - ADAM_B1 ** ADAM_STEP)
        v_hat = v_new / (1.0 - ADAM_B2 ** ADAM_STEP)
        g_ref[...] = g
        delta_ref[...] = -ADAM_LR * (m_hat / (jnp.sqrt(v_hat) + ADAM_EPS) + ADAM_WD * w_ref[...])
        m_out[...] = m_new
        v_out[...] = v_new

    shape = jax.ShapeDtypeStruct(w.shape, F32)
    return pl.pallas_call(
        body, out_shape=(shape,) * 4,
        grid_spec=pltpu.PrefetchScalarGridSpec(
            num_scalar_prefetch=1, grid=grid, in_specs=list(land_specs) + list(own_specs) + [p_spec, p_spec, p_spec],
            out_specs=(p_spec,) * 4),
        name=name, compiler_params=_params())(me, *lands, *owns, w, m, v)


def _row(p, layer):
    return p[layer][None, :]


def _square(name, a, b, dims, out_dtype, deps=()):
    if a.shape == (D_MODEL, SEQ):
        return _matmul(name, a, b, grid=(2, 1), a_spec=pl.BlockSpec((512, SEQ), lambda i, k: (i, 0)),
                       b_spec=pl.BlockSpec((SEQ, D_MODEL), lambda i, k: (0, 0)),
                       o_spec=pl.BlockSpec((512, D_MODEL), lambda i, k: (i, 0)),
                       out_shape=jax.ShapeDtypeStruct((D_MODEL, D_MODEL), out_dtype), dims=NN, acc_shape=(8, 128),
                       deps=deps)
    return _matmul(name, a, b, grid=(SEQ // TM, 1), a_spec=pl.BlockSpec((TM, D_MODEL), lambda i, k: (i, 0)),
                   b_spec=pl.BlockSpec((D_MODEL, D_MODEL), lambda i, k: (0, 0)),
                   o_spec=pl.BlockSpec((TM, D_MODEL), lambda i, k: (i, 0)),
                   out_shape=jax.ShapeDtypeStruct((SEQ, D_MODEL), out_dtype), dims=dims, acc_shape=(8, 128), deps=deps)


def _grouped_matmul(name, a_list, b, *, n_tiles, a_block, b_spec, o_spec, out_shape):
    n_groups = len(a_list)

    def a_spec(g):
        def index(j, i):
            mine = j // 3
            return (jnp.where(mine == g, i, jnp.where(mine < g, 0, n_tiles - 1)), 0)
        return pl.BlockSpec(a_block, index)

    def body(*refs):
        b_ref, o_ref = refs[n_groups], refs[n_groups + 1]
        mine = pl.program_id(0) // 3
        for g in range(n_groups):
            @pl.when(mine == g)
            def _(g=g):
                o_ref[...] = jnp.dot(refs[g][...], b_ref[...], preferred_element_type=F32).astype(o_ref.dtype)

    return pl.pallas_call(
        body, out_shape=out_shape, grid=(3 * n_groups, n_tiles), in_specs=[a_spec(g) for g in range(n_groups)] + [b_spec],
        out_specs=o_spec, name=name, compiler_params=_params())(*a_list, b)


def _qkv_fwd(name, hs, w):
    return _grouped_matmul(name, hs, w, n_tiles=SEQ // TM, a_block=(TM, D_MODEL),
                           b_spec=pl.BlockSpec((D_MODEL, D_MODEL), lambda j, i: (0, j)),
                           o_spec=pl.BlockSpec((TM, D_MODEL), lambda j, i: (i, j)),
                           out_shape=jax.ShapeDtypeStruct((SEQ, 3 * len(hs) * D_MODEL), BF16))


def _qkv_dw(name, hs_t, dqkv):
    return _grouped_matmul(name, hs_t, dqkv, n_tiles=2, a_block=(512, SEQ),
                           b_spec=pl.BlockSpec((None, SEQ, D_MODEL), lambda j, i: (j, 0, 0)),
                           o_spec=pl.BlockSpec((512, D_MODEL), lambda j, i: (i, j)),
                           out_shape=jax.ShapeDtypeStruct((D_MODEL, 3 * len(hs_t) * D_MODEL), BF16))


def _proj_do_sorted(name, d_a, w_o):
    def body(da_ref, w_ref, *refs):
        value = lax.dot_general(da_ref[...], w_ref[...], NT, preferred_element_type=F32)
        refs[0][...] = value
        for j, dil in enumerate(SORTED):
            _sort_tile(refs[-1], value, dil, refs[1 + j])

    tile = pl.BlockSpec((TM, D_MODEL), lambda i: (i, 0))
    shapes = [jax.ShapeDtypeStruct((SEQ, D_MODEL), F32)] + [jax.ShapeDtypeStruct((dil, SEQ // dil, D_MODEL), F32) for dil in SORTED]
    result = pl.pallas_call(
        body, out_shape=tuple(shapes), grid=(SEQ // TM,),
        in_specs=[tile, pl.BlockSpec((D_MODEL, D_MODEL), lambda i: (0, 0))],
        out_specs=tuple([tile] + [_sorted_spec(dil) for dil in SORTED]), scratch_shapes=[_sort_scratch()],
        name=name, compiler_params=_params())(d_a, w_o)
    return [t.reshape(SEQ, D_MODEL) for t in result]


def _qkv_dh(name, dqkv, w, n_chunks, deps):
    return _matmul(name, dqkv, w, grid=(n_chunks // 3, SEQ // TM, 1),
                   a_spec=pl.BlockSpec((3, TM, D_MODEL), lambda g, i, k: (g, i, 0)),
                   b_spec=pl.BlockSpec((D_MODEL, 3 * D_MODEL), lambda g, i, k: (0, g)),
                   o_spec=pl.BlockSpec((None, TM, D_MODEL), lambda g, i, k: (g, i, 0)),
                   out_shape=jax.ShapeDtypeStruct((n_chunks // 3, SEQ, D_MODEL), F32), dims=NT, acc_shape=(8, 128),
                   deps=deps, inner=3)


def _local_step(x, target, norms, rpb, fetch, emit, deps):
    mix_pre, mix_post, ffn_pre, ffn_post = norms
    slopes = 2.0 ** (-8.0 * jnp.arange(1, N_HEADS + 1, dtype=F32) / N_HEADS)
    rpb_pad = jnp.pad(rpb, ((0, 0), (0, 1), (0, 128 - 31)))
    saved = []

    for layer in range(2):
        tag = f"l{layer}"
        if layer == 0:
            h = _rms_fwd(tag + "_norm_mix", x, _row(mix_pre, layer), out_dtype=BF16, deps=deps)
            hs = [h]
            w_qkv, w_o = fetch("na", deps, [h])
            table = _rpb_table(rpb_pad)
            qkv = _qkv_fwd(tag + "_qkv", hs, w_qkv)
            o, lse = _na_fwd(qkv, table)
            mixer = (hs, qkv, o, lse, table)
        else:
            hs = [t.reshape(SEQ, D_MODEL) for t in
                  _rms_fwd(tag + "_norm_mix", x, _row(mix_pre, layer), out_dtype=BF16, sorted_too=True)]
            w_qkv, w_o = fetch("dil", [x], [hs[0]])
            qkv = _qkv_fwd(tag + "_qkv", hs, w_qkv)
            outs, lses = zip(*[_dil_fwd(g, qkv, slopes * dil) for g, (_, dil) in enumerate(DIL_GROUPS)])
            merged, lse_total = _dil_merge(outs, lses)
            o = merged[0]
            mixer = (hs, qkv, merged, lses, lse_total)
        a = _square(tag + "_proj", o, w_o, NN, F32)
        x1 = _rms_fwd(tag + "_post_mix", a, _row(mix_post, layer), res=x)
        h2 = _rms_fwd(tag + "_norm_ffn", x1, _row(ffn_pre, layer), out_dtype=BF16)
        w_gu, w_down = fetch(f"ffn{layer}", [a], [h2])
        act, hg, hu, act_t = _ffn_up(tag + "_ffn_up", h2, w_gu)
        f = _matmul(tag + "_ffn_down", act, w_down, grid=(SEQ // TM, N_DEV // 4),
                    a_spec=pl.BlockSpec((4, TM, FF_PAD), lambda i, k: (k, i, 0)),
                    b_spec=pl.BlockSpec((4, FF_PAD, D_MODEL), lambda i, k: (k, 0, 0)),
                    o_spec=pl.BlockSpec((TM, D_MODEL), lambda i, k: (i, 0)),
                    out_shape=jax.ShapeDtypeStruct((SEQ, D_MODEL), F32), dims=NN, acc_shape=(TM, D_MODEL), inner=4)
        x2 = _rms_fwd(tag + "_post_ffn", f, _row(ffn_post, layer), res=x1)
        transposed = ([t.T for t in hs], o.astype(BF16).T, h2.T, act_t)
        saved.append((x, mixer, a, x1, transposed, hg, hu, f, w_qkv, w_o, w_gu, w_down))
        x = x2

    dx, loss = _loss_head("loss_head", x, target)
    d_norm = {k: [None, None] for k in ("mix_pre", "mix_post", "ffn_pre", "ffn_post")}
    d_rpb = None

    for layer in (1, 0):
        tag = f"b{layer}"
        x0, mixer, a, x1, (h_t, o_t, h2_t, act_t), hg, hu, f, w_qkv, w_o, w_gu, w_down = saved[layer]
        d_f, d_norm["ffn_post"][layer] = _rms_bwd(tag + "_post_ffn", f, _row(ffn_post, layer), [dx], out_dtype=BF16)
        dgu = _ffn_bwd_act(tag + "_ffn_act", d_f, w_down, hg, hu)
        d_down = _matmul(
            tag + "_ffn_ddown", act_t, d_f, grid=(N_DEV, 1),
            a_spec=pl.BlockSpec((None, FF_PAD, SEQ), lambda d, k: (d, 0, 0)),
            b_spec=pl.BlockSpec((SEQ, D_MODEL), lambda d, k: (0, 0)),
            o_spec=pl.BlockSpec((None, FF_PAD, D_MODEL), lambda d, k: (d, 0, 0)),
            out_shape=jax.ShapeDtypeStruct((N_DEV, FF_PAD, D_MODEL), BF16), dims=NN, acc_shape=(8, 128))
        d_gu = _matmul(
            tag + "_ffn_dgu", h2_t, dgu, grid=(2, N_DEV, 1),
            a_spec=pl.BlockSpec((D_MODEL, SEQ), lambda t, d, k: (0, 0)),
            b_spec=pl.BlockSpec((None, None, SEQ, FF_PAD), lambda t, d, k: (t, d, 0, 0)),
            o_spec=pl.BlockSpec((None, None, D_MODEL, FF_PAD), lambda t, d, k: (d, t, 0, 0)),
            out_shape=jax.ShapeDtypeStruct((N_DEV, 2, D_MODEL, FF_PAD), BF16), dims=NN, acc_shape=(8, 128))
        sent = emit(f"ffn{layer}", [d_gu, d_down])
        d_h2 = _matmul(
            tag + "_ffn_dh", dgu, w_gu, grid=(SEQ // TM, 4),
            a_spec=pl.BlockSpec((None, 4, TM, FF_PAD), lambda i, k: (k // 2, k % 2, i, 0)),
            b_spec=pl.BlockSpec((4, None, D_MODEL, FF_PAD), lambda i, k: (k % 2, k // 2, 0, 0)),
            o_spec=pl.BlockSpec((TM, D_MODEL), lambda i, k: (i, 0)),
            out_shape=jax.ShapeDtypeStruct((SEQ, D_MODEL), F32), dims=NT, acc_shape=(TM, D_MODEL), deps=sent, inner=4)
        dx1, d_norm["ffn_pre"][layer] = _rms_bwd(tag + "_norm_ffn", x1, _row(ffn_pre, layer), [d_h2], res=dx)
        d_a, d_norm["mix_post"][layer] = _rms_bwd(tag + "_post_mix", a, _row(mix_post, layer), [dx1], out_dtype=BF16)
        d_wo = _square(tag + "_proj_dw", o_t, d_a, NN, BF16)
        if layer == 0:
            _, qkv, o, lse, table = mixer
            d_o = _square(tag + "_proj_do", d_a, w_o, NT, BF16)
            dqkv, gp = _na_bwd(qkv, table, d_o, lse)
            d_rpb = _rpb_grad(gp)[:, :15, :31]
            sent = emit("na", [_qkv_dw(tag + "_qkv_dw", h_t, dqkv), d_wo])
            d_h = _qkv_dh(tag + "_qkv_dh", dqkv, w_qkv, 3, sent)
            dx, d_norm["mix_pre"][layer] = _rms_bwd(tag + "_norm_mix", x0, _row(mix_pre, layer), [d_h[0]], res=dx1)
        else:
            _, qkv, merged, lses, lse_total = mixer
            d_o = _proj_do_sorted(tag + "_proj_do", d_a, w_o)
            dqkv = lax.empty((3 * len(DIL_GROUPS), SEQ, D_MODEL), BF16)
            for g, (_, dil) in enumerate(DIL_GROUPS):
                dqkv = _dil_bwd(g, qkv, slopes * dil, d_o[g], merged[g], lses[g], lse_total[g], dqkv)
            sent = emit("dil", [_qkv_dw(tag + "_qkv_dw", h_t, dqkv), d_wo])
            d_h = _qkv_dh(tag + "_qkv_dh", dqkv, w_qkv, 9, sent)
            dx, d_norm["mix_pre"][layer] = _rms_bwd(tag + "_norm_mix", x0, _row(mix_pre, layer), None, res=dx1, groups=d_h)

    d_gains = [jnp.concatenate(d_norm[k], axis=0) for k in ("mix_pre", "mix_post", "ffn_pre", "ffn_post")]
    return loss, dx, d_gains, d_rpb


RPB_SIZE = N_HEADS * 15 * 31


def _pack_small(gains, rpb, last=None):
    top = jnp.concatenate(gains, axis=0).reshape(64, 128)
    bottom = jnp.pad(rpb.reshape(-1), (0, 64 * 128 - RPB_SIZE))
    if last is not None:
        bottom = bottom + jnp.pad(last.reshape(1), (64 * 128 - 1, 0))
    return jnp.concatenate([top, bottom.reshape(64, 128)], axis=0)


def _unpack_small(p):
    gains = p[:64].reshape(4, 2, D_MODEL)
    rpb = p[64:].reshape(-1)[:RPB_SIZE].reshape(1, N_HEADS, 15, 31)
    return [gains[i] for i in range(4)], rpb


GROUPS = ("na", "ffn0", "dil", "ffn1")


def kernel(x, norm_mix_pre, norm_mix_post, norm_ffn_pre, norm_ffn_post, na_w_qkv, na_w_o, na_rpb, dil_w_qkv, dil_w_o, ffn_w_gate, ffn_w_up, ffn_w_down, loss_target, m_norm_mix_pre, m_norm_mix_post, m_norm_ffn_pre, m_norm_ffn_post, m_na_w_qkv, m_na_w_o, m_na_rpb, m_dil_w_qkv, m_dil_w_o, m_ffn_w_gate, m_ffn_w_up, m_ffn_w_down, v_norm_mix_pre, v_norm_mix_post, v_norm_ffn_pre, v_norm_ffn_post, v_na_w_qkv, v_na_w_o, v_na_rpb, v_dil_w_qkv, v_dil_w_o, v_ffn_w_gate, v_ffn_w_up, v_ffn_w_down):
    na_cols, dil_cols, o_rows = 3 * D_MODEL // N_DEV, 9 * D_MODEL // N_DEV, D_MODEL // N_DEV
    ff_pad = FF_PAD - FF_SHARD
    me = (4 * lax.axis_index("x") + 2 * lax.axis_index("y") + lax.axis_index("c")).astype(jnp.int32).reshape(1)

    full = {
        "na": [((D_MODEL, 3 * D_MODEL), _columns(na_cols)), ((N_DEV, o_rows, D_MODEL), _leading)],
        "dil": [((D_MODEL, 9 * D_MODEL), _columns(dil_cols)), ((N_DEV, o_rows, D_MODEL), _leading)],
        "ffn0": [((N_DEV, 2, D_MODEL, FF_PAD), _leading), ((N_DEV, FF_PAD, D_MODEL), _leading)],
        "ffn1": [((N_DEV, 2, D_MODEL, FF_PAD), _leading), ((N_DEV, FF_PAD, D_MODEL), _leading)],
    }
    block = {
        "na": [(D_MODEL, na_cols), (o_rows, D_MODEL)], "dil": [(D_MODEL, dil_cols), (o_rows, D_MODEL)],
        "ffn0": [(2, D_MODEL, FF_PAD), (FF_PAD, D_MODEL)], "ffn1": [(2, D_MODEL, FF_PAD), (FF_PAD, D_MODEL)],
    }

    land_shapes = [jax.ShapeDtypeStruct(full[g][t][0], BF16) for g in GROUPS for t in range(2)]
    windows = [full[g][t][1] for g in GROUPS for t in range(2)]
    shards, lands = _prep_weights(me, na_w_qkv, na_w_o, dil_w_qkv, dil_w_o, ffn_w_gate, ffn_w_up, ffn_w_down, land_shapes)
    sems, shards, lands = _gather_start("gather_start", shards, lands, windows, [2] * len(GROUPS))

    def fetch(group, early, late):
        gi = GROUPS.index(group)
        mine = slice(2 * gi, 2 * gi + 2)
        pass_sems, shards_g, lands_g = _gather_pass_on(f"gather_pass_{group}", sems[gi], shards[mine], lands[mine],
                                                       windows[mine], early)
        qkv, o = _gather_wait(f"gather_wait_{group}", sems[gi], pass_sems, shards_g, lands_g, windows[mine], late)
        return (qkv, o.reshape(D_MODEL, D_MODEL)) if group in ("na", "dil") else (qkv, o)

    def grad_source(group, t):
        return _columns(block[group][0][1]) if (group in ("na", "dil") and t == 0) else _leading

    in_flight = {}

    def emit(group, grads):
        if group in ("na", "dil"):
            grads = [grads[0], grads[1].reshape(N_DEV, o_rows, D_MODEL)]
        sets = [(t, grad_source(group, t), t, _by_distance) for t in range(2)]
        landing = [lax.empty((N_DEV - 1,) + block[group][t], BF16) for t in range(2)]
        sems_g, grads, landing, tok = _send_start(f"exchange_start_{group}", grads, landing, [sets])
        in_flight[group] = (sems_g[0], grads, landing, sets)
        return [tok]

    norms = (norm_mix_pre, norm_mix_post, norm_ffn_pre, norm_ffn_post)
    loss, grad_x, d_gains, d_rpb = _local_step(x[0], loss_target[0], norms, na_rpb[0], fetch, emit, [shards[0]])

    landed, sent = {}, {}

    def wait_for(group, after):
        sems_g, grads, landing, sets = in_flight[group]
        sent[group], landed[group] = _send_wait(f"exchange_wait_{group}", sems_g, grads, landing, sets, after)

    for group in ("ffn1", "dil", "ffn0"):
        wait_for(group, [grad_x])

    def one(rows, tile, ncols, columns):
        own = (pl.BlockSpec((tile, ncols), lambda i, me: (i, me[0])) if columns
               else pl.BlockSpec((None, tile, ncols), lambda i, me: (me[0], i, 0)))
        return dict(grid=(rows // tile,), land_specs=[pl.BlockSpec((N_DEV - 1, tile, ncols), lambda i, me: (0, i, 0))],
                    own_specs=[own], p_spec=pl.BlockSpec((None, tile, ncols), lambda i, me: (0, i, 0)))

    def layered(block_shape, index, p_block, n_tiles):
        def specs(lead_size, lead):
            shape = (lead_size,) + block_shape
            return [pl.BlockSpec(shape, lambda l, r, me: index(lead(me), jnp.where(l == 0, r, n_tiles - 1))),
                    pl.BlockSpec(shape, lambda l, r, me: index(lead(me), jnp.where(l == 0, 0, r)))]
        return dict(grid=(2, n_tiles), land_specs=specs(N_DEV - 1, lambda me: 0), own_specs=specs(None, lambda me: me[0]),
                    p_spec=pl.BlockSpec(p_block, lambda l, r, me: (l, r, 0)))

    gu_lands, gu_owns = [landed["ffn0"][0], landed["ffn1"][0]], [sent["ffn0"][0], sent["ffn1"][0]]
    down_lands, down_owns = [landed["ffn0"][1], landed["ffn1"][1]], [sent["ffn0"][1], sent["ffn1"][1]]
    updates = {
        "dil_w_qkv": _adamw("adamw_dil_qkv", me, [landed["dil"][0]], [sent["dil"][0]], dil_w_qkv, m_dil_w_qkv, v_dil_w_qkv,
                            **one(D_MODEL, 128, dil_cols, True)),
        "dil_w_o": _adamw("adamw_dil_o", me, [landed["dil"][1]], [sent["dil"][1]], dil_w_o, m_dil_w_o, v_dil_w_o,
                          **one(o_rows, o_rows, D_MODEL, False)),
        "ffn_w_gate": _adamw("adamw_gate", me, gu_lands, gu_owns, ffn_w_gate, m_ffn_w_gate, v_ffn_w_gate,
                             **layered((None, 128, FF_PAD), lambda lead, r: (lead, 0, r, 0), (None, 128, FF_SHARD), 8)),
        "ffn_w_up": _adamw("adamw_up", me, gu_lands, gu_owns, ffn_w_up, m_ffn_w_up, v_ffn_w_up,
                           **layered((None, 128, FF_PAD), lambda lead, r: (lead, 1, r, 0), (None, 128, FF_SHARD), 8)),
        "ffn_w_down": _adamw("adamw_down", me, down_lands, down_owns, ffn_w_down, m_ffn_w_down, v_ffn_w_down,
                             **layered((176, D_MODEL), lambda lead, r: (lead, r, 0), (None, 176, D_MODEL), 2)),
    }
    done = [u[0] for u in updates.values()]
    small = _all_gather("gather_small", [_pack_small(d_gains, d_rpb, loss)], [jax.ShapeDtypeStruct((N_DEV, 128, 128), F32)],
                        [_leading], deps=done)[0]
    wait_for("na", [small])
    updates["na_w_qkv"] = _adamw("adamw_na_qkv", me, [landed["na"][0]], [sent["na"][0]], na_w_qkv, m_na_w_qkv, v_na_w_qkv,
                                 **one(D_MODEL, 256, na_cols, True))
    updates["na_w_o"] = _adamw("adamw_na_o", me, [landed["na"][1]], [sent["na"][1]], na_w_o, m_na_w_o, v_na_w_o,
                               **one(o_rows, o_rows, D_MODEL, False))
    gains = [norm_mix_pre, norm_mix_post, norm_ffn_pre, norm_ffn_post]
    m_gains = [m_norm_mix_pre, m_norm_mix_post, m_norm_ffn_pre, m_norm_ffn_post]
    v_gains = [v_norm_mix_pre, v_norm_mix_post, v_norm_ffn_pre, v_norm_ffn_post]
    packed = _adamw("adamw_small", me, [small], (), _pack_small(gains, na_rpb)[None], _pack_small(m_gains, m_na_rpb)[None],
                    _pack_small(v_gains, v_na_rpb)[None], grid=(1,),
                    land_specs=[pl.BlockSpec((N_DEV, 128, 128), lambda i, me: (0, 0, 0))], own_specs=[],
                    p_spec=pl.BlockSpec((None, 128, 128), lambda i, me: (0, 0, 0)))
    small_out = [_unpack_small(p[0]) for p in packed]

    order = ["na_w_qkv", "na_w_o", "na_rpb", "dil_w_qkv", "dil_w_o", "ffn_w_gate", "ffn_w_up", "ffn_w_down"]
    result = [packed[0][0, 127, 127], grad_x[None]]
    for kind in range(4):
        gains_k, rpb_k = small_out[kind]
        result += gains_k
        result += [rpb_k if name == "na_rpb" else updates[name][kind] for name in order]
    return tuple(result)
```

```python
import functools

import jax
import jax.numpy as jnp
from jax import lax
from jax.experimental import pallas as pl
from jax.experimental.pallas import tpu as pltpu

F32 = jnp.float32
BF16 = jnp.bfloat16
MESH = pl.DeviceIdType.MESH
ANY = pl.BlockSpec(memory_space=pl.ANY)
HBM = pl.BlockSpec(memory_space=pltpu.HBM)
SEM = pl.BlockSpec(memory_space=pltpu.SEMAPHORE)
EFFECT = pltpu.SideEffectType.DATAFLOW_SIDE_EFFECTING

N_DEV = 8
SEQ = 2048
D_MODEL = 1024
N_HEADS = 16
HEAD_DIM = 64
GRID_W = 64
NA_ROWS = 8
SEQ_ROWS = SEQ // GRID_W
DIL_GROUPS = ((128, 1), (512, 4), (2048, 16))
BAND = 128
RADIUS = 64
FF_SHARD = 352
FF_PAD = 384
RMS_EPS = 1e-6
NEG_INF = -1e30
Q_SCALE = HEAD_DIM ** -0.5

ADAM_LR = 0.001
ADAM_B1 = 0.9
ADAM_B2 = 0.999
ADAM_EPS = 1e-08
ADAM_WD = 0.01
ADAM_STEP = 10

VMEM_LIMIT = 56 * 1024 * 1024
TM = 512
TM_MM = 1024

NN = (((1,), (0,)), ((), ()))
NT = (((1,), (1,)), ((), ()))
TN = (((0,), (0,)), ((), ()))


def _params():
    return pltpu.CompilerParams(vmem_limit_bytes=VMEM_LIMIT)


def _matmul(name, a, b, *, grid, a_spec, b_spec, o_spec, out_shape, dims, acc_shape, deps=(), inner=1):
    nk = grid[-1]
    kaxis = len(grid) - 1

    def body(a_ref, b_ref, *rest):
        o_ref, acc_ref = rest[-2], rest[-1]
        if inner == 1:
            part = lax.dot_general(a_ref[...].astype(BF16), b_ref[...].astype(BF16), dims, preferred_element_type=F32)
        elif len(b_ref.shape) == 2:
            a_all = jnp.concatenate([a_ref[j].astype(BF16) for j in range(inner)], axis=1)
            part = lax.dot_general(a_all, b_ref[...].astype(BF16), dims, preferred_element_type=F32)
        else:
            part = sum(lax.dot_general(a_ref[j].astype(BF16), b_ref[j].astype(BF16), dims, preferred_element_type=F32)
                       for j in range(inner))
        if nk == 1:
            o_ref[...] = part.astype(o_ref.dtype)
        else:
            k = pl.program_id(kaxis)

            @pl.when(k == 0)
            def _():
                acc_ref[...] = part

            @pl.when(k > 0)
            def _():
                acc_ref[...] += part

            @pl.when(k == nk - 1)
            def _():
                o_ref[...] = acc_ref[...].astype(o_ref.dtype)

    return pl.pallas_call(
        body, out_shape=out_shape, grid=grid, in_specs=[a_spec, b_spec] + [ANY] * len(deps), out_specs=o_spec,
        scratch_shapes=[pltpu.VMEM(acc_shape, F32)], name=name, compiler_params=_params())(a, b, *deps)


SORTED = tuple(d for _, d in DIL_GROUPS if d > 1)
LANE_CHUNKS = D_MODEL // 128


def _sort_scratch(tm=TM):
    return pltpu.VMEM((LANE_CHUNKS, tm, 128), F32)


def _sorted_view(t, dil):
    return t.reshape(dil, SEQ // dil, D_MODEL)


def _sorted_spec(dil, lead=(), tm=TM):
    return pl.BlockSpec((None,) * len(lead) + (dil, tm // dil, D_MODEL), lambda i: tuple(lead) + (0, i, 0))


def _sort_tile(scratch, value, dil, out_ref):
    tm = value.shape[0]
    for c in range(LANE_CHUNKS):
        scratch[c] = value[:, 128 * c:128 * (c + 1)]
    for r in range(dil):
        rows = [scratch.at[c][pl.ds(r, tm // dil, stride=dil), :] for c in range(LANE_CHUNKS)]
        out_ref[r] = jnp.concatenate(rows, axis=1).astype(out_ref.dtype)


def _unsort_tile(scratch, in_ref, dil):
    for r in range(dil):
        value = in_ref[r].astype(F32)
        for c in range(LANE_CHUNKS):
            scratch.at[c][pl.ds(r, value.shape[0], stride=dil), :] = value[:, 128 * c:128 * (c + 1)]
    return jnp.concatenate([scratch[c] for c in range(LANE_CHUNKS)], axis=1)


def _rms_fwd(name, x, g, res=None, out_dtype=F32, deps=(), sorted_too=False):
    n_tiles = SEQ // TM
    has_res = res is not None
    n_in = 2 + has_res + len(deps)

    def body(*refs):
        x_ref, g_ref = refs[0], refs[1]
        xv = x_ref[...]
        r = lax.rsqrt(jnp.mean(xv * xv, axis=-1, keepdims=True) + RMS_EPS)
        y = xv * r * g_ref[...]
        if has_res:
            y = refs[2][...] + y
        refs[n_in][...] = y.astype(out_dtype)
        if sorted_too:
            for j, dil in enumerate(SORTED):
                _sort_tile(refs[-1], y, dil, refs[n_in + 1 + j])

    tile = pl.BlockSpec((TM, D_MODEL), lambda i: (i, 0))
    gspec = pl.BlockSpec((1, D_MODEL), lambda i: (0, 0))
    ins = [x, g] + ([res] if has_res else []) + list(deps)
    specs = [tile, gspec] + ([tile] if has_res else []) + [ANY] * len(deps)
    shapes, out_specs = [jax.ShapeDtypeStruct((SEQ, D_MODEL), out_dtype)], [tile]
    if sorted_too:
        shapes += [jax.ShapeDtypeStruct((dil, SEQ // dil, D_MODEL), out_dtype) for dil in SORTED]
        out_specs += [_sorted_spec(dil) for dil in SORTED]
    result = pl.pallas_call(
        body, out_shape=tuple(shapes), grid=(n_tiles,), in_specs=specs, out_specs=tuple(out_specs),
        scratch_shapes=[_sort_scratch()] if sorted_too else [], name=name, compiler_params=_params())(*ins)
    return result if sorted_too else result[0]


def _rms_bwd(name, x, g, dys, res=None, out_dtype=F32, groups=None):
    n_tiles = SEQ // TM
    n_dy = len(dys) if groups is None else 1 + len(SORTED)
    has_res = res is not None

    def body(*refs):
        x_ref, g_ref = refs[0], refs[1]
        dy_refs = refs[2:2 + n_dy]
        res_ref = refs[2 + n_dy] if has_res else None
        dx_ref, dg_ref, acc_ref = refs[2 + n_dy + has_res:5 + n_dy + has_res]
        i = pl.program_id(0)
        xv = x_ref[...]
        r = lax.rsqrt(jnp.mean(xv * xv, axis=-1, keepdims=True) + RMS_EPS)
        xn = xv * r
        dy = dy_refs[0][...].astype(F32)
        for j, extra in enumerate(dy_refs[1:]):
            dy = dy + (extra[...].astype(F32) if groups is None else _unsort_tile(refs[-1], extra, SORTED[j]))
        dyg = dy * g_ref[...]
        dx = r * (dyg - xn * jnp.mean(dyg * xn, axis=-1, keepdims=True))
        if has_res:
            dx = res_ref[...] + dx
        dx_ref[...] = dx.astype(dx_ref.dtype)
        part = jnp.sum((dy * xn).reshape(TM // 8, 8, D_MODEL), axis=0)

        @pl.when(i == 0)
        def _():
            acc_ref[...] = part

        @pl.when(i > 0)
        def _():
            acc_ref[...] += part

        @pl.when(i == n_tiles - 1)
        def _():
            dg_ref[...] = jnp.broadcast_to(jnp.sum(acc_ref[...], axis=0, keepdims=True), (8, D_MODEL))

    tile = pl.BlockSpec((TM, D_MODEL), lambda i: (i, 0))
    gspec = pl.BlockSpec((1, D_MODEL), lambda i: (0, 0))
    if groups is None:
        dy_ins, dy_specs = list(dys), [tile] * n_dy
    else:
        dy_ins = [groups] + [groups.reshape(n_dy, dil, SEQ // dil, D_MODEL) for dil in SORTED]
        dy_specs = [pl.BlockSpec((None, TM, D_MODEL), lambda i: (0, i, 0))]
        dy_specs += [_sorted_spec(dil, lead=(1 + j,)) for j, dil in enumerate(SORTED)]
    ins = [x, g] + dy_ins + ([res] if has_res else [])
    specs = [tile, gspec] + dy_specs + ([tile] if has_res else [])
    dx, dg = pl.pallas_call(
        body, out_shape=(jax.ShapeDtypeStruct((SEQ, D_MODEL), out_dtype), jax.ShapeDtypeStruct((8, D_MODEL), F32)),
        grid=(n_tiles,), in_specs=specs,
        out_specs=(tile, pl.BlockSpec((8, D_MODEL), lambda i: (0, 0))),
        scratch_shapes=[pltpu.VMEM((8, D_MODEL), F32)] + ([_sort_scratch()] if groups is not None else []),
        name=name, compiler_params=_params())(*ins)
    return dx, dg[0:1]


def _loss_head(name, y, target):
    n_tiles = SEQ // TM

    def body(y_ref, t_ref, dy_ref, loss_ref, acc_ref):
        i = pl.program_id(0)
        diff = y_ref[...] - t_ref[...]
        dy_ref[...] = diff * (1.0 / D_MODEL)
        part = jnp.sum((diff * diff).reshape(TM // 8, 8, D_MODEL), axis=0)

        @pl.when(i == 0)
        def _():
            acc_ref[...] = part

        @pl.when(i > 0)
        def _():
            acc_ref[...] += part

        @pl.when(i == n_tiles - 1)
        def _():
            loss_ref[...] = jnp.full((8, 128), jnp.sum(acc_ref[...]) * (0.5 / D_MODEL), F32)

    tile = pl.BlockSpec((TM, D_MODEL), lambda i: (i, 0))
    dy, loss = pl.pallas_call(
        body, out_shape=(jax.ShapeDtypeStruct((SEQ, D_MODEL), F32), jax.ShapeDtypeStruct((8, 128), F32)),
        grid=(n_tiles,), in_specs=[tile, tile], out_specs=(tile, pl.BlockSpec((8, 128), lambda i: (0, 0))),
        scratch_shapes=[pltpu.VMEM((8, D_MODEL), F32)], name=name, compiler_params=_params())(y, target)
    return dy, loss[0, 0]


def _row_index(shape):
    return lax.broadcasted_iota(jnp.int32, shape, 0)


def _lane_index(shape):
    return lax.broadcasted_iota(jnp.int32, shape, len(shape) - 1)


def _skew_rows(t, direction):
    q = _row_index(t.shape) & (GRID_W - 1)
    for bit in range(6):
        step = 1 << bit
        shift = step if direction > 0 else 128 - step
        t = jnp.where((q & step) != 0, pltpu.roll(t, shift, 1), t)
    return t


def _rpb_table(rpb_pad):
    rows = 16 * GRID_W

    def body(r_ref, t_ref):
        lane = _lane_index((rows, 128))
        v = pltpu.roll(r_ref[...], 128 - 15, 1)
        t = _skew_rows(jnp.broadcast_to(v[:, None, :], (16, GRID_W, 128)).reshape(rows, 128), +1)
        t = jnp.where(lane < GRID_W, t, 0.0)
        below = jnp.concatenate([t[GRID_W:], jnp.zeros((GRID_W, 128), F32)], axis=0)
        t_ref[...] = (t + pltpu.roll(below, GRID_W, 1)).reshape(16, GRID_W, 128)

    return pl.pallas_call(
        body, out_shape=jax.ShapeDtypeStruct((N_HEADS, 16, GRID_W, 128), F32), grid=(N_HEADS,),
        in_specs=[pl.BlockSpec((None, 16, 128), lambda h: (h, 0, 0))],
        out_specs=pl.BlockSpec((None, 16, GRID_W, 128), lambda h: (h, 0, 0, 0)),
        name="rpb_table", compiler_params=_params())(rpb_pad)


def _rpb_grad(gp):
    rows = 16 * GRID_W

    def body(g_ref, o_ref):
        lane = _lane_index((rows, 128))
        g = g_ref[...].reshape(rows, 128)
        low = jnp.where(lane < GRID_W, g, 0.0)
        high = pltpu.roll(jnp.where(lane >= GRID_W, g, 0.0), GRID_W, 1)
        above = jnp.concatenate([jnp.zeros((GRID_W, 128), F32), high[:rows - GRID_W]], axis=0)
        diag = jnp.sum(_skew_rows(low + above, -1).reshape(16, GRID_W, 128), axis=1)
        o_ref[...] = pltpu.roll(diag, 15, 1)

    return pl.pallas_call(
        body, out_shape=jax.ShapeDtypeStruct((N_HEADS, 16, 128), F32), grid=(N_HEADS,),
        in_specs=[pl.BlockSpec((None, 16, GRID_W, 128), lambda h: (h, 0, 0, 0))],
        out_specs=pl.BlockSpec((None, 16, 128), lambda h: (h, 0, 0)),
        name="rpb_grad", compiler_params=_params())(gp)


NA_KEYS = NA_ROWS * GRID_W


def _na_window(i):
    first_row = jnp.clip(i - NA_ROWS // 2, 0, SEQ_ROWS - NA_ROWS)
    return pl.multiple_of(first_row * GRID_W, GRID_W), first_row - i + NA_ROWS - 1


NA_STEP = 8


def _na_valid():
    q = _row_index((2 * GRID_W, NA_KEYS)) & (GRID_W - 1)
    k = _lane_index((2 * GRID_W, NA_KEYS)) & (GRID_W - 1)
    first_col = jnp.clip(q - 8, 0, GRID_W - 16)
    return (k >= first_col) & (k < first_col + 16)


def _head_masks():
    lane = _lane_index((1, 128))
    return (lane < HEAD_DIM, lane >= HEAD_DIM)


def _stack_heads(t, masks):
    zero = jnp.zeros_like(t)
    return jnp.concatenate([jnp.where(masks[0], t, zero), jnp.where(masks[1], t, zero)], axis=0)


def _unstack_heads(t, masks):
    n = t.shape[0] // 2
    return jnp.where(masks[0], t[:n], t[n:])


def _stack_columns(t):
    return jnp.concatenate([t[:, 0:1], t[:, HEAD_DIM:HEAD_DIM + 1]], axis=0)


def _na_scores(qs, kw, tp_ref, dr0, valid):
    s = lax.dot_general(qs, kw, NT, preferred_element_type=F32)
    bias = jnp.concatenate(
        [jnp.concatenate([tp_ref[a, pl.ds(dr0 + 2 * c, 1), :, :].reshape(GRID_W, 128) for c in range(4)], axis=1)
         for a in range(2)], axis=0)
    return jnp.where(valid, s + bias, NEG_INF)


def _na_specs():
    q_spec = pl.BlockSpec((NA_STEP * GRID_W, 128), lambda hp, i: (i, hp))
    k_spec = pl.BlockSpec((SEQ, 128), lambda hp, i: (0, 8 + hp))
    v_spec = pl.BlockSpec((SEQ, 128), lambda hp, i: (0, 16 + hp))
    tp_spec = pl.BlockSpec((2, 16, GRID_W, 128), lambda hp, i: (hp, 0, 0, 0))
    return q_spec, k_spec, v_spec, tp_spec


def _na_fwd(qkv, table):
    def body(q_ref, k_ref, v_ref, tp_ref, o_ref, lse_ref):
        valid = _na_valid()
        masks = _head_masks()
        for r in range(NA_STEP):
            rows = slice(r * GRID_W, (r + 1) * GRID_W)
            start, dr0 = _na_window(pl.program_id(1) * NA_STEP + r)
            kw = k_ref[pl.ds(start, NA_KEYS), :]
            vw = v_ref[pl.ds(start, NA_KEYS), :]
            s = _na_scores(_stack_heads(q_ref[rows, :] * Q_SCALE, masks), kw, tp_ref, dr0, valid)
            m = jnp.max(s, axis=-1, keepdims=True)
            p = jnp.exp(s - m)
            denom = jnp.sum(p, axis=-1, keepdims=True)
            out = jnp.dot(p.astype(BF16), vw, preferred_element_type=F32) / denom
            o_ref[rows, :] = _unstack_heads(out, masks).astype(o_ref.dtype)
            lse_ref[rows, :] = _unstack_heads(jnp.broadcast_to(m + jnp.log(denom), (2 * GRID_W, 128)), masks)

    q_spec, k_spec, v_spec, tp_spec = _na_specs()
    return pl.pallas_call(
        body, out_shape=(jax.ShapeDtypeStruct((SEQ, D_MODEL), BF16), jax.ShapeDtypeStruct((SEQ, D_MODEL), F32)),
        grid=(N_HEADS // 2, SEQ_ROWS // NA_STEP), in_specs=[q_spec, k_spec, v_spec, tp_spec],
        out_specs=(q_spec, q_spec), name="na_fwd", compiler_params=_params())(qkv, qkv, qkv, table)


def _na_bwd(qkv, table, d_out, lse):
    def body(q_ref, k_ref, v_ref, tp_ref, do_ref, lse_ref, dqkv_ref, gp_ref, dk_acc, dv_acc):
        step = pl.program_id(1)

        @pl.when(step == 0)
        def _():
            dk_acc[...] = jnp.zeros_like(dk_acc)
            dv_acc[...] = jnp.zeros_like(dv_acc)
            gp_ref[...] = jnp.zeros_like(gp_ref)

        valid = _na_valid()
        masks = _head_masks()
        for r in range(NA_STEP):
            rows = slice(r * GRID_W, (r + 1) * GRID_W)
            i = step * NA_STEP + r
            start, dr0 = _na_window(i)
            kw = k_ref[pl.ds(start, NA_KEYS), :]
            vw = v_ref[pl.ds(start, NA_KEYS), :]
            qs = _stack_heads(q_ref[rows, :] * Q_SCALE, masks)
            dos = _stack_heads(do_ref[rows, :], masks)
            p = jnp.exp(_na_scores(qs, kw, tp_ref, dr0, valid) - _stack_columns(lse_ref[rows, :]))
            dp = lax.dot_general(dos, vw, NT, preferred_element_type=F32)
            ds = p * (dp - jnp.sum(p * dp, axis=-1, keepdims=True))
            for a in range(2):
                for c in range(4):
                    gp_ref[a, pl.ds(dr0 + 2 * c, 1), :, :] += (
                        ds[a * GRID_W:(a + 1) * GRID_W, 128 * c:128 * (c + 1)].reshape(1, GRID_W, 128))
            dsb = ds.astype(BF16)
            dq = _unstack_heads(jnp.dot(dsb, kw, preferred_element_type=F32), masks) * Q_SCALE
            dqkv_ref[0, pl.ds(pl.multiple_of(i * GRID_W, GRID_W), GRID_W), :] = dq.astype(dqkv_ref.dtype)
            dk_acc[pl.ds(start, NA_KEYS), :] += lax.dot_general(dsb, qs, TN, preferred_element_type=F32)
            dv_acc[pl.ds(start, NA_KEYS), :] += lax.dot_general(p.astype(BF16), dos, TN, preferred_element_type=F32)

        @pl.when(step == SEQ_ROWS // NA_STEP - 1)
        def _():
            dqkv_ref[1] = dk_acc[...].astype(dqkv_ref.dtype)
            dqkv_ref[2] = dv_acc[...].astype(dqkv_ref.dtype)

    q_spec, k_spec, v_spec, tp_spec = _na_specs()
    return pl.pallas_call(
        body,
        out_shape=(jax.ShapeDtypeStruct((3, SEQ, D_MODEL), BF16), jax.ShapeDtypeStruct((N_HEADS, 16, GRID_W, 128), F32)),
        grid=(N_HEADS // 2, SEQ_ROWS // NA_STEP), in_specs=[q_spec, k_spec, v_spec, tp_spec, q_spec, q_spec],
        out_specs=(pl.BlockSpec((3, SEQ, 128), lambda hp, i: (0, 0, hp)), tp_spec),
        scratch_shapes=[pltpu.VMEM((SEQ, 128), F32), pltpu.VMEM((SEQ, 128), F32)],
        name="na_bwd", compiler_params=_params())(qkv, qkv, qkv, table, d_out, lse)


DIL_STEP = 4


def _dil_geometry(group):
    dil = DIL_GROUPS[group][1]
    sub_len = SEQ // dil
    blocks = sub_len // BAND
    return dil, sub_len, max(blocks // DIL_STEP, 1), max(DIL_STEP // blocks, 1), min(2 * BAND, sub_len)


def _dil_block(step, r, sub_len, subs):
    per_sub = DIL_STEP // subs
    return (r // per_sub) * sub_len, step * per_sub + r % per_sub


def _dil_window(b, sub_len, n_keys):
    if n_keys == sub_len:
        return 0
    return pl.multiple_of(jnp.clip(b * BAND - RADIUS, 0, sub_len - n_keys), RADIUS)


def _dil_bias(b, start, n_keys, slope_ref, hp):
    row = _row_index((2 * BAND, n_keys))
    qpos = b * BAND + (row & (BAND - 1))
    kpos = start + _lane_index((2 * BAND, n_keys))
    dist = jnp.abs(qpos - kpos)
    slope = jnp.where(row < BAND, slope_ref[2 * hp], slope_ref[2 * hp + 1])
    return slope * dist.astype(F32), dist <= RADIUS


def _dil_scores(qs, kw, penalty, valid):
    return jnp.where(valid, lax.dot_general(qs, kw, NT, preferred_element_type=F32) - penalty, NEG_INF)


def _dil_specs(group):
    dil, sub_len, steps, subs, _ = _dil_geometry(group)
    col = group * 24
    rows = DIL_STEP * BAND
    q_spec = pl.BlockSpec((rows, 128), lambda n, hp, b: (n * steps + b, col + hp))
    k_spec = pl.BlockSpec((subs * sub_len, 128), lambda n, hp, b: (n, col + 8 + hp))
    v_spec = pl.BlockSpec((subs * sub_len, 128), lambda n, hp, b: (n, col + 16 + hp))
    tile = pl.BlockSpec((rows, 128), lambda n, hp, b: (n * steps + b, hp))
    smem = pl.BlockSpec(memory_space=pltpu.SMEM)
    return (dil // subs, N_HEADS // 2, steps), q_spec, k_spec, v_spec, tile, smem


def _dil_fwd(group, qkv, slopes):
    _, sub_len, _, subs, n_keys = _dil_geometry(group)

    def body(q_ref, k_ref, v_ref, slope_ref, o_ref, lse_ref):
        hp = pl.program_id(1)
        masks = _head_masks()
        for r in range(DIL_STEP):
            rows = slice(r * BAND, (r + 1) * BAND)
            base, b = _dil_block(pl.program_id(2), r, sub_len, subs)
            start = _dil_window(b, sub_len, n_keys)
            kw = k_ref[pl.ds(base + start, n_keys), :]
            vw = v_ref[pl.ds(base + start, n_keys), :]
            penalty, valid = _dil_bias(b, start, n_keys, slope_ref, hp)
            s = _dil_scores(_stack_heads(q_ref[rows, :] * Q_SCALE, masks), kw, penalty, valid)
            m = jnp.max(s, axis=-1, keepdims=True)
            p = jnp.exp(s - m)
            denom = jnp.sum(p, axis=-1, keepdims=True)
            out = jnp.dot(p.astype(BF16), vw, preferred_element_type=F32) / denom
            o_ref[rows, :] = _unstack_heads(out, masks)
            lse_ref[rows, :] = _unstack_heads(jnp.broadcast_to(m + jnp.log(denom), (2 * BAND, 128)), masks)

    grid, q_spec, k_spec, v_spec, tile, smem = _dil_specs(group)
    return pl.pallas_call(
        body, out_shape=(jax.ShapeDtypeStruct((SEQ, D_MODEL), F32), jax.ShapeDtypeStruct((SEQ, D_MODEL), F32)),
        grid=grid, in_specs=[q_spec, k_spec, v_spec, smem], out_specs=(tile, tile),
        name=f"dil_fwd_{group}", compiler_params=_params())(qkv, qkv, qkv, slopes)


def _dil_merge(outs, lses):
    n_sorted = len(SORTED)

    def body(*refs):
        o_refs, l_refs = refs[:3], refs[3:6]
        out_refs, lse_refs, scratch = refs[6:7 + n_sorted], refs[7 + n_sorted:8 + 2 * n_sorted], refs[-1]
        os_ = [o_refs[0][...]] + [_unsort_tile(scratch, o_refs[1 + j], dil) for j, dil in enumerate(SORTED)]
        ls = [l_refs[0][...]] + [_unsort_tile(scratch, l_refs[1 + j], dil) for j, dil in enumerate(SORTED)]
        m = jnp.maximum(jnp.maximum(ls[0], ls[1]), ls[2])
        es = [jnp.exp(v - m) for v in ls]
        total = es[0] + es[1] + es[2]
        merged = (es[0] * os_[0] + es[1] * os_[1] + es[2] * os_[2]) / total
        lse = m + jnp.log(total)
        out_refs[0][...] = merged
        lse_refs[0][...] = lse
        for j, dil in enumerate(SORTED):
            _sort_tile(scratch, merged, dil, out_refs[1 + j])
            _sort_tile(scratch, lse, dil, lse_refs[1 + j])

    tm = 256
    tile = pl.BlockSpec((tm, D_MODEL), lambda i: (i, 0))
    specs = [tile] + [_sorted_spec(dil, tm=tm) for dil in SORTED]
    shapes = [jax.ShapeDtypeStruct((SEQ, D_MODEL), F32)] + [jax.ShapeDtypeStruct((dil, SEQ // dil, D_MODEL), F32) for dil in SORTED]
    views = lambda ts: [ts[0]] + [_sorted_view(t, dil) for t, dil in zip(ts[1:], SORTED)]
    result = pl.pallas_call(
        body, out_shape=tuple(shapes * 2), grid=(SEQ // tm,), in_specs=specs * 2, out_specs=tuple(specs * 2),
        scratch_shapes=[_sort_scratch(tm)], name="dil_merge", compiler_params=_params())(*views(outs), *views(lses))
    flat = [t.reshape(SEQ, D_MODEL) for t in result]
    return flat[:1 + n_sorted], flat[1 + n_sorted:]


def _dil_bwd(group, qkv, slopes, d_out, out, lse_group, lse_total, into):
    _, sub_len, steps, subs, n_keys = _dil_geometry(group)

    def body(q_ref, k_ref, v_ref, slope_ref, do_ref, o_ref, lg_ref, lt_ref, into_ref, dqkv_ref, dk_acc, dv_acc):
        hp, step = pl.program_id(1), pl.program_id(2)

        @pl.when(step == 0)
        def _():
            dk_acc[...] = jnp.zeros_like(dk_acc)
            dv_acc[...] = jnp.zeros_like(dv_acc)

        masks = _head_masks()
        for r in range(DIL_STEP):
            rows = slice(r * BAND, (r + 1) * BAND)
            base, b = _dil_block(step, r, sub_len, subs)
            start = _dil_window(b, sub_len, n_keys)
            keys = pl.ds(base + start, n_keys)
            kw = k_ref[keys, :]
            vw = v_ref[keys, :]
            penalty, valid = _dil_bias(b, start, n_keys, slope_ref, hp)
            qs = _stack_heads(q_ref[rows, :] * Q_SCALE, masks)
            lse2 = lg_ref[rows, :]
            weight = jnp.exp(lse2 - lt_ref[rows, :])
            do2 = do_ref[rows, :]
            dogs = _stack_heads((weight * do2).astype(BF16), masks)
            delta = _stack_columns(weight) * jnp.sum(_stack_heads(do2 * o_ref[rows, :], masks), axis=-1, keepdims=True)
            p = jnp.exp(_dil_scores(qs, kw, penalty, valid) - _stack_columns(lse2))
            dp = lax.dot_general(dogs, vw, NT, preferred_element_type=F32)
            dsb = (p * (dp - delta)).astype(BF16)
            dq = _unstack_heads(jnp.dot(dsb, kw, preferred_element_type=F32), masks) * Q_SCALE
            dqkv_ref[0, pl.ds(pl.multiple_of(base + b * BAND, BAND), BAND), :] = dq.astype(dqkv_ref.dtype)
            dk_acc[keys, :] += lax.dot_general(dsb, qs, TN, preferred_element_type=F32)
            dv_acc[keys, :] += lax.dot_general(p.astype(BF16), dogs, TN, preferred_element_type=F32)

        @pl.when(step == steps - 1)
        def _():
            dqkv_ref[1] = dk_acc[...].astype(dqkv_ref.dtype)
            dqkv_ref[2] = dv_acc[...].astype(dqkv_ref.dtype)

    grid, q_spec, k_spec, v_spec, tile, smem = _dil_specs(group)
    return pl.pallas_call(
        body, out_shape=jax.ShapeDtypeStruct(into.shape, into.dtype), grid=grid,
        in_specs=[q_spec, k_spec, v_spec, smem, tile, tile, tile, tile, ANY],
        out_specs=pl.BlockSpec((3, subs * sub_len, 128), lambda n, hp, b: (group, n, hp)),
        scratch_shapes=[pltpu.VMEM((subs * sub_len, 128), F32), pltpu.VMEM((subs * sub_len, 128), F32)],
        input_output_aliases={8: 0}, name=f"dil_bwd_{group}", compiler_params=_params(),
    )(qkv, qkv, qkv, slopes, d_out, out, lse_group, lse_total, into)


def _ffn_up(name, h, w_gu):
    def body(h_ref, wg_ref, wu_ref, act_ref, hg_ref, hu_ref, act_t_ref):
        hv = h_ref[...]
        hg = jnp.dot(hv, wg_ref[...], preferred_element_type=F32)
        hu = jnp.dot(hv, wu_ref[...], preferred_element_type=F32)
        act = hg * jax.nn.sigmoid(hg) * hu
        act_ref[...] = act.astype(act_ref.dtype)
        act_t_ref[...] = act.T.astype(act_t_ref.dtype)
        hg_ref[...] = hg.astype(hg_ref.dtype)
        hu_ref[...] = hu.astype(hu_ref.dtype)

    out = pl.BlockSpec((None, TM_MM, FF_PAD), lambda d, i: (d, i, 0))
    shape = jax.ShapeDtypeStruct((N_DEV, SEQ, FF_PAD), BF16)
    return pl.pallas_call(
        body, out_shape=(shape, shape, shape, jax.ShapeDtypeStruct((N_DEV, FF_PAD, SEQ), BF16)), grid=(N_DEV, SEQ // TM_MM),
        in_specs=[pl.BlockSpec((TM_MM, D_MODEL), lambda d, i: (i, 0)),
                  pl.BlockSpec((None, None, D_MODEL, FF_PAD), lambda d, i: (d, 0, 0, 0)),
                  pl.BlockSpec((None, None, D_MODEL, FF_PAD), lambda d, i: (d, 1, 0, 0))],
        out_specs=(out, out, out, pl.BlockSpec((None, FF_PAD, TM_MM), lambda d, i: (d, 0, i))),
        name=name, compiler_params=_params())(h, w_gu, w_gu)


def _ffn_bwd_act(name, d_f, w_down, hg, hu):
    def body(df_ref, wd_ref, hg_ref, hu_ref, dgu_ref):
        dact = lax.dot_general(df_ref[...], wd_ref[...], NT, preferred_element_type=F32)
        hgv = hg_ref[...].astype(F32)
        sig = jax.nn.sigmoid(hgv)
        dgu_ref[0] = (dact * hu_ref[...].astype(F32) * (sig * (1.0 + hgv * (1.0 - sig)))).astype(dgu_ref.dtype)
        dgu_ref[1] = (dact * hgv * sig).astype(dgu_ref.dtype)

    tile = pl.BlockSpec((None, TM_MM, FF_PAD), lambda d, i: (d, i, 0))
    return pl.pallas_call(
        body, out_shape=jax.ShapeDtypeStruct((2, N_DEV, SEQ, FF_PAD), BF16), grid=(N_DEV, SEQ // TM_MM),
        in_specs=[pl.BlockSpec((TM_MM, D_MODEL), lambda d, i: (i, 0)),
                  pl.BlockSpec((None, FF_PAD, D_MODEL), lambda d, i: (d, 0, 0)), tile, tile],
        out_specs=pl.BlockSpec((2, None, TM_MM, FF_PAD), lambda d, i: (0, d, i, 0)),
        name=name, compiler_params=_params())(d_f, w_down, hg, hu)


def _position():
    return lax.axis_index("x"), lax.axis_index("y"), lax.axis_index("c")


def _flat(p):
    return 4 * p[0] + 2 * p[1] + p[2]


def _peer(me, k):
    x, y, c = me
    return (1 - x if k & 4 else x, 1 - y if k & 2 else y, 1 - c if k & 1 else c)


def _columns(width):
    return lambda ref, d: ref.at[:, pl.ds(pl.multiple_of(d * width, 128), width)]


def _leading(ref, d):
    return ref.at[d]


def _whole(ref, d):
    return ref


def _by_sender(window):
    return lambda ref, sender, k: window(ref, sender)


def _by_distance(ref, sender, k):
    return ref.at[k - 1]


def _prep_weights(me, na_qkv, na_o, dil_qkv, dil_o, gate, up, down, land_shapes):
    na_cols, dil_cols = na_qkv.shape[-1], dil_qkv.shape[-1]
    o_rows = na_o.shape[1]
    tiles = 4
    rows, rows_o = D_MODEL // tiles, o_rows // tiles

    def body(me_ref, naq, nao, dq, do_, g0, u0, d0, g1, u1, d1, *outs):
        def put(t, index, value):
            outs[t][index] = value
            outs[8 + t][index] = value

        put(0, ..., naq[...].astype(BF16))
        put(1, ..., nao[...].astype(BF16))
        put(4, ..., dq[...].astype(BF16))
        put(5, ..., do_[...].astype(BF16))
        for t, (g, u, d) in ((2, (g0, u0, d0)), (6, (g1, u1, d1))):
            for j, part in enumerate((g, u)):
                put(t, (j, slice(None), slice(0, FF_SHARD)), part[...].astype(BF16))
                put(t, (j, slice(None), slice(FF_SHARD, FF_PAD)), jnp.zeros((rows, FF_PAD - FF_SHARD), BF16))
            put(t + 1, (slice(0, FF_SHARD), slice(None)), d[...].astype(BF16))
            put(t + 1, (slice(FF_SHARD, FF_PAD), slice(None)), jnp.zeros((FF_PAD - FF_SHARD, D_MODEL), BF16))

    def tiled(width):
        return pl.BlockSpec((None, rows, width), lambda i, me: (0, i, 0))

    def layer(l, width):
        return pl.BlockSpec((None, rows, width), lambda i, me: (l, i, 0))

    def whole_layer(l):
        return pl.BlockSpec((None, FF_SHARD, D_MODEL), lambda i, me: (l, 0, 0))

    in_specs = [tiled(na_cols), pl.BlockSpec((None, rows_o, D_MODEL), lambda i, me: (0, i, 0)), tiled(dil_cols),
                pl.BlockSpec((None, rows_o, D_MODEL), lambda i, me: (0, i, 0)),
                layer(0, FF_SHARD), layer(0, FF_SHARD), whole_layer(0), layer(1, FF_SHARD), layer(1, FF_SHARD), whole_layer(1)]
    o_shard = pl.BlockSpec((rows_o, D_MODEL), lambda i, me: (i, 0))
    o_land = pl.BlockSpec((None, rows_o, D_MODEL), lambda i, me: (me[0], i, 0))
    gu_shard = pl.BlockSpec((2, rows, FF_PAD), lambda i, me: (0, i, 0))
    gu_land = pl.BlockSpec((None, 2, rows, FF_PAD), lambda i, me: (me[0], 0, i, 0))
    down_shard = pl.BlockSpec((FF_PAD, D_MODEL), lambda i, me: (0, 0))
    down_land = pl.BlockSpec((None, FF_PAD, D_MODEL), lambda i, me: (me[0], 0, 0))

    def qkv_shard(width):
        return pl.BlockSpec((rows, width), lambda i, me: (i, 0))

    def qkv_land(width):
        return pl.BlockSpec((rows, width), lambda i, me: (i, me[0]))

    shard_specs = [qkv_shard(na_cols), o_shard, gu_shard, down_shard, qkv_shard(dil_cols), o_shard, gu_shard, down_shard]
    land_specs = [qkv_land(na_cols), o_land, gu_land, down_land, qkv_land(dil_cols), o_land, gu_land, down_land]
    shard_shapes = [jax.ShapeDtypeStruct(s, BF16) for s in
                    ((D_MODEL, na_cols), (o_rows, D_MODEL), (2, D_MODEL, FF_PAD), (FF_PAD, D_MODEL),
                     (D_MODEL, dil_cols), (o_rows, D_MODEL), (2, D_MODEL, FF_PAD), (FF_PAD, D_MODEL))]
    result = pl.pallas_call(
        body, out_shape=tuple(shard_shapes + list(land_shapes)),
        grid_spec=pltpu.PrefetchScalarGridSpec(num_scalar_prefetch=1, grid=(tiles,), in_specs=in_specs,
                                               out_specs=tuple(shard_specs + land_specs)),
        name="prep_weights", compiler_params=_params())(me, na_qkv, na_o, dil_qkv, dil_o, gate, up, down, gate, up, down)
    return list(result[:8]), list(result[8:])


def _remote_copies(sets, src_refs, land_refs, send_sems, recv_sems, outgoing):
    me = _position()
    copies = []
    for t, (si, src_of, li, dst_of) in enumerate(sets):
        for k in range(1, N_DEV):
            other = _peer(me, k)
            sender = me if outgoing else other
            copies.append(pltpu.make_async_remote_copy(
                src_ref=src_of(src_refs[si], _flat(other)), dst_ref=dst_of(land_refs[li], _flat(sender), k),
                send_sem=send_sems.at[(N_DEV - 1) * t + k - 1], recv_sem=recv_sems.at[(N_DEV - 1) * t + k - 1],
                device_id=other, device_id_type=MESH))
    return copies


def _send_start(name, srcs, lands, sets_by_group):
    n_src, n_land, n_groups = len(srcs), len(lands), len(sets_by_group)

    def body(*refs):
        src_refs, land_refs = refs[:n_src], refs[n_src:n_src + n_land]
        outs = refs[n_src + n_land:]
        for g, sets in enumerate(sets_by_group):
            for cp in _remote_copies(sets, src_refs, land_refs, outs[2 * g], outs[2 * g + 1], True):
                cp.start()
        outs[-1][...] = jnp.zeros_like(outs[-1])

    sem_shapes = []
    for sets in sets_by_group:
        sem_shapes += [pltpu.SemaphoreType.DMA((len(sets) * (N_DEV - 1),))] * 2
    thru = [pltpu.HBM(a.shape, a.dtype) for a in list(srcs) + list(lands)]
    n_sem = len(sem_shapes)
    result = pl.pallas_call(
        body, out_shape=tuple(sem_shapes + thru + [jax.ShapeDtypeStruct((8, 128), F32)]),
        in_specs=[HBM] * (n_src + n_land),
        out_specs=tuple([SEM] * n_sem + [HBM] * (n_src + n_land) + [pl.BlockSpec(memory_space=pltpu.VMEM)]),
        input_output_aliases={i: n_sem + i for i in range(n_src + n_land)},
        compiler_params=pltpu.CompilerParams(has_side_effects=EFFECT), name=name,
    )(*[pltpu.with_memory_space_constraint(a, pltpu.HBM) for a in list(srcs) + list(lands)])
    sems = [(result[2 * g], result[2 * g + 1]) for g in range(n_groups)]
    return sems, list(result[n_sem:n_sem + n_src]), list(result[n_sem + n_src:n_sem + n_src + n_land]), result[-1]


def _send_wait(name, sems, srcs, lands, sets, after):
    n_src, n_land = len(srcs), len(lands)

    def body(*refs):
        src_refs, land_refs = refs[:n_src], refs[n_src:n_src + n_land]
        send_sems, recv_sems = refs[n_src + n_land], refs[n_src + n_land + 1]
        for cp in _remote_copies(sets, src_refs, land_refs, send_sems, recv_sems, True):
            cp.wait_send()
        for cp in _remote_copies(sets, src_refs, land_refs, send_sems, recv_sems, False):
            cp.wait_recv()

    thru = [pltpu.HBM(a.shape, a.dtype) for a in list(srcs) + list(lands)]
    result = pl.pallas_call(
        body, out_shape=tuple(thru), in_specs=[HBM] * (n_src + n_land) + [SEM, SEM] + [ANY] * len(after),
        out_specs=tuple([HBM] * (n_src + n_land)), input_output_aliases={i: i for i in range(n_src + n_land)},
        compiler_params=pltpu.CompilerParams(has_side_effects=EFFECT), name=name,
    )(*srcs, *lands, sems[0], sems[1], *after)
    return list(result[:n_src]), list(result[n_src:])


DIRECT = (1, 2, 4, 6)
PASSED = DIRECT[1:]


def _hbm_passthrough(body, name, arrays, n_sem_in, sem_out_shapes, extra):
    n, n_out = len(arrays), len(sem_out_shapes)
    return pl.pallas_call(
        body, out_shape=tuple(list(sem_out_shapes) + [pltpu.HBM(a.shape, a.dtype) for a in arrays]),
        in_specs=[HBM] * n + [SEM] * n_sem_in + [ANY] * len(extra), out_specs=tuple([SEM] * n_out + [HBM] * n),
        input_output_aliases={i: n_out + i for i in range(n)},
        compiler_params=pltpu.CompilerParams(has_side_effects=EFFECT), name=name)


def _shard_copy(src_ref, land_ref, window, block, to, send_sem, recv_sem, from_shard):
    dst = window(land_ref, _flat(block))
    return pltpu.make_async_remote_copy(src_ref=src_ref if from_shard else dst, dst_ref=dst, send_sem=send_sem,
                                        recv_sem=recv_sem, device_id=to, device_id_type=MESH)


def _gather_start(name, shards, lands, windows, group_sizes):
    n = len(shards)

    def body(*refs):
        shard_refs, land_refs, outs = refs[:n], refs[n:2 * n], refs[2 * n:]
        me = _position()
        t = 0
        for g, size in enumerate(group_sizes):
            for local in range(size):
                for j, k in enumerate(DIRECT):
                    i = len(DIRECT) * local + j
                    _shard_copy(shard_refs[t], land_refs[t], windows[t], me, _peer(me, k), outs[2 * g].at[i],
                                outs[2 * g + 1].at[i], True).start()
                t += 1

    sem_shapes = [pltpu.SemaphoreType.DMA((len(DIRECT) * size,)) for size in group_sizes for _ in range(2)]
    arrays = [pltpu.with_memory_space_constraint(a, pltpu.HBM) for a in list(shards) + list(lands)]
    result = _hbm_passthrough(body, name, arrays, 0, sem_shapes, ())(*arrays)
    n_sem = len(sem_shapes)
    sems = [(result[2 * g], result[2 * g + 1]) for g in range(len(group_sizes))]
    return sems, list(result[n_sem:n_sem + n]), list(result[n_sem + n:])


def _gather_pass_on(name, sems, shards, lands, windows, after):
    n = len(shards)

    def body(*refs):
        shard_refs, land_refs = refs[:n], refs[n:2 * n]
        recv_sems = refs[2 * n + 1]
        pass_send, pass_recv = refs[2 * n + 2 + len(after)], refs[2 * n + 3 + len(after)]
        me = _position()
        sibling = _peer(me, 1)
        for t in range(n):
            for j, k in enumerate(PASSED):
                sender = _peer(me, k)
                arrived = len(DIRECT) * t + 1 + j
                _shard_copy(shard_refs[t], land_refs[t], windows[t], sender, me, refs[2 * n].at[arrived], recv_sems.at[arrived],
                            True).wait_recv()
                i = len(PASSED) * t + j
                _shard_copy(shard_refs[t], land_refs[t], windows[t], sender, sibling, pass_send.at[i], pass_recv.at[i],
                            False).start()

    sem_shapes = [pltpu.SemaphoreType.DMA((len(PASSED) * n,))] * 2
    result = _hbm_passthrough(body, name, list(shards) + list(lands), 2, sem_shapes, after)(
        *shards, *lands, sems[0], sems[1], *after)
    return (result[0], result[1]), list(result[2:2 + n]), list(result[2 + n:])


def _gather_wait(name, sems, pass_sems, shards, lands, windows, after):
    n = len(shards)

    def body(*refs):
        shard_refs, land_refs = refs[:n], refs[n:2 * n]
        send_sems, recv_sems, pass_send, pass_recv = refs[2 * n:2 * n + 4]
        me = _position()
        sibling = _peer(me, 1)
        for t in range(n):
            for j, k in enumerate(DIRECT):
                i = len(DIRECT) * t + j
                _shard_copy(shard_refs[t], land_refs[t], windows[t], me, _peer(me, k), send_sems.at[i], recv_sems.at[i],
                            True).wait_send()
            _shard_copy(shard_refs[t], land_refs[t], windows[t], sibling, me, send_sems.at[len(DIRECT) * t],
                        recv_sems.at[len(DIRECT) * t], True).wait_recv()
            for j, k in enumerate(PASSED):
                i = len(PASSED) * t + j
                _shard_copy(shard_refs[t], land_refs[t], windows[t], _peer(me, k), sibling, pass_send.at[i], pass_recv.at[i],
                            False).wait_send()
                _shard_copy(shard_refs[t], land_refs[t], windows[t], _peer(sibling, k), me, pass_send.at[i], pass_recv.at[i],
                            False).wait_recv()

    result = _hbm_passthrough(body, name, list(shards) + list(lands), 4, [], after)(
        *shards, *lands, sems[0], sems[1], pass_sems[0], pass_sems[1], *after)
    return list(result[n:])


def _all_gather(name, locals_, out_shapes, windows, deps=()):
    n = len(locals_)

    def body(*refs):
        src_refs, out_refs = refs[:n], refs[n + len(deps):2 * n + len(deps)]
        send_sems, recv_sems, local_sems = refs[2 * n + len(deps):]
        x, y, c = _position()
        me, sibling = (x, y, c), (x, y, 1 - c)
        chips = [(1 - x, y), (x, 1 - y), (1 - x, 1 - y)]

        def copy(t, k, block, to, from_local=False):
            dst = windows[t](out_refs[t], _flat(block))
            return pltpu.make_async_remote_copy(
                src_ref=src_refs[t] if from_local else dst, dst_ref=dst, send_sem=send_sems.at[t, k],
                recv_sem=recv_sems.at[t, k], device_id=to, device_id_type=MESH)

        mine = [pltpu.make_async_copy(src_refs[t], windows[t](out_refs[t], _flat(me)), local_sems.at[t]) for t in range(n)]
        sends = []
        for t in range(n):
            mine[t].start()
            sends.append(copy(t, 0, me, sibling, True))
            sends += [copy(t, 1 + j, me, (*chip, c), True) for j, chip in enumerate(chips)]
        for cp in sends:
            cp.start()
        for t in range(n):
            for j, chip in enumerate(chips):
                copy(t, 1 + j, (*chip, c), me).wait_recv()
                passed = copy(t, 4 + j, (*chip, c), sibling)
                passed.start()
                sends.append(passed)
        for t in range(n):
            copy(t, 0, sibling, me).wait_recv()
            for j, chip in enumerate(chips):
                copy(t, 4 + j, (*chip, 1 - c), me).wait_recv()
        for cp in sends:
            cp.wait_send()
        for cp in mine:
            cp.wait()

    return pl.pallas_call(
        body, out_shape=tuple(out_shapes), in_specs=[ANY] * (n + len(deps)), out_specs=tuple([ANY] * n),
        scratch_shapes=[pltpu.SemaphoreType.DMA((n, 7)), pltpu.SemaphoreType.DMA((n, 7)), pltpu.SemaphoreType.DMA((n,))],
        name=name)(*locals_, *deps)


def _adamw(name, me, lands, owns, w, m, v, *, grid, land_specs, own_specs, p_spec):
    n_land = len(lands)

    def body(me_ref, *refs):
        land_refs, own_refs = refs[:n_land], refs[n_land:n_land + len(owns)]
        w_ref, m_ref, v_ref, g_ref, delta_ref, m_out, v_out = refs[n_land + len(owns):]
        ncols = w_ref.shape[-1]
        sums = []
        for i, land_ref in enumerate(land_refs):
            g = own_refs[i][...].astype(F32) if owns else land_ref[0].astype(F32)
            for s in range(0 if owns else 1, land_ref.shape[0]):
                g = g + land_ref[s].astype(F32)
            sums.append(g[:, :ncols])
        g = sums[0] if n_land == 1 else jnp.where(pl.program_id(0) == 0, sums[0], sums[1])
        m_new = ADAM_B1 * m_ref[...] + (1.0 - ADAM_B1) * g
        v_new = ADAM_B2 * v_ref[...] + (1.0 - ADAM_B2) * jnp.square(g)
        m_hat = m_new / (1.0 - ADAM_B1 ** ADAM_STEP)
        v_hat = v_new / (1.0 - ADAM_B2 ** ADAM_STEP)
        g_ref[...] = g
        delta_ref[...] = -ADAM_LR * (m_hat / (jnp.sqrt(v_hat) + ADAM_EPS) + ADAM_WD * w_ref[...])
        m_out[...] = m_new
        v_out[...] = v_new

    shape = jax.ShapeDtypeStruct(w.shape, F32)
    return pl.pallas_call(
        body, out_shape=(shape,) * 4,
        grid_spec=pltpu.PrefetchScalarGridSpec(
            num_scalar_prefetch=1, grid=grid, in_specs=list(land_specs) + list(own_specs) + [p_spec, p_spec, p_spec],
            out_specs=(p_spec,) * 4),
        name=name, compiler_params=_params())(me, *lands, *owns, w, m, v)


def _row(p, layer):
    return p[layer][None, :]


def _square(name, a, b, dims, out_dtype, deps=()):
    if a.shape == (D_MODEL, SEQ):
        return _matmul(name, a, b, grid=(2, 1), a_spec=pl.BlockSpec((512, SEQ), lambda i, k: (i, 0)),
                       b_spec=pl.BlockSpec((SEQ, D_MODEL), lambda i, k: (0, 0)),
                       o_spec=pl.BlockSpec((512, D_MODEL), lambda i, k: (i, 0)),
                       out_shape=jax.ShapeDtypeStruct((D_MODEL, D_MODEL), out_dtype), dims=NN, acc_shape=(8, 128),
                       deps=deps)
    return _matmul(name, a, b, grid=(SEQ // TM_MM, 1), a_spec=pl.BlockSpec((TM_MM, D_MODEL), lambda i, k: (i, 0)),
                   b_spec=pl.BlockSpec((D_MODEL, D_MODEL), lambda i, k: (0, 0)),
                   o_spec=pl.BlockSpec((TM_MM, D_MODEL), lambda i, k: (i, 0)),
                   out_shape=jax.ShapeDtypeStruct((SEQ, D_MODEL), out_dtype), dims=dims, acc_shape=(8, 128), deps=deps)


def _grouped_matmul(name, a_list, b, *, n_tiles, a_block, b_spec, o_spec, out_shape):
    n_groups = len(a_list)

    def a_spec(g):
        def index(j, i):
            mine = j // 3
            return (jnp.where(mine == g, i, jnp.where(mine < g, 0, n_tiles - 1)), 0)
        return pl.BlockSpec(a_block, index)

    def body(*refs):
        b_ref, o_ref = refs[n_groups], refs[n_groups + 1]
        mine = pl.program_id(0) // 3
        for g in range(n_groups):
            @pl.when(mine == g)
            def _(g=g):
                o_ref[...] = jnp.dot(refs[g][...], b_ref[...], preferred_element_type=F32).astype(o_ref.dtype)

    return pl.pallas_call(
        body, out_shape=out_shape, grid=(3 * n_groups, n_tiles), in_specs=[a_spec(g) for g in range(n_groups)] + [b_spec],
        out_specs=o_spec, name=name, compiler_params=_params())(*a_list, b)


def _qkv_fwd(name, hs, w):
    return _grouped_matmul(name, hs, w, n_tiles=SEQ // TM_MM, a_block=(TM_MM, D_MODEL),
                           b_spec=pl.BlockSpec((D_MODEL, D_MODEL), lambda j, i: (0, j)),
                           o_spec=pl.BlockSpec((TM_MM, D_MODEL), lambda j, i: (i, j)),
                           out_shape=jax.ShapeDtypeStruct((SEQ, 3 * len(hs) * D_MODEL), BF16))


def _qkv_dw(name, hs_t, dqkv):
    return _grouped_matmul(name, hs_t, dqkv, n_tiles=2, a_block=(512, SEQ),
                           b_spec=pl.BlockSpec((None, SEQ, D_MODEL), lambda j, i: (j, 0, 0)),
                           o_spec=pl.BlockSpec((512, D_MODEL), lambda j, i: (i, j)),
                           out_shape=jax.ShapeDtypeStruct((D_MODEL, 3 * len(hs_t) * D_MODEL), BF16))


def _proj_do_sorted(name, d_a, w_o):
    def body(da_ref, w_ref, *refs):
        value = lax.dot_general(da_ref[...], w_ref[...], NT, preferred_element_type=F32)
        refs[0][...] = value
        for j, dil in enumerate(SORTED):
            _sort_tile(refs[-1], value, dil, refs[1 + j])

    tile = pl.BlockSpec((TM, D_MODEL), lambda i: (i, 0))
    shapes = [jax.ShapeDtypeStruct((SEQ, D_MODEL), F32)] + [jax.ShapeDtypeStruct((dil, SEQ // dil, D_MODEL), F32) for dil in SORTED]
    result = pl.pallas_call(
        body, out_shape=tuple(shapes), grid=(SEQ // TM,),
        in_specs=[tile, pl.BlockSpec((D_MODEL, D_MODEL), lambda i: (0, 0))],
        out_specs=tuple([tile] + [_sorted_spec(dil) for dil in SORTED]), scratch_shapes=[_sort_scratch()],
        name=name, compiler_params=_params())(d_a, w_o)
    return [t.reshape(SEQ, D_MODEL) for t in result]


def _qkv_dh(name, dqkv, w, n_chunks, deps):
    return _matmul(name, dqkv, w, grid=(n_chunks // 3, SEQ // TM, 1),
                   a_spec=pl.BlockSpec((3, TM, D_MODEL), lambda g, i, k: (g, i, 0)),
                   b_spec=pl.BlockSpec((D_MODEL, 3 * D_MODEL), lambda g, i, k: (0, g)),
                   o_spec=pl.BlockSpec((None, TM, D_MODEL), lambda g, i, k: (g, i, 0)),
                   out_shape=jax.ShapeDtypeStruct((n_chunks // 3, SEQ, D_MODEL), F32), dims=NT, acc_shape=(8, 128),
                   deps=deps, inner=3)


def _local_step(x, target, norms, rpb, fetch, emit, deps):
    mix_pre, mix_post, ffn_pre, ffn_post = norms
    slopes = 2.0 ** (-8.0 * jnp.arange(1, N_HEADS + 1, dtype=F32) / N_HEADS)
    rpb_pad = jnp.pad(rpb, ((0, 0), (0, 1), (0, 128 - 31)))
    saved = []

    for layer in range(2):
        tag = f"l{layer}"
        if layer == 0:
            h = _rms_fwd(tag + "_norm_mix", x, _row(mix_pre, layer), out_dtype=BF16, deps=deps)
            hs = [h]
            table = _rpb_table(rpb_pad)
            w_qkv, w_o = fetch("na", [table, h], [h])
            qkv = _qkv_fwd(tag + "_qkv", hs, w_qkv)
            o, lse = _na_fwd(qkv, table)
            mixer = (hs, qkv, o, lse, table)
        else:
            hs = [t.reshape(SEQ, D_MODEL) for t in
                  _rms_fwd(tag + "_norm_mix", x, _row(mix_pre, layer), out_dtype=BF16, sorted_too=True)]
            w_qkv, w_o = fetch("dil", [x], [hs[0]])
            qkv = _qkv_fwd(tag + "_qkv", hs, w_qkv)
            outs, lses = zip(*[_dil_fwd(g, qkv, slopes * dil) for g, (_, dil) in enumerate(DIL_GROUPS)])
            merged, lse_total = _dil_merge(outs, lses)
            o = merged[0]
            mixer = (hs, qkv, merged, lses, lse_total)
        a = _square(tag + "_proj", o, w_o, NN, F32)
        x1 = _rms_fwd(tag + "_post_mix", a, _row(mix_post, layer), res=x)
        h2 = _rms_fwd(tag + "_norm_ffn", x1, _row(ffn_pre, layer), out_dtype=BF16)
        w_gu, w_down = fetch(f"ffn{layer}", [a], [h2])
        act, hg, hu, act_t = _ffn_up(tag + "_ffn_up", h2, w_gu)
        f = _matmul(tag + "_ffn_down", act, w_down, grid=(SEQ // TM_MM, N_DEV // 4),
                    a_spec=pl.BlockSpec((4, TM_MM, FF_PAD), lambda i, k: (k, i, 0)),
                    b_spec=pl.BlockSpec((4, FF_PAD, D_MODEL), lambda i, k: (k, 0, 0)),
                    o_spec=pl.BlockSpec((TM_MM, D_MODEL), lambda i, k: (i, 0)),
                    out_shape=jax.ShapeDtypeStruct((SEQ, D_MODEL), F32), dims=NN, acc_shape=(TM_MM, D_MODEL), inner=4)
        x2 = _rms_fwd(tag + "_post_ffn", f, _row(ffn_post, layer), res=x1)
        transposed = ([t.T for t in hs], o.astype(BF16).T, h2.T, act_t)
        saved.append((x, mixer, a, x1, transposed, hg, hu, f, w_qkv, w_o, w_gu, w_down))
        x = x2

    dx, loss = _loss_head("loss_head", x, target)
    d_norm = {k: [None, None] for k in ("mix_pre", "mix_post", "ffn_pre", "ffn_post")}
    d_rpb = None

    for layer in (1, 0):
        tag = f"b{layer}"
        x0, mixer, a, x1, (h_t, o_t, h2_t, act_t), hg, hu, f, w_qkv, w_o, w_gu, w_down = saved[layer]
        d_f, d_norm["ffn_post"][layer] = _rms_bwd(tag + "_post_ffn", f, _row(ffn_post, layer), [dx], out_dtype=BF16)
        dgu = _ffn_bwd_act(tag + "_ffn_act", d_f, w_down, hg, hu)
        d_down = _matmul(
            tag + "_ffn_ddown", act_t, d_f, grid=(N_DEV, 1),
            a_spec=pl.BlockSpec((None, FF_PAD, SEQ), lambda d, k: (d, 0, 0)),
            b_spec=pl.BlockSpec((SEQ, D_MODEL), lambda d, k: (0, 0)),
            o_spec=pl.BlockSpec((None, FF_PAD, D_MODEL), lambda d, k: (d, 0, 0)),
            out_shape=jax.ShapeDtypeStruct((N_DEV, FF_PAD, D_MODEL), BF16), dims=NN, acc_shape=(8, 128))
        d_gu = _matmul(
            tag + "_ffn_dgu", h2_t, dgu, grid=(2, N_DEV, 1),
            a_spec=pl.BlockSpec((D_MODEL, SEQ), lambda t, d, k: (0, 0)),
            b_spec=pl.BlockSpec((None, None, SEQ, FF_PAD), lambda t, d, k: (t, d, 0, 0)),
            o_spec=pl.BlockSpec((None, None, D_MODEL, FF_PAD), lambda t, d, k: (d, t, 0, 0)),
            out_shape=jax.ShapeDtypeStruct((N_DEV, 2, D_MODEL, FF_PAD), BF16), dims=NN, acc_shape=(8, 128))
        sent = emit(f"ffn{layer}", [d_gu, d_down])
        d_h2 = _matmul(
            tag + "_ffn_dh", dgu, w_gu, grid=(SEQ // TM_MM, 4),
            a_spec=pl.BlockSpec((None, 4, TM_MM, FF_PAD), lambda i, k: (k // 2, k % 2, i, 0)),
            b_spec=pl.BlockSpec((4, None, D_MODEL, FF_PAD), lambda i, k: (k % 2, k // 2, 0, 0)),
            o_spec=pl.BlockSpec((TM_MM, D_MODEL), lambda i, k: (i, 0)),
            out_shape=jax.ShapeDtypeStruct((SEQ, D_MODEL), F32), dims=NT, acc_shape=(TM_MM, D_MODEL), deps=sent, inner=4)
        dx1, d_norm["ffn_pre"][layer] = _rms_bwd(tag + "_norm_ffn", x1, _row(ffn_pre, layer), [d_h2], res=dx)
        d_a, d_norm["mix_post"][layer] = _rms_bwd(tag + "_post_mix", a, _row(mix_post, layer), [dx1], out_dtype=BF16)
        d_wo = _square(tag + "_proj_dw", o_t, d_a, NN, BF16)
        if layer == 0:
            _, qkv, o, lse, table = mixer
            d_o = _square(tag + "_proj_do", d_a, w_o, NT, BF16)
            dqkv, gp = _na_bwd(qkv, table, d_o, lse)
            d_rpb = _rpb_grad(gp)[:, :15, :31]
            sent = emit("na", [_qkv_dw(tag + "_qkv_dw", h_t, dqkv), d_wo])
            d_h = _qkv_dh(tag + "_qkv_dh", dqkv, w_qkv, 3, sent)
            dx, d_norm["mix_pre"][layer] = _rms_bwd(tag + "_norm_mix", x0, _row(mix_pre, layer), [d_h[0]], res=dx1)
        else:
            _, qkv, merged, lses, lse_total = mixer
            d_o = _proj_do_sorted(tag + "_proj_do", d_a, w_o)
            dqkv = lax.empty((3 * len(DIL_GROUPS), SEQ, D_MODEL), BF16)
            for g, (_, dil) in enumerate(DIL_GROUPS):
                dqkv = _dil_bwd(g, qkv, slopes * dil, d_o[g], merged[g], lses[g], lse_total[g], dqkv)
            sent = emit("dil", [_qkv_dw(tag + "_qkv_dw", h_t, dqkv), d_wo])
            d_h = _qkv_dh(tag + "_qkv_dh", dqkv, w_qkv, 9, sent)
            dx, d_norm["mix_pre"][layer] = _rms_bwd(tag + "_norm_mix", x0, _row(mix_pre, layer), None, res=dx1, groups=d_h)

    d_gains = [jnp.concatenate(d_norm[k], axis=0) for k in ("mix_pre", "mix_post", "ffn_pre", "ffn_post")]
    return loss, dx, d_gains, d_rpb


RPB_SIZE = N_HEADS * 15 * 31


def _pack_small(gains, rpb, last=None):
    top = jnp.concatenate(gains, axis=0).reshape(64, 128)
    bottom = jnp.pad(rpb.reshape(-1), (0, 64 * 128 - RPB_SIZE))
    if last is not None:
        bottom = bottom + jnp.pad(last.reshape(1), (64 * 128 - 1, 0))
    return jnp.concatenate([top, bottom.reshape(64, 128)], axis=0)


def _unpack_small(p):
    gains = p[:64].reshape(4, 2, D_MODEL)
    rpb = p[64:].reshape(-1)[:RPB_SIZE].reshape(1, N_HEADS, 15, 31)
    return [gains[i] for i in range(4)], rpb


GROUPS = ("na", "ffn0", "dil", "ffn1")


def kernel(x, norm_mix_pre, norm_mix_post, norm_ffn_pre, norm_ffn_post, na_w_qkv, na_w_o, na_rpb, dil_w_qkv, dil_w_o, ffn_w_gate, ffn_w_up, ffn_w_down, loss_target, m_norm_mix_pre, m_norm_mix_post, m_norm_ffn_pre, m_norm_ffn_post, m_na_w_qkv, m_na_w_o, m_na_rpb, m_dil_w_qkv, m_dil_w_o, m_ffn_w_gate, m_ffn_w_up, m_ffn_w_down, v_norm_mix_pre, v_norm_mix_post, v_norm_ffn_pre, v_norm_ffn_post, v_na_w_qkv, v_na_w_o, v_na_rpb, v_dil_w_qkv, v_dil_w_o, v_ffn_w_gate, v_ffn_w_up, v_ffn_w_down):
    na_cols, dil_cols, o_rows = 3 * D_MODEL // N_DEV, 9 * D_MODEL // N_DEV, D_MODEL // N_DEV
    ff_pad = FF_PAD - FF_SHARD
    me = (4 * lax.axis_index("x") + 2 * lax.axis_index("y") + lax.axis_index("c")).astype(jnp.int32).reshape(1)

    full = {
        "na": [((D_MODEL, 3 * D_MODEL), _columns(na_cols)), ((N_DEV, o_rows, D_MODEL), _leading)],
        "dil": [((D_MODEL, 9 * D_MODEL), _columns(dil_cols)), ((N_DEV, o_rows, D_MODEL), _leading)],
        "ffn0": [((N_DEV, 2, D_MODEL, FF_PAD), _leading), ((N_DEV, FF_PAD, D_MODEL), _leading)],
        "ffn1": [((N_DEV, 2, D_MODEL, FF_PAD), _leading), ((N_DEV, FF_PAD, D_MODEL), _leading)],
    }
    block = {
        "na": [(D_MODEL, na_cols), (o_rows, D_MODEL)], "dil": [(D_MODEL, dil_cols), (o_rows, D_MODEL)],
        "ffn0": [(2, D_MODEL, FF_PAD), (FF_PAD, D_MODEL)], "ffn1": [(2, D_MODEL, FF_PAD), (FF_PAD, D_MODEL)],
    }

    land_shapes = [jax.ShapeDtypeStruct(full[g][t][0], BF16) for g in GROUPS for t in range(2)]
    windows = [full[g][t][1] for g in GROUPS for t in range(2)]
    shards, lands = _prep_weights(me, na_w_qkv, na_w_o, dil_w_qkv, dil_w_o, ffn_w_gate, ffn_w_up, ffn_w_down, land_shapes)
    sems, shards, lands = _gather_start("gather_start", shards, lands, windows, [2] * len(GROUPS))

    def fetch(group, early, late):
        gi = GROUPS.index(group)
        mine = slice(2 * gi, 2 * gi + 2)
        pass_sems, shards_g, lands_g = _gather_pass_on(f"gather_pass_{group}", sems[gi], shards[mine], lands[mine],
                                                       windows[mine], early)
        qkv, o = _gather_wait(f"gather_wait_{group}", sems[gi], pass_sems, shards_g, lands_g, windows[mine], late)
        return (qkv, o.reshape(D_MODEL, D_MODEL)) if group in ("na", "dil") else (qkv, o)

    def grad_source(group, t):
        return _columns(block[group][0][1]) if (group in ("na", "dil") and t == 0) else _leading

    in_flight = {}

    def emit(group, grads):
        if group in ("na", "dil"):
            grads = [grads[0], grads[1].reshape(N_DEV, o_rows, D_MODEL)]
        sets = [(t, grad_source(group, t), t, _by_distance) for t in range(2)]
        landing = [lax.empty((N_DEV - 1,) + block[group][t], BF16) for t in range(2)]
        sems_g, grads, landing, tok = _send_start(f"exchange_start_{group}", grads, landing, [sets])
        in_flight[group] = (sems_g[0], grads, landing, sets)
        return [tok]

    norms = (norm_mix_pre, norm_mix_post, norm_ffn_pre, norm_ffn_post)
    loss, grad_x, d_gains, d_rpb = _local_step(x[0], loss_target[0], norms, na_rpb[0], fetch, emit, [shards[0]])

    landed, sent = {}, {}

    def wait_for(group, after):
        sems_g, grads, landing, sets = in_flight[group]
        sent[group], landed[group] = _send_wait(f"exchange_wait_{group}", sems_g, grads, landing, sets, after)

    for group in ("ffn1", "dil", "ffn0"):
        wait_for(group, [grad_x])

    def one(rows, tile, ncols, columns):
        own = (pl.BlockSpec((tile, ncols), lambda i, me: (i, me[0])) if columns
               else pl.BlockSpec((None, tile, ncols), lambda i, me: (me[0], i, 0)))
        return dict(grid=(rows // tile,), land_specs=[pl.BlockSpec((N_DEV - 1, tile, ncols), lambda i, me: (0, i, 0))],
                    own_specs=[own], p_spec=pl.BlockSpec((None, tile, ncols), lambda i, me: (0, i, 0)))

    def layered(block_shape, index, p_block, n_tiles):
        def specs(lead_size, lead):
            shape = (lead_size,) + block_shape
            return [pl.BlockSpec(shape, lambda l, r, me: index(lead(me), jnp.where(l == 0, r, n_tiles - 1))),
                    pl.BlockSpec(shape, lambda l, r, me: index(lead(me), jnp.where(l == 0, 0, r)))]
        return dict(grid=(2, n_tiles), land_specs=specs(N_DEV - 1, lambda me: 0), own_specs=specs(None, lambda me: me[0]),
                    p_spec=pl.BlockSpec(p_block, lambda l, r, me: (l, r, 0)))

    gu_lands, gu_owns = [landed["ffn0"][0], landed["ffn1"][0]], [sent["ffn0"][0], sent["ffn1"][0]]
    down_lands, down_owns = [landed["ffn0"][1], landed["ffn1"][1]], [sent["ffn0"][1], sent["ffn1"][1]]
    updates = {
        "dil_w_qkv": _adamw("adamw_dil_qkv", me, [landed["dil"][0]], [sent["dil"][0]], dil_w_qkv, m_dil_w_qkv, v_dil_w_qkv,
                            **one(D_MODEL, 128, dil_cols, True)),
        "dil_w_o": _adamw("adamw_dil_o", me, [landed["dil"][1]], [sent["dil"][1]], dil_w_o, m_dil_w_o, v_dil_w_o,
                          **one(o_rows, o_rows, D_MODEL, False)),
        "ffn_w_gate": _adamw("adamw_gate", me, gu_lands, gu_owns, ffn_w_gate, m_ffn_w_gate, v_ffn_w_gate,
                             **layered((None, 128, FF_PAD), lambda lead, r: (lead, 0, r, 0), (None, 128, FF_SHARD), 8)),
        "ffn_w_up": _adamw("adamw_up", me, gu_lands, gu_owns, ffn_w_up, m_ffn_w_up, v_ffn_w_up,
                           **layered((None, 128, FF_PAD), lambda lead, r: (lead, 1, r, 0), (None, 128, FF_SHARD), 8)),
        "ffn_w_down": _adamw("adamw_down", me, down_lands, down_owns, ffn_w_down, m_ffn_w_down, v_ffn_w_down,
                             **layered((176, D_MODEL), lambda lead, r: (lead, r, 0), (None, 176, D_MODEL), 2)),
    }
    done = [u[0] for u in updates.values()]
    small = _all_gather("gather_small", [_pack_small(d_gains, d_rpb, loss)], [jax.ShapeDtypeStruct((N_DEV, 128, 128), F32)],
                        [_leading], deps=done)[0]
    wait_for("na", [small])
    updates["na_w_qkv"] = _adamw("adamw_na_qkv", me, [landed["na"][0]], [sent["na"][0]], na_w_qkv, m_na_w_qkv, v_na_w_qkv,
                                 **one(D_MODEL, 256, na_cols, True))
    updates["na_w_o"] = _adamw("adamw_na_o", me, [landed["na"][1]], [sent["na"][1]], na_w_o, m_na_w_o, v_na_w_o,
                               **one(o_rows, o_rows, D_MODEL, False))
    gains = [norm_mix_pre, norm_mix_post, norm_ffn_pre, norm_ffn_post]
    m_gains = [m_norm_mix_pre, m_norm_mix_post, m_norm_ffn_pre, m_norm_ffn_post]
    v_gains = [v_norm_mix_pre, v_norm_mix_post, v_norm_ffn_pre, v_norm_ffn_post]
    packed = _adamw("adamw_small", me, [small], (), _pack_small(gains, na_rpb)[None], _pack_small(m_gains, m_na_rpb)[None],
                    _pack_small(v_gains, v_na_rpb)[None], grid=(1,),
                    land_specs=[pl.BlockSpec((N_DEV, 128, 128), lambda i, me: (0, 0, 0))], own_specs=[],
                    p_spec=pl.BlockSpec((None, 128, 128), lambda i, me: (0, 0, 0)))
    small_out = [_unpack_small(p[0]) for p in packed]

    order = ["na_w_qkv", "na_w_o", "na_rpb", "dil_w_qkv", "dil_w_o", "ffn_w_gate", "ffn_w_up", "ffn_w_down"]
    result = [packed[0][0, 127, 127], grad_x[None]]
    for kind in range(4):
        gains_k, rpb_k = small_out[kind]
        result += gains_k
        result += [rpb_k if name == "na_rpb" else updates[name][kind] for name in order]
    return tuple(result)
```

```python
import functools

import jax
import jax.numpy as jnp
from jax import lax
from jax.experimental import pallas as pl
from jax.experimental.pallas import tpu as pltpu

F32 = jnp.float32
BF16 = jnp.bfloat16
MESH = pl.DeviceIdType.MESH
ANY = pl.BlockSpec(memory_space=pl.ANY)
HBM = pl.BlockSpec(memory_space=pltpu.HBM)
SEM = pl.BlockSpec(memory_space=pltpu.SEMAPHORE)
EFFECT = pltpu.SideEffectType.DATAFLOW_SIDE_EFFECTING

N_DEV = 8
SEQ = 2048
D_MODEL = 1024
N_HEADS = 16
HEAD_DIM = 64
GRID_W = 64
NA_ROWS = 8
SEQ_ROWS = SEQ // GRID_W
DIL_GROUPS = ((128, 1), (512, 4), (2048, 16))
BAND = 128
RADIUS = 64
FF_SHARD = 352
FF_PAD = 384
RMS_EPS = 1e-6
NEG_INF = -1e30
Q_SCALE = HEAD_DIM ** -0.5

ADAM_LR = 0.001
ADAM_B1 = 0.9
ADAM_B2 = 0.999
ADAM_EPS = 1e-08
ADAM_WD = 0.01
ADAM_STEP = 10

VMEM_LIMIT = 56 * 1024 * 1024
TM = 512
TM_MM = 1024

NN = (((1,), (0,)), ((), ()))
NT = (((1,), (1,)), ((), ()))
TN = (((0,), (0,)), ((), ()))


def _params():
    return pltpu.CompilerParams(vmem_limit_bytes=VMEM_LIMIT)


def _matmul(name, a, b, *, grid, a_spec, b_spec, o_spec, out_shape, dims, acc_shape, deps=(), inner=1):
    nk = grid[-1]
    kaxis = len(grid) - 1

    def body(a_ref, b_ref, *rest):
        o_ref, acc_ref = rest[-2], rest[-1]
        if inner == 1:
            part = lax.dot_general(a_ref[...].astype(BF16), b_ref[...].astype(BF16), dims, preferred_element_type=F32)
        elif len(b_ref.shape) == 2:
            a_all = jnp.concatenate([a_ref[j].astype(BF16) for j in range(inner)], axis=1)
            part = lax.dot_general(a_all, b_ref[...].astype(BF16), dims, preferred_element_type=F32)
        else:
            part = sum(lax.dot_general(a_ref[j].astype(BF16), b_ref[j].astype(BF16), dims, preferred_element_type=F32)
                       for j in range(inner))
        if nk == 1:
            o_ref[...] = part.astype(o_ref.dtype)
        else:
            k = pl.program_id(kaxis)

            @pl.when(k == 0)
            def _():
                acc_ref[...] = part

            @pl.when(k > 0)
            def _():
                acc_ref[...] += part

            @pl.when(k == nk - 1)
            def _():
                o_ref[...] = acc_ref[...].astype(o_ref.dtype)

    return pl.pallas_call(
        body, out_shape=out_shape, grid=grid, in_specs=[a_spec, b_spec] + [ANY] * len(deps), out_specs=o_spec,
        scratch_shapes=[pltpu.VMEM(acc_shape, F32)], name=name, compiler_params=_params())(a, b, *deps)


SORTED = tuple(d for _, d in DIL_GROUPS if d > 1)
LANE_CHUNKS = D_MODEL // 128


def _sort_scratch(tm=TM):
    return pltpu.VMEM((LANE_CHUNKS, tm, 128), F32)


def _sorted_view(t, dil):
    return t.reshape(dil, SEQ // dil, D_MODEL)


def _sorted_spec(dil, lead=(), tm=TM):
    return pl.BlockSpec((None,) * len(lead) + (dil, tm // dil, D_MODEL), lambda i: tuple(lead) + (0, i, 0))


def _sort_tile(scratch, value, dil, out_ref):
    tm = value.shape[0]
    for c in range(LANE_CHUNKS):
        scratch[c] = value[:, 128 * c:128 * (c + 1)]
    for r in range(dil):
        rows = [scratch.at[c][pl.ds(r, tm // dil, stride=dil), :] for c in range(LANE_CHUNKS)]
        out_ref[r] = jnp.concatenate(rows, axis=1).astype(out_ref.dtype)


def _unsort_tile(scratch, in_ref, dil):
    for r in range(dil):
        value = in_ref[r].astype(F32)
        for c in range(LANE_CHUNKS):
            scratch.at[c][pl.ds(r, value.shape[0], stride=dil), :] = value[:, 128 * c:128 * (c + 1)]
    return jnp.concatenate([scratch[c] for c in range(LANE_CHUNKS)], axis=1)


def _rms_fwd(name, x, g, res=None, out_dtype=F32, deps=(), sorted_too=False):
    n_tiles = SEQ // TM
    has_res = res is not None
    n_in = 2 + has_res + len(deps)

    def body(*refs):
        x_ref, g_ref = refs[0], refs[1]
        xv = x_ref[...]
        r = lax.rsqrt(jnp.mean(xv * xv, axis=-1, keepdims=True) + RMS_EPS)
        y = xv * r * g_ref[...]
        if has_res:
            y = refs[2][...] + y
        refs[n_in][...] = y.astype(out_dtype)
        if sorted_too:
            for j, dil in enumerate(SORTED):
                _sort_tile(refs[-1], y, dil, refs[n_in + 1 + j])

    tile = pl.BlockSpec((TM, D_MODEL), lambda i: (i, 0))
    gspec = pl.BlockSpec((1, D_MODEL), lambda i: (0, 0))
    ins = [x, g] + ([res] if has_res else []) + list(deps)
    specs = [tile, gspec] + ([tile] if has_res else []) + [ANY] * len(deps)
    shapes, out_specs = [jax.ShapeDtypeStruct((SEQ, D_MODEL), out_dtype)], [tile]
    if sorted_too:
        shapes += [jax.ShapeDtypeStruct((dil, SEQ // dil, D_MODEL), out_dtype) for dil in SORTED]
        out_specs += [_sorted_spec(dil) for dil in SORTED]
    result = pl.pallas_call(
        body, out_shape=tuple(shapes), grid=(n_tiles,), in_specs=specs, out_specs=tuple(out_specs),
        scratch_shapes=[_sort_scratch()] if sorted_too else [], name=name, compiler_params=_params())(*ins)
    return result if sorted_too else result[0]


def _rms_bwd(name, x, g, dys, res=None, out_dtype=F32, groups=None):
    n_tiles = SEQ // TM
    n_dy = len(dys) if groups is None else 1 + len(SORTED)
    has_res = res is not None

    def body(*refs):
        x_ref, g_ref = refs[0], refs[1]
        dy_refs = refs[2:2 + n_dy]
        res_ref = refs[2 + n_dy] if has_res else None
        dx_ref, dg_ref, acc_ref = refs[2 + n_dy + has_res:5 + n_dy + has_res]
        i = pl.program_id(0)
        xv = x_ref[...]
        r = lax.rsqrt(jnp.mean(xv * xv, axis=-1, keepdims=True) + RMS_EPS)
        xn = xv * r
        dy = dy_refs[0][...].astype(F32)
        for j, extra in enumerate(dy_refs[1:]):
            dy = dy + (extra[...].astype(F32) if groups is None else _unsort_tile(refs[-1], extra, SORTED[j]))
        dyg = dy * g_ref[...]
        dx = r * (dyg - xn * jnp.mean(dyg * xn, axis=-1, keepdims=True))
        if has_res:
            dx = res_ref[...] + dx
        dx_ref[...] = dx.astype(dx_ref.dtype)
        part = jnp.sum((dy * xn).reshape(TM // 8, 8, D_MODEL), axis=0)

        @pl.when(i == 0)
        def _():
            acc_ref[...] = part

        @pl.when(i > 0)
        def _():
            acc_ref[...] += part

        @pl.when(i == n_tiles - 1)
        def _():
            dg_ref[...] = jnp.broadcast_to(jnp.sum(acc_ref[...], axis=0, keepdims=True), (8, D_MODEL))

    tile = pl.BlockSpec((TM, D_MODEL), lambda i: (i, 0))
    gspec = pl.BlockSpec((1, D_MODEL), lambda i: (0, 0))
    if groups is None:
        dy_ins, dy_specs = list(dys), [tile] * n_dy
    else:
        dy_ins = [groups] + [groups.reshape(n_dy, dil, SEQ // dil, D_MODEL) for dil in SORTED]
        dy_specs = [pl.BlockSpec((None, TM, D_MODEL), lambda i: (0, i, 0))]
        dy_specs += [_sorted_spec(dil, lead=(1 + j,)) for j, dil in enumerate(SORTED)]
    ins = [x, g] + dy_ins + ([res] if has_res else [])
    specs = [tile, gspec] + dy_specs + ([tile] if has_res else [])
    dx, dg = pl.pallas_call(
        body, out_shape=(jax.ShapeDtypeStruct((SEQ, D_MODEL), out_dtype), jax.ShapeDtypeStruct((8, D_MODEL), F32)),
        grid=(n_tiles,), in_specs=specs,
        out_specs=(tile, pl.BlockSpec((8, D_MODEL), lambda i: (0, 0))),
        scratch_shapes=[pltpu.VMEM((8, D_MODEL), F32)] + ([_sort_scratch()] if groups is not None else []),
        name=name, compiler_params=_params())(*ins)
    return dx, dg[0:1]


def _loss_head(name, y, target):
    n_tiles = SEQ // TM

    def body(y_ref, t_ref, dy_ref, loss_ref, acc_ref):
        i = pl.program_id(0)
        diff = y_ref[...] - t_ref[...]
        dy_ref[...] = diff * (1.0 / D_MODEL)
        part = jnp.sum((diff * diff).reshape(TM // 8, 8, D_MODEL), axis=0)

        @pl.when(i == 0)
        def _():
            acc_ref[...] = part

        @pl.when(i > 0)
        def _():
            acc_ref[...] += part

        @pl.when(i == n_tiles - 1)
        def _():
            loss_ref[...] = jnp.full((8, 128), jnp.sum(acc_ref[...]) * (0.5 / D_MODEL), F32)

    tile = pl.BlockSpec((TM, D_MODEL), lambda i: (i, 0))
    dy, loss = pl.pallas_call(
        body, out_shape=(jax.ShapeDtypeStruct((SEQ, D_MODEL), F32), jax.ShapeDtypeStruct((8, 128), F32)),
        grid=(n_tiles,), in_specs=[tile, tile], out_specs=(tile, pl.BlockSpec((8, 128), lambda i: (0, 0))),
        scratch_shapes=[pltpu.VMEM((8, D_MODEL), F32)], name=name, compiler_params=_params())(y, target)
    return dy, loss[0, 0]


def _row_index(shape):
    return lax.broadcasted_iota(jnp.int32, shape, 0)


def _lane_index(shape):
    return lax.broadcasted_iota(jnp.int32, shape, len(shape) - 1)


def _skew_rows(t, direction):
    q = _row_index(t.shape) & (GRID_W - 1)
    for bit in range(6):
        step = 1 << bit
        shift = step if direction > 0 else 128 - step
        t = jnp.where((q & step) != 0, pltpu.roll(t, shift, 1), t)
    return t


def _rpb_table(rpb_pad):
    rows = 16 * GRID_W

    def body(r_ref, t_ref):
        lane = _lane_index((rows, 128))
        v = pltpu.roll(r_ref[...], 128 - 15, 1)
        t = _skew_rows(jnp.broadcast_to(v[:, None, :], (16, GRID_W, 128)).reshape(rows, 128), +1)
        t = jnp.where(lane < GRID_W, t, 0.0)
        below = jnp.concatenate([t[GRID_W:], jnp.zeros((GRID_W, 128), F32)], axis=0)
        first_col = jnp.clip((_row_index((rows, 128)) & (GRID_W - 1)) - 8, 0, GRID_W - 16)
        key_col = lane & (GRID_W - 1)
        in_window = (key_col >= first_col) & (key_col < first_col + 16)
        t_ref[...] = jnp.where(in_window, t + pltpu.roll(below, GRID_W, 1), NEG_INF).reshape(16, GRID_W, 128)

    return pl.pallas_call(
        body, out_shape=jax.ShapeDtypeStruct((N_HEADS, 16, GRID_W, 128), F32), grid=(N_HEADS,),
        in_specs=[pl.BlockSpec((None, 16, 128), lambda h: (h, 0, 0))],
        out_specs=pl.BlockSpec((None, 16, GRID_W, 128), lambda h: (h, 0, 0, 0)),
        name="rpb_table", compiler_params=_params())(rpb_pad)


def _rpb_grad(gp):
    rows = 16 * GRID_W

    def body(g_ref, o_ref):
        lane = _lane_index((rows, 128))
        g = g_ref[...].reshape(rows, 128)
        low = jnp.where(lane < GRID_W, g, 0.0)
        high = pltpu.roll(jnp.where(lane >= GRID_W, g, 0.0), GRID_W, 1)
        above = jnp.concatenate([jnp.zeros((GRID_W, 128), F32), high[:rows - GRID_W]], axis=0)
        diag = jnp.sum(_skew_rows(low + above, -1).reshape(16, GRID_W, 128), axis=1)
        o_ref[...] = pltpu.roll(diag, 15, 1)

    return pl.pallas_call(
        body, out_shape=jax.ShapeDtypeStruct((N_HEADS, 16, 128), F32), grid=(N_HEADS,),
        in_specs=[pl.BlockSpec((None, 16, GRID_W, 128), lambda h: (h, 0, 0, 0))],
        out_specs=pl.BlockSpec((None, 16, 128), lambda h: (h, 0, 0)),
        name="rpb_grad", compiler_params=_params())(gp)


NA_KEYS = NA_ROWS * GRID_W


def _na_window(i):
    first_row = jnp.clip(i - NA_ROWS // 2, 0, SEQ_ROWS - NA_ROWS)
    return pl.multiple_of(first_row * GRID_W, GRID_W), first_row - i + NA_ROWS - 1


NA_STEP = 8


def _head_masks():
    lane = _lane_index((1, 128))
    return (lane < HEAD_DIM, lane >= HEAD_DIM)


def _stack_heads(t, masks):
    zero = jnp.zeros_like(t)
    return jnp.concatenate([jnp.where(masks[0], t, zero), jnp.where(masks[1], t, zero)], axis=0)


def _unstack_heads(t, masks):
    n = t.shape[0] // 2
    return jnp.where(masks[0], t[:n], t[n:])


def _stack_columns(t):
    return jnp.concatenate([t[:, 0:1], t[:, HEAD_DIM:HEAD_DIM + 1]], axis=0)


def _na_scores(qs, kw, tp_ref, dr0):
    s = lax.dot_general(qs, kw, NT, preferred_element_type=F32)
    bias = jnp.concatenate(
        [jnp.concatenate([tp_ref[a, pl.ds(dr0 + 2 * c, 1), :, :].reshape(GRID_W, 128) for c in range(4)], axis=1)
         for a in range(2)], axis=0)
    return s + bias


def _na_specs():
    q_spec = pl.BlockSpec((NA_STEP * GRID_W, 128), lambda hp, i: (i, hp))
    k_spec = pl.BlockSpec((SEQ, 128), lambda hp, i: (0, 8 + hp))
    v_spec = pl.BlockSpec((SEQ, 128), lambda hp, i: (0, 16 + hp))
    tp_spec = pl.BlockSpec((2, 16, GRID_W, 128), lambda hp, i: (hp, 0, 0, 0))
    return q_spec, k_spec, v_spec, tp_spec


def _na_fwd(qkv, table):
    def body(q_ref, k_ref, v_ref, tp_ref, o_ref, lse_ref):
        masks = _head_masks()
        for r in range(NA_STEP):
            rows = slice(r * GRID_W, (r + 1) * GRID_W)
            start, dr0 = _na_window(pl.program_id(1) * NA_STEP + r)
            kw = k_ref[pl.ds(start, NA_KEYS), :]
            vw = v_ref[pl.ds(start, NA_KEYS), :]
            s = _na_scores(_stack_heads(q_ref[rows, :] * Q_SCALE, masks), kw, tp_ref, dr0)
            m = jnp.max(s, axis=-1, keepdims=True)
            p = jnp.exp(s - m)
            denom = jnp.sum(p, axis=-1, keepdims=True)
            out = jnp.dot(p.astype(BF16), vw, preferred_element_type=F32) / denom
            o_ref[rows, :] = _unstack_heads(out, masks).astype(o_ref.dtype)
            lse_ref[rows, :] = _unstack_heads(jnp.broadcast_to(m + jnp.log(denom), (2 * GRID_W, 128)), masks)

    q_spec, k_spec, v_spec, tp_spec = _na_specs()
    return pl.pallas_call(
        body, out_shape=(jax.ShapeDtypeStruct((SEQ, D_MODEL), BF16), jax.ShapeDtypeStruct((SEQ, D_MODEL), F32)),
        grid=(N_HEADS // 2, SEQ_ROWS // NA_STEP), in_specs=[q_spec, k_spec, v_spec, tp_spec],
        out_specs=(q_spec, q_spec), name="na_fwd", compiler_params=_params())(qkv, qkv, qkv, table)


def _na_bwd(qkv, table, d_out, lse):
    def body(q_ref, k_ref, v_ref, tp_ref, do_ref, lse_ref, dqkv_ref, gp_ref, dk_acc, dv_acc):
        step = pl.program_id(1)

        @pl.when(step == 0)
        def _():
            dk_acc[...] = jnp.zeros_like(dk_acc)
            dv_acc[...] = jnp.zeros_like(dv_acc)
            gp_ref[...] = jnp.zeros_like(gp_ref)

        masks = _head_masks()
        for r in range(NA_STEP):
            rows = slice(r * GRID_W, (r + 1) * GRID_W)
            i = step * NA_STEP + r
            start, dr0 = _na_window(i)
            kw = k_ref[pl.ds(start, NA_KEYS), :]
            vw = v_ref[pl.ds(start, NA_KEYS), :]
            qs = _stack_heads(q_ref[rows, :] * Q_SCALE, masks)
            dos = _stack_heads(do_ref[rows, :], masks)
            p = jnp.exp(_na_scores(qs, kw, tp_ref, dr0) - _stack_columns(lse_ref[rows, :]))
            dp = lax.dot_general(dos, vw, NT, preferred_element_type=F32)
            ds = p * (dp - jnp.sum(p * dp, axis=-1, keepdims=True))
            for a in range(2):
                for c in range(4):
                    gp_ref[a, pl.ds(dr0 + 2 * c, 1), :, :] += (
                        ds[a * GRID_W:(a + 1) * GRID_W, 128 * c:128 * (c + 1)].reshape(1, GRID_W, 128))
            dsb = ds.astype(BF16)
            dq = _unstack_heads(jnp.dot(dsb, kw, preferred_element_type=F32), masks) * Q_SCALE
            dqkv_ref[0, pl.ds(pl.multiple_of(i * GRID_W, GRID_W), GRID_W), :] = dq.astype(dqkv_ref.dtype)
            dk_acc[pl.ds(start, NA_KEYS), :] += lax.dot_general(dsb, qs, TN, preferred_element_type=F32)
            dv_acc[pl.ds(start, NA_KEYS), :] += lax.dot_general(p.astype(BF16), dos, TN, preferred_element_type=F32)

        @pl.when(step == SEQ_ROWS // NA_STEP - 1)
        def _():
            dqkv_ref[1] = dk_acc[...].astype(dqkv_ref.dtype)
            dqkv_ref[2] = dv_acc[...].astype(dqkv_ref.dtype)

    q_spec, k_spec, v_spec, tp_spec = _na_specs()
    return pl.pallas_call(
        body,
        out_shape=(jax.ShapeDtypeStruct((3, SEQ, D_MODEL), BF16), jax.ShapeDtypeStruct((N_HEADS, 16, GRID_W, 128), F32)),
        grid=(N_HEADS // 2, SEQ_ROWS // NA_STEP), in_specs=[q_spec, k_spec, v_spec, tp_spec, q_spec, q_spec],
        out_specs=(pl.BlockSpec((3, SEQ, 128), lambda hp, i: (0, 0, hp)), tp_spec),
        scratch_shapes=[pltpu.VMEM((SEQ, 128), F32), pltpu.VMEM((SEQ, 128), F32)],
        name="na_bwd", compiler_params=_params())(qkv, qkv, qkv, table, d_out, lse)


DIL_STEP = 4


def _dil_geometry(group):
    dil = DIL_GROUPS[group][1]
    sub_len = SEQ // dil
    blocks = sub_len // BAND
    return dil, sub_len, max(blocks // DIL_STEP, 1), max(DIL_STEP // blocks, 1), min(2 * BAND, sub_len)


def _dil_block(step, r, sub_len, subs):
    per_sub = DIL_STEP // subs
    return (r // per_sub) * sub_len, step * per_sub + r % per_sub


def _dil_window(b, sub_len, n_keys):
    if n_keys == sub_len:
        return 0
    return pl.multiple_of(jnp.clip(b * BAND - RADIUS, 0, sub_len - n_keys), RADIUS)


def _dil_penalties(group, slopes):
    _, sub_len, _, _, n_keys = _dil_geometry(group)
    cases = 1 if n_keys == sub_len else 3
    lead = RADIUS * jnp.arange(cases)[:, None, None]
    dist = jnp.abs(lead + jnp.arange(BAND)[None, :, None] - jnp.arange(n_keys)[None, None, :])
    dist = dist[:, None]
    penalty = jnp.where(dist <= RADIUS, slopes[None, :, None, None] * dist.astype(F32), -NEG_INF)
    return penalty.reshape(cases, N_HEADS // 2, 2 * BAND, n_keys)


def _dil_penalty(table_ref, b, start):
    if table_ref.shape[0] == 1:
        return table_ref[0]
    return table_ref[pl.ds((b * BAND - start) // RADIUS, 1), :, :].reshape(table_ref.shape[1:])


def _dil_scores(qs, kw, penalty):
    return lax.dot_general(qs, kw, NT, preferred_element_type=F32) - penalty


def _dil_specs(group):
    dil, sub_len, steps, subs, n_keys = _dil_geometry(group)
    col = group * 24
    rows = DIL_STEP * BAND
    q_spec = pl.BlockSpec((rows, 128), lambda n, hp, b: (n * steps + b, col + hp))
    k_spec = pl.BlockSpec((subs * sub_len, 128), lambda n, hp, b: (n, col + 8 + hp))
    v_spec = pl.BlockSpec((subs * sub_len, 128), lambda n, hp, b: (n, col + 16 + hp))
    tile = pl.BlockSpec((rows, 128), lambda n, hp, b: (n * steps + b, hp))
    table = pl.BlockSpec((1 if n_keys == sub_len else 3, None, 2 * BAND, n_keys), lambda n, hp, b: (0, hp, 0, 0))
    return (dil // subs, N_HEADS // 2, steps), q_spec, k_spec, v_spec, tile, table


def _dil_fwd(group, qkv, penalties):
    _, sub_len, _, subs, n_keys = _dil_geometry(group)

    def body(q_ref, k_ref, v_ref, table_ref, o_ref, lse_ref):
        masks = _head_masks()
        for r in range(DIL_STEP):
            rows = slice(r * BAND, (r + 1) * BAND)
            base, b = _dil_block(pl.program_id(2), r, sub_len, subs)
            start = _dil_window(b, sub_len, n_keys)
            kw = k_ref[pl.ds(base + start, n_keys), :]
            vw = v_ref[pl.ds(base + start, n_keys), :]
            s = _dil_scores(_stack_heads(q_ref[rows, :] * Q_SCALE, masks), kw, _dil_penalty(table_ref, b, start))
            m = jnp.max(s, axis=-1, keepdims=True)
            p = jnp.exp(s - m)
            denom = jnp.sum(p, axis=-1, keepdims=True)
            out = jnp.dot(p.astype(BF16), vw, preferred_element_type=F32) / denom
            o_ref[rows, :] = _unstack_heads(out, masks)
            lse_ref[rows, :] = _unstack_heads(jnp.broadcast_to(m + jnp.log(denom), (2 * BAND, 128)), masks)

    grid, q_spec, k_spec, v_spec, tile, table = _dil_specs(group)
    return pl.pallas_call(
        body, out_shape=(jax.ShapeDtypeStruct((SEQ, D_MODEL), F32), jax.ShapeDtypeStruct((SEQ, D_MODEL), F32)),
        grid=grid, in_specs=[q_spec, k_spec, v_spec, table], out_specs=(tile, tile),
        name=f"dil_fwd_{group}", compiler_params=_params())(qkv, qkv, qkv, penalties)


def _dil_merge(outs, lses):
    n_sorted = len(SORTED)

    def body(*refs):
        o_refs, l_refs = refs[:3], refs[3:6]
        out_refs, lse_refs, scratch = refs[6:7 + n_sorted], refs[7 + n_sorted:8 + 2 * n_sorted], refs[-1]
        os_ = [o_refs[0][...]] + [_unsort_tile(scratch, o_refs[1 + j], dil) for j, dil in enumerate(SORTED)]
        ls = [l_refs[0][...]] + [_unsort_tile(scratch, l_refs[1 + j], dil) for j, dil in enumerate(SORTED)]
        m = jnp.maximum(jnp.maximum(ls[0], ls[1]), ls[2])
        es = [jnp.exp(v - m) for v in ls]
        total = es[0] + es[1] + es[2]
        merged = (es[0] * os_[0] + es[1] * os_[1] + es[2] * os_[2]) / total
        lse = m + jnp.log(total)
        out_refs[0][...] = merged
        lse_refs[0][...] = lse
        for j, dil in enumerate(SORTED):
            _sort_tile(scratch, merged, dil, out_refs[1 + j])
            _sort_tile(scratch, lse, dil, lse_refs[1 + j])

    tm = 256
    tile = pl.BlockSpec((tm, D_MODEL), lambda i: (i, 0))
    specs = [tile] + [_sorted_spec(dil, tm=tm) for dil in SORTED]
    shapes = [jax.ShapeDtypeStruct((SEQ, D_MODEL), F32)] + [jax.ShapeDtypeStruct((dil, SEQ // dil, D_MODEL), F32) for dil in SORTED]
    views = lambda ts: [ts[0]] + [_sorted_view(t, dil) for t, dil in zip(ts[1:], SORTED)]
    result = pl.pallas_call(
        body, out_shape=tuple(shapes * 2), grid=(SEQ // tm,), in_specs=specs * 2, out_specs=tuple(specs * 2),
        scratch_shapes=[_sort_scratch(tm)], name="dil_merge", compiler_params=_params())(*views(outs), *views(lses))
    flat = [t.reshape(SEQ, D_MODEL) for t in result]
    return flat[:1 + n_sorted], flat[1 + n_sorted:]


def _dil_bwd(group, qkv, penalties, d_out, out, lse_group, lse_total, into):
    _, sub_len, steps, subs, n_keys = _dil_geometry(group)

    def body(q_ref, k_ref, v_ref, table_ref, do_ref, o_ref, lg_ref, lt_ref, into_ref, dqkv_ref, dk_acc, dv_acc):
        step = pl.program_id(2)

        @pl.when(step == 0)
        def _():
            dk_acc[...] = jnp.zeros_like(dk_acc)
            dv_acc[...] = jnp.zeros_like(dv_acc)

        masks = _head_masks()
        for r in range(DIL_STEP):
            rows = slice(r * BAND, (r + 1) * BAND)
            base, b = _dil_block(step, r, sub_len, subs)
            start = _dil_window(b, sub_len, n_keys)
            keys = pl.ds(base + start, n_keys)
            kw = k_ref[keys, :]
            vw = v_ref[keys, :]
            qs = _stack_heads(q_ref[rows, :] * Q_SCALE, masks)
            lse2 = lg_ref[rows, :]
            weight = jnp.exp(lse2 - lt_ref[rows, :])
            do2 = do_ref[rows, :]
            dogs = _stack_heads((weight * do2).astype(BF16), masks)
            delta = _stack_columns(weight) * jnp.sum(_stack_heads(do2 * o_ref[rows, :], masks), axis=-1, keepdims=True)
            p = jnp.exp(_dil_scores(qs, kw, _dil_penalty(table_ref, b, start)) - _stack_columns(lse2))
            dp = lax.dot_general(dogs, vw, NT, preferred_element_type=F32)
            dsb = (p * (dp - delta)).astype(BF16)
            dq = _unstack_heads(jnp.dot(dsb, kw, preferred_element_type=F32), masks) * Q_SCALE
            dqkv_ref[0, pl.ds(pl.multiple_of(base + b * BAND, BAND), BAND), :] = dq.astype(dqkv_ref.dtype)
            dk_acc[keys, :] += lax.dot_general(dsb, qs, TN, preferred_element_type=F32)
            dv_acc[keys, :] += lax.dot_general(p.astype(BF16), dogs, TN, preferred_element_type=F32)

        @pl.when(step == steps - 1)
        def _():
            dqkv_ref[1] = dk_acc[...].astype(dqkv_ref.dtype)
            dqkv_ref[2] = dv_acc[...].astype(dqkv_ref.dtype)

    grid, q_spec, k_spec, v_spec, tile, table = _dil_specs(group)
    return pl.pallas_call(
        body, out_shape=jax.ShapeDtypeStruct(into.shape, into.dtype), grid=grid,
        in_specs=[q_spec, k_spec, v_spec, table, tile, tile, tile, tile, ANY],
        out_specs=pl.BlockSpec((3, subs * sub_len, 128), lambda n, hp, b: (group, n, hp)),
        scratch_shapes=[pltpu.VMEM((subs * sub_len, 128), F32), pltpu.VMEM((subs * sub_len, 128), F32)],
        input_output_aliases={8: 0}, name=f"dil_bwd_{group}", compiler_params=_params(),
    )(qkv, qkv, qkv, penalties, d_out, out, lse_group, lse_total, into)


def _ffn_up(name, h, w_gu):
    def body(h_ref, wg_ref, wu_ref, act_ref, hg_ref, hu_ref, act_t_ref):
        hv = h_ref[...]
        hg = jnp.dot(hv, wg_ref[...], preferred_element_type=F32)
        hu = jnp.dot(hv, wu_ref[...], preferred_element_type=F32)
        act = hg * jax.nn.sigmoid(hg) * hu
        act_ref[...] = act.astype(act_ref.dtype)
        act_t_ref[...] = act.T.astype(act_t_ref.dtype)
        hg_ref[...] = hg.astype(hg_ref.dtype)
        hu_ref[...] = hu.astype(hu_ref.dtype)

    out = pl.BlockSpec((None, TM_MM, FF_PAD), lambda d, i: (d, i, 0))
    shape = jax.ShapeDtypeStruct((N_DEV, SEQ, FF_PAD), BF16)
    return pl.pallas_call(
        body, out_shape=(shape, shape, shape, jax.ShapeDtypeStruct((N_DEV, FF_PAD, SEQ), BF16)), grid=(N_DEV, SEQ // TM_MM),
        in_specs=[pl.BlockSpec((TM_MM, D_MODEL), lambda d, i: (i, 0)),
                  pl.BlockSpec((None, None, D_MODEL, FF_PAD), lambda d, i: (d, 0, 0, 0)),
                  pl.BlockSpec((None, None, D_MODEL, FF_PAD), lambda d, i: (d, 1, 0, 0))],
        out_specs=(out, out, out, pl.BlockSpec((None, FF_PAD, TM_MM), lambda d, i: (d, 0, i))),
        name=name, compiler_params=_params())(h, w_gu, w_gu)


def _ffn_bwd_act(name, d_f, w_down, hg, hu):
    def body(df_ref, wd_ref, hg_ref, hu_ref, dgu_ref):
        dact = lax.dot_general(df_ref[...], wd_ref[...], NT, preferred_element_type=F32)
        hgv = hg_ref[...].astype(F32)
        sig = jax.nn.sigmoid(hgv)
        dgu_ref[0] = (dact * hu_ref[...].astype(F32) * (sig * (1.0 + hgv * (1.0 - sig)))).astype(dgu_ref.dtype)
        dgu_ref[1] = (dact * hgv * sig).astype(dgu_ref.dtype)

    tile = pl.BlockSpec((None, TM_MM, FF_PAD), lambda d, i: (d, i, 0))
    return pl.pallas_call(
        body, out_shape=jax.ShapeDtypeStruct((2, N_DEV, SEQ, FF_PAD), BF16), grid=(N_DEV, SEQ // TM_MM),
        in_specs=[pl.BlockSpec((TM_MM, D_MODEL), lambda d, i: (i, 0)),
                  pl.BlockSpec((None, FF_PAD, D_MODEL), lambda d, i: (d, 0, 0)), tile, tile],
        out_specs=pl.BlockSpec((2, None, TM_MM, FF_PAD), lambda d, i: (0, d, i, 0)),
        name=name, compiler_params=_params())(d_f, w_down, hg, hu)


def _position():
    return lax.axis_index("x"), lax.axis_index("y"), lax.axis_index("c")


def _flat(p):
    return 4 * p[0] + 2 * p[1] + p[2]


def _peer(me, k):
    x, y, c = me
    return (1 - x if k & 4 else x, 1 - y if k & 2 else y, 1 - c if k & 1 else c)


def _columns(width):
    return lambda ref, d: ref.at[:, pl.ds(pl.multiple_of(d * width, 128), width)]


def _leading(ref, d):
    return ref.at[d]


def _whole(ref, d):
    return ref


def _by_sender(window):
    return lambda ref, sender, k: window(ref, sender)


def _by_distance(ref, sender, k):
    return ref.at[k - 1]


def _prep_weights(me, na_qkv, na_o, dil_qkv, dil_o, gate, up, down, land_shapes):
    na_cols, dil_cols = na_qkv.shape[-1], dil_qkv.shape[-1]
    o_rows = na_o.shape[1]
    tiles = 4
    rows, rows_o = D_MODEL // tiles, o_rows // tiles

    def body(me_ref, naq, nao, dq, do_, g0, u0, d0, g1, u1, d1, *outs):
        def put(t, index, value):
            outs[t][index] = value
            outs[8 + t][index] = value

        put(0, ..., naq[...].astype(BF16))
        put(1, ..., nao[...].astype(BF16))
        put(4, ..., dq[...].astype(BF16))
        put(5, ..., do_[...].astype(BF16))
        for t, (g, u, d) in ((2, (g0, u0, d0)), (6, (g1, u1, d1))):
            for j, part in enumerate((g, u)):
                put(t, (j, slice(None), slice(0, FF_SHARD)), part[...].astype(BF16))
                put(t, (j, slice(None), slice(FF_SHARD, FF_PAD)), jnp.zeros((rows, FF_PAD - FF_SHARD), BF16))
            put(t + 1, (slice(0, FF_SHARD), slice(None)), d[...].astype(BF16))
            put(t + 1, (slice(FF_SHARD, FF_PAD), slice(None)), jnp.zeros((FF_PAD - FF_SHARD, D_MODEL), BF16))

    def tiled(width):
        return pl.BlockSpec((None, rows, width), lambda i, me: (0, i, 0))

    def layer(l, width):
        return pl.BlockSpec((None, rows, width), lambda i, me: (l, i, 0))

    def whole_layer(l):
        return pl.BlockSpec((None, FF_SHARD, D_MODEL), lambda i, me: (l, 0, 0))

    in_specs = [tiled(na_cols), pl.BlockSpec((None, rows_o, D_MODEL), lambda i, me: (0, i, 0)), tiled(dil_cols),
                pl.BlockSpec((None, rows_o, D_MODEL), lambda i, me: (0, i, 0)),
                layer(0, FF_SHARD), layer(0, FF_SHARD), whole_layer(0), layer(1, FF_SHARD), layer(1, FF_SHARD), whole_layer(1)]
    o_shard = pl.BlockSpec((rows_o, D_MODEL), lambda i, me: (i, 0))
    o_land = pl.BlockSpec((None, rows_o, D_MODEL), lambda i, me: (me[0], i, 0))
    gu_shard = pl.BlockSpec((2, rows, FF_PAD), lambda i, me: (0, i, 0))
    gu_land = pl.BlockSpec((None, 2, rows, FF_PAD), lambda i, me: (me[0], 0, i, 0))
    down_shard = pl.BlockSpec((FF_PAD, D_MODEL), lambda i, me: (0, 0))
    down_land = pl.BlockSpec((None, FF_PAD, D_MODEL), lambda i, me: (me[0], 0, 0))

    def qkv_shard(width):
        return pl.BlockSpec((rows, width), lambda i, me: (i, 0))

    def qkv_land(width):
        return pl.BlockSpec((rows, width), lambda i, me: (i, me[0]))

    shard_specs = [qkv_shard(na_cols), o_shard, gu_shard, down_shard, qkv_shard(dil_cols), o_shard, gu_shard, down_shard]
    land_specs = [qkv_land(na_cols), o_land, gu_land, down_land, qkv_land(dil_cols), o_land, gu_land, down_land]
    shard_shapes = [jax.ShapeDtypeStruct(s, BF16) for s in
                    ((D_MODEL, na_cols), (o_rows, D_MODEL), (2, D_MODEL, FF_PAD), (FF_PAD, D_MODEL),
                     (D_MODEL, dil_cols), (o_rows, D_MODEL), (2, D_MODEL, FF_PAD), (FF_PAD, D_MODEL))]
    result = pl.pallas_call(
        body, out_shape=tuple(shard_shapes + list(land_shapes)),
        grid_spec=pltpu.PrefetchScalarGridSpec(num_scalar_prefetch=1, grid=(tiles,), in_specs=in_specs,
                                               out_specs=tuple(shard_specs + land_specs)),
        name="prep_weights", compiler_params=_params())(me, na_qkv, na_o, dil_qkv, dil_o, gate, up, down, gate, up, down)
    return list(result[:8]), list(result[8:])


def _remote_copies(sets, src_refs, land_refs, send_sems, recv_sems, outgoing):
    me = _position()
    copies = []
    for t, (si, src_of, li, dst_of) in enumerate(sets):
        for k in range(1, N_DEV):
            other = _peer(me, k)
            sender = me if outgoing else other
            copies.append(pltpu.make_async_remote_copy(
                src_ref=src_of(src_refs[si], _flat(other)), dst_ref=dst_of(land_refs[li], _flat(sender), k),
                send_sem=send_sems.at[(N_DEV - 1) * t + k - 1], recv_sem=recv_sems.at[(N_DEV - 1) * t + k - 1],
                device_id=other, device_id_type=MESH))
    return copies


def _send_start(name, srcs, lands, sets_by_group):
    n_src, n_land, n_groups = len(srcs), len(lands), len(sets_by_group)

    def body(*refs):
        src_refs, land_refs = refs[:n_src], refs[n_src:n_src + n_land]
        outs = refs[n_src + n_land:]
        for g, sets in enumerate(sets_by_group):
            for cp in _remote_copies(sets, src_refs, land_refs, outs[2 * g], outs[2 * g + 1], True):
                cp.start()
        outs[-1][...] = jnp.zeros_like(outs[-1])

    sem_shapes = []
    for sets in sets_by_group:
        sem_shapes += [pltpu.SemaphoreType.DMA((len(sets) * (N_DEV - 1),))] * 2
    thru = [pltpu.HBM(a.shape, a.dtype) for a in list(srcs) + list(lands)]
    n_sem = len(sem_shapes)
    result = pl.pallas_call(
        body, out_shape=tuple(sem_shapes + thru + [jax.ShapeDtypeStruct((8, 128), F32)]),
        in_specs=[HBM] * (n_src + n_land),
        out_specs=tuple([SEM] * n_sem + [HBM] * (n_src + n_land) + [pl.BlockSpec(memory_space=pltpu.VMEM)]),
        input_output_aliases={i: n_sem + i for i in range(n_src + n_land)},
        compiler_params=pltpu.CompilerParams(has_side_effects=EFFECT), name=name,
    )(*[pltpu.with_memory_space_constraint(a, pltpu.HBM) for a in list(srcs) + list(lands)])
    sems = [(result[2 * g], result[2 * g + 1]) for g in range(n_groups)]
    return sems, list(result[n_sem:n_sem + n_src]), list(result[n_sem + n_src:n_sem + n_src + n_land]), result[-1]


def _send_wait(name, sems, srcs, lands, sets, after):
    n_src, n_land = len(srcs), len(lands)

    def body(*refs):
        src_refs, land_refs = refs[:n_src], refs[n_src:n_src + n_land]
        send_sems, recv_sems = refs[n_src + n_land], refs[n_src + n_land + 1]
        for cp in _remote_copies(sets, src_refs, land_refs, send_sems, recv_sems, True):
            cp.wait_send()
        for cp in _remote_copies(sets, src_refs, land_refs, send_sems, recv_sems, False):
            cp.wait_recv()

    thru = [pltpu.HBM(a.shape, a.dtype) for a in list(srcs) + list(lands)]
    result = pl.pallas_call(
        body, out_shape=tuple(thru), in_specs=[HBM] * (n_src + n_land) + [SEM, SEM] + [ANY] * len(after),
        out_specs=tuple([HBM] * (n_src + n_land)), input_output_aliases={i: i for i in range(n_src + n_land)},
        compiler_params=pltpu.CompilerParams(has_side_effects=EFFECT), name=name,
    )(*srcs, *lands, sems[0], sems[1], *after)
    return list(result[:n_src]), list(result[n_src:])


DIRECT = (1, 2, 4, 6)
PASSED = DIRECT[1:]


def _hbm_passthrough(body, name, arrays, n_sem_in, sem_out_shapes, extra):
    n, n_out = len(arrays), len(sem_out_shapes)
    return pl.pallas_call(
        body, out_shape=tuple(list(sem_out_shapes) + [pltpu.HBM(a.shape, a.dtype) for a in arrays]),
        in_specs=[HBM] * n + [SEM] * n_sem_in + [ANY] * len(extra), out_specs=tuple([SEM] * n_out + [HBM] * n),
        input_output_aliases={i: n_out + i for i in range(n)},
        compiler_params=pltpu.CompilerParams(has_side_effects=EFFECT), name=name)


def _shard_copy(src_ref, land_ref, window, block, to, send_sem, recv_sem, from_shard):
    dst = window(land_ref, _flat(block))
    return pltpu.make_async_remote_copy(src_ref=src_ref if from_shard else dst, dst_ref=dst, send_sem=send_sem,
                                        recv_sem=recv_sem, device_id=to, device_id_type=MESH)


def _gather_start(name, shards, lands, windows, group_sizes):
    n = len(shards)

    def body(*refs):
        shard_refs, land_refs, outs = refs[:n], refs[n:2 * n], refs[2 * n:]
        me = _position()
        t = 0
        for g, size in enumerate(group_sizes):
            for local in range(size):
                for j, k in enumerate(DIRECT):
                    i = len(DIRECT) * local + j
                    _shard_copy(shard_refs[t], land_refs[t], windows[t], me, _peer(me, k), outs[2 * g].at[i],
                                outs[2 * g + 1].at[i], True).start()
                t += 1

    sem_shapes = [pltpu.SemaphoreType.DMA((len(DIRECT) * size,)) for size in group_sizes for _ in range(2)]
    arrays = [pltpu.with_memory_space_constraint(a, pltpu.HBM) for a in list(shards) + list(lands)]
    result = _hbm_passthrough(body, name, arrays, 0, sem_shapes, ())(*arrays)
    n_sem = len(sem_shapes)
    sems = [(result[2 * g], result[2 * g + 1]) for g in range(len(group_sizes))]
    return sems, list(result[n_sem:n_sem + n]), list(result[n_sem + n:])


def _gather_pass_on(name, sems, shards, lands, windows, after):
    n = len(shards)

    def body(*refs):
        shard_refs, land_refs = refs[:n], refs[n:2 * n]
        recv_sems = refs[2 * n + 1]
        pass_send, pass_recv = refs[2 * n + 2 + len(after)], refs[2 * n + 3 + len(after)]
        me = _position()
        sibling = _peer(me, 1)
        for t in range(n):
            for j, k in enumerate(PASSED):
                sender = _peer(me, k)
                arrived = len(DIRECT) * t + 1 + j
                _shard_copy(shard_refs[t], land_refs[t], windows[t], sender, me, refs[2 * n].at[arrived], recv_sems.at[arrived],
                            True).wait_recv()
                i = len(PASSED) * t + j
                _shard_copy(shard_refs[t], land_refs[t], windows[t], sender, sibling, pass_send.at[i], pass_recv.at[i],
                            False).start()

    sem_shapes = [pltpu.SemaphoreType.DMA((len(PASSED) * n,))] * 2
    result = _hbm_passthrough(body, name, list(shards) + list(lands), 2, sem_shapes, after)(
        *shards, *lands, sems[0], sems[1], *after)
    return (result[0], result[1]), list(result[2:2 + n]), list(result[2 + n:])


def _gather_wait(name, sems, pass_sems, shards, lands, windows, after):
    n = len(shards)

    def body(*refs):
        shard_refs, land_refs = refs[:n], refs[n:2 * n]
        send_sems, recv_sems, pass_send, pass_recv = refs[2 * n:2 * n + 4]
        me = _position()
        sibling = _peer(me, 1)
        for t in range(n):
            for j, k in enumerate(DIRECT):
                i = len(DIRECT) * t + j
                _shard_copy(shard_refs[t], land_refs[t], windows[t], me, _peer(me, k), send_sems.at[i], recv_sems.at[i],
                            True).wait_send()
            _shard_copy(shard_refs[t], land_refs[t], windows[t], sibling, me, send_sems.at[len(DIRECT) * t],
                        recv_sems.at[len(DIRECT) * t], True).wait_recv()
            for j, k in enumerate(PASSED):
                i = len(PASSED) * t + j
                _shard_copy(shard_refs[t], land_refs[t], windows[t], _peer(me, k), sibling, pass_send.at[i], pass_recv.at[i],
                            False).wait_send()
                _shard_copy(shard_refs[t], land_refs[t], windows[t], _peer(sibling, k), me, pass_send.at[i], pass_recv.at[i],
                            False).wait_recv()

    result = _hbm_passthrough(body, name, list(shards) + list(lands), 4, [], after)(
        *shards, *lands, sems[0], sems[1], pass_sems[0], pass_sems[1], *after)
    return list(result[n:])


def _all_gather(name, locals_, out_shapes, windows, deps=()):
    n = len(locals_)

    def body(*refs):
        src_refs, out_refs = refs[:n], refs[n + len(deps):2 * n + len(deps)]
        send_sems, recv_sems, local_sems = refs[2 * n + len(deps):]
        x, y, c = _position()
        me, sibling = (x, y, c), (x, y, 1 - c)
        chips = [(1 - x, y), (x, 1 - y), (1 - x, 1 - y)]

        def copy(t, k, block, to, from_local=False):
            dst = windows[t](out_refs[t], _flat(block))
            return pltpu.make_async_remote_copy(
                src_ref=src_refs[t] if from_local else dst, dst_ref=dst, send_sem=send_sems.at[t, k],
                recv_sem=recv_sems.at[t, k], device_id=to, device_id_type=MESH)

        mine = [pltpu.make_async_copy(src_refs[t], windows[t](out_refs[t], _flat(me)), local_sems.at[t]) for t in range(n)]
        sends = []
        for t in range(n):
            mine[t].start()
            sends.append(copy(t, 0, me, sibling, True))
            sends += [copy(t, 1 + j, me, (*chip, c), True) for j, chip in enumerate(chips)]
        for cp in sends:
            cp.start()
        for t in range(n):
            for j, chip in enumerate(chips):
                copy(t, 1 + j, (*chip, c), me).wait_recv()
                passed = copy(t, 4 + j, (*chip, c), sibling)
                passed.start()
                sends.append(passed)
        for t in range(n):
            copy(t, 0, sibling, me).wait_recv()
            for j, chip in enumerate(chips):
                copy(t, 4 + j, (*chip, 1 - c), me).wait_recv()
        for cp in sends:
            cp.wait_send()
        for cp in mine:
            cp.wait()

    return pl.pallas_call(
        body, out_shape=tuple(out_shapes), in_specs=[ANY] * (n + len(deps)), out_specs=tuple([ANY] * n),
        scratch_shapes=[pltpu.SemaphoreType.DMA((n, 7)), pltpu.SemaphoreType.DMA((n, 7)), pltpu.SemaphoreType.DMA((n,))],
        name=name)(*locals_, *deps)


def _adamw(name, me, lands, owns, w, m, v, *, grid, land_specs, own_specs, p_spec):
    n_land = len(lands)

    def body(me_ref, *refs):
        land_refs, own_refs = refs[:n_land], refs[n_land:n_land + len(owns)]
        w_ref, m_ref, v_ref, g_ref, delta_ref, m_out, v_out = refs[n_land + len(owns):]
        ncols = w_ref.shape[-1]
        sums = []
        for i, land_ref in enumerate(land_refs):
            g = own_refs[i][...].astype(F32) if owns else land_ref[0].astype(F32)
            for s in range(0 if owns else 1, land_ref.shape[0]):
                g = g + land_ref[s].astype(F32)
            sums.append(g[:, :ncols])
        g = sums[0] if n_land == 1 else jnp.where(pl.program_id(0) == 0, sums[0], sums[1])
        m_new = ADAM_B1 * m_ref[...] + (1.0 - ADAM_B1) * g
        v_new = ADAM_B2 * v_ref[...] + (1.0 - ADAM_B2) * jnp.square(g)
        m_hat = m_new / (1.0 - ADAM_B1 ** ADAM_STEP)
        v_hat = v_new / (1.0 - ADAM_B2 ** ADAM_STEP)
        g_ref[...] = g
        delta_ref[...] = -ADAM_LR * (m_hat / (jnp.sqrt(v_hat) + ADAM_EPS) + ADAM_WD * w_ref[...])
        m_out[...] = m_new
        v_out[...] = v_new

    shape = jax.ShapeDtypeStruct(w.shape, F32)
    return pl.pallas_call(
        body, out_shape=(shape,) * 4,
        grid_spec=pltpu.PrefetchScalarGridSpec(
            num_scalar_prefetch=1, grid=grid, in_specs=list(land_specs) + list(own_specs) + [p_spec, p_spec, p_spec],
            out_specs=(p_spec,) * 4),
        name=name, compiler_params=_params())(me, *lands, *owns, w, m, v)


def _row(p, layer):
    return p[layer][None, :]


def _square(name, a, b, dims, out_dtype, deps=()):
    if a.shape == (D_MODEL, SEQ):
        return _matmul(name, a, b, grid=(2, 1), a_spec=pl.BlockSpec((512, SEQ), lambda i, k: (i, 0)),
                       b_spec=pl.BlockSpec((SEQ, D_MODEL), lambda i, k: (0, 0)),
                       o_spec=pl.BlockSpec((512, D_MODEL), lambda i, k: (i, 0)),
                       out_shape=jax.ShapeDtypeStruct((D_MODEL, D_MODEL), out_dtype), dims=NN, acc_shape=(8, 128),
                       deps=deps)
    return _matmul(name, a, b, grid=(SEQ // TM_MM, 1), a_spec=pl.BlockSpec((TM_MM, D_MODEL), lambda i, k: (i, 0)),
                   b_spec=pl.BlockSpec((D_MODEL, D_MODEL), lambda i, k: (0, 0)),
                   o_spec=pl.BlockSpec((TM_MM, D_MODEL), lambda i, k: (i, 0)),
                   out_shape=jax.ShapeDtypeStruct((SEQ, D_MODEL), out_dtype), dims=dims, acc_shape=(8, 128), deps=deps)


def _grouped_matmul(name, a_list, b, *, n_tiles, a_block, b_spec, o_spec, out_shape):
    n_groups = len(a_list)

    def a_spec(g):
        def index(j, i):
            mine = j // 3
            return (jnp.where(mine == g, i, jnp.where(mine < g, 0, n_tiles - 1)), 0)
        return pl.BlockSpec(a_block, index)

    def body(*refs):
        b_ref, o_ref = refs[n_groups], refs[n_groups + 1]
        mine = pl.program_id(0) // 3
        for g in range(n_groups):
            @pl.when(mine == g)
            def _(g=g):
                o_ref[...] = jnp.dot(refs[g][...], b_ref[...], preferred_element_type=F32).astype(o_ref.dtype)

    return pl.pallas_call(
        body, out_shape=out_shape, grid=(3 * n_groups, n_tiles), in_specs=[a_spec(g) for g in range(n_groups)] + [b_spec],
        out_specs=o_spec, name=name, compiler_params=_params())(*a_list, b)


def _qkv_fwd(name, hs, w):
    return _grouped_matmul(name, hs, w, n_tiles=SEQ // TM_MM, a_block=(TM_MM, D_MODEL),
                           b_spec=pl.BlockSpec((D_MODEL, D_MODEL), lambda j, i: (0, j)),
                           o_spec=pl.BlockSpec((TM_MM, D_MODEL), lambda j, i: (i, j)),
                           out_shape=jax.ShapeDtypeStruct((SEQ, 3 * len(hs) * D_MODEL), BF16))


def _qkv_dw(name, hs_t, dqkv):
    return _grouped_matmul(name, hs_t, dqkv, n_tiles=2, a_block=(512, SEQ),
                           b_spec=pl.BlockSpec((None, SEQ, D_MODEL), lambda j, i: (j, 0, 0)),
                           o_spec=pl.BlockSpec((512, D_MODEL), lambda j, i: (i, j)),
                           out_shape=jax.ShapeDtypeStruct((D_MODEL, 3 * len(hs_t) * D_MODEL), BF16))


def _proj_do_sorted(name, d_a, w_o):
    def body(da_ref, w_ref, *refs):
        value = lax.dot_general(da_ref[...], w_ref[...], NT, preferred_element_type=F32)
        refs[0][...] = value
        for j, dil in enumerate(SORTED):
            _sort_tile(refs[-1], value, dil, refs[1 + j])

    tile = pl.BlockSpec((TM, D_MODEL), lambda i: (i, 0))
    shapes = [jax.ShapeDtypeStruct((SEQ, D_MODEL), F32)] + [jax.ShapeDtypeStruct((dil, SEQ // dil, D_MODEL), F32) for dil in SORTED]
    result = pl.pallas_call(
        body, out_shape=tuple(shapes), grid=(SEQ // TM,),
        in_specs=[tile, pl.BlockSpec((D_MODEL, D_MODEL), lambda i: (0, 0))],
        out_specs=tuple([tile] + [_sorted_spec(dil) for dil in SORTED]), scratch_shapes=[_sort_scratch()],
        name=name, compiler_params=_params())(d_a, w_o)
    return [t.reshape(SEQ, D_MODEL) for t in result]


def _qkv_dh(name, dqkv, w, n_chunks, deps):
    return _matmul(name, dqkv, w, grid=(n_chunks // 3, SEQ // TM, 1),
                   a_spec=pl.BlockSpec((3, TM, D_MODEL), lambda g, i, k: (g, i, 0)),
                   b_spec=pl.BlockSpec((D_MODEL, 3 * D_MODEL), lambda g, i, k: (0, g)),
                   o_spec=pl.BlockSpec((None, TM, D_MODEL), lambda g, i, k: (g, i, 0)),
                   out_shape=jax.ShapeDtypeStruct((n_chunks // 3, SEQ, D_MODEL), F32), dims=NT, acc_shape=(8, 128),
                   deps=deps, inner=3)


def _local_step(x, target, norms, rpb, fetch, emit, deps):
    mix_pre, mix_post, ffn_pre, ffn_post = norms
    slopes = 2.0 ** (-8.0 * jnp.arange(1, N_HEADS + 1, dtype=F32) / N_HEADS)
    rpb_pad = jnp.pad(rpb, ((0, 0), (0, 1), (0, 128 - 31)))
    saved = []

    for layer in range(2):
        tag = f"l{layer}"
        if layer == 0:
            h = _rms_fwd(tag + "_norm_mix", x, _row(mix_pre, layer), out_dtype=BF16, deps=deps)
            hs = [h]
            table = _rpb_table(rpb_pad)
            w_qkv, w_o = fetch("na", [table, h], [h])
            qkv = _qkv_fwd(tag + "_qkv", hs, w_qkv)
            o, lse = _na_fwd(qkv, table)
            mixer = (hs, qkv, o, lse, table)
        else:
            hs = [t.reshape(SEQ, D_MODEL) for t in
                  _rms_fwd(tag + "_norm_mix", x, _row(mix_pre, layer), out_dtype=BF16, sorted_too=True)]
            w_qkv, w_o = fetch("dil", [x], [hs[0]])
            qkv = _qkv_fwd(tag + "_qkv", hs, w_qkv)
            penalties = [_dil_penalties(g, slopes * dil) for g, (_, dil) in enumerate(DIL_GROUPS)]
            outs, lses = zip(*[_dil_fwd(g, qkv, penalties[g]) for g in range(len(DIL_GROUPS))])
            merged, lse_total = _dil_merge(outs, lses)
            o = merged[0]
            mixer = (hs, qkv, merged, lses, lse_total, penalties)
        a = _square(tag + "_proj", o, w_o, NN, F32)
        x1 = _rms_fwd(tag + "_post_mix", a, _row(mix_post, layer), res=x)
        h2 = _rms_fwd(tag + "_norm_ffn", x1, _row(ffn_pre, layer), out_dtype=BF16)
        w_gu, w_down = fetch(f"ffn{layer}", [a], [h2])
        act, hg, hu, act_t = _ffn_up(tag + "_ffn_up", h2, w_gu)
        f = _matmul(tag + "_ffn_down", act, w_down, grid=(SEQ // TM_MM, N_DEV // 4),
                    a_spec=pl.BlockSpec((4, TM_MM, FF_PAD), lambda i, k: (k, i, 0)),
                    b_spec=pl.BlockSpec((4, FF_PAD, D_MODEL), lambda i, k: (k, 0, 0)),
                    o_spec=pl.BlockSpec((TM_MM, D_MODEL), lambda i, k: (i, 0)),
                    out_shape=jax.ShapeDtypeStruct((SEQ, D_MODEL), F32), dims=NN, acc_shape=(TM_MM, D_MODEL), inner=4)
        x2 = _rms_fwd(tag + "_post_ffn", f, _row(ffn_post, layer), res=x1)
        transposed = ([t.T for t in hs], o.astype(BF16).T, h2.T, act_t)
        saved.append((x, mixer, a, x1, transposed, hg, hu, f, w_qkv, w_o, w_gu, w_down))
        x = x2

    dx, loss = _loss_head("loss_head", x, target)
    d_norm = {k: [None, None] for k in ("mix_pre", "mix_post", "ffn_pre", "ffn_post")}
    d_rpb = None

    for layer in (1, 0):
        tag = f"b{layer}"
        x0, mixer, a, x1, (h_t, o_t, h2_t, act_t), hg, hu, f, w_qkv, w_o, w_gu, w_down = saved[layer]
        d_f, d_norm["ffn_post"][layer] = _rms_bwd(tag + "_post_ffn", f, _row(ffn_post, layer), [dx], out_dtype=BF16)
        dgu = _ffn_bwd_act(tag + "_ffn_act", d_f, w_down, hg, hu)
        d_down = _matmul(
            tag + "_ffn_ddown", act_t, d_f, grid=(N_DEV, 1),
            a_spec=pl.BlockSpec((None, FF_PAD, SEQ), lambda d, k: (d, 0, 0)),
            b_spec=pl.BlockSpec((SEQ, D_MODEL), lambda d, k: (0, 0)),
            o_spec=pl.BlockSpec((None, FF_PAD, D_MODEL), lambda d, k: (d, 0, 0)),
            out_shape=jax.ShapeDtypeStruct((N_DEV, FF_PAD, D_MODEL), BF16), dims=NN, acc_shape=(8, 128))
        d_gu = _matmul(
            tag + "_ffn_dgu", h2_t, dgu, grid=(2, N_DEV, 1),
            a_spec=pl.BlockSpec((D_MODEL, SEQ), lambda t, d, k: (0, 0)),
            b_spec=pl.BlockSpec((None, None, SEQ, FF_PAD), lambda t, d, k: (t, d, 0, 0)),
            o_spec=pl.BlockSpec((None, None, D_MODEL, FF_PAD), lambda t, d, k: (d, t, 0, 0)),
            out_shape=jax.ShapeDtypeStruct((N_DEV, 2, D_MODEL, FF_PAD), BF16), dims=NN, acc_shape=(8, 128))
        sent = emit(f"ffn{layer}", [d_gu, d_down])
        d_h2 = _matmul(
            tag + "_ffn_dh", dgu, w_gu, grid=(SEQ // TM_MM, 4),
            a_spec=pl.BlockSpec((None, 4, TM_MM, FF_PAD), lambda i, k: (k // 2, k % 2, i, 0)),
            b_spec=pl.BlockSpec((4, None, D_MODEL, FF_PAD), lambda i, k: (k % 2, k // 2, 0, 0)),
            o_spec=pl.BlockSpec((TM_MM, D_MODEL), lambda i, k: (i, 0)),
            out_shape=jax.ShapeDtypeStruct((SEQ, D_MODEL), F32), dims=NT, acc_shape=(TM_MM, D_MODEL), deps=sent, inner=4)
        dx1, d_norm["ffn_pre"][layer] = _rms_bwd(tag + "_norm_ffn", x1, _row(ffn_pre, layer), [d_h2], res=dx)
        d_a, d_norm["mix_post"][layer] = _rms_bwd(tag + "_post_mix", a, _row(mix_post, layer), [dx1], out_dtype=BF16)
        d_wo = _square(tag + "_proj_dw", o_t, d_a, NN, BF16)
        if layer == 0:
            _, qkv, o, lse, table = mixer
            d_o = _square(tag + "_proj_do", d_a, w_o, NT, BF16)
            dqkv, gp = _na_bwd(qkv, table, d_o, lse)
            d_rpb = _rpb_grad(gp)[:, :15, :31]
            sent = emit("na", [_qkv_dw(tag + "_qkv_dw", h_t, dqkv), d_wo])
            d_h = _qkv_dh(tag + "_qkv_dh", dqkv, w_qkv, 3, sent)
            dx, d_norm["mix_pre"][layer] = _rms_bwd(tag + "_norm_mix", x0, _row(mix_pre, layer), [d_h[0]], res=dx1)
        else:
            _, qkv, merged, lses, lse_total, penalties = mixer
            d_o = _proj_do_sorted(tag + "_proj_do", d_a, w_o)
            dqkv = lax.empty((3 * len(DIL_GROUPS), SEQ, D_MODEL), BF16)
            for g, (_, dil) in enumerate(DIL_GROUPS):
                dqkv = _dil_bwd(g, qkv, penalties[g], d_o[g], merged[g], lses[g], lse_total[g], dqkv)
            sent = emit("dil", [_qkv_dw(tag + "_qkv_dw", h_t, dqkv), d_wo])
            d_h = _qkv_dh(tag + "_qkv_dh", dqkv, w_qkv, 9, sent)
            dx, d_norm["mix_pre"][layer] = _rms_bwd(tag + "_norm_mix", x0, _row(mix_pre, layer), None, res=dx1, groups=d_h)

    d_gains = [jnp.concatenate(d_norm[k], axis=0) for k in ("mix_pre", "mix_post", "ffn_pre", "ffn_post")]
    return loss, dx, d_gains, d_rpb


RPB_SIZE = N_HEADS * 15 * 31


def _pack_small(gains, rpb, last=None):
    top = jnp.concatenate(gains, axis=0).reshape(64, 128)
    bottom = jnp.pad(rpb.reshape(-1), (0, 64 * 128 - RPB_SIZE))
    if last is not None:
        bottom = bottom + jnp.pad(last.reshape(1), (64 * 128 - 1, 0))
    return jnp.concatenate([top, bottom.reshape(64, 128)], axis=0)


def _unpack_small(p):
    gains = p[:64].reshape(4, 2, D_MODEL)
    rpb = p[64:].reshape(-1)[:RPB_SIZE].reshape(1, N_HEADS, 15, 31)
    return [gains[i] for i in range(4)], rpb


GROUPS = ("na", "ffn0", "dil", "ffn1")


def kernel(x, norm_mix_pre, norm_mix_post, norm_ffn_pre, norm_ffn_post, na_w_qkv, na_w_o, na_rpb, dil_w_qkv, dil_w_o, ffn_w_gate, ffn_w_up, ffn_w_down, loss_target, m_norm_mix_pre, m_norm_mix_post, m_norm_ffn_pre, m_norm_ffn_post, m_na_w_qkv, m_na_w_o, m_na_rpb, m_dil_w_qkv, m_dil_w_o, m_ffn_w_gate, m_ffn_w_up, m_ffn_w_down, v_norm_mix_pre, v_norm_mix_post, v_norm_ffn_pre, v_norm_ffn_post, v_na_w_qkv, v_na_w_o, v_na_rpb, v_dil_w_qkv, v_dil_w_o, v_ffn_w_gate, v_ffn_w_up, v_ffn_w_down):
    na_cols, dil_cols, o_rows = 3 * D_MODEL // N_DEV, 9 * D_MODEL // N_DEV, D_MODEL // N_DEV
    ff_pad = FF_PAD - FF_SHARD
    me = (4 * lax.axis_index("x") + 2 * lax.axis_index("y") + lax.axis_index("c")).astype(jnp.int32).reshape(1)

    full = {
        "na": [((D_MODEL, 3 * D_MODEL), _columns(na_cols)), ((N_DEV, o_rows, D_MODEL), _leading)],
        "dil": [((D_MODEL, 9 * D_MODEL), _columns(dil_cols)), ((N_DEV, o_rows, D_MODEL), _leading)],
        "ffn0": [((N_DEV, 2, D_MODEL, FF_PAD), _leading), ((N_DEV, FF_PAD, D_MODEL), _leading)],
        "ffn1": [((N_DEV, 2, D_MODEL, FF_PAD), _leading), ((N_DEV, FF_PAD, D_MODEL), _leading)],
    }
    block = {
        "na": [(D_MODEL, na_cols), (o_rows, D_MODEL)], "dil": [(D_MODEL, dil_cols), (o_rows, D_MODEL)],
        "ffn0": [(2, D_MODEL, FF_PAD), (FF_PAD, D_MODEL)], "ffn1": [(2, D_MODEL, FF_PAD), (FF_PAD, D_MODEL)],
    }

    land_shapes = [jax.ShapeDtypeStruct(full[g][t][0], BF16) for g in GROUPS for t in range(2)]
    windows = [full[g][t][1] for g in GROUPS for t in range(2)]
    shards, lands = _prep_weights(me, na_w_qkv, na_w_o, dil_w_qkv, dil_w_o, ffn_w_gate, ffn_w_up, ffn_w_down, land_shapes)
    sems, shards, lands = _gather_start("gather_start", shards, lands, windows, [2] * len(GROUPS))

    def fetch(group, early, late):
        gi = GROUPS.index(group)
        mine = slice(2 * gi, 2 * gi + 2)
        pass_sems, shards_g, lands_g = _gather_pass_on(f"gather_pass_{group}", sems[gi], shards[mine], lands[mine],
                                                       windows[mine], early)
        qkv, o = _gather_wait(f"gather_wait_{group}", sems[gi], pass_sems, shards_g, lands_g, windows[mine], late)
        return (qkv, o.reshape(D_MODEL, D_MODEL)) if group in ("na", "dil") else (qkv, o)

    def grad_source(group, t):
        return _columns(block[group][0][1]) if (group in ("na", "dil") and t == 0) else _leading

    in_flight = {}

    def emit(group, grads):
        if group in ("na", "dil"):
            grads = [grads[0], grads[1].reshape(N_DEV, o_rows, D_MODEL)]
        sets = [(t, grad_source(group, t), t, _by_distance) for t in range(2)]
        landing = [lax.empty((N_DEV - 1,) + block[group][t], BF16) for t in range(2)]
        sems_g, grads, landing, tok = _send_start(f"exchange_start_{group}", grads, landing, [sets])
        in_flight[group] = (sems_g[0], grads, landing, sets)
        return [tok]

    norms = (norm_mix_pre, norm_mix_post, norm_ffn_pre, norm_ffn_post)
    loss, grad_x, d_gains, d_rpb = _local_step(x[0], loss_target[0], norms, na_rpb[0], fetch, emit, [shards[0]])

    landed, sent = {}, {}

    def wait_for(group, after):
        sems_g, grads, landing, sets = in_flight[group]
        sent[group], landed[group] = _send_wait(f"exchange_wait_{group}", sems_g, grads, landing, sets, after)

    for group in ("ffn1", "dil", "ffn0"):
        wait_for(group, [grad_x])

    def one(rows, tile, ncols, columns):
        own = (pl.BlockSpec((tile, ncols), lambda i, me: (i, me[0])) if columns
               else pl.BlockSpec((None, tile, ncols), lambda i, me: (me[0], i, 0)))
        return dict(grid=(rows // tile,), land_specs=[pl.BlockSpec((N_DEV - 1, tile, ncols), lambda i, me: (0, i, 0))],
                    own_specs=[own], p_spec=pl.BlockSpec((None, tile, ncols), lambda i, me: (0, i, 0)))

    def layered(block_shape, index, p_block, n_tiles):
        def specs(lead_size, lead):
            shape = (lead_size,) + block_shape
            return [pl.BlockSpec(shape, lambda l, r, me: index(lead(me), jnp.where(l == 0, r, n_tiles - 1))),
                    pl.BlockSpec(shape, lambda l, r, me: index(lead(me), jnp.where(l == 0, 0, r)))]
        return dict(grid=(2, n_tiles), land_specs=specs(N_DEV - 1, lambda me: 0), own_specs=specs(None, lambda me: me[0]),
                    p_spec=pl.BlockSpec(p_block, lambda l, r, me: (l, r, 0)))

    gu_lands, gu_owns = [landed["ffn0"][0], landed["ffn1"][0]], [sent["ffn0"][0], sent["ffn1"][0]]
    down_lands, down_owns = [landed["ffn0"][1], landed["ffn1"][1]], [sent["ffn0"][1], sent["ffn1"][1]]
    updates = {
        "dil_w_qkv": _adamw("adamw_dil_qkv", me, [landed["dil"][0]], [sent["dil"][0]], dil_w_qkv, m_dil_w_qkv, v_dil_w_qkv,
                            **one(D_MODEL, 128, dil_cols, True)),
        "dil_w_o": _adamw("adamw_dil_o", me, [landed["dil"][1]], [sent["dil"][1]], dil_w_o, m_dil_w_o, v_dil_w_o,
                          **one(o_rows, o_rows, D_MODEL, False)),
        "ffn_w_gate": _adamw("adamw_gate", me, gu_lands, gu_owns, ffn_w_gate, m_ffn_w_gate, v_ffn_w_gate,
                             **layered((None, 128, FF_PAD), lambda lead, r: (lead, 0, r, 0), (None, 128, FF_SHARD), 8)),
        "ffn_w_up": _adamw("adamw_up", me, gu_lands, gu_owns, ffn_w_up, m_ffn_w_up, v_ffn_w_up,
                           **layered((None, 128, FF_PAD), lambda lead, r: (lead, 1, r, 0), (None, 128, FF_SHARD), 8)),
        "ffn_w_down": _adamw("adamw_down", me, down_lands, down_owns, ffn_w_down, m_ffn_w_down, v_ffn_w_down,
                             **layered((176, D_MODEL), lambda lead, r: (lead, r, 0), (None, 176, D_MODEL), 2)),
    }
    done = [u[0] for u in updates.values()]
    small = _all_gather("gather_small", [_pack_small(d_gains, d_rpb, loss)], [jax.ShapeDtypeStruct((N_DEV, 128, 128), F32)],
                        [_leading], deps=done)[0]
    wait_for("na", [small])
    updates["na_w_qkv"] = _adamw("adamw_na_qkv", me, [landed["na"][0]], [sent["na"][0]], na_w_qkv, m_na_w_qkv, v_na_w_qkv,
                                 **one(D_MODEL, 256, na_cols, True))
    updates["na_w_o"] = _adamw("adamw_na_o", me, [landed["na"][1]], [sent["na"][1]], na_w_o, m_na_w_o, v_na_w_o,
                               **one(o_rows, o_rows, D_MODEL, False))
    gains = [norm_mix_pre, norm_mix_post, norm_ffn_pre, norm_ffn_post]
    m_gains = [m_norm_mix_pre, m_norm_mix_post, m_norm_ffn_pre, m_norm_ffn_post]
    v_gains = [v_norm_mix_pre, v_norm_mix_post, v_norm_ffn_pre, v_norm_ffn_post]
    packed = _adamw("adamw_small", me, [small], (), _pack_small(gains, na_rpb)[None], _pack_small(m_gains, m_na_rpb)[None],
                    _pack_small(v_gains, v_na_rpb)[None], grid=(1,),
                    land_specs=[pl.BlockSpec((N_DEV, 128, 128), lambda i, me: (0, 0, 0))], own_specs=[],
                    p_spec=pl.BlockSpec((None, 128, 128), lambda i, me: (0, 0, 0)))
    small_out = [_unpack_small(p[0]) for p in packed]

    order = ["na_w_qkv", "na_w_o", "na_rpb", "dil_w_qkv", "dil_w_o", "ffn_w_gate", "ffn_w_up", "ffn_w_down"]
    result = [packed[0][0, 127, 127], grad_x[None]]
    for kind in range(4):
        gains_k, rpb_k = small_out[kind]
        result += gains_k
        result += [rpb_k if name == "na_rpb" else updates[name][kind] for name in order]
    return tuple(result)
```

```python
import functools

import jax
import jax.numpy as jnp
from jax import lax
from jax.experimental import pallas as pl
from jax.experimental.pallas import tpu as pltpu

F32 = jnp.float32
BF16 = jnp.bfloat16
MESH = pl.DeviceIdType.MESH
ANY = pl.BlockSpec(memory_space=pl.ANY)
HBM = pl.BlockSpec(memory_space=pltpu.HBM)
SEM = pl.BlockSpec(memory_space=pltpu.SEMAPHORE)
EFFECT = pltpu.SideEffectType.DATAFLOW_SIDE_EFFECTING

N_DEV = 8
SEQ = 2048
D_MODEL = 1024
N_HEADS = 16
HEAD_DIM = 64
GRID_W = 64
NA_ROWS = 8
SEQ_ROWS = SEQ // GRID_W
DIL_GROUPS = ((128, 1), (512, 4), (2048, 16))
BAND = 128
RADIUS = 64
FF_SHARD = 352
FF_PAD = 384
RMS_EPS = 1e-6
NEG_INF = -1e30
Q_SCALE = HEAD_DIM ** -0.5

ADAM_LR = 0.001
ADAM_B1 = 0.9
ADAM_B2 = 0.999
ADAM_EPS = 1e-08
ADAM_WD = 0.01
ADAM_STEP = 10

VMEM_LIMIT = 56 * 1024 * 1024
TM = 512
TM_MM = 1024

NN = (((1,), (0,)), ((), ()))
NT = (((1,), (1,)), ((), ()))
TN = (((0,), (0,)), ((), ()))


def _params():
    return pltpu.CompilerParams(vmem_limit_bytes=VMEM_LIMIT)


def _matmul(name, a, b, *, grid, a_spec, b_spec, o_spec, out_shape, dims, acc_shape, deps=(), inner=1):
    nk = grid[-1]
    kaxis = len(grid) - 1

    def body(a_ref, b_ref, *rest):
        o_ref, acc_ref = rest[-2], rest[-1]
        if inner == 1:
            part = lax.dot_general(a_ref[...].astype(BF16), b_ref[...].astype(BF16), dims, preferred_element_type=F32)
        elif len(b_ref.shape) == 2:
            a_all = jnp.concatenate([a_ref[j].astype(BF16) for j in range(inner)], axis=1)
            part = lax.dot_general(a_all, b_ref[...].astype(BF16), dims, preferred_element_type=F32)
        else:
            part = sum(lax.dot_general(a_ref[j].astype(BF16), b_ref[j].astype(BF16), dims, preferred_element_type=F32)
                       for j in range(inner))
        if nk == 1:
            o_ref[...] = part.astype(o_ref.dtype)
        else:
            k = pl.program_id(kaxis)

            @pl.when(k == 0)
            def _():
                acc_ref[...] = part

            @pl.when(k > 0)
            def _():
                acc_ref[...] += part

            @pl.when(k == nk - 1)
            def _():
                o_ref[...] = acc_ref[...].astype(o_ref.dtype)

    return pl.pallas_call(
        body, out_shape=out_shape, grid=grid, in_specs=[a_spec, b_spec] + [ANY] * len(deps), out_specs=o_spec,
        scratch_shapes=[pltpu.VMEM(acc_shape, F32)], name=name, compiler_params=_params())(a, b, *deps)


SORTED = tuple(d for _, d in DIL_GROUPS if d > 1)
LANE_CHUNKS = D_MODEL // 128


def _sort_scratch(tm=TM):
    return pltpu.VMEM((LANE_CHUNKS, tm, 128), F32)


def _sorted_view(t, dil):
    return t.reshape(dil, SEQ // dil, D_MODEL)


def _sorted_spec(dil, lead=(), tm=TM):
    return pl.BlockSpec((None,) * len(lead) + (dil, tm // dil, D_MODEL), lambda i: tuple(lead) + (0, i, 0))


def _sort_tile(scratch, value, dil, out_ref):
    tm = value.shape[0]
    for c in range(LANE_CHUNKS):
        scratch[c] = value[:, 128 * c:128 * (c + 1)]
    for r in range(dil):
        rows = [scratch.at[c][pl.ds(r, tm // dil, stride=dil), :] for c in range(LANE_CHUNKS)]
        out_ref[r] = jnp.concatenate(rows, axis=1).astype(out_ref.dtype)


def _unsort_tile(scratch, in_ref, dil):
    for r in range(dil):
        value = in_ref[r].astype(F32)
        for c in range(LANE_CHUNKS):
            scratch.at[c][pl.ds(r, value.shape[0], stride=dil), :] = value[:, 128 * c:128 * (c + 1)]
    return jnp.concatenate([scratch[c] for c in range(LANE_CHUNKS)], axis=1)


def _rms_fwd(name, x, g, res=None, out_dtype=F32, deps=(), sorted_too=False):
    n_tiles = SEQ // TM
    has_res = res is not None
    n_in = 2 + has_res + len(deps)

    def body(*refs):
        x_ref, g_ref = refs[0], refs[1]
        xv = x_ref[...]
        r = lax.rsqrt(jnp.mean(xv * xv, axis=-1, keepdims=True) + RMS_EPS)
        y = xv * r * g_ref[...]
        if has_res:
            y = refs[2][...] + y
        refs[n_in][...] = y.astype(out_dtype)
        if sorted_too:
            for j, dil in enumerate(SORTED):
                _sort_tile(refs[-1], y, dil, refs[n_in + 1 + j])

    tile = pl.BlockSpec((TM, D_MODEL), lambda i: (i, 0))
    gspec = pl.BlockSpec((1, D_MODEL), lambda i: (0, 0))
    ins = [x, g] + ([res] if has_res else []) + list(deps)
    specs = [tile, gspec] + ([tile] if has_res else []) + [ANY] * len(deps)
    shapes, out_specs = [jax.ShapeDtypeStruct((SEQ, D_MODEL), out_dtype)], [tile]
    if sorted_too:
        shapes += [jax.ShapeDtypeStruct((dil, SEQ // dil, D_MODEL), out_dtype) for dil in SORTED]
        out_specs += [_sorted_spec(dil) for dil in SORTED]
    result = pl.pallas_call(
        body, out_shape=tuple(shapes), grid=(n_tiles,), in_specs=specs, out_specs=tuple(out_specs),
        scratch_shapes=[_sort_scratch()] if sorted_too else [], name=name, compiler_params=_params())(*ins)
    return result if sorted_too else result[0]


def _rms_bwd(name, x, g, dys, res=None, out_dtype=F32, groups=None, deps=()):
    n_tiles = SEQ // TM
    n_dy = len(dys) if groups is None else 1 + len(SORTED)
    has_res = res is not None

    def body(*refs):
        x_ref, g_ref = refs[0], refs[1]
        dy_refs = refs[2:2 + n_dy]
        res_ref = refs[2 + n_dy] if has_res else None
        first_out = 2 + n_dy + has_res + len(deps)
        dx_ref, dg_ref, acc_ref = refs[first_out:first_out + 3]
        i = pl.program_id(0)
        xv = x_ref[...]
        r = lax.rsqrt(jnp.mean(xv * xv, axis=-1, keepdims=True) + RMS_EPS)
        xn = xv * r
        dy = dy_refs[0][...].astype(F32)
        for j, extra in enumerate(dy_refs[1:]):
            dy = dy + (extra[...].astype(F32) if groups is None else _unsort_tile(refs[-1], extra, SORTED[j]))
        dyg = dy * g_ref[...]
        dx = r * (dyg - xn * jnp.mean(dyg * xn, axis=-1, keepdims=True))
        if has_res:
            dx = res_ref[...] + dx
        dx_ref[...] = dx.astype(dx_ref.dtype)
        part = jnp.sum((dy * xn).reshape(TM // 8, 8, D_MODEL), axis=0)

        @pl.when(i == 0)
        def _():
            acc_ref[...] = part

        @pl.when(i > 0)
        def _():
            acc_ref[...] += part

        @pl.when(i == n_tiles - 1)
        def _():
            dg_ref[...] = jnp.broadcast_to(jnp.sum(acc_ref[...], axis=0, keepdims=True), (8, D_MODEL))

    tile = pl.BlockSpec((TM, D_MODEL), lambda i: (i, 0))
    gspec = pl.BlockSpec((1, D_MODEL), lambda i: (0, 0))
    if groups is None:
        dy_ins, dy_specs = list(dys), [tile] * n_dy
    else:
        dy_ins = [groups] + [groups.reshape(n_dy, dil, SEQ // dil, D_MODEL) for dil in SORTED]
        dy_specs = [pl.BlockSpec((None, TM, D_MODEL), lambda i: (0, i, 0))]
        dy_specs += [_sorted_spec(dil, lead=(1 + j,)) for j, dil in enumerate(SORTED)]
    ins = [x, g] + dy_ins + ([res] if has_res else []) + list(deps)
    specs = [tile, gspec] + dy_specs + ([tile] if has_res else []) + [ANY] * len(deps)
    dx, dg = pl.pallas_call(
        body, out_shape=(jax.ShapeDtypeStruct((SEQ, D_MODEL), out_dtype), jax.ShapeDtypeStruct((8, D_MODEL), F32)),
        grid=(n_tiles,), in_specs=specs,
        out_specs=(tile, pl.BlockSpec((8, D_MODEL), lambda i: (0, 0))),
        scratch_shapes=[pltpu.VMEM((8, D_MODEL), F32)] + ([_sort_scratch()] if groups is not None else []),
        name=name, compiler_params=_params())(*ins)
    return dx, dg[0:1]


def _loss_head(name, y, target):
    n_tiles = SEQ // TM

    def body(y_ref, t_ref, dy_ref, loss_ref, acc_ref):
        i = pl.program_id(0)
        diff = y_ref[...] - t_ref[...]
        dy_ref[...] = diff * (1.0 / D_MODEL)
        part = jnp.sum((diff * diff).reshape(TM // 8, 8, D_MODEL), axis=0)

        @pl.when(i == 0)
        def _():
            acc_ref[...] = part

        @pl.when(i > 0)
        def _():
            acc_ref[...] += part

        @pl.when(i == n_tiles - 1)
        def _():
            loss_ref[...] = jnp.full((8, 128), jnp.sum(acc_ref[...]) * (0.5 / D_MODEL), F32)

    tile = pl.BlockSpec((TM, D_MODEL), lambda i: (i, 0))
    dy, loss = pl.pallas_call(
        body, out_shape=(jax.ShapeDtypeStruct((SEQ, D_MODEL), F32), jax.ShapeDtypeStruct((8, 128), F32)),
        grid=(n_tiles,), in_specs=[tile, tile], out_specs=(tile, pl.BlockSpec((8, 128), lambda i: (0, 0))),
        scratch_shapes=[pltpu.VMEM((8, D_MODEL), F32)], name=name, compiler_params=_params())(y, target)
    return dy, loss[0, 0]


def _row_index(shape):
    return lax.broadcasted_iota(jnp.int32, shape, 0)


def _lane_index(shape):
    return lax.broadcasted_iota(jnp.int32, shape, len(shape) - 1)


def _skew_rows(t, direction):
    q = _row_index(t.shape) & (GRID_W - 1)
    for bit in range(6):
        step = 1 << bit
        shift = step if direction > 0 else 128 - step
        t = jnp.where((q & step) != 0, pltpu.roll(t, shift, 1), t)
    return t


def _rpb_table(rpb_pad):
    rows = 16 * GRID_W

    def body(r_ref, t_ref):
        lane = _lane_index((rows, 128))
        v = pltpu.roll(r_ref[...], 128 - 15, 1)
        t = _skew_rows(jnp.broadcast_to(v[:, None, :], (16, GRID_W, 128)).reshape(rows, 128), +1)
        t = jnp.where(lane < GRID_W, t, 0.0)
        below = jnp.concatenate([t[GRID_W:], jnp.zeros((GRID_W, 128), F32)], axis=0)
        first_col = jnp.clip((_row_index((rows, 128)) & (GRID_W - 1)) - 8, 0, GRID_W - 16)
        key_col = lane & (GRID_W - 1)
        in_window = (key_col >= first_col) & (key_col < first_col + 16)
        t_ref[...] = jnp.where(in_window, t + pltpu.roll(below, GRID_W, 1), NEG_INF).reshape(16, GRID_W, 128)

    return pl.pallas_call(
        body, out_shape=jax.ShapeDtypeStruct((N_HEADS, 16, GRID_W, 128), F32), grid=(N_HEADS,),
        in_specs=[pl.BlockSpec((None, 16, 128), lambda h: (h, 0, 0))],
        out_specs=pl.BlockSpec((None, 16, GRID_W, 128), lambda h: (h, 0, 0, 0)),
        name="rpb_table", compiler_params=_params())(rpb_pad)


def _rpb_grad(gp):
    rows = 16 * GRID_W

    def body(g_ref, o_ref):
        lane = _lane_index((rows, 128))
        g = g_ref[...].reshape(rows, 128)
        low = jnp.where(lane < GRID_W, g, 0.0)
        high = pltpu.roll(jnp.where(lane >= GRID_W, g, 0.0), GRID_W, 1)
        above = jnp.concatenate([jnp.zeros((GRID_W, 128), F32), high[:rows - GRID_W]], axis=0)
        diag = jnp.sum(_skew_rows(low + above, -1).reshape(16, GRID_W, 128), axis=1)
        o_ref[...] = pltpu.roll(diag, 15, 1)

    return pl.pallas_call(
        body, out_shape=jax.ShapeDtypeStruct((N_HEADS, 16, 128), F32), grid=(N_HEADS,),
        in_specs=[pl.BlockSpec((None, 16, GRID_W, 128), lambda h: (h, 0, 0, 0))],
        out_specs=pl.BlockSpec((None, 16, 128), lambda h: (h, 0, 0)),
        name="rpb_grad", compiler_params=_params())(gp)


NA_KEYS = NA_ROWS * GRID_W


def _na_window(i):
    first_row = jnp.clip(i - NA_ROWS // 2, 0, SEQ_ROWS - NA_ROWS)
    return pl.multiple_of(first_row * GRID_W, GRID_W), first_row - i + NA_ROWS - 1


NA_STEP = 8


def _head_masks():
    lane = _lane_index((1, 128))
    return (lane < HEAD_DIM, lane >= HEAD_DIM)


def _stack_heads(t, masks):
    zero = jnp.zeros_like(t)
    return jnp.concatenate([jnp.where(masks[0], t, zero), jnp.where(masks[1], t, zero)], axis=0)


def _unstack_heads(t, masks):
    n = t.shape[0] // 2
    return jnp.where(masks[0], t[:n], t[n:])


def _stack_columns(t):
    return jnp.concatenate([t[:, 0:1], t[:, HEAD_DIM:HEAD_DIM + 1]], axis=0)


def _na_scores(qs, kw, tp_ref, dr0):
    s = lax.dot_general(qs, kw, NT, preferred_element_type=F32)
    bias = jnp.concatenate(
        [jnp.concatenate([tp_ref[a, pl.ds(dr0 + 2 * c, 1), :, :].reshape(GRID_W, 128) for c in range(4)], axis=1)
         for a in range(2)], axis=0)
    return s + bias


def _na_specs():
    q_spec = pl.BlockSpec((NA_STEP * GRID_W, 128), lambda hp, i: (i, hp))
    k_spec = pl.BlockSpec((SEQ, 128), lambda hp, i: (0, 8 + hp))
    v_spec = pl.BlockSpec((SEQ, 128), lambda hp, i: (0, 16 + hp))
    tp_spec = pl.BlockSpec((2, 16, GRID_W, 128), lambda hp, i: (hp, 0, 0, 0))
    return q_spec, k_spec, v_spec, tp_spec


def _na_fwd(qkv, table):
    def body(q_ref, k_ref, v_ref, tp_ref, o_ref, lse_ref):
        masks = _head_masks()
        for r in range(NA_STEP):
            rows = slice(r * GRID_W, (r + 1) * GRID_W)
            start, dr0 = _na_window(pl.program_id(1) * NA_STEP + r)
            kw = k_ref[pl.ds(start, NA_KEYS), :]
            vw = v_ref[pl.ds(start, NA_KEYS), :]
            s = _na_scores(_stack_heads(q_ref[rows, :] * Q_SCALE, masks), kw, tp_ref, dr0)
            m = jnp.max(s, axis=-1, keepdims=True)
            p = jnp.exp(s - m)
            denom = jnp.sum(p, axis=-1, keepdims=True)
            out = jnp.dot(p.astype(BF16), vw, preferred_element_type=F32) / denom
            o_ref[rows, :] = _unstack_heads(out, masks).astype(o_ref.dtype)
            lse_ref[rows, :] = _unstack_heads(jnp.broadcast_to(m + jnp.log(denom), (2 * GRID_W, 128)), masks)

    q_spec, k_spec, v_spec, tp_spec = _na_specs()
    return pl.pallas_call(
        body, out_shape=(jax.ShapeDtypeStruct((SEQ, D_MODEL), BF16), jax.ShapeDtypeStruct((SEQ, D_MODEL), F32)),
        grid=(N_HEADS // 2, SEQ_ROWS // NA_STEP), in_specs=[q_spec, k_spec, v_spec, tp_spec],
        out_specs=(q_spec, q_spec), name="na_fwd", compiler_params=_params())(qkv, qkv, qkv, table)


def _na_bwd(qkv, table, d_out, lse):
    def body(q_ref, k_ref, v_ref, tp_ref, do_ref, lse_ref, dqkv_ref, gp_ref, dk_acc, dv_acc):
        step = pl.program_id(1)

        @pl.when(step == 0)
        def _():
            dk_acc[...] = jnp.zeros_like(dk_acc)
            dv_acc[...] = jnp.zeros_like(dv_acc)
            gp_ref[...] = jnp.zeros_like(gp_ref)

        masks = _head_masks()
        for r in range(NA_STEP):
            rows = slice(r * GRID_W, (r + 1) * GRID_W)
            i = step * NA_STEP + r
            start, dr0 = _na_window(i)
            kw = k_ref[pl.ds(start, NA_KEYS), :]
            vw = v_ref[pl.ds(start, NA_KEYS), :]
            qs = _stack_heads(q_ref[rows, :] * Q_SCALE, masks)
            dos = _stack_heads(do_ref[rows, :], masks)
            p = jnp.exp(_na_scores(qs, kw, tp_ref, dr0) - _stack_columns(lse_ref[rows, :]))
            dp = lax.dot_general(dos, vw, NT, preferred_element_type=F32)
            ds = p * (dp - jnp.sum(p * dp, axis=-1, keepdims=True))
            for a in range(2):
                for c in range(4):
                    gp_ref[a, pl.ds(dr0 + 2 * c, 1), :, :] += (
                        ds[a * GRID_W:(a + 1) * GRID_W, 128 * c:128 * (c + 1)].reshape(1, GRID_W, 128))
            dsb = ds.astype(BF16)
            dq = _unstack_heads(jnp.dot(dsb, kw, preferred_element_type=F32), masks) * Q_SCALE
            dqkv_ref[0, pl.ds(pl.multiple_of(i * GRID_W, GRID_W), GRID_W), :] = dq.astype(dqkv_ref.dtype)
            dk_acc[pl.ds(start, NA_KEYS), :] += lax.dot_general(dsb, qs, TN, preferred_element_type=F32)
            dv_acc[pl.ds(start, NA_KEYS), :] += lax.dot_general(p.astype(BF16), dos, TN, preferred_element_type=F32)

        @pl.when(step == SEQ_ROWS // NA_STEP - 1)
        def _():
            dqkv_ref[1] = dk_acc[...].astype(dqkv_ref.dtype)
            dqkv_ref[2] = dv_acc[...].astype(dqkv_ref.dtype)

    q_spec, k_spec, v_spec, tp_spec = _na_specs()
    return pl.pallas_call(
        body,
        out_shape=(jax.ShapeDtypeStruct((3, SEQ, D_MODEL), BF16), jax.ShapeDtypeStruct((N_HEADS, 16, GRID_W, 128), F32)),
        grid=(N_HEADS // 2, SEQ_ROWS // NA_STEP), in_specs=[q_spec, k_spec, v_spec, tp_spec, q_spec, q_spec],
        out_specs=(pl.BlockSpec((3, SEQ, 128), lambda hp, i: (0, 0, hp)), tp_spec),
        scratch_shapes=[pltpu.VMEM((SEQ, 128), F32), pltpu.VMEM((SEQ, 128), F32)],
        name="na_bwd", compiler_params=_params())(qkv, qkv, qkv, table, d_out, lse)


DIL_STEP = 4


def _dil_geometry(group):
    dil = DIL_GROUPS[group][1]
    sub_len = SEQ // dil
    blocks = sub_len // BAND
    return dil, sub_len, max(blocks // DIL_STEP, 1), max(DIL_STEP // blocks, 1), min(2 * BAND, sub_len)


def _dil_block(step, r, sub_len, subs):
    per_sub = DIL_STEP // subs
    return (r // per_sub) * sub_len, step * per_sub + r % per_sub


def _dil_window(b, sub_len, n_keys):
    if n_keys == sub_len:
        return 0
    return pl.multiple_of(jnp.clip(b * BAND - RADIUS, 0, sub_len - n_keys), RADIUS)


def _dil_bias(b, start, n_keys, slope_ref, hp):
    row = _row_index((2 * BAND, n_keys))
    qpos = b * BAND + (row & (BAND - 1))
    kpos = start + _lane_index((2 * BAND, n_keys))
    dist = jnp.abs(qpos - kpos)
    slope = jnp.where(row < BAND, slope_ref[2 * hp], slope_ref[2 * hp + 1])
    return slope * dist.astype(F32), dist <= RADIUS


def _dil_scores(qs, kw, penalty, valid):
    return jnp.where(valid, lax.dot_general(qs, kw, NT, preferred_element_type=F32) - penalty, NEG_INF)


def _dil_specs(group):
    dil, sub_len, steps, subs, _ = _dil_geometry(group)
    col = group * 24
    rows = DIL_STEP * BAND
    q_spec = pl.BlockSpec((rows, 128), lambda n, hp, b: (n * steps + b, col + hp))
    k_spec = pl.BlockSpec((subs * sub_len, 128), lambda n, hp, b: (n, col + 8 + hp))
    v_spec = pl.BlockSpec((subs * sub_len, 128), lambda n, hp, b: (n, col + 16 + hp))
    tile = pl.BlockSpec((rows, 128), lambda n, hp, b: (n * steps + b, hp))
    smem = pl.BlockSpec(memory_space=pltpu.SMEM)
    return (dil // subs, N_HEADS // 2, steps), q_spec, k_spec, v_spec, tile, smem


def _dil_fwd(group, qkv, slopes):
    _, sub_len, _, subs, n_keys = _dil_geometry(group)

    def body(q_ref, k_ref, v_ref, slope_ref, o_ref, lse_ref):
        hp = pl.program_id(1)
        masks = _head_masks()
        for r in range(DIL_STEP):
            rows = slice(r * BAND, (r + 1) * BAND)
            base, b = _dil_block(pl.program_id(2), r, sub_len, subs)
            start = _dil_window(b, sub_len, n_keys)
            kw = k_ref[pl.ds(base + start, n_keys), :]
            vw = v_ref[pl.ds(base + start, n_keys), :]
            penalty, valid = _dil_bias(b, start, n_keys, slope_ref, hp)
            s = _dil_scores(_stack_heads(q_ref[rows, :] * Q_SCALE, masks), kw, penalty, valid)
            m = jnp.max(s, axis=-1, keepdims=True)
            p = jnp.exp(s - m)
            denom = jnp.sum(p, axis=-1, keepdims=True)
            out = jnp.dot(p.astype(BF16), vw, preferred_element_type=F32) / denom
            o_ref[rows, :] = _unstack_heads(out, masks)
            lse_ref[rows, :] = _unstack_heads(jnp.broadcast_to(m + jnp.log(denom), (2 * BAND, 128)), masks)

    grid, q_spec, k_spec, v_spec, tile, smem = _dil_specs(group)
    return pl.pallas_call(
        body, out_shape=(jax.ShapeDtypeStruct((SEQ, D_MODEL), F32), jax.ShapeDtypeStruct((SEQ, D_MODEL), F32)),
        grid=grid, in_specs=[q_spec, k_spec, v_spec, smem], out_specs=(tile, tile),
        name=f"dil_fwd_{group}", compiler_params=_params())(qkv, qkv, qkv, slopes)


def _dil_merge(outs, lses):
    n_sorted = len(SORTED)

    def body(*refs):
        o_refs, l_refs = refs[:3], refs[3:6]
        out_refs, lse_refs, scratch = refs[6:7 + n_sorted], refs[7 + n_sorted:8 + 2 * n_sorted], refs[-1]
        os_ = [o_refs[0][...]] + [_unsort_tile(scratch, o_refs[1 + j], dil) for j, dil in enumerate(SORTED)]
        ls = [l_refs[0][...]] + [_unsort_tile(scratch, l_refs[1 + j], dil) for j, dil in enumerate(SORTED)]
        m = jnp.maximum(jnp.maximum(ls[0], ls[1]), ls[2])
        es = [jnp.exp(v - m) for v in ls]
        total = es[0] + es[1] + es[2]
        merged = (es[0] * os_[0] + es[1] * os_[1] + es[2] * os_[2]) / total
        lse = m + jnp.log(total)
        out_refs[0][...] = merged
        lse_refs[0][...] = lse
        for j, dil in enumerate(SORTED):
            _sort_tile(scratch, merged, dil, out_refs[1 + j])
            _sort_tile(scratch, lse, dil, lse_refs[1 + j])

    tm = 256
    tile = pl.BlockSpec((tm, D_MODEL), lambda i: (i, 0))
    specs = [tile] + [_sorted_spec(dil, tm=tm) for dil in SORTED]
    shapes = [jax.ShapeDtypeStruct((SEQ, D_MODEL), F32)] + [jax.ShapeDtypeStruct((dil, SEQ // dil, D_MODEL), F32) for dil in SORTED]
    views = lambda ts: [ts[0]] + [_sorted_view(t, dil) for t, dil in zip(ts[1:], SORTED)]
    result = pl.pallas_call(
        body, out_shape=tuple(shapes * 2), grid=(SEQ // tm,), in_specs=specs * 2, out_specs=tuple(specs * 2),
        scratch_shapes=[_sort_scratch(tm)], name="dil_merge", compiler_params=_params())(*views(outs), *views(lses))
    flat = [t.reshape(SEQ, D_MODEL) for t in result]
    return flat[:1 + n_sorted], flat[1 + n_sorted:]


def _dil_bwd(group, qkv, slopes, d_out, out, lse_group, lse_total, into):
    _, sub_len, steps, subs, n_keys = _dil_geometry(group)

    def body(q_ref, k_ref, v_ref, slope_ref, do_ref, o_ref, lg_ref, lt_ref, into_ref, dqkv_ref, dk_acc, dv_acc):
        hp, step = pl.program_id(1), pl.program_id(2)

        @pl.when(step == 0)
        def _():
            dk_acc[...] = jnp.zeros_like(dk_acc)
            dv_acc[...] = jnp.zeros_like(dv_acc)

        masks = _head_masks()
        for r in range(DIL_STEP):
            rows = slice(r * BAND, (r + 1) * BAND)
            base, b = _dil_block(step, r, sub_len, subs)
            start = _dil_window(b, sub_len, n_keys)
            keys = pl.ds(base + start, n_keys)
            kw = k_ref[keys, :]
            vw = v_ref[keys, :]
            penalty, valid = _dil_bias(b, start, n_keys, slope_ref, hp)
            qs = _stack_heads(q_ref[rows, :] * Q_SCALE, masks)
            lse2 = lg_ref[rows, :]
            weight = jnp.exp(lse2 - lt_ref[rows, :])
            do2 = do_ref[rows, :]
            dogs = _stack_heads((weight * do2).astype(BF16), masks)
            delta = _stack_columns(weight) * jnp.sum(_stack_heads(do2 * o_ref[rows, :], masks), axis=-1, keepdims=True)
            p = jnp.exp(_dil_scores(qs, kw, penalty, valid) - _stack_columns(lse2))
            dp = lax.dot_general(dogs, vw, NT, preferred_element_type=F32)
            dsb = (p * (dp - delta)).astype(BF16)
            dq = _unstack_heads(jnp.dot(dsb, kw, preferred_element_type=F32), masks) * Q_SCALE
            dqkv_ref[0, pl.ds(pl.multiple_of(base + b * BAND, BAND), BAND), :] = dq.astype(dqkv_ref.dtype)
            dk_acc[keys, :] += lax.dot_general(dsb, qs, TN, preferred_element_type=F32)
            dv_acc[keys, :] += lax.dot_general(p.astype(BF16), dogs, TN, preferred_element_type=F32)

        @pl.when(step == steps - 1)
        def _():
            dqkv_ref[1] = dk_acc[...].astype(dqkv_ref.dtype)
            dqkv_ref[2] = dv_acc[...].astype(dqkv_ref.dtype)

    grid, q_spec, k_spec, v_spec, tile, smem = _dil_specs(group)
    return pl.pallas_call(
        body, out_shape=jax.ShapeDtypeStruct(into.shape, into.dtype), grid=grid,
        in_specs=[q_spec, k_spec, v_spec, smem, tile, tile, tile, tile, ANY],
        out_specs=pl.BlockSpec((3, subs * sub_len, 128), lambda n, hp, b: (group, n, hp)),
        scratch_shapes=[pltpu.VMEM((subs * sub_len, 128), F32), pltpu.VMEM((subs * sub_len, 128), F32)],
        input_output_aliases={8: 0}, name=f"dil_bwd_{group}", compiler_params=_params(),
    )(qkv, qkv, qkv, slopes, d_out, out, lse_group, lse_total, into)


def _accumulate_rows(acc_ref, i, first, part):
    rows = pl.ds(pl.multiple_of(i * TM_MM, TM_MM), TM_MM)

    @pl.when(first)
    def _():
        acc_ref[rows, :] = part

    @pl.when(jnp.logical_not(first))
    def _():
        acc_ref[rows, :] += part


def _ffn_specs():
    tile = pl.BlockSpec((TM_MM, D_MODEL), lambda d, i: (i, 0))
    gate = pl.BlockSpec((None, None, D_MODEL, FF_PAD), lambda d, i: (d, 0, 0, 0))
    up = pl.BlockSpec((None, None, D_MODEL, FF_PAD), lambda d, i: (d, 1, 0, 0))
    down = pl.BlockSpec((None, FF_PAD, D_MODEL), lambda d, i: (d, 0, 0))
    hidden = pl.BlockSpec((None, TM_MM, FF_PAD), lambda d, i: (d, i, 0))
    whole = pl.BlockSpec((SEQ, D_MODEL), lambda d, i: (0, 0))
    return tile, gate, up, down, hidden, whole


def _ffn_fwd(name, h, w_gu, w_down):
    def body(h_ref, wg_ref, wu_ref, wd_ref, f_ref, hg_ref, hu_ref, act_t_ref):
        hv = h_ref[...]
        hg = jnp.dot(hv, wg_ref[...], preferred_element_type=F32)
        hu = jnp.dot(hv, wu_ref[...], preferred_element_type=F32)
        act = hg * jax.nn.sigmoid(hg) * hu
        act_t_ref[...] = act.T.astype(act_t_ref.dtype)
        hg_ref[...] = hg.astype(hg_ref.dtype)
        hu_ref[...] = hu.astype(hu_ref.dtype)
        _accumulate_rows(f_ref, pl.program_id(1), pl.program_id(0) == 0,
                         jnp.dot(act.astype(BF16), wd_ref[...], preferred_element_type=F32))

    tile, gate, up, down, hidden, whole = _ffn_specs()
    shape = jax.ShapeDtypeStruct((N_DEV, SEQ, FF_PAD), BF16)
    return pl.pallas_call(
        body, out_shape=(jax.ShapeDtypeStruct((SEQ, D_MODEL), F32), shape, shape, jax.ShapeDtypeStruct((N_DEV, FF_PAD, SEQ), BF16)),
        grid=(N_DEV, SEQ // TM_MM), in_specs=[tile, gate, up, down],
        out_specs=(whole, hidden, hidden, pl.BlockSpec((None, FF_PAD, TM_MM), lambda d, i: (d, 0, i))),
        name=name, compiler_params=_params())(h, w_gu, w_gu, w_down)


def _ffn_bwd(name, d_f, w_gu, w_down, hg, hu):
    def body(df_ref, wg_ref, wu_ref, wd_ref, hg_ref, hu_ref, dgu_ref, dh_ref):
        dact = lax.dot_general(df_ref[...], wd_ref[...], NT, preferred_element_type=F32)
        hgv = hg_ref[...].astype(F32)
        sig = jax.nn.sigmoid(hgv)
        d_gate = (dact * hu_ref[...].astype(F32) * (sig * (1.0 + hgv * (1.0 - sig)))).astype(BF16)
        d_up = (dact * hgv * sig).astype(BF16)
        dgu_ref[0] = d_gate
        dgu_ref[1] = d_up
        part = (lax.dot_general(d_gate, wg_ref[...], NT, preferred_element_type=F32)
                + lax.dot_general(d_up, wu_ref[...], NT, preferred_element_type=F32))
        _accumulate_rows(dh_ref, pl.program_id(1), pl.program_id(0) == 0, part)

    tile, gate, up, down, hidden, whole = _ffn_specs()
    return pl.pallas_call(
        body, out_shape=(jax.ShapeDtypeStruct((2, N_DEV, SEQ, FF_PAD), BF16), jax.ShapeDtypeStruct((SEQ, D_MODEL), F32)),
        grid=(N_DEV, SEQ // TM_MM), in_specs=[tile, gate, up, down, hidden, hidden],
        out_specs=(pl.BlockSpec((2, None, TM_MM, FF_PAD), lambda d, i: (0, d, i, 0)), whole),
        name=name, compiler_params=_params())(d_f, w_gu, w_gu, w_down, hg, hu)


def _position():
    return lax.axis_index("x"), lax.axis_index("y"), lax.axis_index("c")


def _flat(p):
    return 4 * p[0] + 2 * p[1] + p[2]


def _peer(me, k):
    x, y, c = me
    return (1 - x if k & 4 else x, 1 - y if k & 2 else y, 1 - c if k & 1 else c)


def _columns(width):
    return lambda ref, d: ref.at[:, pl.ds(pl.multiple_of(d * width, 128), width)]


def _leading(ref, d):
    return ref.at[d]


def _whole(ref, d):
    return ref


def _by_sender(window):
    return lambda ref, sender, k: window(ref, sender)


def _by_distance(ref, sender, k):
    return ref.at[k - 1]


def _prep_weights(me, na_qkv, na_o, dil_qkv, dil_o, gate, up, down, land_shapes):
    na_cols, dil_cols = na_qkv.shape[-1], dil_qkv.shape[-1]
    o_rows = na_o.shape[1]
    tiles = 4
    rows, rows_o = D_MODEL // tiles, o_rows // tiles

    def body(me_ref, naq, nao, dq, do_, g0, u0, d0, g1, u1, d1, *outs):
        def put(t, index, value):
            outs[t][index] = value
            outs[8 + t][index] = value

        put(0, ..., naq[...].astype(BF16))
        put(1, ..., nao[...].astype(BF16))
        put(4, ..., dq[...].astype(BF16))
        put(5, ..., do_[...].astype(BF16))
        for t, (g, u, d) in ((2, (g0, u0, d0)), (6, (g1, u1, d1))):
            for j, part in enumerate((g, u)):
                put(t, (j, slice(None), slice(0, FF_SHARD)), part[...].astype(BF16))
                put(t, (j, slice(None), slice(FF_SHARD, FF_PAD)), jnp.zeros((rows, FF_PAD - FF_SHARD), BF16))
            put(t + 1, (slice(0, FF_SHARD), slice(None)), d[...].astype(BF16))
            put(t + 1, (slice(FF_SHARD, FF_PAD), slice(None)), jnp.zeros((FF_PAD - FF_SHARD, D_MODEL), BF16))

    def tiled(width):
        return pl.BlockSpec((None, rows, width), lambda i, me: (0, i, 0))

    def layer(l, width):
        return pl.BlockSpec((None, rows, width), lambda i, me: (l, i, 0))

    def whole_layer(l):
        return pl.BlockSpec((None, FF_SHARD, D_MODEL), lambda i, me: (l, 0, 0))

    in_specs = [tiled(na_cols), pl.BlockSpec((None, rows_o, D_MODEL), lambda i, me: (0, i, 0)), tiled(dil_cols),
                pl.BlockSpec((None, rows_o, D_MODEL), lambda i, me: (0, i, 0)),
                layer(0, FF_SHARD), layer(0, FF_SHARD), whole_layer(0), layer(1, FF_SHARD), layer(1, FF_SHARD), whole_layer(1)]
    o_shard = pl.BlockSpec((rows_o, D_MODEL), lambda i, me: (i, 0))
    o_land = pl.BlockSpec((None, rows_o, D_MODEL), lambda i, me: (me[0], i, 0))
    gu_shard = pl.BlockSpec((2, rows, FF_PAD), lambda i, me: (0, i, 0))
    gu_land = pl.BlockSpec((None, 2, rows, FF_PAD), lambda i, me: (me[0], 0, i, 0))
    down_shard = pl.BlockSpec((FF_PAD, D_MODEL), lambda i, me: (0, 0))
    down_land = pl.BlockSpec((None, FF_PAD, D_MODEL), lambda i, me: (me[0], 0, 0))

    def qkv_shard(width):
        return pl.BlockSpec((rows, width), lambda i, me: (i, 0))

    def qkv_land(width):
        return pl.BlockSpec((rows, width), lambda i, me: (i, me[0]))

    shard_specs = [qkv_shard(na_cols), o_shard, gu_shard, down_shard, qkv_shard(dil_cols), o_shard, gu_shard, down_shard]
    land_specs = [qkv_land(na_cols), o_land, gu_land, down_land, qkv_land(dil_cols), o_land, gu_land, down_land]
    shard_shapes = [jax.ShapeDtypeStruct(s, BF16) for s in
                    ((D_MODEL, na_cols), (o_rows, D_MODEL), (2, D_MODEL, FF_PAD), (FF_PAD, D_MODEL),
                     (D_MODEL, dil_cols), (o_rows, D_MODEL), (2, D_MODEL, FF_PAD), (FF_PAD, D_MODEL))]
    result = pl.pallas_call(
        body, out_shape=tuple(shard_shapes + list(land_shapes)),
        grid_spec=pltpu.PrefetchScalarGridSpec(num_scalar_prefetch=1, grid=(tiles,), in_specs=in_specs,
                                               out_specs=tuple(shard_specs + land_specs)),
        name="prep_weights", compiler_params=_params())(me, na_qkv, na_o, dil_qkv, dil_o, gate, up, down, gate, up, down)
    return list(result[:8]), list(result[8:])


def _remote_copies(sets, src_refs, land_refs, send_sems, recv_sems, outgoing):
    me = _position()
    copies = []
    for t, (si, src_of, li, dst_of) in enumerate(sets):
        for k in range(1, N_DEV):
            other = _peer(me, k)
            sender = me if outgoing else other
            copies.append(pltpu.make_async_remote_copy(
                src_ref=src_of(src_refs[si], _flat(other)), dst_ref=dst_of(land_refs[li], _flat(sender), k),
                send_sem=send_sems.at[(N_DEV - 1) * t + k - 1], recv_sem=recv_sems.at[(N_DEV - 1) * t + k - 1],
                device_id=other, device_id_type=MESH))
    return copies


def _send_start(name, srcs, lands, sets_by_group):
    n_src, n_land, n_groups = len(srcs), len(lands), len(sets_by_group)

    def body(*refs):
        src_refs, land_refs = refs[:n_src], refs[n_src:n_src + n_land]
        outs = refs[n_src + n_land:]
        for g, sets in enumerate(sets_by_group):
            for cp in _remote_copies(sets, src_refs, land_refs, outs[2 * g], outs[2 * g + 1], True):
                cp.start()
        outs[-1][...] = jnp.zeros_like(outs[-1])

    sem_shapes = []
    for sets in sets_by_group:
        sem_shapes += [pltpu.SemaphoreType.DMA((len(sets) * (N_DEV - 1),))] * 2
    thru = [pltpu.HBM(a.shape, a.dtype) for a in list(srcs) + list(lands)]
    n_sem = len(sem_shapes)
    result = pl.pallas_call(
        body, out_shape=tuple(sem_shapes + thru + [jax.ShapeDtypeStruct((8, 128), F32)]),
        in_specs=[HBM] * (n_src + n_land),
        out_specs=tuple([SEM] * n_sem + [HBM] * (n_src + n_land) + [pl.BlockSpec(memory_space=pltpu.VMEM)]),
        input_output_aliases={i: n_sem + i for i in range(n_src + n_land)},
        compiler_params=pltpu.CompilerParams(has_side_effects=EFFECT), name=name,
    )(*[pltpu.with_memory_space_constraint(a, pltpu.HBM) for a in list(srcs) + list(lands)])
    sems = [(result[2 * g], result[2 * g + 1]) for g in range(n_groups)]
    return sems, list(result[n_sem:n_sem + n_src]), list(result[n_sem + n_src:n_sem + n_src + n_land]), result[-1]


def _send_wait(name, sems, srcs, lands, sets, after):
    n_src, n_land = len(srcs), len(lands)

    def body(*refs):
        src_refs, land_refs = refs[:n_src], refs[n_src:n_src + n_land]
        send_sems, recv_sems = refs[n_src + n_land], refs[n_src + n_land + 1]
        for cp in _remote_copies(sets, src_refs, land_refs, send_sems, recv_sems, True):
            cp.wait_send()
        for cp in _remote_copies(sets, src_refs, land_refs, send_sems, recv_sems, False):
            cp.wait_recv()

    thru = [pltpu.HBM(a.shape, a.dtype) for a in list(srcs) + list(lands)]
    result = pl.pallas_call(
        body, out_shape=tuple(thru), in_specs=[HBM] * (n_src + n_land) + [SEM, SEM] + [ANY] * len(after),
        out_specs=tuple([HBM] * (n_src + n_land)), input_output_aliases={i: i for i in range(n_src + n_land)},
        compiler_params=pltpu.CompilerParams(has_side_effects=EFFECT), name=name,
    )(*srcs, *lands, sems[0], sems[1], *after)
    return list(result[:n_src]), list(result[n_src:])


DIRECT = (1, 2, 4, 6)
PASSED = DIRECT[1:]


def _hbm_passthrough(body, name, arrays, n_sem_in, sem_out_shapes, extra):
    n, n_out = len(arrays), len(sem_out_shapes)
    return pl.pallas_call(
        body, out_shape=tuple(list(sem_out_shapes) + [pltpu.HBM(a.shape, a.dtype) for a in arrays]),
        in_specs=[HBM] * n + [SEM] * n_sem_in + [ANY] * len(extra), out_specs=tuple([SEM] * n_out + [HBM] * n),
        input_output_aliases={i: n_out + i for i in range(n)},
        compiler_params=pltpu.CompilerParams(has_side_effects=EFFECT), name=name)


def _shard_copy(src_ref, land_ref, window, block, to, send_sem, recv_sem, from_shard):
    dst = window(land_ref, _flat(block))
    return pltpu.make_async_remote_copy(src_ref=src_ref if from_shard else dst, dst_ref=dst, send_sem=send_sem,
                                        recv_sem=recv_sem, device_id=to, device_id_type=MESH)


def _gather_start(name, shards, lands, windows, group_sizes):
    n = len(shards)

    def body(*refs):
        shard_refs, land_refs, outs = refs[:n], refs[n:2 * n], refs[2 * n:]
        me = _position()
        t = 0
        for g, size in enumerate(group_sizes):
            for local in range(size):
                for j, k in enumerate(DIRECT):
                    i = len(DIRECT) * local + j
                    _shard_copy(shard_refs[t], land_refs[t], windows[t], me, _peer(me, k), outs[2 * g].at[i],
                                outs[2 * g + 1].at[i], True).start()
                t += 1

    sem_shapes = [pltpu.SemaphoreType.DMA((len(DIRECT) * size,)) for size in group_sizes for _ in range(2)]
    arrays = [pltpu.with_memory_space_constraint(a, pltpu.HBM) for a in list(shards) + list(lands)]
    result = _hbm_passthrough(body, name, arrays, 0, sem_shapes, ())(*arrays)
    n_sem = len(sem_shapes)
    sems = [(result[2 * g], result[2 * g + 1]) for g in range(len(group_sizes))]
    return sems, list(result[n_sem:n_sem + n]), list(result[n_sem + n:])


def _gather_pass_on(name, sems, shards, lands, windows, after):
    n = len(shards)

    def body(*refs):
        shard_refs, land_refs = refs[:n], refs[n:2 * n]
        recv_sems = refs[2 * n + 1]
        pass_send, pass_recv = refs[2 * n + 2 + len(after)], refs[2 * n + 3 + len(after)]
        me = _position()
        sibling = _peer(me, 1)
        for t in range(n):
            for j, k in enumerate(PASSED):
                sender = _peer(me, k)
                arrived = len(DIRECT) * t + 1 + j
                _shard_copy(shard_refs[t], land_refs[t], windows[t], sender, me, refs[2 * n].at[arrived], recv_sems.at[arrived],
                            True).wait_recv()
                i = len(PASSED) * t + j
                _shard_copy(shard_refs[t], land_refs[t], windows[t], sender, sibling, pass_send.at[i], pass_recv.at[i],
                            False).start()

    sem_shapes = [pltpu.SemaphoreType.DMA((len(PASSED) * n,))] * 2
    result = _hbm_passthrough(body, name, list(shards) + list(lands), 2, sem_shapes, after)(
        *shards, *lands, sems[0], sems[1], *after)
    return (result[0], result[1]), list(result[2:2 + n]), list(result[2 + n:])


def _gather_wait(name, sems, pass_sems, shards, lands, windows, after):
    n = len(shards)

    def body(*refs):
        shard_refs, land_refs = refs[:n], refs[n:2 * n]
        send_sems, recv_sems, pass_send, pass_recv = refs[2 * n:2 * n + 4]
        me = _position()
        sibling = _peer(me, 1)
        for t in range(n):
            for j, k in enumerate(DIRECT):
                i = len(DIRECT) * t + j
                _shard_copy(shard_refs[t], land_refs[t], windows[t], me, _peer(me, k), send_sems.at[i], recv_sems.at[i],
                            True).wait_send()
            _shard_copy(shard_refs[t], land_refs[t], windows[t], sibling, me, send_sems.at[len(DIRECT) * t],
                        recv_sems.at[len(DIRECT) * t], True).wait_recv()
            for j, k in enumerate(PASSED):
                i = len(PASSED) * t + j
                _shard_copy(shard_refs[t], land_refs[t], windows[t], _peer(me, k), sibling, pass_send.at[i], pass_recv.at[i],
                            False).wait_send()
                _shard_copy(shard_refs[t], land_refs[t], windows[t], _peer(sibling, k), me, pass_send.at[i], pass_recv.at[i],
                            False).wait_recv()

    result = _hbm_passthrough(body, name, list(shards) + list(lands), 4, [], after)(
        *shards, *lands, sems[0], sems[1], pass_sems[0], pass_sems[1], *after)
    return list(result[n:])


def _all_gather(name, locals_, out_shapes, windows, deps=()):
    n = len(locals_)

    def body(*refs):
        src_refs, out_refs = refs[:n], refs[n + len(deps):2 * n + len(deps)]
        send_sems, recv_sems, local_sems = refs[2 * n + len(deps):]
        x, y, c = _position()
        me, sibling = (x, y, c), (x, y, 1 - c)
        chips = [(1 - x, y), (x, 1 - y), (1 - x, 1 - y)]

        def copy(t, k, block, to, from_local=False):
            dst = windows[t](out_refs[t], _flat(block))
            return pltpu.make_async_remote_copy(
                src_ref=src_refs[t] if from_local else dst, dst_ref=dst, send_sem=send_sems.at[t, k],
                recv_sem=recv_sems.at[t, k], device_id=to, device_id_type=MESH)

        mine = [pltpu.make_async_copy(src_refs[t], windows[t](out_refs[t], _flat(me)), local_sems.at[t]) for t in range(n)]
        sends = []
        for t in range(n):
            mine[t].start()
            sends.append(copy(t, 0, me, sibling, True))
            sends += [copy(t, 1 + j, me, (*chip, c), True) for j, chip in enumerate(chips)]
        for cp in sends:
            cp.start()
        for t in range(n):
            for j, chip in enumerate(chips):
                copy(t, 1 + j, (*chip, c), me).wait_recv()
                passed = copy(t, 4 + j, (*chip, c), sibling)
                passed.start()
                sends.append(passed)
        for t in range(n):
            copy(t, 0, sibling, me).wait_recv()
            for j, chip in enumerate(chips):
                copy(t, 4 + j, (*chip, 1 - c), me).wait_recv()
        for cp in sends:
            cp.wait_send()
        for cp in mine:
            cp.wait()

    return pl.pallas_call(
        body, out_shape=tuple(out_shapes), in_specs=[ANY] * (n + len(deps)), out_specs=tuple([ANY] * n),
        scratch_shapes=[pltpu.SemaphoreType.DMA((n, 7)), pltpu.SemaphoreType.DMA((n, 7)), pltpu.SemaphoreType.DMA((n,))],
        name=name)(*locals_, *deps)


def _adamw(name, me, lands, owns, w, m, v, *, grid, land_specs, own_specs, p_spec):
    n_land = len(lands)

    def body(me_ref, *refs):
        land_refs, own_refs = refs[:n_land], refs[n_land:n_land + len(owns)]
        w_ref, m_ref, v_ref, g_ref, delta_ref, m_out, v_out = refs[n_land + len(owns):]
        ncols = w_ref.shape[-1]
        sums = []
        for i, land_ref in enumerate(land_refs):
            g = own_refs[i][...].astype(F32) if owns else land_ref[0].astype(F32)
            for s in range(0 if owns else 1, land_ref.shape[0]):
                g = g + land_ref[s].astype(F32)
            sums.append(g[:, :ncols])
        g = sums[0] if n_land == 1 else jnp.where(pl.program_id(0) == 0, sums[0], sums[1])
        m_new = ADAM_B1 * m_ref[...] + (1.0 - ADAM_B1) * g
        v_new = ADAM_B2 * v_ref[...] + (1.0 - ADAM_B2) * jnp.square(g)
        m_hat = m_new / (1.0 - ADAM_B1 ** ADAM_STEP)
        v_hat = v_new / (1.0 - ADAM_B2 ** ADAM_STEP)
        g_ref[...] = g
        delta_ref[...] = -ADAM_LR * (m_hat / (jnp.sqrt(v_hat) + ADAM_EPS) + ADAM_WD * w_ref[...])
        m_out[...] = m_new
        v_out[...] = v_new

    shape = jax.ShapeDtypeStruct(w.shape, F32)
    return pl.pallas_call(
        body, out_shape=(shape,) * 4,
        grid_spec=pltpu.PrefetchScalarGridSpec(
            num_scalar_prefetch=1, grid=grid, in_specs=list(land_specs) + list(own_specs) + [p_spec, p_spec, p_spec],
            out_specs=(p_spec,) * 4),
        name=name, compiler_params=_params())(me, *lands, *owns, w, m, v)


def _row(p, layer):
    return p[layer][None, :]


def _square(name, a, b, dims, out_dtype, deps=()):
    if a.shape == (D_MODEL, SEQ):
        return _matmul(name, a, b, grid=(2, 1), a_spec=pl.BlockSpec((512, SEQ), lambda i, k: (i, 0)),
                       b_spec=pl.BlockSpec((SEQ, D_MODEL), lambda i, k: (0, 0)),
                       o_spec=pl.BlockSpec((512, D_MODEL), lambda i, k: (i, 0)),
                       out_shape=jax.ShapeDtypeStruct((D_MODEL, D_MODEL), out_dtype), dims=NN, acc_shape=(8, 128),
                       deps=deps)
    return _matmul(name, a, b, grid=(SEQ // TM_MM, 1), a_spec=pl.BlockSpec((TM_MM, D_MODEL), lambda i, k: (i, 0)),
                   b_spec=pl.BlockSpec((D_MODEL, D_MODEL), lambda i, k: (0, 0)),
                   o_spec=pl.BlockSpec((TM_MM, D_MODEL), lambda i, k: (i, 0)),
                   out_shape=jax.ShapeDtypeStruct((SEQ, D_MODEL), out_dtype), dims=dims, acc_shape=(8, 128), deps=deps)


def _grouped_matmul(name, a_list, b, *, n_tiles, a_block, b_spec, o_spec, out_shape):
    n_groups = len(a_list)

    def a_spec(g):
        def index(j, i):
            mine = j // 3
            return (jnp.where(mine == g, i, jnp.where(mine < g, 0, n_tiles - 1)), 0)
        return pl.BlockSpec(a_block, index)

    def body(*refs):
        b_ref, o_ref = refs[n_groups], refs[n_groups + 1]
        mine = pl.program_id(0) // 3
        for g in range(n_groups):
            @pl.when(mine == g)
            def _(g=g):
                o_ref[...] = jnp.dot(refs[g][...], b_ref[...], preferred_element_type=F32).astype(o_ref.dtype)

    return pl.pallas_call(
        body, out_shape=out_shape, grid=(3 * n_groups, n_tiles), in_specs=[a_spec(g) for g in range(n_groups)] + [b_spec],
        out_specs=o_spec, name=name, compiler_params=_params())(*a_list, b)


def _qkv_fwd(name, hs, w):
    return _grouped_matmul(name, hs, w, n_tiles=SEQ // TM_MM, a_block=(TM_MM, D_MODEL),
                           b_spec=pl.BlockSpec((D_MODEL, D_MODEL), lambda j, i: (0, j)),
                           o_spec=pl.BlockSpec((TM_MM, D_MODEL), lambda j, i: (i, j)),
                           out_shape=jax.ShapeDtypeStruct((SEQ, 3 * len(hs) * D_MODEL), BF16))


def _qkv_dw(name, hs_t, dqkv):
    return _grouped_matmul(name, hs_t, dqkv, n_tiles=2, a_block=(512, SEQ),
                           b_spec=pl.BlockSpec((None, SEQ, D_MODEL), lambda j, i: (j, 0, 0)),
                           o_spec=pl.BlockSpec((512, D_MODEL), lambda j, i: (i, j)),
                           out_shape=jax.ShapeDtypeStruct((D_MODEL, 3 * len(hs_t) * D_MODEL), BF16))


def _proj_do_sorted(name, d_a, w_o):
    def body(da_ref, w_ref, *refs):
        value = lax.dot_general(da_ref[...], w_ref[...], NT, preferred_element_type=F32)
        refs[0][...] = value
        for j, dil in enumerate(SORTED):
            _sort_tile(refs[-1], value, dil, refs[1 + j])

    tile = pl.BlockSpec((TM, D_MODEL), lambda i: (i, 0))
    shapes = [jax.ShapeDtypeStruct((SEQ, D_MODEL), F32)] + [jax.ShapeDtypeStruct((dil, SEQ // dil, D_MODEL), F32) for dil in SORTED]
    result = pl.pallas_call(
        body, out_shape=tuple(shapes), grid=(SEQ // TM,),
        in_specs=[tile, pl.BlockSpec((D_MODEL, D_MODEL), lambda i: (0, 0))],
        out_specs=tuple([tile] + [_sorted_spec(dil) for dil in SORTED]), scratch_shapes=[_sort_scratch()],
        name=name, compiler_params=_params())(d_a, w_o)
    return [t.reshape(SEQ, D_MODEL) for t in result]


def _qkv_dh(name, dqkv, w, n_chunks, deps):
    return _matmul(name, dqkv, w, grid=(n_chunks // 3, SEQ // TM, 1),
                   a_spec=pl.BlockSpec((3, TM, D_MODEL), lambda g, i, k: (g, i, 0)),
                   b_spec=pl.BlockSpec((D_MODEL, 3 * D_MODEL), lambda g, i, k: (0, g)),
                   o_spec=pl.BlockSpec((None, TM, D_MODEL), lambda g, i, k: (g, i, 0)),
                   out_shape=jax.ShapeDtypeStruct((n_chunks // 3, SEQ, D_MODEL), F32), dims=NT, acc_shape=(8, 128),
                   deps=deps, inner=3)


def _local_step(x, target, norms, rpb, fetch, emit, deps):
    mix_pre, mix_post, ffn_pre, ffn_post = norms
    slopes = 2.0 ** (-8.0 * jnp.arange(1, N_HEADS + 1, dtype=F32) / N_HEADS)
    rpb_pad = jnp.pad(rpb, ((0, 0), (0, 1), (0, 128 - 31)))
    saved = []

    for layer in range(2):
        tag = f"l{layer}"
        if layer == 0:
            h = _rms_fwd(tag + "_norm_mix", x, _row(mix_pre, layer), out_dtype=BF16, deps=deps)
            hs = [h]
            table = _rpb_table(rpb_pad)
            w_qkv, w_o = fetch("na", [table, h], [h])
            qkv = _qkv_fwd(tag + "_qkv", hs, w_qkv)
            o, lse = _na_fwd(qkv, table)
            mixer = (hs, qkv, o, lse, table)
        else:
            hs = [t.reshape(SEQ, D_MODEL) for t in
                  _rms_fwd(tag + "_norm_mix", x, _row(mix_pre, layer), out_dtype=BF16, sorted_too=True)]
            w_qkv, w_o = fetch("dil", [x], [hs[0]])
            qkv = _qkv_fwd(tag + "_qkv", hs, w_qkv)
            outs, lses = zip(*[_dil_fwd(g, qkv, slopes * dil) for g, (_, dil) in enumerate(DIL_GROUPS)])
            merged, lse_total = _dil_merge(outs, lses)
            o = merged[0]
            mixer = (hs, qkv, merged, lses, lse_total)
        a = _square(tag + "_proj", o, w_o, NN, F32)
        x1 = _rms_fwd(tag + "_post_mix", a, _row(mix_post, layer), res=x)
        h2 = _rms_fwd(tag + "_norm_ffn", x1, _row(ffn_pre, layer), out_dtype=BF16)
        w_gu, w_down = fetch(f"ffn{layer}", [a], [h2])
        f, hg, hu, act_t = _ffn_fwd(tag + "_ffn", h2, w_gu, w_down)
        x2 = _rms_fwd(tag + "_post_ffn", f, _row(ffn_post, layer), res=x1)
        transposed = ([t.T for t in hs], o.astype(BF16).T, h2.T, act_t)
        saved.append((x, mixer, a, x1, transposed, hg, hu, f, w_qkv, w_o, w_gu, w_down))
        x = x2

    dx, loss = _loss_head("loss_head", x, target)
    d_norm = {k: [None, None] for k in ("mix_pre", "mix_post", "ffn_pre", "ffn_post")}
    d_rpb = None

    for layer in (1, 0):
        tag = f"b{layer}"
        x0, mixer, a, x1, (h_t, o_t, h2_t, act_t), hg, hu, f, w_qkv, w_o, w_gu, w_down = saved[layer]
        d_f, d_norm["ffn_post"][layer] = _rms_bwd(tag + "_post_ffn", f, _row(ffn_post, layer), [dx], out_dtype=BF16)
        dgu, d_h2 = _ffn_bwd(tag + "_ffn", d_f, w_gu, w_down, hg, hu)
        d_down = _matmul(
            tag + "_ffn_ddown", act_t, d_f, grid=(N_DEV, 1),
            a_spec=pl.BlockSpec((None, FF_PAD, SEQ), lambda d, k: (d, 0, 0)),
            b_spec=pl.BlockSpec((SEQ, D_MODEL), lambda d, k: (0, 0)),
            o_spec=pl.BlockSpec((None, FF_PAD, D_MODEL), lambda d, k: (d, 0, 0)),
            out_shape=jax.ShapeDtypeStruct((N_DEV, FF_PAD, D_MODEL), BF16), dims=NN, acc_shape=(8, 128))
        d_gu = _matmul(
            tag + "_ffn_dgu", h2_t, dgu, grid=(2, N_DEV, 1),
            a_spec=pl.BlockSpec((D_MODEL, SEQ), lambda t, d, k: (0, 0)),
            b_spec=pl.BlockSpec((None, None, SEQ, FF_PAD), lambda t, d, k: (t, d, 0, 0)),
            o_spec=pl.BlockSpec((None, None, D_MODEL, FF_PAD), lambda t, d, k: (d, t, 0, 0)),
            out_shape=jax.ShapeDtypeStruct((N_DEV, 2, D_MODEL, FF_PAD), BF16), dims=NN, acc_shape=(8, 128))
        sent = emit(f"ffn{layer}", [d_gu, d_down])
        dx1, d_norm["ffn_pre"][layer] = _rms_bwd(tag + "_norm_ffn", x1, _row(ffn_pre, layer), [d_h2], res=dx, deps=sent)
        d_a, d_norm["mix_post"][layer] = _rms_bwd(tag + "_post_mix", a, _row(mix_post, layer), [dx1], out_dtype=BF16)
        d_wo = _square(tag + "_proj_dw", o_t, d_a, NN, BF16)
        if layer == 0:
            _, qkv, o, lse, table = mixer
            d_o = _square(tag + "_proj_do", d_a, w_o, NT, BF16)
            dqkv, gp = _na_bwd(qkv, table, d_o, lse)
            d_rpb = _rpb_grad(gp)[:, :15, :31]
            sent = emit("na", [_qkv_dw(tag + "_qkv_dw", h_t, dqkv), d_wo])
            d_h = _qkv_dh(tag + "_qkv_dh", dqkv, w_qkv, 3, sent)
            dx, d_norm["mix_pre"][layer] = _rms_bwd(tag + "_norm_mix", x0, _row(mix_pre, layer), [d_h[0]], res=dx1)
        else:
            _, qkv, merged, lses, lse_total = mixer
            d_o = _proj_do_sorted(tag + "_proj_do", d_a, w_o)
            dqkv = lax.empty((3 * len(DIL_GROUPS), SEQ, D_MODEL), BF16)
            for g, (_, dil) in enumerate(DIL_GROUPS):
                dqkv = _dil_bwd(g, qkv, slopes * dil, d_o[g], merged[g], lses[g], lse_total[g], dqkv)
            sent = emit("dil", [_qkv_dw(tag + "_qkv_dw", h_t, dqkv), d_wo])
            d_h = _qkv_dh(tag + "_qkv_dh", dqkv, w_qkv, 9, sent)
            dx, d_norm["mix_pre"][layer] = _rms_bwd(tag + "_norm_mix", x0, _row(mix_pre, layer), None, res=dx1, groups=d_h)

    d_gains = [jnp.concatenate(d_norm[k], axis=0) for k in ("mix_pre", "mix_post", "ffn_pre", "ffn_post")]
    return loss, dx, d_gains, d_rpb


RPB_SIZE = N_HEADS * 15 * 31


def _pack_small(gains, rpb, last=None):
    top = jnp.concatenate(gains, axis=0).reshape(64, 128)
    bottom = jnp.pad(rpb.reshape(-1), (0, 64 * 128 - RPB_SIZE))
    if last is not None:
        bottom = bottom + jnp.pad(last.reshape(1), (64 * 128 - 1, 0))
    return jnp.concatenate([top, bottom.reshape(64, 128)], axis=0)


def _unpack_small(p):
    gains = p[:64].reshape(4, 2, D_MODEL)
    rpb = p[64:].reshape(-1)[:RPB_SIZE].reshape(1, N_HEADS, 15, 31)
    return [gains[i] for i in range(4)], rpb


GROUPS = ("na", "ffn0", "dil", "ffn1")


def kernel(x, norm_mix_pre, norm_mix_post, norm_ffn_pre, norm_ffn_post, na_w_qkv, na_w_o, na_rpb, dil_w_qkv, dil_w_o, ffn_w_gate, ffn_w_up, ffn_w_down, loss_target, m_norm_mix_pre, m_norm_mix_post, m_norm_ffn_pre, m_norm_ffn_post, m_na_w_qkv, m_na_w_o, m_na_rpb, m_dil_w_qkv, m_dil_w_o, m_ffn_w_gate, m_ffn_w_up, m_ffn_w_down, v_norm_mix_pre, v_norm_mix_post, v_norm_ffn_pre, v_norm_ffn_post, v_na_w_qkv, v_na_w_o, v_na_rpb, v_dil_w_qkv, v_dil_w_o, v_ffn_w_gate, v_ffn_w_up, v_ffn_w_down):
    na_cols, dil_cols, o_rows = 3 * D_MODEL // N_DEV, 9 * D_MODEL // N_DEV, D_MODEL // N_DEV
    ff_pad = FF_PAD - FF_SHARD
    me = (4 * lax.axis_index("x") + 2 * lax.axis_index("y") + lax.axis_index("c")).astype(jnp.int32).reshape(1)

    full = {
        "na": [((D_MODEL, 3 * D_MODEL), _columns(na_cols)), ((N_DEV, o_rows, D_MODEL), _leading)],
        "dil": [((D_MODEL, 9 * D_MODEL), _columns(dil_cols)), ((N_DEV, o_rows, D_MODEL), _leading)],
        "ffn0": [((N_DEV, 2, D_MODEL, FF_PAD), _leading), ((N_DEV, FF_PAD, D_MODEL), _leading)],
        "ffn1": [((N_DEV, 2, D_MODEL, FF_PAD), _leading), ((N_DEV, FF_PAD, D_MODEL), _leading)],
    }
    block = {
        "na": [(D_MODEL, na_cols), (o_rows, D_MODEL)], "dil": [(D_MODEL, dil_cols), (o_rows, D_MODEL)],
        "ffn0": [(2, D_MODEL, FF_PAD), (FF_PAD, D_MODEL)], "ffn1": [(2, D_MODEL, FF_PAD), (FF_PAD, D_MODEL)],
    }

    land_shapes = [jax.ShapeDtypeStruct(full[g][t][0], BF16) for g in GROUPS for t in range(2)]
    windows = [full[g][t][1] for g in GROUPS for t in range(2)]
    shards, lands = _prep_weights(me, na_w_qkv, na_w_o, dil_w_qkv, dil_w_o, ffn_w_gate, ffn_w_up, ffn_w_down, land_shapes)
    sems, shards, lands = _gather_start("gather_start", shards, lands, windows, [2] * len(GROUPS))

    def fetch(group, early, late):
        gi = GROUPS.index(group)
        mine = slice(2 * gi, 2 * gi + 2)
        pass_sems, shards_g, lands_g = _gather_pass_on(f"gather_pass_{group}", sems[gi], shards[mine], lands[mine],
                                                       windows[mine], early)
        qkv, o = _gather_wait(f"gather_wait_{group}", sems[gi], pass_sems, shards_g, lands_g, windows[mine], late)
        return (qkv, o.reshape(D_MODEL, D_MODEL)) if group in ("na", "dil") else (qkv, o)

    def grad_source(group, t):
        return _columns(block[group][0][1]) if (group in ("na", "dil") and t == 0) else _leading

    in_flight = {}

    def emit(group, grads):
        if group in ("na", "dil"):
            grads = [grads[0], grads[1].reshape(N_DEV, o_rows, D_MODEL)]
        sets = [(t, grad_source(group, t), t, _by_distance) for t in range(2)]
        landing = [lax.empty((N_DEV - 1,) + block[group][t], BF16) for t in range(2)]
        sems_g, grads, landing, tok = _send_start(f"exchange_start_{group}", grads, landing, [sets])
        in_flight[group] = (sems_g[0], grads, landing, sets)
        return [tok]

    norms = (norm_mix_pre, norm_mix_post, norm_ffn_pre, norm_ffn_post)
    loss, grad_x, d_gains, d_rpb = _local_step(x[0], loss_target[0], norms, na_rpb[0], fetch, emit, [shards[0]])

    landed, sent = {}, {}

    def wait_for(group, after):
        sems_g, grads, landing, sets = in_flight[group]
        sent[group], landed[group] = _send_wait(f"exchange_wait_{group}", sems_g, grads, landing, sets, after)

    for group in ("ffn1", "dil", "ffn0"):
        wait_for(group, [grad_x])

    def one(rows, tile, ncols, columns):
        own = (pl.BlockSpec((tile, ncols), lambda i, me: (i, me[0])) if columns
               else pl.BlockSpec((None, tile, ncols), lambda i, me: (me[0], i, 0)))
        return dict(grid=(rows // tile,), land_specs=[pl.BlockSpec((N_DEV - 1, tile, ncols), lambda i, me: (0, i, 0))],
                    own_specs=[own], p_spec=pl.BlockSpec((None, tile, ncols), lambda i, me: (0, i, 0)))

    def layered(block_shape, index, p_block, n_tiles):
        def specs(lead_size, lead):
            shape = (lead_size,) + block_shape
            return [pl.BlockSpec(shape, lambda l, r, me: index(lead(me), jnp.where(l == 0, r, n_tiles - 1))),
                    pl.BlockSpec(shape, lambda l, r, me: index(lead(me), jnp.where(l == 0, 0, r)))]
        return dict(grid=(2, n_tiles), land_specs=specs(N_DEV - 1, lambda me: 0), own_specs=specs(None, lambda me: me[0]),
                    p_spec=pl.BlockSpec(p_block, lambda l, r, me: (l, r, 0)))

    gu_lands, gu_owns = [landed["ffn0"][0], landed["ffn1"][0]], [sent["ffn0"][0], sent["ffn1"][0]]
    down_lands, down_owns = [landed["ffn0"][1], landed["ffn1"][1]], [sent["ffn0"][1], sent["ffn1"][1]]
    updates = {
        "dil_w_qkv": _adamw("adamw_dil_qkv", me, [landed["dil"][0]], [sent["dil"][0]], dil_w_qkv, m_dil_w_qkv, v_dil_w_qkv,
                            **one(D_MODEL, 128, dil_cols, True)),
        "dil_w_o": _adamw("adamw_dil_o", me, [landed["dil"][1]], [sent["dil"][1]], dil_w_o, m_dil_w_o, v_dil_w_o,
                          **one(o_rows, o_rows, D_MODEL, False)),
        "ffn_w_gate": _adamw("adamw_gate", me, gu_lands, gu_owns, ffn_w_gate, m_ffn_w_gate, v_ffn_w_gate,
                             **layered((None, 128, FF_PAD), lambda lead, r: (lead, 0, r, 0), (None, 128, FF_SHARD), 8)),
        "ffn_w_up": _adamw("adamw_up", me, gu_lands, gu_owns, ffn_w_up, m_ffn_w_up, v_ffn_w_up,
                           **layered((None, 128, FF_PAD), lambda lead, r: (lead, 1, r, 0), (None, 128, FF_SHARD), 8)),
        "ffn_w_down": _adamw("adamw_down", me, down_lands, down_owns, ffn_w_down, m_ffn_w_down, v_ffn_w_down,
                             **layered((176, D_MODEL), lambda lead, r: (lead, r, 0), (None, 176, D_MODEL), 2)),
    }
    done = [u[0] for u in updates.values()]
    small = _all_gather("gather_small", [_pack_small(d_gains, d_rpb, loss)], [jax.ShapeDtypeStruct((N_DEV, 128, 128), F32)],
                        [_leading], deps=done)[0]
    wait_for("na", [small])
    updates["na_w_qkv"] = _adamw("adamw_na_qkv", me, [landed["na"][0]], [sent["na"][0]], na_w_qkv, m_na_w_qkv, v_na_w_qkv,
                                 **one(D_MODEL, 256, na_cols, True))
    updates["na_w_o"] = _adamw("adamw_na_o", me, [landed["na"][1]], [sent["na"][1]], na_w_o, m_na_w_o, v_na_w_o,
                               **one(o_rows, o_rows, D_MODEL, False))
    gains = [norm_mix_pre, norm_mix_post, norm_ffn_pre, norm_ffn_post]
    m_gains = [m_norm_mix_pre, m_norm_mix_post, m_norm_ffn_pre, m_norm_ffn_post]
    v_gains = [v_norm_mix_pre, v_norm_mix_post, v_norm_ffn_pre, v_norm_ffn_post]
    packed = _adamw("adamw_small", me, [small], (), _pack_small(gains, na_rpb)[None], _pack_small(m_gains, m_na_rpb)[None],
                    _pack_small(v_gains, v_na_rpb)[None], grid=(1,),
                    land_specs=[pl.BlockSpec((N_DEV, 128, 128), lambda i, me: (0, 0, 0))], own_specs=[],
                    p_spec=pl.BlockSpec((None, 128, 128), lambda i, me: (0, 0, 0)))
    small_out = [_unpack_small(p[0]) for p in packed]

    order = ["na_w_qkv", "na_w_o", "na_rpb", "dil_w_qkv", "dil_w_o", "ffn_w_gate", "ffn_w_up", "ffn_w_down"]
    result = [packed[0][0, 127, 127], grad_x[None]]
    for kind in range(4):
        gains_k, rpb_k = small_out[kind]
        result += gains_k
        result += [rpb_k if name == "na_rpb" else updates[name][kind] for name in order]
    return tuple(result)
```

```python
import functools

import jax
import jax.numpy as jnp
from jax import lax
from jax.experimental import pallas as pl
from jax.experimental.pallas import tpu as pltpu

F32 = jnp.float32
BF16 = jnp.bfloat16
MESH = pl.DeviceIdType.MESH
ANY = pl.BlockSpec(memory_space=pl.ANY)
HBM = pl.BlockSpec(memory_space=pltpu.HBM)
SEM = pl.BlockSpec(memory_space=pltpu.SEMAPHORE)
EFFECT = pltpu.SideEffectType.DATAFLOW_SIDE_EFFECTING

N_DEV = 8
SEQ = 2048
D_MODEL = 1024
N_HEADS = 16
HEAD_DIM = 64
GRID_W = 64
NA_ROWS = 8
SEQ_ROWS = SEQ // GRID_W
DIL_GROUPS = ((128, 1), (512, 4), (2048, 16))
BAND = 128
RADIUS = 64
FF_SHARD = 352
FF_PAD = 384
RMS_EPS = 1e-6
NEG_INF = -1e30
Q_SCALE = HEAD_DIM ** -0.5

ADAM_LR = 0.001
ADAM_B1 = 0.9
ADAM_B2 = 0.999
ADAM_EPS = 1e-08
ADAM_WD = 0.01
ADAM_STEP = 10

VMEM_LIMIT = 56 * 1024 * 1024
TM = 512
TM_MM = 1024

NN = (((1,), (0,)), ((), ()))
NT = (((1,), (1,)), ((), ()))
TN = (((0,), (0,)), ((), ()))


def _params():
    return pltpu.CompilerParams(vmem_limit_bytes=VMEM_LIMIT)


def _matmul(name, a, b, *, grid, a_spec, b_spec, o_spec, out_shape, dims, acc_shape, deps=(), inner=1):
    nk = grid[-1]
    kaxis = len(grid) - 1

    def body(a_ref, b_ref, *rest):
        o_ref, acc_ref = rest[-2], rest[-1]
        if inner == 1:
            part = lax.dot_general(a_ref[...].astype(BF16), b_ref[...].astype(BF16), dims, preferred_element_type=F32)
        elif len(b_ref.shape) == 2:
            a_all = jnp.concatenate([a_ref[j].astype(BF16) for j in range(inner)], axis=1)
            part = lax.dot_general(a_all, b_ref[...].astype(BF16), dims, preferred_element_type=F32)
        else:
            part = sum(lax.dot_general(a_ref[j].astype(BF16), b_ref[j].astype(BF16), dims, preferred_element_type=F32)
                       for j in range(inner))
        if nk == 1:
            o_ref[...] = part.astype(o_ref.dtype)
        else:
            k = pl.program_id(kaxis)

            @pl.when(k == 0)
            def _():
                acc_ref[...] = part

            @pl.when(k > 0)
            def _():
                acc_ref[...] += part

            @pl.when(k == nk - 1)
            def _():
                o_ref[...] = acc_ref[...].astype(o_ref.dtype)

    return pl.pallas_call(
        body, out_shape=out_shape, grid=grid, in_specs=[a_spec, b_spec] + [ANY] * len(deps), out_specs=o_spec,
        scratch_shapes=[pltpu.VMEM(acc_shape, F32)], name=name, compiler_params=_params())(a, b, *deps)


SORTED = tuple(d for _, d in DIL_GROUPS if d > 1)
LANE_CHUNKS = D_MODEL // 128


def _sort_scratch(tm=TM):
    return pltpu.VMEM((LANE_CHUNKS, tm, 128), F32)


def _sorted_view(t, dil):
    return t.reshape(dil, SEQ // dil, D_MODEL)


def _sorted_spec(dil, lead=(), tm=TM):
    return pl.BlockSpec((None,) * len(lead) + (dil, tm // dil, D_MODEL), lambda i: tuple(lead) + (0, i, 0))


def _sort_tile(scratch, value, dil, out_ref):
    tm = value.shape[0]
    for c in range(LANE_CHUNKS):
        scratch[c] = value[:, 128 * c:128 * (c + 1)]
    for r in range(dil):
        rows = [scratch.at[c][pl.ds(r, tm // dil, stride=dil), :] for c in range(LANE_CHUNKS)]
        out_ref[r] = jnp.concatenate(rows, axis=1).astype(out_ref.dtype)


def _unsort_tile(scratch, in_ref, dil):
    for r in range(dil):
        value = in_ref[r].astype(F32)
        for c in range(LANE_CHUNKS):
            scratch.at[c][pl.ds(r, value.shape[0], stride=dil), :] = value[:, 128 * c:128 * (c + 1)]
    return jnp.concatenate([scratch[c] for c in range(LANE_CHUNKS)], axis=1)


def _rms_fwd(name, x, g, res=None, out_dtype=F32, deps=(), sorted_too=False):
    n_tiles = SEQ // TM
    has_res = res is not None
    n_in = 2 + has_res + len(deps)

    def body(*refs):
        x_ref, g_ref = refs[0], refs[1]
        xv = x_ref[...]
        r = lax.rsqrt(jnp.mean(xv * xv, axis=-1, keepdims=True) + RMS_EPS)
        y = xv * r * g_ref[...]
        if has_res:
            y = refs[2][...] + y
        refs[n_in][...] = y.astype(out_dtype)
        if sorted_too:
            for j, dil in enumerate(SORTED):
                _sort_tile(refs[-1], y, dil, refs[n_in + 1 + j])

    tile = pl.BlockSpec((TM, D_MODEL), lambda i: (i, 0))
    gspec = pl.BlockSpec((1, D_MODEL), lambda i: (0, 0))
    ins = [x, g] + ([res] if has_res else []) + list(deps)
    specs = [tile, gspec] + ([tile] if has_res else []) + [ANY] * len(deps)
    shapes, out_specs = [jax.ShapeDtypeStruct((SEQ, D_MODEL), out_dtype)], [tile]
    if sorted_too:
        shapes += [jax.ShapeDtypeStruct((dil, SEQ // dil, D_MODEL), out_dtype) for dil in SORTED]
        out_specs += [_sorted_spec(dil) for dil in SORTED]
    result = pl.pallas_call(
        body, out_shape=tuple(shapes), grid=(n_tiles,), in_specs=specs, out_specs=tuple(out_specs),
        scratch_shapes=[_sort_scratch()] if sorted_too else [], name=name, compiler_params=_params())(*ins)
    return result if sorted_too else result[0]


def _rms_bwd(name, x, g, dys, res=None, out_dtype=F32, groups=None, deps=()):
    n_tiles = SEQ // TM
    n_dy = len(dys) if groups is None else 1 + len(SORTED)
    has_res = res is not None

    def body(*refs):
        x_ref, g_ref = refs[0], refs[1]
        dy_refs = refs[2:2 + n_dy]
        res_ref = refs[2 + n_dy] if has_res else None
        first_out = 2 + n_dy + has_res + len(deps)
        dx_ref, dg_ref, acc_ref = refs[first_out:first_out + 3]
        i = pl.program_id(0)
        xv = x_ref[...]
        r = lax.rsqrt(jnp.mean(xv * xv, axis=-1, keepdims=True) + RMS_EPS)
        xn = xv * r
        dy = dy_refs[0][...].astype(F32)
        for j, extra in enumerate(dy_refs[1:]):
            dy = dy + (extra[...].astype(F32) if groups is None else _unsort_tile(refs[-1], extra, SORTED[j]))
        dyg = dy * g_ref[...]
        dx = r * (dyg - xn * jnp.mean(dyg * xn, axis=-1, keepdims=True))
        if has_res:
            dx = res_ref[...] + dx
        dx_ref[...] = dx.astype(dx_ref.dtype)
        part = jnp.sum((dy * xn).reshape(TM // 8, 8, D_MODEL), axis=0)

        @pl.when(i == 0)
        def _():
            acc_ref[...] = part

        @pl.when(i > 0)
        def _():
            acc_ref[...] += part

        @pl.when(i == n_tiles - 1)
        def _():
            dg_ref[...] = jnp.broadcast_to(jnp.sum(acc_ref[...], axis=0, keepdims=True), (8, D_MODEL))

    tile = pl.BlockSpec((TM, D_MODEL), lambda i: (i, 0))
    gspec = pl.BlockSpec((1, D_MODEL), lambda i: (0, 0))
    if groups is None:
        dy_ins, dy_specs = list(dys), [tile] * n_dy
    else:
        dy_ins = [groups] + [groups.reshape(n_dy, dil, SEQ // dil, D_MODEL) for dil in SORTED]
        dy_specs = [pl.BlockSpec((None, TM, D_MODEL), lambda i: (0, i, 0))]
        dy_specs += [_sorted_spec(dil, lead=(1 + j,)) for j, dil in enumerate(SORTED)]
    ins = [x, g] + dy_ins + ([res] if has_res else []) + list(deps)
    specs = [tile, gspec] + dy_specs + ([tile] if has_res else []) + [ANY] * len(deps)
    dx, dg = pl.pallas_call(
        body, out_shape=(jax.ShapeDtypeStruct((SEQ, D_MODEL), out_dtype), jax.ShapeDtypeStruct((8, D_MODEL), F32)),
        grid=(n_tiles,), in_specs=specs,
        out_specs=(tile, pl.BlockSpec((8, D_MODEL), lambda i: (0, 0))),
        scratch_shapes=[pltpu.VMEM((8, D_MODEL), F32)] + ([_sort_scratch()] if groups is not None else []),
        name=name, compiler_params=_params())(*ins)
    return dx, dg[0:1]


def _loss_head(name, y, target):
    n_tiles = SEQ // TM

    def body(y_ref, t_ref, dy_ref, loss_ref, acc_ref):
        i = pl.program_id(0)
        diff = y_ref[...] - t_ref[...]
        dy_ref[...] = diff * (1.0 / D_MODEL)
        part = jnp.sum((diff * diff).reshape(TM // 8, 8, D_MODEL), axis=0)

        @pl.when(i == 0)
        def _():
            acc_ref[...] = part

        @pl.when(i > 0)
        def _():
            acc_ref[...] += part

        @pl.when(i == n_tiles - 1)
        def _():
            loss_ref[...] = jnp.full((8, 128), jnp.sum(acc_ref[...]) * (0.5 / D_MODEL), F32)

    tile = pl.BlockSpec((TM, D_MODEL), lambda i: (i, 0))
    dy, loss = pl.pallas_call(
        body, out_shape=(jax.ShapeDtypeStruct((SEQ, D_MODEL), F32), jax.ShapeDtypeStruct((8, 128), F32)),
        grid=(n_tiles,), in_specs=[tile, tile], out_specs=(tile, pl.BlockSpec((8, 128), lambda i: (0, 0))),
        scratch_shapes=[pltpu.VMEM((8, D_MODEL), F32)], name=name, compiler_params=_params())(y, target)
    return dy, loss[0, 0]


def _row_index(shape):
    return lax.broadcasted_iota(jnp.int32, shape, 0)


def _lane_index(shape):
    return lax.broadcasted_iota(jnp.int32, shape, len(shape) - 1)


def _skew_rows(t, direction):
    q = _row_index(t.shape) & (GRID_W - 1)
    for bit in range(6):
        step = 1 << bit
        shift = step if direction > 0 else 128 - step
        t = jnp.where((q & step) != 0, pltpu.roll(t, shift, 1), t)
    return t


def _rpb_table(rpb_pad):
    rows = 16 * GRID_W

    def body(r_ref, t_ref):
        lane = _lane_index((rows, 128))
        v = pltpu.roll(r_ref[...], 128 - 15, 1)
        t = _skew_rows(jnp.broadcast_to(v[:, None, :], (16, GRID_W, 128)).reshape(rows, 128), +1)
        t = jnp.where(lane < GRID_W, t, 0.0)
        below = jnp.concatenate([t[GRID_W:], jnp.zeros((GRID_W, 128), F32)], axis=0)
        first_col = jnp.clip((_row_index((rows, 128)) & (GRID_W - 1)) - 8, 0, GRID_W - 16)
        key_col = lane & (GRID_W - 1)
        in_window = (key_col >= first_col) & (key_col < first_col + 16)
        t_ref[...] = jnp.where(in_window, t + pltpu.roll(below, GRID_W, 1), NEG_INF).reshape(16, GRID_W, 128)

    return pl.pallas_call(
        body, out_shape=jax.ShapeDtypeStruct((N_HEADS, 16, GRID_W, 128), F32), grid=(N_HEADS,),
        in_specs=[pl.BlockSpec((None, 16, 128), lambda h: (h, 0, 0))],
        out_specs=pl.BlockSpec((None, 16, GRID_W, 128), lambda h: (h, 0, 0, 0)),
        name="rpb_table", compiler_params=_params())(rpb_pad)


def _rpb_grad(gp):
    rows = 16 * GRID_W

    def body(g_ref, o_ref):
        lane = _lane_index((rows, 128))
        g = g_ref[...].reshape(rows, 128)
        low = jnp.where(lane < GRID_W, g, 0.0)
        high = pltpu.roll(jnp.where(lane >= GRID_W, g, 0.0), GRID_W, 1)
        above = jnp.concatenate([jnp.zeros((GRID_W, 128), F32), high[:rows - GRID_W]], axis=0)
        diag = jnp.sum(_skew_rows(low + above, -1).reshape(16, GRID_W, 128), axis=1)
        o_ref[...] = pltpu.roll(diag, 15, 1)

    return pl.pallas_call(
        body, out_shape=jax.ShapeDtypeStruct((N_HEADS, 16, 128), F32), grid=(N_HEADS,),
        in_specs=[pl.BlockSpec((None, 16, GRID_W, 128), lambda h: (h, 0, 0, 0))],
        out_specs=pl.BlockSpec((None, 16, 128), lambda h: (h, 0, 0)),
        name="rpb_grad", compiler_params=_params())(gp)


NA_KEYS = NA_ROWS * GRID_W


def _na_window(i):
    first_row = jnp.clip(i - NA_ROWS // 2, 0, SEQ_ROWS - NA_ROWS)
    return pl.multiple_of(first_row * GRID_W, GRID_W), first_row - i + NA_ROWS - 1


NA_STEP = 8


def _head_masks():
    lane = _lane_index((1, 128))
    return (lane < HEAD_DIM, lane >= HEAD_DIM)


def _stack_heads(t, masks):
    zero = jnp.zeros_like(t)
    return jnp.concatenate([jnp.where(masks[0], t, zero), jnp.where(masks[1], t, zero)], axis=0)


def _unstack_heads(t, masks):
    n = t.shape[0] // 2
    return jnp.where(masks[0], t[:n], t[n:])


def _stack_columns(t):
    return jnp.concatenate([t[:, 0:1], t[:, HEAD_DIM:HEAD_DIM + 1]], axis=0)


def _na_scores(qs, kw, tp_ref, dr0):
    s = lax.dot_general(qs, kw, NT, preferred_element_type=F32)
    bias = jnp.concatenate(
        [jnp.concatenate([tp_ref[a, pl.ds(dr0 + 2 * c, 1), :, :].reshape(GRID_W, 128) for c in range(4)], axis=1)
         for a in range(2)], axis=0)
    return s + bias


def _na_specs():
    q_spec = pl.BlockSpec((NA_STEP * GRID_W, 128), lambda hp, i: (i, hp))
    k_spec = pl.BlockSpec((SEQ, 128), lambda hp, i: (0, 8 + hp))
    v_spec = pl.BlockSpec((SEQ, 128), lambda hp, i: (0, 16 + hp))
    tp_spec = pl.BlockSpec((2, 16, GRID_W, 128), lambda hp, i: (hp, 0, 0, 0))
    return q_spec, k_spec, v_spec, tp_spec


def _na_fwd(qkv, table):
    def body(q_ref, k_ref, v_ref, tp_ref, o_ref, lse_ref):
        masks = _head_masks()
        for r in range(NA_STEP):
            rows = slice(r * GRID_W, (r + 1) * GRID_W)
            start, dr0 = _na_window(pl.program_id(1) * NA_STEP + r)
            kw = k_ref[pl.ds(start, NA_KEYS), :]
            vw = v_ref[pl.ds(start, NA_KEYS), :]
            s = _na_scores(_stack_heads(q_ref[rows, :] * Q_SCALE, masks), kw, tp_ref, dr0)
            m = jnp.max(s, axis=-1, keepdims=True)
            p = jnp.exp(s - m)
            denom = jnp.sum(p, axis=-1, keepdims=True)
            out = jnp.dot(p.astype(BF16), vw, preferred_element_type=F32) / denom
            o_ref[rows, :] = _unstack_heads(out, masks).astype(o_ref.dtype)
            lse_ref[rows, :] = _unstack_heads(jnp.broadcast_to(m + jnp.log(denom), (2 * GRID_W, 128)), masks)

    q_spec, k_spec, v_spec, tp_spec = _na_specs()
    return pl.pallas_call(
        body, out_shape=(jax.ShapeDtypeStruct((SEQ, D_MODEL), BF16), jax.ShapeDtypeStruct((SEQ, D_MODEL), F32)),
        grid=(N_HEADS // 2, SEQ_ROWS // NA_STEP), in_specs=[q_spec, k_spec, v_spec, tp_spec],
        out_specs=(q_spec, q_spec), name="na_fwd", compiler_params=_params())(qkv, qkv, qkv, table)


def _na_bwd(qkv, table, d_out, lse):
    def body(q_ref, k_ref, v_ref, tp_ref, do_ref, lse_ref, dqkv_ref, gp_ref, dk_acc, dv_acc):
        step = pl.program_id(1)

        @pl.when(step == 0)
        def _():
            dk_acc[...] = jnp.zeros_like(dk_acc)
            dv_acc[...] = jnp.zeros_like(dv_acc)
            gp_ref[...] = jnp.zeros_like(gp_ref)

        masks = _head_masks()
        for r in range(NA_STEP):
            rows = slice(r * GRID_W, (r + 1) * GRID_W)
            i = step * NA_STEP + r
            start, dr0 = _na_window(i)
            kw = k_ref[pl.ds(start, NA_KEYS), :]
            vw = v_ref[pl.ds(start, NA_KEYS), :]
            qs = _stack_heads(q_ref[rows, :] * Q_SCALE, masks)
            dos = _stack_heads(do_ref[rows, :], masks)
            p = jnp.exp(_na_scores(qs, kw, tp_ref, dr0) - _stack_columns(lse_ref[rows, :]))
            dp = lax.dot_general(dos, vw, NT, preferred_element_type=F32)
            ds = p * (dp - jnp.sum(p * dp, axis=-1, keepdims=True))
            for a in range(2):
                for c in range(4):
                    gp_ref[a, pl.ds(dr0 + 2 * c, 1), :, :] += (
                        ds[a * GRID_W:(a + 1) * GRID_W, 128 * c:128 * (c + 1)].reshape(1, GRID_W, 128))
            dsb = ds.astype(BF16)
            dq = _unstack_heads(jnp.dot(dsb, kw, preferred_element_type=F32), masks) * Q_SCALE
            dqkv_ref[0, pl.ds(pl.multiple_of(i * GRID_W, GRID_W), GRID_W), :] = dq.astype(dqkv_ref.dtype)
            dk_acc[pl.ds(start, NA_KEYS), :] += lax.dot_general(dsb, qs, TN, preferred_element_type=F32)
            dv_acc[pl.ds(start, NA_KEYS), :] += lax.dot_general(p.astype(BF16), dos, TN, preferred_element_type=F32)

        @pl.when(step == SEQ_ROWS // NA_STEP - 1)
        def _():
            dqkv_ref[1] = dk_acc[...].astype(dqkv_ref.dtype)
            dqkv_ref[2] = dv_acc[...].astype(dqkv_ref.dtype)

    q_spec, k_spec, v_spec, tp_spec = _na_specs()
    return pl.pallas_call(
        body,
        out_shape=(jax.ShapeDtypeStruct((3, SEQ, D_MODEL), BF16), jax.ShapeDtypeStruct((N_HEADS, 16, GRID_W, 128), F32)),
        grid=(N_HEADS // 2, SEQ_ROWS // NA_STEP), in_specs=[q_spec, k_spec, v_spec, tp_spec, q_spec, q_spec],
        out_specs=(pl.BlockSpec((3, SEQ, 128), lambda hp, i: (0, 0, hp)), tp_spec),
        scratch_shapes=[pltpu.VMEM((SEQ, 128), F32), pltpu.VMEM((SEQ, 128), F32)],
        name="na_bwd", compiler_params=_params())(qkv, qkv, qkv, table, d_out, lse)


DIL_STEP = 8


def _dil_geometry(group):
    dil = DIL_GROUPS[group][1]
    sub_len = SEQ // dil
    blocks = sub_len // BAND
    return dil, sub_len, max(blocks // DIL_STEP, 1), max(DIL_STEP // blocks, 1), min(2 * BAND, sub_len)


def _dil_block(step, r, sub_len, subs):
    per_sub = DIL_STEP // subs
    return (r // per_sub) * sub_len, step * per_sub + r % per_sub


def _dil_window(b, sub_len, n_keys):
    if n_keys == sub_len:
        return 0
    return pl.multiple_of(jnp.clip(b * BAND - RADIUS, 0, sub_len - n_keys), RADIUS)


def _dil_bias(b, start, n_keys, slope_ref, hp):
    row = _row_index((2 * BAND, n_keys))
    qpos = b * BAND + (row & (BAND - 1))
    kpos = start + _lane_index((2 * BAND, n_keys))
    dist = jnp.abs(qpos - kpos)
    slope = jnp.where(row < BAND, slope_ref[2 * hp], slope_ref[2 * hp + 1])
    return slope * dist.astype(F32), dist <= RADIUS


def _dil_scores(qs, kw, penalty, valid):
    return jnp.where(valid, lax.dot_general(qs, kw, NT, preferred_element_type=F32) - penalty, NEG_INF)


def _dil_specs(group):
    dil, sub_len, steps, subs, _ = _dil_geometry(group)
    col = group * 24
    rows = DIL_STEP * BAND
    q_spec = pl.BlockSpec((rows, 128), lambda n, hp, b: (n * steps + b, col + hp))
    k_spec = pl.BlockSpec((subs * sub_len, 128), lambda n, hp, b: (n, col + 8 + hp))
    v_spec = pl.BlockSpec((subs * sub_len, 128), lambda n, hp, b: (n, col + 16 + hp))
    tile = pl.BlockSpec((rows, 128), lambda n, hp, b: (n * steps + b, hp))
    smem = pl.BlockSpec(memory_space=pltpu.SMEM)
    return (dil // subs, N_HEADS // 2, steps), q_spec, k_spec, v_spec, tile, smem


def _dil_fwd(group, qkv, slopes):
    _, sub_len, _, subs, n_keys = _dil_geometry(group)

    def body(q_ref, k_ref, v_ref, slope_ref, o_ref, lse_ref):
        hp = pl.program_id(1)
        masks = _head_masks()
        for r in range(DIL_STEP):
            rows = slice(r * BAND, (r + 1) * BAND)
            base, b = _dil_block(pl.program_id(2), r, sub_len, subs)
            start = _dil_window(b, sub_len, n_keys)
            kw = k_ref[pl.ds(base + start, n_keys), :]
            vw = v_ref[pl.ds(base + start, n_keys), :]
            penalty, valid = _dil_bias(b, start, n_keys, slope_ref, hp)
            s = _dil_scores(_stack_heads(q_ref[rows, :] * Q_SCALE, masks), kw, penalty, valid)
            m = jnp.max(s, axis=-1, keepdims=True)
            p = jnp.exp(s - m)
            denom = jnp.sum(p, axis=-1, keepdims=True)
            out = jnp.dot(p.astype(BF16), vw, preferred_element_type=F32) / denom
            o_ref[rows, :] = _unstack_heads(out, masks).astype(o_ref.dtype)
            lse_ref[rows, :] = _unstack_heads(jnp.broadcast_to(m + jnp.log(denom), (2 * BAND, 128)), masks)

    grid, q_spec, k_spec, v_spec, tile, smem = _dil_specs(group)
    return pl.pallas_call(
        body, out_shape=(jax.ShapeDtypeStruct((SEQ, D_MODEL), BF16), jax.ShapeDtypeStruct((SEQ, D_MODEL), F32)),
        grid=grid, in_specs=[q_spec, k_spec, v_spec, smem], out_specs=(tile, tile),
        name=f"dil_fwd_{group}", compiler_params=_params())(qkv, qkv, qkv, slopes)


def _dil_merge(outs, lses):
    n_sorted = len(SORTED)

    def body(*refs):
        o_refs, l_refs = refs[:3], refs[3:6]
        out_refs, lse_refs, scratch = refs[6:7 + n_sorted], refs[7 + n_sorted:8 + 2 * n_sorted], refs[-1]
        os_ = [o_refs[0][...]] + [_unsort_tile(scratch, o_refs[1 + j], dil) for j, dil in enumerate(SORTED)]
        ls = [l_refs[0][...]] + [_unsort_tile(scratch, l_refs[1 + j], dil) for j, dil in enumerate(SORTED)]
        m = jnp.maximum(jnp.maximum(ls[0], ls[1]), ls[2])
        es = [jnp.exp(v - m) for v in ls]
        total = es[0] + es[1] + es[2]
        merged = (es[0] * os_[0] + es[1] * os_[1] + es[2] * os_[2]) / total
        lse = m + jnp.log(total)
        out_refs[0][...] = merged
        lse_refs[0][...] = lse
        for j, dil in enumerate(SORTED):
            _sort_tile(scratch, merged, dil, out_refs[1 + j])
            _sort_tile(scratch, lse, dil, lse_refs[1 + j])

    tm = 256
    tile = pl.BlockSpec((tm, D_MODEL), lambda i: (i, 0))
    specs = [tile] + [_sorted_spec(dil, tm=tm) for dil in SORTED]
    shapes = [jax.ShapeDtypeStruct((SEQ, D_MODEL), F32)] + [jax.ShapeDtypeStruct((dil, SEQ // dil, D_MODEL), F32) for dil in SORTED]
    views = lambda ts: [ts[0]] + [_sorted_view(t, dil) for t, dil in zip(ts[1:], SORTED)]
    result = pl.pallas_call(
        body, out_shape=tuple(shapes * 2), grid=(SEQ // tm,), in_specs=specs * 2, out_specs=tuple(specs * 2),
        scratch_shapes=[_sort_scratch(tm)], name="dil_merge", compiler_params=_params())(*views(outs), *views(lses))
    flat = [t.reshape(SEQ, D_MODEL) for t in result]
    return flat[:1 + n_sorted], flat[1 + n_sorted:]


def _dil_bwd(group, qkv, slopes, d_out, out, lse_group, lse_total, into):
    _, sub_len, steps, subs, n_keys = _dil_geometry(group)

    def body(q_ref, k_ref, v_ref, slope_ref, do_ref, o_ref, lg_ref, lt_ref, into_ref, dqkv_ref, dk_acc, dv_acc):
        hp, step = pl.program_id(1), pl.program_id(2)

        @pl.when(step == 0)
        def _():
            dk_acc[...] = jnp.zeros_like(dk_acc)
            dv_acc[...] = jnp.zeros_like(dv_acc)

        masks = _head_masks()
        for r in range(DIL_STEP):
            rows = slice(r * BAND, (r + 1) * BAND)
            base, b = _dil_block(step, r, sub_len, subs)
            start = _dil_window(b, sub_len, n_keys)
            keys = pl.ds(base + start, n_keys)
            kw = k_ref[keys, :]
            vw = v_ref[keys, :]
            penalty, valid = _dil_bias(b, start, n_keys, slope_ref, hp)
            qs = _stack_heads(q_ref[rows, :] * Q_SCALE, masks)
            lse2 = lg_ref[rows, :]
            weight = jnp.exp(lse2 - lt_ref[rows, :])
            do2 = do_ref[rows, :]
            dogs = _stack_heads((weight * do2).astype(BF16), masks)
            delta = _stack_columns(weight) * jnp.sum(_stack_heads(do2 * o_ref[rows, :], masks), axis=-1, keepdims=True)
            p = jnp.exp(_dil_scores(qs, kw, penalty, valid) - _stack_columns(lse2))
            dp = lax.dot_general(dogs, vw, NT, preferred_element_type=F32)
            dsb = (p * (dp - delta)).astype(BF16)
            dq = _unstack_heads(jnp.dot(dsb, kw, preferred_element_type=F32), masks) * Q_SCALE
            dqkv_ref[0, pl.ds(pl.multiple_of(base + b * BAND, BAND), BAND), :] = dq.astype(dqkv_ref.dtype)
            dk_acc[keys, :] += lax.dot_general(dsb, qs, TN, preferred_element_type=F32)
            dv_acc[keys, :] += lax.dot_general(p.astype(BF16), dogs, TN, preferred_element_type=F32)

        @pl.when(step == steps - 1)
        def _():
            dqkv_ref[1] = dk_acc[...].astype(dqkv_ref.dtype)
            dqkv_ref[2] = dv_acc[...].astype(dqkv_ref.dtype)

    grid, q_spec, k_spec, v_spec, tile, smem = _dil_specs(group)
    return pl.pallas_call(
        body, out_shape=jax.ShapeDtypeStruct(into.shape, into.dtype), grid=grid,
        in_specs=[q_spec, k_spec, v_spec, smem, tile, tile, tile, tile, ANY],
        out_specs=pl.BlockSpec((3, subs * sub_len, 128), lambda n, hp, b: (group, n, hp)),
        scratch_shapes=[pltpu.VMEM((subs * sub_len, 128), F32), pltpu.VMEM((subs * sub_len, 128), F32)],
        input_output_aliases={8: 0}, name=f"dil_bwd_{group}", compiler_params=_params(),
    )(qkv, qkv, qkv, slopes, d_out, out, lse_group, lse_total, into)


def _accumulate_rows(acc_ref, i, first, part):
    rows = pl.ds(pl.multiple_of(i * TM_MM, TM_MM), TM_MM)

    @pl.when(first)
    def _():
        acc_ref[rows, :] = part

    @pl.when(jnp.logical_not(first))
    def _():
        acc_ref[rows, :] += part


def _ffn_specs():
    tile = pl.BlockSpec((TM_MM, D_MODEL), lambda d, i: (i, 0))
    gate = pl.BlockSpec((None, None, D_MODEL, FF_PAD), lambda d, i: (d, 0, 0, 0))
    up = pl.BlockSpec((None, None, D_MODEL, FF_PAD), lambda d, i: (d, 1, 0, 0))
    down = pl.BlockSpec((None, FF_PAD, D_MODEL), lambda d, i: (d, 0, 0))
    hidden = pl.BlockSpec((None, TM_MM, FF_PAD), lambda d, i: (d, i, 0))
    whole = pl.BlockSpec((SEQ, D_MODEL), lambda d, i: (0, 0))
    return tile, gate, up, down, hidden, whole


def _ffn_fwd(name, h, w_gu, w_down):
    def body(h_ref, wg_ref, wu_ref, wd_ref, f_ref, hg_ref, hu_ref, act_t_ref):
        hv = h_ref[...]
        hg = jnp.dot(hv, wg_ref[...], preferred_element_type=F32)
        hu = jnp.dot(hv, wu_ref[...], preferred_element_type=F32)
        act = hg * jax.nn.sigmoid(hg) * hu
        act_t_ref[...] = act.T.astype(act_t_ref.dtype)
        hg_ref[...] = hg.astype(hg_ref.dtype)
        hu_ref[...] = hu.astype(hu_ref.dtype)
        _accumulate_rows(f_ref, pl.program_id(1), pl.program_id(0) == 0,
                         jnp.dot(act.astype(BF16), wd_ref[...], preferred_element_type=F32))

    tile, gate, up, down, hidden, whole = _ffn_specs()
    shape = jax.ShapeDtypeStruct((N_DEV, SEQ, FF_PAD), BF16)
    return pl.pallas_call(
        body, out_shape=(jax.ShapeDtypeStruct((SEQ, D_MODEL), F32), shape, shape, jax.ShapeDtypeStruct((N_DEV, FF_PAD, SEQ), BF16)),
        grid=(N_DEV, SEQ // TM_MM), in_specs=[tile, gate, up, down],
        out_specs=(whole, hidden, hidden, pl.BlockSpec((None, FF_PAD, TM_MM), lambda d, i: (d, 0, i))),
        name=name, compiler_params=_params())(h, w_gu, w_gu, w_down)


def _ffn_dgu(name, h_t, dgu):
    def body(h_ref, dgu_ref, o_ref):
        both = jnp.dot(h_ref[...], jnp.concatenate([dgu_ref[0], dgu_ref[1]], axis=1), preferred_element_type=F32)
        o_ref[0] = both[:, :FF_PAD].astype(o_ref.dtype)
        o_ref[1] = both[:, FF_PAD:].astype(o_ref.dtype)

    return pl.pallas_call(
        body, out_shape=jax.ShapeDtypeStruct((N_DEV, 2, D_MODEL, FF_PAD), BF16), grid=(N_DEV,),
        in_specs=[pl.BlockSpec((D_MODEL, SEQ), lambda d: (0, 0)), pl.BlockSpec((2, None, SEQ, FF_PAD), lambda d: (0, d, 0, 0))],
        out_specs=pl.BlockSpec((None, 2, D_MODEL, FF_PAD), lambda d: (d, 0, 0, 0)),
        name=name, compiler_params=_params())(h_t, dgu)


def _ffn_bwd(name, d_f, w_gu, w_down, hg, hu):
    def body(df_ref, wg_ref, wu_ref, wd_ref, hg_ref, hu_ref, dgu_ref, dh_ref):
        dact = lax.dot_general(df_ref[...], wd_ref[...], NT, preferred_element_type=F32)
        hgv = hg_ref[...].astype(F32)
        sig = jax.nn.sigmoid(hgv)
        d_gate = (dact * hu_ref[...].astype(F32) * (sig * (1.0 + hgv * (1.0 - sig)))).astype(BF16)
        d_up = (dact * hgv * sig).astype(BF16)
        dgu_ref[0] = d_gate
        dgu_ref[1] = d_up
        part = lax.dot_general(jnp.concatenate([d_gate, d_up], axis=1), jnp.concatenate([wg_ref[...], wu_ref[...]], axis=1),
                               NT, preferred_element_type=F32)
        _accumulate_rows(dh_ref, pl.program_id(1), pl.program_id(0) == 0, part)

    tile, gate, up, down, hidden, whole = _ffn_specs()
    return pl.pallas_call(
        body, out_shape=(jax.ShapeDtypeStruct((2, N_DEV, SEQ, FF_PAD), BF16), jax.ShapeDtypeStruct((SEQ, D_MODEL), F32)),
        grid=(N_DEV, SEQ // TM_MM), in_specs=[tile, gate, up, down, hidden, hidden],
        out_specs=(pl.BlockSpec((2, None, TM_MM, FF_PAD), lambda d, i: (0, d, i, 0)), whole),
        name=name, compiler_params=_params())(d_f, w_gu, w_gu, w_down, hg, hu)


def _position():
    return lax.axis_index("x"), lax.axis_index("y"), lax.axis_index("c")


def _flat(p):
    return 4 * p[0] + 2 * p[1] + p[2]


def _peer(me, k):
    x, y, c = me
    return (1 - x if k & 4 else x, 1 - y if k & 2 else y, 1 - c if k & 1 else c)


def _columns(width):
    return lambda ref, d: ref.at[:, pl.ds(pl.multiple_of(d * width, 128), width)]


def _leading(ref, d):
    return ref.at[d]


def _whole(ref, d):
    return ref


def _by_sender(window):
    return lambda ref, sender, k: window(ref, sender)


def _by_distance(ref, sender, k):
    return ref.at[k - 1]


def _prep_weights(me, na_qkv, na_o, dil_qkv, dil_o, gate, up, down, land_shapes):
    na_cols, dil_cols = na_qkv.shape[-1], dil_qkv.shape[-1]
    o_rows = na_o.shape[1]
    tiles = 4
    rows, rows_o = D_MODEL // tiles, o_rows // tiles

    def body(me_ref, naq, nao, dq, do_, g0, u0, d0, g1, u1, d1, *outs):
        def put(t, index, value):
            outs[t][index] = value
            outs[8 + t][index] = value

        put(0, ..., naq[...].astype(BF16))
        put(1, ..., nao[...].astype(BF16))
        put(4, ..., dq[...].astype(BF16))
        put(5, ..., do_[...].astype(BF16))
        for t, (g, u, d) in ((2, (g0, u0, d0)), (6, (g1, u1, d1))):
            for j, part in enumerate((g, u)):
                put(t, (j, slice(None), slice(0, FF_SHARD)), part[...].astype(BF16))
                put(t, (j, slice(None), slice(FF_SHARD, FF_PAD)), jnp.zeros((rows, FF_PAD - FF_SHARD), BF16))
            put(t + 1, (slice(0, FF_SHARD), slice(None)), d[...].astype(BF16))
            put(t + 1, (slice(FF_SHARD, FF_PAD), slice(None)), jnp.zeros((FF_PAD - FF_SHARD, D_MODEL), BF16))

    def tiled(width):
        return pl.BlockSpec((None, rows, width), lambda i, me: (0, i, 0))

    def layer(l, width):
        return pl.BlockSpec((None, rows, width), lambda i, me: (l, i, 0))

    def whole_layer(l):
        return pl.BlockSpec((None, FF_SHARD, D_MODEL), lambda i, me: (l, 0, 0))

    in_specs = [tiled(na_cols), pl.BlockSpec((None, rows_o, D_MODEL), lambda i, me: (0, i, 0)), tiled(dil_cols),
                pl.BlockSpec((None, rows_o, D_MODEL), lambda i, me: (0, i, 0)),
                layer(0, FF_SHARD), layer(0, FF_SHARD), whole_layer(0), layer(1, FF_SHARD), layer(1, FF_SHARD), whole_layer(1)]
    o_shard = pl.BlockSpec((rows_o, D_MODEL), lambda i, me: (i, 0))
    o_land = pl.BlockSpec((None, rows_o, D_MODEL), lambda i, me: (me[0], i, 0))
    gu_shard = pl.BlockSpec((2, rows, FF_PAD), lambda i, me: (0, i, 0))
    gu_land = pl.BlockSpec((None, 2, rows, FF_PAD), lambda i, me: (me[0], 0, i, 0))
    down_shard = pl.BlockSpec((FF_PAD, D_MODEL), lambda i, me: (0, 0))
    down_land = pl.BlockSpec((None, FF_PAD, D_MODEL), lambda i, me: (me[0], 0, 0))

    def qkv_shard(width):
        return pl.BlockSpec((rows, width), lambda i, me: (i, 0))

    def qkv_land(width):
        return pl.BlockSpec((rows, width), lambda i, me: (i, me[0]))

    shard_specs = [qkv_shard(na_cols), o_shard, gu_shard, down_shard, qkv_shard(dil_cols), o_shard, gu_shard, down_shard]
    land_specs = [qkv_land(na_cols), o_land, gu_land, down_land, qkv_land(dil_cols), o_land, gu_land, down_land]
    shard_shapes = [jax.ShapeDtypeStruct(s, BF16) for s in
                    ((D_MODEL, na_cols), (o_rows, D_MODEL), (2, D_MODEL, FF_PAD), (FF_PAD, D_MODEL),
                     (D_MODEL, dil_cols), (o_rows, D_MODEL), (2, D_MODEL, FF_PAD), (FF_PAD, D_MODEL))]
    result = pl.pallas_call(
        body, out_shape=tuple(shard_shapes + list(land_shapes)),
        grid_spec=pltpu.PrefetchScalarGridSpec(num_scalar_prefetch=1, grid=(tiles,), in_specs=in_specs,
                                               out_specs=tuple(shard_specs + land_specs)),
        name="prep_weights", compiler_params=_params())(me, na_qkv, na_o, dil_qkv, dil_o, gate, up, down, gate, up, down)
    return list(result[:8]), list(result[8:])


def _remote_copies(sets, src_refs, land_refs, send_sems, recv_sems, outgoing):
    me = _position()
    copies = []
    for t, (si, src_of, li, dst_of) in enumerate(sets):
        for k in range(1, N_DEV):
            other = _peer(me, k)
            sender = me if outgoing else other
            copies.append(pltpu.make_async_remote_copy(
                src_ref=src_of(src_refs[si], _flat(other)), dst_ref=dst_of(land_refs[li], _flat(sender), k),
                send_sem=send_sems.at[(N_DEV - 1) * t + k - 1], recv_sem=recv_sems.at[(N_DEV - 1) * t + k - 1],
                device_id=other, device_id_type=MESH))
    return copies


def _send_start(name, srcs, lands, sets_by_group):
    n_src, n_land, n_groups = len(srcs), len(lands), len(sets_by_group)

    def body(*refs):
        src_refs, land_refs = refs[:n_src], refs[n_src:n_src + n_land]
        outs = refs[n_src + n_land:]
        for g, sets in enumerate(sets_by_group):
            for cp in _remote_copies(sets, src_refs, land_refs, outs[2 * g], outs[2 * g + 1], True):
                cp.start()
        outs[-1][...] = jnp.zeros_like(outs[-1])

    sem_shapes = []
    for sets in sets_by_group:
        sem_shapes += [pltpu.SemaphoreType.DMA((len(sets) * (N_DEV - 1),))] * 2
    thru = [pltpu.HBM(a.shape, a.dtype) for a in list(srcs) + list(lands)]
    n_sem = len(sem_shapes)
    result = pl.pallas_call(
        body, out_shape=tuple(sem_shapes + thru + [jax.ShapeDtypeStruct((8, 128), F32)]),
        in_specs=[HBM] * (n_src + n_land),
        out_specs=tuple([SEM] * n_sem + [HBM] * (n_src + n_land) + [pl.BlockSpec(memory_space=pltpu.VMEM)]),
        input_output_aliases={i: n_sem + i for i in range(n_src + n_land)},
        compiler_params=pltpu.CompilerParams(has_side_effects=EFFECT), name=name,
    )(*[pltpu.with_memory_space_constraint(a, pltpu.HBM) for a in list(srcs) + list(lands)])
    sems = [(result[2 * g], result[2 * g + 1]) for g in range(n_groups)]
    return sems, list(result[n_sem:n_sem + n_src]), list(result[n_sem + n_src:n_sem + n_src + n_land]), result[-1]


def _send_wait(name, sems, srcs, lands, sets, after):
    n_src, n_land = len(srcs), len(lands)

    def body(*refs):
        src_refs, land_refs = refs[:n_src], refs[n_src:n_src + n_land]
        send_sems, recv_sems = refs[n_src + n_land], refs[n_src + n_land + 1]
        for cp in _remote_copies(sets, src_refs, land_refs, send_sems, recv_sems, True):
            cp.wait_send()
        for cp in _remote_copies(sets, src_refs, land_refs, send_sems, recv_sems, False):
            cp.wait_recv()

    thru = [pltpu.HBM(a.shape, a.dtype) for a in list(srcs) + list(lands)]
    result = pl.pallas_call(
        body, out_shape=tuple(thru), in_specs=[HBM] * (n_src + n_land) + [SEM, SEM] + [ANY] * len(after),
        out_specs=tuple([HBM] * (n_src + n_land)), input_output_aliases={i: i for i in range(n_src + n_land)},
        compiler_params=pltpu.CompilerParams(has_side_effects=EFFECT), name=name,
    )(*srcs, *lands, sems[0], sems[1], *after)
    return list(result[:n_src]), list(result[n_src:])


DIRECT = (1, 2, 4, 6)
PASSED = DIRECT[1:]


def _hbm_passthrough(body, name, arrays, n_sem_in, sem_out_shapes, extra):
    n, n_out = len(arrays), len(sem_out_shapes)
    return pl.pallas_call(
        body, out_shape=tuple(list(sem_out_shapes) + [pltpu.HBM(a.shape, a.dtype) for a in arrays]),
        in_specs=[HBM] * n + [SEM] * n_sem_in + [ANY] * len(extra), out_specs=tuple([SEM] * n_out + [HBM] * n),
        input_output_aliases={i: n_out + i for i in range(n)},
        compiler_params=pltpu.CompilerParams(has_side_effects=EFFECT), name=name)


def _shard_copy(src_ref, land_ref, window, block, to, send_sem, recv_sem, from_shard):
    dst = window(land_ref, _flat(block))
    return pltpu.make_async_remote_copy(src_ref=src_ref if from_shard else dst, dst_ref=dst, send_sem=send_sem,
                                        recv_sem=recv_sem, device_id=to, device_id_type=MESH)


def _gather_start(name, shards, lands, windows, group_sizes):
    n = len(shards)

    def body(*refs):
        shard_refs, land_refs, outs = refs[:n], refs[n:2 * n], refs[2 * n:]
        me = _position()
        t = 0
        for g, size in enumerate(group_sizes):
            for local in range(size):
                for j, k in enumerate(DIRECT):
                    i = len(DIRECT) * local + j
                    _shard_copy(shard_refs[t], land_refs[t], windows[t], me, _peer(me, k), outs[2 * g].at[i],
                                outs[2 * g + 1].at[i], True).start()
                t += 1

    sem_shapes = [pltpu.SemaphoreType.DMA((len(DIRECT) * size,)) for size in group_sizes for _ in range(2)]
    arrays = [pltpu.with_memory_space_constraint(a, pltpu.HBM) for a in list(shards) + list(lands)]
    result = _hbm_passthrough(body, name, arrays, 0, sem_shapes, ())(*arrays)
    n_sem = len(sem_shapes)
    sems = [(result[2 * g], result[2 * g + 1]) for g in range(len(group_sizes))]
    return sems, list(result[n_sem:n_sem + n]), list(result[n_sem + n:])


def _gather_pass_on(name, sems, shards, lands, windows, after):
    n = len(shards)

    def body(*refs):
        shard_refs, land_refs = refs[:n], refs[n:2 * n]
        recv_sems = refs[2 * n + 1]
        pass_send, pass_recv = refs[2 * n + 2 + len(after)], refs[2 * n + 3 + len(after)]
        me = _position()
        sibling = _peer(me, 1)
        for t in range(n):
            for j, k in enumerate(PASSED):
                sender = _peer(me, k)
                arrived = len(DIRECT) * t + 1 + j
                _shard_copy(shard_refs[t], land_refs[t], windows[t], sender, me, refs[2 * n].at[arrived], recv_sems.at[arrived],
                            True).wait_recv()
                i = len(PASSED) * t + j
                _shard_copy(shard_refs[t], land_refs[t], windows[t], sender, sibling, pass_send.at[i], pass_recv.at[i],
                            False).start()

    sem_shapes = [pltpu.SemaphoreType.DMA((len(PASSED) * n,))] * 2
    result = _hbm_passthrough(body, name, list(shards) + list(lands), 2, sem_shapes, after)(
        *shards, *lands, sems[0], sems[1], *after)
    return (result[0], result[1]), list(result[2:2 + n]), list(result[2 + n:])


def _gather_wait(name, sems, pass_sems, shards, lands, windows, after):
    n = len(shards)

    def body(*refs):
        shard_refs, land_refs = refs[:n], refs[n:2 * n]
        send_sems, recv_sems, pass_send, pass_recv = refs[2 * n:2 * n + 4]
        me = _position()
        sibling = _peer(me, 1)
        for t in range(n):
            for j, k in enumerate(DIRECT):
                i = len(DIRECT) * t + j
                _shard_copy(shard_refs[t], land_refs[t], windows[t], me, _peer(me, k), send_sems.at[i], recv_sems.at[i],
                            True).wait_send()
            _shard_copy(shard_refs[t], land_refs[t], windows[t], sibling, me, send_sems.at[len(DIRECT) * t],
                        recv_sems.at[len(DIRECT) * t], True).wait_recv()
            for j, k in enumerate(PASSED):
                i = len(PASSED) * t + j
                _shard_copy(shard_refs[t], land_refs[t], windows[t], _peer(me, k), sibling, pass_send.at[i], pass_recv.at[i],
                            False).wait_send()
                _shard_copy(shard_refs[t], land_refs[t], windows[t], _peer(sibling, k), me, pass_send.at[i], pass_recv.at[i],
                            False).wait_recv()

    result = _hbm_passthrough(body, name, list(shards) + list(lands), 4, [], after)(
        *shards, *lands, sems[0], sems[1], pass_sems[0], pass_sems[1], *after)
    return list(result[n:])


def _all_gather(name, locals_, out_shapes, windows, deps=()):
    n = len(locals_)

    def body(*refs):
        src_refs, out_refs = refs[:n], refs[n + len(deps):2 * n + len(deps)]
        send_sems, recv_sems, local_sems = refs[2 * n + len(deps):]
        x, y, c = _position()
        me, sibling = (x, y, c), (x, y, 1 - c)
        chips = [(1 - x, y), (x, 1 - y), (1 - x, 1 - y)]

        def copy(t, k, block, to, from_local=False):
            dst = windows[t](out_refs[t], _flat(block))
            return pltpu.make_async_remote_copy(
                src_ref=src_refs[t] if from_local else dst, dst_ref=dst, send_sem=send_sems.at[t, k],
                recv_sem=recv_sems.at[t, k], device_id=to, device_id_type=MESH)

        mine = [pltpu.make_async_copy(src_refs[t], windows[t](out_refs[t], _flat(me)), local_sems.at[t]) for t in range(n)]
        sends = []
        for t in range(n):
            mine[t].start()
            sends.append(copy(t, 0, me, sibling, True))
            sends += [copy(t, 1 + j, me, (*chip, c), True) for j, chip in enumerate(chips)]
        for cp in sends:
            cp.start()
        for t in range(n):
            for j, chip in enumerate(chips):
                copy(t, 1 + j, (*chip, c), me).wait_recv()
                passed = copy(t, 4 + j, (*chip, c), sibling)
                passed.start()
                sends.append(passed)
        for t in range(n):
            copy(t, 0, sibling, me).wait_recv()
            for j, chip in enumerate(chips):
                copy(t, 4 + j, (*chip, 1 - c), me).wait_recv()
        for cp in sends:
            cp.wait_send()
        for cp in mine:
            cp.wait()

    return pl.pallas_call(
        body, out_shape=tuple(out_shapes), in_specs=[ANY] * (n + len(deps)), out_specs=tuple([ANY] * n),
        scratch_shapes=[pltpu.SemaphoreType.DMA((n, 7)), pltpu.SemaphoreType.DMA((n, 7)), pltpu.SemaphoreType.DMA((n,))],
        name=name)(*locals_, *deps)


def _adamw(name, me, lands, owns, w, m, v, *, grid, land_specs, own_specs, p_spec):
    n_land = len(lands)

    def body(me_ref, *refs):
        land_refs, own_refs = refs[:n_land], refs[n_land:n_land + len(owns)]
        w_ref, m_ref, v_ref, g_ref, delta_ref, m_out, v_out = refs[n_land + len(owns):]
        ncols = w_ref.shape[-1]
        sums = []
        for i, land_ref in enumerate(land_refs):
            g = own_refs[i][...].astype(F32) if owns else land_ref[0].astype(F32)
            for s in range(0 if owns else 1, land_ref.shape[0]):
                g = g + land_ref[s].astype(F32)
            sums.append(g[:, :ncols])
        g = sums[0] if n_land == 1 else jnp.where(pl.program_id(0) == 0, sums[0], sums[1])
        m_new = ADAM_B1 * m_ref[...] + (1.0 - ADAM_B1) * g
        v_new = ADAM_B2 * v_ref[...] + (1.0 - ADAM_B2) * jnp.square(g)
        m_hat = m_new / (1.0 - ADAM_B1 ** ADAM_STEP)
        v_hat = v_new / (1.0 - ADAM_B2 ** ADAM_STEP)
        g_ref[...] = g
        delta_ref[...] = -ADAM_LR * (m_hat / (jnp.sqrt(v_hat) + ADAM_EPS) + ADAM_WD * w_ref[...])
        m_out[...] = m_new
        v_out[...] = v_new

    shape = jax.ShapeDtypeStruct(w.shape, F32)
    return pl.pallas_call(
        body, out_shape=(shape,) * 4,
        grid_spec=pltpu.PrefetchScalarGridSpec(
            num_scalar_prefetch=1, grid=grid, in_specs=list(land_specs) + list(own_specs) + [p_spec, p_spec, p_spec],
            out_specs=(p_spec,) * 4),
        name=name, compiler_params=_params())(me, *lands, *owns, w, m, v)


def _row(p, layer):
    return p[layer][None, :]


def _square(name, a, b, dims, out_dtype, deps=()):
    if a.shape == (D_MODEL, SEQ):
        return _matmul(name, a, b, grid=(2, 1), a_spec=pl.BlockSpec((512, SEQ), lambda i, k: (i, 0)),
                       b_spec=pl.BlockSpec((SEQ, D_MODEL), lambda i, k: (0, 0)),
                       o_spec=pl.BlockSpec((512, D_MODEL), lambda i, k: (i, 0)),
                       out_shape=jax.ShapeDtypeStruct((D_MODEL, D_MODEL), out_dtype), dims=NN, acc_shape=(8, 128),
                       deps=deps)
    return _matmul(name, a, b, grid=(SEQ // TM_MM, 1), a_spec=pl.BlockSpec((TM_MM, D_MODEL), lambda i, k: (i, 0)),
                   b_spec=pl.BlockSpec((D_MODEL, D_MODEL), lambda i, k: (0, 0)),
                   o_spec=pl.BlockSpec((TM_MM, D_MODEL), lambda i, k: (i, 0)),
                   out_shape=jax.ShapeDtypeStruct((SEQ, D_MODEL), out_dtype), dims=dims, acc_shape=(8, 128), deps=deps)


def _grouped_matmul(name, a_list, b, *, n_tiles, a_block, b_spec, o_spec, out_shape):
    n_groups = len(a_list)

    def a_spec(g):
        def index(j, i):
            mine = j // 3
            return (jnp.where(mine == g, i, jnp.where(mine < g, 0, n_tiles - 1)), 0)
        return pl.BlockSpec(a_block, index)

    def body(*refs):
        b_ref, o_ref = refs[n_groups], refs[n_groups + 1]
        mine = pl.program_id(0) // 3
        for g in range(n_groups):
            @pl.when(mine == g)
            def _(g=g):
                o_ref[...] = jnp.dot(refs[g][...], b_ref[...], preferred_element_type=F32).astype(o_ref.dtype)

    return pl.pallas_call(
        body, out_shape=out_shape, grid=(3 * n_groups, n_tiles), in_specs=[a_spec(g) for g in range(n_groups)] + [b_spec],
        out_specs=o_spec, name=name, compiler_params=_params())(*a_list, b)


def _qkv_fwd(name, hs, w):
    return _grouped_matmul(name, hs, w, n_tiles=SEQ // TM_MM, a_block=(TM_MM, D_MODEL),
                           b_spec=pl.BlockSpec((D_MODEL, D_MODEL), lambda j, i: (0, j)),
                           o_spec=pl.BlockSpec((TM_MM, D_MODEL), lambda j, i: (i, j)),
                           out_shape=jax.ShapeDtypeStruct((SEQ, 3 * len(hs) * D_MODEL), BF16))


def _qkv_dw(name, hs_t, dqkv):
    return _grouped_matmul(name, hs_t, dqkv, n_tiles=2, a_block=(512, SEQ),
                           b_spec=pl.BlockSpec((None, SEQ, D_MODEL), lambda j, i: (j, 0, 0)),
                           o_spec=pl.BlockSpec((512, D_MODEL), lambda j, i: (i, j)),
                           out_shape=jax.ShapeDtypeStruct((D_MODEL, 3 * len(hs_t) * D_MODEL), BF16))


def _proj_do_sorted(name, d_a, w_o):
    def body(da_ref, w_ref, *refs):
        value = lax.dot_general(da_ref[...], w_ref[...], NT, preferred_element_type=F32)
        refs[0][...] = value
        for j, dil in enumerate(SORTED):
            _sort_tile(refs[-1], value, dil, refs[1 + j])

    tile = pl.BlockSpec((TM, D_MODEL), lambda i: (i, 0))
    shapes = [jax.ShapeDtypeStruct((SEQ, D_MODEL), F32)] + [jax.ShapeDtypeStruct((dil, SEQ // dil, D_MODEL), F32) for dil in SORTED]
    result = pl.pallas_call(
        body, out_shape=tuple(shapes), grid=(SEQ // TM,),
        in_specs=[tile, pl.BlockSpec((D_MODEL, D_MODEL), lambda i: (0, 0))],
        out_specs=tuple([tile] + [_sorted_spec(dil) for dil in SORTED]), scratch_shapes=[_sort_scratch()],
        name=name, compiler_params=_params())(d_a, w_o)
    return [t.reshape(SEQ, D_MODEL) for t in result]


def _qkv_dh(name, dqkv, w, n_chunks, deps):
    return _matmul(name, dqkv, w, grid=(n_chunks // 3, SEQ // TM, 1),
                   a_spec=pl.BlockSpec((3, TM, D_MODEL), lambda g, i, k: (g, i, 0)),
                   b_spec=pl.BlockSpec((D_MODEL, 3 * D_MODEL), lambda g, i, k: (0, g)),
                   o_spec=pl.BlockSpec((None, TM, D_MODEL), lambda g, i, k: (g, i, 0)),
                   out_shape=jax.ShapeDtypeStruct((n_chunks // 3, SEQ, D_MODEL), F32), dims=NT, acc_shape=(8, 128),
                   deps=deps, inner=3)


def _local_step(x, target, norms, rpb, fetch, emit, deps):
    mix_pre, mix_post, ffn_pre, ffn_post = norms
    slopes = 2.0 ** (-8.0 * jnp.arange(1, N_HEADS + 1, dtype=F32) / N_HEADS)
    rpb_pad = jnp.pad(rpb, ((0, 0), (0, 1), (0, 128 - 31)))
    saved = []

    for layer in range(2):
        tag = f"l{layer}"
        if layer == 0:
            h = _rms_fwd(tag + "_norm_mix", x, _row(mix_pre, layer), out_dtype=BF16, deps=deps)
            hs = [h]
            table = _rpb_table(rpb_pad)
            w_qkv, w_o = fetch("na", [table, h], [h])
            qkv = _qkv_fwd(tag + "_qkv", hs, w_qkv)
            o, lse = _na_fwd(qkv, table)
            mixer = (hs, qkv, o, lse, table)
        else:
            hs = [t.reshape(SEQ, D_MODEL) for t in
                  _rms_fwd(tag + "_norm_mix", x, _row(mix_pre, layer), out_dtype=BF16, sorted_too=True)]
            w_qkv, w_o = fetch("dil", [x], [hs[0]])
            qkv = _qkv_fwd(tag + "_qkv", hs, w_qkv)
            outs, lses = zip(*[_dil_fwd(g, qkv, slopes * dil) for g, (_, dil) in enumerate(DIL_GROUPS)])
            merged, lse_total = _dil_merge(outs, lses)
            o = merged[0]
            mixer = (hs, qkv, merged, lses, lse_total)
        a = _square(tag + "_proj", o, w_o, NN, F32)
        x1 = _rms_fwd(tag + "_post_mix", a, _row(mix_post, layer), res=x)
        h2 = _rms_fwd(tag + "_norm_ffn", x1, _row(ffn_pre, layer), out_dtype=BF16)
        w_gu, w_down = fetch(f"ffn{layer}", [a], [h2])
        f, hg, hu, act_t = _ffn_fwd(tag + "_ffn", h2, w_gu, w_down)
        x2 = _rms_fwd(tag + "_post_ffn", f, _row(ffn_post, layer), res=x1)
        transposed = ([t.T for t in hs], o.astype(BF16).T, h2.T, act_t)
        saved.append((x, mixer, a, x1, transposed, hg, hu, f, w_qkv, w_o, w_gu, w_down))
        x = x2

    dx, loss = _loss_head("loss_head", x, target)
    d_norm = {k: [None, None] for k in ("mix_pre", "mix_post", "ffn_pre", "ffn_post")}
    d_rpb = None

    for layer in (1, 0):
        tag = f"b{layer}"
        x0, mixer, a, x1, (h_t, o_t, h2_t, act_t), hg, hu, f, w_qkv, w_o, w_gu, w_down = saved[layer]
        d_f, d_norm["ffn_post"][layer] = _rms_bwd(tag + "_post_ffn", f, _row(ffn_post, layer), [dx], out_dtype=BF16)
        dgu, d_h2 = _ffn_bwd(tag + "_ffn", d_f, w_gu, w_down, hg, hu)
        d_down = _matmul(
            tag + "_ffn_ddown", act_t, d_f, grid=(N_DEV, 1),
            a_spec=pl.BlockSpec((None, FF_PAD, SEQ), lambda d, k: (d, 0, 0)),
            b_spec=pl.BlockSpec((SEQ, D_MODEL), lambda d, k: (0, 0)),
            o_spec=pl.BlockSpec((None, FF_PAD, D_MODEL), lambda d, k: (d, 0, 0)),
            out_shape=jax.ShapeDtypeStruct((N_DEV, FF_PAD, D_MODEL), BF16), dims=NN, acc_shape=(8, 128))
        d_gu = _ffn_dgu(tag + "_ffn_dgu", h2_t, dgu)
        sent = emit(f"ffn{layer}", [d_gu, d_down])
        dx1, d_norm["ffn_pre"][layer] = _rms_bwd(tag + "_norm_ffn", x1, _row(ffn_pre, layer), [d_h2], res=dx, deps=sent)
        d_a, d_norm["mix_post"][layer] = _rms_bwd(tag + "_post_mix", a, _row(mix_post, layer), [dx1], out_dtype=BF16)
        d_wo = _square(tag + "_proj_dw", o_t, d_a, NN, BF16)
        if layer == 0:
            _, qkv, o, lse, table = mixer
            d_o = _square(tag + "_proj_do", d_a, w_o, NT, BF16)
            dqkv, gp = _na_bwd(qkv, table, d_o, lse)
            d_rpb = _rpb_grad(gp)[:, :15, :31]
            sent = emit("na", [_qkv_dw(tag + "_qkv_dw", h_t, dqkv), d_wo])
            d_h = _qkv_dh(tag + "_qkv_dh", dqkv, w_qkv, 3, sent)
            dx, d_norm["mix_pre"][layer] = _rms_bwd(tag + "_norm_mix", x0, _row(mix_pre, layer), [d_h[0]], res=dx1)
        else:
            _, qkv, merged, lses, lse_total = mixer
            d_o = _proj_do_sorted(tag + "_proj_do", d_a, w_o)
            dqkv = lax.empty((3 * len(DIL_GROUPS), SEQ, D_MODEL), BF16)
            for g, (_, dil) in enumerate(DIL_GROUPS):
                dqkv = _dil_bwd(g, qkv, slopes * dil, d_o[g], merged[g], lses[g], lse_total[g], dqkv)
            sent = emit("dil", [_qkv_dw(tag + "_qkv_dw", h_t, dqkv), d_wo])
            d_h = _qkv_dh(tag + "_qkv_dh", dqkv, w_qkv, 9, sent)
            dx, d_norm["mix_pre"][layer] = _rms_bwd(tag + "_norm_mix", x0, _row(mix_pre, layer), None, res=dx1, groups=d_h)

    d_gains = [jnp.concatenate(d_norm[k], axis=0) for k in ("mix_pre", "mix_post", "ffn_pre", "ffn_post")]
    return loss, dx, d_gains, d_rpb


RPB_SIZE = N_HEADS * 15 * 31


def _pack_small(gains, rpb, last=None):
    top = jnp.concatenate(gains, axis=0).reshape(64, 128)
    bottom = jnp.pad(rpb.reshape(-1), (0, 64 * 128 - RPB_SIZE))
    if last is not None:
        bottom = bottom + jnp.pad(last.reshape(1), (64 * 128 - 1, 0))
    return jnp.concatenate([top, bottom.reshape(64, 128)], axis=0)


def _unpack_small(p):
    gains = p[:64].reshape(4, 2, D_MODEL)
    rpb = p[64:].reshape(-1)[:RPB_SIZE].reshape(1, N_HEADS, 15, 31)
    return [gains[i] for i in range(4)], rpb


GROUPS = ("na", "ffn0", "dil", "ffn1")


def kernel(x, norm_mix_pre, norm_mix_post, norm_ffn_pre, norm_ffn_post, na_w_qkv, na_w_o, na_rpb, dil_w_qkv, dil_w_o, ffn_w_gate, ffn_w_up, ffn_w_down, loss_target, m_norm_mix_pre, m_norm_mix_post, m_norm_ffn_pre, m_norm_ffn_post, m_na_w_qkv, m_na_w_o, m_na_rpb, m_dil_w_qkv, m_dil_w_o, m_ffn_w_gate, m_ffn_w_up, m_ffn_w_down, v_norm_mix_pre, v_norm_mix_post, v_norm_ffn_pre, v_norm_ffn_post, v_na_w_qkv, v_na_w_o, v_na_rpb, v_dil_w_qkv, v_dil_w_o, v_ffn_w_gate, v_ffn_w_up, v_ffn_w_down):
    na_cols, dil_cols, o_rows = 3 * D_MODEL // N_DEV, 9 * D_MODEL // N_DEV, D_MODEL // N_DEV
    ff_pad = FF_PAD - FF_SHARD
    me = (4 * lax.axis_index("x") + 2 * lax.axis_index("y") + lax.axis_index("c")).astype(jnp.int32).reshape(1)

    full = {
        "na": [((D_MODEL, 3 * D_MODEL), _columns(na_cols)), ((N_DEV, o_rows, D_MODEL), _leading)],
        "dil": [((D_MODEL, 9 * D_MODEL), _columns(dil_cols)), ((N_DEV, o_rows, D_MODEL), _leading)],
        "ffn0": [((N_DEV, 2, D_MODEL, FF_PAD), _leading), ((N_DEV, FF_PAD, D_MODEL), _leading)],
        "ffn1": [((N_DEV, 2, D_MODEL, FF_PAD), _leading), ((N_DEV, FF_PAD, D_MODEL), _leading)],
    }
    block = {
        "na": [(D_MODEL, na_cols), (o_rows, D_MODEL)], "dil": [(D_MODEL, dil_cols), (o_rows, D_MODEL)],
        "ffn0": [(2, D_MODEL, FF_PAD), (FF_PAD, D_MODEL)], "ffn1": [(2, D_MODEL, FF_PAD), (FF_PAD, D_MODEL)],
    }

    land_shapes = [jax.ShapeDtypeStruct(full[g][t][0], BF16) for g in GROUPS for t in range(2)]
    windows = [full[g][t][1] for g in GROUPS for t in range(2)]
    shards, lands = _prep_weights(me, na_w_qkv, na_w_o, dil_w_qkv, dil_w_o, ffn_w_gate, ffn_w_up, ffn_w_down, land_shapes)
    sems, shards, lands = _gather_start("gather_start", shards, lands, windows, [2] * len(GROUPS))

    def fetch(group, early, late):
        gi = GROUPS.index(group)
        mine = slice(2 * gi, 2 * gi + 2)
        pass_sems, shards_g, lands_g = _gather_pass_on(f"gather_pass_{group}", sems[gi], shards[mine], lands[mine],
                                                       windows[mine], early)
        qkv, o = _gather_wait(f"gather_wait_{group}", sems[gi], pass_sems, shards_g, lands_g, windows[mine], late)
        return (qkv, o.reshape(D_MODEL, D_MODEL)) if group in ("na", "dil") else (qkv, o)

    def grad_source(group, t):
        return _columns(block[group][0][1]) if (group in ("na", "dil") and t == 0) else _leading

    in_flight = {}

    def emit(group, grads):
        if group in ("na", "dil"):
            grads = [grads[0], grads[1].reshape(N_DEV, o_rows, D_MODEL)]
        sets = [(t, grad_source(group, t), t, _by_distance) for t in range(2)]
        landing = [lax.empty((N_DEV - 1,) + block[group][t], BF16) for t in range(2)]
        sems_g, grads, landing, tok = _send_start(f"exchange_start_{group}", grads, landing, [sets])
        in_flight[group] = (sems_g[0], grads, landing, sets)
        return [tok]

    norms = (norm_mix_pre, norm_mix_post, norm_ffn_pre, norm_ffn_post)
    loss, grad_x, d_gains, d_rpb = _local_step(x[0], loss_target[0], norms, na_rpb[0], fetch, emit, [shards[0]])

    landed, sent = {}, {}

    def wait_for(group, after):
        sems_g, grads, landing, sets = in_flight[group]
        sent[group], landed[group] = _send_wait(f"exchange_wait_{group}", sems_g, grads, landing, sets, after)

    for group in ("ffn1", "dil", "ffn0"):
        wait_for(group, [grad_x])

    def one(rows, tile, ncols, columns):
        own = (pl.BlockSpec((tile, ncols), lambda i, me: (i, me[0])) if columns
               else pl.BlockSpec((None, tile, ncols), lambda i, me: (me[0], i, 0)))
        return dict(grid=(rows // tile,), land_specs=[pl.BlockSpec((N_DEV - 1, tile, ncols), lambda i, me: (0, i, 0))],
                    own_specs=[own], p_spec=pl.BlockSpec((None, tile, ncols), lambda i, me: (0, i, 0)))

    def layered(block_shape, index, p_block, n_tiles):
        def specs(lead_size, lead):
            shape = (lead_size,) + block_shape
            return [pl.BlockSpec(shape, lambda l, r, me: index(lead(me), jnp.where(l == 0, r, n_tiles - 1))),
                    pl.BlockSpec(shape, lambda l, r, me: index(lead(me), jnp.where(l == 0, 0, r)))]
        return dict(grid=(2, n_tiles), land_specs=specs(N_DEV - 1, lambda me: 0), own_specs=specs(None, lambda me: me[0]),
                    p_spec=pl.BlockSpec(p_block, lambda l, r, me: (l, r, 0)))

    gu_lands, gu_owns = [landed["ffn0"][0], landed["ffn1"][0]], [sent["ffn0"][0], sent["ffn1"][0]]
    down_lands, down_owns = [landed["ffn0"][1], landed["ffn1"][1]], [sent["ffn0"][1], sent["ffn1"][1]]
    updates = {
        "dil_w_qkv": _adamw("adamw_dil_qkv", me, [landed["dil"][0]], [sent["dil"][0]], dil_w_qkv, m_dil_w_qkv, v_dil_w_qkv,
                            **one(D_MODEL, 128, dil_cols, True)),
        "dil_w_o": _adamw("adamw_dil_o", me, [landed["dil"][1]], [sent["dil"][1]], dil_w_o, m_dil_w_o, v_dil_w_o,
                          **one(o_rows, o_rows, D_MODEL, False)),
        "ffn_w_gate": _adamw("adamw_gate", me, gu_lands, gu_owns, ffn_w_gate, m_ffn_w_gate, v_ffn_w_gate,
                             **layered((None, 128, FF_PAD), lambda lead, r: (lead, 0, r, 0), (None, 128, FF_SHARD), 8)),
        "ffn_w_up": _adamw("adamw_up", me, gu_lands, gu_owns, ffn_w_up, m_ffn_w_up, v_ffn_w_up,
                           **layered((None, 128, FF_PAD), lambda lead, r: (lead, 1, r, 0), (None, 128, FF_SHARD), 8)),
        "ffn_w_down": _adamw("adamw_down", me, down_lands, down_owns, ffn_w_down, m_ffn_w_down, v_ffn_w_down,
                             **layered((176, D_MODEL), lambda lead, r: (lead, r, 0), (None, 176, D_MODEL), 2)),
    }
    done = [u[0] for u in updates.values()]
    small = _all_gather("gather_small", [_pack_small(d_gains, d_rpb, loss)], [jax.ShapeDtypeStruct((N_DEV, 128, 128), F32)],
                        [_leading], deps=done)[0]
    wait_for("na", [small])
    updates["na_w_qkv"] = _adamw("adamw_na_qkv", me, [landed["na"][0]], [sent["na"][0]], na_w_qkv, m_na_w_qkv, v_na_w_qkv,
                                 **one(D_MODEL, 256, na_cols, True))
    updates["na_w_o"] = _adamw("adamw_na_o", me, [landed["na"][1]], [sent["na"][1]], na_w_o, m_na_w_o, v_na_w_o,
                               **one(o_rows, o_rows, D_MODEL, False))
    gains = [norm_mix_pre, norm_mix_post, norm_ffn_pre, norm_ffn_post]
    m_gains = [m_norm_mix_pre, m_norm_mix_post, m_norm_ffn_pre, m_norm_ffn_post]
    v_gains = [v_norm_mix_pre, v_norm_mix_post, v_norm_ffn_pre, v_norm_ffn_post]
    packed = _adamw("adamw_small", me, [small], (), _pack_small(gains, na_rpb)[None], _pack_small(m_gains, m_na_rpb)[None],
                    _pack_small(v_gains, v_na_rpb)[None], grid=(1,),
                    land_specs=[pl.BlockSpec((N_DEV, 128, 128), lambda i, me: (0, 0, 0))], own_specs=[],
                    p_spec=pl.BlockSpec((None, 128, 128), lambda i, me: (0, 0, 0)))
    small_out = [_unpack_small(p[0]) for p in packed]

    order = ["na_w_qkv", "na_w_o", "na_rpb", "dil_w_qkv", "dil_w_o", "ffn_w_gate", "ffn_w_up", "ffn_w_down"]
    result = [packed[0][0, 127, 127], grad_x[None]]
    for kind in range(4):
        gains_k, rpb_k = small_out[kind]
        result += gains_k
        result += [rpb_k if name == "na_rpb" else updates[name][kind] for name in order]
    return tuple(result)
```

```python
import functools

import jax
import jax.numpy as jnp
from jax import lax
from jax.experimental import pallas as pl
from jax.experimental.pallas import tpu as pltpu

F32 = jnp.float32
BF16 = jnp.bfloat16
MESH = pl.DeviceIdType.MESH
ANY = pl.BlockSpec(memory_space=pl.ANY)
HBM = pl.BlockSpec(memory_space=pltpu.HBM)
SEM = pl.BlockSpec(memory_space=pltpu.SEMAPHORE)
EFFECT = pltpu.SideEffectType.DATAFLOW_SIDE_EFFECTING

N_DEV = 8
SEQ = 2048
D_MODEL = 1024
N_HEADS = 16
HEAD_DIM = 64
GRID_W = 64
NA_ROWS = 8
SEQ_ROWS = SEQ // GRID_W
DIL_GROUPS = ((128, 1), (512, 4), (2048, 16))
BAND = 128
RADIUS = 64
FF_SHARD = 352
FF_PAD = 384
RMS_EPS = 1e-6
NEG_INF = -1e30
Q_SCALE = HEAD_DIM ** -0.5

ADAM_LR = 0.001
ADAM_B1 = 0.9
ADAM_B2 = 0.999
ADAM_EPS = 1e-08
ADAM_WD = 0.01
ADAM_STEP = 10

VMEM_LIMIT = 56 * 1024 * 1024
TM = 512
TM_MM = 1024

NN = (((1,), (0,)), ((), ()))
NT = (((1,), (1,)), ((), ()))
TN = (((0,), (0,)), ((), ()))


def _params():
    return pltpu.CompilerParams(vmem_limit_bytes=VMEM_LIMIT)


def _matmul(name, a, b, *, grid, a_spec, b_spec, o_spec, out_shape, dims, acc_shape, deps=(), inner=1):
    nk = grid[-1]
    kaxis = len(grid) - 1

    def body(a_ref, b_ref, *rest):
        o_ref, acc_ref = rest[-2], rest[-1]
        if inner == 1:
            part = lax.dot_general(a_ref[...].astype(BF16), b_ref[...].astype(BF16), dims, preferred_element_type=F32)
        elif len(b_ref.shape) == 2:
            a_all = jnp.concatenate([a_ref[j].astype(BF16) for j in range(inner)], axis=1)
            part = lax.dot_general(a_all, b_ref[...].astype(BF16), dims, preferred_element_type=F32)
        else:
            part = sum(lax.dot_general(a_ref[j].astype(BF16), b_ref[j].astype(BF16), dims, preferred_element_type=F32)
                       for j in range(inner))
        if nk == 1:
            o_ref[...] = part.astype(o_ref.dtype)
        else:
            k = pl.program_id(kaxis)

            @pl.when(k == 0)
            def _():
                acc_ref[...] = part

            @pl.when(k > 0)
            def _():
                acc_ref[...] += part

            @pl.when(k == nk - 1)
            def _():
                o_ref[...] = acc_ref[...].astype(o_ref.dtype)

    return pl.pallas_call(
        body, out_shape=out_shape, grid=grid, in_specs=[a_spec, b_spec] + [ANY] * len(deps), out_specs=o_spec,
        scratch_shapes=[pltpu.VMEM(acc_shape, F32)], name=name, compiler_params=_params())(a, b, *deps)


SORTED = tuple(d for _, d in DIL_GROUPS if d > 1)
LANE_CHUNKS = D_MODEL // 128


def _sort_scratch(tm=TM):
    return pltpu.VMEM((LANE_CHUNKS, tm, 128), F32)


def _sorted_view(t, dil):
    return t.reshape(dil, SEQ // dil, D_MODEL)


def _sorted_spec(dil, lead=(), tm=TM):
    return pl.BlockSpec((None,) * len(lead) + (dil, tm // dil, D_MODEL), lambda i: tuple(lead) + (0, i, 0))


def _sort_tile(scratch, value, dil, out_ref):
    tm = value.shape[0]
    for c in range(LANE_CHUNKS):
        scratch[c] = value[:, 128 * c:128 * (c + 1)]
    for r in range(dil):
        rows = [scratch.at[c][pl.ds(r, tm // dil, stride=dil), :] for c in range(LANE_CHUNKS)]
        out_ref[r] = jnp.concatenate(rows, axis=1).astype(out_ref.dtype)


def _unsort_tile(scratch, in_ref, dil):
    for r in range(dil):
        value = in_ref[r].astype(F32)
        for c in range(LANE_CHUNKS):
            scratch.at[c][pl.ds(r, value.shape[0], stride=dil), :] = value[:, 128 * c:128 * (c + 1)]
    return jnp.concatenate([scratch[c] for c in range(LANE_CHUNKS)], axis=1)


def _rms_fwd(name, x, g, res=None, out_dtype=F32, deps=(), sorted_too=False):
    n_tiles = SEQ // TM
    has_res = res is not None
    n_in = 2 + has_res + len(deps)

    def body(*refs):
        x_ref, g_ref = refs[0], refs[1]
        xv = x_ref[...]
        r = lax.rsqrt(jnp.mean(xv * xv, axis=-1, keepdims=True) + RMS_EPS)
        y = xv * r * g_ref[...]
        if has_res:
            y = refs[2][...] + y
        refs[n_in][...] = y.astype(out_dtype)
        if sorted_too:
            for j, dil in enumerate(SORTED):
                _sort_tile(refs[-1], y, dil, refs[n_in + 1 + j])

    tile = pl.BlockSpec((TM, D_MODEL), lambda i: (i, 0))
    gspec = pl.BlockSpec((1, D_MODEL), lambda i: (0, 0))
    ins = [x, g] + ([res] if has_res else []) + list(deps)
    specs = [tile, gspec] + ([tile] if has_res else []) + [ANY] * len(deps)
    shapes, out_specs = [jax.ShapeDtypeStruct((SEQ, D_MODEL), out_dtype)], [tile]
    if sorted_too:
        shapes += [jax.ShapeDtypeStruct((dil, SEQ // dil, D_MODEL), out_dtype) for dil in SORTED]
        out_specs += [_sorted_spec(dil) for dil in SORTED]
    result = pl.pallas_call(
        body, out_shape=tuple(shapes), grid=(n_tiles,), in_specs=specs, out_specs=tuple(out_specs),
        scratch_shapes=[_sort_scratch()] if sorted_too else [], name=name, compiler_params=_params())(*ins)
    return result if sorted_too else result[0]


def _rms_bwd(name, x, g, dys, res=None, out_dtype=F32, groups=None, deps=()):
    n_tiles = SEQ // TM
    n_dy = len(dys) if groups is None else 1 + len(SORTED)
    has_res = res is not None

    def body(*refs):
        x_ref, g_ref = refs[0], refs[1]
        dy_refs = refs[2:2 + n_dy]
        res_ref = refs[2 + n_dy] if has_res else None
        first_out = 2 + n_dy + has_res + len(deps)
        dx_ref, dg_ref, acc_ref = refs[first_out:first_out + 3]
        i = pl.program_id(0)
        xv = x_ref[...]
        r = lax.rsqrt(jnp.mean(xv * xv, axis=-1, keepdims=True) + RMS_EPS)
        xn = xv * r
        dy = dy_refs[0][...].astype(F32)
        for j, extra in enumerate(dy_refs[1:]):
            dy = dy + (extra[...].astype(F32) if groups is None else _unsort_tile(refs[-1], extra, SORTED[j]))
        dyg = dy * g_ref[...]
        dx = r * (dyg - xn * jnp.mean(dyg * xn, axis=-1, keepdims=True))
        if has_res:
            dx = res_ref[...] + dx
        dx_ref[...] = dx.astype(dx_ref.dtype)
        part = jnp.sum((dy * xn).reshape(TM // 8, 8, D_MODEL), axis=0)

        @pl.when(i == 0)
        def _():
            acc_ref[...] = part

        @pl.when(i > 0)
        def _():
            acc_ref[...] += part

        @pl.when(i == n_tiles - 1)
        def _():
            dg_ref[...] = jnp.broadcast_to(jnp.sum(acc_ref[...], axis=0, keepdims=True), (8, D_MODEL))

    tile = pl.BlockSpec((TM, D_MODEL), lambda i: (i, 0))
    gspec = pl.BlockSpec((1, D_MODEL), lambda i: (0, 0))
    if groups is None:
        dy_ins, dy_specs = list(dys), [tile] * n_dy
    else:
        dy_ins = [groups] + [groups.reshape(n_dy, dil, SEQ // dil, D_MODEL) for dil in SORTED]
        dy_specs = [pl.BlockSpec((None, TM, D_MODEL), lambda i: (0, i, 0))]
        dy_specs += [_sorted_spec(dil, lead=(1 + j,)) for j, dil in enumerate(SORTED)]
    ins = [x, g] + dy_ins + ([res] if has_res else []) + list(deps)
    specs = [tile, gspec] + dy_specs + ([tile] if has_res else []) + [ANY] * len(deps)
    dx, dg = pl.pallas_call(
        body, out_shape=(jax.ShapeDtypeStruct((SEQ, D_MODEL), out_dtype), jax.ShapeDtypeStruct((8, D_MODEL), F32)),
        grid=(n_tiles,), in_specs=specs,
        out_specs=(tile, pl.BlockSpec((8, D_MODEL), lambda i: (0, 0))),
        scratch_shapes=[pltpu.VMEM((8, D_MODEL), F32)] + ([_sort_scratch()] if groups is not None else []),
        name=name, compiler_params=_params())(*ins)
    return dx, dg[0:1]


def _loss_head(name, y, target):
    n_tiles = SEQ // TM

    def body(y_ref, t_ref, dy_ref, loss_ref, acc_ref):
        i = pl.program_id(0)
        diff = y_ref[...] - t_ref[...]
        dy_ref[...] = diff * (1.0 / D_MODEL)
        part = jnp.sum((diff * diff).reshape(TM // 8, 8, D_MODEL), axis=0)

        @pl.when(i == 0)
        def _():
            acc_ref[...] = part

        @pl.when(i > 0)
        def _():
            acc_ref[...] += part

        @pl.when(i == n_tiles - 1)
        def _():
            loss_ref[...] = jnp.full((8, 128), jnp.sum(acc_ref[...]) * (0.5 / D_MODEL), F32)

    tile = pl.BlockSpec((TM, D_MODEL), lambda i: (i, 0))
    dy, loss = pl.pallas_call(
        body, out_shape=(jax.ShapeDtypeStruct((SEQ, D_MODEL), F32), jax.ShapeDtypeStruct((8, 128), F32)),
        grid=(n_tiles,), in_specs=[tile, tile], out_specs=(tile, pl.BlockSpec((8, 128), lambda i: (0, 0))),
        scratch_shapes=[pltpu.VMEM((8, D_MODEL), F32)], name=name, compiler_params=_params())(y, target)
    return dy, loss[0, 0]


def _row_index(shape):
    return lax.broadcasted_iota(jnp.int32, shape, 0)


def _lane_index(shape):
    return lax.broadcasted_iota(jnp.int32, shape, len(shape) - 1)


def _skew_rows(t, direction):
    q = _row_index(t.shape) & (GRID_W - 1)
    for bit in range(6):
        step = 1 << bit
        shift = step if direction > 0 else 128 - step
        t = jnp.where((q & step) != 0, pltpu.roll(t, shift, 1), t)
    return t


def _rpb_table(rpb_pad):
    rows = 16 * GRID_W

    def body(r_ref, t_ref):
        lane = _lane_index((rows, 128))
        v = pltpu.roll(r_ref[...], 128 - 15, 1)
        t = _skew_rows(jnp.broadcast_to(v[:, None, :], (16, GRID_W, 128)).reshape(rows, 128), +1)
        t = jnp.where(lane < GRID_W, t, 0.0)
        below = jnp.concatenate([t[GRID_W:], jnp.zeros((GRID_W, 128), F32)], axis=0)
        first_col = jnp.clip((_row_index((rows, 128)) & (GRID_W - 1)) - 8, 0, GRID_W - 16)
        key_col = lane & (GRID_W - 1)
        in_window = (key_col >= first_col) & (key_col < first_col + 16)
        t_ref[...] = jnp.where(in_window, t + pltpu.roll(below, GRID_W, 1), NEG_INF).reshape(16, GRID_W, 128)

    return pl.pallas_call(
        body, out_shape=jax.ShapeDtypeStruct((N_HEADS, 16, GRID_W, 128), F32), grid=(N_HEADS,),
        in_specs=[pl.BlockSpec((None, 16, 128), lambda h: (h, 0, 0))],
        out_specs=pl.BlockSpec((None, 16, GRID_W, 128), lambda h: (h, 0, 0, 0)),
        name="rpb_table", compiler_params=_params())(rpb_pad)


def _rpb_grad(gp):
    rows = 16 * GRID_W

    def body(g_ref, o_ref):
        lane = _lane_index((rows, 128))
        g = g_ref[...].reshape(rows, 128)
        low = jnp.where(lane < GRID_W, g, 0.0)
        high = pltpu.roll(jnp.where(lane >= GRID_W, g, 0.0), GRID_W, 1)
        above = jnp.concatenate([jnp.zeros((GRID_W, 128), F32), high[:rows - GRID_W]], axis=0)
        diag = jnp.sum(_skew_rows(low + above, -1).reshape(16, GRID_W, 128), axis=1)
        o_ref[...] = pltpu.roll(diag, 15, 1)

    return pl.pallas_call(
        body, out_shape=jax.ShapeDtypeStruct((N_HEADS, 16, 128), F32), grid=(N_HEADS,),
        in_specs=[pl.BlockSpec((None, 16, GRID_W, 128), lambda h: (h, 0, 0, 0))],
        out_specs=pl.BlockSpec((None, 16, 128), lambda h: (h, 0, 0)),
        name="rpb_grad", compiler_params=_params())(gp)


NA_KEYS = NA_ROWS * GRID_W


def _na_window(i):
    first_row = jnp.clip(i - NA_ROWS // 2, 0, SEQ_ROWS - NA_ROWS)
    return pl.multiple_of(first_row * GRID_W, GRID_W), first_row - i + NA_ROWS - 1


NA_STEP = 16


def _head_masks():
    lane = _lane_index((1, 128))
    return (lane < HEAD_DIM, lane >= HEAD_DIM)


def _stack_heads(t, masks):
    zero = jnp.zeros_like(t)
    return jnp.concatenate([jnp.where(masks[0], t, zero), jnp.where(masks[1], t, zero)], axis=0)


def _unstack_heads(t, masks):
    n = t.shape[0] // 2
    return jnp.where(masks[0], t[:n], t[n:])


def _stack_columns(t):
    return jnp.concatenate([t[:, 0:1], t[:, HEAD_DIM:HEAD_DIM + 1]], axis=0)


def _na_scores(qs, kw, tp_ref, dr0):
    s = lax.dot_general(qs, kw, NT, preferred_element_type=F32)
    bias = jnp.concatenate(
        [jnp.concatenate([tp_ref[a, pl.ds(dr0 + 2 * c, 1), :, :].reshape(GRID_W, 128) for c in range(4)], axis=1)
         for a in range(2)], axis=0)
    return s + bias


def _na_specs():
    q_spec = pl.BlockSpec((NA_STEP * GRID_W, 128), lambda hp, i: (i, hp))
    k_spec = pl.BlockSpec((SEQ, 128), lambda hp, i: (0, 8 + hp))
    v_spec = pl.BlockSpec((SEQ, 128), lambda hp, i: (0, 16 + hp))
    tp_spec = pl.BlockSpec((2, 16, GRID_W, 128), lambda hp, i: (hp, 0, 0, 0))
    return q_spec, k_spec, v_spec, tp_spec


def _na_fwd(qkv, table):
    def body(q_ref, k_ref, v_ref, tp_ref, o_ref, lse_ref):
        masks = _head_masks()
        for r in range(NA_STEP):
            rows = slice(r * GRID_W, (r + 1) * GRID_W)
            start, dr0 = _na_window(pl.program_id(1) * NA_STEP + r)
            kw = k_ref[pl.ds(start, NA_KEYS), :]
            vw = v_ref[pl.ds(start, NA_KEYS), :]
            s = _na_scores(_stack_heads(q_ref[rows, :] * Q_SCALE, masks), kw, tp_ref, dr0)
            m = jnp.max(s, axis=-1, keepdims=True)
            p = jnp.exp(s - m)
            denom = jnp.sum(p, axis=-1, keepdims=True)
            out = jnp.dot(p.astype(BF16), vw, preferred_element_type=F32) / denom
            o_ref[rows, :] = _unstack_heads(out, masks).astype(o_ref.dtype)
            lse_ref[rows, :] = _unstack_heads(jnp.broadcast_to(m + jnp.log(denom), (2 * GRID_W, 128)), masks)

    q_spec, k_spec, v_spec, tp_spec = _na_specs()
    return pl.pallas_call(
        body, out_shape=(jax.ShapeDtypeStruct((SEQ, D_MODEL), BF16), jax.ShapeDtypeStruct((SEQ, D_MODEL), F32)),
        grid=(N_HEADS // 2, SEQ_ROWS // NA_STEP), in_specs=[q_spec, k_spec, v_spec, tp_spec],
        out_specs=(q_spec, q_spec), name="na_fwd", compiler_params=_params())(qkv, qkv, qkv, table)


def _na_bwd(qkv, table, d_out, lse):
    def body(q_ref, k_ref, v_ref, tp_ref, do_ref, lse_ref, dqkv_ref, gp_ref, dk_acc, dv_acc):
        step = pl.program_id(1)

        @pl.when(step == 0)
        def _():
            dk_acc[...] = jnp.zeros_like(dk_acc)
            dv_acc[...] = jnp.zeros_like(dv_acc)
            gp_ref[...] = jnp.zeros_like(gp_ref)

        masks = _head_masks()
        for r in range(NA_STEP):
            rows = slice(r * GRID_W, (r + 1) * GRID_W)
            i = step * NA_STEP + r
            start, dr0 = _na_window(i)
            kw = k_ref[pl.ds(start, NA_KEYS), :]
            vw = v_ref[pl.ds(start, NA_KEYS), :]
            qs = _stack_heads(q_ref[rows, :] * Q_SCALE, masks)
            dos = _stack_heads(do_ref[rows, :], masks)
            p = jnp.exp(_na_scores(qs, kw, tp_ref, dr0) - _stack_columns(lse_ref[rows, :]))
            dp = lax.dot_general(dos, vw, NT, preferred_element_type=F32)
            ds = p * (dp - jnp.sum(p * dp, axis=-1, keepdims=True))
            for a in range(2):
                for c in range(4):
                    gp_ref[a, pl.ds(dr0 + 2 * c, 1), :, :] += (
                        ds[a * GRID_W:(a + 1) * GRID_W, 128 * c:128 * (c + 1)].reshape(1, GRID_W, 128))
            dsb = ds.astype(BF16)
            dq = _unstack_heads(jnp.dot(dsb, kw, preferred_element_type=F32), masks) * Q_SCALE
            dqkv_ref[0, pl.ds(pl.multiple_of(i * GRID_W, GRID_W), GRID_W), :] = dq.astype(dqkv_ref.dtype)
            dk_acc[pl.ds(start, NA_KEYS), :] += lax.dot_general(dsb, qs, TN, preferred_element_type=F32)
            dv_acc[pl.ds(start, NA_KEYS), :] += lax.dot_general(p.astype(BF16), dos, TN, preferred_element_type=F32)

        @pl.when(step == SEQ_ROWS // NA_STEP - 1)
        def _():
            dqkv_ref[1] = dk_acc[...].astype(dqkv_ref.dtype)
            dqkv_ref[2] = dv_acc[...].astype(dqkv_ref.dtype)

    q_spec, k_spec, v_spec, tp_spec = _na_specs()
    return pl.pallas_call(
        body,
        out_shape=(jax.ShapeDtypeStruct((3, SEQ, D_MODEL), BF16), jax.ShapeDtypeStruct((N_HEADS, 16, GRID_W, 128), F32)),
        grid=(N_HEADS // 2, SEQ_ROWS // NA_STEP), in_specs=[q_spec, k_spec, v_spec, tp_spec, q_spec, q_spec],
        out_specs=(pl.BlockSpec((3, SEQ, 128), lambda hp, i: (0, 0, hp)), tp_spec),
        scratch_shapes=[pltpu.VMEM((SEQ, 128), F32), pltpu.VMEM((SEQ, 128), F32)],
        name="na_bwd", compiler_params=_params())(qkv, qkv, qkv, table, d_out, lse)


DIL_STEP = 16


def _dil_geometry(group):
    dil = DIL_GROUPS[group][1]
    sub_len = SEQ // dil
    blocks = sub_len // BAND
    return dil, sub_len, max(blocks // DIL_STEP, 1), max(DIL_STEP // blocks, 1), min(2 * BAND, sub_len)


def _dil_block(step, r, sub_len, subs):
    per_sub = DIL_STEP // subs
    return (r // per_sub) * sub_len, step * per_sub + r % per_sub


def _dil_window(b, sub_len, n_keys):
    if n_keys == sub_len:
        return 0
    return pl.multiple_of(jnp.clip(b * BAND - RADIUS, 0, sub_len - n_keys), RADIUS)


def _dil_bias(b, start, n_keys, slope_ref, hp):
    row = _row_index((2 * BAND, n_keys))
    qpos = b * BAND + (row & (BAND - 1))
    kpos = start + _lane_index((2 * BAND, n_keys))
    dist = jnp.abs(qpos - kpos)
    slope = jnp.where(row < BAND, slope_ref[2 * hp], slope_ref[2 * hp + 1])
    return slope * dist.astype(F32), dist <= RADIUS


def _dil_scores(qs, kw, penalty, valid):
    return jnp.where(valid, lax.dot_general(qs, kw, NT, preferred_element_type=F32) - penalty, NEG_INF)


def _dil_specs(group):
    dil, sub_len, steps, subs, _ = _dil_geometry(group)
    col = group * 24
    rows = DIL_STEP * BAND
    q_spec = pl.BlockSpec((rows, 128), lambda n, hp, b: (n * steps + b, col + hp))
    k_spec = pl.BlockSpec((subs * sub_len, 128), lambda n, hp, b: (n, col + 8 + hp))
    v_spec = pl.BlockSpec((subs * sub_len, 128), lambda n, hp, b: (n, col + 16 + hp))
    tile = pl.BlockSpec((rows, 128), lambda n, hp, b: (n * steps + b, hp))
    smem = pl.BlockSpec(memory_space=pltpu.SMEM)
    return (dil // subs, N_HEADS // 2, steps), q_spec, k_spec, v_spec, tile, smem


def _dil_fwd(group, qkv, slopes):
    _, sub_len, _, subs, n_keys = _dil_geometry(group)

    def body(q_ref, k_ref, v_ref, slope_ref, o_ref, lse_ref):
        hp = pl.program_id(1)
        masks = _head_masks()
        for r in range(DIL_STEP):
            rows = slice(r * BAND, (r + 1) * BAND)
            base, b = _dil_block(pl.program_id(2), r, sub_len, subs)
            start = _dil_window(b, sub_len, n_keys)
            kw = k_ref[pl.ds(base + start, n_keys), :]
            vw = v_ref[pl.ds(base + start, n_keys), :]
            penalty, valid = _dil_bias(b, start, n_keys, slope_ref, hp)
            s = _dil_scores(_stack_heads(q_ref[rows, :] * Q_SCALE, masks), kw, penalty, valid)
            m = jnp.max(s, axis=-1, keepdims=True)
            p = jnp.exp(s - m)
            denom = jnp.sum(p, axis=-1, keepdims=True)
            out = jnp.dot(p.astype(BF16), vw, preferred_element_type=F32) / denom
            o_ref[rows, :] = _unstack_heads(out, masks).astype(o_ref.dtype)
            lse_ref[rows, :] = _unstack_heads(jnp.broadcast_to(m + jnp.log(denom), (2 * BAND, 128)), masks)

    grid, q_spec, k_spec, v_spec, tile, smem = _dil_specs(group)
    return pl.pallas_call(
        body, out_shape=(jax.ShapeDtypeStruct((SEQ, D_MODEL), BF16), jax.ShapeDtypeStruct((SEQ, D_MODEL), F32)),
        grid=grid, in_specs=[q_spec, k_spec, v_spec, smem], out_specs=(tile, tile),
        name=f"dil_fwd_{group}", compiler_params=_params())(qkv, qkv, qkv, slopes)


def _dil_merge(outs, lses):
    n_sorted = len(SORTED)

    def body(*refs):
        o_refs, l_refs = refs[:3], refs[3:6]
        out_refs, lse_refs, scratch = refs[6:7 + n_sorted], refs[7 + n_sorted:8 + 2 * n_sorted], refs[-1]
        os_ = [o_refs[0][...]] + [_unsort_tile(scratch, o_refs[1 + j], dil) for j, dil in enumerate(SORTED)]
        ls = [l_refs[0][...]] + [_unsort_tile(scratch, l_refs[1 + j], dil) for j, dil in enumerate(SORTED)]
        m = jnp.maximum(jnp.maximum(ls[0], ls[1]), ls[2])
        es = [jnp.exp(v - m) for v in ls]
        total = es[0] + es[1] + es[2]
        merged = (es[0] * os_[0] + es[1] * os_[1] + es[2] * os_[2]) / total
        lse = m + jnp.log(total)
        out_refs[0][...] = merged
        lse_refs[0][...] = lse
        for j, dil in enumerate(SORTED):
            _sort_tile(scratch, merged, dil, out_refs[1 + j])
            _sort_tile(scratch, lse, dil, lse_refs[1 + j])

    tm = 256
    tile = pl.BlockSpec((tm, D_MODEL), lambda i: (i, 0))
    specs = [tile] + [_sorted_spec(dil, tm=tm) for dil in SORTED]
    shapes = [jax.ShapeDtypeStruct((SEQ, D_MODEL), F32)] + [jax.ShapeDtypeStruct((dil, SEQ // dil, D_MODEL), F32) for dil in SORTED]
    views = lambda ts: [ts[0]] + [_sorted_view(t, dil) for t, dil in zip(ts[1:], SORTED)]
    result = pl.pallas_call(
        body, out_shape=tuple(shapes * 2), grid=(SEQ // tm,), in_specs=specs * 2, out_specs=tuple(specs * 2),
        scratch_shapes=[_sort_scratch(tm)], name="dil_merge", compiler_params=_params())(*views(outs), *views(lses))
    flat = [t.reshape(SEQ, D_MODEL) for t in result]
    return flat[:1 + n_sorted], flat[1 + n_sorted:]


def _dil_bwd(group, qkv, slopes, d_out, out, lse_group, lse_total, into):
    _, sub_len, steps, subs, n_keys = _dil_geometry(group)

    def body(q_ref, k_ref, v_ref, slope_ref, do_ref, o_ref, lg_ref, lt_ref, into_ref, dqkv_ref, dk_acc, dv_acc):
        hp, step = pl.program_id(1), pl.program_id(2)

        @pl.when(step == 0)
        def _():
            dk_acc[...] = jnp.zeros_like(dk_acc)
            dv_acc[...] = jnp.zeros_like(dv_acc)

        masks = _head_masks()
        for r in range(DIL_STEP):
            rows = slice(r * BAND, (r + 1) * BAND)
            base, b = _dil_block(step, r, sub_len, subs)
            start = _dil_window(b, sub_len, n_keys)
            keys = pl.ds(base + start, n_keys)
            kw = k_ref[keys, :]
            vw = v_ref[keys, :]
            penalty, valid = _dil_bias(b, start, n_keys, slope_ref, hp)
            qs = _stack_heads(q_ref[rows, :] * Q_SCALE, masks)
            lse2 = lg_ref[rows, :]
            weight = jnp.exp(lse2 - lt_ref[rows, :])
            do2 = do_ref[rows, :]
            dogs = _stack_heads((weight * do2).astype(BF16), masks)
            delta = _stack_columns(weight) * jnp.sum(_stack_heads(do2 * o_ref[rows, :], masks), axis=-1, keepdims=True)
            p = jnp.exp(_dil_scores(qs, kw, penalty, valid) - _stack_columns(lse2))
            dp = lax.dot_general(dogs, vw, NT, preferred_element_type=F32)
            dsb = (p * (dp - delta)).astype(BF16)
            dq = _unstack_heads(jnp.dot(dsb, kw, preferred_element_type=F32), masks) * Q_SCALE
            dqkv_ref[0, pl.ds(pl.multiple_of(base + b * BAND, BAND), BAND), :] = dq.astype(dqkv_ref.dtype)
            dk_acc[keys, :] += lax.dot_general(dsb, qs, TN, preferred_element_type=F32)
            dv_acc[keys, :] += lax.dot_general(p.astype(BF16), dogs, TN, preferred_element_type=F32)

        @pl.when(step == steps - 1)
        def _():
            dqkv_ref[1] = dk_acc[...].astype(dqkv_ref.dtype)
            dqkv_ref[2] = dv_acc[...].astype(dqkv_ref.dtype)

    grid, q_spec, k_spec, v_spec, tile, smem = _dil_specs(group)
    return pl.pallas_call(
        body, out_shape=jax.ShapeDtypeStruct(into.shape, into.dtype), grid=grid,
        in_specs=[q_spec, k_spec, v_spec, smem, tile, tile, tile, tile, ANY],
        out_specs=pl.BlockSpec((3, subs * sub_len, 128), lambda n, hp, b: (group, n, hp)),
        scratch_shapes=[pltpu.VMEM((subs * sub_len, 128), F32), pltpu.VMEM((subs * sub_len, 128), F32)],
        input_output_aliases={8: 0}, name=f"dil_bwd_{group}", compiler_params=_params(),
    )(qkv, qkv, qkv, slopes, d_out, out, lse_group, lse_total, into)


def _accumulate_rows(acc_ref, i, first, part):
    rows = pl.ds(pl.multiple_of(i * TM_MM, TM_MM), TM_MM)

    @pl.when(first)
    def _():
        acc_ref[rows, :] = part

    @pl.when(jnp.logical_not(first))
    def _():
        acc_ref[rows, :] += part


def _ffn_specs():
    tile = pl.BlockSpec((TM_MM, D_MODEL), lambda d, i: (i, 0))
    gate = pl.BlockSpec((None, None, D_MODEL, FF_PAD), lambda d, i: (d, 0, 0, 0))
    up = pl.BlockSpec((None, None, D_MODEL, FF_PAD), lambda d, i: (d, 1, 0, 0))
    down = pl.BlockSpec((None, FF_PAD, D_MODEL), lambda d, i: (d, 0, 0))
    hidden = pl.BlockSpec((None, TM_MM, FF_PAD), lambda d, i: (d, i, 0))
    whole = pl.BlockSpec((SEQ, D_MODEL), lambda d, i: (0, 0))
    return tile, gate, up, down, hidden, whole


def _ffn_fwd(name, h, w_gu, w_down):
    def body(h_ref, wg_ref, wu_ref, wd_ref, f_ref, hg_ref, hu_ref, act_t_ref):
        hv = h_ref[...]
        hg = jnp.dot(hv, wg_ref[...], preferred_element_type=F32)
        hu = jnp.dot(hv, wu_ref[...], preferred_element_type=F32)
        act = hg * jax.nn.sigmoid(hg) * hu
        act_t_ref[...] = act.T.astype(act_t_ref.dtype)
        hg_ref[...] = hg.astype(hg_ref.dtype)
        hu_ref[...] = hu.astype(hu_ref.dtype)
        _accumulate_rows(f_ref, pl.program_id(1), pl.program_id(0) == 0,
                         jnp.dot(act.astype(BF16), wd_ref[...], preferred_element_type=F32))

    tile, gate, up, down, hidden, whole = _ffn_specs()
    shape = jax.ShapeDtypeStruct((N_DEV, SEQ, FF_PAD), BF16)
    return pl.pallas_call(
        body, out_shape=(jax.ShapeDtypeStruct((SEQ, D_MODEL), F32), shape, shape, jax.ShapeDtypeStruct((N_DEV, FF_PAD, SEQ), BF16)),
        grid=(N_DEV, SEQ // TM_MM), in_specs=[tile, gate, up, down],
        out_specs=(whole, hidden, hidden, pl.BlockSpec((None, FF_PAD, TM_MM), lambda d, i: (d, 0, i))),
        name=name, compiler_params=_params())(h, w_gu, w_gu, w_down)


def _ffn_dgu(name, h_t, dgu):
    def body(h_ref, dgu_ref, o_ref):
        both = jnp.dot(h_ref[...], jnp.concatenate([dgu_ref[0], dgu_ref[1]], axis=1), preferred_element_type=F32)
        o_ref[0] = both[:, :FF_PAD].astype(o_ref.dtype)
        o_ref[1] = both[:, FF_PAD:].astype(o_ref.dtype)

    return pl.pallas_call(
        body, out_shape=jax.ShapeDtypeStruct((N_DEV, 2, D_MODEL, FF_PAD), BF16), grid=(N_DEV,),
        in_specs=[pl.BlockSpec((D_MODEL, SEQ), lambda d: (0, 0)), pl.BlockSpec((2, None, SEQ, FF_PAD), lambda d: (0, d, 0, 0))],
        out_specs=pl.BlockSpec((None, 2, D_MODEL, FF_PAD), lambda d: (d, 0, 0, 0)),
        name=name, compiler_params=_params())(h_t, dgu)


def _ffn_bwd(name, d_f, w_gu, w_down, hg, hu):
    def body(df_ref, wg_ref, wu_ref, wd_ref, hg_ref, hu_ref, dgu_ref, dh_ref):
        dact = lax.dot_general(df_ref[...], wd_ref[...], NT, preferred_element_type=F32)
        hgv = hg_ref[...].astype(F32)
        sig = jax.nn.sigmoid(hgv)
        d_gate = (dact * hu_ref[...].astype(F32) * (sig * (1.0 + hgv * (1.0 - sig)))).astype(BF16)
        d_up = (dact * hgv * sig).astype(BF16)
        dgu_ref[0] = d_gate
        dgu_ref[1] = d_up
        part = lax.dot_general(jnp.concatenate([d_gate, d_up], axis=1), jnp.concatenate([wg_ref[...], wu_ref[...]], axis=1),
                               NT, preferred_element_type=F32)
        _accumulate_rows(dh_ref, pl.program_id(1), pl.program_id(0) == 0, part)

    tile, gate, up, down, hidden, whole = _ffn_specs()
    return pl.pallas_call(
        body, out_shape=(jax.ShapeDtypeStruct((2, N_DEV, SEQ, FF_PAD), BF16), jax.ShapeDtypeStruct((SEQ, D_MODEL), F32)),
        grid=(N_DEV, SEQ // TM_MM), in_specs=[tile, gate, up, down, hidden, hidden],
        out_specs=(pl.BlockSpec((2, None, TM_MM, FF_PAD), lambda d, i: (0, d, i, 0)), whole),
        name=name, compiler_params=_params())(d_f, w_gu, w_gu, w_down, hg, hu)


def _position():
    return lax.axis_index("x"), lax.axis_index("y"), lax.axis_index("c")


def _flat(p):
    return 4 * p[0] + 2 * p[1] + p[2]


def _peer(me, k):
    x, y, c = me
    return (1 - x if k & 4 else x, 1 - y if k & 2 else y, 1 - c if k & 1 else c)


def _columns(width):
    return lambda ref, d: ref.at[:, pl.ds(pl.multiple_of(d * width, 128), width)]


def _leading(ref, d):
    return ref.at[d]


def _whole(ref, d):
    return ref


def _by_sender(window):
    return lambda ref, sender, k: window(ref, sender)


def _by_distance(ref, sender, k):
    return ref.at[k - 1]


def _prep_weights(me, na_qkv, na_o, dil_qkv, dil_o, gate, up, down, land_shapes):
    na_cols, dil_cols = na_qkv.shape[-1], dil_qkv.shape[-1]
    o_rows = na_o.shape[1]
    tiles = 4
    rows, rows_o = D_MODEL // tiles, o_rows // tiles

    def body(me_ref, naq, nao, dq, do_, g0, u0, d0, g1, u1, d1, *outs):
        def put(t, index, value):
            outs[t][index] = value
            outs[8 + t][index] = value

        put(0, ..., naq[...].astype(BF16))
        put(1, ..., nao[...].astype(BF16))
        put(4, ..., dq[...].astype(BF16))
        put(5, ..., do_[...].astype(BF16))
        for t, (g, u, d) in ((2, (g0, u0, d0)), (6, (g1, u1, d1))):
            for j, part in enumerate((g, u)):
                put(t, (j, slice(None), slice(0, FF_SHARD)), part[...].astype(BF16))
                put(t, (j, slice(None), slice(FF_SHARD, FF_PAD)), jnp.zeros((rows, FF_PAD - FF_SHARD), BF16))
            put(t + 1, (slice(0, FF_SHARD), slice(None)), d[...].astype(BF16))
            put(t + 1, (slice(FF_SHARD, FF_PAD), slice(None)), jnp.zeros((FF_PAD - FF_SHARD, D_MODEL), BF16))

    def tiled(width):
        return pl.BlockSpec((None, rows, width), lambda i, me: (0, i, 0))

    def layer(l, width):
        return pl.BlockSpec((None, rows, width), lambda i, me: (l, i, 0))

    def whole_layer(l):
        return pl.BlockSpec((None, FF_SHARD, D_MODEL), lambda i, me: (l, 0, 0))

    in_specs = [tiled(na_cols), pl.BlockSpec((None, rows_o, D_MODEL), lambda i, me: (0, i, 0)), tiled(dil_cols),
                pl.BlockSpec((None, rows_o, D_MODEL), lambda i, me: (0, i, 0)),
                layer(0, FF_SHARD), layer(0, FF_SHARD), whole_layer(0), layer(1, FF_SHARD), layer(1, FF_SHARD), whole_layer(1)]
    o_shard = pl.BlockSpec((rows_o, D_MODEL), lambda i, me: (i, 0))
    o_land = pl.BlockSpec((None, rows_o, D_MODEL), lambda i, me: (me[0], i, 0))
    gu_shard = pl.BlockSpec((2, rows, FF_PAD), lambda i, me: (0, i, 0))
    gu_land = pl.BlockSpec((None, 2, rows, FF_PAD), lambda i, me: (me[0], 0, i, 0))
    down_shard = pl.BlockSpec((FF_PAD, D_MODEL), lambda i, me: (0, 0))
    down_land = pl.BlockSpec((None, FF_PAD, D_MODEL), lambda i, me: (me[0], 0, 0))

    def qkv_shard(width):
        return pl.BlockSpec((rows, width), lambda i, me: (i, 0))

    def qkv_land(width):
        return pl.BlockSpec((rows, width), lambda i, me: (i, me[0]))

    shard_specs = [qkv_shard(na_cols), o_shard, gu_shard, down_shard, qkv_shard(dil_cols), o_shard, gu_shard, down_shard]
    land_specs = [qkv_land(na_cols), o_land, gu_land, down_land, qkv_land(dil_cols), o_land, gu_land, down_land]
    shard_shapes = [jax.ShapeDtypeStruct(s, BF16) for s in
                    ((D_MODEL, na_cols), (o_rows, D_MODEL), (2, D_MODEL, FF_PAD), (FF_PAD, D_MODEL),
                     (D_MODEL, dil_cols), (o_rows, D_MODEL), (2, D_MODEL, FF_PAD), (FF_PAD, D_MODEL))]
    result = pl.pallas_call(
        body, out_shape=tuple(shard_shapes + list(land_shapes)),
        grid_spec=pltpu.PrefetchScalarGridSpec(num_scalar_prefetch=1, grid=(tiles,), in_specs=in_specs,
                                               out_specs=tuple(shard_specs + land_specs)),
        name="prep_weights", compiler_params=_params())(me, na_qkv, na_o, dil_qkv, dil_o, gate, up, down, gate, up, down)
    return list(result[:8]), list(result[8:])


def _remote_copies(sets, src_refs, land_refs, send_sems, recv_sems, outgoing):
    me = _position()
    copies = []
    for t, (si, src_of, li, dst_of) in enumerate(sets):
        for k in range(1, N_DEV):
            other = _peer(me, k)
            sender = me if outgoing else other
            copies.append(pltpu.make_async_remote_copy(
                src_ref=src_of(src_refs[si], _flat(other)), dst_ref=dst_of(land_refs[li], _flat(sender), k),
                send_sem=send_sems.at[(N_DEV - 1) * t + k - 1], recv_sem=recv_sems.at[(N_DEV - 1) * t + k - 1],
                device_id=other, device_id_type=MESH))
    return copies


def _send_start(name, srcs, lands, sets_by_group):
    n_src, n_land, n_groups = len(srcs), len(lands), len(sets_by_group)

    def body(*refs):
        src_refs, land_refs = refs[:n_src], refs[n_src:n_src + n_land]
        outs = refs[n_src + n_land:]
        for g, sets in enumerate(sets_by_group):
            for cp in _remote_copies(sets, src_refs, land_refs, outs[2 * g], outs[2 * g + 1], True):
                cp.start()
        outs[-1][...] = jnp.zeros_like(outs[-1])

    sem_shapes = []
    for sets in sets_by_group:
        sem_shapes += [pltpu.SemaphoreType.DMA((len(sets) * (N_DEV - 1),))] * 2
    thru = [pltpu.HBM(a.shape, a.dtype) for a in list(srcs) + list(lands)]
    n_sem = len(sem_shapes)
    result = pl.pallas_call(
        body, out_shape=tuple(sem_shapes + thru + [jax.ShapeDtypeStruct((8, 128), F32)]),
        in_specs=[HBM] * (n_src + n_land),
        out_specs=tuple([SEM] * n_sem + [HBM] * (n_src + n_land) + [pl.BlockSpec(memory_space=pltpu.VMEM)]),
        input_output_aliases={i: n_sem + i for i in range(n_src + n_land)},
        compiler_params=pltpu.CompilerParams(has_side_effects=EFFECT), name=name,
    )(*[pltpu.with_memory_space_constraint(a, pltpu.HBM) for a in list(srcs) + list(lands)])
    sems = [(result[2 * g], result[2 * g + 1]) for g in range(n_groups)]
    return sems, list(result[n_sem:n_sem + n_src]), list(result[n_sem + n_src:n_sem + n_src + n_land]), result[-1]


def _send_wait(name, sems, srcs, lands, sets, after):
    n_src, n_land = len(srcs), len(lands)

    def body(*refs):
        src_refs, land_refs = refs[:n_src], refs[n_src:n_src + n_land]
        send_sems, recv_sems = refs[n_src + n_land], refs[n_src + n_land + 1]
        for cp in _remote_copies(sets, src_refs, land_refs, send_sems, recv_sems, True):
            cp.wait_send()
        for cp in _remote_copies(sets, src_refs, land_refs, send_sems, recv_sems, False):
            cp.wait_recv()

    thru = [pltpu.HBM(a.shape, a.dtype) for a in list(srcs) + list(lands)]
    result = pl.pallas_call(
        body, out_shape=tuple(thru), in_specs=[HBM] * (n_src + n_land) + [SEM, SEM] + [ANY] * len(after),
        out_specs=tuple([HBM] * (n_src + n_land)), input_output_aliases={i: i for i in range(n_src + n_land)},
        compiler_params=pltpu.CompilerParams(has_side_effects=EFFECT), name=name,
    )(*srcs, *lands, sems[0], sems[1], *after)
    return list(result[:n_src]), list(result[n_src:])


DIRECT = (1, 2, 4, 6)
PASSED = DIRECT[1:]


def _hbm_passthrough(body, name, arrays, n_sem_in, sem_out_shapes, extra):
    n, n_out = len(arrays), len(sem_out_shapes)
    return pl.pallas_call(
        body, out_shape=tuple(list(sem_out_shapes) + [pltpu.HBM(a.shape, a.dtype) for a in arrays]),
        in_specs=[HBM] * n + [SEM] * n_sem_in + [ANY] * len(extra), out_specs=tuple([SEM] * n_out + [HBM] * n),
        input_output_aliases={i: n_out + i for i in range(n)},
        compiler_params=pltpu.CompilerParams(has_side_effects=EFFECT), name=name)


def _shard_copy(src_ref, land_ref, window, block, to, send_sem, recv_sem, from_shard):
    dst = window(land_ref, _flat(block))
    return pltpu.make_async_remote_copy(src_ref=src_ref if from_shard else dst, dst_ref=dst, send_sem=send_sem,
                                        recv_sem=recv_sem, device_id=to, device_id_type=MESH)


def _gather_start(name, shards, lands, windows, group_sizes):
    n = len(shards)

    def body(*refs):
        shard_refs, land_refs, outs = refs[:n], refs[n:2 * n], refs[2 * n:]
        me = _position()
        t = 0
        for g, size in enumerate(group_sizes):
            for local in range(size):
                for j, k in enumerate(DIRECT):
                    i = len(DIRECT) * local + j
                    _shard_copy(shard_refs[t], land_refs[t], windows[t], me, _peer(me, k), outs[2 * g].at[i],
                                outs[2 * g + 1].at[i], True).start()
                t += 1

    sem_shapes = [pltpu.SemaphoreType.DMA((len(DIRECT) * size,)) for size in group_sizes for _ in range(2)]
    arrays = [pltpu.with_memory_space_constraint(a, pltpu.HBM) for a in list(shards) + list(lands)]
    result = _hbm_passthrough(body, name, arrays, 0, sem_shapes, ())(*arrays)
    n_sem = len(sem_shapes)
    sems = [(result[2 * g], result[2 * g + 1]) for g in range(len(group_sizes))]
    return sems, list(result[n_sem:n_sem + n]), list(result[n_sem + n:])


def _gather_pass_on(name, sems, shards, lands, windows, after):
    n = len(shards)

    def body(*refs):
        shard_refs, land_refs = refs[:n], refs[n:2 * n]
        recv_sems = refs[2 * n + 1]
        pass_send, pass_recv = refs[2 * n + 2 + len(after)], refs[2 * n + 3 + len(after)]
        me = _position()
        sibling = _peer(me, 1)
        for t in range(n):
            for j, k in enumerate(PASSED):
                sender = _peer(me, k)
                arrived = len(DIRECT) * t + 1 + j
                _shard_copy(shard_refs[t], land_refs[t], windows[t], sender, me, refs[2 * n].at[arrived], recv_sems.at[arrived],
                            True).wait_recv()
                i = len(PASSED) * t + j
                _shard_copy(shard_refs[t], land_refs[t], windows[t], sender, sibling, pass_send.at[i], pass_recv.at[i],
                            False).start()

    sem_shapes = [pltpu.SemaphoreType.DMA((len(PASSED) * n,))] * 2
    result = _hbm_passthrough(body, name, list(shards) + list(lands), 2, sem_shapes, after)(
        *shards, *lands, sems[0], sems[1], *after)
    return (result[0], result[1]), list(result[2:2 + n]), list(result[2 + n:])


def _gather_wait(name, sems, pass_sems, shards, lands, windows, after):
    n = len(shards)

    def body(*refs):
        shard_refs, land_refs = refs[:n], refs[n:2 * n]
        send_sems, recv_sems, pass_send, pass_recv = refs[2 * n:2 * n + 4]
        me = _position()
        sibling = _peer(me, 1)
        for t in range(n):
            for j, k in enumerate(DIRECT):
                i = len(DIRECT) * t + j
                _shard_copy(shard_refs[t], land_refs[t], windows[t], me, _peer(me, k), send_sems.at[i], recv_sems.at[i],
                            True).wait_send()
            _shard_copy(shard_refs[t], land_refs[t], windows[t], sibling, me, send_sems.at[len(DIRECT) * t],
                        recv_sems.at[len(DIRECT) * t], True).wait_recv()
            for j, k in enumerate(PASSED):
                i = len(PASSED) * t + j
                _shard_copy(shard_refs[t], land_refs[t], windows[t], _peer(me, k), sibling, pass_send.at[i], pass_recv.at[i],
                            False).wait_send()
                _shard_copy(shard_refs[t], land_refs[t], windows[t], _peer(sibling, k), me, pass_send.at[i], pass_recv.at[i],
                            False).wait_recv()

    result = _hbm_passthrough(body, name, list(shards) + list(lands), 4, [], after)(
        *shards, *lands, sems[0], sems[1], pass_sems[0], pass_sems[1], *after)
    return list(result[n:])


def _all_gather(name, locals_, out_shapes, windows, deps=()):
    n = len(locals_)

    def body(*refs):
        src_refs, out_refs = refs[:n], refs[n + len(deps):2 * n + len(deps)]
        send_sems, recv_sems, local_sems = refs[2 * n + len(deps):]
        x, y, c = _position()
        me, sibling = (x, y, c), (x, y, 1 - c)
        chips = [(1 - x, y), (x, 1 - y), (1 - x, 1 - y)]

        def copy(t, k, block, to, from_local=False):
            dst = windows[t](out_refs[t], _flat(block))
            return pltpu.make_async_remote_copy(
                src_ref=src_refs[t] if from_local else dst, dst_ref=dst, send_sem=send_sems.at[t, k],
                recv_sem=recv_sems.at[t, k], device_id=to, device_id_type=MESH)

        mine = [pltpu.make_async_copy(src_refs[t], windows[t](out_refs[t], _flat(me)), local_sems.at[t]) for t in range(n)]
        sends = []
        for t in range(n):
            mine[t].start()
            sends.append(copy(t, 0, me, sibling, True))
            sends += [copy(t, 1 + j, me, (*chip, c), True) for j, chip in enumerate(chips)]
        for cp in sends:
            cp.start()
        for t in range(n):
            for j, chip in enumerate(chips):
                copy(t, 1 + j, (*chip, c), me).wait_recv()
                passed = copy(t, 4 + j, (*chip, c), sibling)
                passed.start()
                sends.append(passed)
        for t in range(n):
            copy(t, 0, sibling, me).wait_recv()
            for j, chip in enumerate(chips):
                copy(t, 4 + j, (*chip, 1 - c), me).wait_recv()
        for cp in sends:
            cp.wait_send()
        for cp in mine:
            cp.wait()

    return pl.pallas_call(
        body, out_shape=tuple(out_shapes), in_specs=[ANY] * (n + len(deps)), out_specs=tuple([ANY] * n),
        scratch_shapes=[pltpu.SemaphoreType.DMA((n, 7)), pltpu.SemaphoreType.DMA((n, 7)), pltpu.SemaphoreType.DMA((n,))],
        name=name)(*locals_, *deps)


def _adamw(name, me, lands, owns, w, m, v, *, grid, land_specs, own_specs, p_spec):
    n_land = len(lands)

    def body(me_ref, *refs):
        land_refs, own_refs = refs[:n_land], refs[n_land:n_land + len(owns)]
        w_ref, m_ref, v_ref, g_ref, delta_ref, m_out, v_out = refs[n_land + len(owns):]
        ncols = w_ref.shape[-1]
        sums = []
        for i, land_ref in enumerate(land_refs):
            g = own_refs[i][...].astype(F32) if owns else land_ref[0].astype(F32)
            for s in range(0 if owns else 1, land_ref.shape[0]):
                g = g + land_ref[s].astype(F32)
            sums.append(g[:, :ncols])
        g = sums[0] if n_land == 1 else jnp.where(pl.program_id(0) == 0, sums[0], sums[1])
        m_new = ADAM_B1 * m_ref[...] + (1.0 - ADAM_B1) * g
        v_new = ADAM_B2 * v_ref[...] + (1.0 - ADAM_B2) * jnp.square(g)
        m_hat = m_new / (1.0 - ADAM_B1 ** ADAM_STEP)
        v_hat = v_new / (1.0 - ADAM_B2 ** ADAM_STEP)
        g_ref[...] = g
        delta_ref[...] = -ADAM_LR * (m_hat / (jnp.sqrt(v_hat) + ADAM_EPS) + ADAM_WD * w_ref[...])
        m_out[...] = m_new
        v_out[...] = v_new

    shape = jax.ShapeDtypeStruct(w.shape, F32)
    return pl.pallas_call(
        body, out_shape=(shape,) * 4,
        grid_spec=pltpu.PrefetchScalarGridSpec(
            num_scalar_prefetch=1, grid=grid, in_specs=list(land_specs) + list(own_specs) + [p_spec, p_spec, p_spec],
            out_specs=(p_spec,) * 4),
        name=name, compiler_params=_params())(me, *lands, *owns, w, m, v)


def _row(p, layer):
    return p[layer][None, :]


def _square(name, a, b, dims, out_dtype, deps=()):
    if a.shape == (D_MODEL, SEQ):
        return _matmul(name, a, b, grid=(2, 1), a_spec=pl.BlockSpec((512, SEQ), lambda i, k: (i, 0)),
                       b_spec=pl.BlockSpec((SEQ, D_MODEL), lambda i, k: (0, 0)),
                       o_spec=pl.BlockSpec((512, D_MODEL), lambda i, k: (i, 0)),
                       out_shape=jax.ShapeDtypeStruct((D_MODEL, D_MODEL), out_dtype), dims=NN, acc_shape=(8, 128),
                       deps=deps)
    return _matmul(name, a, b, grid=(SEQ // TM_MM, 1), a_spec=pl.BlockSpec((TM_MM, D_MODEL), lambda i, k: (i, 0)),
                   b_spec=pl.BlockSpec((D_MODEL, D_MODEL), lambda i, k: (0, 0)),
                   o_spec=pl.BlockSpec((TM_MM, D_MODEL), lambda i, k: (i, 0)),
                   out_shape=jax.ShapeDtypeStruct((SEQ, D_MODEL), out_dtype), dims=dims, acc_shape=(8, 128), deps=deps)


def _grouped_matmul(name, a_list, b, *, n_tiles, a_block, b_spec, o_spec, out_shape):
    n_groups = len(a_list)

    def a_spec(g):
        def index(j, i):
            mine = j // 3
            return (jnp.where(mine == g, i, jnp.where(mine < g, 0, n_tiles - 1)), 0)
        return pl.BlockSpec(a_block, index)

    def body(*refs):
        b_ref, o_ref = refs[n_groups], refs[n_groups + 1]
        mine = pl.program_id(0) // 3
        for g in range(n_groups):
            @pl.when(mine == g)
            def _(g=g):
                o_ref[...] = jnp.dot(refs[g][...], b_ref[...], preferred_element_type=F32).astype(o_ref.dtype)

    return pl.pallas_call(
        body, out_shape=out_shape, grid=(3 * n_groups, n_tiles), in_specs=[a_spec(g) for g in range(n_groups)] + [b_spec],
        out_specs=o_spec, name=name, compiler_params=_params())(*a_list, b)


def _qkv_fwd(name, hs, w):
    return _grouped_matmul(name, hs, w, n_tiles=SEQ // TM_MM, a_block=(TM_MM, D_MODEL),
                           b_spec=pl.BlockSpec((D_MODEL, D_MODEL), lambda j, i: (0, j)),
                           o_spec=pl.BlockSpec((TM_MM, D_MODEL), lambda j, i: (i, j)),
                           out_shape=jax.ShapeDtypeStruct((SEQ, 3 * len(hs) * D_MODEL), BF16))


def _qkv_dw(name, hs_t, dqkv):
    return _grouped_matmul(name, hs_t, dqkv, n_tiles=2, a_block=(512, SEQ),
                           b_spec=pl.BlockSpec((None, SEQ, D_MODEL), lambda j, i: (j, 0, 0)),
                           o_spec=pl.BlockSpec((512, D_MODEL), lambda j, i: (i, j)),
                           out_shape=jax.ShapeDtypeStruct((D_MODEL, 3 * len(hs_t) * D_MODEL), BF16))


def _proj_do_sorted(name, d_a, w_o):
    def body(da_ref, w_ref, *refs):
        value = lax.dot_general(da_ref[...], w_ref[...], NT, preferred_element_type=F32)
        refs[0][...] = value
        for j, dil in enumerate(SORTED):
            _sort_tile(refs[-1], value, dil, refs[1 + j])

    tile = pl.BlockSpec((TM, D_MODEL), lambda i: (i, 0))
    shapes = [jax.ShapeDtypeStruct((SEQ, D_MODEL), F32)] + [jax.ShapeDtypeStruct((dil, SEQ // dil, D_MODEL), F32) for dil in SORTED]
    result = pl.pallas_call(
        body, out_shape=tuple(shapes), grid=(SEQ // TM,),
        in_specs=[tile, pl.BlockSpec((D_MODEL, D_MODEL), lambda i: (0, 0))],
        out_specs=tuple([tile] + [_sorted_spec(dil) for dil in SORTED]), scratch_shapes=[_sort_scratch()],
        name=name, compiler_params=_params())(d_a, w_o)
    return [t.reshape(SEQ, D_MODEL) for t in result]


def _qkv_dh(name, dqkv, w, n_chunks, deps):
    return _matmul(name, dqkv, w, grid=(n_chunks // 3, SEQ // TM, 1),
                   a_spec=pl.BlockSpec((3, TM, D_MODEL), lambda g, i, k: (g, i, 0)),
                   b_spec=pl.BlockSpec((D_MODEL, 3 * D_MODEL), lambda g, i, k: (0, g)),
                   o_spec=pl.BlockSpec((None, TM, D_MODEL), lambda g, i, k: (g, i, 0)),
                   out_shape=jax.ShapeDtypeStruct((n_chunks // 3, SEQ, D_MODEL), F32), dims=NT, acc_shape=(8, 128),
                   deps=deps, inner=3)


def _local_step(x, target, norms, rpb, fetch, emit, deps):
    mix_pre, mix_post, ffn_pre, ffn_post = norms
    slopes = 2.0 ** (-8.0 * jnp.arange(1, N_HEADS + 1, dtype=F32) / N_HEADS)
    rpb_pad = jnp.pad(rpb, ((0, 0), (0, 1), (0, 128 - 31)))
    saved = []

    for layer in range(2):
        tag = f"l{layer}"
        if layer == 0:
            h = _rms_fwd(tag + "_norm_mix", x, _row(mix_pre, layer), out_dtype=BF16, deps=deps)
            hs = [h]
            table = _rpb_table(rpb_pad)
            w_qkv, w_o = fetch("na", [table, h], [h])
            qkv = _qkv_fwd(tag + "_qkv", hs, w_qkv)
            o, lse = _na_fwd(qkv, table)
            mixer = (hs, qkv, o, lse, table)
        else:
            hs = [t.reshape(SEQ, D_MODEL) for t in
                  _rms_fwd(tag + "_norm_mix", x, _row(mix_pre, layer), out_dtype=BF16, sorted_too=True)]
            w_qkv, w_o = fetch("dil", [x], [hs[0]])
            qkv = _qkv_fwd(tag + "_qkv", hs, w_qkv)
            outs, lses = zip(*[_dil_fwd(g, qkv, slopes * dil) for g, (_, dil) in enumerate(DIL_GROUPS)])
            merged, lse_total = _dil_merge(outs, lses)
            o = merged[0]
            mixer = (hs, qkv, merged, lses, lse_total)
        a = _square(tag + "_proj", o, w_o, NN, F32)
        x1 = _rms_fwd(tag + "_post_mix", a, _row(mix_post, layer), res=x)
        h2 = _rms_fwd(tag + "_norm_ffn", x1, _row(ffn_pre, layer), out_dtype=BF16)
        w_gu, w_down = fetch(f"ffn{layer}", [a], [h2])
        f, hg, hu, act_t = _ffn_fwd(tag + "_ffn", h2, w_gu, w_down)
        x2 = _rms_fwd(tag + "_post_ffn", f, _row(ffn_post, layer), res=x1)
        transposed = ([t.T for t in hs], o.astype(BF16).T, h2.T, act_t)
        saved.append((x, mixer, a, x1, transposed, hg, hu, f, w_qkv, w_o, w_gu, w_down))
        x = x2

    dx, loss = _loss_head("loss_head", x, target)
    d_norm = {k: [None, None] for k in ("mix_pre", "mix_post", "ffn_pre", "ffn_post")}
    d_rpb = None

    for layer in (1, 0):
        tag = f"b{layer}"
        x0, mixer, a, x1, (h_t, o_t, h2_t, act_t), hg, hu, f, w_qkv, w_o, w_gu, w_down = saved[layer]
        d_f, d_norm["ffn_post"][layer] = _rms_bwd(tag + "_post_ffn", f, _row(ffn_post, layer), [dx], out_dtype=BF16)
        dgu, d_h2 = _ffn_bwd(tag + "_ffn", d_f, w_gu, w_down, hg, hu)
        d_down = _matmul(
            tag + "_ffn_ddown", act_t, d_f, grid=(N_DEV, 1),
            a_spec=pl.BlockSpec((None, FF_PAD, SEQ), lambda d, k: (d, 0, 0)),
            b_spec=pl.BlockSpec((SEQ, D_MODEL), lambda d, k: (0, 0)),
            o_spec=pl.BlockSpec((None, FF_PAD, D_MODEL), lambda d, k: (d, 0, 0)),
            out_shape=jax.ShapeDtypeStruct((N_DEV, FF_PAD, D_MODEL), BF16), dims=NN, acc_shape=(8, 128))
        d_gu = _ffn_dgu(tag + "_ffn_dgu", h2_t, dgu)
        sent = emit(f"ffn{layer}", [d_gu, d_down])
        dx1, d_norm["ffn_pre"][layer] = _rms_bwd(tag + "_norm_ffn", x1, _row(ffn_pre, layer), [d_h2], res=dx, deps=sent)
        d_a, d_norm["mix_post"][layer] = _rms_bwd(tag + "_post_mix", a, _row(mix_post, layer), [dx1], out_dtype=BF16)
        d_wo = _square(tag + "_proj_dw", o_t, d_a, NN, BF16)
        if layer == 0:
            _, qkv, o, lse, table = mixer
            d_o = _square(tag + "_proj_do", d_a, w_o, NT, BF16)
            dqkv, gp = _na_bwd(qkv, table, d_o, lse)
            d_rpb = _rpb_grad(gp)[:, :15, :31]
            sent = emit("na", [_qkv_dw(tag + "_qkv_dw", h_t, dqkv), d_wo])
            d_h = _qkv_dh(tag + "_qkv_dh", dqkv, w_qkv, 3, sent)
            dx, d_norm["mix_pre"][layer] = _rms_bwd(tag + "_norm_mix", x0, _row(mix_pre, layer), [d_h[0]], res=dx1)
        else:
            _, qkv, merged, lses, lse_total = mixer
            d_o = _proj_do_sorted(tag + "_proj_do", d_a, w_o)
            dqkv = lax.empty((3 * len(DIL_GROUPS), SEQ, D_MODEL), BF16)
            for g, (_, dil) in enumerate(DIL_GROUPS):
                dqkv = _dil_bwd(g, qkv, slopes * dil, d_o[g], merged[g], lses[g], lse_total[g], dqkv)
            sent = emit("dil", [_qkv_dw(tag + "_qkv_dw", h_t, dqkv), d_wo])
            d_h = _qkv_dh(tag + "_qkv_dh", dqkv, w_qkv, 9, sent)
            dx, d_norm["mix_pre"][layer] = _rms_bwd(tag + "_norm_mix", x0, _row(mix_pre, layer), None, res=dx1, groups=d_h)

    d_gains = [jnp.concatenate(d_norm[k], axis=0) for k in ("mix_pre", "mix_post", "ffn_pre", "ffn_post")]
    return loss, dx, d_gains, d_rpb


RPB_SIZE = N_HEADS * 15 * 31


def _pack_small(gains, rpb, last=None):
    top = jnp.concatenate(gains, axis=0).reshape(64, 128)
    bottom = jnp.pad(rpb.reshape(-1), (0, 64 * 128 - RPB_SIZE))
    if last is not None:
        bottom = bottom + jnp.pad(last.reshape(1), (64 * 128 - 1, 0))
    return jnp.concatenate([top, bottom.reshape(64, 128)], axis=0)


def _unpack_small(p):
    gains = p[:64].reshape(4, 2, D_MODEL)
    rpb = p[64:].reshape(-1)[:RPB_SIZE].reshape(1, N_HEADS, 15, 31)
    return [gains[i] for i in range(4)], rpb


GROUPS = ("na", "ffn0", "dil", "ffn1")


def kernel(x, norm_mix_pre, norm_mix_post, norm_ffn_pre, norm_ffn_post, na_w_qkv, na_w_o, na_rpb, dil_w_qkv, dil_w_o, ffn_w_gate, ffn_w_up, ffn_w_down, loss_target, m_norm_mix_pre, m_norm_mix_post, m_norm_ffn_pre, m_norm_ffn_post, m_na_w_qkv, m_na_w_o, m_na_rpb, m_dil_w_qkv, m_dil_w_o, m_ffn_w_gate, m_ffn_w_up, m_ffn_w_down, v_norm_mix_pre, v_norm_mix_post, v_norm_ffn_pre, v_norm_ffn_post, v_na_w_qkv, v_na_w_o, v_na_rpb, v_dil_w_qkv, v_dil_w_o, v_ffn_w_gate, v_ffn_w_up, v_ffn_w_down):
    na_cols, dil_cols, o_rows = 3 * D_MODEL // N_DEV, 9 * D_MODEL // N_DEV, D_MODEL // N_DEV
    ff_pad = FF_PAD - FF_SHARD
    me = (4 * lax.axis_index("x") + 2 * lax.axis_index("y") + lax.axis_index("c")).astype(jnp.int32).reshape(1)

    full = {
        "na": [((D_MODEL, 3 * D_MODEL), _columns(na_cols)), ((N_DEV, o_rows, D_MODEL), _leading)],
        "dil": [((D_MODEL, 9 * D_MODEL), _columns(dil_cols)), ((N_DEV, o_rows, D_MODEL), _leading)],
        "ffn0": [((N_DEV, 2, D_MODEL, FF_PAD), _leading), ((N_DEV, FF_PAD, D_MODEL), _leading)],
        "ffn1": [((N_DEV, 2, D_MODEL, FF_PAD), _leading), ((N_DEV, FF_PAD, D_MODEL), _leading)],
    }
    block = {
        "na": [(D_MODEL, na_cols), (o_rows, D_MODEL)], "dil": [(D_MODEL, dil_cols), (o_rows, D_MODEL)],
        "ffn0": [(2, D_MODEL, FF_PAD), (FF_PAD, D_MODEL)], "ffn1": [(2, D_MODEL, FF_PAD), (FF_PAD, D_MODEL)],
    }

    land_shapes = [jax.ShapeDtypeStruct(full[g][t][0], BF16) for g in GROUPS for t in range(2)]
    windows = [full[g][t][1] for g in GROUPS for t in range(2)]
    shards, lands = _prep_weights(me, na_w_qkv, na_w_o, dil_w_qkv, dil_w_o, ffn_w_gate, ffn_w_up, ffn_w_down, land_shapes)
    sems, shards, lands = _gather_start("gather_start", shards, lands, windows, [2] * len(GROUPS))

    def fetch(group, early, late):
        gi = GROUPS.index(group)
        mine = slice(2 * gi, 2 * gi + 2)
        pass_sems, shards_g, lands_g = _gather_pass_on(f"gather_pass_{group}", sems[gi], shards[mine], lands[mine],
                                                       windows[mine], early)
        qkv, o = _gather_wait(f"gather_wait_{group}", sems[gi], pass_sems, shards_g, lands_g, windows[mine], late)
        return (qkv, o.reshape(D_MODEL, D_MODEL)) if group in ("na", "dil") else (qkv, o)

    def grad_source(group, t):
        return _columns(block[group][0][1]) if (group in ("na", "dil") and t == 0) else _leading

    in_flight = {}

    def emit(group, grads):
        if group in ("na", "dil"):
            grads = [grads[0], grads[1].reshape(N_DEV, o_rows, D_MODEL)]
        sets = [(t, grad_source(group, t), t, _by_distance) for t in range(2)]
        landing = [lax.empty((N_DEV - 1,) + block[group][t], BF16) for t in range(2)]
        sems_g, grads, landing, tok = _send_start(f"exchange_start_{group}", grads, landing, [sets])
        in_flight[group] = (sems_g[0], grads, landing, sets)
        return [tok]

    norms = (norm_mix_pre, norm_mix_post, norm_ffn_pre, norm_ffn_post)
    loss, grad_x, d_gains, d_rpb = _local_step(x[0], loss_target[0], norms, na_rpb[0], fetch, emit, [shards[0]])

    landed, sent = {}, {}

    def wait_for(group, after):
        sems_g, grads, landing, sets = in_flight[group]
        sent[group], landed[group] = _send_wait(f"exchange_wait_{group}", sems_g, grads, landing, sets, after)

    for group in ("ffn1", "dil", "ffn0"):
        wait_for(group, [grad_x])

    def one(rows, tile, ncols, columns):
        own = (pl.BlockSpec((tile, ncols), lambda i, me: (i, me[0])) if columns
               else pl.BlockSpec((None, tile, ncols), lambda i, me: (me[0], i, 0)))
        return dict(grid=(rows // tile,), land_specs=[pl.BlockSpec((N_DEV - 1, tile, ncols), lambda i, me: (0, i, 0))],
                    own_specs=[own], p_spec=pl.BlockSpec((None, tile, ncols), lambda i, me: (0, i, 0)))

    def layered(block_shape, index, p_block, n_tiles):
        def specs(lead_size, lead):
            shape = (lead_size,) + block_shape
            return [pl.BlockSpec(shape, lambda l, r, me: index(lead(me), jnp.where(l == 0, r, n_tiles - 1))),
                    pl.BlockSpec(shape, lambda l, r, me: index(lead(me), jnp.where(l == 0, 0, r)))]
        return dict(grid=(2, n_tiles), land_specs=specs(N_DEV - 1, lambda me: 0), own_specs=specs(None, lambda me: me[0]),
                    p_spec=pl.BlockSpec(p_block, lambda l, r, me: (l, r, 0)))

    gu_lands, gu_owns = [landed["ffn0"][0], landed["ffn1"][0]], [sent["ffn0"][0], sent["ffn1"][0]]
    down_lands, down_owns = [landed["ffn0"][1], landed["ffn1"][1]], [sent["ffn0"][1], sent["ffn1"][1]]
    updates = {
        "dil_w_qkv": _adamw("adamw_dil_qkv", me, [landed["dil"][0]], [sent["dil"][0]], dil_w_qkv, m_dil_w_qkv, v_dil_w_qkv,
                            **one(D_MODEL, 128, dil_cols, True)),
        "dil_w_o": _adamw("adamw_dil_o", me, [landed["dil"][1]], [sent["dil"][1]], dil_w_o, m_dil_w_o, v_dil_w_o,
                          **one(o_rows, o_rows, D_MODEL, False)),
        "ffn_w_gate": _adamw("adamw_gate", me, gu_lands, gu_owns, ffn_w_gate, m_ffn_w_gate, v_ffn_w_gate,
                             **layered((None, 128, FF_PAD), lambda lead, r: (lead, 0, r, 0), (None, 128, FF_SHARD), 8)),
        "ffn_w_up": _adamw("adamw_up", me, gu_lands, gu_owns, ffn_w_up, m_ffn_w_up, v_ffn_w_up,
                           **layered((None, 128, FF_PAD), lambda lead, r: (lead, 1, r, 0), (None, 128, FF_SHARD), 8)),
        "ffn_w_down": _adamw("adamw_down", me, down_lands, down_owns, ffn_w_down, m_ffn_w_down, v_ffn_w_down,
                             **layered((176, D_MODEL), lambda lead, r: (lead, r, 0), (None, 176, D_MODEL), 2)),
    }
    done = [u[0] for u in updates.values()]
    small = _all_gather("gather_small", [_pack_small(d_gains, d_rpb, loss)], [jax.ShapeDtypeStruct((N_DEV, 128, 128), F32)],
                        [_leading], deps=done)[0]
    wait_for("na", [small])
    updates["na_w_qkv"] = _adamw("adamw_na_qkv", me, [landed["na"][0]], [sent["na"][0]], na_w_qkv, m_na_w_qkv, v_na_w_qkv,
                                 **one(D_MODEL, 256, na_cols, True))
    updates["na_w_o"] = _adamw("adamw_na_o", me, [landed["na"][1]], [sent["na"][1]], na_w_o, m_na_w_o, v_na_w_o,
                               **one(o_rows, o_rows, D_MODEL, False))
    gains = [norm_mix_pre, norm_mix_post, norm_ffn_pre, norm_ffn_post]
    m_gains = [m_norm_mix_pre, m_norm_mix_post, m_norm_ffn_pre, m_norm_ffn_post]
    v_gains = [v_norm_mix_pre, v_norm_mix_post, v_norm_ffn_pre, v_norm_ffn_post]
    packed = _adamw("adamw_small", me, [small], (), _pack_small(gains, na_rpb)[None], _pack_small(m_gains, m_na_rpb)[None],
                    _pack_small(v_gains, v_na_rpb)[None], grid=(1,),
                    land_specs=[pl.BlockSpec((N_DEV, 128, 128), lambda i, me: (0, 0, 0))], own_specs=[],
                    p_spec=pl.BlockSpec((None, 128, 128), lambda i, me: (0, 0, 0)))
    small_out = [_unpack_small(p[0]) for p in packed]

    order = ["na_w_qkv", "na_w_o", "na_rpb", "dil_w_qkv", "dil_w_o", "ffn_w_gate", "ffn_w_up", "ffn_w_down"]
    result = [packed[0][0, 127, 127], grad_x[None]]
    for kind in range(4):
        gains_k, rpb_k = small_out[kind]
        result += gains_k
        result += [rpb_k if name == "na_rpb" else updates[name][kind] for name in order]
    return tuple(result)
```

```python
import functools

import jax
import jax.numpy as jnp
from jax import lax
from jax.experimental import pallas as pl
from jax.experimental.pallas import tpu as pltpu

F32 = jnp.float32
BF16 = jnp.bfloat16
MESH = pl.DeviceIdType.MESH
ANY = pl.BlockSpec(memory_space=pl.ANY)
HBM = pl.BlockSpec(memory_space=pltpu.HBM)
SEM = pl.BlockSpec(memory_space=pltpu.SEMAPHORE)
EFFECT = pltpu.SideEffectType.DATAFLOW_SIDE_EFFECTING

N_DEV = 8
SEQ = 2048
D_MODEL = 1024
N_HEADS = 16
HEAD_DIM = 64
GRID_W = 64
NA_ROWS = 8
SEQ_ROWS = SEQ // GRID_W
DIL_GROUPS = ((128, 1), (512, 4), (2048, 16))
BAND = 128
RADIUS = 64
FF_SHARD = 352
FF_PAD = 384
RMS_EPS = 1e-6
NEG_INF = -1e30
Q_SCALE = HEAD_DIM ** -0.5

ADAM_LR = 0.001
ADAM_B1 = 0.9
ADAM_B2 = 0.999
ADAM_EPS = 1e-08
ADAM_WD = 0.01
ADAM_STEP = 10

VMEM_LIMIT = 56 * 1024 * 1024
TM = 256
TM_MM = 1024

NN = (((1,), (0,)), ((), ()))
NT = (((1,), (1,)), ((), ()))
TN = (((0,), (0,)), ((), ()))


def _params():
    return pltpu.CompilerParams(vmem_limit_bytes=VMEM_LIMIT)


def _matmul(name, a, b, *, grid, a_spec, b_spec, o_spec, out_shape, dims, acc_shape, deps=(), inner=1):
    nk = grid[-1]
    kaxis = len(grid) - 1

    def body(a_ref, b_ref, *rest):
        o_ref, acc_ref = rest[-2], rest[-1]
        if inner == 1:
            part = lax.dot_general(a_ref[...].astype(BF16), b_ref[...].astype(BF16), dims, preferred_element_type=F32)
        elif len(b_ref.shape) == 2:
            a_all = jnp.concatenate([a_ref[j].astype(BF16) for j in range(inner)], axis=1)
            part = lax.dot_general(a_all, b_ref[...].astype(BF16), dims, preferred_element_type=F32)
        else:
            part = sum(lax.dot_general(a_ref[j].astype(BF16), b_ref[j].astype(BF16), dims, preferred_element_type=F32)
                       for j in range(inner))
        if nk == 1:
            o_ref[...] = part.astype(o_ref.dtype)
        else:
            k = pl.program_id(kaxis)

            @pl.when(k == 0)
            def _():
                acc_ref[...] = part

            @pl.when(k > 0)
            def _():
                acc_ref[...] += part

            @pl.when(k == nk - 1)
            def _():
                o_ref[...] = acc_ref[...].astype(o_ref.dtype)

    return pl.pallas_call(
        body, out_shape=out_shape, grid=grid, in_specs=[a_spec, b_spec] + [ANY] * len(deps), out_specs=o_spec,
        scratch_shapes=[pltpu.VMEM(acc_shape, F32)], name=name, compiler_params=_params())(a, b, *deps)


SORTED = tuple(d for _, d in DIL_GROUPS if d > 1)
LANE_CHUNKS = D_MODEL // 128


def _sort_scratch(tm=TM):
    return pltpu.VMEM((LANE_CHUNKS, tm, 128), F32)


def _sorted_view(t, dil):
    return t.reshape(dil, SEQ // dil, D_MODEL)


def _sorted_spec(dil, lead=(), tm=TM):
    return pl.BlockSpec((None,) * len(lead) + (dil, tm // dil, D_MODEL), lambda i: tuple(lead) + (0, i, 0))


def _sort_tile(scratch, value, dil, out_ref):
    tm = value.shape[0]
    for c in range(LANE_CHUNKS):
        scratch[c] = value[:, 128 * c:128 * (c + 1)]
    for r in range(dil):
        rows = [scratch.at[c][pl.ds(r, tm // dil, stride=dil), :] for c in range(LANE_CHUNKS)]
        out_ref[r] = jnp.concatenate(rows, axis=1).astype(out_ref.dtype)


def _unsort_tile(scratch, in_ref, dil):
    for r in range(dil):
        value = in_ref[r].astype(F32)
        for c in range(LANE_CHUNKS):
            scratch.at[c][pl.ds(r, value.shape[0], stride=dil), :] = value[:, 128 * c:128 * (c + 1)]
    return jnp.concatenate([scratch[c] for c in range(LANE_CHUNKS)], axis=1)


def _rms_fwd(name, x, g, res=None, out_dtype=F32, deps=(), sorted_too=False):
    n_tiles = SEQ // TM
    has_res = res is not None
    n_in = 2 + has_res + len(deps)

    def body(*refs):
        x_ref, g_ref = refs[0], refs[1]
        xv = x_ref[...]
        r = lax.rsqrt(jnp.mean(xv * xv, axis=-1, keepdims=True) + RMS_EPS)
        y = xv * r * g_ref[...]
        if has_res:
            y = refs[2][...] + y
        refs[n_in][...] = y.astype(out_dtype)
        if sorted_too:
            for j, dil in enumerate(SORTED):
                _sort_tile(refs[-1], y, dil, refs[n_in + 1 + j])

    tile = pl.BlockSpec((TM, D_MODEL), lambda i: (i, 0))
    gspec = pl.BlockSpec((1, D_MODEL), lambda i: (0, 0))
    ins = [x, g] + ([res] if has_res else []) + list(deps)
    specs = [tile, gspec] + ([tile] if has_res else []) + [ANY] * len(deps)
    shapes, out_specs = [jax.ShapeDtypeStruct((SEQ, D_MODEL), out_dtype)], [tile]
    if sorted_too:
        shapes += [jax.ShapeDtypeStruct((dil, SEQ // dil, D_MODEL), out_dtype) for dil in SORTED]
        out_specs += [_sorted_spec(dil) for dil in SORTED]
    result = pl.pallas_call(
        body, out_shape=tuple(shapes), grid=(n_tiles,), in_specs=specs, out_specs=tuple(out_specs),
        scratch_shapes=[_sort_scratch()] if sorted_too else [], name=name, compiler_params=_params())(*ins)
    return result if sorted_too else result[0]


def _rms_bwd(name, x, g, dys, res=None, out_dtype=F32, groups=None, deps=()):
    n_tiles = SEQ // TM
    n_dy = len(dys) if groups is None else 1 + len(SORTED)
    has_res = res is not None

    def body(*refs):
        x_ref, g_ref = refs[0], refs[1]
        dy_refs = refs[2:2 + n_dy]
        res_ref = refs[2 + n_dy] if has_res else None
        first_out = 2 + n_dy + has_res + len(deps)
        dx_ref, dg_ref, acc_ref = refs[first_out:first_out + 3]
        i = pl.program_id(0)
        xv = x_ref[...]
        r = lax.rsqrt(jnp.mean(xv * xv, axis=-1, keepdims=True) + RMS_EPS)
        xn = xv * r
        dy = dy_refs[0][...].astype(F32)
        for j, extra in enumerate(dy_refs[1:]):
            dy = dy + (extra[...].astype(F32) if groups is None else _unsort_tile(refs[-1], extra, SORTED[j]))
        dyg = dy * g_ref[...]
        dx = r * (dyg - xn * jnp.mean(dyg * xn, axis=-1, keepdims=True))
        if has_res:
            dx = res_ref[...] + dx
        dx_ref[...] = dx.astype(dx_ref.dtype)
        part = jnp.sum((dy * xn).reshape(TM // 8, 8, D_MODEL), axis=0)

        @pl.when(i == 0)
        def _():
            acc_ref[...] = part

        @pl.when(i > 0)
        def _():
            acc_ref[...] += part

        @pl.when(i == n_tiles - 1)
        def _():
            dg_ref[...] = jnp.broadcast_to(jnp.sum(acc_ref[...], axis=0, keepdims=True), (8, D_MODEL))

    tile = pl.BlockSpec((TM, D_MODEL), lambda i: (i, 0))
    gspec = pl.BlockSpec((1, D_MODEL), lambda i: (0, 0))
    if groups is None:
        dy_ins, dy_specs = list(dys), [tile] * n_dy
    else:
        dy_ins = [groups] + [groups.reshape(n_dy, dil, SEQ // dil, D_MODEL) for dil in SORTED]
        dy_specs = [pl.BlockSpec((None, TM, D_MODEL), lambda i: (0, i, 0))]
        dy_specs += [_sorted_spec(dil, lead=(1 + j,)) for j, dil in enumerate(SORTED)]
    ins = [x, g] + dy_ins + ([res] if has_res else []) + list(deps)
    specs = [tile, gspec] + dy_specs + ([tile] if has_res else []) + [ANY] * len(deps)
    dx, dg = pl.pallas_call(
        body, out_shape=(jax.ShapeDtypeStruct((SEQ, D_MODEL), out_dtype), jax.ShapeDtypeStruct((8, D_MODEL), F32)),
        grid=(n_tiles,), in_specs=specs,
        out_specs=(tile, pl.BlockSpec((8, D_MODEL), lambda i: (0, 0))),
        scratch_shapes=[pltpu.VMEM((8, D_MODEL), F32)] + ([_sort_scratch()] if groups is not None else []),
        name=name, compiler_params=_params())(*ins)
    return dx, dg[0:1]


def _loss_head(name, y, target):
    n_tiles = SEQ // TM

    def body(y_ref, t_ref, dy_ref, loss_ref, acc_ref):
        i = pl.program_id(0)
        diff = y_ref[...] - t_ref[...]
        dy_ref[...] = diff * (1.0 / D_MODEL)
        part = jnp.sum((diff * diff).reshape(TM // 8, 8, D_MODEL), axis=0)

        @pl.when(i == 0)
        def _():
            acc_ref[...] = part

        @pl.when(i > 0)
        def _():
            acc_ref[...] += part

        @pl.when(i == n_tiles - 1)
        def _():
            loss_ref[...] = jnp.full((8, 128), jnp.sum(acc_ref[...]) * (0.5 / D_MODEL), F32)

    tile = pl.BlockSpec((TM, D_MODEL), lambda i: (i, 0))
    dy, loss = pl.pallas_call(
        body, out_shape=(jax.ShapeDtypeStruct((SEQ, D_MODEL), F32), jax.ShapeDtypeStruct((8, 128), F32)),
        grid=(n_tiles,), in_specs=[tile, tile], out_specs=(tile, pl.BlockSpec((8, 128), lambda i: (0, 0))),
        scratch_shapes=[pltpu.VMEM((8, D_MODEL), F32)], name=name, compiler_params=_params())(y, target)
    return dy, loss[0, 0]


def _row_index(shape):
    return lax.broadcasted_iota(jnp.int32, shape, 0)


def _lane_index(shape):
    return lax.broadcasted_iota(jnp.int32, shape, len(shape) - 1)


def _skew_rows(t, direction):
    q = _row_index(t.shape) & (GRID_W - 1)
    for bit in range(6):
        step = 1 << bit
        shift = step if direction > 0 else 128 - step
        t = jnp.where((q & step) != 0, pltpu.roll(t, shift, 1), t)
    return t


def _rpb_table(rpb_pad):
    rows = 16 * GRID_W

    def body(r_ref, t_ref):
        lane = _lane_index((rows, 128))
        v = pltpu.roll(r_ref[...], 128 - 15, 1)
        t = _skew_rows(jnp.broadcast_to(v[:, None, :], (16, GRID_W, 128)).reshape(rows, 128), +1)
        t = jnp.where(lane < GRID_W, t, 0.0)
        below = jnp.concatenate([t[GRID_W:], jnp.zeros((GRID_W, 128), F32)], axis=0)
        first_col = jnp.clip((_row_index((rows, 128)) & (GRID_W - 1)) - 8, 0, GRID_W - 16)
        key_col = lane & (GRID_W - 1)
        in_window = (key_col >= first_col) & (key_col < first_col + 16)
        t_ref[...] = jnp.where(in_window, t + pltpu.roll(below, GRID_W, 1), NEG_INF).reshape(16, GRID_W, 128)

    return pl.pallas_call(
        body, out_shape=jax.ShapeDtypeStruct((N_HEADS, 16, GRID_W, 128), F32), grid=(N_HEADS,),
        in_specs=[pl.BlockSpec((None, 16, 128), lambda h: (h, 0, 0))],
        out_specs=pl.BlockSpec((None, 16, GRID_W, 128), lambda h: (h, 0, 0, 0)),
        name="rpb_table", compiler_params=_params())(rpb_pad)


def _rpb_grad(gp):
    rows = 16 * GRID_W

    def body(g_ref, o_ref):
        lane = _lane_index((rows, 128))
        g = g_ref[...].reshape(rows, 128)
        low = jnp.where(lane < GRID_W, g, 0.0)
        high = pltpu.roll(jnp.where(lane >= GRID_W, g, 0.0), GRID_W, 1)
        above = jnp.concatenate([jnp.zeros((GRID_W, 128), F32), high[:rows - GRID_W]], axis=0)
        diag = jnp.sum(_skew_rows(low + above, -1).reshape(16, GRID_W, 128), axis=1)
        o_ref[...] = pltpu.roll(diag, 15, 1)

    return pl.pallas_call(
        body, out_shape=jax.ShapeDtypeStruct((N_HEADS, 16, 128), F32), grid=(N_HEADS,),
        in_specs=[pl.BlockSpec((None, 16, GRID_W, 128), lambda h: (h, 0, 0, 0))],
        out_specs=pl.BlockSpec((None, 16, 128), lambda h: (h, 0, 0)),
        name="rpb_grad", compiler_params=_params())(gp)


NA_KEYS = NA_ROWS * GRID_W


def _na_window(i):
    first_row = jnp.clip(i - NA_ROWS // 2, 0, SEQ_ROWS - NA_ROWS)
    return pl.multiple_of(first_row * GRID_W, GRID_W), first_row - i + NA_ROWS - 1


NA_STEP = 32


def _head_masks():
    lane = _lane_index((1, 128))
    return (lane < HEAD_DIM, lane >= HEAD_DIM)


def _stack_heads(t, masks):
    zero = jnp.zeros_like(t)
    return jnp.concatenate([jnp.where(masks[0], t, zero), jnp.where(masks[1], t, zero)], axis=0)


def _unstack_heads(t, masks):
    n = t.shape[0] // 2
    return jnp.where(masks[0], t[:n], t[n:])


def _stack_columns(t):
    return jnp.concatenate([t[:, 0:1], t[:, HEAD_DIM:HEAD_DIM + 1]], axis=0)


def _na_scores(qs, kw, tp_ref, dr0):
    s = lax.dot_general(qs, kw, NT, preferred_element_type=F32)
    bias = jnp.concatenate(
        [jnp.concatenate([tp_ref[a, pl.ds(dr0 + 2 * c, 1), :, :].reshape(GRID_W, 128) for c in range(4)], axis=1)
         for a in range(2)], axis=0)
    return s + bias


def _na_specs():
    q_spec = pl.BlockSpec((NA_STEP * GRID_W, 128), lambda hp, i: (i, hp))
    k_spec = pl.BlockSpec((SEQ, 128), lambda hp, i: (0, 8 + hp))
    v_spec = pl.BlockSpec((SEQ, 128), lambda hp, i: (0, 16 + hp))
    tp_spec = pl.BlockSpec((2, 16, GRID_W, 128), lambda hp, i: (hp, 0, 0, 0))
    return q_spec, k_spec, v_spec, tp_spec


def _na_fwd(qkv, table):
    def body(q_ref, k_ref, v_ref, tp_ref, o_ref, lse_ref):
        masks = _head_masks()
        for r in range(NA_STEP):
            rows = slice(r * GRID_W, (r + 1) * GRID_W)
            start, dr0 = _na_window(pl.program_id(1) * NA_STEP + r)
            kw = k_ref[pl.ds(start, NA_KEYS), :]
            vw = v_ref[pl.ds(start, NA_KEYS), :]
            s = _na_scores(_stack_heads(q_ref[rows, :] * Q_SCALE, masks), kw, tp_ref, dr0)
            m = jnp.max(s, axis=-1, keepdims=True)
            p = jnp.exp(s - m)
            denom = jnp.sum(p, axis=-1, keepdims=True)
            out = jnp.dot(p.astype(BF16), vw, preferred_element_type=F32) / denom
            o_ref[rows, :] = _unstack_heads(out, masks).astype(o_ref.dtype)
            lse_ref[rows, :] = _unstack_heads(jnp.broadcast_to(m + jnp.log(denom), (2 * GRID_W, 128)), masks)

    q_spec, k_spec, v_spec, tp_spec = _na_specs()
    return pl.pallas_call(
        body, out_shape=(jax.ShapeDtypeStruct((SEQ, D_MODEL), BF16), jax.ShapeDtypeStruct((SEQ, D_MODEL), F32)),
        grid=(N_HEADS // 2, SEQ_ROWS // NA_STEP), in_specs=[q_spec, k_spec, v_spec, tp_spec],
        out_specs=(q_spec, q_spec), name="na_fwd", compiler_params=_params())(qkv, qkv, qkv, table)


def _na_bwd(qkv, table, d_out, lse):
    def body(q_ref, k_ref, v_ref, tp_ref, do_ref, lse_ref, dqkv_ref, gp_ref, dk_acc, dv_acc):
        step = pl.program_id(1)

        @pl.when(step == 0)
        def _():
            dk_acc[...] = jnp.zeros_like(dk_acc)
            dv_acc[...] = jnp.zeros_like(dv_acc)
            gp_ref[...] = jnp.zeros_like(gp_ref)

        masks = _head_masks()
        for r in range(NA_STEP):
            rows = slice(r * GRID_W, (r + 1) * GRID_W)
            i = step * NA_STEP + r
            start, dr0 = _na_window(i)
            kw = k_ref[pl.ds(start, NA_KEYS), :]
            vw = v_ref[pl.ds(start, NA_KEYS), :]
            qs = _stack_heads(q_ref[rows, :] * Q_SCALE, masks)
            dos = _stack_heads(do_ref[rows, :], masks)
            p = jnp.exp(_na_scores(qs, kw, tp_ref, dr0) - _stack_columns(lse_ref[rows, :]))
            dp = lax.dot_general(dos, vw, NT, preferred_element_type=F32)
            ds = p * (dp - jnp.sum(p * dp, axis=-1, keepdims=True))
            for a in range(2):
                for c in range(4):
                    gp_ref[a, pl.ds(dr0 + 2 * c, 1), :, :] += (
                        ds[a * GRID_W:(a + 1) * GRID_W, 128 * c:128 * (c + 1)].reshape(1, GRID_W, 128))
            dsb = ds.astype(BF16)
            dq = _unstack_heads(jnp.dot(dsb, kw, preferred_element_type=F32), masks) * Q_SCALE
            dqkv_ref[0, pl.ds(pl.multiple_of(i * GRID_W, GRID_W), GRID_W), :] = dq.astype(dqkv_ref.dtype)
            dk_acc[pl.ds(start, NA_KEYS), :] += lax.dot_general(dsb, qs, TN, preferred_element_type=F32)
            dv_acc[pl.ds(start, NA_KEYS), :] += lax.dot_general(p.astype(BF16), dos, TN, preferred_element_type=F32)

        @pl.when(step == SEQ_ROWS // NA_STEP - 1)
        def _():
            dqkv_ref[1] = dk_acc[...].astype(dqkv_ref.dtype)
            dqkv_ref[2] = dv_acc[...].astype(dqkv_ref.dtype)

    q_spec, k_spec, v_spec, tp_spec = _na_specs()
    return pl.pallas_call(
        body,
        out_shape=(jax.ShapeDtypeStruct((3, SEQ, D_MODEL), BF16), jax.ShapeDtypeStruct((N_HEADS, 16, GRID_W, 128), F32)),
        grid=(N_HEADS // 2, SEQ_ROWS // NA_STEP), in_specs=[q_spec, k_spec, v_spec, tp_spec, q_spec, q_spec],
        out_specs=(pl.BlockSpec((3, SEQ, 128), lambda hp, i: (0, 0, hp)), tp_spec),
        scratch_shapes=[pltpu.VMEM((SEQ, 128), F32), pltpu.VMEM((SEQ, 128), F32)],
        name="na_bwd", compiler_params=_params())(qkv, qkv, qkv, table, d_out, lse)


DIL_STEP = 16


def _dil_geometry(group):
    dil = DIL_GROUPS[group][1]
    sub_len = SEQ // dil
    blocks = sub_len // BAND
    return dil, sub_len, max(blocks // DIL_STEP, 1), max(DIL_STEP // blocks, 1), min(2 * BAND, sub_len)


def _dil_block(step, r, sub_len, subs):
    per_sub = DIL_STEP // subs
    return (r // per_sub) * sub_len, step * per_sub + r % per_sub


def _dil_window(b, sub_len, n_keys):
    if n_keys == sub_len:
        return 0
    return pl.multiple_of(jnp.clip(b * BAND - RADIUS, 0, sub_len - n_keys), RADIUS)


def _dil_bias(b, start, n_keys, slope_ref, hp):
    row = _row_index((2 * BAND, n_keys))
    qpos = b * BAND + (row & (BAND - 1))
    kpos = start + _lane_index((2 * BAND, n_keys))
    dist = jnp.abs(qpos - kpos)
    slope = jnp.where(row < BAND, slope_ref[2 * hp], slope_ref[2 * hp + 1])
    return slope * dist.astype(F32), dist <= RADIUS


def _dil_scores(qs, kw, penalty, valid):
    return jnp.where(valid, lax.dot_general(qs, kw, NT, preferred_element_type=F32) - penalty, NEG_INF)


def _dil_specs(group):
    dil, sub_len, steps, subs, _ = _dil_geometry(group)
    col = group * 24
    rows = DIL_STEP * BAND
    q_spec = pl.BlockSpec((rows, 128), lambda n, hp, b: (n * steps + b, col + hp))
    k_spec = pl.BlockSpec((subs * sub_len, 128), lambda n, hp, b: (n, col + 8 + hp))
    v_spec = pl.BlockSpec((subs * sub_len, 128), lambda n, hp, b: (n, col + 16 + hp))
    tile = pl.BlockSpec((rows, 128), lambda n, hp, b: (n * steps + b, hp))
    smem = pl.BlockSpec(memory_space=pltpu.SMEM)
    return (dil // subs, N_HEADS // 2, steps), q_spec, k_spec, v_spec, tile, smem


def _dil_fwd(group, qkv, slopes):
    _, sub_len, _, subs, n_keys = _dil_geometry(group)

    def body(q_ref, k_ref, v_ref, slope_ref, o_ref, lse_ref):
        hp = pl.program_id(1)
        masks = _head_masks()
        for r in range(DIL_STEP):
            rows = slice(r * BAND, (r + 1) * BAND)
            base, b = _dil_block(pl.program_id(2), r, sub_len, subs)
            start = _dil_window(b, sub_len, n_keys)
            kw = k_ref[pl.ds(base + start, n_keys), :]
            vw = v_ref[pl.ds(base + start, n_keys), :]
            penalty, valid = _dil_bias(b, start, n_keys, slope_ref, hp)
            s = _dil_scores(_stack_heads(q_ref[rows, :] * Q_SCALE, masks), kw, penalty, valid)
            m = jnp.max(s, axis=-1, keepdims=True)
            p = jnp.exp(s - m)
            denom = jnp.sum(p, axis=-1, keepdims=True)
            out = jnp.dot(p.astype(BF16), vw, preferred_element_type=F32) / denom
            o_ref[rows, :] = _unstack_heads(out, masks).astype(o_ref.dtype)
            lse_ref[rows, :] = _unstack_heads(jnp.broadcast_to(m + jnp.log(denom), (2 * BAND, 128)), masks)

    grid, q_spec, k_spec, v_spec, tile, smem = _dil_specs(group)
    return pl.pallas_call(
        body, out_shape=(jax.ShapeDtypeStruct((SEQ, D_MODEL), BF16), jax.ShapeDtypeStruct((SEQ, D_MODEL), F32)),
        grid=grid, in_specs=[q_spec, k_spec, v_spec, smem], out_specs=(tile, tile),
        name=f"dil_fwd_{group}", compiler_params=_params())(qkv, qkv, qkv, slopes)


def _dil_merge(outs, lses):
    n_sorted = len(SORTED)

    def body(*refs):
        o_refs, l_refs = refs[:3], refs[3:6]
        out_refs, lse_refs, scratch = refs[6:7 + n_sorted], refs[7 + n_sorted:8 + 2 * n_sorted], refs[-1]
        os_ = [o_refs[0][...]] + [_unsort_tile(scratch, o_refs[1 + j], dil) for j, dil in enumerate(SORTED)]
        ls = [l_refs[0][...]] + [_unsort_tile(scratch, l_refs[1 + j], dil) for j, dil in enumerate(SORTED)]
        m = jnp.maximum(jnp.maximum(ls[0], ls[1]), ls[2])
        es = [jnp.exp(v - m) for v in ls]
        total = es[0] + es[1] + es[2]
        merged = (es[0] * os_[0] + es[1] * os_[1] + es[2] * os_[2]) / total
        lse = m + jnp.log(total)
        out_refs[0][...] = merged
        lse_refs[0][...] = lse
        for j, dil in enumerate(SORTED):
            _sort_tile(scratch, merged, dil, out_refs[1 + j])
            _sort_tile(scratch, lse, dil, lse_refs[1 + j])

    tm = 256
    tile = pl.BlockSpec((tm, D_MODEL), lambda i: (i, 0))
    specs = [tile] + [_sorted_spec(dil, tm=tm) for dil in SORTED]
    shapes = [jax.ShapeDtypeStruct((SEQ, D_MODEL), F32)] + [jax.ShapeDtypeStruct((dil, SEQ // dil, D_MODEL), F32) for dil in SORTED]
    views = lambda ts: [ts[0]] + [_sorted_view(t, dil) for t, dil in zip(ts[1:], SORTED)]
    result = pl.pallas_call(
        body, out_shape=tuple(shapes * 2), grid=(SEQ // tm,), in_specs=specs * 2, out_specs=tuple(specs * 2),
        scratch_shapes=[_sort_scratch(tm)], name="dil_merge", compiler_params=_params())(*views(outs), *views(lses))
    flat = [t.reshape(SEQ, D_MODEL) for t in result]
    return flat[:1 + n_sorted], flat[1 + n_sorted:]


def _dil_bwd(group, qkv, slopes, d_out, out, lse_group, lse_total, into):
    _, sub_len, steps, subs, n_keys = _dil_geometry(group)

    def body(q_ref, k_ref, v_ref, slope_ref, do_ref, o_ref, lg_ref, lt_ref, into_ref, dqkv_ref, dk_acc, dv_acc):
        hp, step = pl.program_id(1), pl.program_id(2)

        @pl.when(step == 0)
        def _():
            dk_acc[...] = jnp.zeros_like(dk_acc)
            dv_acc[...] = jnp.zeros_like(dv_acc)

        masks = _head_masks()
        for r in range(DIL_STEP):
            rows = slice(r * BAND, (r + 1) * BAND)
            base, b = _dil_block(step, r, sub_len, subs)
            start = _dil_window(b, sub_len, n_keys)
            keys = pl.ds(base + start, n_keys)
            kw = k_ref[keys, :]
            vw = v_ref[keys, :]
            penalty, valid = _dil_bias(b, start, n_keys, slope_ref, hp)
            qs = _stack_heads(q_ref[rows, :] * Q_SCALE, masks)
            lse2 = lg_ref[rows, :]
            weight = jnp.exp(lse2 - lt_ref[rows, :])
            do2 = do_ref[rows, :]
            dogs = _stack_heads((weight * do2).astype(BF16), masks)
            delta = _stack_columns(weight) * jnp.sum(_stack_heads(do2 * o_ref[rows, :], masks), axis=-1, keepdims=True)
            p = jnp.exp(_dil_scores(qs, kw, penalty, valid) - _stack_columns(lse2))
            dp = lax.dot_general(dogs, vw, NT, preferred_element_type=F32)
            dsb = (p * (dp - delta)).astype(BF16)
            dq = _unstack_heads(jnp.dot(dsb, kw, preferred_element_type=F32), masks) * Q_SCALE
            dqkv_ref[0, pl.ds(pl.multiple_of(base + b * BAND, BAND), BAND), :] = dq.astype(dqkv_ref.dtype)
            dk_acc[keys, :] += lax.dot_general(dsb, qs, TN, preferred_element_type=F32)
            dv_acc[keys, :] += lax.dot_general(p.astype(BF16), dogs, TN, preferred_element_type=F32)

        @pl.when(step == steps - 1)
        def _():
            dqkv_ref[1] = dk_acc[...].astype(dqkv_ref.dtype)
            dqkv_ref[2] = dv_acc[...].astype(dqkv_ref.dtype)

    grid, q_spec, k_spec, v_spec, tile, smem = _dil_specs(group)
    return pl.pallas_call(
        body, out_shape=jax.ShapeDtypeStruct(into.shape, into.dtype), grid=grid,
        in_specs=[q_spec, k_spec, v_spec, smem, tile, tile, tile, tile, ANY],
        out_specs=pl.BlockSpec((3, subs * sub_len, 128), lambda n, hp, b: (group, n, hp)),
        scratch_shapes=[pltpu.VMEM((subs * sub_len, 128), F32), pltpu.VMEM((subs * sub_len, 128), F32)],
        input_output_aliases={8: 0}, name=f"dil_bwd_{group}", compiler_params=_params(),
    )(qkv, qkv, qkv, slopes, d_out, out, lse_group, lse_total, into)


def _accumulate_rows(acc_ref, i, first, part):
    rows = pl.ds(pl.multiple_of(i * TM_MM, TM_MM), TM_MM)

    @pl.when(first)
    def _():
        acc_ref[rows, :] = part

    @pl.when(jnp.logical_not(first))
    def _():
        acc_ref[rows, :] += part


def _ffn_specs():
    tile = pl.BlockSpec((TM_MM, D_MODEL), lambda d, i: (i, 0))
    gate = pl.BlockSpec((None, None, D_MODEL, FF_PAD), lambda d, i: (d, 0, 0, 0))
    up = pl.BlockSpec((None, None, D_MODEL, FF_PAD), lambda d, i: (d, 1, 0, 0))
    down = pl.BlockSpec((None, FF_PAD, D_MODEL), lambda d, i: (d, 0, 0))
    hidden = pl.BlockSpec((None, TM_MM, FF_PAD), lambda d, i: (d, i, 0))
    whole = pl.BlockSpec((SEQ, D_MODEL), lambda d, i: (0, 0))
    return tile, gate, up, down, hidden, whole


def _ffn_fwd(name, h, w_gu, w_down):
    def body(h_ref, wg_ref, wu_ref, wd_ref, f_ref, hg_ref, hu_ref, act_t_ref):
        hv = h_ref[...]
        hg = jnp.dot(hv, wg_ref[...], preferred_element_type=F32)
        hu = jnp.dot(hv, wu_ref[...], preferred_element_type=F32)
        act = hg * jax.nn.sigmoid(hg) * hu
        act_t_ref[...] = act.T.astype(act_t_ref.dtype)
        hg_ref[...] = hg.astype(hg_ref.dtype)
        hu_ref[...] = hu.astype(hu_ref.dtype)
        _accumulate_rows(f_ref, pl.program_id(1), pl.program_id(0) == 0,
                         jnp.dot(act.astype(BF16), wd_ref[...], preferred_element_type=F32))

    tile, gate, up, down, hidden, whole = _ffn_specs()
    shape = jax.ShapeDtypeStruct((N_DEV, SEQ, FF_PAD), BF16)
    return pl.pallas_call(
        body, out_shape=(jax.ShapeDtypeStruct((SEQ, D_MODEL), F32), shape, shape, jax.ShapeDtypeStruct((N_DEV, FF_PAD, SEQ), BF16)),
        grid=(N_DEV, SEQ // TM_MM), in_specs=[tile, gate, up, down],
        out_specs=(whole, hidden, hidden, pl.BlockSpec((None, FF_PAD, TM_MM), lambda d, i: (d, 0, i))),
        name=name, compiler_params=_params())(h, w_gu, w_gu, w_down)


def _ffn_dgu(name, h_t, dgu):
    def body(h_ref, dgu_ref, o_ref):
        both = jnp.dot(h_ref[...], jnp.concatenate([dgu_ref[0], dgu_ref[1]], axis=1), preferred_element_type=F32)
        o_ref[0] = both[:, :FF_PAD].astype(o_ref.dtype)
        o_ref[1] = both[:, FF_PAD:].astype(o_ref.dtype)

    return pl.pallas_call(
        body, out_shape=jax.ShapeDtypeStruct((N_DEV, 2, D_MODEL, FF_PAD), BF16), grid=(N_DEV,),
        in_specs=[pl.BlockSpec((D_MODEL, SEQ), lambda d: (0, 0)), pl.BlockSpec((2, None, SEQ, FF_PAD), lambda d: (0, d, 0, 0))],
        out_specs=pl.BlockSpec((None, 2, D_MODEL, FF_PAD), lambda d: (d, 0, 0, 0)),
        name=name, compiler_params=_params())(h_t, dgu)


def _ffn_bwd(name, d_f, w_gu, w_down, hg, hu):
    def body(df_ref, wg_ref, wu_ref, wd_ref, hg_ref, hu_ref, dgu_ref, dh_ref):
        dact = lax.dot_general(df_ref[...], wd_ref[...], NT, preferred_element_type=F32)
        hgv = hg_ref[...].astype(F32)
        sig = jax.nn.sigmoid(hgv)
        d_gate = (dact * hu_ref[...].astype(F32) * (sig * (1.0 + hgv * (1.0 - sig)))).astype(BF16)
        d_up = (dact * hgv * sig).astype(BF16)
        dgu_ref[0] = d_gate
        dgu_ref[1] = d_up
        part = lax.dot_general(jnp.concatenate([d_gate, d_up], axis=1), jnp.concatenate([wg_ref[...], wu_ref[...]], axis=1),
                               NT, preferred_element_type=F32)
        _accumulate_rows(dh_ref, pl.program_id(1), pl.program_id(0) == 0, part)

    tile, gate, up, down, hidden, whole = _ffn_specs()
    return pl.pallas_call(
        body, out_shape=(jax.ShapeDtypeStruct((2, N_DEV, SEQ, FF_PAD), BF16), jax.ShapeDtypeStruct((SEQ, D_MODEL), F32)),
        grid=(N_DEV, SEQ // TM_MM), in_specs=[tile, gate, up, down, hidden, hidden],
        out_specs=(pl.BlockSpec((2, None, TM_MM, FF_PAD), lambda d, i: (0, d, i, 0)), whole),
        name=name, compiler_params=_params())(d_f, w_gu, w_gu, w_down, hg, hu)


def _position():
    return lax.axis_index("x"), lax.axis_index("y"), lax.axis_index("c")


def _flat(p):
    return 4 * p[0] + 2 * p[1] + p[2]


def _peer(me, k):
    x, y, c = me
    return (1 - x if k & 4 else x, 1 - y if k & 2 else y, 1 - c if k & 1 else c)


def _columns(width):
    return lambda ref, d: ref.at[:, pl.ds(pl.multiple_of(d * width, 128), width)]


def _leading(ref, d):
    return ref.at[d]


def _whole(ref, d):
    return ref


def _by_sender(window):
    return lambda ref, sender, k: window(ref, sender)


def _by_distance(ref, sender, k):
    return ref.at[k - 1]


def _prep_weights(me, na_qkv, na_o, dil_qkv, dil_o, gate, up, down, land_shapes):
    na_cols, dil_cols = na_qkv.shape[-1], dil_qkv.shape[-1]
    o_rows = na_o.shape[1]
    tiles = 4
    rows, rows_o = D_MODEL // tiles, o_rows // tiles

    def body(me_ref, naq, nao, dq, do_, g0, u0, d0, g1, u1, d1, *outs):
        def put(t, index, value):
            outs[t][index] = value
            outs[8 + t][index] = value

        put(0, ..., naq[...].astype(BF16))
        put(1, ..., nao[...].astype(BF16))
        put(4, ..., dq[...].astype(BF16))
        put(5, ..., do_[...].astype(BF16))
        for t, (g, u, d) in ((2, (g0, u0, d0)), (6, (g1, u1, d1))):
            for j, part in enumerate((g, u)):
                put(t, (j, slice(None), slice(0, FF_SHARD)), part[...].astype(BF16))
                put(t, (j, slice(None), slice(FF_SHARD, FF_PAD)), jnp.zeros((rows, FF_PAD - FF_SHARD), BF16))
            put(t + 1, (slice(0, FF_SHARD), slice(None)), d[...].astype(BF16))
            put(t + 1, (slice(FF_SHARD, FF_PAD), slice(None)), jnp.zeros((FF_PAD - FF_SHARD, D_MODEL), BF16))

    def tiled(width):
        return pl.BlockSpec((None, rows, width), lambda i, me: (0, i, 0))

    def layer(l, width):
        return pl.BlockSpec((None, rows, width), lambda i, me: (l, i, 0))

    def whole_layer(l):
        return pl.BlockSpec((None, FF_SHARD, D_MODEL), lambda i, me: (l, 0, 0))

    in_specs = [tiled(na_cols), pl.BlockSpec((None, rows_o, D_MODEL), lambda i, me: (0, i, 0)), tiled(dil_cols),
                pl.BlockSpec((None, rows_o, D_MODEL), lambda i, me: (0, i, 0)),
                layer(0, FF_SHARD), layer(0, FF_SHARD), whole_layer(0), layer(1, FF_SHARD), layer(1, FF_SHARD), whole_layer(1)]
    o_shard = pl.BlockSpec((rows_o, D_MODEL), lambda i, me: (i, 0))
    o_land = pl.BlockSpec((None, rows_o, D_MODEL), lambda i, me: (me[0], i, 0))
    gu_shard = pl.BlockSpec((2, rows, FF_PAD), lambda i, me: (0, i, 0))
    gu_land = pl.BlockSpec((None, 2, rows, FF_PAD), lambda i, me: (me[0], 0, i, 0))
    down_shard = pl.BlockSpec((FF_PAD, D_MODEL), lambda i, me: (0, 0))
    down_land = pl.BlockSpec((None, FF_PAD, D_MODEL), lambda i, me: (me[0], 0, 0))

    def qkv_shard(width):
        return pl.BlockSpec((rows, width), lambda i, me: (i, 0))

    def qkv_land(width):
        return pl.BlockSpec((rows, width), lambda i, me: (i, me[0]))

    shard_specs = [qkv_shard(na_cols), o_shard, gu_shard, down_shard, qkv_shard(dil_cols), o_shard, gu_shard, down_shard]
    land_specs = [qkv_land(na_cols), o_land, gu_land, down_land, qkv_land(dil_cols), o_land, gu_land, down_land]
    shard_shapes = [jax.ShapeDtypeStruct(s, BF16) for s in
                    ((D_MODEL, na_cols), (o_rows, D_MODEL), (2, D_MODEL, FF_PAD), (FF_PAD, D_MODEL),
                     (D_MODEL, dil_cols), (o_rows, D_MODEL), (2, D_MODEL, FF_PAD), (FF_PAD, D_MODEL))]
    result = pl.pallas_call(
        body, out_shape=tuple(shard_shapes + list(land_shapes)),
        grid_spec=pltpu.PrefetchScalarGridSpec(num_scalar_prefetch=1, grid=(tiles,), in_specs=in_specs,
                                               out_specs=tuple(shard_specs + land_specs)),
        name="prep_weights", compiler_params=_params())(me, na_qkv, na_o, dil_qkv, dil_o, gate, up, down, gate, up, down)
    return list(result[:8]), list(result[8:])


def _remote_copies(sets, src_refs, land_refs, send_sems, recv_sems, outgoing):
    me = _position()
    copies = []
    for t, (si, src_of, li, dst_of) in enumerate(sets):
        for k in range(1, N_DEV):
            other = _peer(me, k)
            sender = me if outgoing else other
            copies.append(pltpu.make_async_remote_copy(
                src_ref=src_of(src_refs[si], _flat(other)), dst_ref=dst_of(land_refs[li], _flat(sender), k),
                send_sem=send_sems.at[(N_DEV - 1) * t + k - 1], recv_sem=recv_sems.at[(N_DEV - 1) * t + k - 1],
                device_id=other, device_id_type=MESH))
    return copies


def _send_start(name, srcs, lands, sets_by_group):
    n_src, n_land, n_groups = len(srcs), len(lands), len(sets_by_group)

    def body(*refs):
        src_refs, land_refs = refs[:n_src], refs[n_src:n_src + n_land]
        outs = refs[n_src + n_land:]
        for g, sets in enumerate(sets_by_group):
            for cp in _remote_copies(sets, src_refs, land_refs, outs[2 * g], outs[2 * g + 1], True):
                cp.start()
        outs[-1][...] = jnp.zeros_like(outs[-1])

    sem_shapes = []
    for sets in sets_by_group:
        sem_shapes += [pltpu.SemaphoreType.DMA((len(sets) * (N_DEV - 1),))] * 2
    thru = [pltpu.HBM(a.shape, a.dtype) for a in list(srcs) + list(lands)]
    n_sem = len(sem_shapes)
    result = pl.pallas_call(
        body, out_shape=tuple(sem_shapes + thru + [jax.ShapeDtypeStruct((8, 128), F32)]),
        in_specs=[HBM] * (n_src + n_land),
        out_specs=tuple([SEM] * n_sem + [HBM] * (n_src + n_land) + [pl.BlockSpec(memory_space=pltpu.VMEM)]),
        input_output_aliases={i: n_sem + i for i in range(n_src + n_land)},
        compiler_params=pltpu.CompilerParams(has_side_effects=EFFECT), name=name,
    )(*[pltpu.with_memory_space_constraint(a, pltpu.HBM) for a in list(srcs) + list(lands)])
    sems = [(result[2 * g], result[2 * g + 1]) for g in range(n_groups)]
    return sems, list(result[n_sem:n_sem + n_src]), list(result[n_sem + n_src:n_sem + n_src + n_land]), result[-1]


def _send_wait(name, sems, srcs, lands, sets, after):
    n_src, n_land = len(srcs), len(lands)

    def body(*refs):
        src_refs, land_refs = refs[:n_src], refs[n_src:n_src + n_land]
        send_sems, recv_sems = refs[n_src + n_land], refs[n_src + n_land + 1]
        for cp in _remote_copies(sets, src_refs, land_refs, send_sems, recv_sems, True):
            cp.wait_send()
        for cp in _remote_copies(sets, src_refs, land_refs, send_sems, recv_sems, False):
            cp.wait_recv()

    thru = [pltpu.HBM(a.shape, a.dtype) for a in list(srcs) + list(lands)]
    result = pl.pallas_call(
        body, out_shape=tuple(thru), in_specs=[HBM] * (n_src + n_land) + [SEM, SEM] + [ANY] * len(after),
        out_specs=tuple([HBM] * (n_src + n_land)), input_output_aliases={i: i for i in range(n_src + n_land)},
        compiler_params=pltpu.CompilerParams(has_side_effects=EFFECT), name=name,
    )(*srcs, *lands, sems[0], sems[1], *after)
    return list(result[:n_src]), list(result[n_src:])


DIRECT = (1, 2, 4, 6)
PASSED = DIRECT[1:]


def _hbm_passthrough(body, name, arrays, n_sem_in, sem_out_shapes, extra):
    n, n_out = len(arrays), len(sem_out_shapes)
    return pl.pallas_call(
        body, out_shape=tuple(list(sem_out_shapes) + [pltpu.HBM(a.shape, a.dtype) for a in arrays]),
        in_specs=[HBM] * n + [SEM] * n_sem_in + [ANY] * len(extra), out_specs=tuple([SEM] * n_out + [HBM] * n),
        input_output_aliases={i: n_out + i for i in range(n)},
        compiler_params=pltpu.CompilerParams(has_side_effects=EFFECT), name=name)


def _shard_copy(src_ref, land_ref, window, block, to, send_sem, recv_sem, from_shard):
    dst = window(land_ref, _flat(block))
    return pltpu.make_async_remote_copy(src_ref=src_ref if from_shard else dst, dst_ref=dst, send_sem=send_sem,
                                        recv_sem=recv_sem, device_id=to, device_id_type=MESH)


def _gather_start(name, shards, lands, windows, group_sizes):
    n = len(shards)

    def body(*refs):
        shard_refs, land_refs, outs = refs[:n], refs[n:2 * n], refs[2 * n:]
        me = _position()
        t = 0
        for g, size in enumerate(group_sizes):
            for local in range(size):
                for j, k in enumerate(DIRECT):
                    i = len(DIRECT) * local + j
                    _shard_copy(shard_refs[t], land_refs[t], windows[t], me, _peer(me, k), outs[2 * g].at[i],
                                outs[2 * g + 1].at[i], True).start()
                t += 1

    sem_shapes = [pltpu.SemaphoreType.DMA((len(DIRECT) * size,)) for size in group_sizes for _ in range(2)]
    arrays = [pltpu.with_memory_space_constraint(a, pltpu.HBM) for a in list(shards) + list(lands)]
    result = _hbm_passthrough(body, name, arrays, 0, sem_shapes, ())(*arrays)
    n_sem = len(sem_shapes)
    sems = [(result[2 * g], result[2 * g + 1]) for g in range(len(group_sizes))]
    return sems, list(result[n_sem:n_sem + n]), list(result[n_sem + n:])


def _gather_pass_on(name, sems, shards, lands, windows, after):
    n = len(shards)

    def body(*refs):
        shard_refs, land_refs = refs[:n], refs[n:2 * n]
        recv_sems = refs[2 * n + 1]
        pass_send, pass_recv = refs[2 * n + 2 + len(after)], refs[2 * n + 3 + len(after)]
        me = _position()
        sibling = _peer(me, 1)
        for t in range(n):
            for j, k in enumerate(PASSED):
                sender = _peer(me, k)
                arrived = len(DIRECT) * t + 1 + j
                _shard_copy(shard_refs[t], land_refs[t], windows[t], sender, me, refs[2 * n].at[arrived], recv_sems.at[arrived],
                            True).wait_recv()
                i = len(PASSED) * t + j
                _shard_copy(shard_refs[t], land_refs[t], windows[t], sender, sibling, pass_send.at[i], pass_recv.at[i],
                            False).start()

    sem_shapes = [pltpu.SemaphoreType.DMA((len(PASSED) * n,))] * 2
    result = _hbm_passthrough(body, name, list(shards) + list(lands), 2, sem_shapes, after)(
        *shards, *lands, sems[0], sems[1], *after)
    return (result[0], result[1]), list(result[2:2 + n]), list(result[2 + n:])


def _gather_wait(name, sems, pass_sems, shards, lands, windows, after):
    n = len(shards)

    def body(*refs):
        shard_refs, land_refs = refs[:n], refs[n:2 * n]
        send_sems, recv_sems, pass_send, pass_recv = refs[2 * n:2 * n + 4]
        me = _position()
        sibling = _peer(me, 1)
        for t in range(n):
            for j, k in enumerate(DIRECT):
                i = len(DIRECT) * t + j
                _shard_copy(shard_refs[t], land_refs[t], windows[t], me, _peer(me, k), send_sems.at[i], recv_sems.at[i],
                            True).wait_send()
            _shard_copy(shard_refs[t], land_refs[t], windows[t], sibling, me, send_sems.at[len(DIRECT) * t],
                        recv_sems.at[len(DIRECT) * t], True).wait_recv()
            for j, k in enumerate(PASSED):
                i = len(PASSED) * t + j
                _shard_copy(shard_refs[t], land_refs[t], windows[t], _peer(me, k), sibling, pass_send.at[i], pass_recv.at[i],
                            False).wait_send()
                _shard_copy(shard_refs[t], land_refs[t], windows[t], _peer(sibling, k), me, pass_send.at[i], pass_recv.at[i],
                            False).wait_recv()

    result = _hbm_passthrough(body, name, list(shards) + list(lands), 4, [], after)(
        *shards, *lands, sems[0], sems[1], pass_sems[0], pass_sems[1], *after)
    return list(result[n:])


def _all_gather(name, locals_, out_shapes, windows, deps=()):
    n = len(locals_)

    def body(*refs):
        src_refs, out_refs = refs[:n], refs[n + len(deps):2 * n + len(deps)]
        send_sems, recv_sems, local_sems = refs[2 * n + len(deps):]
        x, y, c = _position()
        me, sibling = (x, y, c), (x, y, 1 - c)
        chips = [(1 - x, y), (x, 1 - y), (1 - x, 1 - y)]

        def copy(t, k, block, to, from_local=False):
            dst = windows[t](out_refs[t], _flat(block))
            return pltpu.make_async_remote_copy(
                src_ref=src_refs[t] if from_local else dst, dst_ref=dst, send_sem=send_sems.at[t, k],
                recv_sem=recv_sems.at[t, k], device_id=to, device_id_type=MESH)

        mine = [pltpu.make_async_copy(src_refs[t], windows[t](out_refs[t], _flat(me)), local_sems.at[t]) for t in range(n)]
        sends = []
        for t in range(n):
            mine[t].start()
            sends.append(copy(t, 0, me, sibling, True))
            sends += [copy(t, 1 + j, me, (*chip, c), True) for j, chip in enumerate(chips)]
        for cp in sends:
            cp.start()
        for t in range(n):
            for j, chip in enumerate(chips):
                copy(t, 1 + j, (*chip, c), me).wait_recv()
                passed = copy(t, 4 + j, (*chip, c), sibling)
                passed.start()
                sends.append(passed)
        for t in range(n):
            copy(t, 0, sibling, me).wait_recv()
            for j, chip in enumerate(chips):
                copy(t, 4 + j, (*chip, 1 - c), me).wait_recv()
        for cp in sends:
            cp.wait_send()
        for cp in mine:
            cp.wait()

    return pl.pallas_call(
        body, out_shape=tuple(out_shapes), in_specs=[ANY] * (n + len(deps)), out_specs=tuple([ANY] * n),
        scratch_shapes=[pltpu.SemaphoreType.DMA((n, 7)), pltpu.SemaphoreType.DMA((n, 7)), pltpu.SemaphoreType.DMA((n,))],
        name=name)(*locals_, *deps)


def _adamw(name, me, lands, owns, w, m, v, *, grid, land_specs, own_specs, p_spec):
    n_land = len(lands)

    def body(me_ref, *refs):
        land_refs, own_refs = refs[:n_land], refs[n_land:n_land + len(owns)]
        w_ref, m_ref, v_ref, g_ref, delta_ref, m_out, v_out = refs[n_land + len(owns):]
        ncols = w_ref.shape[-1]
        sums = []
        for i, land_ref in enumerate(land_refs):
            g = own_refs[i][...].astype(F32) if owns else land_ref[0].astype(F32)
            for s in range(0 if owns else 1, land_ref.shape[0]):
                g = g + land_ref[s].astype(F32)
            sums.append(g[:, :ncols])
        g = sums[0] if n_land == 1 else jnp.where(pl.program_id(0) == 0, sums[0], sums[1])
        m_new = ADAM_B1 * m_ref[...] + (1.0 - ADAM_B1) * g
        v_new = ADAM_B2 * v_ref[...] + (1.0 - ADAM_B2) * jnp.square(g)
        m_hat = m_new / (1.0 - ADAM_B1 ** ADAM_STEP)
        v_hat = v_new / (1.0 - ADAM_B2 ** ADAM_STEP)
        g_ref[...] = g
        delta_ref[...] = -ADAM_LR * (m_hat / (jnp.sqrt(v_hat) + ADAM_EPS) + ADAM_WD * w_ref[...])
        m_out[...] = m_new
        v_out[...] = v_new

    shape = jax.ShapeDtypeStruct(w.shape, F32)
    return pl.pallas_call(
        body, out_shape=(shape,) * 4,
        grid_spec=pltpu.PrefetchScalarGridSpec(
            num_scalar_prefetch=1, grid=grid, in_specs=list(land_specs) + list(own_specs) + [p_spec, p_spec, p_spec],
            out_specs=(p_spec,) * 4),
        name=name, compiler_params=_params())(me, *lands, *owns, w, m, v)


def _row(p, layer):
    return p[layer][None, :]


def _square(name, a, b, dims, out_dtype, deps=()):
    if a.shape == (D_MODEL, SEQ):
        return _matmul(name, a, b, grid=(2, 1), a_spec=pl.BlockSpec((512, SEQ), lambda i, k: (i, 0)),
                       b_spec=pl.BlockSpec((SEQ, D_MODEL), lambda i, k: (0, 0)),
                       o_spec=pl.BlockSpec((512, D_MODEL), lambda i, k: (i, 0)),
                       out_shape=jax.ShapeDtypeStruct((D_MODEL, D_MODEL), out_dtype), dims=NN, acc_shape=(8, 128),
                       deps=deps)
    return _matmul(name, a, b, grid=(SEQ // TM_MM, 1), a_spec=pl.BlockSpec((TM_MM, D_MODEL), lambda i, k: (i, 0)),
                   b_spec=pl.BlockSpec((D_MODEL, D_MODEL), lambda i, k: (0, 0)),
                   o_spec=pl.BlockSpec((TM_MM, D_MODEL), lambda i, k: (i, 0)),
                   out_shape=jax.ShapeDtypeStruct((SEQ, D_MODEL), out_dtype), dims=dims, acc_shape=(8, 128), deps=deps)


def _grouped_matmul(name, a_list, b, *, n_tiles, a_block, b_spec, o_spec, out_shape):
    n_groups = len(a_list)

    def a_spec(g):
        def index(j, i):
            mine = j // 3
            return (jnp.where(mine == g, i, jnp.where(mine < g, 0, n_tiles - 1)), 0)
        return pl.BlockSpec(a_block, index)

    def body(*refs):
        b_ref, o_ref = refs[n_groups], refs[n_groups + 1]
        mine = pl.program_id(0) // 3
        for g in range(n_groups):
            @pl.when(mine == g)
            def _(g=g):
                o_ref[...] = jnp.dot(refs[g][...], b_ref[...], preferred_element_type=F32).astype(o_ref.dtype)

    return pl.pallas_call(
        body, out_shape=out_shape, grid=(3 * n_groups, n_tiles), in_specs=[a_spec(g) for g in range(n_groups)] + [b_spec],
        out_specs=o_spec, name=name, compiler_params=_params())(*a_list, b)


def _qkv_fwd(name, hs, w):
    return _grouped_matmul(name, hs, w, n_tiles=SEQ // TM_MM, a_block=(TM_MM, D_MODEL),
                           b_spec=pl.BlockSpec((D_MODEL, D_MODEL), lambda j, i: (0, j)),
                           o_spec=pl.BlockSpec((TM_MM, D_MODEL), lambda j, i: (i, j)),
                           out_shape=jax.ShapeDtypeStruct((SEQ, 3 * len(hs) * D_MODEL), BF16))


def _qkv_dw(name, hs_t, dqkv):
    return _grouped_matmul(name, hs_t, dqkv, n_tiles=2, a_block=(512, SEQ),
                           b_spec=pl.BlockSpec((None, SEQ, D_MODEL), lambda j, i: (j, 0, 0)),
                           o_spec=pl.BlockSpec((512, D_MODEL), lambda j, i: (i, j)),
                           out_shape=jax.ShapeDtypeStruct((D_MODEL, 3 * len(hs_t) * D_MODEL), BF16))


def _proj_do_sorted(name, d_a, w_o):
    def body(da_ref, w_ref, *refs):
        value = lax.dot_general(da_ref[...], w_ref[...], NT, preferred_element_type=F32)
        refs[0][...] = value
        for j, dil in enumerate(SORTED):
            _sort_tile(refs[-1], value, dil, refs[1 + j])

    tile = pl.BlockSpec((TM, D_MODEL), lambda i: (i, 0))
    shapes = [jax.ShapeDtypeStruct((SEQ, D_MODEL), F32)] + [jax.ShapeDtypeStruct((dil, SEQ // dil, D_MODEL), F32) for dil in SORTED]
    result = pl.pallas_call(
        body, out_shape=tuple(shapes), grid=(SEQ // TM,),
        in_specs=[tile, pl.BlockSpec((D_MODEL, D_MODEL), lambda i: (0, 0))],
        out_specs=tuple([tile] + [_sorted_spec(dil) for dil in SORTED]), scratch_shapes=[_sort_scratch()],
        name=name, compiler_params=_params())(d_a, w_o)
    return [t.reshape(SEQ, D_MODEL) for t in result]


def _qkv_dh(name, dqkv, w, n_chunks, deps):
    tm = TM_MM // 2
    return _matmul(name, dqkv, w, grid=(n_chunks // 3, SEQ // tm, 1),
                   a_spec=pl.BlockSpec((3, tm, D_MODEL), lambda g, i, k: (g, i, 0)),
                   b_spec=pl.BlockSpec((D_MODEL, 3 * D_MODEL), lambda g, i, k: (0, g)),
                   o_spec=pl.BlockSpec((None, tm, D_MODEL), lambda g, i, k: (g, i, 0)),
                   out_shape=jax.ShapeDtypeStruct((n_chunks // 3, SEQ, D_MODEL), F32), dims=NT, acc_shape=(8, 128),
                   deps=deps, inner=3)


def _local_step(x, target, norms, rpb, fetch, emit, deps):
    mix_pre, mix_post, ffn_pre, ffn_post = norms
    slopes = 2.0 ** (-8.0 * jnp.arange(1, N_HEADS + 1, dtype=F32) / N_HEADS)
    rpb_pad = jnp.pad(rpb, ((0, 0), (0, 1), (0, 128 - 31)))
    saved = []

    for layer in range(2):
        tag = f"l{layer}"
        if layer == 0:
            h = _rms_fwd(tag + "_norm_mix", x, _row(mix_pre, layer), out_dtype=BF16, deps=deps)
            hs = [h]
            table = _rpb_table(rpb_pad)
            w_qkv, w_o = fetch("na", [table, h], [h])
            qkv = _qkv_fwd(tag + "_qkv", hs, w_qkv)
            o, lse = _na_fwd(qkv, table)
            mixer = (hs, qkv, o, lse, table)
        else:
            hs = [t.reshape(SEQ, D_MODEL) for t in
                  _rms_fwd(tag + "_norm_mix", x, _row(mix_pre, layer), out_dtype=BF16, sorted_too=True)]
            w_qkv, w_o = fetch("dil", [x], [hs[0]])
            qkv = _qkv_fwd(tag + "_qkv", hs, w_qkv)
            outs, lses = zip(*[_dil_fwd(g, qkv, slopes * dil) for g, (_, dil) in enumerate(DIL_GROUPS)])
            merged, lse_total = _dil_merge(outs, lses)
            o = merged[0]
            mixer = (hs, qkv, merged, lses, lse_total)
        a = _square(tag + "_proj", o, w_o, NN, F32)
        x1 = _rms_fwd(tag + "_post_mix", a, _row(mix_post, layer), res=x)
        h2 = _rms_fwd(tag + "_norm_ffn", x1, _row(ffn_pre, layer), out_dtype=BF16)
        w_gu, w_down = fetch(f"ffn{layer}", [a], [h2])
        f, hg, hu, act_t = _ffn_fwd(tag + "_ffn", h2, w_gu, w_down)
        x2 = _rms_fwd(tag + "_post_ffn", f, _row(ffn_post, layer), res=x1)
        transposed = ([t.T for t in hs], o.astype(BF16).T, h2.T, act_t)
        saved.append((x, mixer, a, x1, transposed, hg, hu, f, w_qkv, w_o, w_gu, w_down))
        x = x2

    dx, loss = _loss_head("loss_head", x, target)
    d_norm = {k: [None, None] for k in ("mix_pre", "mix_post", "ffn_pre", "ffn_post")}
    d_rpb = None

    for layer in (1, 0):
        tag = f"b{layer}"
        x0, mixer, a, x1, (h_t, o_t, h2_t, act_t), hg, hu, f, w_qkv, w_o, w_gu, w_down = saved[layer]
        d_f, d_norm["ffn_post"][layer] = _rms_bwd(tag + "_post_ffn", f, _row(ffn_post, layer), [dx], out_dtype=BF16)
        dgu, d_h2 = _ffn_bwd(tag + "_ffn", d_f, w_gu, w_down, hg, hu)
        d_down = _matmul(
            tag + "_ffn_ddown", act_t, d_f, grid=(N_DEV, 1),
            a_spec=pl.BlockSpec((None, FF_PAD, SEQ), lambda d, k: (d, 0, 0)),
            b_spec=pl.BlockSpec((SEQ, D_MODEL), lambda d, k: (0, 0)),
            o_spec=pl.BlockSpec((None, FF_PAD, D_MODEL), lambda d, k: (d, 0, 0)),
            out_shape=jax.ShapeDtypeStruct((N_DEV, FF_PAD, D_MODEL), BF16), dims=NN, acc_shape=(8, 128))
        d_gu = _ffn_dgu(tag + "_ffn_dgu", h2_t, dgu)
        sent = emit(f"ffn{layer}", [d_gu, d_down])
        dx1, d_norm["ffn_pre"][layer] = _rms_bwd(tag + "_norm_ffn", x1, _row(ffn_pre, layer), [d_h2], res=dx, deps=sent)
        d_a, d_norm["mix_post"][layer] = _rms_bwd(tag + "_post_mix", a, _row(mix_post, layer), [dx1], out_dtype=BF16)
        d_wo = _square(tag + "_proj_dw", o_t, d_a, NN, BF16)
        if layer == 0:
            _, qkv, o, lse, table = mixer
            d_o = _square(tag + "_proj_do", d_a, w_o, NT, BF16)
            dqkv, gp = _na_bwd(qkv, table, d_o, lse)
            d_rpb = _rpb_grad(gp)[:, :15, :31]
            sent = emit("na", [_qkv_dw(tag + "_qkv_dw", h_t, dqkv), d_wo])
            d_h = _qkv_dh(tag + "_qkv_dh", dqkv, w_qkv, 3, sent)
            dx, d_norm["mix_pre"][layer] = _rms_bwd(tag + "_norm_mix", x0, _row(mix_pre, layer), [d_h[0]], res=dx1)
        else:
            _, qkv, merged, lses, lse_total = mixer
            d_o = _proj_do_sorted(tag + "_proj_do", d_a, w_o)
            dqkv = lax.empty((3 * len(DIL_GROUPS), SEQ, D_MODEL), BF16)
            for g, (_, dil) in enumerate(DIL_GROUPS):
                dqkv = _dil_bwd(g, qkv, slopes * dil, d_o[g], merged[g], lses[g], lse_total[g], dqkv)
            sent = emit("dil", [_qkv_dw(tag + "_qkv_dw", h_t, dqkv), d_wo])
            d_h = _qkv_dh(tag + "_qkv_dh", dqkv, w_qkv, 9, sent)
            dx, d_norm["mix_pre"][layer] = _rms_bwd(tag + "_norm_mix", x0, _row(mix_pre, layer), None, res=dx1, groups=d_h)

    d_gains = [jnp.concatenate(d_norm[k], axis=0) for k in ("mix_pre", "mix_post", "ffn_pre", "ffn_post")]
    return loss, dx, d_gains, d_rpb


RPB_SIZE = N_HEADS * 15 * 31


def _pack_small(gains, rpb, last=None):
    top = jnp.concatenate(gains, axis=0).reshape(64, 128)
    bottom = jnp.pad(rpb.reshape(-1), (0, 64 * 128 - RPB_SIZE))
    if last is not None:
        bottom = bottom + jnp.pad(last.reshape(1), (64 * 128 - 1, 0))
    return jnp.concatenate([top, bottom.reshape(64, 128)], axis=0)


def _unpack_small(p):
    gains = p[:64].reshape(4, 2, D_MODEL)
    rpb = p[64:].reshape(-1)[:RPB_SIZE].reshape(1, N_HEADS, 15, 31)
    return [gains[i] for i in range(4)], rpb


GROUPS = ("na", "ffn0", "dil", "ffn1")


def kernel(x, norm_mix_pre, norm_mix_post, norm_ffn_pre, norm_ffn_post, na_w_qkv, na_w_o, na_rpb, dil_w_qkv, dil_w_o, ffn_w_gate, ffn_w_up, ffn_w_down, loss_target, m_norm_mix_pre, m_norm_mix_post, m_norm_ffn_pre, m_norm_ffn_post, m_na_w_qkv, m_na_w_o, m_na_rpb, m_dil_w_qkv, m_dil_w_o, m_ffn_w_gate, m_ffn_w_up, m_ffn_w_down, v_norm_mix_pre, v_norm_mix_post, v_norm_ffn_pre, v_norm_ffn_post, v_na_w_qkv, v_na_w_o, v_na_rpb, v_dil_w_qkv, v_dil_w_o, v_ffn_w_gate, v_ffn_w_up, v_ffn_w_down):
    na_cols, dil_cols, o_rows = 3 * D_MODEL // N_DEV, 9 * D_MODEL // N_DEV, D_MODEL // N_DEV
    ff_pad = FF_PAD - FF_SHARD
    me = (4 * lax.axis_index("x") + 2 * lax.axis_index("y") + lax.axis_index("c")).astype(jnp.int32).reshape(1)

    full = {
        "na": [((D_MODEL, 3 * D_MODEL), _columns(na_cols)), ((N_DEV, o_rows, D_MODEL), _leading)],
        "dil": [((D_MODEL, 9 * D_MODEL), _columns(dil_cols)), ((N_DEV, o_rows, D_MODEL), _leading)],
        "ffn0": [((N_DEV, 2, D_MODEL, FF_PAD), _leading), ((N_DEV, FF_PAD, D_MODEL), _leading)],
        "ffn1": [((N_DEV, 2, D_MODEL, FF_PAD), _leading), ((N_DEV, FF_PAD, D_MODEL), _leading)],
    }
    block = {
        "na": [(D_MODEL, na_cols), (o_rows, D_MODEL)], "dil": [(D_MODEL, dil_cols), (o_rows, D_MODEL)],
        "ffn0": [(2, D_MODEL, FF_PAD), (FF_PAD, D_MODEL)], "ffn1": [(2, D_MODEL, FF_PAD), (FF_PAD, D_MODEL)],
    }

    land_shapes = [jax.ShapeDtypeStruct(full[g][t][0], BF16) for g in GROUPS for t in range(2)]
    windows = [full[g][t][1] for g in GROUPS for t in range(2)]
    shards, lands = _prep_weights(me, na_w_qkv, na_w_o, dil_w_qkv, dil_w_o, ffn_w_gate, ffn_w_up, ffn_w_down, land_shapes)
    sems, shards, lands = _gather_start("gather_start", shards, lands, windows, [2] * len(GROUPS))

    def fetch(group, early, late):
        gi = GROUPS.index(group)
        mine = slice(2 * gi, 2 * gi + 2)
        pass_sems, shards_g, lands_g = _gather_pass_on(f"gather_pass_{group}", sems[gi], shards[mine], lands[mine],
                                                       windows[mine], early)
        qkv, o = _gather_wait(f"gather_wait_{group}", sems[gi], pass_sems, shards_g, lands_g, windows[mine], late)
        return (qkv, o.reshape(D_MODEL, D_MODEL)) if group in ("na", "dil") else (qkv, o)

    def grad_source(group, t):
        return _columns(block[group][0][1]) if (group in ("na", "dil") and t == 0) else _leading

    in_flight = {}

    def emit(group, grads):
        if group in ("na", "dil"):
            grads = [grads[0], grads[1].reshape(N_DEV, o_rows, D_MODEL)]
        sets = [(t, grad_source(group, t), t, _by_distance) for t in range(2)]
        landing = [lax.empty((N_DEV - 1,) + block[group][t], BF16) for t in range(2)]
        sems_g, grads, landing, tok = _send_start(f"exchange_start_{group}", grads, landing, [sets])
        in_flight[group] = (sems_g[0], grads, landing, sets)
        return [tok]

    norms = (norm_mix_pre, norm_mix_post, norm_ffn_pre, norm_ffn_post)
    loss, grad_x, d_gains, d_rpb = _local_step(x[0], loss_target[0], norms, na_rpb[0], fetch, emit, [shards[0]])

    landed, sent = {}, {}

    def wait_for(group, after):
        sems_g, grads, landing, sets = in_flight[group]
        sent[group], landed[group] = _send_wait(f"exchange_wait_{group}", sems_g, grads, landing, sets, after)

    for group in ("ffn1", "dil", "ffn0"):
        wait_for(group, [grad_x])

    def one(rows, tile, ncols, columns):
        own = (pl.BlockSpec((tile, ncols), lambda i, me: (i, me[0])) if columns
               else pl.BlockSpec((None, tile, ncols), lambda i, me: (me[0], i, 0)))
        return dict(grid=(rows // tile,), land_specs=[pl.BlockSpec((N_DEV - 1, tile, ncols), lambda i, me: (0, i, 0))],
                    own_specs=[own], p_spec=pl.BlockSpec((None, tile, ncols), lambda i, me: (0, i, 0)))

    def layered(block_shape, index, p_block, n_tiles):
        def specs(lead_size, lead):
            shape = (lead_size,) + block_shape
            return [pl.BlockSpec(shape, lambda l, r, me: index(lead(me), jnp.where(l == 0, r, n_tiles - 1))),
                    pl.BlockSpec(shape, lambda l, r, me: index(lead(me), jnp.where(l == 0, 0, r)))]
        return dict(grid=(2, n_tiles), land_specs=specs(N_DEV - 1, lambda me: 0), own_specs=specs(None, lambda me: me[0]),
                    p_spec=pl.BlockSpec(p_block, lambda l, r, me: (l, r, 0)))

    gu_lands, gu_owns = [landed["ffn0"][0], landed["ffn1"][0]], [sent["ffn0"][0], sent["ffn1"][0]]
    down_lands, down_owns = [landed["ffn0"][1], landed["ffn1"][1]], [sent["ffn0"][1], sent["ffn1"][1]]
    updates = {
        "dil_w_qkv": _adamw("adamw_dil_qkv", me, [landed["dil"][0]], [sent["dil"][0]], dil_w_qkv, m_dil_w_qkv, v_dil_w_qkv,
                            **one(D_MODEL, 128, dil_cols, True)),
        "dil_w_o": _adamw("adamw_dil_o", me, [landed["dil"][1]], [sent["dil"][1]], dil_w_o, m_dil_w_o, v_dil_w_o,
                          **one(o_rows, o_rows, D_MODEL, False)),
        "ffn_w_gate": _adamw("adamw_gate", me, gu_lands, gu_owns, ffn_w_gate, m_ffn_w_gate, v_ffn_w_gate,
                             **layered((None, 128, FF_PAD), lambda lead, r: (lead, 0, r, 0), (None, 128, FF_SHARD), 8)),
        "ffn_w_up": _adamw("adamw_up", me, gu_lands, gu_owns, ffn_w_up, m_ffn_w_up, v_ffn_w_up,
                           **layered((None, 128, FF_PAD), lambda lead, r: (lead, 1, r, 0), (None, 128, FF_SHARD), 8)),
        "ffn_w_down": _adamw("adamw_down", me, down_lands, down_owns, ffn_w_down, m_ffn_w_down, v_ffn_w_down,
                             **layered((176, D_MODEL), lambda lead, r: (lead, r, 0), (None, 176, D_MODEL), 2)),
    }
    done = [u[0] for u in updates.values()]
    small = _all_gather("gather_small", [_pack_small(d_gains, d_rpb, loss)], [jax.ShapeDtypeStruct((N_DEV, 128, 128), F32)],
                        [_leading], deps=done)[0]
    wait_for("na", [small])
    updates["na_w_qkv"] = _adamw("adamw_na_qkv", me, [landed["na"][0]], [sent["na"][0]], na_w_qkv, m_na_w_qkv, v_na_w_qkv,
                                 **one(D_MODEL, 256, na_cols, True))
    updates["na_w_o"] = _adamw("adamw_na_o", me, [landed["na"][1]], [sent["na"][1]], na_w_o, m_na_w_o, v_na_w_o,
                               **one(o_rows, o_rows, D_MODEL, False))
    gains = [norm_mix_pre, norm_mix_post, norm_ffn_pre, norm_ffn_post]
    m_gains = [m_norm_mix_pre, m_norm_mix_post, m_norm_ffn_pre, m_norm_ffn_post]
    v_gains = [v_norm_mix_pre, v_norm_mix_post, v_norm_ffn_pre, v_norm_ffn_post]
    packed = _adamw("adamw_small", me, [small], (), _pack_small(gains, na_rpb)[None], _pack_small(m_gains, m_na_rpb)[None],
                    _pack_small(v_gains, v_na_rpb)[None], grid=(1,),
                    land_specs=[pl.BlockSpec((N_DEV, 128, 128), lambda i, me: (0, 0, 0))], own_specs=[],
                    p_spec=pl.BlockSpec((None, 128, 128), lambda i, me: (0, 0, 0)))
    small_out = [_unpack_small(p[0]) for p in packed]

    order = ["na_w_qkv", "na_w_o", "na_rpb", "dil_w_qkv", "dil_w_o", "ffn_w_gate", "ffn_w_up", "ffn_w_down"]
    result = [packed[0][0, 127, 127], grad_x[None]]
    for kind in range(4):
        gains_k, rpb_k = small_out[kind]
        result += gains_k
        result += [rpb_k if name == "na_rpb" else updates[name][kind] for name in order]
    return tuple(result)
```

```python
import functools

import jax
import jax.numpy as jnp
from jax import lax
from jax.experimental import pallas as pl
from jax.experimental.pallas import tpu as pltpu

F32 = jnp.float32
BF16 = jnp.bfloat16
MESH = pl.DeviceIdType.MESH
ANY = pl.BlockSpec(memory_space=pl.ANY)
HBM = pl.BlockSpec(memory_space=pltpu.HBM)
SEM = pl.BlockSpec(memory_space=pltpu.SEMAPHORE)
EFFECT = pltpu.SideEffectType.DATAFLOW_SIDE_EFFECTING

N_DEV = 8
SEQ = 2048
D_MODEL = 1024
N_HEADS = 16
HEAD_DIM = 64
GRID_W = 64
NA_ROWS = 8
SEQ_ROWS = SEQ // GRID_W
DIL_GROUPS = ((128, 1), (512, 4), (2048, 16))
BAND = 128
RADIUS = 64
FF_SHARD = 352
FF_PAD = 384
RMS_EPS = 1e-6
NEG_INF = -1e30
Q_SCALE = HEAD_DIM ** -0.5

ADAM_LR = 0.001
ADAM_B1 = 0.9
ADAM_B2 = 0.999
ADAM_EPS = 1e-08
ADAM_WD = 0.01
ADAM_STEP = 10

VMEM_LIMIT = 56 * 1024 * 1024
TM = 512
TM_ROW = 1024
TM_MM = 1024

NN = (((1,), (0,)), ((), ()))
NT = (((1,), (1,)), ((), ()))
TN = (((0,), (0,)), ((), ()))


def _params():
    return pltpu.CompilerParams(vmem_limit_bytes=VMEM_LIMIT)


def _matmul(name, a, b, *, grid, a_spec, b_spec, o_spec, out_shape, dims, acc_shape, deps=(), inner=1):
    nk = grid[-1]
    kaxis = len(grid) - 1

    def body(a_ref, b_ref, *rest):
        o_ref, acc_ref = rest[-2], rest[-1]
        if inner == 1:
            part = lax.dot_general(a_ref[...].astype(BF16), b_ref[...].astype(BF16), dims, preferred_element_type=F32)
        elif len(b_ref.shape) == 2:
            a_all = jnp.concatenate([a_ref[j].astype(BF16) for j in range(inner)], axis=1)
            part = lax.dot_general(a_all, b_ref[...].astype(BF16), dims, preferred_element_type=F32)
        else:
            part = sum(lax.dot_general(a_ref[j].astype(BF16), b_ref[j].astype(BF16), dims, preferred_element_type=F32)
                       for j in range(inner))
        if nk == 1:
            o_ref[...] = part.astype(o_ref.dtype)
        else:
            k = pl.program_id(kaxis)

            @pl.when(k == 0)
            def _():
                acc_ref[...] = part

            @pl.when(k > 0)
            def _():
                acc_ref[...] += part

            @pl.when(k == nk - 1)
            def _():
                o_ref[...] = acc_ref[...].astype(o_ref.dtype)

    return pl.pallas_call(
        body, out_shape=out_shape, grid=grid, in_specs=[a_spec, b_spec] + [ANY] * len(deps), out_specs=o_spec,
        scratch_shapes=[pltpu.VMEM(acc_shape, F32)], name=name, compiler_params=_params())(a, b, *deps)


SORTED = tuple(d for _, d in DIL_GROUPS if d > 1)
LANE_CHUNKS = D_MODEL // 128


def _sort_scratch(tm=TM):
    return pltpu.VMEM((LANE_CHUNKS, tm, 128), F32)


def _sorted_view(t, dil):
    return t.reshape(dil, SEQ // dil, D_MODEL)


def _sorted_spec(dil, lead=(), tm=TM):
    return pl.BlockSpec((None,) * len(lead) + (dil, tm // dil, D_MODEL), lambda i: tuple(lead) + (0, i, 0))


def _sort_tile(scratch, value, dil, out_ref):
    tm = value.shape[0]
    for c in range(LANE_CHUNKS):
        scratch[c] = value[:, 128 * c:128 * (c + 1)]
    for r in range(dil):
        rows = [scratch.at[c][pl.ds(r, tm // dil, stride=dil), :] for c in range(LANE_CHUNKS)]
        out_ref[r] = jnp.concatenate(rows, axis=1).astype(out_ref.dtype)


def _unsort_tile(scratch, in_ref, dil):
    for r in range(dil):
        value = in_ref[r].astype(F32)
        for c in range(LANE_CHUNKS):
            scratch.at[c][pl.ds(r, value.shape[0], stride=dil), :] = value[:, 128 * c:128 * (c + 1)]
    return jnp.concatenate([scratch[c] for c in range(LANE_CHUNKS)], axis=1)


def _rms_fwd(name, x, g, res=None, out_dtype=F32, deps=(), sorted_too=False):
    tm = TM if sorted_too else TM_ROW
    n_tiles = SEQ // tm
    has_res = res is not None
    n_in = 2 + has_res + len(deps)

    def body(*refs):
        x_ref, g_ref = refs[0], refs[1]
        xv = x_ref[...]
        r = lax.rsqrt(jnp.mean(xv * xv, axis=-1, keepdims=True) + RMS_EPS)
        y = xv * r * g_ref[...]
        if has_res:
            y = refs[2][...] + y
        refs[n_in][...] = y.astype(out_dtype)
        if sorted_too:
            for j, dil in enumerate(SORTED):
                _sort_tile(refs[-1], y, dil, refs[n_in + 1 + j])

    tile = pl.BlockSpec((tm, D_MODEL), lambda i: (i, 0))
    gspec = pl.BlockSpec((1, D_MODEL), lambda i: (0, 0))
    ins = [x, g] + ([res] if has_res else []) + list(deps)
    specs = [tile, gspec] + ([tile] if has_res else []) + [ANY] * len(deps)
    shapes, out_specs = [jax.ShapeDtypeStruct((SEQ, D_MODEL), out_dtype)], [tile]
    if sorted_too:
        shapes += [jax.ShapeDtypeStruct((dil, SEQ // dil, D_MODEL), out_dtype) for dil in SORTED]
        out_specs += [_sorted_spec(dil) for dil in SORTED]
    result = pl.pallas_call(
        body, out_shape=tuple(shapes), grid=(n_tiles,), in_specs=specs, out_specs=tuple(out_specs),
        scratch_shapes=[_sort_scratch()] if sorted_too else [], name=name, compiler_params=_params())(*ins)
    return result if sorted_too else result[0]


def _rms_bwd(name, x, g, dys, res=None, out_dtype=F32, groups=None, deps=()):
    tm = TM if groups is not None else TM_ROW
    n_tiles = SEQ // tm
    n_dy = len(dys) if groups is None else 1 + len(SORTED)
    has_res = res is not None

    def body(*refs):
        x_ref, g_ref = refs[0], refs[1]
        dy_refs = refs[2:2 + n_dy]
        res_ref = refs[2 + n_dy] if has_res else None
        first_out = 2 + n_dy + has_res + len(deps)
        dx_ref, dg_ref, acc_ref = refs[first_out:first_out + 3]
        i = pl.program_id(0)
        xv = x_ref[...]
        r = lax.rsqrt(jnp.mean(xv * xv, axis=-1, keepdims=True) + RMS_EPS)
        xn = xv * r
        dy = dy_refs[0][...].astype(F32)
        for j, extra in enumerate(dy_refs[1:]):
            dy = dy + (extra[...].astype(F32) if groups is None else _unsort_tile(refs[-1], extra, SORTED[j]))
        dyg = dy * g_ref[...]
        dx = r * (dyg - xn * jnp.mean(dyg * xn, axis=-1, keepdims=True))
        if has_res:
            dx = res_ref[...] + dx
        dx_ref[...] = dx.astype(dx_ref.dtype)
        part = jnp.sum((dy * xn).reshape(tm // 8, 8, D_MODEL), axis=0)

        @pl.when(i == 0)
        def _():
            acc_ref[...] = part

        @pl.when(i > 0)
        def _():
            acc_ref[...] += part

        @pl.when(i == n_tiles - 1)
        def _():
            dg_ref[...] = jnp.broadcast_to(jnp.sum(acc_ref[...], axis=0, keepdims=True), (8, D_MODEL))

    tile = pl.BlockSpec((tm, D_MODEL), lambda i: (i, 0))
    gspec = pl.BlockSpec((1, D_MODEL), lambda i: (0, 0))
    if groups is None:
        dy_ins, dy_specs = list(dys), [tile] * n_dy
    else:
        dy_ins = [groups] + [groups.reshape(n_dy, dil, SEQ // dil, D_MODEL) for dil in SORTED]
        dy_specs = [pl.BlockSpec((None, tm, D_MODEL), lambda i: (0, i, 0))]
        dy_specs += [_sorted_spec(dil, lead=(1 + j,)) for j, dil in enumerate(SORTED)]
    ins = [x, g] + dy_ins + ([res] if has_res else []) + list(deps)
    specs = [tile, gspec] + dy_specs + ([tile] if has_res else []) + [ANY] * len(deps)
    dx, dg = pl.pallas_call(
        body, out_shape=(jax.ShapeDtypeStruct((SEQ, D_MODEL), out_dtype), jax.ShapeDtypeStruct((8, D_MODEL), F32)),
        grid=(n_tiles,), in_specs=specs,
        out_specs=(tile, pl.BlockSpec((8, D_MODEL), lambda i: (0, 0))),
        scratch_shapes=[pltpu.VMEM((8, D_MODEL), F32)] + ([_sort_scratch()] if groups is not None else []),
        name=name, compiler_params=_params())(*ins)
    return dx, dg[0:1]


def _loss_head(name, y, target):
    tm = TM_ROW
    n_tiles = SEQ // tm

    def body(y_ref, t_ref, dy_ref, loss_ref, acc_ref):
        i = pl.program_id(0)
        diff = y_ref[...] - t_ref[...]
        dy_ref[...] = diff * (1.0 / D_MODEL)
        part = jnp.sum((diff * diff).reshape(tm // 8, 8, D_MODEL), axis=0)

        @pl.when(i == 0)
        def _():
            acc_ref[...] = part

        @pl.when(i > 0)
        def _():
            acc_ref[...] += part

        @pl.when(i == n_tiles - 1)
        def _():
            loss_ref[...] = jnp.full((8, 128), jnp.sum(acc_ref[...]) * (0.5 / D_MODEL), F32)

    tile = pl.BlockSpec((tm, D_MODEL), lambda i: (i, 0))
    dy, loss = pl.pallas_call(
        body, out_shape=(jax.ShapeDtypeStruct((SEQ, D_MODEL), F32), jax.ShapeDtypeStruct((8, 128), F32)),
        grid=(n_tiles,), in_specs=[tile, tile], out_specs=(tile, pl.BlockSpec((8, 128), lambda i: (0, 0))),
        scratch_shapes=[pltpu.VMEM((8, D_MODEL), F32)], name=name, compiler_params=_params())(y, target)
    return dy, loss[0, 0]


def _row_index(shape):
    return lax.broadcasted_iota(jnp.int32, shape, 0)


def _lane_index(shape):
    return lax.broadcasted_iota(jnp.int32, shape, len(shape) - 1)


def _skew_rows(t, direction):
    q = _row_index(t.shape) & (GRID_W - 1)
    for bit in range(6):
        step = 1 << bit
        shift = step if direction > 0 else 128 - step
        t = jnp.where((q & step) != 0, pltpu.roll(t, shift, 1), t)
    return t


def _rpb_table(rpb_pad):
    rows = 16 * GRID_W

    def body(r_ref, t_ref):
        lane = _lane_index((rows, 128))
        v = pltpu.roll(r_ref[...], 128 - 15, 1)
        t = _skew_rows(jnp.broadcast_to(v[:, None, :], (16, GRID_W, 128)).reshape(rows, 128), +1)
        t = jnp.where(lane < GRID_W, t, 0.0)
        below = jnp.concatenate([t[GRID_W:], jnp.zeros((GRID_W, 128), F32)], axis=0)
        first_col = jnp.clip((_row_index((rows, 128)) & (GRID_W - 1)) - 8, 0, GRID_W - 16)
        key_col = lane & (GRID_W - 1)
        in_window = (key_col >= first_col) & (key_col < first_col + 16)
        t_ref[...] = jnp.where(in_window, t + pltpu.roll(below, GRID_W, 1), NEG_INF).reshape(16, GRID_W, 128)

    return pl.pallas_call(
        body, out_shape=jax.ShapeDtypeStruct((N_HEADS, 16, GRID_W, 128), F32), grid=(N_HEADS,),
        in_specs=[pl.BlockSpec((None, 16, 128), lambda h: (h, 0, 0))],
        out_specs=pl.BlockSpec((None, 16, GRID_W, 128), lambda h: (h, 0, 0, 0)),
        name="rpb_table", compiler_params=_params())(rpb_pad)


def _rpb_grad(gp):
    rows = 16 * GRID_W

    def body(g_ref, o_ref):
        lane = _lane_index((rows, 128))
        g = g_ref[...].reshape(rows, 128)
        low = jnp.where(lane < GRID_W, g, 0.0)
        high = pltpu.roll(jnp.where(lane >= GRID_W, g, 0.0), GRID_W, 1)
        above = jnp.concatenate([jnp.zeros((GRID_W, 128), F32), high[:rows - GRID_W]], axis=0)
        diag = jnp.sum(_skew_rows(low + above, -1).reshape(16, GRID_W, 128), axis=1)
        o_ref[...] = pltpu.roll(diag, 15, 1)

    return pl.pallas_call(
        body, out_shape=jax.ShapeDtypeStruct((N_HEADS, 16, 128), F32), grid=(N_HEADS,),
        in_specs=[pl.BlockSpec((None, 16, GRID_W, 128), lambda h: (h, 0, 0, 0))],
        out_specs=pl.BlockSpec((None, 16, 128), lambda h: (h, 0, 0)),
        name="rpb_grad", compiler_params=_params())(gp)


NA_KEYS = NA_ROWS * GRID_W


def _na_window(i):
    first_row = jnp.clip(i - NA_ROWS // 2, 0, SEQ_ROWS - NA_ROWS)
    return pl.multiple_of(first_row * GRID_W, GRID_W), first_row - i + NA_ROWS - 1


NA_STEP = 32


def _head_masks():
    lane = _lane_index((1, 128))
    return (lane < HEAD_DIM, lane >= HEAD_DIM)


def _stack_heads(t, masks):
    zero = jnp.zeros_like(t)
    return jnp.concatenate([jnp.where(masks[0], t, zero), jnp.where(masks[1], t, zero)], axis=0)


def _unstack_heads(t, masks):
    n = t.shape[0] // 2
    return jnp.where(masks[0], t[:n], t[n:])


def _stack_columns(t):
    return jnp.concatenate([t[:, 0:1], t[:, HEAD_DIM:HEAD_DIM + 1]], axis=0)


def _na_scores(qs, kw, tp_ref, dr0):
    s = lax.dot_general(qs, kw, NT, preferred_element_type=F32)
    bias = jnp.concatenate(
        [jnp.concatenate([tp_ref[a, pl.ds(dr0 + 2 * c, 1), :, :].reshape(GRID_W, 128) for c in range(4)], axis=1)
         for a in range(2)], axis=0)
    return s + bias


def _na_specs():
    q_spec = pl.BlockSpec((NA_STEP * GRID_W, 128), lambda hp, i: (i, hp))
    k_spec = pl.BlockSpec((SEQ, 128), lambda hp, i: (0, 8 + hp))
    v_spec = pl.BlockSpec((SEQ, 128), lambda hp, i: (0, 16 + hp))
    tp_spec = pl.BlockSpec((2, 16, GRID_W, 128), lambda hp, i: (hp, 0, 0, 0))
    return q_spec, k_spec, v_spec, tp_spec


def _na_fwd(qkv, table):
    def body(q_ref, k_ref, v_ref, tp_ref, o_ref, lse_ref):
        masks = _head_masks()
        for r in range(NA_STEP):
            rows = slice(r * GRID_W, (r + 1) * GRID_W)
            start, dr0 = _na_window(pl.program_id(1) * NA_STEP + r)
            kw = k_ref[pl.ds(start, NA_KEYS), :]
            vw = v_ref[pl.ds(start, NA_KEYS), :]
            s = _na_scores(_stack_heads(q_ref[rows, :] * Q_SCALE, masks), kw, tp_ref, dr0)
            m = jnp.max(s, axis=-1, keepdims=True)
            p = jnp.exp(s - m)
            denom = jnp.sum(p, axis=-1, keepdims=True)
            out = jnp.dot(p.astype(BF16), vw, preferred_element_type=F32) / denom
            o_ref[rows, :] = _unstack_heads(out, masks).astype(o_ref.dtype)
            lse_ref[rows, :] = _unstack_heads(jnp.broadcast_to(m + jnp.log(denom), (2 * GRID_W, 128)), masks)

    q_spec, k_spec, v_spec, tp_spec = _na_specs()
    return pl.pallas_call(
        body, out_shape=(jax.ShapeDtypeStruct((SEQ, D_MODEL), BF16), jax.ShapeDtypeStruct((SEQ, D_MODEL), F32)),
        grid=(N_HEADS // 2, SEQ_ROWS // NA_STEP), in_specs=[q_spec, k_spec, v_spec, tp_spec],
        out_specs=(q_spec, q_spec), name="na_fwd", compiler_params=_params())(qkv, qkv, qkv, table)


def _na_bwd(qkv, table, d_out, lse):
    def body(q_ref, k_ref, v_ref, tp_ref, do_ref, lse_ref, dqkv_ref, gp_ref, dk_acc, dv_acc):
        step = pl.program_id(1)

        @pl.when(step == 0)
        def _():
            dk_acc[...] = jnp.zeros_like(dk_acc)
            dv_acc[...] = jnp.zeros_like(dv_acc)
            gp_ref[...] = jnp.zeros_like(gp_ref)

        masks = _head_masks()
        for r in range(NA_STEP):
            rows = slice(r * GRID_W, (r + 1) * GRID_W)
            i = step * NA_STEP + r
            start, dr0 = _na_window(i)
            kw = k_ref[pl.ds(start, NA_KEYS), :]
            vw = v_ref[pl.ds(start, NA_KEYS), :]
            qs = _stack_heads(q_ref[rows, :] * Q_SCALE, masks)
            dos = _stack_heads(do_ref[rows, :], masks)
            p = jnp.exp(_na_scores(qs, kw, tp_ref, dr0) - _stack_columns(lse_ref[rows, :]))
            dp = lax.dot_general(dos, vw, NT, preferred_element_type=F32)
            ds = p * (dp - jnp.sum(p * dp, axis=-1, keepdims=True))
            for a in range(2):
                for c in range(4):
                    gp_ref[a, pl.ds(dr0 + 2 * c, 1), :, :] += (
                        ds[a * GRID_W:(a + 1) * GRID_W, 128 * c:128 * (c + 1)].reshape(1, GRID_W, 128))
            dsb = ds.astype(BF16)
            dq = _unstack_heads(jnp.dot(dsb, kw, preferred_element_type=F32), masks) * Q_SCALE
            dqkv_ref[0, pl.ds(pl.multiple_of(i * GRID_W, GRID_W), GRID_W), :] = dq.astype(dqkv_ref.dtype)
            dk_acc[pl.ds(start, NA_KEYS), :] += lax.dot_general(dsb, qs, TN, preferred_element_type=F32)
            dv_acc[pl.ds(start, NA_KEYS), :] += lax.dot_general(p.astype(BF16), dos, TN, preferred_element_type=F32)

        @pl.when(step == SEQ_ROWS // NA_STEP - 1)
        def _():
            dqkv_ref[1] = dk_acc[...].astype(dqkv_ref.dtype)
            dqkv_ref[2] = dv_acc[...].astype(dqkv_ref.dtype)

    q_spec, k_spec, v_spec, tp_spec = _na_specs()
    return pl.pallas_call(
        body,
        out_shape=(jax.ShapeDtypeStruct((3, SEQ, D_MODEL), BF16), jax.ShapeDtypeStruct((N_HEADS, 16, GRID_W, 128), F32)),
        grid=(N_HEADS // 2, SEQ_ROWS // NA_STEP), in_specs=[q_spec, k_spec, v_spec, tp_spec, q_spec, q_spec],
        out_specs=(pl.BlockSpec((3, SEQ, 128), lambda hp, i: (0, 0, hp)), tp_spec),
        scratch_shapes=[pltpu.VMEM((SEQ, 128), F32), pltpu.VMEM((SEQ, 128), F32)],
        name="na_bwd", compiler_params=_params())(qkv, qkv, qkv, table, d_out, lse)


DIL_STEP = 16


def _dil_geometry(group):
    dil = DIL_GROUPS[group][1]
    sub_len = SEQ // dil
    blocks = sub_len // BAND
    return dil, sub_len, max(blocks // DIL_STEP, 1), max(DIL_STEP // blocks, 1), min(2 * BAND, sub_len)


def _dil_block(step, r, sub_len, subs):
    per_sub = DIL_STEP // subs
    return (r // per_sub) * sub_len, step * per_sub + r % per_sub


def _dil_window(b, sub_len, n_keys):
    if n_keys == sub_len:
        return 0
    return pl.multiple_of(jnp.clip(b * BAND - RADIUS, 0, sub_len - n_keys), RADIUS)


def _dil_bias(b, start, n_keys, slope_ref, hp):
    row = _row_index((2 * BAND, n_keys))
    qpos = b * BAND + (row & (BAND - 1))
    kpos = start + _lane_index((2 * BAND, n_keys))
    dist = jnp.abs(qpos - kpos)
    slope = jnp.where(row < BAND, slope_ref[2 * hp], slope_ref[2 * hp + 1])
    return slope * dist.astype(F32), dist <= RADIUS


def _dil_scores(qs, kw, penalty, valid):
    return jnp.where(valid, lax.dot_general(qs, kw, NT, preferred_element_type=F32) - penalty, NEG_INF)


def _dil_specs(group):
    dil, sub_len, steps, subs, _ = _dil_geometry(group)
    col = group * 24
    rows = DIL_STEP * BAND
    q_spec = pl.BlockSpec((rows, 128), lambda n, hp, b: (n * steps + b, col + hp))
    k_spec = pl.BlockSpec((subs * sub_len, 128), lambda n, hp, b: (n, col + 8 + hp))
    v_spec = pl.BlockSpec((subs * sub_len, 128), lambda n, hp, b: (n, col + 16 + hp))
    tile = pl.BlockSpec((rows, 128), lambda n, hp, b: (n * steps + b, hp))
    smem = pl.BlockSpec(memory_space=pltpu.SMEM)
    return (dil // subs, N_HEADS // 2, steps), q_spec, k_spec, v_spec, tile, smem


def _dil_fwd(group, qkv, slopes):
    _, sub_len, _, subs, n_keys = _dil_geometry(group)

    def body(q_ref, k_ref, v_ref, slope_ref, o_ref, lse_ref):
        hp = pl.program_id(1)
        masks = _head_masks()
        for r in range(DIL_STEP):
            rows = slice(r * BAND, (r + 1) * BAND)
            base, b = _dil_block(pl.program_id(2), r, sub_len, subs)
            start = _dil_window(b, sub_len, n_keys)
            kw = k_ref[pl.ds(base + start, n_keys), :]
            vw = v_ref[pl.ds(base + start, n_keys), :]
            penalty, valid = _dil_bias(b, start, n_keys, slope_ref, hp)
            s = _dil_scores(_stack_heads(q_ref[rows, :] * Q_SCALE, masks), kw, penalty, valid)
            m = jnp.max(s, axis=-1, keepdims=True)
            p = jnp.exp(s - m)
            denom = jnp.sum(p, axis=-1, keepdims=True)
            out = jnp.dot(p.astype(BF16), vw, preferred_element_type=F32) / denom
            o_ref[rows, :] = _unstack_heads(out, masks).astype(o_ref.dtype)
            lse_ref[rows, :] = _unstack_heads(jnp.broadcast_to(m + jnp.log(denom), (2 * BAND, 128)), masks)

    grid, q_spec, k_spec, v_spec, tile, smem = _dil_specs(group)
    return pl.pallas_call(
        body, out_shape=(jax.ShapeDtypeStruct((SEQ, D_MODEL), BF16), jax.ShapeDtypeStruct((SEQ, D_MODEL), F32)),
        grid=grid, in_specs=[q_spec, k_spec, v_spec, smem], out_specs=(tile, tile),
        name=f"dil_fwd_{group}", compiler_params=_params())(qkv, qkv, qkv, slopes)


def _dil_merge(outs, lses):
    n_sorted = len(SORTED)

    def body(*refs):
        o_refs, l_refs = refs[:3], refs[3:6]
        out_refs, lse_refs, scratch = refs[6:7 + n_sorted], refs[7 + n_sorted:8 + 2 * n_sorted], refs[-1]
        os_ = [o_refs[0][...]] + [_unsort_tile(scratch, o_refs[1 + j], dil) for j, dil in enumerate(SORTED)]
        ls = [l_refs[0][...]] + [_unsort_tile(scratch, l_refs[1 + j], dil) for j, dil in enumerate(SORTED)]
        m = jnp.maximum(jnp.maximum(ls[0], ls[1]), ls[2])
        es = [jnp.exp(v - m) for v in ls]
        total = es[0] + es[1] + es[2]
        merged = (es[0] * os_[0] + es[1] * os_[1] + es[2] * os_[2]) / total
        lse = m + jnp.log(total)
        out_refs[0][...] = merged
        lse_refs[0][...] = lse
        for j, dil in enumerate(SORTED):
            _sort_tile(scratch, merged, dil, out_refs[1 + j])
            _sort_tile(scratch, lse, dil, lse_refs[1 + j])

    tm = 256
    tile = pl.BlockSpec((tm, D_MODEL), lambda i: (i, 0))
    specs = [tile] + [_sorted_spec(dil, tm=tm) for dil in SORTED]
    shapes = [jax.ShapeDtypeStruct((SEQ, D_MODEL), F32)] + [jax.ShapeDtypeStruct((dil, SEQ // dil, D_MODEL), F32) for dil in SORTED]
    views = lambda ts: [ts[0]] + [_sorted_view(t, dil) for t, dil in zip(ts[1:], SORTED)]
    result = pl.pallas_call(
        body, out_shape=tuple(shapes * 2), grid=(SEQ // tm,), in_specs=specs * 2, out_specs=tuple(specs * 2),
        scratch_shapes=[_sort_scratch(tm)], name="dil_merge", compiler_params=_params())(*views(outs), *views(lses))
    flat = [t.reshape(SEQ, D_MODEL) for t in result]
    return flat[:1 + n_sorted], flat[1 + n_sorted:]


def _dil_bwd(group, qkv, slopes, d_out, out, lse_group, lse_total, into):
    _, sub_len, steps, subs, n_keys = _dil_geometry(group)

    def body(q_ref, k_ref, v_ref, slope_ref, do_ref, o_ref, lg_ref, lt_ref, into_ref, dqkv_ref, dk_acc, dv_acc):
        hp, step = pl.program_id(1), pl.program_id(2)

        @pl.when(step == 0)
        def _():
            dk_acc[...] = jnp.zeros_like(dk_acc)
            dv_acc[...] = jnp.zeros_like(dv_acc)

        masks = _head_masks()
        for r in range(DIL_STEP):
            rows = slice(r * BAND, (r + 1) * BAND)
            base, b = _dil_block(step, r, sub_len, subs)
            start = _dil_window(b, sub_len, n_keys)
            keys = pl.ds(base + start, n_keys)
            kw = k_ref[keys, :]
            vw = v_ref[keys, :]
            penalty, valid = _dil_bias(b, start, n_keys, slope_ref, hp)
            qs = _stack_heads(q_ref[rows, :] * Q_SCALE, masks)
            lse2 = lg_ref[rows, :]
            weight = jnp.exp(lse2 - lt_ref[rows, :])
            do2 = do_ref[rows, :]
            dogs = _stack_heads((weight * do2).astype(BF16), masks)
            delta = _stack_columns(weight) * jnp.sum(_stack_heads(do2 * o_ref[rows, :], masks), axis=-1, keepdims=True)
            p = jnp.exp(_dil_scores(qs, kw, penalty, valid) - _stack_columns(lse2))
            dp = lax.dot_general(dogs, vw, NT, preferred_element_type=F32)
            dsb = (p * (dp - delta)).astype(BF16)
            dq = _unstack_heads(jnp.dot(dsb, kw, preferred_element_type=F32), masks) * Q_SCALE
            dqkv_ref[0, pl.ds(pl.multiple_of(base + b * BAND, BAND), BAND), :] = dq.astype(dqkv_ref.dtype)
            dk_acc[keys, :] += lax.dot_general(dsb, qs, TN, preferred_element_type=F32)
            dv_acc[keys, :] += lax.dot_general(p.astype(BF16), dogs, TN, preferred_element_type=F32)

        @pl.when(step == steps - 1)
        def _():
            dqkv_ref[1] = dk_acc[...].astype(dqkv_ref.dtype)
            dqkv_ref[2] = dv_acc[...].astype(dqkv_ref.dtype)

    grid, q_spec, k_spec, v_spec, tile, smem = _dil_specs(group)
    return pl.pallas_call(
        body, out_shape=jax.ShapeDtypeStruct(into.shape, into.dtype), grid=grid,
        in_specs=[q_spec, k_spec, v_spec, smem, tile, tile, tile, tile, ANY],
        out_specs=pl.BlockSpec((3, subs * sub_len, 128), lambda n, hp, b: (group, n, hp)),
        scratch_shapes=[pltpu.VMEM((subs * sub_len, 128), F32), pltpu.VMEM((subs * sub_len, 128), F32)],
        input_output_aliases={8: 0}, name=f"dil_bwd_{group}", compiler_params=_params(),
    )(qkv, qkv, qkv, slopes, d_out, out, lse_group, lse_total, into)


def _accumulate_rows(acc_ref, i, first, part):
    rows = pl.ds(pl.multiple_of(i * TM_MM, TM_MM), TM_MM)

    @pl.when(first)
    def _():
        acc_ref[rows, :] = part

    @pl.when(jnp.logical_not(first))
    def _():
        acc_ref[rows, :] += part


def _ffn_specs():
    tile = pl.BlockSpec((TM_MM, D_MODEL), lambda d, i: (i, 0))
    gate = pl.BlockSpec((None, None, D_MODEL, FF_PAD), lambda d, i: (d, 0, 0, 0))
    up = pl.BlockSpec((None, None, D_MODEL, FF_PAD), lambda d, i: (d, 1, 0, 0))
    down = pl.BlockSpec((None, FF_PAD, D_MODEL), lambda d, i: (d, 0, 0))
    hidden = pl.BlockSpec((None, TM_MM, FF_PAD), lambda d, i: (d, i, 0))
    whole = pl.BlockSpec((SEQ, D_MODEL), lambda d, i: (0, 0))
    return tile, gate, up, down, hidden, whole


def _ffn_fwd(name, h, w_gu, w_down):
    def body(h_ref, wg_ref, wu_ref, wd_ref, f_ref, hg_ref, hu_ref, act_t_ref):
        hv = h_ref[...]
        hg = jnp.dot(hv, wg_ref[...], preferred_element_type=F32)
        hu = jnp.dot(hv, wu_ref[...], preferred_element_type=F32)
        act = hg * jax.nn.sigmoid(hg) * hu
        act_t_ref[...] = act.T.astype(act_t_ref.dtype)
        hg_ref[...] = hg.astype(hg_ref.dtype)
        hu_ref[...] = hu.astype(hu_ref.dtype)
        _accumulate_rows(f_ref, pl.program_id(1), pl.program_id(0) == 0,
                         jnp.dot(act.astype(BF16), wd_ref[...], preferred_element_type=F32))

    tile, gate, up, down, hidden, whole = _ffn_specs()
    shape = jax.ShapeDtypeStruct((N_DEV, SEQ, FF_PAD), BF16)
    return pl.pallas_call(
        body, out_shape=(jax.ShapeDtypeStruct((SEQ, D_MODEL), F32), shape, shape, jax.ShapeDtypeStruct((N_DEV, FF_PAD, SEQ), BF16)),
        grid=(N_DEV, SEQ // TM_MM), in_specs=[tile, gate, up, down],
        out_specs=(whole, hidden, hidden, pl.BlockSpec((None, FF_PAD, TM_MM), lambda d, i: (d, 0, i))),
        name=name, compiler_params=_params())(h, w_gu, w_gu, w_down)


def _ffn_dgu(name, h_t, dgu):
    def body(h_ref, dgu_ref, o_ref):
        both = jnp.dot(h_ref[...], jnp.concatenate([dgu_ref[0], dgu_ref[1]], axis=1), preferred_element_type=F32)
        o_ref[0] = both[:, :FF_PAD].astype(o_ref.dtype)
        o_ref[1] = both[:, FF_PAD:].astype(o_ref.dtype)

    return pl.pallas_call(
        body, out_shape=jax.ShapeDtypeStruct((N_DEV, 2, D_MODEL, FF_PAD), BF16), grid=(N_DEV,),
        in_specs=[pl.BlockSpec((D_MODEL, SEQ), lambda d: (0, 0)), pl.BlockSpec((2, None, SEQ, FF_PAD), lambda d: (0, d, 0, 0))],
        out_specs=pl.BlockSpec((None, 2, D_MODEL, FF_PAD), lambda d: (d, 0, 0, 0)),
        name=name, compiler_params=_params())(h_t, dgu)


def _ffn_bwd(name, d_f, w_gu, w_down, hg, hu):
    def body(df_ref, wg_ref, wu_ref, wd_ref, hg_ref, hu_ref, dgu_ref, dh_ref):
        dact = lax.dot_general(df_ref[...], wd_ref[...], NT, preferred_element_type=F32)
        hgv = hg_ref[...].astype(F32)
        sig = jax.nn.sigmoid(hgv)
        d_gate = (dact * hu_ref[...].astype(F32) * (sig * (1.0 + hgv * (1.0 - sig)))).astype(BF16)
        d_up = (dact * hgv * sig).astype(BF16)
        dgu_ref[0] = d_gate
        dgu_ref[1] = d_up
        part = lax.dot_general(jnp.concatenate([d_gate, d_up], axis=1), jnp.concatenate([wg_ref[...], wu_ref[...]], axis=1),
                               NT, preferred_element_type=F32)
        _accumulate_rows(dh_ref, pl.program_id(1), pl.program_id(0) == 0, part)

    tile, gate, up, down, hidden, whole = _ffn_specs()
    return pl.pallas_call(
        body, out_shape=(jax.ShapeDtypeStruct((2, N_DEV, SEQ, FF_PAD), BF16), jax.ShapeDtypeStruct((SEQ, D_MODEL), F32)),
        grid=(N_DEV, SEQ // TM_MM), in_specs=[tile, gate, up, down, hidden, hidden],
        out_specs=(pl.BlockSpec((2, None, TM_MM, FF_PAD), lambda d, i: (0, d, i, 0)), whole),
        name=name, compiler_params=_params())(d_f, w_gu, w_gu, w_down, hg, hu)


def _position():
    return lax.axis_index("x"), lax.axis_index("y"), lax.axis_index("c")


def _flat(p):
    return 4 * p[0] + 2 * p[1] + p[2]


def _peer(me, k):
    x, y, c = me
    return (1 - x if k & 4 else x, 1 - y if k & 2 else y, 1 - c if k & 1 else c)


def _columns(width):
    return lambda ref, d: ref.at[:, pl.ds(pl.multiple_of(d * width, 128), width)]


def _leading(ref, d):
    return ref.at[d]


def _whole(ref, d):
    return ref


def _by_sender(window):
    return lambda ref, sender, k: window(ref, sender)


def _by_distance(ref, sender, k):
    return ref.at[k - 1]


def _prep_weights(me, na_qkv, na_o, dil_qkv, dil_o, gate, up, down, land_shapes):
    na_cols, dil_cols = na_qkv.shape[-1], dil_qkv.shape[-1]
    o_rows = na_o.shape[1]
    tiles = 4
    rows, rows_o = D_MODEL // tiles, o_rows // tiles

    def body(me_ref, naq, nao, dq, do_, g0, u0, d0, g1, u1, d1, *outs):
        def put(t, index, value):
            outs[t][index] = value
            outs[8 + t][index] = value

        put(0, ..., naq[...].astype(BF16))
        put(1, ..., nao[...].astype(BF16))
        put(4, ..., dq[...].astype(BF16))
        put(5, ..., do_[...].astype(BF16))
        for t, (g, u, d) in ((2, (g0, u0, d0)), (6, (g1, u1, d1))):
            for j, part in enumerate((g, u)):
                put(t, (j, slice(None), slice(0, FF_SHARD)), part[...].astype(BF16))
                put(t, (j, slice(None), slice(FF_SHARD, FF_PAD)), jnp.zeros((rows, FF_PAD - FF_SHARD), BF16))
            put(t + 1, (slice(0, FF_SHARD), slice(None)), d[...].astype(BF16))
            put(t + 1, (slice(FF_SHARD, FF_PAD), slice(None)), jnp.zeros((FF_PAD - FF_SHARD, D_MODEL), BF16))

    def tiled(width):
        return pl.BlockSpec((None, rows, width), lambda i, me: (0, i, 0))

    def layer(l, width):
        return pl.BlockSpec((None, rows, width), lambda i, me: (l, i, 0))

    def whole_layer(l):
        return pl.BlockSpec((None, FF_SHARD, D_MODEL), lambda i, me: (l, 0, 0))

    in_specs = [tiled(na_cols), pl.BlockSpec((None, rows_o, D_MODEL), lambda i, me: (0, i, 0)), tiled(dil_cols),
                pl.BlockSpec((None, rows_o, D_MODEL), lambda i, me: (0, i, 0)),
                layer(0, FF_SHARD), layer(0, FF_SHARD), whole_layer(0), layer(1, FF_SHARD), layer(1, FF_SHARD), whole_layer(1)]
    o_shard = pl.BlockSpec((rows_o, D_MODEL), lambda i, me: (i, 0))
    o_land = pl.BlockSpec((None, rows_o, D_MODEL), lambda i, me: (me[0], i, 0))
    gu_shard = pl.BlockSpec((2, rows, FF_PAD), lambda i, me: (0, i, 0))
    gu_land = pl.BlockSpec((None, 2, rows, FF_PAD), lambda i, me: (me[0], 0, i, 0))
    down_shard = pl.BlockSpec((FF_PAD, D_MODEL), lambda i, me: (0, 0))
    down_land = pl.BlockSpec((None, FF_PAD, D_MODEL), lambda i, me: (me[0], 0, 0))

    def qkv_shard(width):
        return pl.BlockSpec((rows, width), lambda i, me: (i, 0))

    def qkv_land(width):
        return pl.BlockSpec((rows, width), lambda i, me: (i, me[0]))

    shard_specs = [qkv_shard(na_cols), o_shard, gu_shard, down_shard, qkv_shard(dil_cols), o_shard, gu_shard, down_shard]
    land_specs = [qkv_land(na_cols), o_land, gu_land, down_land, qkv_land(dil_cols), o_land, gu_land, down_land]
    shard_shapes = [jax.ShapeDtypeStruct(s, BF16) for s in
                    ((D_MODEL, na_cols), (o_rows, D_MODEL), (2, D_MODEL, FF_PAD), (FF_PAD, D_MODEL),
                     (D_MODEL, dil_cols), (o_rows, D_MODEL), (2, D_MODEL, FF_PAD), (FF_PAD, D_MODEL))]
    result = pl.pallas_call(
        body, out_shape=tuple(shard_shapes + list(land_shapes)),
        grid_spec=pltpu.PrefetchScalarGridSpec(num_scalar_prefetch=1, grid=(tiles,), in_specs=in_specs,
                                               out_specs=tuple(shard_specs + land_specs)),
        name="prep_weights", compiler_params=_params())(me, na_qkv, na_o, dil_qkv, dil_o, gate, up, down, gate, up, down)
    return list(result[:8]), list(result[8:])


def _remote_copies(sets, src_refs, land_refs, send_sems, recv_sems, outgoing):
    me = _position()
    copies = []
    for t, (si, src_of, li, dst_of) in enumerate(sets):
        for k in range(1, N_DEV):
            other = _peer(me, k)
            sender = me if outgoing else other
            copies.append(pltpu.make_async_remote_copy(
                src_ref=src_of(src_refs[si], _flat(other)), dst_ref=dst_of(land_refs[li], _flat(sender), k),
                send_sem=send_sems.at[(N_DEV - 1) * t + k - 1], recv_sem=recv_sems.at[(N_DEV - 1) * t + k - 1],
                device_id=other, device_id_type=MESH))
    return copies


def _send_start(name, srcs, lands, sets_by_group):
    n_src, n_land, n_groups = len(srcs), len(lands), len(sets_by_group)

    def body(*refs):
        src_refs, land_refs = refs[:n_src], refs[n_src:n_src + n_land]
        outs = refs[n_src + n_land:]
        for g, sets in enumerate(sets_by_group):
            for cp in _remote_copies(sets, src_refs, land_refs, outs[2 * g], outs[2 * g + 1], True):
                cp.start()
        outs[-1][...] = jnp.zeros_like(outs[-1])

    sem_shapes = []
    for sets in sets_by_group:
        sem_shapes += [pltpu.SemaphoreType.DMA((len(sets) * (N_DEV - 1),))] * 2
    thru = [pltpu.HBM(a.shape, a.dtype) for a in list(srcs) + list(lands)]
    n_sem = len(sem_shapes)
    result = pl.pallas_call(
        body, out_shape=tuple(sem_shapes + thru + [jax.ShapeDtypeStruct((8, 128), F32)]),
        in_specs=[HBM] * (n_src + n_land),
        out_specs=tuple([SEM] * n_sem + [HBM] * (n_src + n_land) + [pl.BlockSpec(memory_space=pltpu.VMEM)]),
        input_output_aliases={i: n_sem + i for i in range(n_src + n_land)},
        compiler_params=pltpu.CompilerParams(has_side_effects=EFFECT), name=name,
    )(*[pltpu.with_memory_space_constraint(a, pltpu.HBM) for a in list(srcs) + list(lands)])
    sems = [(result[2 * g], result[2 * g + 1]) for g in range(n_groups)]
    return sems, list(result[n_sem:n_sem + n_src]), list(result[n_sem + n_src:n_sem + n_src + n_land]), result[-1]


def _send_wait(name, sems, srcs, lands, sets, after):
    n_src, n_land = len(srcs), len(lands)

    def body(*refs):
        src_refs, land_refs = refs[:n_src], refs[n_src:n_src + n_land]
        send_sems, recv_sems = refs[n_src + n_land], refs[n_src + n_land + 1]
        for cp in _remote_copies(sets, src_refs, land_refs, send_sems, recv_sems, True):
            cp.wait_send()
        for cp in _remote_copies(sets, src_refs, land_refs, send_sems, recv_sems, False):
            cp.wait_recv()

    thru = [pltpu.HBM(a.shape, a.dtype) for a in list(srcs) + list(lands)]
    result = pl.pallas_call(
        body, out_shape=tuple(thru), in_specs=[HBM] * (n_src + n_land) + [SEM, SEM] + [ANY] * len(after),
        out_specs=tuple([HBM] * (n_src + n_land)), input_output_aliases={i: i for i in range(n_src + n_land)},
        compiler_params=pltpu.CompilerParams(has_side_effects=EFFECT), name=name,
    )(*srcs, *lands, sems[0], sems[1], *after)
    return list(result[:n_src]), list(result[n_src:])


DIRECT = (1, 2, 4, 6)
PASSED = DIRECT[1:]


def _hbm_passthrough(body, name, arrays, n_sem_in, sem_out_shapes, extra):
    n, n_out = len(arrays), len(sem_out_shapes)
    return pl.pallas_call(
        body, out_shape=tuple(list(sem_out_shapes) + [pltpu.HBM(a.shape, a.dtype) for a in arrays]),
        in_specs=[HBM] * n + [SEM] * n_sem_in + [ANY] * len(extra), out_specs=tuple([SEM] * n_out + [HBM] * n),
        input_output_aliases={i: n_out + i for i in range(n)},
        compiler_params=pltpu.CompilerParams(has_side_effects=EFFECT), name=name)


def _shard_copy(src_ref, land_ref, window, block, to, send_sem, recv_sem, from_shard):
    dst = window(land_ref, _flat(block))
    return pltpu.make_async_remote_copy(src_ref=src_ref if from_shard else dst, dst_ref=dst, send_sem=send_sem,
                                        recv_sem=recv_sem, device_id=to, device_id_type=MESH)


def _gather_start(name, shards, lands, windows, group_sizes):
    n = len(shards)

    def body(*refs):
        shard_refs, land_refs, outs = refs[:n], refs[n:2 * n], refs[2 * n:]
        me = _position()
        t = 0
        for g, size in enumerate(group_sizes):
            for local in range(size):
                for j, k in enumerate(DIRECT):
                    i = len(DIRECT) * local + j
                    _shard_copy(shard_refs[t], land_refs[t], windows[t], me, _peer(me, k), outs[2 * g].at[i],
                                outs[2 * g + 1].at[i], True).start()
                t += 1

    sem_shapes = [pltpu.SemaphoreType.DMA((len(DIRECT) * size,)) for size in group_sizes for _ in range(2)]
    arrays = [pltpu.with_memory_space_constraint(a, pltpu.HBM) for a in list(shards) + list(lands)]
    result = _hbm_passthrough(body, name, arrays, 0, sem_shapes, ())(*arrays)
    n_sem = len(sem_shapes)
    sems = [(result[2 * g], result[2 * g + 1]) for g in range(len(group_sizes))]
    return sems, list(result[n_sem:n_sem + n]), list(result[n_sem + n:])


def _gather_pass_on(name, sems, shards, lands, windows, after):
    n = len(shards)

    def body(*refs):
        shard_refs, land_refs = refs[:n], refs[n:2 * n]
        recv_sems = refs[2 * n + 1]
        pass_send, pass_recv = refs[2 * n + 2 + len(after)], refs[2 * n + 3 + len(after)]
        me = _position()
        sibling = _peer(me, 1)
        for t in range(n):
            for j, k in enumerate(PASSED):
                sender = _peer(me, k)
                arrived = len(DIRECT) * t + 1 + j
                _shard_copy(shard_refs[t], land_refs[t], windows[t], sender, me, refs[2 * n].at[arrived], recv_sems.at[arrived],
                            True).wait_recv()
                i = len(PASSED) * t + j
                _shard_copy(shard_refs[t], land_refs[t], windows[t], sender, sibling, pass_send.at[i], pass_recv.at[i],
                            False).start()

    sem_shapes = [pltpu.SemaphoreType.DMA((len(PASSED) * n,))] * 2
    result = _hbm_passthrough(body, name, list(shards) + list(lands), 2, sem_shapes, after)(
        *shards, *lands, sems[0], sems[1], *after)
    return (result[0], result[1]), list(result[2:2 + n]), list(result[2 + n:])


def _gather_wait(name, sems, pass_sems, shards, lands, windows, after):
    n = len(shards)

    def body(*refs):
        shard_refs, land_refs = refs[:n], refs[n:2 * n]
        send_sems, recv_sems, pass_send, pass_recv = refs[2 * n:2 * n + 4]
        me = _position()
        sibling = _peer(me, 1)
        for t in range(n):
            for j, k in enumerate(DIRECT):
                i = len(DIRECT) * t + j
                _shard_copy(shard_refs[t], land_refs[t], windows[t], me, _peer(me, k), send_sems.at[i], recv_sems.at[i],
                            True).wait_send()
            _shard_copy(shard_refs[t], land_refs[t], windows[t], sibling, me, send_sems.at[len(DIRECT) * t],
                        recv_sems.at[len(DIRECT) * t], True).wait_recv()
            for j, k in enumerate(PASSED):
                i = len(PASSED) * t + j
                _shard_copy(shard_refs[t], land_refs[t], windows[t], _peer(me, k), sibling, pass_send.at[i], pass_recv.at[i],
                            False).wait_send()
                _shard_copy(shard_refs[t], land_refs[t], windows[t], _peer(sibling, k), me, pass_send.at[i], pass_recv.at[i],
                            False).wait_recv()

    result = _hbm_passthrough(body, name, list(shards) + list(lands), 4, [], after)(
        *shards, *lands, sems[0], sems[1], pass_sems[0], pass_sems[1], *after)
    return list(result[n:])


def _all_gather(name, locals_, out_shapes, windows, deps=()):
    n = len(locals_)

    def body(*refs):
        src_refs, out_refs = refs[:n], refs[n + len(deps):2 * n + len(deps)]
        send_sems, recv_sems, local_sems = refs[2 * n + len(deps):]
        x, y, c = _position()
        me, sibling = (x, y, c), (x, y, 1 - c)
        chips = [(1 - x, y), (x, 1 - y), (1 - x, 1 - y)]

        def copy(t, k, block, to, from_local=False):
            dst = windows[t](out_refs[t], _flat(block))
            return pltpu.make_async_remote_copy(
                src_ref=src_refs[t] if from_local else dst, dst_ref=dst, send_sem=send_sems.at[t, k],
                recv_sem=recv_sems.at[t, k], device_id=to, device_id_type=MESH)

        mine = [pltpu.make_async_copy(src_refs[t], windows[t](out_refs[t], _flat(me)), local_sems.at[t]) for t in range(n)]
        sends = []
        for t in range(n):
            mine[t].start()
            sends.append(copy(t, 0, me, sibling, True))
            sends += [copy(t, 1 + j, me, (*chip, c), True) for j, chip in enumerate(chips)]
        for cp in sends:
            cp.start()
        for t in range(n):
            for j, chip in enumerate(chips):
                copy(t, 1 + j, (*chip, c), me).wait_recv()
                passed = copy(t, 4 + j, (*chip, c), sibling)
                passed.start()
                sends.append(passed)
        for t in range(n):
            copy(t, 0, sibling, me).wait_recv()
            for j, chip in enumerate(chips):
                copy(t, 4 + j, (*chip, 1 - c), me).wait_recv()
        for cp in sends:
            cp.wait_send()
        for cp in mine:
            cp.wait()

    return pl.pallas_call(
        body, out_shape=tuple(out_shapes), in_specs=[ANY] * (n + len(deps)), out_specs=tuple([ANY] * n),
        scratch_shapes=[pltpu.SemaphoreType.DMA((n, 7)), pltpu.SemaphoreType.DMA((n, 7)), pltpu.SemaphoreType.DMA((n,))],
        name=name)(*locals_, *deps)


def _adamw(name, me, lands, owns, w, m, v, *, grid, land_specs, own_specs, p_spec):
    n_land = len(lands)

    def body(me_ref, *refs):
        land_refs, own_refs = refs[:n_land], refs[n_land:n_land + len(owns)]
        w_ref, m_ref, v_ref, g_ref, delta_ref, m_out, v_out = refs[n_land + len(owns):]
        ncols = w_ref.shape[-1]
        sums = []
        for i, land_ref in enumerate(land_refs):
            g = own_refs[i][...].astype(F32) if owns else land_ref[0].astype(F32)
            for s in range(0 if owns else 1, land_ref.shape[0]):
                g = g + land_ref[s].astype(F32)
            sums.append(g[:, :ncols])
        g = sums[0] if n_land == 1 else jnp.where(pl.program_id(0) == 0, sums[0], sums[1])
        m_new = ADAM_B1 * m_ref[...] + (1.0 - ADAM_B1) * g
        v_new = ADAM_B2 * v_ref[...] + (1.0 - ADAM_B2) * jnp.square(g)
        m_hat = m_new / (1.0 - ADAM_B1 ** ADAM_STEP)
        v_hat = v_new / (1.0 - ADAM_B2 ** ADAM_STEP)
        g_ref[...] = g
        delta_ref[...] = -ADAM_LR * (m_hat / (jnp.sqrt(v_hat) + ADAM_EPS) + ADAM_WD * w_ref[...])
        m_out[...] = m_new
        v_out[...] = v_new

    shape = jax.ShapeDtypeStruct(w.shape, F32)
    return pl.pallas_call(
        body, out_shape=(shape,) * 4,
        grid_spec=pltpu.PrefetchScalarGridSpec(
            num_scalar_prefetch=1, grid=grid, in_specs=list(land_specs) + list(own_specs) + [p_spec, p_spec, p_spec],
            out_specs=(p_spec,) * 4),
        name=name, compiler_params=_params())(me, *lands, *owns, w, m, v)


def _row(p, layer):
    return p[layer][None, :]


def _square(name, a, b, dims, out_dtype, deps=()):
    if a.shape == (D_MODEL, SEQ):
        return _matmul(name, a, b, grid=(2, 1), a_spec=pl.BlockSpec((512, SEQ), lambda i, k: (i, 0)),
                       b_spec=pl.BlockSpec((SEQ, D_MODEL), lambda i, k: (0, 0)),
                       o_spec=pl.BlockSpec((512, D_MODEL), lambda i, k: (i, 0)),
                       out_shape=jax.ShapeDtypeStruct((D_MODEL, D_MODEL), out_dtype), dims=NN, acc_shape=(8, 128),
                       deps=deps)
    return _matmul(name, a, b, grid=(SEQ // TM_MM, 1), a_spec=pl.BlockSpec((TM_MM, D_MODEL), lambda i, k: (i, 0)),
                   b_spec=pl.BlockSpec((D_MODEL, D_MODEL), lambda i, k: (0, 0)),
                   o_spec=pl.BlockSpec((TM_MM, D_MODEL), lambda i, k: (i, 0)),
                   out_shape=jax.ShapeDtypeStruct((SEQ, D_MODEL), out_dtype), dims=dims, acc_shape=(8, 128), deps=deps)


def _grouped_matmul(name, a_list, b, *, n_tiles, a_block, b_spec, o_spec, out_shape):
    n_groups = len(a_list)

    def a_spec(g):
        def index(j, i):
            mine = j // 3
            return (jnp.where(mine == g, i, jnp.where(mine < g, 0, n_tiles - 1)), 0)
        return pl.BlockSpec(a_block, index)

    def body(*refs):
        b_ref, o_ref = refs[n_groups], refs[n_groups + 1]
        mine = pl.program_id(0) // 3
        for g in range(n_groups):
            @pl.when(mine == g)
            def _(g=g):
                o_ref[...] = jnp.dot(refs[g][...], b_ref[...], preferred_element_type=F32).astype(o_ref.dtype)

    return pl.pallas_call(
        body, out_shape=out_shape, grid=(3 * n_groups, n_tiles), in_specs=[a_spec(g) for g in range(n_groups)] + [b_spec],
        out_specs=o_spec, name=name, compiler_params=_params())(*a_list, b)


def _qkv_fwd(name, hs, w):
    return _grouped_matmul(name, hs, w, n_tiles=SEQ // TM_MM, a_block=(TM_MM, D_MODEL),
                           b_spec=pl.BlockSpec((D_MODEL, D_MODEL), lambda j, i: (0, j)),
                           o_spec=pl.BlockSpec((TM_MM, D_MODEL), lambda j, i: (i, j)),
                           out_shape=jax.ShapeDtypeStruct((SEQ, 3 * len(hs) * D_MODEL), BF16))


def _qkv_dw(name, hs_t, dqkv):
    return _grouped_matmul(name, hs_t, dqkv, n_tiles=2, a_block=(512, SEQ),
                           b_spec=pl.BlockSpec((None, SEQ, D_MODEL), lambda j, i: (j, 0, 0)),
                           o_spec=pl.BlockSpec((512, D_MODEL), lambda j, i: (i, j)),
                           out_shape=jax.ShapeDtypeStruct((D_MODEL, 3 * len(hs_t) * D_MODEL), BF16))


def _proj_do_sorted(name, d_a, w_o):
    def body(da_ref, w_ref, *refs):
        value = lax.dot_general(da_ref[...], w_ref[...], NT, preferred_element_type=F32)
        refs[0][...] = value
        for j, dil in enumerate(SORTED):
            _sort_tile(refs[-1], value, dil, refs[1 + j])

    tile = pl.BlockSpec((TM, D_MODEL), lambda i: (i, 0))
    shapes = [jax.ShapeDtypeStruct((SEQ, D_MODEL), F32)] + [jax.ShapeDtypeStruct((dil, SEQ // dil, D_MODEL), F32) for dil in SORTED]
    result = pl.pallas_call(
        body, out_shape=tuple(shapes), grid=(SEQ // TM,),
        in_specs=[tile, pl.BlockSpec((D_MODEL, D_MODEL), lambda i: (0, 0))],
        out_specs=tuple([tile] + [_sorted_spec(dil) for dil in SORTED]), scratch_shapes=[_sort_scratch()],
        name=name, compiler_params=_params())(d_a, w_o)
    return [t.reshape(SEQ, D_MODEL) for t in result]


def _qkv_dh(name, dqkv, w, n_chunks, deps):
    tm = TM_MM // 2
    return _matmul(name, dqkv, w, grid=(n_chunks // 3, SEQ // tm, 1),
                   a_spec=pl.BlockSpec((3, tm, D_MODEL), lambda g, i, k: (g, i, 0)),
                   b_spec=pl.BlockSpec((D_MODEL, 3 * D_MODEL), lambda g, i, k: (0, g)),
                   o_spec=pl.BlockSpec((None, tm, D_MODEL), lambda g, i, k: (g, i, 0)),
                   out_shape=jax.ShapeDtypeStruct((n_chunks // 3, SEQ, D_MODEL), F32), dims=NT, acc_shape=(8, 128),
                   deps=deps, inner=3)


def _local_step(x, target, norms, rpb, fetch, emit, deps):
    mix_pre, mix_post, ffn_pre, ffn_post = norms
    slopes = 2.0 ** (-8.0 * jnp.arange(1, N_HEADS + 1, dtype=F32) / N_HEADS)
    rpb_pad = jnp.pad(rpb, ((0, 0), (0, 1), (0, 128 - 31)))
    saved = []

    for layer in range(2):
        tag = f"l{layer}"
        if layer == 0:
            h = _rms_fwd(tag + "_norm_mix", x, _row(mix_pre, layer), out_dtype=BF16, deps=deps)
            hs = [h]
            table = _rpb_table(rpb_pad)
            w_qkv, w_o = fetch("na", [table, h], [h])
            qkv = _qkv_fwd(tag + "_qkv", hs, w_qkv)
            o, lse = _na_fwd(qkv, table)
            mixer = (hs, qkv, o, lse, table)
        else:
            hs = [t.reshape(SEQ, D_MODEL) for t in
                  _rms_fwd(tag + "_norm_mix", x, _row(mix_pre, layer), out_dtype=BF16, sorted_too=True)]
            w_qkv, w_o = fetch("dil", [x], [hs[0]])
            qkv = _qkv_fwd(tag + "_qkv", hs, w_qkv)
            outs, lses = zip(*[_dil_fwd(g, qkv, slopes * dil) for g, (_, dil) in enumerate(DIL_GROUPS)])
            merged, lse_total = _dil_merge(outs, lses)
            o = merged[0]
            mixer = (hs, qkv, merged, lses, lse_total)
        a = _square(tag + "_proj", o, w_o, NN, F32)
        x1 = _rms_fwd(tag + "_post_mix", a, _row(mix_post, layer), res=x)
        h2 = _rms_fwd(tag + "_norm_ffn", x1, _row(ffn_pre, layer), out_dtype=BF16)
        w_gu, w_down = fetch(f"ffn{layer}", [a], [h2])
        f, hg, hu, act_t = _ffn_fwd(tag + "_ffn", h2, w_gu, w_down)
        x2 = _rms_fwd(tag + "_post_ffn", f, _row(ffn_post, layer), res=x1)
        transposed = ([t.T for t in hs], o.astype(BF16).T, h2.T, act_t)
        saved.append((x, mixer, a, x1, transposed, hg, hu, f, w_qkv, w_o, w_gu, w_down))
        x = x2

    dx, loss = _loss_head("loss_head", x, target)
    d_norm = {k: [None, None] for k in ("mix_pre", "mix_post", "ffn_pre", "ffn_post")}
    d_rpb = None

    for layer in (1, 0):
        tag = f"b{layer}"
        x0, mixer, a, x1, (h_t, o_t, h2_t, act_t), hg, hu, f, w_qkv, w_o, w_gu, w_down = saved[layer]
        d_f, d_norm["ffn_post"][layer] = _rms_bwd(tag + "_post_ffn", f, _row(ffn_post, layer), [dx], out_dtype=BF16)
        dgu, d_h2 = _ffn_bwd(tag + "_ffn", d_f, w_gu, w_down, hg, hu)
        d_down = _matmul(
            tag + "_ffn_ddown", act_t, d_f, grid=(N_DEV, 1),
            a_spec=pl.BlockSpec((None, FF_PAD, SEQ), lambda d, k: (d, 0, 0)),
            b_spec=pl.BlockSpec((SEQ, D_MODEL), lambda d, k: (0, 0)),
            o_spec=pl.BlockSpec((None, FF_PAD, D_MODEL), lambda d, k: (d, 0, 0)),
            out_shape=jax.ShapeDtypeStruct((N_DEV, FF_PAD, D_MODEL), BF16), dims=NN, acc_shape=(8, 128))
        d_gu = _ffn_dgu(tag + "_ffn_dgu", h2_t, dgu)
        sent = emit(f"ffn{layer}", [d_gu, d_down])
        dx1, d_norm["ffn_pre"][layer] = _rms_bwd(tag + "_norm_ffn", x1, _row(ffn_pre, layer), [d_h2], res=dx, deps=sent)
        d_a, d_norm["mix_post"][layer] = _rms_bwd(tag + "_post_mix", a, _row(mix_post, layer), [dx1], out_dtype=BF16)
        d_wo = _square(tag + "_proj_dw", o_t, d_a, NN, BF16)
        if layer == 0:
            _, qkv, o, lse, table = mixer
            d_o = _square(tag + "_proj_do", d_a, w_o, NT, BF16)
            dqkv, gp = _na_bwd(qkv, table, d_o, lse)
            d_rpb = _rpb_grad(gp)[:, :15, :31]
            sent = emit("na", [_qkv_dw(tag + "_qkv_dw", h_t, dqkv), d_wo])
            d_h = _qkv_dh(tag + "_qkv_dh", dqkv, w_qkv, 3, sent)
            dx, d_norm["mix_pre"][layer] = _rms_bwd(tag + "_norm_mix", x0, _row(mix_pre, layer), [d_h[0]], res=dx1)
        else:
            _, qkv, merged, lses, lse_total = mixer
            d_o = _proj_do_sorted(tag + "_proj_do", d_a, w_o)
            dqkv = lax.empty((3 * len(DIL_GROUPS), SEQ, D_MODEL), BF16)
            for g, (_, dil) in enumerate(DIL_GROUPS):
                dqkv = _dil_bwd(g, qkv, slopes * dil, d_o[g], merged[g], lses[g], lse_total[g], dqkv)
            sent = emit("dil", [_qkv_dw(tag + "_qkv_dw", h_t, dqkv), d_wo])
            d_h = _qkv_dh(tag + "_qkv_dh", dqkv, w_qkv, 9, sent)
            dx, d_norm["mix_pre"][layer] = _rms_bwd(tag + "_norm_mix", x0, _row(mix_pre, layer), None, res=dx1, groups=d_h)

    d_gains = [jnp.concatenate(d_norm[k], axis=0) for k in ("mix_pre", "mix_post", "ffn_pre", "ffn_post")]
    return loss, dx, d_gains, d_rpb


RPB_SIZE = N_HEADS * 15 * 31


def _pack_small(gains, rpb, last=None):
    top = jnp.concatenate(gains, axis=0).reshape(64, 128)
    bottom = jnp.pad(rpb.reshape(-1), (0, 64 * 128 - RPB_SIZE))
    if last is not None:
        bottom = bottom + jnp.pad(last.reshape(1), (64 * 128 - 1, 0))
    return jnp.concatenate([top, bottom.reshape(64, 128)], axis=0)


def _unpack_small(p):
    gains = p[:64].reshape(4, 2, D_MODEL)
    rpb = p[64:].reshape(-1)[:RPB_SIZE].reshape(1, N_HEADS, 15, 31)
    return [gains[i] for i in range(4)], rpb


GROUPS = ("na", "ffn0", "dil", "ffn1")


def kernel(x, norm_mix_pre, norm_mix_post, norm_ffn_pre, norm_ffn_post, na_w_qkv, na_w_o, na_rpb, dil_w_qkv, dil_w_o, ffn_w_gate, ffn_w_up, ffn_w_down, loss_target, m_norm_mix_pre, m_norm_mix_post, m_norm_ffn_pre, m_norm_ffn_post, m_na_w_qkv, m_na_w_o, m_na_rpb, m_dil_w_qkv, m_dil_w_o, m_ffn_w_gate, m_ffn_w_up, m_ffn_w_down, v_norm_mix_pre, v_norm_mix_post, v_norm_ffn_pre, v_norm_ffn_post, v_na_w_qkv, v_na_w_o, v_na_rpb, v_dil_w_qkv, v_dil_w_o, v_ffn_w_gate, v_ffn_w_up, v_ffn_w_down):
    na_cols, dil_cols, o_rows = 3 * D_MODEL // N_DEV, 9 * D_MODEL // N_DEV, D_MODEL // N_DEV
    ff_pad = FF_PAD - FF_SHARD
    me = (4 * lax.axis_index("x") + 2 * lax.axis_index("y") + lax.axis_index("c")).astype(jnp.int32).reshape(1)

    full = {
        "na": [((D_MODEL, 3 * D_MODEL), _columns(na_cols)), ((N_DEV, o_rows, D_MODEL), _leading)],
        "dil": [((D_MODEL, 9 * D_MODEL), _columns(dil_cols)), ((N_DEV, o_rows, D_MODEL), _leading)],
        "ffn0": [((N_DEV, 2, D_MODEL, FF_PAD), _leading), ((N_DEV, FF_PAD, D_MODEL), _leading)],
        "ffn1": [((N_DEV, 2, D_MODEL, FF_PAD), _leading), ((N_DEV, FF_PAD, D_MODEL), _leading)],
    }
    block = {
        "na": [(D_MODEL, na_cols), (o_rows, D_MODEL)], "dil": [(D_MODEL, dil_cols), (o_rows, D_MODEL)],
        "ffn0": [(2, D_MODEL, FF_PAD), (FF_PAD, D_MODEL)], "ffn1": [(2, D_MODEL, FF_PAD), (FF_PAD, D_MODEL)],
    }

    land_shapes = [jax.ShapeDtypeStruct(full[g][t][0], BF16) for g in GROUPS for t in range(2)]
    windows = [full[g][t][1] for g in GROUPS for t in range(2)]
    shards, lands = _prep_weights(me, na_w_qkv, na_w_o, dil_w_qkv, dil_w_o, ffn_w_gate, ffn_w_up, ffn_w_down, land_shapes)
    sems, shards, lands = _gather_start("gather_start", shards, lands, windows, [2] * len(GROUPS))

    def fetch(group, early, late):
        gi = GROUPS.index(group)
        mine = slice(2 * gi, 2 * gi + 2)
        pass_sems, shards_g, lands_g = _gather_pass_on(f"gather_pass_{group}", sems[gi], shards[mine], lands[mine],
                                                       windows[mine], early)
        qkv, o = _gather_wait(f"gather_wait_{group}", sems[gi], pass_sems, shards_g, lands_g, windows[mine], late)
        return (qkv, o.reshape(D_MODEL, D_MODEL)) if group in ("na", "dil") else (qkv, o)

    def grad_source(group, t):
        return _columns(block[group][0][1]) if (group in ("na", "dil") and t == 0) else _leading

    in_flight = {}

    def emit(group, grads):
        if group in ("na", "dil"):
            grads = [grads[0], grads[1].reshape(N_DEV, o_rows, D_MODEL)]
        sets = [(t, grad_source(group, t), t, _by_distance) for t in range(2)]
        landing = [lax.empty((N_DEV - 1,) + block[group][t], BF16) for t in range(2)]
        sems_g, grads, landing, tok = _send_start(f"exchange_start_{group}", grads, landing, [sets])
        in_flight[group] = (sems_g[0], grads, landing, sets)
        return [tok]

    norms = (norm_mix_pre, norm_mix_post, norm_ffn_pre, norm_ffn_post)
    loss, grad_x, d_gains, d_rpb = _local_step(x[0], loss_target[0], norms, na_rpb[0], fetch, emit, [shards[0]])

    landed, sent = {}, {}

    def wait_for(group, after):
        sems_g, grads, landing, sets = in_flight[group]
        sent[group], landed[group] = _send_wait(f"exchange_wait_{group}", sems_g, grads, landing, sets, after)

    for group in ("ffn1", "dil", "ffn0"):
        wait_for(group, [grad_x])

    def one(rows, tile, ncols, columns):
        own = (pl.BlockSpec((tile, ncols), lambda i, me: (i, me[0])) if columns
               else pl.BlockSpec((None, tile, ncols), lambda i, me: (me[0], i, 0)))
        return dict(grid=(rows // tile,), land_specs=[pl.BlockSpec((N_DEV - 1, tile, ncols), lambda i, me: (0, i, 0))],
                    own_specs=[own], p_spec=pl.BlockSpec((None, tile, ncols), lambda i, me: (0, i, 0)))

    def layered(block_shape, index, p_block, n_tiles):
        def specs(lead_size, lead):
            shape = (lead_size,) + block_shape
            return [pl.BlockSpec(shape, lambda l, r, me: index(lead(me), jnp.where(l == 0, r, n_tiles - 1))),
                    pl.BlockSpec(shape, lambda l, r, me: index(lead(me), jnp.where(l == 0, 0, r)))]
        return dict(grid=(2, n_tiles), land_specs=specs(N_DEV - 1, lambda me: 0), own_specs=specs(None, lambda me: me[0]),
                    p_spec=pl.BlockSpec(p_block, lambda l, r, me: (l, r, 0)))

    gu_lands, gu_owns = [landed["ffn0"][0], landed["ffn1"][0]], [sent["ffn0"][0], sent["ffn1"][0]]
    down_lands, down_owns = [landed["ffn0"][1], landed["ffn1"][1]], [sent["ffn0"][1], sent["ffn1"][1]]
    updates = {
        "dil_w_qkv": _adamw("adamw_dil_qkv", me, [landed["dil"][0]], [sent["dil"][0]], dil_w_qkv, m_dil_w_qkv, v_dil_w_qkv,
                            **one(D_MODEL, 128, dil_cols, True)),
        "dil_w_o": _adamw("adamw_dil_o", me, [landed["dil"][1]], [sent["dil"][1]], dil_w_o, m_dil_w_o, v_dil_w_o,
                          **one(o_rows, o_rows, D_MODEL, False)),
        "ffn_w_gate": _adamw("adamw_gate", me, gu_lands, gu_owns, ffn_w_gate, m_ffn_w_gate, v_ffn_w_gate,
                             **layered((None, 128, FF_PAD), lambda lead, r: (lead, 0, r, 0), (None, 128, FF_SHARD), 8)),
        "ffn_w_up": _adamw("adamw_up", me, gu_lands, gu_owns, ffn_w_up, m_ffn_w_up, v_ffn_w_up,
                           **layered((None, 128, FF_PAD), lambda lead, r: (lead, 1, r, 0), (None, 128, FF_SHARD), 8)),
        "ffn_w_down": _adamw("adamw_down", me, down_lands, down_owns, ffn_w_down, m_ffn_w_down, v_ffn_w_down,
                             **layered((176, D_MODEL), lambda lead, r: (lead, r, 0), (None, 176, D_MODEL), 2)),
    }
    done = [u[0] for u in updates.values()]
    small = _all_gather("gather_small", [_pack_small(d_gains, d_rpb, loss)], [jax.ShapeDtypeStruct((N_DEV, 128, 128), F32)],
                        [_leading], deps=done)[0]
    wait_for("na", [small])
    updates["na_w_qkv"] = _adamw("adamw_na_qkv", me, [landed["na"][0]], [sent["na"][0]], na_w_qkv, m_na_w_qkv, v_na_w_qkv,
                                 **one(D_MODEL, 256, na_cols, True))
    updates["na_w_o"] = _adamw("adamw_na_o", me, [landed["na"][1]], [sent["na"][1]], na_w_o, m_na_w_o, v_na_w_o,
                               **one(o_rows, o_rows, D_MODEL, False))
    gains = [norm_mix_pre, norm_mix_post, norm_ffn_pre, norm_ffn_post]
    m_gains = [m_norm_mix_pre, m_norm_mix_post, m_norm_ffn_pre, m_norm_ffn_post]
    v_gains = [v_norm_mix_pre, v_norm_mix_post, v_norm_ffn_pre, v_norm_ffn_post]
    packed = _adamw("adamw_small", me, [small], (), _pack_small(gains, na_rpb)[None], _pack_small(m_gains, m_na_rpb)[None],
                    _pack_small(v_gains, v_na_rpb)[None], grid=(1,),
                    land_specs=[pl.BlockSpec((N_DEV, 128, 128), lambda i, me: (0, 0, 0))], own_specs=[],
                    p_spec=pl.BlockSpec((None, 128, 128), lambda i, me: (0, 0, 0)))
    small_out = [_unpack_small(p[0]) for p in packed]

    order = ["na_w_qkv", "na_w_o", "na_rpb", "dil_w_qkv", "dil_w_o", "ffn_w_gate", "ffn_w_up", "ffn_w_down"]
    result = [packed[0][0, 127, 127], grad_x[None]]
    for kind in range(4):
        gains_k, rpb_k = small_out[kind]
        result += gains_k
        result += [rpb_k if name == "na_rpb" else updates[name][kind] for name in order]
    return tuple(result)
```

```python
import functools

import jax
import jax.numpy as jnp
from jax import lax
from jax.experimental import pallas as pl
from jax.experimental.pallas import tpu as pltpu

F32 = jnp.float32
BF16 = jnp.bfloat16
MESH = pl.DeviceIdType.MESH
ANY = pl.BlockSpec(memory_space=pl.ANY)
HBM = pl.BlockSpec(memory_space=pltpu.HBM)
SEM = pl.BlockSpec(memory_space=pltpu.SEMAPHORE)
EFFECT = pltpu.SideEffectType.DATAFLOW_SIDE_EFFECTING

N_DEV = 8
SEQ = 2048
D_MODEL = 1024
N_HEADS = 16
HEAD_DIM = 64
GRID_W = 64
NA_ROWS = 8
SEQ_ROWS = SEQ // GRID_W
DIL_GROUPS = ((128, 1), (512, 4), (2048, 16))
BAND = 128
RADIUS = 64
FF_SHARD = 352
FF_PAD = 384
RMS_EPS = 1e-6
NEG_INF = -1e30
Q_SCALE = HEAD_DIM ** -0.5

ADAM_LR = 0.001
ADAM_B1 = 0.9
ADAM_B2 = 0.999
ADAM_EPS = 1e-08
ADAM_WD = 0.01
ADAM_STEP = 10

VMEM_LIMIT = 56 * 1024 * 1024
TM = 512
TM_ROW = 1024
TM_MM = 1024

NN = (((1,), (0,)), ((), ()))
NT = (((1,), (1,)), ((), ()))
TN = (((0,), (0,)), ((), ()))


def _params():
    return pltpu.CompilerParams(vmem_limit_bytes=VMEM_LIMIT)


def _matmul(name, a, b, *, grid, a_spec, b_spec, o_spec, out_shape, dims, acc_shape, deps=(), inner=1):
    nk = grid[-1]
    kaxis = len(grid) - 1

    def body(a_ref, b_ref, *rest):
        o_ref, acc_ref = rest[-2], rest[-1]
        if inner == 1:
            part = lax.dot_general(a_ref[...].astype(BF16), b_ref[...].astype(BF16), dims, preferred_element_type=F32)
        elif len(b_ref.shape) == 2:
            a_all = jnp.concatenate([a_ref[j].astype(BF16) for j in range(inner)], axis=1)
            part = lax.dot_general(a_all, b_ref[...].astype(BF16), dims, preferred_element_type=F32)
        else:
            part = sum(lax.dot_general(a_ref[j].astype(BF16), b_ref[j].astype(BF16), dims, preferred_element_type=F32)
                       for j in range(inner))
        if nk == 1:
            o_ref[...] = part.astype(o_ref.dtype)
        else:
            k = pl.program_id(kaxis)

            @pl.when(k == 0)
            def _():
                acc_ref[...] = part

            @pl.when(k > 0)
            def _():
                acc_ref[...] += part

            @pl.when(k == nk - 1)
            def _():
                o_ref[...] = acc_ref[...].astype(o_ref.dtype)

    return pl.pallas_call(
        body, out_shape=out_shape, grid=grid, in_specs=[a_spec, b_spec] + [ANY] * len(deps), out_specs=o_spec,
        scratch_shapes=[pltpu.VMEM(acc_shape, F32)], name=name, compiler_params=_params())(a, b, *deps)


SORTED = tuple(d for _, d in DIL_GROUPS if d > 1)
LANE_CHUNKS = D_MODEL // 128


def _sort_scratch(tm=TM):
    return pltpu.VMEM((LANE_CHUNKS, tm, 128), F32)


def _sorted_view(t, dil):
    return t.reshape(dil, SEQ // dil, D_MODEL)


def _sorted_spec(dil, lead=(), tm=TM):
    return pl.BlockSpec((None,) * len(lead) + (dil, tm // dil, D_MODEL), lambda i: tuple(lead) + (0, i, 0))


def _sort_tile(scratch, value, dil, out_ref):
    tm = value.shape[0]
    for c in range(LANE_CHUNKS):
        scratch[c] = value[:, 128 * c:128 * (c + 1)]
    for r in range(dil):
        rows = [scratch.at[c][pl.ds(r, tm // dil, stride=dil), :] for c in range(LANE_CHUNKS)]
        out_ref[r] = jnp.concatenate(rows, axis=1).astype(out_ref.dtype)


def _unsort_tile(scratch, in_ref, dil):
    for r in range(dil):
        value = in_ref[r].astype(F32)
        for c in range(LANE_CHUNKS):
            scratch.at[c][pl.ds(r, value.shape[0], stride=dil), :] = value[:, 128 * c:128 * (c + 1)]
    return jnp.concatenate([scratch[c] for c in range(LANE_CHUNKS)], axis=1)


def _rms_fwd(name, x, g, res=None, out_dtype=F32, deps=(), sorted_too=False):
    tm = TM if sorted_too else TM_ROW
    n_tiles = SEQ // tm
    has_res = res is not None
    n_in = 2 + has_res + len(deps)

    def body(*refs):
        x_ref, g_ref = refs[0], refs[1]
        xv = x_ref[...]
        r = lax.rsqrt(jnp.mean(xv * xv, axis=-1, keepdims=True) + RMS_EPS)
        y = xv * r * g_ref[...]
        if has_res:
            y = refs[2][...] + y
        refs[n_in][...] = y.astype(out_dtype)
        if sorted_too:
            for j, dil in enumerate(SORTED):
                _sort_tile(refs[-1], y, dil, refs[n_in + 1 + j])

    tile = pl.BlockSpec((tm, D_MODEL), lambda i: (i, 0))
    gspec = pl.BlockSpec((1, D_MODEL), lambda i: (0, 0))
    ins = [x, g] + ([res] if has_res else []) + list(deps)
    specs = [tile, gspec] + ([tile] if has_res else []) + [ANY] * len(deps)
    shapes, out_specs = [jax.ShapeDtypeStruct((SEQ, D_MODEL), out_dtype)], [tile]
    if sorted_too:
        shapes += [jax.ShapeDtypeStruct((dil, SEQ // dil, D_MODEL), out_dtype) for dil in SORTED]
        out_specs += [_sorted_spec(dil) for dil in SORTED]
    result = pl.pallas_call(
        body, out_shape=tuple(shapes), grid=(n_tiles,), in_specs=specs, out_specs=tuple(out_specs),
        scratch_shapes=[_sort_scratch()] if sorted_too else [], name=name, compiler_params=_params())(*ins)
    return result if sorted_too else result[0]


def _rms_bwd(name, x, g, dys, res=None, out_dtype=F32, groups=None, deps=()):
    tm = TM if groups is not None else TM_ROW
    n_tiles = SEQ // tm
    n_dy = len(dys) if groups is None else 1 + len(SORTED)
    has_res = res is not None

    def body(*refs):
        x_ref, g_ref = refs[0], refs[1]
        dy_refs = refs[2:2 + n_dy]
        res_ref = refs[2 + n_dy] if has_res else None
        first_out = 2 + n_dy + has_res + len(deps)
        dx_ref, dg_ref, acc_ref = refs[first_out:first_out + 3]
        i = pl.program_id(0)
        xv = x_ref[...]
        r = lax.rsqrt(jnp.mean(xv * xv, axis=-1, keepdims=True) + RMS_EPS)
        xn = xv * r
        dy = dy_refs[0][...].astype(F32)
        for j, extra in enumerate(dy_refs[1:]):
            dy = dy + (extra[...].astype(F32) if groups is None else _unsort_tile(refs[-1], extra, SORTED[j]))
        dyg = dy * g_ref[...]
        dx = r * (dyg - xn * jnp.mean(dyg * xn, axis=-1, keepdims=True))
        if has_res:
            dx = res_ref[...] + dx
        dx_ref[...] = dx.astype(dx_ref.dtype)
        part = jnp.sum((dy * xn).reshape(tm // 8, 8, D_MODEL), axis=0)

        @pl.when(i == 0)
        def _():
            acc_ref[...] = part

        @pl.when(i > 0)
        def _():
            acc_ref[...] += part

        @pl.when(i == n_tiles - 1)
        def _():
            dg_ref[...] = jnp.broadcast_to(jnp.sum(acc_ref[...], axis=0, keepdims=True), (8, D_MODEL))

    tile = pl.BlockSpec((tm, D_MODEL), lambda i: (i, 0))
    gspec = pl.BlockSpec((1, D_MODEL), lambda i: (0, 0))
    if groups is None:
        dy_ins, dy_specs = list(dys), [tile] * n_dy
    else:
        dy_ins = [groups] + [groups.reshape(n_dy, dil, SEQ // dil, D_MODEL) for dil in SORTED]
        dy_specs = [pl.BlockSpec((None, tm, D_MODEL), lambda i: (0, i, 0))]
        dy_specs += [_sorted_spec(dil, lead=(1 + j,)) for j, dil in enumerate(SORTED)]
    ins = [x, g] + dy_ins + ([res] if has_res else []) + list(deps)
    specs = [tile, gspec] + dy_specs + ([tile] if has_res else []) + [ANY] * len(deps)
    dx, dg = pl.pallas_call(
        body, out_shape=(jax.ShapeDtypeStruct((SEQ, D_MODEL), out_dtype), jax.ShapeDtypeStruct((8, D_MODEL), F32)),
        grid=(n_tiles,), in_specs=specs,
        out_specs=(tile, pl.BlockSpec((8, D_MODEL), lambda i: (0, 0))),
        scratch_shapes=[pltpu.VMEM((8, D_MODEL), F32)] + ([_sort_scratch()] if groups is not None else []),
        name=name, compiler_params=_params())(*ins)
    return dx, dg[0:1]


def _rms(xv):
    return lax.rsqrt(jnp.mean(xv * xv, axis=-1, keepdims=True) + RMS_EPS)


def _post_norm_fwd(name, a, g_post, res, g_next, sorted_too=False):
    def body(a_ref, gp_ref, res_ref, gn_ref, x_ref, h_ref, *rest):
        av = a_ref[...]
        xv = res_ref[...] + av * _rms(av) * gp_ref[...]
        x_ref[...] = xv
        y = xv * _rms(xv) * gn_ref[...]
        h_ref[...] = y.astype(h_ref.dtype)
        if sorted_too:
            for j, dil in enumerate(SORTED):
                _sort_tile(rest[-1], y, dil, rest[j])

    tile = pl.BlockSpec((TM, D_MODEL), lambda i: (i, 0))
    gspec = pl.BlockSpec((1, D_MODEL), lambda i: (0, 0))
    shapes = [jax.ShapeDtypeStruct((SEQ, D_MODEL), F32), jax.ShapeDtypeStruct((SEQ, D_MODEL), BF16)]
    out_specs = [tile, tile]
    if sorted_too:
        shapes += [jax.ShapeDtypeStruct((dil, SEQ // dil, D_MODEL), BF16) for dil in SORTED]
        out_specs += [_sorted_spec(dil) for dil in SORTED]
    result = pl.pallas_call(
        body, out_shape=tuple(shapes), grid=(SEQ // TM,), in_specs=[tile, gspec, tile, gspec], out_specs=tuple(out_specs),
        scratch_shapes=[_sort_scratch()] if sorted_too else [], name=name, compiler_params=_params())(a, g_post, res, g_next)
    return result[0], list(result[1:])


def _norm_post_bwd(name, x, g_norm, dys, res, a, g_post, groups=None, deps=()):
    n_tiles = SEQ // TM
    n_dy = len(dys) if groups is None else 1 + len(SORTED)

    def body(*refs):
        x_ref, g_ref = refs[0], refs[1]
        dy_refs = refs[2:2 + n_dy]
        res_ref, a_ref, gp_ref = refs[2 + n_dy:5 + n_dy]
        dx_ref, da_ref, dg_ref, dgp_ref, acc_ref, accp_ref = refs[5 + n_dy + len(deps):11 + n_dy + len(deps)]
        i = pl.program_id(0)
        xv = x_ref[...]
        r = _rms(xv)
        xn = xv * r
        dy = dy_refs[0][...].astype(F32)
        for j, extra in enumerate(dy_refs[1:]):
            dy = dy + (extra[...].astype(F32) if groups is None else _unsort_tile(refs[-1], extra, SORTED[j]))
        dyg = dy * g_ref[...]
        dx = res_ref[...] + r * (dyg - xn * jnp.mean(dyg * xn, axis=-1, keepdims=True))
        dx_ref[...] = dx
        av = a_ref[...]
        ra = _rms(av)
        an = av * ra
        dxg = dx * gp_ref[...]
        da_ref[...] = (ra * (dxg - an * jnp.mean(dxg * an, axis=-1, keepdims=True))).astype(da_ref.dtype)
        part = jnp.sum((dy * xn).reshape(TM // 8, 8, D_MODEL), axis=0)
        part_p = jnp.sum((dx * an).reshape(TM // 8, 8, D_MODEL), axis=0)

        @pl.when(i == 0)
        def _():
            acc_ref[...] = part
            accp_ref[...] = part_p

        @pl.when(i > 0)
        def _():
            acc_ref[...] += part
            accp_ref[...] += part_p

        @pl.when(i == n_tiles - 1)
        def _():
            dg_ref[...] = jnp.broadcast_to(jnp.sum(acc_ref[...], axis=0, keepdims=True), (8, D_MODEL))
            dgp_ref[...] = jnp.broadcast_to(jnp.sum(accp_ref[...], axis=0, keepdims=True), (8, D_MODEL))

    tile = pl.BlockSpec((TM, D_MODEL), lambda i: (i, 0))
    gspec = pl.BlockSpec((1, D_MODEL), lambda i: (0, 0))
    gain = pl.BlockSpec((8, D_MODEL), lambda i: (0, 0))
    if groups is None:
        dy_ins, dy_specs = list(dys), [tile] * n_dy
    else:
        dy_ins = [groups] + [groups.reshape(n_dy, dil, SEQ // dil, D_MODEL) for dil in SORTED]
        dy_specs = [pl.BlockSpec((None, TM, D_MODEL), lambda i: (0, i, 0))]
        dy_specs += [_sorted_spec(dil, lead=(1 + j,)) for j, dil in enumerate(SORTED)]
    dx, da, dg, dgp = pl.pallas_call(
        body,
        out_shape=(jax.ShapeDtypeStruct((SEQ, D_MODEL), F32), jax.ShapeDtypeStruct((SEQ, D_MODEL), BF16),
                   jax.ShapeDtypeStruct((8, D_MODEL), F32), jax.ShapeDtypeStruct((8, D_MODEL), F32)),
        grid=(n_tiles,), in_specs=[tile, gspec] + dy_specs + [tile, tile, gspec] + [ANY] * len(deps),
        out_specs=(tile, tile, gain, gain),
        scratch_shapes=[pltpu.VMEM((8, D_MODEL), F32)] * 2 + ([_sort_scratch()] if groups is not None else []),
        name=name, compiler_params=_params())(x, g_norm, *dy_ins, res, a, g_post, *deps)
    return dx, da, dg[0:1], dgp[0:1]


def _loss_head(name, y, target):
    tm = TM_ROW
    n_tiles = SEQ // tm

    def body(y_ref, t_ref, dy_ref, loss_ref, acc_ref):
        i = pl.program_id(0)
        diff = y_ref[...] - t_ref[...]
        dy_ref[...] = diff * (1.0 / D_MODEL)
        part = jnp.sum((diff * diff).reshape(tm // 8, 8, D_MODEL), axis=0)

        @pl.when(i == 0)
        def _():
            acc_ref[...] = part

        @pl.when(i > 0)
        def _():
            acc_ref[...] += part

        @pl.when(i == n_tiles - 1)
        def _():
            loss_ref[...] = jnp.full((8, 128), jnp.sum(acc_ref[...]) * (0.5 / D_MODEL), F32)

    tile = pl.BlockSpec((tm, D_MODEL), lambda i: (i, 0))
    dy, loss = pl.pallas_call(
        body, out_shape=(jax.ShapeDtypeStruct((SEQ, D_MODEL), F32), jax.ShapeDtypeStruct((8, 128), F32)),
        grid=(n_tiles,), in_specs=[tile, tile], out_specs=(tile, pl.BlockSpec((8, 128), lambda i: (0, 0))),
        scratch_shapes=[pltpu.VMEM((8, D_MODEL), F32)], name=name, compiler_params=_params())(y, target)
    return dy, loss[0, 0]


def _row_index(shape):
    return lax.broadcasted_iota(jnp.int32, shape, 0)


def _lane_index(shape):
    return lax.broadcasted_iota(jnp.int32, shape, len(shape) - 1)


def _skew_rows(t, direction):
    q = _row_index(t.shape) & (GRID_W - 1)
    for bit in range(6):
        step = 1 << bit
        shift = step if direction > 0 else 128 - step
        t = jnp.where((q & step) != 0, pltpu.roll(t, shift, 1), t)
    return t


def _rpb_table(rpb_pad):
    rows = 16 * GRID_W

    def body(r_ref, t_ref):
        lane = _lane_index((rows, 128))
        v = pltpu.roll(r_ref[...], 128 - 15, 1)
        t = _skew_rows(jnp.broadcast_to(v[:, None, :], (16, GRID_W, 128)).reshape(rows, 128), +1)
        t = jnp.where(lane < GRID_W, t, 0.0)
        below = jnp.concatenate([t[GRID_W:], jnp.zeros((GRID_W, 128), F32)], axis=0)
        first_col = jnp.clip((_row_index((rows, 128)) & (GRID_W - 1)) - 8, 0, GRID_W - 16)
        key_col = lane & (GRID_W - 1)
        in_window = (key_col >= first_col) & (key_col < first_col + 16)
        t_ref[...] = jnp.where(in_window, t + pltpu.roll(below, GRID_W, 1), NEG_INF).reshape(16, GRID_W, 128)

    return pl.pallas_call(
        body, out_shape=jax.ShapeDtypeStruct((N_HEADS, 16, GRID_W, 128), F32), grid=(N_HEADS,),
        in_specs=[pl.BlockSpec((None, 16, 128), lambda h: (h, 0, 0))],
        out_specs=pl.BlockSpec((None, 16, GRID_W, 128), lambda h: (h, 0, 0, 0)),
        name="rpb_table", compiler_params=_params())(rpb_pad)


def _rpb_grad(gp):
    rows = 16 * GRID_W

    def body(g_ref, o_ref):
        lane = _lane_index((rows, 128))
        g = g_ref[...].reshape(rows, 128)
        low = jnp.where(lane < GRID_W, g, 0.0)
        high = pltpu.roll(jnp.where(lane >= GRID_W, g, 0.0), GRID_W, 1)
        above = jnp.concatenate([jnp.zeros((GRID_W, 128), F32), high[:rows - GRID_W]], axis=0)
        diag = jnp.sum(_skew_rows(low + above, -1).reshape(16, GRID_W, 128), axis=1)
        o_ref[...] = pltpu.roll(diag, 15, 1)

    return pl.pallas_call(
        body, out_shape=jax.ShapeDtypeStruct((N_HEADS, 16, 128), F32), grid=(N_HEADS,),
        in_specs=[pl.BlockSpec((None, 16, GRID_W, 128), lambda h: (h, 0, 0, 0))],
        out_specs=pl.BlockSpec((None, 16, 128), lambda h: (h, 0, 0)),
        name="rpb_grad", compiler_params=_params())(gp)


NA_KEYS = NA_ROWS * GRID_W


def _na_window(i):
    first_row = jnp.clip(i - NA_ROWS // 2, 0, SEQ_ROWS - NA_ROWS)
    return pl.multiple_of(first_row * GRID_W, GRID_W), first_row - i + NA_ROWS - 1


NA_STEP = 32


def _head_masks():
    lane = _lane_index((1, 128))
    return (lane < HEAD_DIM, lane >= HEAD_DIM)


def _stack_heads(t, masks):
    zero = jnp.zeros_like(t)
    return jnp.concatenate([jnp.where(masks[0], t, zero), jnp.where(masks[1], t, zero)], axis=0)


def _unstack_heads(t, masks):
    n = t.shape[0] // 2
    return jnp.where(masks[0], t[:n], t[n:])


def _stack_columns(t):
    return jnp.concatenate([t[:, 0:1], t[:, HEAD_DIM:HEAD_DIM + 1]], axis=0)


def _na_scores(qs, kw, tp_ref, dr0):
    s = lax.dot_general(qs, kw, NT, preferred_element_type=F32)
    bias = jnp.concatenate(
        [jnp.concatenate([tp_ref[a, pl.ds(dr0 + 2 * c, 1), :, :].reshape(GRID_W, 128) for c in range(4)], axis=1)
         for a in range(2)], axis=0)
    return s + bias


def _na_specs():
    q_spec = pl.BlockSpec((NA_STEP * GRID_W, 128), lambda hp, i: (i, hp))
    k_spec = pl.BlockSpec((SEQ, 128), lambda hp, i: (0, 8 + hp))
    v_spec = pl.BlockSpec((SEQ, 128), lambda hp, i: (0, 16 + hp))
    tp_spec = pl.BlockSpec((2, 16, GRID_W, 128), lambda hp, i: (hp, 0, 0, 0))
    return q_spec, k_spec, v_spec, tp_spec


def _na_fwd(qkv, table):
    def body(q_ref, k_ref, v_ref, tp_ref, o_ref, lse_ref):
        masks = _head_masks()
        for r in range(NA_STEP):
            rows = slice(r * GRID_W, (r + 1) * GRID_W)
            start, dr0 = _na_window(pl.program_id(1) * NA_STEP + r)
            kw = k_ref[pl.ds(start, NA_KEYS), :]
            vw = v_ref[pl.ds(start, NA_KEYS), :]
            s = _na_scores(_stack_heads(q_ref[rows, :] * Q_SCALE, masks), kw, tp_ref, dr0)
            m = jnp.max(s, axis=-1, keepdims=True)
            p = jnp.exp(s - m)
            denom = jnp.sum(p, axis=-1, keepdims=True)
            out = jnp.dot(p.astype(BF16), vw, preferred_element_type=F32) / denom
            o_ref[rows, :] = _unstack_heads(out, masks).astype(o_ref.dtype)
            lse_ref[rows, :] = _unstack_heads(jnp.broadcast_to(m + jnp.log(denom), (2 * GRID_W, 128)), masks)

    q_spec, k_spec, v_spec, tp_spec = _na_specs()
    return pl.pallas_call(
        body, out_shape=(jax.ShapeDtypeStruct((SEQ, D_MODEL), BF16), jax.ShapeDtypeStruct((SEQ, D_MODEL), F32)),
        grid=(N_HEADS // 2, SEQ_ROWS // NA_STEP), in_specs=[q_spec, k_spec, v_spec, tp_spec],
        out_specs=(q_spec, q_spec), name="na_fwd", compiler_params=_params())(qkv, qkv, qkv, table)


def _na_bwd(qkv, table, d_out, lse):
    def body(q_ref, k_ref, v_ref, tp_ref, do_ref, lse_ref, dqkv_ref, gp_ref, dk_acc, dv_acc):
        step = pl.program_id(1)

        @pl.when(step == 0)
        def _():
            dk_acc[...] = jnp.zeros_like(dk_acc)
            dv_acc[...] = jnp.zeros_like(dv_acc)
            gp_ref[...] = jnp.zeros_like(gp_ref)

        masks = _head_masks()
        for r in range(NA_STEP):
            rows = slice(r * GRID_W, (r + 1) * GRID_W)
            i = step * NA_STEP + r
            start, dr0 = _na_window(i)
            kw = k_ref[pl.ds(start, NA_KEYS), :]
            vw = v_ref[pl.ds(start, NA_KEYS), :]
            qs = _stack_heads(q_ref[rows, :] * Q_SCALE, masks)
            dos = _stack_heads(do_ref[rows, :], masks)
            p = jnp.exp(_na_scores(qs, kw, tp_ref, dr0) - _stack_columns(lse_ref[rows, :]))
            dp = lax.dot_general(dos, vw, NT, preferred_element_type=F32)
            ds = p * (dp - jnp.sum(p * dp, axis=-1, keepdims=True))
            for a in range(2):
                for c in range(4):
                    gp_ref[a, pl.ds(dr0 + 2 * c, 1), :, :] += (
                        ds[a * GRID_W:(a + 1) * GRID_W, 128 * c:128 * (c + 1)].reshape(1, GRID_W, 128))
            dsb = ds.astype(BF16)
            dq = _unstack_heads(jnp.dot(dsb, kw, preferred_element_type=F32), masks) * Q_SCALE
            dqkv_ref[0, pl.ds(pl.multiple_of(i * GRID_W, GRID_W), GRID_W), :] = dq.astype(dqkv_ref.dtype)
            dk_acc[pl.ds(start, NA_KEYS), :] += lax.dot_general(dsb, qs, TN, preferred_element_type=F32)
            dv_acc[pl.ds(start, NA_KEYS), :] += lax.dot_general(p.astype(BF16), dos, TN, preferred_element_type=F32)

        @pl.when(step == SEQ_ROWS // NA_STEP - 1)
        def _():
            dqkv_ref[1] = dk_acc[...].astype(dqkv_ref.dtype)
            dqkv_ref[2] = dv_acc[...].astype(dqkv_ref.dtype)

    q_spec, k_spec, v_spec, tp_spec = _na_specs()
    return pl.pallas_call(
        body,
        out_shape=(jax.ShapeDtypeStruct((3, SEQ, D_MODEL), BF16), jax.ShapeDtypeStruct((N_HEADS, 16, GRID_W, 128), F32)),
        grid=(N_HEADS // 2, SEQ_ROWS // NA_STEP), in_specs=[q_spec, k_spec, v_spec, tp_spec, q_spec, q_spec],
        out_specs=(pl.BlockSpec((3, SEQ, 128), lambda hp, i: (0, 0, hp)), tp_spec),
        scratch_shapes=[pltpu.VMEM((SEQ, 128), F32), pltpu.VMEM((SEQ, 128), F32)],
        name="na_bwd", compiler_params=_params())(qkv, qkv, qkv, table, d_out, lse)


DIL_STEP = 16


def _dil_geometry(group):
    dil = DIL_GROUPS[group][1]
    sub_len = SEQ // dil
    blocks = sub_len // BAND
    return dil, sub_len, max(blocks // DIL_STEP, 1), max(DIL_STEP // blocks, 1), min(2 * BAND, sub_len)


def _dil_block(step, r, sub_len, subs):
    per_sub = DIL_STEP // subs
    return (r // per_sub) * sub_len, step * per_sub + r % per_sub


def _dil_window(b, sub_len, n_keys):
    if n_keys == sub_len:
        return 0
    return pl.multiple_of(jnp.clip(b * BAND - RADIUS, 0, sub_len - n_keys), RADIUS)


def _dil_bias(b, start, n_keys, slope_ref, hp):
    row = _row_index((2 * BAND, n_keys))
    qpos = b * BAND + (row & (BAND - 1))
    kpos = start + _lane_index((2 * BAND, n_keys))
    dist = jnp.abs(qpos - kpos)
    slope = jnp.where(row < BAND, slope_ref[2 * hp], slope_ref[2 * hp + 1])
    return slope * dist.astype(F32), dist <= RADIUS


def _dil_scores(qs, kw, penalty, valid):
    return jnp.where(valid, lax.dot_general(qs, kw, NT, preferred_element_type=F32) - penalty, NEG_INF)


def _dil_specs(group):
    dil, sub_len, steps, subs, _ = _dil_geometry(group)
    col = group * 24
    rows = DIL_STEP * BAND
    q_spec = pl.BlockSpec((rows, 128), lambda n, hp, b: (n * steps + b, col + hp))
    k_spec = pl.BlockSpec((subs * sub_len, 128), lambda n, hp, b: (n, col + 8 + hp))
    v_spec = pl.BlockSpec((subs * sub_len, 128), lambda n, hp, b: (n, col + 16 + hp))
    tile = pl.BlockSpec((rows, 128), lambda n, hp, b: (n * steps + b, hp))
    smem = pl.BlockSpec(memory_space=pltpu.SMEM)
    return (dil // subs, N_HEADS // 2, steps), q_spec, k_spec, v_spec, tile, smem


def _dil_fwd(group, qkv, slopes):
    _, sub_len, _, subs, n_keys = _dil_geometry(group)

    def body(q_ref, k_ref, v_ref, slope_ref, o_ref, lse_ref):
        hp = pl.program_id(1)
        masks = _head_masks()
        for r in range(DIL_STEP):
            rows = slice(r * BAND, (r + 1) * BAND)
            base, b = _dil_block(pl.program_id(2), r, sub_len, subs)
            start = _dil_window(b, sub_len, n_keys)
            kw = k_ref[pl.ds(base + start, n_keys), :]
            vw = v_ref[pl.ds(base + start, n_keys), :]
            penalty, valid = _dil_bias(b, start, n_keys, slope_ref, hp)
            s = _dil_scores(_stack_heads(q_ref[rows, :] * Q_SCALE, masks), kw, penalty, valid)
            m = jnp.max(s, axis=-1, keepdims=True)
            p = jnp.exp(s - m)
            denom = jnp.sum(p, axis=-1, keepdims=True)
            out = jnp.dot(p.astype(BF16), vw, preferred_element_type=F32) / denom
            o_ref[rows, :] = _unstack_heads(out, masks).astype(o_ref.dtype)
            lse_ref[rows, :] = _unstack_heads(jnp.broadcast_to(m + jnp.log(denom), (2 * BAND, 128)), masks)

    grid, q_spec, k_spec, v_spec, tile, smem = _dil_specs(group)
    return pl.pallas_call(
        body, out_shape=(jax.ShapeDtypeStruct((SEQ, D_MODEL), BF16), jax.ShapeDtypeStruct((SEQ, D_MODEL), F32)),
        grid=grid, in_specs=[q_spec, k_spec, v_spec, smem], out_specs=(tile, tile),
        name=f"dil_fwd_{group}", compiler_params=_params())(qkv, qkv, qkv, slopes)


def _dil_merge(outs, lses):
    n_sorted = len(SORTED)

    def body(*refs):
        o_refs, l_refs = refs[:3], refs[3:6]
        out_refs, lse_refs, scratch = refs[6:7 + n_sorted], refs[7 + n_sorted:8 + 2 * n_sorted], refs[-1]
        os_ = [o_refs[0][...]] + [_unsort_tile(scratch, o_refs[1 + j], dil) for j, dil in enumerate(SORTED)]
        ls = [l_refs[0][...]] + [_unsort_tile(scratch, l_refs[1 + j], dil) for j, dil in enumerate(SORTED)]
        m = jnp.maximum(jnp.maximum(ls[0], ls[1]), ls[2])
        es = [jnp.exp(v - m) for v in ls]
        total = es[0] + es[1] + es[2]
        merged = (es[0] * os_[0] + es[1] * os_[1] + es[2] * os_[2]) / total
        lse = m + jnp.log(total)
        out_refs[0][...] = merged
        lse_refs[0][...] = lse
        for j, dil in enumerate(SORTED):
            _sort_tile(scratch, merged, dil, out_refs[1 + j])
            _sort_tile(scratch, lse, dil, lse_refs[1 + j])

    tm = 256
    tile = pl.BlockSpec((tm, D_MODEL), lambda i: (i, 0))
    specs = [tile] + [_sorted_spec(dil, tm=tm) for dil in SORTED]
    shapes = [jax.ShapeDtypeStruct((SEQ, D_MODEL), F32)] + [jax.ShapeDtypeStruct((dil, SEQ // dil, D_MODEL), F32) for dil in SORTED]
    views = lambda ts: [ts[0]] + [_sorted_view(t, dil) for t, dil in zip(ts[1:], SORTED)]
    result = pl.pallas_call(
        body, out_shape=tuple(shapes * 2), grid=(SEQ // tm,), in_specs=specs * 2, out_specs=tuple(specs * 2),
        scratch_shapes=[_sort_scratch(tm)], name="dil_merge", compiler_params=_params())(*views(outs), *views(lses))
    flat = [t.reshape(SEQ, D_MODEL) for t in result]
    return flat[:1 + n_sorted], flat[1 + n_sorted:]


def _dil_bwd(group, qkv, slopes, d_out, out, lse_group, lse_total, into):
    _, sub_len, steps, subs, n_keys = _dil_geometry(group)

    def body(q_ref, k_ref, v_ref, slope_ref, do_ref, o_ref, lg_ref, lt_ref, into_ref, dqkv_ref, dk_acc, dv_acc):
        hp, step = pl.program_id(1), pl.program_id(2)

        @pl.when(step == 0)
        def _():
            dk_acc[...] = jnp.zeros_like(dk_acc)
            dv_acc[...] = jnp.zeros_like(dv_acc)

        masks = _head_masks()
        for r in range(DIL_STEP):
            rows = slice(r * BAND, (r + 1) * BAND)
            base, b = _dil_block(step, r, sub_len, subs)
            start = _dil_window(b, sub_len, n_keys)
            keys = pl.ds(base + start, n_keys)
            kw = k_ref[keys, :]
            vw = v_ref[keys, :]
            penalty, valid = _dil_bias(b, start, n_keys, slope_ref, hp)
            qs = _stack_heads(q_ref[rows, :] * Q_SCALE, masks)
            lse2 = lg_ref[rows, :]
            weight = jnp.exp(lse2 - lt_ref[rows, :])
            do2 = do_ref[rows, :]
            dogs = _stack_heads((weight * do2).astype(BF16), masks)
            delta = _stack_columns(weight) * jnp.sum(_stack_heads(do2 * o_ref[rows, :], masks), axis=-1, keepdims=True)
            p = jnp.exp(_dil_scores(qs, kw, penalty, valid) - _stack_columns(lse2))
            dp = lax.dot_general(dogs, vw, NT, preferred_element_type=F32)
            dsb = (p * (dp - delta)).astype(BF16)
            dq = _unstack_heads(jnp.dot(dsb, kw, preferred_element_type=F32), masks) * Q_SCALE
            dqkv_ref[0, pl.ds(pl.multiple_of(base + b * BAND, BAND), BAND), :] = dq.astype(dqkv_ref.dtype)
            dk_acc[keys, :] += lax.dot_general(dsb, qs, TN, preferred_element_type=F32)
            dv_acc[keys, :] += lax.dot_general(p.astype(BF16), dogs, TN, preferred_element_type=F32)

        @pl.when(step == steps - 1)
        def _():
            dqkv_ref[1] = dk_acc[...].astype(dqkv_ref.dtype)
            dqkv_ref[2] = dv_acc[...].astype(dqkv_ref.dtype)

    grid, q_spec, k_spec, v_spec, tile, smem = _dil_specs(group)
    return pl.pallas_call(
        body, out_shape=jax.ShapeDtypeStruct(into.shape, into.dtype), grid=grid,
        in_specs=[q_spec, k_spec, v_spec, smem, tile, tile, tile, tile, ANY],
        out_specs=pl.BlockSpec((3, subs * sub_len, 128), lambda n, hp, b: (group, n, hp)),
        scratch_shapes=[pltpu.VMEM((subs * sub_len, 128), F32), pltpu.VMEM((subs * sub_len, 128), F32)],
        input_output_aliases={8: 0}, name=f"dil_bwd_{group}", compiler_params=_params(),
    )(qkv, qkv, qkv, slopes, d_out, out, lse_group, lse_total, into)


def _accumulate_rows(acc_ref, i, first, part):
    rows = pl.ds(pl.multiple_of(i * TM_MM, TM_MM), TM_MM)

    @pl.when(first)
    def _():
        acc_ref[rows, :] = part

    @pl.when(jnp.logical_not(first))
    def _():
        acc_ref[rows, :] += part


def _ffn_specs():
    tile = pl.BlockSpec((TM_MM, D_MODEL), lambda d, i: (i, 0))
    gate = pl.BlockSpec((None, None, D_MODEL, FF_PAD), lambda d, i: (d, 0, 0, 0))
    up = pl.BlockSpec((None, None, D_MODEL, FF_PAD), lambda d, i: (d, 1, 0, 0))
    down = pl.BlockSpec((None, FF_PAD, D_MODEL), lambda d, i: (d, 0, 0))
    hidden = pl.BlockSpec((None, TM_MM, FF_PAD), lambda d, i: (d, i, 0))
    whole = pl.BlockSpec((SEQ, D_MODEL), lambda d, i: (0, 0))
    return tile, gate, up, down, hidden, whole


def _ffn_fwd(name, h, w_gu, w_down):
    def body(h_ref, wg_ref, wu_ref, wd_ref, f_ref, hg_ref, hu_ref, act_t_ref):
        hv = h_ref[...]
        hg = jnp.dot(hv, wg_ref[...], preferred_element_type=F32)
        hu = jnp.dot(hv, wu_ref[...], preferred_element_type=F32)
        act = hg * jax.nn.sigmoid(hg) * hu
        act_t_ref[...] = act.T.astype(act_t_ref.dtype)
        hg_ref[...] = hg.astype(hg_ref.dtype)
        hu_ref[...] = hu.astype(hu_ref.dtype)
        _accumulate_rows(f_ref, pl.program_id(1), pl.program_id(0) == 0,
                         jnp.dot(act.astype(BF16), wd_ref[...], preferred_element_type=F32))

    tile, gate, up, down, hidden, whole = _ffn_specs()
    shape = jax.ShapeDtypeStruct((N_DEV, SEQ, FF_PAD), BF16)
    return pl.pallas_call(
        body, out_shape=(jax.ShapeDtypeStruct((SEQ, D_MODEL), F32), shape, shape, jax.ShapeDtypeStruct((N_DEV, FF_PAD, SEQ), BF16)),
        grid=(N_DEV, SEQ // TM_MM), in_specs=[tile, gate, up, down],
        out_specs=(whole, hidden, hidden, pl.BlockSpec((None, FF_PAD, TM_MM), lambda d, i: (d, 0, i))),
        name=name, compiler_params=_params())(h, w_gu, w_gu, w_down)


def _ffn_dgu(name, h_t, dgu):
    def body(h_ref, dgu_ref, o_ref):
        both = jnp.dot(h_ref[...], jnp.concatenate([dgu_ref[0], dgu_ref[1]], axis=1), preferred_element_type=F32)
        o_ref[0] = both[:, :FF_PAD].astype(o_ref.dtype)
        o_ref[1] = both[:, FF_PAD:].astype(o_ref.dtype)

    return pl.pallas_call(
        body, out_shape=jax.ShapeDtypeStruct((N_DEV, 2, D_MODEL, FF_PAD), BF16), grid=(N_DEV,),
        in_specs=[pl.BlockSpec((D_MODEL, SEQ), lambda d: (0, 0)), pl.BlockSpec((2, None, SEQ, FF_PAD), lambda d: (0, d, 0, 0))],
        out_specs=pl.BlockSpec((None, 2, D_MODEL, FF_PAD), lambda d: (d, 0, 0, 0)),
        name=name, compiler_params=_params())(h_t, dgu)


def _ffn_bwd(name, d_f, w_gu, w_down, hg, hu):
    def body(df_ref, wg_ref, wu_ref, wd_ref, hg_ref, hu_ref, dgu_ref, dh_ref):
        dact = lax.dot_general(df_ref[...], wd_ref[...], NT, preferred_element_type=F32)
        hgv = hg_ref[...].astype(F32)
        sig = jax.nn.sigmoid(hgv)
        d_gate = (dact * hu_ref[...].astype(F32) * (sig * (1.0 + hgv * (1.0 - sig)))).astype(BF16)
        d_up = (dact * hgv * sig).astype(BF16)
        dgu_ref[0] = d_gate
        dgu_ref[1] = d_up
        part = lax.dot_general(jnp.concatenate([d_gate, d_up], axis=1), jnp.concatenate([wg_ref[...], wu_ref[...]], axis=1),
                               NT, preferred_element_type=F32)
        _accumulate_rows(dh_ref, pl.program_id(1), pl.program_id(0) == 0, part)

    tile, gate, up, down, hidden, whole = _ffn_specs()
    return pl.pallas_call(
        body, out_shape=(jax.ShapeDtypeStruct((2, N_DEV, SEQ, FF_PAD), BF16), jax.ShapeDtypeStruct((SEQ, D_MODEL), F32)),
        grid=(N_DEV, SEQ // TM_MM), in_specs=[tile, gate, up, down, hidden, hidden],
        out_specs=(pl.BlockSpec((2, None, TM_MM, FF_PAD), lambda d, i: (0, d, i, 0)), whole),
        name=name, compiler_params=_params())(d_f, w_gu, w_gu, w_down, hg, hu)


def _position():
    return lax.axis_index("x"), lax.axis_index("y"), lax.axis_index("c")


def _flat(p):
    return 4 * p[0] + 2 * p[1] + p[2]


def _peer(me, k):
    x, y, c = me
    return (1 - x if k & 4 else x, 1 - y if k & 2 else y, 1 - c if k & 1 else c)


def _columns(width):
    return lambda ref, d: ref.at[:, pl.ds(pl.multiple_of(d * width, 128), width)]


def _leading(ref, d):
    return ref.at[d]


def _whole(ref, d):
    return ref


def _by_sender(window):
    return lambda ref, sender, k: window(ref, sender)


def _by_distance(ref, sender, k):
    return ref.at[k - 1]


def _prep_weights(me, na_qkv, na_o, dil_qkv, dil_o, gate, up, down, land_shapes):
    na_cols, dil_cols = na_qkv.shape[-1], dil_qkv.shape[-1]
    o_rows = na_o.shape[1]
    tiles = 4
    rows, rows_o = D_MODEL // tiles, o_rows // tiles

    def body(me_ref, naq, nao, dq, do_, g0, u0, d0, g1, u1, d1, *outs):
        def put(t, index, value):
            outs[t][index] = value
            outs[8 + t][index] = value

        put(0, ..., naq[...].astype(BF16))
        put(1, ..., nao[...].astype(BF16))
        put(4, ..., dq[...].astype(BF16))
        put(5, ..., do_[...].astype(BF16))
        for t, (g, u, d) in ((2, (g0, u0, d0)), (6, (g1, u1, d1))):
            for j, part in enumerate((g, u)):
                put(t, (j, slice(None), slice(0, FF_SHARD)), part[...].astype(BF16))
                put(t, (j, slice(None), slice(FF_SHARD, FF_PAD)), jnp.zeros((rows, FF_PAD - FF_SHARD), BF16))
            put(t + 1, (slice(0, FF_SHARD), slice(None)), d[...].astype(BF16))
            put(t + 1, (slice(FF_SHARD, FF_PAD), slice(None)), jnp.zeros((FF_PAD - FF_SHARD, D_MODEL), BF16))

    def tiled(width):
        return pl.BlockSpec((None, rows, width), lambda i, me: (0, i, 0))

    def layer(l, width):
        return pl.BlockSpec((None, rows, width), lambda i, me: (l, i, 0))

    def whole_layer(l):
        return pl.BlockSpec((None, FF_SHARD, D_MODEL), lambda i, me: (l, 0, 0))

    in_specs = [tiled(na_cols), pl.BlockSpec((None, rows_o, D_MODEL), lambda i, me: (0, i, 0)), tiled(dil_cols),
                pl.BlockSpec((None, rows_o, D_MODEL), lambda i, me: (0, i, 0)),
                layer(0, FF_SHARD), layer(0, FF_SHARD), whole_layer(0), layer(1, FF_SHARD), layer(1, FF_SHARD), whole_layer(1)]
    o_shard = pl.BlockSpec((rows_o, D_MODEL), lambda i, me: (i, 0))
    o_land = pl.BlockSpec((None, rows_o, D_MODEL), lambda i, me: (me[0], i, 0))
    gu_shard = pl.BlockSpec((2, rows, FF_PAD), lambda i, me: (0, i, 0))
    gu_land = pl.BlockSpec((None, 2, rows, FF_PAD), lambda i, me: (me[0], 0, i, 0))
    down_shard = pl.BlockSpec((FF_PAD, D_MODEL), lambda i, me: (0, 0))
    down_land = pl.BlockSpec((None, FF_PAD, D_MODEL), lambda i, me: (me[0], 0, 0))

    def qkv_shard(width):
        return pl.BlockSpec((rows, width), lambda i, me: (i, 0))

    def qkv_land(width):
        return pl.BlockSpec((rows, width), lambda i, me: (i, me[0]))

    shard_specs = [qkv_shard(na_cols), o_shard, gu_shard, down_shard, qkv_shard(dil_cols), o_shard, gu_shard, down_shard]
    land_specs = [qkv_land(na_cols), o_land, gu_land, down_land, qkv_land(dil_cols), o_land, gu_land, down_land]
    shard_shapes = [jax.ShapeDtypeStruct(s, BF16) for s in
                    ((D_MODEL, na_cols), (o_rows, D_MODEL), (2, D_MODEL, FF_PAD), (FF_PAD, D_MODEL),
                     (D_MODEL, dil_cols), (o_rows, D_MODEL), (2, D_MODEL, FF_PAD), (FF_PAD, D_MODEL))]
    result = pl.pallas_call(
        body, out_shape=tuple(shard_shapes + list(land_shapes)),
        grid_spec=pltpu.PrefetchScalarGridSpec(num_scalar_prefetch=1, grid=(tiles,), in_specs=in_specs,
                                               out_specs=tuple(shard_specs + land_specs)),
        name="prep_weights", compiler_params=_params())(me, na_qkv, na_o, dil_qkv, dil_o, gate, up, down, gate, up, down)
    return list(result[:8]), list(result[8:])


def _remote_copies(sets, src_refs, land_refs, send_sems, recv_sems, outgoing):
    me = _position()
    copies = []
    for t, (si, src_of, li, dst_of) in enumerate(sets):
        for k in range(1, N_DEV):
            other = _peer(me, k)
            sender = me if outgoing else other
            copies.append(pltpu.make_async_remote_copy(
                src_ref=src_of(src_refs[si], _flat(other)), dst_ref=dst_of(land_refs[li], _flat(sender), k),
                send_sem=send_sems.at[(N_DEV - 1) * t + k - 1], recv_sem=recv_sems.at[(N_DEV - 1) * t + k - 1],
                device_id=other, device_id_type=MESH))
    return copies


def _send_start(name, srcs, lands, sets_by_group):
    n_src, n_land, n_groups = len(srcs), len(lands), len(sets_by_group)

    def body(*refs):
        src_refs, land_refs = refs[:n_src], refs[n_src:n_src + n_land]
        outs = refs[n_src + n_land:]
        for g, sets in enumerate(sets_by_group):
            for cp in _remote_copies(sets, src_refs, land_refs, outs[2 * g], outs[2 * g + 1], True):
                cp.start()
        outs[-1][...] = jnp.zeros_like(outs[-1])

    sem_shapes = []
    for sets in sets_by_group:
        sem_shapes += [pltpu.SemaphoreType.DMA((len(sets) * (N_DEV - 1),))] * 2
    thru = [pltpu.HBM(a.shape, a.dtype) for a in list(srcs) + list(lands)]
    n_sem = len(sem_shapes)
    result = pl.pallas_call(
        body, out_shape=tuple(sem_shapes + thru + [jax.ShapeDtypeStruct((8, 128), F32)]),
        in_specs=[HBM] * (n_src + n_land),
        out_specs=tuple([SEM] * n_sem + [HBM] * (n_src + n_land) + [pl.BlockSpec(memory_space=pltpu.VMEM)]),
        input_output_aliases={i: n_sem + i for i in range(n_src + n_land)},
        compiler_params=pltpu.CompilerParams(has_side_effects=EFFECT), name=name,
    )(*[pltpu.with_memory_space_constraint(a, pltpu.HBM) for a in list(srcs) + list(lands)])
    sems = [(result[2 * g], result[2 * g + 1]) for g in range(n_groups)]
    return sems, list(result[n_sem:n_sem + n_src]), list(result[n_sem + n_src:n_sem + n_src + n_land]), result[-1]


def _send_wait(name, sems, srcs, lands, sets, after):
    n_src, n_land = len(srcs), len(lands)

    def body(*refs):
        src_refs, land_refs = refs[:n_src], refs[n_src:n_src + n_land]
        send_sems, recv_sems = refs[n_src + n_land], refs[n_src + n_land + 1]
        for cp in _remote_copies(sets, src_refs, land_refs, send_sems, recv_sems, True):
            cp.wait_send()
        for cp in _remote_copies(sets, src_refs, land_refs, send_sems, recv_sems, False):
            cp.wait_recv()

    thru = [pltpu.HBM(a.shape, a.dtype) for a in list(srcs) + list(lands)]
    result = pl.pallas_call(
        body, out_shape=tuple(thru), in_specs=[HBM] * (n_src + n_land) + [SEM, SEM] + [ANY] * len(after),
        out_specs=tuple([HBM] * (n_src + n_land)), input_output_aliases={i: i for i in range(n_src + n_land)},
        compiler_params=pltpu.CompilerParams(has_side_effects=EFFECT), name=name,
    )(*srcs, *lands, sems[0], sems[1], *after)
    return list(result[:n_src]), list(result[n_src:])


DIRECT = (1, 2, 4, 6)
PASSED = DIRECT[1:]


def _hbm_passthrough(body, name, arrays, n_sem_in, sem_out_shapes, extra):
    n, n_out = len(arrays), len(sem_out_shapes)
    return pl.pallas_call(
        body, out_shape=tuple(list(sem_out_shapes) + [pltpu.HBM(a.shape, a.dtype) for a in arrays]),
        in_specs=[HBM] * n + [SEM] * n_sem_in + [ANY] * len(extra), out_specs=tuple([SEM] * n_out + [HBM] * n),
        input_output_aliases={i: n_out + i for i in range(n)},
        compiler_params=pltpu.CompilerParams(has_side_effects=EFFECT), name=name)


def _shard_copy(src_ref, land_ref, window, block, to, send_sem, recv_sem, from_shard):
    dst = window(land_ref, _flat(block))
    return pltpu.make_async_remote_copy(src_ref=src_ref if from_shard else dst, dst_ref=dst, send_sem=send_sem,
                                        recv_sem=recv_sem, device_id=to, device_id_type=MESH)


def _gather_start(name, shards, lands, windows, group_sizes):
    n = len(shards)

    def body(*refs):
        shard_refs, land_refs, outs = refs[:n], refs[n:2 * n], refs[2 * n:]
        me = _position()
        t = 0
        for g, size in enumerate(group_sizes):
            for local in range(size):
                for j, k in enumerate(DIRECT):
                    i = len(DIRECT) * local + j
                    _shard_copy(shard_refs[t], land_refs[t], windows[t], me, _peer(me, k), outs[2 * g].at[i],
                                outs[2 * g + 1].at[i], True).start()
                t += 1

    sem_shapes = [pltpu.SemaphoreType.DMA((len(DIRECT) * size,)) for size in group_sizes for _ in range(2)]
    arrays = [pltpu.with_memory_space_constraint(a, pltpu.HBM) for a in list(shards) + list(lands)]
    result = _hbm_passthrough(body, name, arrays, 0, sem_shapes, ())(*arrays)
    n_sem = len(sem_shapes)
    sems = [(result[2 * g], result[2 * g + 1]) for g in range(len(group_sizes))]
    return sems, list(result[n_sem:n_sem + n]), list(result[n_sem + n:])


def _gather_pass_on(name, sems, shards, lands, windows, after):
    n = len(shards)

    def body(*refs):
        shard_refs, land_refs = refs[:n], refs[n:2 * n]
        recv_sems = refs[2 * n + 1]
        pass_send, pass_recv = refs[2 * n + 2 + len(after)], refs[2 * n + 3 + len(after)]
        me = _position()
        sibling = _peer(me, 1)
        for t in range(n):
            for j, k in enumerate(PASSED):
                sender = _peer(me, k)
                arrived = len(DIRECT) * t + 1 + j
                _shard_copy(shard_refs[t], land_refs[t], windows[t], sender, me, refs[2 * n].at[arrived], recv_sems.at[arrived],
                            True).wait_recv()
                i = len(PASSED) * t + j
                _shard_copy(shard_refs[t], land_refs[t], windows[t], sender, sibling, pass_send.at[i], pass_recv.at[i],
                            False).start()

    sem_shapes = [pltpu.SemaphoreType.DMA((len(PASSED) * n,))] * 2
    result = _hbm_passthrough(body, name, list(shards) + list(lands), 2, sem_shapes, after)(
        *shards, *lands, sems[0], sems[1], *after)
    return (result[0], result[1]), list(result[2:2 + n]), list(result[2 + n:])


def _gather_wait(name, sems, pass_sems, shards, lands, windows, after):
    n = len(shards)

    def body(*refs):
        shard_refs, land_refs = refs[:n], refs[n:2 * n]
        send_sems, recv_sems, pass_send, pass_recv = refs[2 * n:2 * n + 4]
        me = _position()
        sibling = _peer(me, 1)
        for t in range(n):
            for j, k in enumerate(DIRECT):
                i = len(DIRECT) * t + j
                _shard_copy(shard_refs[t], land_refs[t], windows[t], me, _peer(me, k), send_sems.at[i], recv_sems.at[i],
                            True).wait_send()
            _shard_copy(shard_refs[t], land_refs[t], windows[t], sibling, me, send_sems.at[len(DIRECT) * t],
                        recv_sems.at[len(DIRECT) * t], True).wait_recv()
            for j, k in enumerate(PASSED):
                i = len(PASSED) * t + j
                _shard_copy(shard_refs[t], land_refs[t], windows[t], _peer(me, k), sibling, pass_send.at[i], pass_recv.at[i],
                            False).wait_send()
                _shard_copy(shard_refs[t], land_refs[t], windows[t], _peer(sibling, k), me, pass_send.at[i], pass_recv.at[i],
                            False).wait_recv()

    result = _hbm_passthrough(body, name, list(shards) + list(lands), 4, [], after)(
        *shards, *lands, sems[0], sems[1], pass_sems[0], pass_sems[1], *after)
    return list(result[n:])


def _all_gather(name, locals_, out_shapes, windows, deps=()):
    n = len(locals_)

    def body(*refs):
        src_refs, out_refs = refs[:n], refs[n + len(deps):2 * n + len(deps)]
        send_sems, recv_sems, local_sems = refs[2 * n + len(deps):]
        x, y, c = _position()
        me, sibling = (x, y, c), (x, y, 1 - c)
        chips = [(1 - x, y), (x, 1 - y), (1 - x, 1 - y)]

        def copy(t, k, block, to, from_local=False):
            dst = windows[t](out_refs[t], _flat(block))
            return pltpu.make_async_remote_copy(
                src_ref=src_refs[t] if from_local else dst, dst_ref=dst, send_sem=send_sems.at[t, k],
                recv_sem=recv_sems.at[t, k], device_id=to, device_id_type=MESH)

        mine = [pltpu.make_async_copy(src_refs[t], windows[t](out_refs[t], _flat(me)), local_sems.at[t]) for t in range(n)]
        sends = []
        for t in range(n):
            mine[t].start()
            sends.append(copy(t, 0, me, sibling, True))
            sends += [copy(t, 1 + j, me, (*chip, c), True) for j, chip in enumerate(chips)]
        for cp in sends:
            cp.start()
        for t in range(n):
            for j, chip in enumerate(chips):
                copy(t, 1 + j, (*chip, c), me).wait_recv()
                passed = copy(t, 4 + j, (*chip, c), sibling)
                passed.start()
                sends.append(passed)
        for t in range(n):
            copy(t, 0, sibling, me).wait_recv()
            for j, chip in enumerate(chips):
                copy(t, 4 + j, (*chip, 1 - c), me).wait_recv()
        for cp in sends:
            cp.wait_send()
        for cp in mine:
            cp.wait()

    return pl.pallas_call(
        body, out_shape=tuple(out_shapes), in_specs=[ANY] * (n + len(deps)), out_specs=tuple([ANY] * n),
        scratch_shapes=[pltpu.SemaphoreType.DMA((n, 7)), pltpu.SemaphoreType.DMA((n, 7)), pltpu.SemaphoreType.DMA((n,))],
        name=name)(*locals_, *deps)


def _adamw(name, me, lands, owns, w, m, v, *, grid, land_specs, own_specs, p_spec):
    n_land = len(lands)

    def body(me_ref, *refs):
        land_refs, own_refs = refs[:n_land], refs[n_land:n_land + len(owns)]
        w_ref, m_ref, v_ref, g_ref, delta_ref, m_out, v_out = refs[n_land + len(owns):]
        ncols = w_ref.shape[-1]
        sums = []
        for i, land_ref in enumerate(land_refs):
            g = own_refs[i][...].astype(F32) if owns else land_ref[0].astype(F32)
            for s in range(0 if owns else 1, land_ref.shape[0]):
                g = g + land_ref[s].astype(F32)
            sums.append(g[:, :ncols])
        g = sums[0] if n_land == 1 else jnp.where(pl.program_id(0) == 0, sums[0], sums[1])
        m_new = ADAM_B1 * m_ref[...] + (1.0 - ADAM_B1) * g
        v_new = ADAM_B2 * v_ref[...] + (1.0 - ADAM_B2) * jnp.square(g)
        m_hat = m_new / (1.0 - ADAM_B1 ** ADAM_STEP)
        v_hat = v_new / (1.0 - ADAM_B2 ** ADAM_STEP)
        g_ref[...] = g
        delta_ref[...] = -ADAM_LR * (m_hat / (jnp.sqrt(v_hat) + ADAM_EPS) + ADAM_WD * w_ref[...])
        m_out[...] = m_new
        v_out[...] = v_new

    shape = jax.ShapeDtypeStruct(w.shape, F32)
    return pl.pallas_call(
        body, out_shape=(shape,) * 4,
        grid_spec=pltpu.PrefetchScalarGridSpec(
            num_scalar_prefetch=1, grid=grid, in_specs=list(land_specs) + list(own_specs) + [p_spec, p_spec, p_spec],
            out_specs=(p_spec,) * 4),
        name=name, compiler_params=_params())(me, *lands, *owns, w, m, v)


def _row(p, layer):
    return p[layer][None, :]


def _square(name, a, b, dims, out_dtype, deps=()):
    if a.shape == (D_MODEL, SEQ):
        return _matmul(name, a, b, grid=(2, 1), a_spec=pl.BlockSpec((512, SEQ), lambda i, k: (i, 0)),
                       b_spec=pl.BlockSpec((SEQ, D_MODEL), lambda i, k: (0, 0)),
                       o_spec=pl.BlockSpec((512, D_MODEL), lambda i, k: (i, 0)),
                       out_shape=jax.ShapeDtypeStruct((D_MODEL, D_MODEL), out_dtype), dims=NN, acc_shape=(8, 128),
                       deps=deps)
    return _matmul(name, a, b, grid=(SEQ // TM_MM, 1), a_spec=pl.BlockSpec((TM_MM, D_MODEL), lambda i, k: (i, 0)),
                   b_spec=pl.BlockSpec((D_MODEL, D_MODEL), lambda i, k: (0, 0)),
                   o_spec=pl.BlockSpec((TM_MM, D_MODEL), lambda i, k: (i, 0)),
                   out_shape=jax.ShapeDtypeStruct((SEQ, D_MODEL), out_dtype), dims=dims, acc_shape=(8, 128), deps=deps)


def _grouped_matmul(name, a_list, b, *, n_tiles, a_block, b_spec, o_spec, out_shape):
    n_groups = len(a_list)

    def a_spec(g):
        def index(j, i):
            mine = j // 3
            return (jnp.where(mine == g, i, jnp.where(mine < g, 0, n_tiles - 1)), 0)
        return pl.BlockSpec(a_block, index)

    def body(*refs):
        b_ref, o_ref = refs[n_groups], refs[n_groups + 1]
        mine = pl.program_id(0) // 3
        for g in range(n_groups):
            @pl.when(mine == g)
            def _(g=g):
                o_ref[...] = jnp.dot(refs[g][...], b_ref[...], preferred_element_type=F32).astype(o_ref.dtype)

    return pl.pallas_call(
        body, out_shape=out_shape, grid=(3 * n_groups, n_tiles), in_specs=[a_spec(g) for g in range(n_groups)] + [b_spec],
        out_specs=o_spec, name=name, compiler_params=_params())(*a_list, b)


def _qkv_fwd(name, hs, w):
    return _grouped_matmul(name, hs, w, n_tiles=SEQ // TM_MM, a_block=(TM_MM, D_MODEL),
                           b_spec=pl.BlockSpec((D_MODEL, D_MODEL), lambda j, i: (0, j)),
                           o_spec=pl.BlockSpec((TM_MM, D_MODEL), lambda j, i: (i, j)),
                           out_shape=jax.ShapeDtypeStruct((SEQ, 3 * len(hs) * D_MODEL), BF16))


def _qkv_dw(name, hs_t, dqkv):
    return _grouped_matmul(name, hs_t, dqkv, n_tiles=2, a_block=(512, SEQ),
                           b_spec=pl.BlockSpec((None, SEQ, D_MODEL), lambda j, i: (j, 0, 0)),
                           o_spec=pl.BlockSpec((512, D_MODEL), lambda j, i: (i, j)),
                           out_shape=jax.ShapeDtypeStruct((D_MODEL, 3 * len(hs_t) * D_MODEL), BF16))


def _proj_do_sorted(name, d_a, w_o):
    def body(da_ref, w_ref, *refs):
        value = lax.dot_general(da_ref[...], w_ref[...], NT, preferred_element_type=F32)
        refs[0][...] = value
        for j, dil in enumerate(SORTED):
            _sort_tile(refs[-1], value, dil, refs[1 + j])

    tile = pl.BlockSpec((TM, D_MODEL), lambda i: (i, 0))
    shapes = [jax.ShapeDtypeStruct((SEQ, D_MODEL), F32)] + [jax.ShapeDtypeStruct((dil, SEQ // dil, D_MODEL), F32) for dil in SORTED]
    result = pl.pallas_call(
        body, out_shape=tuple(shapes), grid=(SEQ // TM,),
        in_specs=[tile, pl.BlockSpec((D_MODEL, D_MODEL), lambda i: (0, 0))],
        out_specs=tuple([tile] + [_sorted_spec(dil) for dil in SORTED]), scratch_shapes=[_sort_scratch()],
        name=name, compiler_params=_params())(d_a, w_o)
    return [t.reshape(SEQ, D_MODEL) for t in result]


def _qkv_dh(name, dqkv, w, n_chunks, deps):
    tm = TM_MM // 2
    return _matmul(name, dqkv, w, grid=(n_chunks // 3, SEQ // tm, 1),
                   a_spec=pl.BlockSpec((3, tm, D_MODEL), lambda g, i, k: (g, i, 0)),
                   b_spec=pl.BlockSpec((D_MODEL, 3 * D_MODEL), lambda g, i, k: (0, g)),
                   o_spec=pl.BlockSpec((None, tm, D_MODEL), lambda g, i, k: (g, i, 0)),
                   out_shape=jax.ShapeDtypeStruct((n_chunks // 3, SEQ, D_MODEL), F32), dims=NT, acc_shape=(8, 128),
                   deps=deps, inner=3)


def _local_step(x, target, norms, rpb, fetch, emit, deps):
    mix_pre, mix_post, ffn_pre, ffn_post = norms
    slopes = 2.0 ** (-8.0 * jnp.arange(1, N_HEADS + 1, dtype=F32) / N_HEADS)
    rpb_pad = jnp.pad(rpb, ((0, 0), (0, 1), (0, 128 - 31)))
    saved = []

    hs = [_rms_fwd("l0_norm_mix", x, _row(mix_pre, 0), out_dtype=BF16, deps=deps)]
    for layer in range(2):
        tag = f"l{layer}"
        if layer == 0:
            table = _rpb_table(rpb_pad)
            w_qkv, w_o = fetch("na", [table, hs[0]], [hs[0]])
            qkv = _qkv_fwd(tag + "_qkv", hs, w_qkv)
            o, lse = _na_fwd(qkv, table)
            mixer = (hs, qkv, o, lse, table)
        else:
            w_qkv, w_o = fetch("dil", [saved[0][7]], [hs[0]])
            qkv = _qkv_fwd(tag + "_qkv", hs, w_qkv)
            outs, lses = zip(*[_dil_fwd(g, qkv, slopes * dil) for g, (_, dil) in enumerate(DIL_GROUPS)])
            merged, lse_total = _dil_merge(outs, lses)
            o = merged[0]
            mixer = (hs, qkv, merged, lses, lse_total)
        a = _square(tag + "_proj", o, w_o, NN, F32)
        x1, (h2,) = _post_norm_fwd(tag + "_post_mix", a, _row(mix_post, layer), x, _row(ffn_pre, layer))
        w_gu, w_down = fetch(f"ffn{layer}", [a], [h2])
        f, hg, hu, act_t = _ffn_fwd(tag + "_ffn", h2, w_gu, w_down)
        transposed = ([t.T for t in hs], o.astype(BF16).T, h2.T, act_t)
        saved.append((x, mixer, a, x1, transposed, hg, hu, f, w_qkv, w_o, w_gu, w_down))
        if layer == 0:
            x, next_hs = _post_norm_fwd(tag + "_post_ffn", f, _row(ffn_post, 0), x1, _row(mix_pre, 1), sorted_too=True)
            hs = [t.reshape(SEQ, D_MODEL) for t in next_hs]
        else:
            x = _rms_fwd(tag + "_post_ffn", f, _row(ffn_post, layer), res=x1)

    dx, loss = _loss_head("loss_head", x, target)
    d_norm = {k: [None, None] for k in ("mix_pre", "mix_post", "ffn_pre", "ffn_post")}
    d_rpb = None

    d_f, d_norm["ffn_post"][1] = _rms_bwd("b1_post_ffn", saved[1][7], _row(ffn_post, 1), [dx], out_dtype=BF16)
    for layer in (1, 0):
        tag = f"b{layer}"
        x0, mixer, a, x1, (h_t, o_t, h2_t, act_t), hg, hu, f, w_qkv, w_o, w_gu, w_down = saved[layer]
        dgu, d_h2 = _ffn_bwd(tag + "_ffn", d_f, w_gu, w_down, hg, hu)
        d_down = _matmul(
            tag + "_ffn_ddown", act_t, d_f, grid=(N_DEV, 1),
            a_spec=pl.BlockSpec((None, FF_PAD, SEQ), lambda d, k: (d, 0, 0)),
            b_spec=pl.BlockSpec((SEQ, D_MODEL), lambda d, k: (0, 0)),
            o_spec=pl.BlockSpec((None, FF_PAD, D_MODEL), lambda d, k: (d, 0, 0)),
            out_shape=jax.ShapeDtypeStruct((N_DEV, FF_PAD, D_MODEL), BF16), dims=NN, acc_shape=(8, 128))
        d_gu = _ffn_dgu(tag + "_ffn_dgu", h2_t, dgu)
        sent = emit(f"ffn{layer}", [d_gu, d_down])
        dx1, d_a, d_norm["ffn_pre"][layer], d_norm["mix_post"][layer] = _norm_post_bwd(
            tag + "_norm_ffn", x1, _row(ffn_pre, layer), [d_h2], dx, a, _row(mix_post, layer), deps=sent)
        d_wo = _square(tag + "_proj_dw", o_t, d_a, NN, BF16)
        if layer == 0:
            _, qkv, o, lse, table = mixer
            d_o = _square(tag + "_proj_do", d_a, w_o, NT, BF16)
            dqkv, gp = _na_bwd(qkv, table, d_o, lse)
            d_rpb = _rpb_grad(gp)[:, :15, :31]
            sent = emit("na", [_qkv_dw(tag + "_qkv_dw", h_t, dqkv), d_wo])
            d_h = _qkv_dh(tag + "_qkv_dh", dqkv, w_qkv, 3, sent)
            dx, d_norm["mix_pre"][layer] = _rms_bwd(tag + "_norm_mix", x0, _row(mix_pre, layer), [d_h[0]], res=dx1)
        else:
            _, qkv, merged, lses, lse_total = mixer
            d_o = _proj_do_sorted(tag + "_proj_do", d_a, w_o)
            dqkv = lax.empty((3 * len(DIL_GROUPS), SEQ, D_MODEL), BF16)
            for g, (_, dil) in enumerate(DIL_GROUPS):
                dqkv = _dil_bwd(g, qkv, slopes * dil, d_o[g], merged[g], lses[g], lse_total[g], dqkv)
            sent = emit("dil", [_qkv_dw(tag + "_qkv_dw", h_t, dqkv), d_wo])
            d_h = _qkv_dh(tag + "_qkv_dh", dqkv, w_qkv, 9, sent)
            dx, d_f, d_norm["mix_pre"][1], d_norm["ffn_post"][0] = _norm_post_bwd(
                tag + "_norm_mix", x0, _row(mix_pre, 1), None, dx1, saved[0][7], _row(ffn_post, 0), groups=d_h)

    d_gains = [jnp.concatenate(d_norm[k], axis=0) for k in ("mix_pre", "mix_post", "ffn_pre", "ffn_post")]
    return loss, dx, d_gains, d_rpb


RPB_SIZE = N_HEADS * 15 * 31


def _pack_small(gains, rpb, last=None):
    top = jnp.concatenate(gains, axis=0).reshape(64, 128)
    bottom = jnp.pad(rpb.reshape(-1), (0, 64 * 128 - RPB_SIZE))
    if last is not None:
        bottom = bottom + jnp.pad(last.reshape(1), (64 * 128 - 1, 0))
    return jnp.concatenate([top, bottom.reshape(64, 128)], axis=0)


def _unpack_small(p):
    gains = p[:64].reshape(4, 2, D_MODEL)
    rpb = p[64:].reshape(-1)[:RPB_SIZE].reshape(1, N_HEADS, 15, 31)
    return [gains[i] for i in range(4)], rpb


GROUPS = ("na", "ffn0", "dil", "ffn1")


def kernel(x, norm_mix_pre, norm_mix_post, norm_ffn_pre, norm_ffn_post, na_w_qkv, na_w_o, na_rpb, dil_w_qkv, dil_w_o, ffn_w_gate, ffn_w_up, ffn_w_down, loss_target, m_norm_mix_pre, m_norm_mix_post, m_norm_ffn_pre, m_norm_ffn_post, m_na_w_qkv, m_na_w_o, m_na_rpb, m_dil_w_qkv, m_dil_w_o, m_ffn_w_gate, m_ffn_w_up, m_ffn_w_down, v_norm_mix_pre, v_norm_mix_post, v_norm_ffn_pre, v_norm_ffn_post, v_na_w_qkv, v_na_w_o, v_na_rpb, v_dil_w_qkv, v_dil_w_o, v_ffn_w_gate, v_ffn_w_up, v_ffn_w_down):
    na_cols, dil_cols, o_rows = 3 * D_MODEL // N_DEV, 9 * D_MODEL // N_DEV, D_MODEL // N_DEV
    ff_pad = FF_PAD - FF_SHARD
    me = (4 * lax.axis_index("x") + 2 * lax.axis_index("y") + lax.axis_index("c")).astype(jnp.int32).reshape(1)

    full = {
        "na": [((D_MODEL, 3 * D_MODEL), _columns(na_cols)), ((N_DEV, o_rows, D_MODEL), _leading)],
        "dil": [((D_MODEL, 9 * D_MODEL), _columns(dil_cols)), ((N_DEV, o_rows, D_MODEL), _leading)],
        "ffn0": [((N_DEV, 2, D_MODEL, FF_PAD), _leading), ((N_DEV, FF_PAD, D_MODEL), _leading)],
        "ffn1": [((N_DEV, 2, D_MODEL, FF_PAD), _leading), ((N_DEV, FF_PAD, D_MODEL), _leading)],
    }
    block = {
        "na": [(D_MODEL, na_cols), (o_rows, D_MODEL)], "dil": [(D_MODEL, dil_cols), (o_rows, D_MODEL)],
        "ffn0": [(2, D_MODEL, FF_PAD), (FF_PAD, D_MODEL)], "ffn1": [(2, D_MODEL, FF_PAD), (FF_PAD, D_MODEL)],
    }

    land_shapes = [jax.ShapeDtypeStruct(full[g][t][0], BF16) for g in GROUPS for t in range(2)]
    windows = [full[g][t][1] for g in GROUPS for t in range(2)]
    shards, lands = _prep_weights(me, na_w_qkv, na_w_o, dil_w_qkv, dil_w_o, ffn_w_gate, ffn_w_up, ffn_w_down, land_shapes)
    sems, shards, lands = _gather_start("gather_start", shards, lands, windows, [2] * len(GROUPS))

    def fetch(group, early, late):
        gi = GROUPS.index(group)
        mine = slice(2 * gi, 2 * gi + 2)
        pass_sems, shards_g, lands_g = _gather_pass_on(f"gather_pass_{group}", sems[gi], shards[mine], lands[mine],
                                                       windows[mine], early)
        qkv, o = _gather_wait(f"gather_wait_{group}", sems[gi], pass_sems, shards_g, lands_g, windows[mine], late)
        return (qkv, o.reshape(D_MODEL, D_MODEL)) if group in ("na", "dil") else (qkv, o)

    def grad_source(group, t):
        return _columns(block[group][0][1]) if (group in ("na", "dil") and t == 0) else _leading

    in_flight = {}

    def emit(group, grads):
        if group in ("na", "dil"):
            grads = [grads[0], grads[1].reshape(N_DEV, o_rows, D_MODEL)]
        sets = [(t, grad_source(group, t), t, _by_distance) for t in range(2)]
        landing = [lax.empty((N_DEV - 1,) + block[group][t], BF16) for t in range(2)]
        sems_g, grads, landing, tok = _send_start(f"exchange_start_{group}", grads, landing, [sets])
        in_flight[group] = (sems_g[0], grads, landing, sets)
        return [tok]

    norms = (norm_mix_pre, norm_mix_post, norm_ffn_pre, norm_ffn_post)
    loss, grad_x, d_gains, d_rpb = _local_step(x[0], loss_target[0], norms, na_rpb[0], fetch, emit, [shards[0]])

    landed, sent = {}, {}

    def wait_for(group, after):
        sems_g, grads, landing, sets = in_flight[group]
        sent[group], landed[group] = _send_wait(f"exchange_wait_{group}", sems_g, grads, landing, sets, after)

    for group in ("ffn1", "dil", "ffn0"):
        wait_for(group, [grad_x])

    def one(rows, tile, ncols, columns):
        own = (pl.BlockSpec((tile, ncols), lambda i, me: (i, me[0])) if columns
               else pl.BlockSpec((None, tile, ncols), lambda i, me: (me[0], i, 0)))
        return dict(grid=(rows // tile,), land_specs=[pl.BlockSpec((N_DEV - 1, tile, ncols), lambda i, me: (0, i, 0))],
                    own_specs=[own], p_spec=pl.BlockSpec((None, tile, ncols), lambda i, me: (0, i, 0)))

    def layered(block_shape, index, p_block, n_tiles):
        def specs(lead_size, lead):
            shape = (lead_size,) + block_shape
            return [pl.BlockSpec(shape, lambda l, r, me: index(lead(me), jnp.where(l == 0, r, n_tiles - 1))),
                    pl.BlockSpec(shape, lambda l, r, me: index(lead(me), jnp.where(l == 0, 0, r)))]
        return dict(grid=(2, n_tiles), land_specs=specs(N_DEV - 1, lambda me: 0), own_specs=specs(None, lambda me: me[0]),
                    p_spec=pl.BlockSpec(p_block, lambda l, r, me: (l, r, 0)))

    gu_lands, gu_owns = [landed["ffn0"][0], landed["ffn1"][0]], [sent["ffn0"][0], sent["ffn1"][0]]
    down_lands, down_owns = [landed["ffn0"][1], landed["ffn1"][1]], [sent["ffn0"][1], sent["ffn1"][1]]
    updates = {
        "dil_w_qkv": _adamw("adamw_dil_qkv", me, [landed["dil"][0]], [sent["dil"][0]], dil_w_qkv, m_dil_w_qkv, v_dil_w_qkv,
                            **one(D_MODEL, 128, dil_cols, True)),
        "dil_w_o": _adamw("adamw_dil_o", me, [landed["dil"][1]], [sent["dil"][1]], dil_w_o, m_dil_w_o, v_dil_w_o,
                          **one(o_rows, o_rows, D_MODEL, False)),
        "ffn_w_gate": _adamw("adamw_gate", me, gu_lands, gu_owns, ffn_w_gate, m_ffn_w_gate, v_ffn_w_gate,
                             **layered((None, 128, FF_PAD), lambda lead, r: (lead, 0, r, 0), (None, 128, FF_SHARD), 8)),
        "ffn_w_up": _adamw("adamw_up", me, gu_lands, gu_owns, ffn_w_up, m_ffn_w_up, v_ffn_w_up,
                           **layered((None, 128, FF_PAD), lambda lead, r: (lead, 1, r, 0), (None, 128, FF_SHARD), 8)),
        "ffn_w_down": _adamw("adamw_down", me, down_lands, down_owns, ffn_w_down, m_ffn_w_down, v_ffn_w_down,
                             **layered((176, D_MODEL), lambda lead, r: (lead, r, 0), (None, 176, D_MODEL), 2)),
    }
    done = [u[0] for u in updates.values()]
    small = _all_gather("gather_small", [_pack_small(d_gains, d_rpb, loss)], [jax.ShapeDtypeStruct((N_DEV, 128, 128), F32)],
                        [_leading], deps=done)[0]
    wait_for("na", [small])
    updates["na_w_qkv"] = _adamw("adamw_na_qkv", me, [landed["na"][0]], [sent["na"][0]], na_w_qkv, m_na_w_qkv, v_na_w_qkv,
                                 **one(D_MODEL, 256, na_cols, True))
    updates["na_w_o"] = _adamw("adamw_na_o", me, [landed["na"][1]], [sent["na"][1]], na_w_o, m_na_w_o, v_na_w_o,
                               **one(o_rows, o_rows, D_MODEL, False))
    gains = [norm_mix_pre, norm_mix_post, norm_ffn_pre, norm_ffn_post]
    m_gains = [m_norm_mix_pre, m_norm_mix_post, m_norm_ffn_pre, m_norm_ffn_post]
    v_gains = [v_norm_mix_pre, v_norm_mix_post, v_norm_ffn_pre, v_norm_ffn_post]
    packed = _adamw("adamw_small", me, [small], (), _pack_small(gains, na_rpb)[None], _pack_small(m_gains, m_na_rpb)[None],
                    _pack_small(v_gains, v_na_rpb)[None], grid=(1,),
                    land_specs=[pl.BlockSpec((N_DEV, 128, 128), lambda i, me: (0, 0, 0))], own_specs=[],
                    p_spec=pl.BlockSpec((None, 128, 128), lambda i, me: (0, 0, 0)))
    small_out = [_unpack_small(p[0]) for p in packed]

    order = ["na_w_qkv", "na_w_o", "na_rpb", "dil_w_qkv", "dil_w_o", "ffn_w_gate", "ffn_w_up", "ffn_w_down"]
    result = [packed[0][0, 127, 127], grad_x[None]]
    for kind in range(4):
        gains_k, rpb_k = small_out[kind]
        result += gains_k
        result += [rpb_k if name == "na_rpb" else updates[name][kind] for name in order]
    return tuple(result)
```

```python
import functools

import jax
import jax.numpy as jnp
from jax import lax
from jax.experimental import pallas as pl
from jax.experimental.pallas import tpu as pltpu

F32 = jnp.float32
BF16 = jnp.bfloat16
MESH = pl.DeviceIdType.MESH
ANY = pl.BlockSpec(memory_space=pl.ANY)
HBM = pl.BlockSpec(memory_space=pltpu.HBM)
SEM = pl.BlockSpec(memory_space=pltpu.SEMAPHORE)
EFFECT = pltpu.SideEffectType.DATAFLOW_SIDE_EFFECTING

N_DEV = 8
SEQ = 2048
D_MODEL = 1024
N_HEADS = 16
HEAD_DIM = 64
GRID_W = 64
NA_ROWS = 8
SEQ_ROWS = SEQ // GRID_W
DIL_GROUPS = ((128, 1), (512, 4), (2048, 16))
BAND = 128
RADIUS = 64
FF_SHARD = 352
FF_PAD = 384
RMS_EPS = 1e-6
NEG_INF = -1e30
Q_SCALE = HEAD_DIM ** -0.5

ADAM_LR = 0.001
ADAM_B1 = 0.9
ADAM_B2 = 0.999
ADAM_EPS = 1e-08
ADAM_WD = 0.01
ADAM_STEP = 10

VMEM_LIMIT = 56 * 1024 * 1024
TM = 512
TM_ROW = 1024
TM_MM = 1024

NN = (((1,), (0,)), ((), ()))
NT = (((1,), (1,)), ((), ()))
TN = (((0,), (0,)), ((), ()))


def _params():
    return pltpu.CompilerParams(vmem_limit_bytes=VMEM_LIMIT)


def _matmul(name, a, b, *, grid, a_spec, b_spec, o_spec, out_shape, dims, acc_shape, deps=(), inner=1):
    nk = grid[-1]
    kaxis = len(grid) - 1

    def body(a_ref, b_ref, *rest):
        o_ref, acc_ref = rest[-2], rest[-1]
        if inner == 1:
            part = lax.dot_general(a_ref[...].astype(BF16), b_ref[...].astype(BF16), dims, preferred_element_type=F32)
        elif len(b_ref.shape) == 2:
            a_all = jnp.concatenate([a_ref[j].astype(BF16) for j in range(inner)], axis=1)
            part = lax.dot_general(a_all, b_ref[...].astype(BF16), dims, preferred_element_type=F32)
        else:
            part = sum(lax.dot_general(a_ref[j].astype(BF16), b_ref[j].astype(BF16), dims, preferred_element_type=F32)
                       for j in range(inner))
        if nk == 1:
            o_ref[...] = part.astype(o_ref.dtype)
        else:
            k = pl.program_id(kaxis)

            @pl.when(k == 0)
            def _():
                acc_ref[...] = part

            @pl.when(k > 0)
            def _():
                acc_ref[...] += part

            @pl.when(k == nk - 1)
            def _():
                o_ref[...] = acc_ref[...].astype(o_ref.dtype)

    return pl.pallas_call(
        body, out_shape=out_shape, grid=grid, in_specs=[a_spec, b_spec] + [ANY] * len(deps), out_specs=o_spec,
        scratch_shapes=[pltpu.VMEM(acc_shape, F32)], name=name, compiler_params=_params())(a, b, *deps)


SORTED = tuple(d for _, d in DIL_GROUPS if d > 1)
LANE_CHUNKS = D_MODEL // 128


def _sort_scratch(tm=TM):
    return pltpu.VMEM((LANE_CHUNKS, tm, 128), F32)


def _sorted_view(t, dil):
    return t.reshape(dil, SEQ // dil, D_MODEL)


def _sorted_spec(dil, lead=(), tm=TM):
    return pl.BlockSpec((None,) * len(lead) + (dil, tm // dil, D_MODEL), lambda i: tuple(lead) + (0, i, 0))


def _sort_tile(scratch, value, dil, out_ref):
    tm = value.shape[0]
    for c in range(LANE_CHUNKS):
        scratch[c] = value[:, 128 * c:128 * (c + 1)]
    for r in range(dil):
        rows = [scratch.at[c][pl.ds(r, tm // dil, stride=dil), :] for c in range(LANE_CHUNKS)]
        out_ref[r] = jnp.concatenate(rows, axis=1).astype(out_ref.dtype)


def _unsort_tile(scratch, in_ref, dil):
    for r in range(dil):
        value = in_ref[r].astype(F32)
        for c in range(LANE_CHUNKS):
            scratch.at[c][pl.ds(r, value.shape[0], stride=dil), :] = value[:, 128 * c:128 * (c + 1)]
    return jnp.concatenate([scratch[c] for c in range(LANE_CHUNKS)], axis=1)


def _rms_fwd(name, x, g, res=None, out_dtype=F32, deps=(), sorted_too=False):
    tm = TM if sorted_too else TM_ROW
    n_tiles = SEQ // tm
    has_res = res is not None
    n_in = 2 + has_res + len(deps)

    def body(*refs):
        x_ref, g_ref = refs[0], refs[1]
        xv = x_ref[...]
        r = lax.rsqrt(jnp.mean(xv * xv, axis=-1, keepdims=True) + RMS_EPS)
        y = xv * r * g_ref[...]
        if has_res:
            y = refs[2][...] + y
        refs[n_in][...] = y.astype(out_dtype)
        if sorted_too:
            for j, dil in enumerate(SORTED):
                _sort_tile(refs[-1], y, dil, refs[n_in + 1 + j])

    tile = pl.BlockSpec((tm, D_MODEL), lambda i: (i, 0))
    gspec = pl.BlockSpec((1, D_MODEL), lambda i: (0, 0))
    ins = [x, g] + ([res] if has_res else []) + list(deps)
    specs = [tile, gspec] + ([tile] if has_res else []) + [ANY] * len(deps)
    shapes, out_specs = [jax.ShapeDtypeStruct((SEQ, D_MODEL), out_dtype)], [tile]
    if sorted_too:
        shapes += [jax.ShapeDtypeStruct((dil, SEQ // dil, D_MODEL), out_dtype) for dil in SORTED]
        out_specs += [_sorted_spec(dil) for dil in SORTED]
    result = pl.pallas_call(
        body, out_shape=tuple(shapes), grid=(n_tiles,), in_specs=specs, out_specs=tuple(out_specs),
        scratch_shapes=[_sort_scratch()] if sorted_too else [], name=name, compiler_params=_params())(*ins)
    return result if sorted_too else result[0]


def _rms_bwd(name, x, g, dys, res=None, out_dtype=F32, groups=None, deps=()):
    tm = TM if groups is not None else TM_ROW
    n_tiles = SEQ // tm
    n_dy = len(dys) if groups is None else 1 + len(SORTED)
    has_res = res is not None

    def body(*refs):
        x_ref, g_ref = refs[0], refs[1]
        dy_refs = refs[2:2 + n_dy]
        res_ref = refs[2 + n_dy] if has_res else None
        first_out = 2 + n_dy + has_res + len(deps)
        dx_ref, dg_ref, acc_ref = refs[first_out:first_out + 3]
        i = pl.program_id(0)
        xv = x_ref[...]
        r = lax.rsqrt(jnp.mean(xv * xv, axis=-1, keepdims=True) + RMS_EPS)
        xn = xv * r
        dy = dy_refs[0][...].astype(F32)
        for j, extra in enumerate(dy_refs[1:]):
            dy = dy + (extra[...].astype(F32) if groups is None else _unsort_tile(refs[-1], extra, SORTED[j]))
        dyg = dy * g_ref[...]
        dx = r * (dyg - xn * jnp.mean(dyg * xn, axis=-1, keepdims=True))
        if has_res:
            dx = res_ref[...] + dx
        dx_ref[...] = dx.astype(dx_ref.dtype)
        part = jnp.sum((dy * xn).reshape(tm // 8, 8, D_MODEL), axis=0)

        @pl.when(i == 0)
        def _():
            acc_ref[...] = part

        @pl.when(i > 0)
        def _():
            acc_ref[...] += part

        @pl.when(i == n_tiles - 1)
        def _():
            dg_ref[...] = jnp.broadcast_to(jnp.sum(acc_ref[...], axis=0, keepdims=True), (8, D_MODEL))

    tile = pl.BlockSpec((tm, D_MODEL), lambda i: (i, 0))
    gspec = pl.BlockSpec((1, D_MODEL), lambda i: (0, 0))
    if groups is None:
        dy_ins, dy_specs = list(dys), [tile] * n_dy
    else:
        dy_ins = [groups] + [groups.reshape(n_dy, dil, SEQ // dil, D_MODEL) for dil in SORTED]
        dy_specs = [pl.BlockSpec((None, tm, D_MODEL), lambda i: (0, i, 0))]
        dy_specs += [_sorted_spec(dil, lead=(1 + j,)) for j, dil in enumerate(SORTED)]
    ins = [x, g] + dy_ins + ([res] if has_res else []) + list(deps)
    specs = [tile, gspec] + dy_specs + ([tile] if has_res else []) + [ANY] * len(deps)
    dx, dg = pl.pallas_call(
        body, out_shape=(jax.ShapeDtypeStruct((SEQ, D_MODEL), out_dtype), jax.ShapeDtypeStruct((8, D_MODEL), F32)),
        grid=(n_tiles,), in_specs=specs,
        out_specs=(tile, pl.BlockSpec((8, D_MODEL), lambda i: (0, 0))),
        scratch_shapes=[pltpu.VMEM((8, D_MODEL), F32)] + ([_sort_scratch()] if groups is not None else []),
        name=name, compiler_params=_params())(*ins)
    return dx, dg[0:1]


def _rms(xv):
    return lax.rsqrt(jnp.mean(xv * xv, axis=-1, keepdims=True) + RMS_EPS)


def _post_norm_fwd(name, a, g_post, res, g_next, sorted_too=False):
    def body(a_ref, gp_ref, res_ref, gn_ref, x_ref, h_ref, *rest):
        av = a_ref[...]
        xv = res_ref[...] + av * _rms(av) * gp_ref[...]
        x_ref[...] = xv
        y = xv * _rms(xv) * gn_ref[...]
        h_ref[...] = y.astype(h_ref.dtype)
        if sorted_too:
            for j, dil in enumerate(SORTED):
                _sort_tile(rest[-1], y, dil, rest[j])

    tile = pl.BlockSpec((TM, D_MODEL), lambda i: (i, 0))
    gspec = pl.BlockSpec((1, D_MODEL), lambda i: (0, 0))
    shapes = [jax.ShapeDtypeStruct((SEQ, D_MODEL), F32), jax.ShapeDtypeStruct((SEQ, D_MODEL), BF16)]
    out_specs = [tile, tile]
    if sorted_too:
        shapes += [jax.ShapeDtypeStruct((dil, SEQ // dil, D_MODEL), BF16) for dil in SORTED]
        out_specs += [_sorted_spec(dil) for dil in SORTED]
    result = pl.pallas_call(
        body, out_shape=tuple(shapes), grid=(SEQ // TM,), in_specs=[tile, gspec, tile, gspec], out_specs=tuple(out_specs),
        scratch_shapes=[_sort_scratch()] if sorted_too else [], name=name, compiler_params=_params())(a, g_post, res, g_next)
    return result[0], list(result[1:])


def _norm_post_bwd(name, x, g_norm, dys, res, a, g_post, groups=None, deps=()):
    n_tiles = SEQ // TM
    n_dy = len(dys) if groups is None else 1 + len(SORTED)

    def body(*refs):
        x_ref, g_ref = refs[0], refs[1]
        dy_refs = refs[2:2 + n_dy]
        res_ref, a_ref, gp_ref = refs[2 + n_dy:5 + n_dy]
        dx_ref, da_ref, dg_ref, dgp_ref, acc_ref, accp_ref = refs[5 + n_dy + len(deps):11 + n_dy + len(deps)]
        i = pl.program_id(0)
        xv = x_ref[...]
        r = _rms(xv)
        xn = xv * r
        dy = dy_refs[0][...].astype(F32)
        for j, extra in enumerate(dy_refs[1:]):
            dy = dy + (extra[...].astype(F32) if groups is None else _unsort_tile(refs[-1], extra, SORTED[j]))
        dyg = dy * g_ref[...]
        dx = res_ref[...] + r * (dyg - xn * jnp.mean(dyg * xn, axis=-1, keepdims=True))
        dx_ref[...] = dx
        av = a_ref[...]
        ra = _rms(av)
        an = av * ra
        dxg = dx * gp_ref[...]
        da_ref[...] = (ra * (dxg - an * jnp.mean(dxg * an, axis=-1, keepdims=True))).astype(da_ref.dtype)
        part = jnp.sum((dy * xn).reshape(TM // 8, 8, D_MODEL), axis=0)
        part_p = jnp.sum((dx * an).reshape(TM // 8, 8, D_MODEL), axis=0)

        @pl.when(i == 0)
        def _():
            acc_ref[...] = part
            accp_ref[...] = part_p

        @pl.when(i > 0)
        def _():
            acc_ref[...] += part
            accp_ref[...] += part_p

        @pl.when(i == n_tiles - 1)
        def _():
            dg_ref[...] = jnp.broadcast_to(jnp.sum(acc_ref[...], axis=0, keepdims=True), (8, D_MODEL))
            dgp_ref[...] = jnp.broadcast_to(jnp.sum(accp_ref[...], axis=0, keepdims=True), (8, D_MODEL))

    tile = pl.BlockSpec((TM, D_MODEL), lambda i: (i, 0))
    gspec = pl.BlockSpec((1, D_MODEL), lambda i: (0, 0))
    gain = pl.BlockSpec((8, D_MODEL), lambda i: (0, 0))
    if groups is None:
        dy_ins, dy_specs = list(dys), [tile] * n_dy
    else:
        dy_ins = [groups] + [groups.reshape(n_dy, dil, SEQ // dil, D_MODEL) for dil in SORTED]
        dy_specs = [pl.BlockSpec((None, TM, D_MODEL), lambda i: (0, i, 0))]
        dy_specs += [_sorted_spec(dil, lead=(1 + j,)) for j, dil in enumerate(SORTED)]
    dx, da, dg, dgp = pl.pallas_call(
        body,
        out_shape=(jax.ShapeDtypeStruct((SEQ, D_MODEL), F32), jax.ShapeDtypeStruct((SEQ, D_MODEL), BF16),
                   jax.ShapeDtypeStruct((8, D_MODEL), F32), jax.ShapeDtypeStruct((8, D_MODEL), F32)),
        grid=(n_tiles,), in_specs=[tile, gspec] + dy_specs + [tile, tile, gspec] + [ANY] * len(deps),
        out_specs=(tile, tile, gain, gain),
        scratch_shapes=[pltpu.VMEM((8, D_MODEL), F32)] * 2 + ([_sort_scratch()] if groups is not None else []),
        name=name, compiler_params=_params())(x, g_norm, *dy_ins, res, a, g_post, *deps)
    return dx, da, dg[0:1], dgp[0:1]


def _loss_head(name, y, target):
    tm = TM_ROW
    n_tiles = SEQ // tm

    def body(y_ref, t_ref, dy_ref, loss_ref, acc_ref):
        i = pl.program_id(0)
        diff = y_ref[...] - t_ref[...]
        dy_ref[...] = diff * (1.0 / D_MODEL)
        part = jnp.sum((diff * diff).reshape(tm // 8, 8, D_MODEL), axis=0)

        @pl.when(i == 0)
        def _():
            acc_ref[...] = part

        @pl.when(i > 0)
        def _():
            acc_ref[...] += part

        @pl.when(i == n_tiles - 1)
        def _():
            loss_ref[...] = jnp.full((8, 128), jnp.sum(acc_ref[...]) * (0.5 / D_MODEL), F32)

    tile = pl.BlockSpec((tm, D_MODEL), lambda i: (i, 0))
    dy, loss = pl.pallas_call(
        body, out_shape=(jax.ShapeDtypeStruct((SEQ, D_MODEL), F32), jax.ShapeDtypeStruct((8, 128), F32)),
        grid=(n_tiles,), in_specs=[tile, tile], out_specs=(tile, pl.BlockSpec((8, 128), lambda i: (0, 0))),
        scratch_shapes=[pltpu.VMEM((8, D_MODEL), F32)], name=name, compiler_params=_params())(y, target)
    return dy, loss[0, 0]


def _row_index(shape):
    return lax.broadcasted_iota(jnp.int32, shape, 0)


def _lane_index(shape):
    return lax.broadcasted_iota(jnp.int32, shape, len(shape) - 1)


def _skew_rows(t, direction):
    q = _row_index(t.shape) & (GRID_W - 1)
    for bit in range(6):
        step = 1 << bit
        shift = step if direction > 0 else 128 - step
        t = jnp.where((q & step) != 0, pltpu.roll(t, shift, 1), t)
    return t


RPB_HEADS = 4


def _rpb_table(rpb_pad):
    rows = RPB_HEADS * 16 * GRID_W

    def body(r_ref, t_ref):
        lane = _lane_index((rows, 128))
        v = pltpu.roll(r_ref[...].reshape(RPB_HEADS * 16, 128), 128 - 15, 1)
        t = _skew_rows(jnp.broadcast_to(v[:, None, :], (RPB_HEADS * 16, GRID_W, 128)).reshape(rows, 128), +1)
        t = jnp.where(lane < GRID_W, t, 0.0)
        below = jnp.concatenate([t[GRID_W:], jnp.zeros((GRID_W, 128), F32)], axis=0)
        first_col = jnp.clip((_row_index((rows, 128)) & (GRID_W - 1)) - 8, 0, GRID_W - 16)
        key_col = lane & (GRID_W - 1)
        in_window = (key_col >= first_col) & (key_col < first_col + 16)
        t_ref[...] = jnp.where(in_window, t + pltpu.roll(below, GRID_W, 1), NEG_INF).reshape(RPB_HEADS, 16, GRID_W, 128)

    return pl.pallas_call(
        body, out_shape=jax.ShapeDtypeStruct((N_HEADS, 16, GRID_W, 128), F32), grid=(N_HEADS // RPB_HEADS,),
        in_specs=[pl.BlockSpec((RPB_HEADS, 16, 128), lambda h: (h, 0, 0))],
        out_specs=pl.BlockSpec((RPB_HEADS, 16, GRID_W, 128), lambda h: (h, 0, 0, 0)),
        name="rpb_table", compiler_params=_params())(rpb_pad)


def _rpb_grad(gp):
    rows = RPB_HEADS * 16 * GRID_W

    def body(g_ref, o_ref):
        lane = _lane_index((rows, 128))
        g = g_ref[...].reshape(rows, 128)
        low = jnp.where(lane < GRID_W, g, 0.0)
        high = pltpu.roll(jnp.where(lane >= GRID_W, g, 0.0), GRID_W, 1)
        above = jnp.concatenate([jnp.zeros((GRID_W, 128), F32), high[:rows - GRID_W]], axis=0)
        diag = jnp.sum(_skew_rows(low + above, -1).reshape(RPB_HEADS * 16, GRID_W, 128), axis=1)
        o_ref[...] = pltpu.roll(diag, 15, 1).reshape(RPB_HEADS, 16, 128)

    return pl.pallas_call(
        body, out_shape=jax.ShapeDtypeStruct((N_HEADS, 16, 128), F32), grid=(N_HEADS // RPB_HEADS,),
        in_specs=[pl.BlockSpec((RPB_HEADS, 16, GRID_W, 128), lambda h: (h, 0, 0, 0))],
        out_specs=pl.BlockSpec((RPB_HEADS, 16, 128), lambda h: (h, 0, 0)),
        name="rpb_grad", compiler_params=_params())(gp)


NA_KEYS = NA_ROWS * GRID_W


def _na_window(i):
    first_row = jnp.clip(i - NA_ROWS // 2, 0, SEQ_ROWS - NA_ROWS)
    return pl.multiple_of(first_row * GRID_W, GRID_W), first_row - i + NA_ROWS - 1


NA_STEP = 32


def _head_masks():
    lane = _lane_index((1, 128))
    return (lane < HEAD_DIM, lane >= HEAD_DIM)


def _stack_heads(t, masks):
    zero = jnp.zeros_like(t)
    return jnp.concatenate([jnp.where(masks[0], t, zero), jnp.where(masks[1], t, zero)], axis=0)


def _unstack_heads(t, masks):
    n = t.shape[0] // 2
    return jnp.where(masks[0], t[:n], t[n:])


def _stack_columns(t):
    return jnp.concatenate([t[:, 0:1], t[:, HEAD_DIM:HEAD_DIM + 1]], axis=0)


def _na_scores(qs, kw, tp_ref, dr0):
    s = lax.dot_general(qs, kw, NT, preferred_element_type=F32)
    bias = jnp.concatenate(
        [jnp.concatenate([tp_ref[a, pl.ds(dr0 + 2 * c, 1), :, :].reshape(GRID_W, 128) for c in range(4)], axis=1)
         for a in range(2)], axis=0)
    return s + bias


def _na_specs():
    q_spec = pl.BlockSpec((NA_STEP * GRID_W, 128), lambda hp, i: (i, hp))
    k_spec = pl.BlockSpec((SEQ, 128), lambda hp, i: (0, 8 + hp))
    v_spec = pl.BlockSpec((SEQ, 128), lambda hp, i: (0, 16 + hp))
    tp_spec = pl.BlockSpec((2, 16, GRID_W, 128), lambda hp, i: (hp, 0, 0, 0))
    return q_spec, k_spec, v_spec, tp_spec


def _na_fwd(qkv, table):
    def body(q_ref, k_ref, v_ref, tp_ref, o_ref, lse_ref):
        masks = _head_masks()
        for r in range(NA_STEP):
            rows = slice(r * GRID_W, (r + 1) * GRID_W)
            start, dr0 = _na_window(pl.program_id(1) * NA_STEP + r)
            kw = k_ref[pl.ds(start, NA_KEYS), :]
            vw = v_ref[pl.ds(start, NA_KEYS), :]
            s = _na_scores(_stack_heads(q_ref[rows, :] * Q_SCALE, masks), kw, tp_ref, dr0)
            m = jnp.max(s, axis=-1, keepdims=True)
            p = jnp.exp(s - m)
            denom = jnp.sum(p, axis=-1, keepdims=True)
            out = jnp.dot(p.astype(BF16), vw, preferred_element_type=F32) / denom
            o_ref[rows, :] = _unstack_heads(out, masks).astype(o_ref.dtype)
            lse_ref[rows, :] = _unstack_heads(jnp.broadcast_to(m + jnp.log(denom), (2 * GRID_W, 128)), masks)

    q_spec, k_spec, v_spec, tp_spec = _na_specs()
    return pl.pallas_call(
        body, out_shape=(jax.ShapeDtypeStruct((SEQ, D_MODEL), BF16), jax.ShapeDtypeStruct((SEQ, D_MODEL), F32)),
        grid=(N_HEADS // 2, SEQ_ROWS // NA_STEP), in_specs=[q_spec, k_spec, v_spec, tp_spec],
        out_specs=(q_spec, q_spec), name="na_fwd", compiler_params=_params())(qkv, qkv, qkv, table)


def _na_bwd(qkv, table, d_out, lse):
    def body(q_ref, k_ref, v_ref, tp_ref, do_ref, lse_ref, dqkv_ref, gp_ref, dk_acc, dv_acc):
        step = pl.program_id(1)

        @pl.when(step == 0)
        def _():
            dk_acc[...] = jnp.zeros_like(dk_acc)
            dv_acc[...] = jnp.zeros_like(dv_acc)
            gp_ref[...] = jnp.zeros_like(gp_ref)

        masks = _head_masks()
        for r in range(NA_STEP):
            rows = slice(r * GRID_W, (r + 1) * GRID_W)
            i = step * NA_STEP + r
            start, dr0 = _na_window(i)
            kw = k_ref[pl.ds(start, NA_KEYS), :]
            vw = v_ref[pl.ds(start, NA_KEYS), :]
            qs = _stack_heads(q_ref[rows, :] * Q_SCALE, masks)
            dos = _stack_heads(do_ref[rows, :], masks)
            p = jnp.exp(_na_scores(qs, kw, tp_ref, dr0) - _stack_columns(lse_ref[rows, :]))
            dp = lax.dot_general(dos, vw, NT, preferred_element_type=F32)
            ds = p * (dp - jnp.sum(p * dp, axis=-1, keepdims=True))
            for a in range(2):
                for c in range(4):
                    gp_ref[a, pl.ds(dr0 + 2 * c, 1), :, :] += (
                        ds[a * GRID_W:(a + 1) * GRID_W, 128 * c:128 * (c + 1)].reshape(1, GRID_W, 128))
            dsb = ds.astype(BF16)
            dq = _unstack_heads(jnp.dot(dsb, kw, preferred_element_type=F32), masks) * Q_SCALE
            dqkv_ref[0, pl.ds(pl.multiple_of(i * GRID_W, GRID_W), GRID_W), :] = dq.astype(dqkv_ref.dtype)
            dk_acc[pl.ds(start, NA_KEYS), :] += lax.dot_general(dsb, qs, TN, preferred_element_type=F32)
            dv_acc[pl.ds(start, NA_KEYS), :] += lax.dot_general(p.astype(BF16), dos, TN, preferred_element_type=F32)

        @pl.when(step == SEQ_ROWS // NA_STEP - 1)
        def _():
            dqkv_ref[1] = dk_acc[...].astype(dqkv_ref.dtype)
            dqkv_ref[2] = dv_acc[...].astype(dqkv_ref.dtype)

    q_spec, k_spec, v_spec, tp_spec = _na_specs()
    return pl.pallas_call(
        body,
        out_shape=(jax.ShapeDtypeStruct((3, SEQ, D_MODEL), BF16), jax.ShapeDtypeStruct((N_HEADS, 16, GRID_W, 128), F32)),
        grid=(N_HEADS // 2, SEQ_ROWS // NA_STEP), in_specs=[q_spec, k_spec, v_spec, tp_spec, q_spec, q_spec],
        out_specs=(pl.BlockSpec((3, SEQ, 128), lambda hp, i: (0, 0, hp)), tp_spec),
        scratch_shapes=[pltpu.VMEM((SEQ, 128), F32), pltpu.VMEM((SEQ, 128), F32)],
        name="na_bwd", compiler_params=_params())(qkv, qkv, qkv, table, d_out, lse)


DIL_STEP = 16


def _dil_geometry(group):
    dil = DIL_GROUPS[group][1]
    sub_len = SEQ // dil
    blocks = sub_len // BAND
    return dil, sub_len, max(blocks // DIL_STEP, 1), max(DIL_STEP // blocks, 1), min(2 * BAND, sub_len)


def _dil_block(step, r, sub_len, subs):
    per_sub = DIL_STEP // subs
    return (r // per_sub) * sub_len, step * per_sub + r % per_sub


def _dil_window(b, sub_len, n_keys):
    if n_keys == sub_len:
        return 0
    return pl.multiple_of(jnp.clip(b * BAND - RADIUS, 0, sub_len - n_keys), RADIUS)


def _dil_bias(b, start, n_keys, slope_ref, hp):
    row = _row_index((2 * BAND, n_keys))
    qpos = b * BAND + (row & (BAND - 1))
    kpos = start + _lane_index((2 * BAND, n_keys))
    dist = jnp.abs(qpos - kpos)
    slope = jnp.where(row < BAND, slope_ref[2 * hp], slope_ref[2 * hp + 1])
    return slope * dist.astype(F32), dist <= RADIUS


def _dil_scores(qs, kw, penalty, valid):
    return jnp.where(valid, lax.dot_general(qs, kw, NT, preferred_element_type=F32) - penalty, NEG_INF)


def _dil_specs(group):
    dil, sub_len, steps, subs, _ = _dil_geometry(group)
    col = group * 24
    rows = DIL_STEP * BAND
    q_spec = pl.BlockSpec((rows, 128), lambda n, hp, b: (n * steps + b, col + hp))
    k_spec = pl.BlockSpec((subs * sub_len, 128), lambda n, hp, b: (n, col + 8 + hp))
    v_spec = pl.BlockSpec((subs * sub_len, 128), lambda n, hp, b: (n, col + 16 + hp))
    tile = pl.BlockSpec((rows, 128), lambda n, hp, b: (n * steps + b, hp))
    smem = pl.BlockSpec(memory_space=pltpu.SMEM)
    return (dil // subs, N_HEADS // 2, steps), q_spec, k_spec, v_spec, tile, smem


def _dil_fwd(group, qkv, slopes):
    _, sub_len, _, subs, n_keys = _dil_geometry(group)

    def body(q_ref, k_ref, v_ref, slope_ref, o_ref, lse_ref):
        hp = pl.program_id(1)
        masks = _head_masks()
        for r in range(DIL_STEP):
            rows = slice(r * BAND, (r + 1) * BAND)
            base, b = _dil_block(pl.program_id(2), r, sub_len, subs)
            start = _dil_window(b, sub_len, n_keys)
            kw = k_ref[pl.ds(base + start, n_keys), :]
            vw = v_ref[pl.ds(base + start, n_keys), :]
            penalty, valid = _dil_bias(b, start, n_keys, slope_ref, hp)
            s = _dil_scores(_stack_heads(q_ref[rows, :] * Q_SCALE, masks), kw, penalty, valid)
            m = jnp.max(s, axis=-1, keepdims=True)
            p = jnp.exp(s - m)
            denom = jnp.sum(p, axis=-1, keepdims=True)
            out = jnp.dot(p.astype(BF16), vw, preferred_element_type=F32) / denom
            o_ref[rows, :] = _unstack_heads(out, masks).astype(o_ref.dtype)
            lse_ref[rows, :] = _unstack_heads(jnp.broadcast_to(m + jnp.log(denom), (2 * BAND, 128)), masks)

    grid, q_spec, k_spec, v_spec, tile, smem = _dil_specs(group)
    return pl.pallas_call(
        body, out_shape=(jax.ShapeDtypeStruct((SEQ, D_MODEL), BF16), jax.ShapeDtypeStruct((SEQ, D_MODEL), F32)),
        grid=grid, in_specs=[q_spec, k_spec, v_spec, smem], out_specs=(tile, tile),
        name=f"dil_fwd_{group}", compiler_params=_params())(qkv, qkv, qkv, slopes)


def _dil_merge(outs, lses):
    n_sorted = len(SORTED)

    def body(*refs):
        o_refs, l_refs = refs[:3], refs[3:6]
        out_refs, lse_refs, scratch = refs[6:7 + n_sorted], refs[7 + n_sorted:8 + 2 * n_sorted], refs[-1]
        os_ = [o_refs[0][...]] + [_unsort_tile(scratch, o_refs[1 + j], dil) for j, dil in enumerate(SORTED)]
        ls = [l_refs[0][...]] + [_unsort_tile(scratch, l_refs[1 + j], dil) for j, dil in enumerate(SORTED)]
        m = jnp.maximum(jnp.maximum(ls[0], ls[1]), ls[2])
        es = [jnp.exp(v - m) for v in ls]
        total = es[0] + es[1] + es[2]
        merged = (es[0] * os_[0] + es[1] * os_[1] + es[2] * os_[2]) / total
        lse = m + jnp.log(total)
        out_refs[0][...] = merged
        lse_refs[0][...] = lse
        for j, dil in enumerate(SORTED):
            _sort_tile(scratch, merged, dil, out_refs[1 + j])
            _sort_tile(scratch, lse, dil, lse_refs[1 + j])

    tm = 256
    tile = pl.BlockSpec((tm, D_MODEL), lambda i: (i, 0))
    specs = [tile] + [_sorted_spec(dil, tm=tm) for dil in SORTED]
    shapes = [jax.ShapeDtypeStruct((SEQ, D_MODEL), F32)] + [jax.ShapeDtypeStruct((dil, SEQ // dil, D_MODEL), F32) for dil in SORTED]
    views = lambda ts: [ts[0]] + [_sorted_view(t, dil) for t, dil in zip(ts[1:], SORTED)]
    result = pl.pallas_call(
        body, out_shape=tuple(shapes * 2), grid=(SEQ // tm,), in_specs=specs * 2, out_specs=tuple(specs * 2),
        scratch_shapes=[_sort_scratch(tm)], name="dil_merge", compiler_params=_params())(*views(outs), *views(lses))
    flat = [t.reshape(SEQ, D_MODEL) for t in result]
    return flat[:1 + n_sorted], flat[1 + n_sorted:]


def _dil_bwd(group, qkv, slopes, d_out, out, lse_group, lse_total, into):
    _, sub_len, steps, subs, n_keys = _dil_geometry(group)

    def body(q_ref, k_ref, v_ref, slope_ref, do_ref, o_ref, lg_ref, lt_ref, into_ref, dqkv_ref, dk_acc, dv_acc):
        hp, step = pl.program_id(1), pl.program_id(2)

        @pl.when(step == 0)
        def _():
            dk_acc[...] = jnp.zeros_like(dk_acc)
            dv_acc[...] = jnp.zeros_like(dv_acc)

        masks = _head_masks()
        for r in range(DIL_STEP):
            rows = slice(r * BAND, (r + 1) * BAND)
            base, b = _dil_block(step, r, sub_len, subs)
            start = _dil_window(b, sub_len, n_keys)
            keys = pl.ds(base + start, n_keys)
            kw = k_ref[keys, :]
            vw = v_ref[keys, :]
            penalty, valid = _dil_bias(b, start, n_keys, slope_ref, hp)
            qs = _stack_heads(q_ref[rows, :] * Q_SCALE, masks)
            lse2 = lg_ref[rows, :]
            weight = jnp.exp(lse2 - lt_ref[rows, :])
            do2 = do_ref[rows, :]
            dogs = _stack_heads((weight * do2).astype(BF16), masks)
            delta = _stack_columns(weight) * jnp.sum(_stack_heads(do2 * o_ref[rows, :], masks), axis=-1, keepdims=True)
            p = jnp.exp(_dil_scores(qs, kw, penalty, valid) - _stack_columns(lse2))
            dp = lax.dot_general(dogs, vw, NT, preferred_element_type=F32)
            dsb = (p * (dp - delta)).astype(BF16)
            dq = _unstack_heads(jnp.dot(dsb, kw, preferred_element_type=F32), masks) * Q_SCALE
            dqkv_ref[0, pl.ds(pl.multiple_of(base + b * BAND, BAND), BAND), :] = dq.astype(dqkv_ref.dtype)
            dk_acc[keys, :] += lax.dot_general(dsb, qs, TN, preferred_element_type=F32)
            dv_acc[keys, :] += lax.dot_general(p.astype(BF16), dogs, TN, preferred_element_type=F32)

        @pl.when(step == steps - 1)
        def _():
            dqkv_ref[1] = dk_acc[...].astype(dqkv_ref.dtype)
            dqkv_ref[2] = dv_acc[...].astype(dqkv_ref.dtype)

    grid, q_spec, k_spec, v_spec, tile, smem = _dil_specs(group)
    return pl.pallas_call(
        body, out_shape=jax.ShapeDtypeStruct(into.shape, into.dtype), grid=grid,
        in_specs=[q_spec, k_spec, v_spec, smem, tile, tile, tile, tile, ANY],
        out_specs=pl.BlockSpec((3, subs * sub_len, 128), lambda n, hp, b: (group, n, hp)),
        scratch_shapes=[pltpu.VMEM((subs * sub_len, 128), F32), pltpu.VMEM((subs * sub_len, 128), F32)],
        input_output_aliases={8: 0}, name=f"dil_bwd_{group}", compiler_params=_params(),
    )(qkv, qkv, qkv, slopes, d_out, out, lse_group, lse_total, into)


def _accumulate_rows(acc_ref, i, first, part):
    rows = pl.ds(pl.multiple_of(i * TM_MM, TM_MM), TM_MM)

    @pl.when(first)
    def _():
        acc_ref[rows, :] = part

    @pl.when(jnp.logical_not(first))
    def _():
        acc_ref[rows, :] += part


def _ffn_specs():
    tile = pl.BlockSpec((TM_MM, D_MODEL), lambda d, i: (i, 0))
    gate = pl.BlockSpec((None, None, D_MODEL, FF_PAD), lambda d, i: (d, 0, 0, 0))
    up = pl.BlockSpec((None, None, D_MODEL, FF_PAD), lambda d, i: (d, 1, 0, 0))
    down = pl.BlockSpec((None, FF_PAD, D_MODEL), lambda d, i: (d, 0, 0))
    hidden = pl.BlockSpec((None, TM_MM, FF_PAD), lambda d, i: (d, i, 0))
    whole = pl.BlockSpec((SEQ, D_MODEL), lambda d, i: (0, 0))
    return tile, gate, up, down, hidden, whole


def _ffn_fwd(name, h, w_gu, w_down):
    def body(h_ref, wg_ref, wu_ref, wd_ref, f_ref, hg_ref, hu_ref, act_t_ref):
        hv = h_ref[...]
        hg = jnp.dot(hv, wg_ref[...], preferred_element_type=F32)
        hu = jnp.dot(hv, wu_ref[...], preferred_element_type=F32)
        act = hg * jax.nn.sigmoid(hg) * hu
        act_t_ref[...] = act.T.astype(act_t_ref.dtype)
        hg_ref[...] = hg.astype(hg_ref.dtype)
        hu_ref[...] = hu.astype(hu_ref.dtype)
        _accumulate_rows(f_ref, pl.program_id(1), pl.program_id(0) == 0,
                         jnp.dot(act.astype(BF16), wd_ref[...], preferred_element_type=F32))

    tile, gate, up, down, hidden, whole = _ffn_specs()
    shape = jax.ShapeDtypeStruct((N_DEV, SEQ, FF_PAD), BF16)
    return pl.pallas_call(
        body, out_shape=(jax.ShapeDtypeStruct((SEQ, D_MODEL), F32), shape, shape, jax.ShapeDtypeStruct((N_DEV, FF_PAD, SEQ), BF16)),
        grid=(N_DEV, SEQ // TM_MM), in_specs=[tile, gate, up, down],
        out_specs=(whole, hidden, hidden, pl.BlockSpec((None, FF_PAD, TM_MM), lambda d, i: (d, 0, i))),
        name=name, compiler_params=_params())(h, w_gu, w_gu, w_down)


def _ffn_dgu(name, h_t, dgu):
    def body(h_ref, dgu_ref, o_ref):
        both = jnp.dot(h_ref[...], jnp.concatenate([dgu_ref[0], dgu_ref[1]], axis=1), preferred_element_type=F32)
        o_ref[0] = both[:, :FF_PAD].astype(o_ref.dtype)
        o_ref[1] = both[:, FF_PAD:].astype(o_ref.dtype)

    return pl.pallas_call(
        body, out_shape=jax.ShapeDtypeStruct((N_DEV, 2, D_MODEL, FF_PAD), BF16), grid=(N_DEV,),
        in_specs=[pl.BlockSpec((D_MODEL, SEQ), lambda d: (0, 0)), pl.BlockSpec((2, None, SEQ, FF_PAD), lambda d: (0, d, 0, 0))],
        out_specs=pl.BlockSpec((None, 2, D_MODEL, FF_PAD), lambda d: (d, 0, 0, 0)),
        name=name, compiler_params=_params())(h_t, dgu)


def _ffn_bwd(name, d_f, w_gu, w_down, hg, hu, act_t):
    n_tiles = SEQ // TM_MM

    def body(df_ref, wg_ref, wu_ref, wd_ref, hg_ref, hu_ref, act_t_ref, dgu_ref, dh_ref, dwd_ref, dwd_acc):
        i = pl.program_id(1)
        d_down = jnp.dot(act_t_ref[...], df_ref[...], preferred_element_type=F32)

        @pl.when(i == 0)
        def _():
            dwd_acc[...] = d_down

        @pl.when(i > 0)
        def _():
            dwd_acc[...] += d_down

        @pl.when(i == n_tiles - 1)
        def _():
            dwd_ref[...] = dwd_acc[...].astype(dwd_ref.dtype)

        dact = lax.dot_general(df_ref[...], wd_ref[...], NT, preferred_element_type=F32)
        hgv = hg_ref[...].astype(F32)
        sig = jax.nn.sigmoid(hgv)
        d_gate = (dact * hu_ref[...].astype(F32) * (sig * (1.0 + hgv * (1.0 - sig)))).astype(BF16)
        d_up = (dact * hgv * sig).astype(BF16)
        dgu_ref[0] = d_gate
        dgu_ref[1] = d_up
        part = lax.dot_general(jnp.concatenate([d_gate, d_up], axis=1), jnp.concatenate([wg_ref[...], wu_ref[...]], axis=1),
                               NT, preferred_element_type=F32)
        _accumulate_rows(dh_ref, pl.program_id(1), pl.program_id(0) == 0, part)

    tile, gate, up, down, hidden, whole = _ffn_specs()
    return pl.pallas_call(
        body, out_shape=(jax.ShapeDtypeStruct((2, N_DEV, SEQ, FF_PAD), BF16), jax.ShapeDtypeStruct((SEQ, D_MODEL), F32),
                         jax.ShapeDtypeStruct((N_DEV, FF_PAD, D_MODEL), BF16)),
        grid=(N_DEV, n_tiles),
        in_specs=[tile, gate, up, down, hidden, hidden, pl.BlockSpec((None, FF_PAD, TM_MM), lambda d, i: (d, 0, i))],
        out_specs=(pl.BlockSpec((2, None, TM_MM, FF_PAD), lambda d, i: (0, d, i, 0)), whole, down),
        scratch_shapes=[pltpu.VMEM((FF_PAD, D_MODEL), F32)],
        name=name, compiler_params=_params())(d_f, w_gu, w_gu, w_down, hg, hu, act_t)


def _position():
    return lax.axis_index("x"), lax.axis_index("y"), lax.axis_index("c")


def _flat(p):
    return 4 * p[0] + 2 * p[1] + p[2]


def _peer(me, k):
    x, y, c = me
    return (1 - x if k & 4 else x, 1 - y if k & 2 else y, 1 - c if k & 1 else c)


def _columns(width):
    return lambda ref, d: ref.at[:, pl.ds(pl.multiple_of(d * width, 128), width)]


def _leading(ref, d):
    return ref.at[d]


def _whole(ref, d):
    return ref


def _by_sender(window):
    return lambda ref, sender, k: window(ref, sender)


def _by_distance(ref, sender, k):
    return ref.at[k - 1]


def _prep_weights(me, na_qkv, na_o, dil_qkv, dil_o, gate, up, down, land_shapes):
    na_cols, dil_cols = na_qkv.shape[-1], dil_qkv.shape[-1]
    o_rows = na_o.shape[1]
    tiles = 4
    rows, rows_o = D_MODEL // tiles, o_rows // tiles

    def body(me_ref, naq, nao, dq, do_, g0, u0, d0, g1, u1, d1, *outs):
        def put(t, index, value):
            outs[t][index] = value
            outs[8 + t][index] = value

        put(0, ..., naq[...].astype(BF16))
        put(1, ..., nao[...].astype(BF16))
        put(4, ..., dq[...].astype(BF16))
        put(5, ..., do_[...].astype(BF16))
        for t, (g, u, d) in ((2, (g0, u0, d0)), (6, (g1, u1, d1))):
            for j, part in enumerate((g, u)):
                put(t, (j, slice(None), slice(0, FF_SHARD)), part[...].astype(BF16))
                put(t, (j, slice(None), slice(FF_SHARD, FF_PAD)), jnp.zeros((rows, FF_PAD - FF_SHARD), BF16))
            put(t + 1, (slice(0, FF_SHARD), slice(None)), d[...].astype(BF16))
            put(t + 1, (slice(FF_SHARD, FF_PAD), slice(None)), jnp.zeros((FF_PAD - FF_SHARD, D_MODEL), BF16))

    def tiled(width):
        return pl.BlockSpec((None, rows, width), lambda i, me: (0, i, 0))

    def layer(l, width):
        return pl.BlockSpec((None, rows, width), lambda i, me: (l, i, 0))

    def whole_layer(l):
        return pl.BlockSpec((None, FF_SHARD, D_MODEL), lambda i, me: (l, 0, 0))

    in_specs = [tiled(na_cols), pl.BlockSpec((None, rows_o, D_MODEL), lambda i, me: (0, i, 0)), tiled(dil_cols),
                pl.BlockSpec((None, rows_o, D_MODEL), lambda i, me: (0, i, 0)),
                layer(0, FF_SHARD), layer(0, FF_SHARD), whole_layer(0), layer(1, FF_SHARD), layer(1, FF_SHARD), whole_layer(1)]
    o_shard = pl.BlockSpec((rows_o, D_MODEL), lambda i, me: (i, 0))
    o_land = pl.BlockSpec((None, rows_o, D_MODEL), lambda i, me: (me[0], i, 0))
    gu_shard = pl.BlockSpec((2, rows, FF_PAD), lambda i, me: (0, i, 0))
    gu_land = pl.BlockSpec((None, 2, rows, FF_PAD), lambda i, me: (me[0], 0, i, 0))
    down_shard = pl.BlockSpec((FF_PAD, D_MODEL), lambda i, me: (0, 0))
    down_land = pl.BlockSpec((None, FF_PAD, D_MODEL), lambda i, me: (me[0], 0, 0))

    def qkv_shard(width):
        return pl.BlockSpec((rows, width), lambda i, me: (i, 0))

    def qkv_land(width):
        return pl.BlockSpec((rows, width), lambda i, me: (i, me[0]))

    shard_specs = [qkv_shard(na_cols), o_shard, gu_shard, down_shard, qkv_shard(dil_cols), o_shard, gu_shard, down_shard]
    land_specs = [qkv_land(na_cols), o_land, gu_land, down_land, qkv_land(dil_cols), o_land, gu_land, down_land]
    shard_shapes = [jax.ShapeDtypeStruct(s, BF16) for s in
                    ((D_MODEL, na_cols), (o_rows, D_MODEL), (2, D_MODEL, FF_PAD), (FF_PAD, D_MODEL),
                     (D_MODEL, dil_cols), (o_rows, D_MODEL), (2, D_MODEL, FF_PAD), (FF_PAD, D_MODEL))]
    result = pl.pallas_call(
        body, out_shape=tuple(shard_shapes + list(land_shapes)),
        grid_spec=pltpu.PrefetchScalarGridSpec(num_scalar_prefetch=1, grid=(tiles,), in_specs=in_specs,
                                               out_specs=tuple(shard_specs + land_specs)),
        name="prep_weights", compiler_params=_params())(me, na_qkv, na_o, dil_qkv, dil_o, gate, up, down, gate, up, down)
    return list(result[:8]), list(result[8:])


def _remote_copies(sets, src_refs, land_refs, send_sems, recv_sems, outgoing):
    me = _position()
    copies = []
    for t, (si, src_of, li, dst_of) in enumerate(sets):
        for k in range(1, N_DEV):
            other = _peer(me, k)
            sender = me if outgoing else other
            copies.append(pltpu.make_async_remote_copy(
                src_ref=src_of(src_refs[si], _flat(other)), dst_ref=dst_of(land_refs[li], _flat(sender), k),
                send_sem=send_sems.at[(N_DEV - 1) * t + k - 1], recv_sem=recv_sems.at[(N_DEV - 1) * t + k - 1],
                device_id=other, device_id_type=MESH))
    return copies


def _send_start(name, srcs, lands, sets_by_group):
    n_src, n_land, n_groups = len(srcs), len(lands), len(sets_by_group)

    def body(*refs):
        src_refs, land_refs = refs[:n_src], refs[n_src:n_src + n_land]
        outs = refs[n_src + n_land:]
        for g, sets in enumerate(sets_by_group):
            for cp in _remote_copies(sets, src_refs, land_refs, outs[2 * g], outs[2 * g + 1], True):
                cp.start()
        outs[-1][...] = jnp.zeros_like(outs[-1])

    sem_shapes = []
    for sets in sets_by_group:
        sem_shapes += [pltpu.SemaphoreType.DMA((len(sets) * (N_DEV - 1),))] * 2
    thru = [pltpu.HBM(a.shape, a.dtype) for a in list(srcs) + list(lands)]
    n_sem = len(sem_shapes)
    result = pl.pallas_call(
        body, out_shape=tuple(sem_shapes + thru + [jax.ShapeDtypeStruct((8, 128), F32)]),
        in_specs=[HBM] * (n_src + n_land),
        out_specs=tuple([SEM] * n_sem + [HBM] * (n_src + n_land) + [pl.BlockSpec(memory_space=pltpu.VMEM)]),
        input_output_aliases={i: n_sem + i for i in range(n_src + n_land)},
        compiler_params=pltpu.CompilerParams(has_side_effects=EFFECT), name=name,
    )(*[pltpu.with_memory_space_constraint(a, pltpu.HBM) for a in list(srcs) + list(lands)])
    sems = [(result[2 * g], result[2 * g + 1]) for g in range(n_groups)]
    return sems, list(result[n_sem:n_sem + n_src]), list(result[n_sem + n_src:n_sem + n_src + n_land]), result[-1]


def _send_wait(name, sems, srcs, lands, sets, after):
    n_src, n_land = len(srcs), len(lands)

    def body(*refs):
        src_refs, land_refs = refs[:n_src], refs[n_src:n_src + n_land]
        send_sems, recv_sems = refs[n_src + n_land], refs[n_src + n_land + 1]
        for cp in _remote_copies(sets, src_refs, land_refs, send_sems, recv_sems, True):
            cp.wait_send()
        for cp in _remote_copies(sets, src_refs, land_refs, send_sems, recv_sems, False):
            cp.wait_recv()

    thru = [pltpu.HBM(a.shape, a.dtype) for a in list(srcs) + list(lands)]
    result = pl.pallas_call(
        body, out_shape=tuple(thru), in_specs=[HBM] * (n_src + n_land) + [SEM, SEM] + [ANY] * len(after),
        out_specs=tuple([HBM] * (n_src + n_land)), input_output_aliases={i: i for i in range(n_src + n_land)},
        compiler_params=pltpu.CompilerParams(has_side_effects=EFFECT), name=name,
    )(*srcs, *lands, sems[0], sems[1], *after)
    return list(result[:n_src]), list(result[n_src:])


DIRECT = (1, 2, 4, 6)
PASSED = DIRECT[1:]


def _hbm_passthrough(body, name, arrays, n_sem_in, sem_out_shapes, extra):
    n, n_out = len(arrays), len(sem_out_shapes)
    return pl.pallas_call(
        body, out_shape=tuple(list(sem_out_shapes) + [pltpu.HBM(a.shape, a.dtype) for a in arrays]),
        in_specs=[HBM] * n + [SEM] * n_sem_in + [ANY] * len(extra), out_specs=tuple([SEM] * n_out + [HBM] * n),
        input_output_aliases={i: n_out + i for i in range(n)},
        compiler_params=pltpu.CompilerParams(has_side_effects=EFFECT), name=name)


def _shard_copy(src_ref, land_ref, window, block, to, send_sem, recv_sem, from_shard):
    dst = window(land_ref, _flat(block))
    return pltpu.make_async_remote_copy(src_ref=src_ref if from_shard else dst, dst_ref=dst, send_sem=send_sem,
                                        recv_sem=recv_sem, device_id=to, device_id_type=MESH)


def _gather_start(name, shards, lands, windows, group_sizes):
    n = len(shards)

    def body(*refs):
        shard_refs, land_refs, outs = refs[:n], refs[n:2 * n], refs[2 * n:]
        me = _position()
        t = 0
        for g, size in enumerate(group_sizes):
            for local in range(size):
                for j, k in enumerate(DIRECT):
                    i = len(DIRECT) * local + j
                    _shard_copy(shard_refs[t], land_refs[t], windows[t], me, _peer(me, k), outs[2 * g].at[i],
                                outs[2 * g + 1].at[i], True).start()
                t += 1

    sem_shapes = [pltpu.SemaphoreType.DMA((len(DIRECT) * size,)) for size in group_sizes for _ in range(2)]
    arrays = [pltpu.with_memory_space_constraint(a, pltpu.HBM) for a in list(shards) + list(lands)]
    result = _hbm_passthrough(body, name, arrays, 0, sem_shapes, ())(*arrays)
    n_sem = len(sem_shapes)
    sems = [(result[2 * g], result[2 * g + 1]) for g in range(len(group_sizes))]
    return sems, list(result[n_sem:n_sem + n]), list(result[n_sem + n:])


def _gather_pass_on(name, sems, shards, lands, windows, after):
    n = len(shards)

    def body(*refs):
        shard_refs, land_refs = refs[:n], refs[n:2 * n]
        recv_sems = refs[2 * n + 1]
        pass_send, pass_recv = refs[2 * n + 2 + len(after)], refs[2 * n + 3 + len(after)]
        me = _position()
        sibling = _peer(me, 1)
        for t in range(n):
            for j, k in enumerate(PASSED):
                sender = _peer(me, k)
                arrived = len(DIRECT) * t + 1 + j
                _shard_copy(shard_refs[t], land_refs[t], windows[t], sender, me, refs[2 * n].at[arrived], recv_sems.at[arrived],
                            True).wait_recv()
                i = len(PASSED) * t + j
                _shard_copy(shard_refs[t], land_refs[t], windows[t], sender, sibling, pass_send.at[i], pass_recv.at[i],
                            False).start()

    sem_shapes = [pltpu.SemaphoreType.DMA((len(PASSED) * n,))] * 2
    result = _hbm_passthrough(body, name, list(shards) + list(lands), 2, sem_shapes, after)(
        *shards, *lands, sems[0], sems[1], *after)
    return (result[0], result[1]), list(result[2:2 + n]), list(result[2 + n:])


def _gather_wait(name, sems, pass_sems, shards, lands, windows, after):
    n = len(shards)

    def body(*refs):
        shard_refs, land_refs = refs[:n], refs[n:2 * n]
        send_sems, recv_sems, pass_send, pass_recv = refs[2 * n:2 * n + 4]
        me = _position()
        sibling = _peer(me, 1)
        for t in range(n):
            for j, k in enumerate(DIRECT):
                i = len(DIRECT) * t + j
                _shard_copy(shard_refs[t], land_refs[t], windows[t], me, _peer(me, k), send_sems.at[i], recv_sems.at[i],
                            True).wait_send()
            _shard_copy(shard_refs[t], land_refs[t], windows[t], sibling, me, send_sems.at[len(DIRECT) * t],
                        recv_sems.at[len(DIRECT) * t], True).wait_recv()
            for j, k in enumerate(PASSED):
                i = len(PASSED) * t + j
                _shard_copy(shard_refs[t], land_refs[t], windows[t], _peer(me, k), sibling, pass_send.at[i], pass_recv.at[i],
                            False).wait_send()
                _shard_copy(shard_refs[t], land_refs[t], windows[t], _peer(sibling, k), me, pass_send.at[i], pass_recv.at[i],
                            False).wait_recv()

    result = _hbm_passthrough(body, name, list(shards) + list(lands), 4, [], after)(
        *shards, *lands, sems[0], sems[1], pass_sems[0], pass_sems[1], *after)
    return list(result[n:])


def _all_gather(name, locals_, out_shapes, windows, deps=()):
    n = len(locals_)

    def body(*refs):
        src_refs, out_refs = refs[:n], refs[n + len(deps):2 * n + len(deps)]
        send_sems, recv_sems, local_sems = refs[2 * n + len(deps):]
        x, y, c = _position()
        me, sibling = (x, y, c), (x, y, 1 - c)
        chips = [(1 - x, y), (x, 1 - y), (1 - x, 1 - y)]

        def copy(t, k, block, to, from_local=False):
            dst = windows[t](out_refs[t], _flat(block))
            return pltpu.make_async_remote_copy(
                src_ref=src_refs[t] if from_local else dst, dst_ref=dst, send_sem=send_sems.at[t, k],
                recv_sem=recv_sems.at[t, k], device_id=to, device_id_type=MESH)

        mine = [pltpu.make_async_copy(src_refs[t], windows[t](out_refs[t], _flat(me)), local_sems.at[t]) for t in range(n)]
        sends = []
        for t in range(n):
            mine[t].start()
            sends.append(copy(t, 0, me, sibling, True))
            sends += [copy(t, 1 + j, me, (*chip, c), True) for j, chip in enumerate(chips)]
        for cp in sends:
            cp.start()
        for t in range(n):
            for j, chip in enumerate(chips):
                copy(t, 1 + j, (*chip, c), me).wait_recv()
                passed = copy(t, 4 + j, (*chip, c), sibling)
                passed.start()
                sends.append(passed)
        for t in range(n):
            copy(t, 0, sibling, me).wait_recv()
            for j, chip in enumerate(chips):
                copy(t, 4 + j, (*chip, 1 - c), me).wait_recv()
        for cp in sends:
            cp.wait_send()
        for cp in mine:
            cp.wait()

    return pl.pallas_call(
        body, out_shape=tuple(out_shapes), in_specs=[ANY] * (n + len(deps)), out_specs=tuple([ANY] * n),
        scratch_shapes=[pltpu.SemaphoreType.DMA((n, 7)), pltpu.SemaphoreType.DMA((n, 7)), pltpu.SemaphoreType.DMA((n,))],
        name=name)(*locals_, *deps)


def _adamw(name, me, lands, owns, w, m, v, *, grid, land_specs, own_specs, p_spec):
    n_land = len(lands)

    def body(me_ref, *refs):
        land_refs, own_refs = refs[:n_land], refs[n_land:n_land + len(owns)]
        w_ref, m_ref, v_ref, g_ref, delta_ref, m_out, v_out = refs[n_land + len(owns):]
        ncols = w_ref.shape[-1]
        sums = []
        for i, land_ref in enumerate(land_refs):
            g = own_refs[i][...].astype(F32) if owns else land_ref[0].astype(F32)
            for s in range(0 if owns else 1, land_ref.shape[0]):
                g = g + land_ref[s].astype(F32)
            sums.append(g[:, :ncols])
        g = sums[0] if n_land == 1 else jnp.where(pl.program_id(0) == 0, sums[0], sums[1])
        m_new = ADAM_B1 * m_ref[...] + (1.0 - ADAM_B1) * g
        v_new = ADAM_B2 * v_ref[...] + (1.0 - ADAM_B2) * jnp.square(g)
        m_hat = m_new / (1.0 - ADAM_B1 ** ADAM_STEP)
        v_hat = v_new / (1.0 - ADAM_B2 ** ADAM_STEP)
        g_ref[...] = g
        delta_ref[...] = -ADAM_LR * (m_hat / (jnp.sqrt(v_hat) + ADAM_EPS) + ADAM_WD * w_ref[...])
        m_out[...] = m_new
        v_out[...] = v_new

    shape = jax.ShapeDtypeStruct(w.shape, F32)
    return pl.pallas_call(
        body, out_shape=(shape,) * 4,
        grid_spec=pltpu.PrefetchScalarGridSpec(
            num_scalar_prefetch=1, grid=grid, in_specs=list(land_specs) + list(own_specs) + [p_spec, p_spec, p_spec],
            out_specs=(p_spec,) * 4),
        name=name, compiler_params=_params())(me, *lands, *owns, w, m, v)


def _row(p, layer):
    return p[layer][None, :]


def _square(name, a, b, dims, out_dtype, deps=()):
    if a.shape == (D_MODEL, SEQ):
        return _matmul(name, a, b, grid=(2, 1), a_spec=pl.BlockSpec((512, SEQ), lambda i, k: (i, 0)),
                       b_spec=pl.BlockSpec((SEQ, D_MODEL), lambda i, k: (0, 0)),
                       o_spec=pl.BlockSpec((512, D_MODEL), lambda i, k: (i, 0)),
                       out_shape=jax.ShapeDtypeStruct((D_MODEL, D_MODEL), out_dtype), dims=NN, acc_shape=(8, 128),
                       deps=deps)
    return _matmul(name, a, b, grid=(SEQ // TM_MM, 1), a_spec=pl.BlockSpec((TM_MM, D_MODEL), lambda i, k: (i, 0)),
                   b_spec=pl.BlockSpec((D_MODEL, D_MODEL), lambda i, k: (0, 0)),
                   o_spec=pl.BlockSpec((TM_MM, D_MODEL), lambda i, k: (i, 0)),
                   out_shape=jax.ShapeDtypeStruct((SEQ, D_MODEL), out_dtype), dims=dims, acc_shape=(8, 128), deps=deps)


def _grouped_matmul(name, a_list, b, *, n_tiles, a_block, b_spec, o_spec, out_shape):
    n_groups = len(a_list)

    def a_spec(g):
        def index(j, i):
            mine = j // 3
            return (jnp.where(mine == g, i, jnp.where(mine < g, 0, n_tiles - 1)), 0)
        return pl.BlockSpec(a_block, index)

    def body(*refs):
        b_ref, o_ref = refs[n_groups], refs[n_groups + 1]
        mine = pl.program_id(0) // 3
        for g in range(n_groups):
            @pl.when(mine == g)
            def _(g=g):
                o_ref[...] = jnp.dot(refs[g][...], b_ref[...], preferred_element_type=F32).astype(o_ref.dtype)

    return pl.pallas_call(
        body, out_shape=out_shape, grid=(3 * n_groups, n_tiles), in_specs=[a_spec(g) for g in range(n_groups)] + [b_spec],
        out_specs=o_spec, name=name, compiler_params=_params())(*a_list, b)


def _qkv_fwd(name, hs, w):
    return _grouped_matmul(name, hs, w, n_tiles=SEQ // TM_MM, a_block=(TM_MM, D_MODEL),
                           b_spec=pl.BlockSpec((D_MODEL, D_MODEL), lambda j, i: (0, j)),
                           o_spec=pl.BlockSpec((TM_MM, D_MODEL), lambda j, i: (i, j)),
                           out_shape=jax.ShapeDtypeStruct((SEQ, 3 * len(hs) * D_MODEL), BF16))


def _qkv_dw(name, hs_t, dqkv):
    return _grouped_matmul(name, hs_t, dqkv, n_tiles=2, a_block=(512, SEQ),
                           b_spec=pl.BlockSpec((None, SEQ, D_MODEL), lambda j, i: (j, 0, 0)),
                           o_spec=pl.BlockSpec((512, D_MODEL), lambda j, i: (i, j)),
                           out_shape=jax.ShapeDtypeStruct((D_MODEL, 3 * len(hs_t) * D_MODEL), BF16))


def _proj_do_sorted(name, d_a, w_o):
    def body(da_ref, w_ref, *refs):
        value = lax.dot_general(da_ref[...], w_ref[...], NT, preferred_element_type=F32)
        refs[0][...] = value
        for j, dil in enumerate(SORTED):
            _sort_tile(refs[-1], value, dil, refs[1 + j])

    tile = pl.BlockSpec((TM, D_MODEL), lambda i: (i, 0))
    shapes = [jax.ShapeDtypeStruct((SEQ, D_MODEL), F32)] + [jax.ShapeDtypeStruct((dil, SEQ // dil, D_MODEL), F32) for dil in SORTED]
    result = pl.pallas_call(
        body, out_shape=tuple(shapes), grid=(SEQ // TM,),
        in_specs=[tile, pl.BlockSpec((D_MODEL, D_MODEL), lambda i: (0, 0))],
        out_specs=tuple([tile] + [_sorted_spec(dil) for dil in SORTED]), scratch_shapes=[_sort_scratch()],
        name=name, compiler_params=_params())(d_a, w_o)
    return [t.reshape(SEQ, D_MODEL) for t in result]


def _qkv_dh(name, dqkv, w, n_chunks, deps):
    tm = TM_MM // 2
    return _matmul(name, dqkv, w, grid=(n_chunks // 3, SEQ // tm, 1),
                   a_spec=pl.BlockSpec((3, tm, D_MODEL), lambda g, i, k: (g, i, 0)),
                   b_spec=pl.BlockSpec((D_MODEL, 3 * D_MODEL), lambda g, i, k: (0, g)),
                   o_spec=pl.BlockSpec((None, tm, D_MODEL), lambda g, i, k: (g, i, 0)),
                   out_shape=jax.ShapeDtypeStruct((n_chunks // 3, SEQ, D_MODEL), F32), dims=NT, acc_shape=(8, 128),
                   deps=deps, inner=3)


def _local_step(x, target, norms, rpb, fetch, emit, deps):
    mix_pre, mix_post, ffn_pre, ffn_post = norms
    slopes = 2.0 ** (-8.0 * jnp.arange(1, N_HEADS + 1, dtype=F32) / N_HEADS)
    rpb_pad = jnp.pad(rpb, ((0, 0), (0, 1), (0, 128 - 31)))
    saved = []

    hs = [_rms_fwd("l0_norm_mix", x, _row(mix_pre, 0), out_dtype=BF16, deps=deps)]
    for layer in range(2):
        tag = f"l{layer}"
        if layer == 0:
            table = _rpb_table(rpb_pad)
            w_qkv, w_o = fetch("na", [table, hs[0]], [hs[0]])
            qkv = _qkv_fwd(tag + "_qkv", hs, w_qkv)
            o, lse = _na_fwd(qkv, table)
            mixer = (hs, qkv, o, lse, table)
        else:
            w_qkv, w_o = fetch("dil", [saved[0][7]], [hs[0]])
            qkv = _qkv_fwd(tag + "_qkv", hs, w_qkv)
            outs, lses = zip(*[_dil_fwd(g, qkv, slopes * dil) for g, (_, dil) in enumerate(DIL_GROUPS)])
            merged, lse_total = _dil_merge(outs, lses)
            o = merged[0]
            mixer = (hs, qkv, merged, lses, lse_total)
        a = _square(tag + "_proj", o, w_o, NN, F32)
        x1, (h2,) = _post_norm_fwd(tag + "_post_mix", a, _row(mix_post, layer), x, _row(ffn_pre, layer))
        w_gu, w_down = fetch(f"ffn{layer}", [a], [h2])
        f, hg, hu, act_t = _ffn_fwd(tag + "_ffn", h2, w_gu, w_down)
        transposed = ([t.T for t in hs], o.astype(BF16).T, h2.T, act_t)
        saved.append((x, mixer, a, x1, transposed, hg, hu, f, w_qkv, w_o, w_gu, w_down))
        if layer == 0:
            x, next_hs = _post_norm_fwd(tag + "_post_ffn", f, _row(ffn_post, 0), x1, _row(mix_pre, 1), sorted_too=True)
            hs = [t.reshape(SEQ, D_MODEL) for t in next_hs]
        else:
            x = _rms_fwd(tag + "_post_ffn", f, _row(ffn_post, layer), res=x1)

    dx, loss = _loss_head("loss_head", x, target)
    d_norm = {k: [None, None] for k in ("mix_pre", "mix_post", "ffn_pre", "ffn_post")}
    d_rpb = None

    d_f, d_norm["ffn_post"][1] = _rms_bwd("b1_post_ffn", saved[1][7], _row(ffn_post, 1), [dx], out_dtype=BF16)
    for layer in (1, 0):
        tag = f"b{layer}"
        x0, mixer, a, x1, (h_t, o_t, h2_t, act_t), hg, hu, f, w_qkv, w_o, w_gu, w_down = saved[layer]
        dgu, d_h2, d_down = _ffn_bwd(tag + "_ffn", d_f, w_gu, w_down, hg, hu, act_t)
        d_gu = _ffn_dgu(tag + "_ffn_dgu", h2_t, dgu)
        sent = emit(f"ffn{layer}", [d_gu, d_down])
        dx1, d_a, d_norm["ffn_pre"][layer], d_norm["mix_post"][layer] = _norm_post_bwd(
            tag + "_norm_ffn", x1, _row(ffn_pre, layer), [d_h2], dx, a, _row(mix_post, layer), deps=sent)
        d_wo = _square(tag + "_proj_dw", o_t, d_a, NN, BF16)
        if layer == 0:
            _, qkv, o, lse, table = mixer
            d_o = _square(tag + "_proj_do", d_a, w_o, NT, BF16)
            dqkv, gp = _na_bwd(qkv, table, d_o, lse)
            d_rpb = _rpb_grad(gp)[:, :15, :31]
            sent = emit("na", [_qkv_dw(tag + "_qkv_dw", h_t, dqkv), d_wo])
            d_h = _qkv_dh(tag + "_qkv_dh", dqkv, w_qkv, 3, sent)
            dx, d_norm["mix_pre"][layer] = _rms_bwd(tag + "_norm_mix", x0, _row(mix_pre, layer), [d_h[0]], res=dx1)
        else:
            _, qkv, merged, lses, lse_total = mixer
            d_o = _proj_do_sorted(tag + "_proj_do", d_a, w_o)
            dqkv = lax.empty((3 * len(DIL_GROUPS), SEQ, D_MODEL), BF16)
            for g, (_, dil) in enumerate(DIL_GROUPS):
                dqkv = _dil_bwd(g, qkv, slopes * dil, d_o[g], merged[g], lses[g], lse_total[g], dqkv)
            sent = emit("dil", [_qkv_dw(tag + "_qkv_dw", h_t, dqkv), d_wo])
            d_h = _qkv_dh(tag + "_qkv_dh", dqkv, w_qkv, 9, sent)
            dx, d_f, d_norm["mix_pre"][1], d_norm["ffn_post"][0] = _norm_post_bwd(
                tag + "_norm_mix", x0, _row(mix_pre, 1), None, dx1, saved[0][7], _row(ffn_post, 0), groups=d_h)

    d_gains = [jnp.concatenate(d_norm[k], axis=0) for k in ("mix_pre", "mix_post", "ffn_pre", "ffn_post")]
    return loss, dx, d_gains, d_rpb


RPB_SIZE = N_HEADS * 15 * 31


def _pack_small(gains, rpb, last=None):
    top = jnp.concatenate(gains, axis=0).reshape(64, 128)
    bottom = jnp.pad(rpb.reshape(-1), (0, 64 * 128 - RPB_SIZE))
    if last is not None:
        bottom = bottom + jnp.pad(last.reshape(1), (64 * 128 - 1, 0))
    return jnp.concatenate([top, bottom.reshape(64, 128)], axis=0)


def _unpack_small(p):
    gains = p[:64].reshape(4, 2, D_MODEL)
    rpb = p[64:].reshape(-1)[:RPB_SIZE].reshape(1, N_HEADS, 15, 31)
    return [gains[i] for i in range(4)], rpb


GROUPS = ("na", "ffn0", "dil", "ffn1")


def kernel(x, norm_mix_pre, norm_mix_post, norm_ffn_pre, norm_ffn_post, na_w_qkv, na_w_o, na_rpb, dil_w_qkv, dil_w_o, ffn_w_gate, ffn_w_up, ffn_w_down, loss_target, m_norm_mix_pre, m_norm_mix_post, m_norm_ffn_pre, m_norm_ffn_post, m_na_w_qkv, m_na_w_o, m_na_rpb, m_dil_w_qkv, m_dil_w_o, m_ffn_w_gate, m_ffn_w_up, m_ffn_w_down, v_norm_mix_pre, v_norm_mix_post, v_norm_ffn_pre, v_norm_ffn_post, v_na_w_qkv, v_na_w_o, v_na_rpb, v_dil_w_qkv, v_dil_w_o, v_ffn_w_gate, v_ffn_w_up, v_ffn_w_down):
    na_cols, dil_cols, o_rows = 3 * D_MODEL // N_DEV, 9 * D_MODEL // N_DEV, D_MODEL // N_DEV
    ff_pad = FF_PAD - FF_SHARD
    me = (4 * lax.axis_index("x") + 2 * lax.axis_index("y") + lax.axis_index("c")).astype(jnp.int32).reshape(1)

    full = {
        "na": [((D_MODEL, 3 * D_MODEL), _columns(na_cols)), ((N_DEV, o_rows, D_MODEL), _leading)],
        "dil": [((D_MODEL, 9 * D_MODEL), _columns(dil_cols)), ((N_DEV, o_rows, D_MODEL), _leading)],
        "ffn0": [((N_DEV, 2, D_MODEL, FF_PAD), _leading), ((N_DEV, FF_PAD, D_MODEL), _leading)],
        "ffn1": [((N_DEV, 2, D_MODEL, FF_PAD), _leading), ((N_DEV, FF_PAD, D_MODEL), _leading)],
    }
    block = {
        "na": [(D_MODEL, na_cols), (o_rows, D_MODEL)], "dil": [(D_MODEL, dil_cols), (o_rows, D_MODEL)],
        "ffn0": [(2, D_MODEL, FF_PAD), (FF_PAD, D_MODEL)], "ffn1": [(2, D_MODEL, FF_PAD), (FF_PAD, D_MODEL)],
    }

    land_shapes = [jax.ShapeDtypeStruct(full[g][t][0], BF16) for g in GROUPS for t in range(2)]
    windows = [full[g][t][1] for g in GROUPS for t in range(2)]
    shards, lands = _prep_weights(me, na_w_qkv, na_w_o, dil_w_qkv, dil_w_o, ffn_w_gate, ffn_w_up, ffn_w_down, land_shapes)
    sems, shards, lands = _gather_start("gather_start", shards, lands, windows, [2] * len(GROUPS))

    def fetch(group, early, late):
        gi = GROUPS.index(group)
        mine = slice(2 * gi, 2 * gi + 2)
        pass_sems, shards_g, lands_g = _gather_pass_on(f"gather_pass_{group}", sems[gi], shards[mine], lands[mine],
                                                       windows[mine], early)
        qkv, o = _gather_wait(f"gather_wait_{group}", sems[gi], pass_sems, shards_g, lands_g, windows[mine], late)
        return (qkv, o.reshape(D_MODEL, D_MODEL)) if group in ("na", "dil") else (qkv, o)

    def grad_source(group, t):
        return _columns(block[group][0][1]) if (group in ("na", "dil") and t == 0) else _leading

    in_flight = {}

    def emit(group, grads):
        if group in ("na", "dil"):
            grads = [grads[0], grads[1].reshape(N_DEV, o_rows, D_MODEL)]
        sets = [(t, grad_source(group, t), t, _by_distance) for t in range(2)]
        landing = [lax.empty((N_DEV - 1,) + block[group][t], BF16) for t in range(2)]
        sems_g, grads, landing, tok = _send_start(f"exchange_start_{group}", grads, landing, [sets])
        in_flight[group] = (sems_g[0], grads, landing, sets)
        return [tok]

    norms = (norm_mix_pre, norm_mix_post, norm_ffn_pre, norm_ffn_post)
    loss, grad_x, d_gains, d_rpb = _local_step(x[0], loss_target[0], norms, na_rpb[0], fetch, emit, [shards[0]])

    landed, sent = {}, {}

    def wait_for(group, after):
        sems_g, grads, landing, sets = in_flight[group]
        sent[group], landed[group] = _send_wait(f"exchange_wait_{group}", sems_g, grads, landing, sets, after)

    for group in ("ffn1", "dil", "ffn0"):
        wait_for(group, [grad_x])

    def one(rows, tile, ncols, columns):
        own = (pl.BlockSpec((tile, ncols), lambda i, me: (i, me[0])) if columns
               else pl.BlockSpec((None, tile, ncols), lambda i, me: (me[0], i, 0)))
        return dict(grid=(rows // tile,), land_specs=[pl.BlockSpec((N_DEV - 1, tile, ncols), lambda i, me: (0, i, 0))],
                    own_specs=[own], p_spec=pl.BlockSpec((None, tile, ncols), lambda i, me: (0, i, 0)))

    def layered(block_shape, index, p_block, n_tiles):
        def specs(lead_size, lead):
            shape = (lead_size,) + block_shape
            return [pl.BlockSpec(shape, lambda l, r, me: index(lead(me), jnp.where(l == 0, r, n_tiles - 1))),
                    pl.BlockSpec(shape, lambda l, r, me: index(lead(me), jnp.where(l == 0, 0, r)))]
        return dict(grid=(2, n_tiles), land_specs=specs(N_DEV - 1, lambda me: 0), own_specs=specs(None, lambda me: me[0]),
                    p_spec=pl.BlockSpec(p_block, lambda l, r, me: (l, r, 0)))

    gu_lands, gu_owns = [landed["ffn0"][0], landed["ffn1"][0]], [sent["ffn0"][0], sent["ffn1"][0]]
    down_lands, down_owns = [landed["ffn0"][1], landed["ffn1"][1]], [sent["ffn0"][1], sent["ffn1"][1]]
    updates = {
        "dil_w_qkv": _adamw("adamw_dil_qkv", me, [landed["dil"][0]], [sent["dil"][0]], dil_w_qkv, m_dil_w_qkv, v_dil_w_qkv,
                            **one(D_MODEL, 128, dil_cols, True)),
        "dil_w_o": _adamw("adamw_dil_o", me, [landed["dil"][1]], [sent["dil"][1]], dil_w_o, m_dil_w_o, v_dil_w_o,
                          **one(o_rows, o_rows, D_MODEL, False)),
        "ffn_w_gate": _adamw("adamw_gate", me, gu_lands, gu_owns, ffn_w_gate, m_ffn_w_gate, v_ffn_w_gate,
                             **layered((None, 128, FF_PAD), lambda lead, r: (lead, 0, r, 0), (None, 128, FF_SHARD), 8)),
        "ffn_w_up": _adamw("adamw_up", me, gu_lands, gu_owns, ffn_w_up, m_ffn_w_up, v_ffn_w_up,
                           **layered((None, 128, FF_PAD), lambda lead, r: (lead, 1, r, 0), (None, 128, FF_SHARD), 8)),
        "ffn_w_down": _adamw("adamw_down", me, down_lands, down_owns, ffn_w_down, m_ffn_w_down, v_ffn_w_down,
                             **layered((176, D_MODEL), lambda lead, r: (lead, r, 0), (None, 176, D_MODEL), 2)),
    }
    done = [u[0] for u in updates.values()]
    small = _all_gather("gather_small", [_pack_small(d_gains, d_rpb, loss)], [jax.ShapeDtypeStruct((N_DEV, 128, 128), F32)],
                        [_leading], deps=done)[0]
    wait_for("na", [small])
    updates["na_w_qkv"] = _adamw("adamw_na_qkv", me, [landed["na"][0]], [sent["na"][0]], na_w_qkv, m_na_w_qkv, v_na_w_qkv,
                                 **one(D_MODEL, 256, na_cols, True))
    updates["na_w_o"] = _adamw("adamw_na_o", me, [landed["na"][1]], [sent["na"][1]], na_w_o, m_na_w_o, v_na_w_o,
                               **one(o_rows, o_rows, D_MODEL, False))
    gains = [norm_mix_pre, norm_mix_post, norm_ffn_pre, norm_ffn_post]
    m_gains = [m_norm_mix_pre, m_norm_mix_post, m_norm_ffn_pre, m_norm_ffn_post]
    v_gains = [v_norm_mix_pre, v_norm_mix_post, v_norm_ffn_pre, v_norm_ffn_post]
    packed = _adamw("adamw_small", me, [small], (), _pack_small(gains, na_rpb)[None], _pack_small(m_gains, m_na_rpb)[None],
                    _pack_small(v_gains, v_na_rpb)[None], grid=(1,),
                    land_specs=[pl.BlockSpec((N_DEV, 128, 128), lambda i, me: (0, 0, 0))], own_specs=[],
                    p_spec=pl.BlockSpec((None, 128, 128), lambda i, me: (0, 0, 0)))
    small_out = [_unpack_small(p[0]) for p in packed]

    order = ["na_w_qkv", "na_w_o", "na_rpb", "dil_w_qkv", "dil_w_o", "ffn_w_gate", "ffn_w_up", "ffn_w_down"]
    result = [packed[0][0, 127, 127], grad_x[None]]
    for kind in range(4):
        gains_k, rpb_k = small_out[kind]
        result += gains_k
        result += [rpb_k if name == "na_rpb" else updates[name][kind] for name in order]
    return tuple(result)
```

```python
import functools

import jax
import jax.numpy as jnp
from jax import lax
from jax.experimental import pallas as pl
from jax.experimental.pallas import tpu as pltpu

F32 = jnp.float32
BF16 = jnp.bfloat16
MESH = pl.DeviceIdType.MESH
ANY = pl.BlockSpec(memory_space=pl.ANY)
HBM = pl.BlockSpec(memory_space=pltpu.HBM)
SEM = pl.BlockSpec(memory_space=pltpu.SEMAPHORE)
EFFECT = pltpu.SideEffectType.DATAFLOW_SIDE_EFFECTING

N_DEV = 8
SEQ = 2048
D_MODEL = 1024
N_HEADS = 16
HEAD_DIM = 64
GRID_W = 64
NA_ROWS = 8
SEQ_ROWS = SEQ // GRID_W
DIL_GROUPS = ((128, 1), (512, 4), (2048, 16))
BAND = 128
RADIUS = 64
FF_SHARD = 352
FF_PAD = 384
RMS_EPS = 1e-6
NEG_INF = -1e30
Q_SCALE = HEAD_DIM ** -0.5

ADAM_LR = 0.001
ADAM_B1 = 0.9
ADAM_B2 = 0.999
ADAM_EPS = 1e-08
ADAM_WD = 0.01
ADAM_STEP = 10

VMEM_LIMIT = 56 * 1024 * 1024
TM = 512
TM_ROW = 1024
TM_MM = 1024

NN = (((1,), (0,)), ((), ()))
NT = (((1,), (1,)), ((), ()))
TN = (((0,), (0,)), ((), ()))


def _params():
    return pltpu.CompilerParams(vmem_limit_bytes=VMEM_LIMIT)


def _matmul(name, a, b, *, grid, a_spec, b_spec, o_spec, out_shape, dims, acc_shape, deps=(), inner=1):
    nk = grid[-1]
    kaxis = len(grid) - 1

    def body(a_ref, b_ref, *rest):
        o_ref, acc_ref = rest[-2], rest[-1]
        if inner == 1:
            part = lax.dot_general(a_ref[...].astype(BF16), b_ref[...].astype(BF16), dims, preferred_element_type=F32)
        elif len(b_ref.shape) == 2:
            a_all = jnp.concatenate([a_ref[j].astype(BF16) for j in range(inner)], axis=1)
            part = lax.dot_general(a_all, b_ref[...].astype(BF16), dims, preferred_element_type=F32)
        else:
            part = sum(lax.dot_general(a_ref[j].astype(BF16), b_ref[j].astype(BF16), dims, preferred_element_type=F32)
                       for j in range(inner))
        if nk == 1:
            o_ref[...] = part.astype(o_ref.dtype)
        else:
            k = pl.program_id(kaxis)

            @pl.when(k == 0)
            def _():
                acc_ref[...] = part

            @pl.when(k > 0)
            def _():
                acc_ref[...] += part

            @pl.when(k == nk - 1)
            def _():
                o_ref[...] = acc_ref[...].astype(o_ref.dtype)

    return pl.pallas_call(
        body, out_shape=out_shape, grid=grid, in_specs=[a_spec, b_spec] + [ANY] * len(deps), out_specs=o_spec,
        scratch_shapes=[pltpu.VMEM(acc_shape, F32)], name=name, compiler_params=_params())(a, b, *deps)


SORTED = tuple(d for _, d in DIL_GROUPS if d > 1)
LANE_CHUNKS = D_MODEL // 128


def _sort_scratch(tm=TM):
    return pltpu.VMEM((LANE_CHUNKS, tm, 128), F32)


def _sorted_view(t, dil):
    return t.reshape(dil, SEQ // dil, D_MODEL)


def _sorted_spec(dil, lead=(), tm=TM):
    return pl.BlockSpec((None,) * len(lead) + (dil, tm // dil, D_MODEL), lambda i: tuple(lead) + (0, i, 0))


def _sort_tile(scratch, value, dil, out_ref):
    tm = value.shape[0]
    for c in range(LANE_CHUNKS):
        scratch[c] = value[:, 128 * c:128 * (c + 1)]
    for r in range(dil):
        rows = [scratch.at[c][pl.ds(r, tm // dil, stride=dil), :] for c in range(LANE_CHUNKS)]
        out_ref[r] = jnp.concatenate(rows, axis=1).astype(out_ref.dtype)


def _unsort_tile(scratch, in_ref, dil):
    for r in range(dil):
        value = in_ref[r].astype(F32)
        for c in range(LANE_CHUNKS):
            scratch.at[c][pl.ds(r, value.shape[0], stride=dil), :] = value[:, 128 * c:128 * (c + 1)]
    return jnp.concatenate([scratch[c] for c in range(LANE_CHUNKS)], axis=1)


def _rms_fwd(name, x, g, res=None, out_dtype=F32, deps=(), sorted_too=False):
    tm = TM if sorted_too else TM_ROW
    n_tiles = SEQ // tm
    has_res = res is not None
    n_in = 2 + has_res + len(deps)

    def body(*refs):
        x_ref, g_ref = refs[0], refs[1]
        xv = x_ref[...]
        r = lax.rsqrt(jnp.mean(xv * xv, axis=-1, keepdims=True) + RMS_EPS)
        y = xv * r * g_ref[...]
        if has_res:
            y = refs[2][...] + y
        refs[n_in][...] = y.astype(out_dtype)
        if sorted_too:
            for j, dil in enumerate(SORTED):
                _sort_tile(refs[-1], y, dil, refs[n_in + 1 + j])

    tile = pl.BlockSpec((tm, D_MODEL), lambda i: (i, 0))
    gspec = pl.BlockSpec((1, D_MODEL), lambda i: (0, 0))
    ins = [x, g] + ([res] if has_res else []) + list(deps)
    specs = [tile, gspec] + ([tile] if has_res else []) + [ANY] * len(deps)
    shapes, out_specs = [jax.ShapeDtypeStruct((SEQ, D_MODEL), out_dtype)], [tile]
    if sorted_too:
        shapes += [jax.ShapeDtypeStruct((dil, SEQ // dil, D_MODEL), out_dtype) for dil in SORTED]
        out_specs += [_sorted_spec(dil) for dil in SORTED]
    result = pl.pallas_call(
        body, out_shape=tuple(shapes), grid=(n_tiles,), in_specs=specs, out_specs=tuple(out_specs),
        scratch_shapes=[_sort_scratch()] if sorted_too else [], name=name, compiler_params=_params())(*ins)
    return result if sorted_too else result[0]


def _rms_bwd(name, x, g, dys, res=None, out_dtype=F32, groups=None, deps=()):
    tm = TM if groups is not None else TM_ROW
    n_tiles = SEQ // tm
    n_dy = len(dys) if groups is None else 1 + len(SORTED)
    has_res = res is not None

    def body(*refs):
        x_ref, g_ref = refs[0], refs[1]
        dy_refs = refs[2:2 + n_dy]
        res_ref = refs[2 + n_dy] if has_res else None
        first_out = 2 + n_dy + has_res + len(deps)
        dx_ref, dg_ref, acc_ref = refs[first_out:first_out + 3]
        i = pl.program_id(0)
        xv = x_ref[...]
        r = lax.rsqrt(jnp.mean(xv * xv, axis=-1, keepdims=True) + RMS_EPS)
        xn = xv * r
        dy = dy_refs[0][...].astype(F32)
        for j, extra in enumerate(dy_refs[1:]):
            dy = dy + (extra[...].astype(F32) if groups is None else _unsort_tile(refs[-1], extra, SORTED[j]))
        dyg = dy * g_ref[...]
        dx = r * (dyg - xn * jnp.mean(dyg * xn, axis=-1, keepdims=True))
        if has_res:
            dx = res_ref[...] + dx
        dx_ref[...] = dx.astype(dx_ref.dtype)
        part = jnp.sum((dy * xn).reshape(tm // 8, 8, D_MODEL), axis=0)

        @pl.when(i == 0)
        def _():
            acc_ref[...] = part

        @pl.when(i > 0)
        def _():
            acc_ref[...] += part

        @pl.when(i == n_tiles - 1)
        def _():
            dg_ref[...] = jnp.broadcast_to(jnp.sum(acc_ref[...], axis=0, keepdims=True), (8, D_MODEL))

    tile = pl.BlockSpec((tm, D_MODEL), lambda i: (i, 0))
    gspec = pl.BlockSpec((1, D_MODEL), lambda i: (0, 0))
    if groups is None:
        dy_ins, dy_specs = list(dys), [tile] * n_dy
    else:
        dy_ins = [groups] + [groups.reshape(n_dy, dil, SEQ // dil, D_MODEL) for dil in SORTED]
        dy_specs = [pl.BlockSpec((None, tm, D_MODEL), lambda i: (0, i, 0))]
        dy_specs += [_sorted_spec(dil, lead=(1 + j,)) for j, dil in enumerate(SORTED)]
    ins = [x, g] + dy_ins + ([res] if has_res else []) + list(deps)
    specs = [tile, gspec] + dy_specs + ([tile] if has_res else []) + [ANY] * len(deps)
    dx, dg = pl.pallas_call(
        body, out_shape=(jax.ShapeDtypeStruct((SEQ, D_MODEL), out_dtype), jax.ShapeDtypeStruct((8, D_MODEL), F32)),
        grid=(n_tiles,), in_specs=specs,
        out_specs=(tile, pl.BlockSpec((8, D_MODEL), lambda i: (0, 0))),
        scratch_shapes=[pltpu.VMEM((8, D_MODEL), F32)] + ([_sort_scratch()] if groups is not None else []),
        name=name, compiler_params=_params())(*ins)
    return dx, dg[0:1]


def _rms(xv):
    return lax.rsqrt(jnp.mean(xv * xv, axis=-1, keepdims=True) + RMS_EPS)


def _post_norm_fwd(name, a, g_post, res, g_next, sorted_too=False):
    def body(a_ref, gp_ref, res_ref, gn_ref, x_ref, h_ref, ht_ref, *rest):
        av = a_ref[...]
        xv = res_ref[...] + av * _rms(av) * gp_ref[...]
        x_ref[...] = xv
        y = xv * _rms(xv) * gn_ref[...]
        h_ref[...] = y.astype(h_ref.dtype)
        ht_ref[...] = y.T.astype(ht_ref.dtype)
        if sorted_too:
            for j, dil in enumerate(SORTED):
                _sort_tile(rest[-1], y, dil, rest[j])

    tile = pl.BlockSpec((TM, D_MODEL), lambda i: (i, 0))
    gspec = pl.BlockSpec((1, D_MODEL), lambda i: (0, 0))
    shapes = [jax.ShapeDtypeStruct((SEQ, D_MODEL), F32), jax.ShapeDtypeStruct((SEQ, D_MODEL), BF16),
              jax.ShapeDtypeStruct((D_MODEL, SEQ), BF16)]
    out_specs = [tile, tile, pl.BlockSpec((D_MODEL, TM), lambda i: (0, i))]
    if sorted_too:
        shapes += [jax.ShapeDtypeStruct((dil, SEQ // dil, D_MODEL), BF16) for dil in SORTED]
        out_specs += [_sorted_spec(dil) for dil in SORTED]
    result = pl.pallas_call(
        body, out_shape=tuple(shapes), grid=(SEQ // TM,), in_specs=[tile, gspec, tile, gspec], out_specs=tuple(out_specs),
        scratch_shapes=[_sort_scratch()] if sorted_too else [], name=name, compiler_params=_params())(a, g_post, res, g_next)
    return result[0], list(result[1:])


def _norm_post_bwd(name, x, g_norm, dys, res, a, g_post, groups=None, deps=()):
    n_tiles = SEQ // TM
    n_dy = len(dys) if groups is None else 1 + len(SORTED)

    def body(*refs):
        x_ref, g_ref = refs[0], refs[1]
        dy_refs = refs[2:2 + n_dy]
        res_ref, a_ref, gp_ref = refs[2 + n_dy:5 + n_dy]
        dx_ref, da_ref, dg_ref, dgp_ref, acc_ref, accp_ref = refs[5 + n_dy + len(deps):11 + n_dy + len(deps)]
        i = pl.program_id(0)
        xv = x_ref[...]
        r = _rms(xv)
        xn = xv * r
        dy = dy_refs[0][...].astype(F32)
        for j, extra in enumerate(dy_refs[1:]):
            dy = dy + (extra[...].astype(F32) if groups is None else _unsort_tile(refs[-1], extra, SORTED[j]))
        dyg = dy * g_ref[...]
        dx = res_ref[...] + r * (dyg - xn * jnp.mean(dyg * xn, axis=-1, keepdims=True))
        dx_ref[...] = dx
        av = a_ref[...]
        ra = _rms(av)
        an = av * ra
        dxg = dx * gp_ref[...]
        da_ref[...] = (ra * (dxg - an * jnp.mean(dxg * an, axis=-1, keepdims=True))).astype(da_ref.dtype)
        part = jnp.sum((dy * xn).reshape(TM // 8, 8, D_MODEL), axis=0)
        part_p = jnp.sum((dx * an).reshape(TM // 8, 8, D_MODEL), axis=0)

        @pl.when(i == 0)
        def _():
            acc_ref[...] = part
            accp_ref[...] = part_p

        @pl.when(i > 0)
        def _():
            acc_ref[...] += part
            accp_ref[...] += part_p

        @pl.when(i == n_tiles - 1)
        def _():
            dg_ref[...] = jnp.broadcast_to(jnp.sum(acc_ref[...], axis=0, keepdims=True), (8, D_MODEL))
            dgp_ref[...] = jnp.broadcast_to(jnp.sum(accp_ref[...], axis=0, keepdims=True), (8, D_MODEL))

    tile = pl.BlockSpec((TM, D_MODEL), lambda i: (i, 0))
    gspec = pl.BlockSpec((1, D_MODEL), lambda i: (0, 0))
    gain = pl.BlockSpec((8, D_MODEL), lambda i: (0, 0))
    if groups is None:
        dy_ins, dy_specs = list(dys), [tile] * n_dy
    else:
        dy_ins = [groups] + [groups.reshape(n_dy, dil, SEQ // dil, D_MODEL) for dil in SORTED]
        dy_specs = [pl.BlockSpec((None, TM, D_MODEL), lambda i: (0, i, 0))]
        dy_specs += [_sorted_spec(dil, lead=(1 + j,)) for j, dil in enumerate(SORTED)]
    dx, da, dg, dgp = pl.pallas_call(
        body,
        out_shape=(jax.ShapeDtypeStruct((SEQ, D_MODEL), F32), jax.ShapeDtypeStruct((SEQ, D_MODEL), BF16),
                   jax.ShapeDtypeStruct((8, D_MODEL), F32), jax.ShapeDtypeStruct((8, D_MODEL), F32)),
        grid=(n_tiles,), in_specs=[tile, gspec] + dy_specs + [tile, tile, gspec] + [ANY] * len(deps),
        out_specs=(tile, tile, gain, gain),
        scratch_shapes=[pltpu.VMEM((8, D_MODEL), F32)] * 2 + ([_sort_scratch()] if groups is not None else []),
        name=name, compiler_params=_params())(x, g_norm, *dy_ins, res, a, g_post, *deps)
    return dx, da, dg[0:1], dgp[0:1]


def _loss_head(name, y, target):
    tm = TM_ROW
    n_tiles = SEQ // tm

    def body(y_ref, t_ref, dy_ref, loss_ref, acc_ref):
        i = pl.program_id(0)
        diff = y_ref[...] - t_ref[...]
        dy_ref[...] = diff * (1.0 / D_MODEL)
        part = jnp.sum((diff * diff).reshape(tm // 8, 8, D_MODEL), axis=0)

        @pl.when(i == 0)
        def _():
            acc_ref[...] = part

        @pl.when(i > 0)
        def _():
            acc_ref[...] += part

        @pl.when(i == n_tiles - 1)
        def _():
            loss_ref[...] = jnp.full((8, 128), jnp.sum(acc_ref[...]) * (0.5 / D_MODEL), F32)

    tile = pl.BlockSpec((tm, D_MODEL), lambda i: (i, 0))
    dy, loss = pl.pallas_call(
        body, out_shape=(jax.ShapeDtypeStruct((SEQ, D_MODEL), F32), jax.ShapeDtypeStruct((8, 128), F32)),
        grid=(n_tiles,), in_specs=[tile, tile], out_specs=(tile, pl.BlockSpec((8, 128), lambda i: (0, 0))),
        scratch_shapes=[pltpu.VMEM((8, D_MODEL), F32)], name=name, compiler_params=_params())(y, target)
    return dy, loss[0, 0]


def _row_index(shape):
    return lax.broadcasted_iota(jnp.int32, shape, 0)


def _lane_index(shape):
    return lax.broadcasted_iota(jnp.int32, shape, len(shape) - 1)


def _skew_rows(t, direction):
    q = _row_index(t.shape) & (GRID_W - 1)
    for bit in range(6):
        step = 1 << bit
        shift = step if direction > 0 else 128 - step
        t = jnp.where((q & step) != 0, pltpu.roll(t, shift, 1), t)
    return t


RPB_HEADS = 4


def _rpb_table(rpb_pad):
    rows = RPB_HEADS * 16 * GRID_W

    def body(r_ref, t_ref):
        lane = _lane_index((rows, 128))
        v = pltpu.roll(r_ref[...].reshape(RPB_HEADS * 16, 128), 128 - 15, 1)
        t = _skew_rows(jnp.broadcast_to(v[:, None, :], (RPB_HEADS * 16, GRID_W, 128)).reshape(rows, 128), +1)
        t = jnp.where(lane < GRID_W, t, 0.0)
        below = jnp.concatenate([t[GRID_W:], jnp.zeros((GRID_W, 128), F32)], axis=0)
        first_col = jnp.clip((_row_index((rows, 128)) & (GRID_W - 1)) - 8, 0, GRID_W - 16)
        key_col = lane & (GRID_W - 1)
        in_window = (key_col >= first_col) & (key_col < first_col + 16)
        t_ref[...] = jnp.where(in_window, t + pltpu.roll(below, GRID_W, 1), NEG_INF).reshape(RPB_HEADS, 16, GRID_W, 128)

    return pl.pallas_call(
        body, out_shape=jax.ShapeDtypeStruct((N_HEADS, 16, GRID_W, 128), F32), grid=(N_HEADS // RPB_HEADS,),
        in_specs=[pl.BlockSpec((RPB_HEADS, 16, 128), lambda h: (h, 0, 0))],
        out_specs=pl.BlockSpec((RPB_HEADS, 16, GRID_W, 128), lambda h: (h, 0, 0, 0)),
        name="rpb_table", compiler_params=_params())(rpb_pad)


def _rpb_grad(gp):
    rows = RPB_HEADS * 16 * GRID_W

    def body(g_ref, o_ref):
        lane = _lane_index((rows, 128))
        g = g_ref[...].reshape(rows, 128)
        low = jnp.where(lane < GRID_W, g, 0.0)
        high = pltpu.roll(jnp.where(lane >= GRID_W, g, 0.0), GRID_W, 1)
        above = jnp.concatenate([jnp.zeros((GRID_W, 128), F32), high[:rows - GRID_W]], axis=0)
        diag = jnp.sum(_skew_rows(low + above, -1).reshape(RPB_HEADS * 16, GRID_W, 128), axis=1)
        o_ref[...] = pltpu.roll(diag, 15, 1).reshape(RPB_HEADS, 16, 128)

    return pl.pallas_call(
        body, out_shape=jax.ShapeDtypeStruct((N_HEADS, 16, 128), F32), grid=(N_HEADS // RPB_HEADS,),
        in_specs=[pl.BlockSpec((RPB_HEADS, 16, GRID_W, 128), lambda h: (h, 0, 0, 0))],
        out_specs=pl.BlockSpec((RPB_HEADS, 16, 128), lambda h: (h, 0, 0)),
        name="rpb_grad", compiler_params=_params())(gp)


NA_KEYS = NA_ROWS * GRID_W


def _na_window(i):
    first_row = jnp.clip(i - NA_ROWS // 2, 0, SEQ_ROWS - NA_ROWS)
    return pl.multiple_of(first_row * GRID_W, GRID_W), first_row - i + NA_ROWS - 1


NA_STEP = 32


def _head_masks():
    lane = _lane_index((1, 128))
    return (lane < HEAD_DIM, lane >= HEAD_DIM)


def _stack_heads(t, masks):
    zero = jnp.zeros_like(t)
    return jnp.concatenate([jnp.where(masks[0], t, zero), jnp.where(masks[1], t, zero)], axis=0)


def _unstack_heads(t, masks):
    n = t.shape[0] // 2
    return jnp.where(masks[0], t[:n], t[n:])


def _stack_columns(t):
    return jnp.concatenate([t[:, 0:1], t[:, HEAD_DIM:HEAD_DIM + 1]], axis=0)


def _na_scores(qs, kw, tp_ref, dr0):
    s = lax.dot_general(qs, kw, NT, preferred_element_type=F32)
    bias = jnp.concatenate(
        [jnp.concatenate([tp_ref[a, pl.ds(dr0 + 2 * c, 1), :, :].reshape(GRID_W, 128) for c in range(4)], axis=1)
         for a in range(2)], axis=0)
    return s + bias


def _na_specs():
    q_spec = pl.BlockSpec((NA_STEP * GRID_W, 128), lambda hp, i: (i, hp))
    k_spec = pl.BlockSpec((SEQ, 128), lambda hp, i: (0, 8 + hp))
    v_spec = pl.BlockSpec((SEQ, 128), lambda hp, i: (0, 16 + hp))
    tp_spec = pl.BlockSpec((2, 16, GRID_W, 128), lambda hp, i: (hp, 0, 0, 0))
    return q_spec, k_spec, v_spec, tp_spec


def _na_fwd(qkv, table):
    def body(q_ref, k_ref, v_ref, tp_ref, o_ref, lse_ref):
        masks = _head_masks()
        for r in range(NA_STEP):
            rows = slice(r * GRID_W, (r + 1) * GRID_W)
            start, dr0 = _na_window(pl.program_id(1) * NA_STEP + r)
            kw = k_ref[pl.ds(start, NA_KEYS), :]
            vw = v_ref[pl.ds(start, NA_KEYS), :]
            s = _na_scores(_stack_heads(q_ref[rows, :] * Q_SCALE, masks), kw, tp_ref, dr0)
            m = jnp.max(s, axis=-1, keepdims=True)
            p = jnp.exp(s - m)
            denom = jnp.sum(p, axis=-1, keepdims=True)
            out = jnp.dot(p.astype(BF16), vw, preferred_element_type=F32) / denom
            o_ref[rows, :] = _unstack_heads(out, masks).astype(o_ref.dtype)
            lse_ref[rows, :] = _unstack_heads(jnp.broadcast_to(m + jnp.log(denom), (2 * GRID_W, 128)), masks)

    q_spec, k_spec, v_spec, tp_spec = _na_specs()
    return pl.pallas_call(
        body, out_shape=(jax.ShapeDtypeStruct((SEQ, D_MODEL), BF16), jax.ShapeDtypeStruct((SEQ, D_MODEL), F32)),
        grid=(N_HEADS // 2, SEQ_ROWS // NA_STEP), in_specs=[q_spec, k_spec, v_spec, tp_spec],
        out_specs=(q_spec, q_spec), name="na_fwd", compiler_params=_params())(qkv, qkv, qkv, table)


def _na_bwd(qkv, table, d_out, lse):
    def body(q_ref, k_ref, v_ref, tp_ref, do_ref, lse_ref, dqkv_ref, gp_ref, dk_acc, dv_acc):
        step = pl.program_id(1)

        @pl.when(step == 0)
        def _():
            dk_acc[...] = jnp.zeros_like(dk_acc)
            dv_acc[...] = jnp.zeros_like(dv_acc)
            gp_ref[...] = jnp.zeros_like(gp_ref)

        masks = _head_masks()
        for r in range(NA_STEP):
            rows = slice(r * GRID_W, (r + 1) * GRID_W)
            i = step * NA_STEP + r
            start, dr0 = _na_window(i)
            kw = k_ref[pl.ds(start, NA_KEYS), :]
            vw = v_ref[pl.ds(start, NA_KEYS), :]
            qs = _stack_heads(q_ref[rows, :] * Q_SCALE, masks)
            dos = _stack_heads(do_ref[rows, :], masks)
            p = jnp.exp(_na_scores(qs, kw, tp_ref, dr0) - _stack_columns(lse_ref[rows, :]))
            dp = lax.dot_general(dos, vw, NT, preferred_element_type=F32)
            ds = p * (dp - jnp.sum(p * dp, axis=-1, keepdims=True))
            for a in range(2):
                for c in range(4):
                    gp_ref[a, pl.ds(dr0 + 2 * c, 1), :, :] += (
                        ds[a * GRID_W:(a + 1) * GRID_W, 128 * c:128 * (c + 1)].reshape(1, GRID_W, 128))
            dsb = ds.astype(BF16)
            dq = _unstack_heads(jnp.dot(dsb, kw, preferred_element_type=F32), masks) * Q_SCALE
            dqkv_ref[0, pl.ds(pl.multiple_of(i * GRID_W, GRID_W), GRID_W), :] = dq.astype(dqkv_ref.dtype)
            dk_acc[pl.ds(start, NA_KEYS), :] += lax.dot_general(dsb, qs, TN, preferred_element_type=F32)
            dv_acc[pl.ds(start, NA_KEYS), :] += lax.dot_general(p.astype(BF16), dos, TN, preferred_element_type=F32)

        @pl.when(step == SEQ_ROWS // NA_STEP - 1)
        def _():
            dqkv_ref[1] = dk_acc[...].astype(dqkv_ref.dtype)
            dqkv_ref[2] = dv_acc[...].astype(dqkv_ref.dtype)

    q_spec, k_spec, v_spec, tp_spec = _na_specs()
    return pl.pallas_call(
        body,
        out_shape=(jax.ShapeDtypeStruct((3, SEQ, D_MODEL), BF16), jax.ShapeDtypeStruct((N_HEADS, 16, GRID_W, 128), F32)),
        grid=(N_HEADS // 2, SEQ_ROWS // NA_STEP), in_specs=[q_spec, k_spec, v_spec, tp_spec, q_spec, q_spec],
        out_specs=(pl.BlockSpec((3, SEQ, 128), lambda hp, i: (0, 0, hp)), tp_spec),
        scratch_shapes=[pltpu.VMEM((SEQ, 128), F32), pltpu.VMEM((SEQ, 128), F32)],
        name="na_bwd", compiler_params=_params())(qkv, qkv, qkv, table, d_out, lse)


DIL_STEP = 16


def _dil_geometry(group):
    dil = DIL_GROUPS[group][1]
    sub_len = SEQ // dil
    blocks = sub_len // BAND
    return dil, sub_len, max(blocks // DIL_STEP, 1), max(DIL_STEP // blocks, 1), min(2 * BAND, sub_len)


def _dil_block(step, r, sub_len, subs):
    per_sub = DIL_STEP // subs
    return (r // per_sub) * sub_len, step * per_sub + r % per_sub


def _dil_window(b, sub_len, n_keys):
    if n_keys == sub_len:
        return 0
    return pl.multiple_of(jnp.clip(b * BAND - RADIUS, 0, sub_len - n_keys), RADIUS)


def _dil_bias(b, start, n_keys, slope_ref, hp):
    row = _row_index((2 * BAND, n_keys))
    qpos = b * BAND + (row & (BAND - 1))
    kpos = start + _lane_index((2 * BAND, n_keys))
    dist = jnp.abs(qpos - kpos)
    slope = jnp.where(row < BAND, slope_ref[2 * hp], slope_ref[2 * hp + 1])
    return slope * dist.astype(F32), dist <= RADIUS


def _dil_scores(qs, kw, penalty, valid):
    return jnp.where(valid, lax.dot_general(qs, kw, NT, preferred_element_type=F32) - penalty, NEG_INF)


def _dil_specs(group):
    dil, sub_len, steps, subs, _ = _dil_geometry(group)
    col = group * 24
    rows = DIL_STEP * BAND
    q_spec = pl.BlockSpec((rows, 128), lambda n, hp, b: (n * steps + b, col + hp))
    k_spec = pl.BlockSpec((subs * sub_len, 128), lambda n, hp, b: (n, col + 8 + hp))
    v_spec = pl.BlockSpec((subs * sub_len, 128), lambda n, hp, b: (n, col + 16 + hp))
    tile = pl.BlockSpec((rows, 128), lambda n, hp, b: (n * steps + b, hp))
    smem = pl.BlockSpec(memory_space=pltpu.SMEM)
    return (dil // subs, N_HEADS // 2, steps), q_spec, k_spec, v_spec, tile, smem


def _dil_fwd(group, qkv, slopes):
    _, sub_len, _, subs, n_keys = _dil_geometry(group)

    def body(q_ref, k_ref, v_ref, slope_ref, o_ref, lse_ref):
        hp = pl.program_id(1)
        masks = _head_masks()
        for r in range(DIL_STEP):
            rows = slice(r * BAND, (r + 1) * BAND)
            base, b = _dil_block(pl.program_id(2), r, sub_len, subs)
            start = _dil_window(b, sub_len, n_keys)
            kw = k_ref[pl.ds(base + start, n_keys), :]
            vw = v_ref[pl.ds(base + start, n_keys), :]
            penalty, valid = _dil_bias(b, start, n_keys, slope_ref, hp)
            s = _dil_scores(_stack_heads(q_ref[rows, :] * Q_SCALE, masks), kw, penalty, valid)
            m = jnp.max(s, axis=-1, keepdims=True)
            p = jnp.exp(s - m)
            denom = jnp.sum(p, axis=-1, keepdims=True)
            out = jnp.dot(p.astype(BF16), vw, preferred_element_type=F32) / denom
            o_ref[rows, :] = _unstack_heads(out, masks).astype(o_ref.dtype)
            lse_ref[rows, :] = _unstack_heads(jnp.broadcast_to(m + jnp.log(denom), (2 * BAND, 128)), masks)

    grid, q_spec, k_spec, v_spec, tile, smem = _dil_specs(group)
    return pl.pallas_call(
        body, out_shape=(jax.ShapeDtypeStruct((SEQ, D_MODEL), BF16), jax.ShapeDtypeStruct((SEQ, D_MODEL), F32)),
        grid=grid, in_specs=[q_spec, k_spec, v_spec, smem], out_specs=(tile, tile),
        name=f"dil_fwd_{group}", compiler_params=_params())(qkv, qkv, qkv, slopes)


def _dil_merge(outs, lses):
    n_sorted = len(SORTED)

    def body(*refs):
        o_refs, l_refs = refs[:3], refs[3:6]
        out_refs, lse_refs, scratch = refs[6:7 + n_sorted], refs[7 + n_sorted:8 + 2 * n_sorted], refs[-1]
        os_ = [o_refs[0][...]] + [_unsort_tile(scratch, o_refs[1 + j], dil) for j, dil in enumerate(SORTED)]
        ls = [l_refs[0][...]] + [_unsort_tile(scratch, l_refs[1 + j], dil) for j, dil in enumerate(SORTED)]
        m = jnp.maximum(jnp.maximum(ls[0], ls[1]), ls[2])
        es = [jnp.exp(v - m) for v in ls]
        total = es[0] + es[1] + es[2]
        merged = (es[0] * os_[0] + es[1] * os_[1] + es[2] * os_[2]) / total
        lse = m + jnp.log(total)
        out_refs[0][...] = merged
        lse_refs[0][...] = lse
        for j, dil in enumerate(SORTED):
            _sort_tile(scratch, merged, dil, out_refs[1 + j])
            _sort_tile(scratch, lse, dil, lse_refs[1 + j])

    tm = 256
    tile = pl.BlockSpec((tm, D_MODEL), lambda i: (i, 0))
    specs = [tile] + [_sorted_spec(dil, tm=tm) for dil in SORTED]
    shapes = [jax.ShapeDtypeStruct((SEQ, D_MODEL), F32)] + [jax.ShapeDtypeStruct((dil, SEQ // dil, D_MODEL), F32) for dil in SORTED]
    views = lambda ts: [ts[0]] + [_sorted_view(t, dil) for t, dil in zip(ts[1:], SORTED)]
    result = pl.pallas_call(
        body, out_shape=tuple(shapes * 2), grid=(SEQ // tm,), in_specs=specs * 2, out_specs=tuple(specs * 2),
        scratch_shapes=[_sort_scratch(tm)], name="dil_merge", compiler_params=_params())(*views(outs), *views(lses))
    flat = [t.reshape(SEQ, D_MODEL) for t in result]
    return flat[:1 + n_sorted], flat[1 + n_sorted:]


def _dil_bwd(group, qkv, slopes, d_out, out, lse_group, lse_total, into):
    _, sub_len, steps, subs, n_keys = _dil_geometry(group)

    def body(q_ref, k_ref, v_ref, slope_ref, do_ref, o_ref, lg_ref, lt_ref, into_ref, dqkv_ref, dk_acc, dv_acc):
        hp, step = pl.program_id(1), pl.program_id(2)

        @pl.when(step == 0)
        def _():
            dk_acc[...] = jnp.zeros_like(dk_acc)
            dv_acc[...] = jnp.zeros_like(dv_acc)

        masks = _head_masks()
        for r in range(DIL_STEP):
            rows = slice(r * BAND, (r + 1) * BAND)
            base, b = _dil_block(step, r, sub_len, subs)
            start = _dil_window(b, sub_len, n_keys)
            keys = pl.ds(base + start, n_keys)
            kw = k_ref[keys, :]
            vw = v_ref[keys, :]
            penalty, valid = _dil_bias(b, start, n_keys, slope_ref, hp)
            qs = _stack_heads(q_ref[rows, :] * Q_SCALE, masks)
            lse2 = lg_ref[rows, :]
            weight = jnp.exp(lse2 - lt_ref[rows, :])
            do2 = do_ref[rows, :]
            dogs = _stack_heads((weight * do2).astype(BF16), masks)
            delta = _stack_columns(weight) * jnp.sum(_stack_heads(do2 * o_ref[rows, :], masks), axis=-1, keepdims=True)
            p = jnp.exp(_dil_scores(qs, kw, penalty, valid) - _stack_columns(lse2))
            dp = lax.dot_general(dogs, vw, NT, preferred_element_type=F32)
            dsb = (p * (dp - delta)).astype(BF16)
            dq = _unstack_heads(jnp.dot(dsb, kw, preferred_element_type=F32), masks) * Q_SCALE
            dqkv_ref[0, pl.ds(pl.multiple_of(base + b * BAND, BAND), BAND), :] = dq.astype(dqkv_ref.dtype)
            dk_acc[keys, :] += lax.dot_general(dsb, qs, TN, preferred_element_type=F32)
            dv_acc[keys, :] += lax.dot_general(p.astype(BF16), dogs, TN, preferred_element_type=F32)

        @pl.when(step == steps - 1)
        def _():
            dqkv_ref[1] = dk_acc[...].astype(dqkv_ref.dtype)
            dqkv_ref[2] = dv_acc[...].astype(dqkv_ref.dtype)

    grid, q_spec, k_spec, v_spec, tile, smem = _dil_specs(group)
    return pl.pallas_call(
        body, out_shape=jax.ShapeDtypeStruct(into.shape, into.dtype), grid=grid,
        in_specs=[q_spec, k_spec, v_spec, smem, tile, tile, tile, tile, ANY],
        out_specs=pl.BlockSpec((3, subs * sub_len, 128), lambda n, hp, b: (group, n, hp)),
        scratch_shapes=[pltpu.VMEM((subs * sub_len, 128), F32), pltpu.VMEM((subs * sub_len, 128), F32)],
        input_output_aliases={8: 0}, name=f"dil_bwd_{group}", compiler_params=_params(),
    )(qkv, qkv, qkv, slopes, d_out, out, lse_group, lse_total, into)


def _accumulate_rows(acc_ref, i, first, part):
    rows = pl.ds(pl.multiple_of(i * TM_MM, TM_MM), TM_MM)

    @pl.when(first)
    def _():
        acc_ref[rows, :] = part

    @pl.when(jnp.logical_not(first))
    def _():
        acc_ref[rows, :] += part


def _ffn_specs():
    tile = pl.BlockSpec((TM_MM, D_MODEL), lambda d, i: (i, 0))
    gate = pl.BlockSpec((None, None, D_MODEL, FF_PAD), lambda d, i: (d, 0, 0, 0))
    up = pl.BlockSpec((None, None, D_MODEL, FF_PAD), lambda d, i: (d, 1, 0, 0))
    down = pl.BlockSpec((None, FF_PAD, D_MODEL), lambda d, i: (d, 0, 0))
    hidden = pl.BlockSpec((None, TM_MM, FF_PAD), lambda d, i: (d, i, 0))
    whole = pl.BlockSpec((SEQ, D_MODEL), lambda d, i: (0, 0))
    return tile, gate, up, down, hidden, whole


def _ffn_fwd(name, h, w_gu, w_down):
    def body(h_ref, wg_ref, wu_ref, wd_ref, f_ref, hg_ref, hu_ref, act_t_ref):
        hv = h_ref[...]
        hg = jnp.dot(hv, wg_ref[...], preferred_element_type=F32)
        hu = jnp.dot(hv, wu_ref[...], preferred_element_type=F32)
        act = hg * jax.nn.sigmoid(hg) * hu
        act_t_ref[...] = act.T.astype(act_t_ref.dtype)
        hg_ref[...] = hg.astype(hg_ref.dtype)
        hu_ref[...] = hu.astype(hu_ref.dtype)
        _accumulate_rows(f_ref, pl.program_id(1), pl.program_id(0) == 0,
                         jnp.dot(act.astype(BF16), wd_ref[...], preferred_element_type=F32))

    tile, gate, up, down, hidden, whole = _ffn_specs()
    shape = jax.ShapeDtypeStruct((N_DEV, SEQ, FF_PAD), BF16)
    return pl.pallas_call(
        body, out_shape=(jax.ShapeDtypeStruct((SEQ, D_MODEL), F32), shape, shape, jax.ShapeDtypeStruct((N_DEV, FF_PAD, SEQ), BF16)),
        grid=(N_DEV, SEQ // TM_MM), in_specs=[tile, gate, up, down],
        out_specs=(whole, hidden, hidden, pl.BlockSpec((None, FF_PAD, TM_MM), lambda d, i: (d, 0, i))),
        name=name, compiler_params=_params())(h, w_gu, w_gu, w_down)


def _ffn_dgu(name, h_t, dgu):
    def body(h_ref, dgu_ref, o_ref):
        both = jnp.dot(h_ref[...], jnp.concatenate([dgu_ref[0], dgu_ref[1]], axis=1), preferred_element_type=F32)
        o_ref[0] = both[:, :FF_PAD].astype(o_ref.dtype)
        o_ref[1] = both[:, FF_PAD:].astype(o_ref.dtype)

    return pl.pallas_call(
        body, out_shape=jax.ShapeDtypeStruct((N_DEV, 2, D_MODEL, FF_PAD), BF16), grid=(N_DEV,),
        in_specs=[pl.BlockSpec((D_MODEL, SEQ), lambda d: (0, 0)), pl.BlockSpec((2, None, SEQ, FF_PAD), lambda d: (0, d, 0, 0))],
        out_specs=pl.BlockSpec((None, 2, D_MODEL, FF_PAD), lambda d: (d, 0, 0, 0)),
        name=name, compiler_params=_params())(h_t, dgu)


def _ffn_bwd(name, d_f, w_gu, w_down, hg, hu, act_t):
    n_tiles = SEQ // TM_MM

    def body(df_ref, wg_ref, wu_ref, wd_ref, hg_ref, hu_ref, act_t_ref, dgu_ref, dh_ref, dwd_ref, dwd_acc):
        i = pl.program_id(1)
        d_down = jnp.dot(act_t_ref[...], df_ref[...], preferred_element_type=F32)

        @pl.when(i == 0)
        def _():
            dwd_acc[...] = d_down

        @pl.when(i > 0)
        def _():
            dwd_acc[...] += d_down

        @pl.when(i == n_tiles - 1)
        def _():
            dwd_ref[...] = dwd_acc[...].astype(dwd_ref.dtype)

        dact = lax.dot_general(df_ref[...], wd_ref[...], NT, preferred_element_type=F32)
        hgv = hg_ref[...].astype(F32)
        sig = jax.nn.sigmoid(hgv)
        d_gate = (dact * hu_ref[...].astype(F32) * (sig * (1.0 + hgv * (1.0 - sig)))).astype(BF16)
        d_up = (dact * hgv * sig).astype(BF16)
        dgu_ref[0] = d_gate
        dgu_ref[1] = d_up
        part = lax.dot_general(jnp.concatenate([d_gate, d_up], axis=1), jnp.concatenate([wg_ref[...], wu_ref[...]], axis=1),
                               NT, preferred_element_type=F32)
        _accumulate_rows(dh_ref, pl.program_id(1), pl.program_id(0) == 0, part)

    tile, gate, up, down, hidden, whole = _ffn_specs()
    return pl.pallas_call(
        body, out_shape=(jax.ShapeDtypeStruct((2, N_DEV, SEQ, FF_PAD), BF16), jax.ShapeDtypeStruct((SEQ, D_MODEL), F32),
                         jax.ShapeDtypeStruct((N_DEV, FF_PAD, D_MODEL), BF16)),
        grid=(N_DEV, n_tiles),
        in_specs=[tile, gate, up, down, hidden, hidden, pl.BlockSpec((None, FF_PAD, TM_MM), lambda d, i: (d, 0, i))],
        out_specs=(pl.BlockSpec((2, None, TM_MM, FF_PAD), lambda d, i: (0, d, i, 0)), whole, down),
        scratch_shapes=[pltpu.VMEM((FF_PAD, D_MODEL), F32)],
        name=name, compiler_params=_params())(d_f, w_gu, w_gu, w_down, hg, hu, act_t)


def _position():
    return lax.axis_index("x"), lax.axis_index("y"), lax.axis_index("c")


def _flat(p):
    return 4 * p[0] + 2 * p[1] + p[2]


def _peer(me, k):
    x, y, c = me
    return (1 - x if k & 4 else x, 1 - y if k & 2 else y, 1 - c if k & 1 else c)


def _columns(width):
    return lambda ref, d: ref.at[:, pl.ds(pl.multiple_of(d * width, 128), width)]


def _leading(ref, d):
    return ref.at[d]


def _whole(ref, d):
    return ref


def _by_sender(window):
    return lambda ref, sender, k: window(ref, sender)


def _by_distance(ref, sender, k):
    return ref.at[k - 1]


def _prep_weights(me, na_qkv, na_o, dil_qkv, dil_o, gate, up, down, land_shapes):
    na_cols, dil_cols = na_qkv.shape[-1], dil_qkv.shape[-1]
    o_rows = na_o.shape[1]
    tiles = 4
    rows, rows_o = D_MODEL // tiles, o_rows // tiles

    def body(me_ref, naq, nao, dq, do_, g0, u0, d0, g1, u1, d1, *outs):
        def put(t, index, value):
            outs[t][index] = value
            outs[8 + t][index] = value

        put(0, ..., naq[...].astype(BF16))
        put(1, ..., nao[...].astype(BF16))
        put(4, ..., dq[...].astype(BF16))
        put(5, ..., do_[...].astype(BF16))
        for t, (g, u, d) in ((2, (g0, u0, d0)), (6, (g1, u1, d1))):
            for j, part in enumerate((g, u)):
                put(t, (j, slice(None), slice(0, FF_SHARD)), part[...].astype(BF16))
                put(t, (j, slice(None), slice(FF_SHARD, FF_PAD)), jnp.zeros((rows, FF_PAD - FF_SHARD), BF16))
            put(t + 1, (slice(0, FF_SHARD), slice(None)), d[...].astype(BF16))
            put(t + 1, (slice(FF_SHARD, FF_PAD), slice(None)), jnp.zeros((FF_PAD - FF_SHARD, D_MODEL), BF16))

    def tiled(width):
        return pl.BlockSpec((None, rows, width), lambda i, me: (0, i, 0))

    def layer(l, width):
        return pl.BlockSpec((None, rows, width), lambda i, me: (l, i, 0))

    def whole_layer(l):
        return pl.BlockSpec((None, FF_SHARD, D_MODEL), lambda i, me: (l, 0, 0))

    in_specs = [tiled(na_cols), pl.BlockSpec((None, rows_o, D_MODEL), lambda i, me: (0, i, 0)), tiled(dil_cols),
                pl.BlockSpec((None, rows_o, D_MODEL), lambda i, me: (0, i, 0)),
                layer(0, FF_SHARD), layer(0, FF_SHARD), whole_layer(0), layer(1, FF_SHARD), layer(1, FF_SHARD), whole_layer(1)]
    o_shard = pl.BlockSpec((rows_o, D_MODEL), lambda i, me: (i, 0))
    o_land = pl.BlockSpec((None, rows_o, D_MODEL), lambda i, me: (me[0], i, 0))
    gu_shard = pl.BlockSpec((2, rows, FF_PAD), lambda i, me: (0, i, 0))
    gu_land = pl.BlockSpec((None, 2, rows, FF_PAD), lambda i, me: (me[0], 0, i, 0))
    down_shard = pl.BlockSpec((FF_PAD, D_MODEL), lambda i, me: (0, 0))
    down_land = pl.BlockSpec((None, FF_PAD, D_MODEL), lambda i, me: (me[0], 0, 0))

    def qkv_shard(width):
        return pl.BlockSpec((rows, width), lambda i, me: (i, 0))

    def qkv_land(width):
        return pl.BlockSpec((rows, width), lambda i, me: (i, me[0]))

    shard_specs = [qkv_shard(na_cols), o_shard, gu_shard, down_shard, qkv_shard(dil_cols), o_shard, gu_shard, down_shard]
    land_specs = [qkv_land(na_cols), o_land, gu_land, down_land, qkv_land(dil_cols), o_land, gu_land, down_land]
    shard_shapes = [jax.ShapeDtypeStruct(s, BF16) for s in
                    ((D_MODEL, na_cols), (o_rows, D_MODEL), (2, D_MODEL, FF_PAD), (FF_PAD, D_MODEL),
                     (D_MODEL, dil_cols), (o_rows, D_MODEL), (2, D_MODEL, FF_PAD), (FF_PAD, D_MODEL))]
    result = pl.pallas_call(
        body, out_shape=tuple(shard_shapes + list(land_shapes)),
        grid_spec=pltpu.PrefetchScalarGridSpec(num_scalar_prefetch=1, grid=(tiles,), in_specs=in_specs,
                                               out_specs=tuple(shard_specs + land_specs)),
        name="prep_weights", compiler_params=_params())(me, na_qkv, na_o, dil_qkv, dil_o, gate, up, down, gate, up, down)
    return list(result[:8]), list(result[8:])


def _remote_copies(sets, src_refs, land_refs, send_sems, recv_sems, outgoing):
    me = _position()
    copies = []
    for t, (si, src_of, li, dst_of) in enumerate(sets):
        for k in range(1, N_DEV):
            other = _peer(me, k)
            sender = me if outgoing else other
            copies.append(pltpu.make_async_remote_copy(
                src_ref=src_of(src_refs[si], _flat(other)), dst_ref=dst_of(land_refs[li], _flat(sender), k),
                send_sem=send_sems.at[(N_DEV - 1) * t + k - 1], recv_sem=recv_sems.at[(N_DEV - 1) * t + k - 1],
                device_id=other, device_id_type=MESH))
    return copies


def _send_start(name, srcs, lands, sets_by_group):
    n_src, n_land, n_groups = len(srcs), len(lands), len(sets_by_group)

    def body(*refs):
        src_refs, land_refs = refs[:n_src], refs[n_src:n_src + n_land]
        outs = refs[n_src + n_land:]
        for g, sets in enumerate(sets_by_group):
            for cp in _remote_copies(sets, src_refs, land_refs, outs[2 * g], outs[2 * g + 1], True):
                cp.start()
        outs[-1][...] = jnp.zeros_like(outs[-1])

    sem_shapes = []
    for sets in sets_by_group:
        sem_shapes += [pltpu.SemaphoreType.DMA((len(sets) * (N_DEV - 1),))] * 2
    thru = [pltpu.HBM(a.shape, a.dtype) for a in list(srcs) + list(lands)]
    n_sem = len(sem_shapes)
    result = pl.pallas_call(
        body, out_shape=tuple(sem_shapes + thru + [jax.ShapeDtypeStruct((8, 128), F32)]),
        in_specs=[HBM] * (n_src + n_land),
        out_specs=tuple([SEM] * n_sem + [HBM] * (n_src + n_land) + [pl.BlockSpec(memory_space=pltpu.VMEM)]),
        input_output_aliases={i: n_sem + i for i in range(n_src + n_land)},
        compiler_params=pltpu.CompilerParams(has_side_effects=EFFECT), name=name,
    )(*[pltpu.with_memory_space_constraint(a, pltpu.HBM) for a in list(srcs) + list(lands)])
    sems = [(result[2 * g], result[2 * g + 1]) for g in range(n_groups)]
    return sems, list(result[n_sem:n_sem + n_src]), list(result[n_sem + n_src:n_sem + n_src + n_land]), result[-1]


def _send_wait(name, sems, srcs, lands, sets, after):
    n_src, n_land = len(srcs), len(lands)

    def body(*refs):
        src_refs, land_refs = refs[:n_src], refs[n_src:n_src + n_land]
        send_sems, recv_sems = refs[n_src + n_land], refs[n_src + n_land + 1]
        for cp in _remote_copies(sets, src_refs, land_refs, send_sems, recv_sems, True):
            cp.wait_send()
        for cp in _remote_copies(sets, src_refs, land_refs, send_sems, recv_sems, False):
            cp.wait_recv()

    thru = [pltpu.HBM(a.shape, a.dtype) for a in list(srcs) + list(lands)]
    result = pl.pallas_call(
        body, out_shape=tuple(thru), in_specs=[HBM] * (n_src + n_land) + [SEM, SEM] + [ANY] * len(after),
        out_specs=tuple([HBM] * (n_src + n_land)), input_output_aliases={i: i for i in range(n_src + n_land)},
        compiler_params=pltpu.CompilerParams(has_side_effects=EFFECT), name=name,
    )(*srcs, *lands, sems[0], sems[1], *after)
    return list(result[:n_src]), list(result[n_src:])


DIRECT = (1, 2, 4, 6)
PASSED = DIRECT[1:]


def _hbm_passthrough(body, name, arrays, n_sem_in, sem_out_shapes, extra):
    n, n_out = len(arrays), len(sem_out_shapes)
    return pl.pallas_call(
        body, out_shape=tuple(list(sem_out_shapes) + [pltpu.HBM(a.shape, a.dtype) for a in arrays]),
        in_specs=[HBM] * n + [SEM] * n_sem_in + [ANY] * len(extra), out_specs=tuple([SEM] * n_out + [HBM] * n),
        input_output_aliases={i: n_out + i for i in range(n)},
        compiler_params=pltpu.CompilerParams(has_side_effects=EFFECT), name=name)


def _shard_copy(src_ref, land_ref, window, block, to, send_sem, recv_sem, from_shard):
    dst = window(land_ref, _flat(block))
    return pltpu.make_async_remote_copy(src_ref=src_ref if from_shard else dst, dst_ref=dst, send_sem=send_sem,
                                        recv_sem=recv_sem, device_id=to, device_id_type=MESH)


def _gather_start(name, shards, lands, windows, group_sizes):
    n = len(shards)

    def body(*refs):
        shard_refs, land_refs, outs = refs[:n], refs[n:2 * n], refs[2 * n:]
        me = _position()
        t = 0
        for g, size in enumerate(group_sizes):
            for local in range(size):
                for j, k in enumerate(DIRECT):
                    i = len(DIRECT) * local + j
                    _shard_copy(shard_refs[t], land_refs[t], windows[t], me, _peer(me, k), outs[2 * g].at[i],
                                outs[2 * g + 1].at[i], True).start()
                t += 1

    sem_shapes = [pltpu.SemaphoreType.DMA((len(DIRECT) * size,)) for size in group_sizes for _ in range(2)]
    arrays = [pltpu.with_memory_space_constraint(a, pltpu.HBM) for a in list(shards) + list(lands)]
    result = _hbm_passthrough(body, name, arrays, 0, sem_shapes, ())(*arrays)
    n_sem = len(sem_shapes)
    sems = [(result[2 * g], result[2 * g + 1]) for g in range(len(group_sizes))]
    return sems, list(result[n_sem:n_sem + n]), list(result[n_sem + n:])


def _gather_pass_on(name, sems, shards, lands, windows, after):
    n = len(shards)

    def body(*refs):
        shard_refs, land_refs = refs[:n], refs[n:2 * n]
        recv_sems = refs[2 * n + 1]
        pass_send, pass_recv = refs[2 * n + 2 + len(after)], refs[2 * n + 3 + len(after)]
        me = _position()
        sibling = _peer(me, 1)
        for t in range(n):
            for j, k in enumerate(PASSED):
                sender = _peer(me, k)
                arrived = len(DIRECT) * t + 1 + j
                _shard_copy(shard_refs[t], land_refs[t], windows[t], sender, me, refs[2 * n].at[arrived], recv_sems.at[arrived],
                            True).wait_recv()
                i = len(PASSED) * t + j
                _shard_copy(shard_refs[t], land_refs[t], windows[t], sender, sibling, pass_send.at[i], pass_recv.at[i],
                            False).start()

    sem_shapes = [pltpu.SemaphoreType.DMA((len(PASSED) * n,))] * 2
    result = _hbm_passthrough(body, name, list(shards) + list(lands), 2, sem_shapes, after)(
        *shards, *lands, sems[0], sems[1], *after)
    return (result[0], result[1]), list(result[2:2 + n]), list(result[2 + n:])


def _gather_wait(name, sems, pass_sems, shards, lands, windows, after):
    n = len(shards)

    def body(*refs):
        shard_refs, land_refs = refs[:n], refs[n:2 * n]
        send_sems, recv_sems, pass_send, pass_recv = refs[2 * n:2 * n + 4]
        me = _position()
        sibling = _peer(me, 1)
        for t in range(n):
            for j, k in enumerate(DIRECT):
                i = len(DIRECT) * t + j
                _shard_copy(shard_refs[t], land_refs[t], windows[t], me, _peer(me, k), send_sems.at[i], recv_sems.at[i],
                            True).wait_send()
            _shard_copy(shard_refs[t], land_refs[t], windows[t], sibling, me, send_sems.at[len(DIRECT) * t],
                        recv_sems.at[len(DIRECT) * t], True).wait_recv()
            for j, k in enumerate(PASSED):
                i = len(PASSED) * t + j
                _shard_copy(shard_refs[t], land_refs[t], windows[t], _peer(me, k), sibling, pass_send.at[i], pass_recv.at[i],
                            False).wait_send()
                _shard_copy(shard_refs[t], land_refs[t], windows[t], _peer(sibling, k), me, pass_send.at[i], pass_recv.at[i],
                            False).wait_recv()

    result = _hbm_passthrough(body, name, list(shards) + list(lands), 4, [], after)(
        *shards, *lands, sems[0], sems[1], pass_sems[0], pass_sems[1], *after)
    return list(result[n:])


def _all_gather(name, locals_, out_shapes, windows, deps=()):
    n = len(locals_)

    def body(*refs):
        src_refs, out_refs = refs[:n], refs[n + len(deps):2 * n + len(deps)]
        send_sems, recv_sems, local_sems = refs[2 * n + len(deps):]
        x, y, c = _position()
        me, sibling = (x, y, c), (x, y, 1 - c)
        chips = [(1 - x, y), (x, 1 - y), (1 - x, 1 - y)]

        def copy(t, k, block, to, from_local=False):
            dst = windows[t](out_refs[t], _flat(block))
            return pltpu.make_async_remote_copy(
                src_ref=src_refs[t] if from_local else dst, dst_ref=dst, send_sem=send_sems.at[t, k],
                recv_sem=recv_sems.at[t, k], device_id=to, device_id_type=MESH)

        mine = [pltpu.make_async_copy(src_refs[t], windows[t](out_refs[t], _flat(me)), local_sems.at[t]) for t in range(n)]
        sends = []
        for t in range(n):
            mine[t].start()
            sends.append(copy(t, 0, me, sibling, True))
            sends += [copy(t, 1 + j, me, (*chip, c), True) for j, chip in enumerate(chips)]
        for cp in sends:
            cp.start()
        for t in range(n):
            for j, chip in enumerate(chips):
                copy(t, 1 + j, (*chip, c), me).wait_recv()
                passed = copy(t, 4 + j, (*chip, c), sibling)
                passed.start()
                sends.append(passed)
        for t in range(n):
            copy(t, 0, sibling, me).wait_recv()
            for j, chip in enumerate(chips):
                copy(t, 4 + j, (*chip, 1 - c), me).wait_recv()
        for cp in sends:
            cp.wait_send()
        for cp in mine:
            cp.wait()

    return pl.pallas_call(
        body, out_shape=tuple(out_shapes), in_specs=[ANY] * (n + len(deps)), out_specs=tuple([ANY] * n),
        scratch_shapes=[pltpu.SemaphoreType.DMA((n, 7)), pltpu.SemaphoreType.DMA((n, 7)), pltpu.SemaphoreType.DMA((n,))],
        name=name)(*locals_, *deps)


def _adamw(name, me, lands, owns, w, m, v, *, grid, land_specs, own_specs, p_spec):
    n_land = len(lands)

    def body(me_ref, *refs):
        land_refs, own_refs = refs[:n_land], refs[n_land:n_land + len(owns)]
        w_ref, m_ref, v_ref, g_ref, delta_ref, m_out, v_out = refs[n_land + len(owns):]
        ncols = w_ref.shape[-1]
        sums = []
        for i, land_ref in enumerate(land_refs):
            g = own_refs[i][...].astype(F32) if owns else land_ref[0].astype(F32)
            for s in range(0 if owns else 1, land_ref.shape[0]):
                g = g + land_ref[s].astype(F32)
            sums.append(g[:, :ncols])
        g = sums[0] if n_land == 1 else jnp.where(pl.program_id(0) == 0, sums[0], sums[1])
        m_new = ADAM_B1 * m_ref[...] + (1.0 - ADAM_B1) * g
        v_new = ADAM_B2 * v_ref[...] + (1.0 - ADAM_B2) * jnp.square(g)
        m_hat = m_new / (1.0 - ADAM_B1 ** ADAM_STEP)
        v_hat = v_new / (1.0 - ADAM_B2 ** ADAM_STEP)
        g_ref[...] = g
        delta_ref[...] = -ADAM_LR * (m_hat / (jnp.sqrt(v_hat) + ADAM_EPS) + ADAM_WD * w_ref[...])
        m_out[...] = m_new
        v_out[...] = v_new

    shape = jax.ShapeDtypeStruct(w.shape, F32)
    return pl.pallas_call(
        body, out_shape=(shape,) * 4,
        grid_spec=pltpu.PrefetchScalarGridSpec(
            num_scalar_prefetch=1, grid=grid, in_specs=list(land_specs) + list(own_specs) + [p_spec, p_spec, p_spec],
            out_specs=(p_spec,) * 4),
        name=name, compiler_params=_params())(me, *lands, *owns, w, m, v)


def _row(p, layer):
    return p[layer][None, :]


def _square(name, a, b, dims, out_dtype, deps=()):
    if a.shape == (D_MODEL, SEQ):
        return _matmul(name, a, b, grid=(2, 1), a_spec=pl.BlockSpec((512, SEQ), lambda i, k: (i, 0)),
                       b_spec=pl.BlockSpec((SEQ, D_MODEL), lambda i, k: (0, 0)),
                       o_spec=pl.BlockSpec((512, D_MODEL), lambda i, k: (i, 0)),
                       out_shape=jax.ShapeDtypeStruct((D_MODEL, D_MODEL), out_dtype), dims=NN, acc_shape=(8, 128),
                       deps=deps)
    return _matmul(name, a, b, grid=(SEQ // TM_MM, 1), a_spec=pl.BlockSpec((TM_MM, D_MODEL), lambda i, k: (i, 0)),
                   b_spec=pl.BlockSpec((D_MODEL, D_MODEL), lambda i, k: (0, 0)),
                   o_spec=pl.BlockSpec((TM_MM, D_MODEL), lambda i, k: (i, 0)),
                   out_shape=jax.ShapeDtypeStruct((SEQ, D_MODEL), out_dtype), dims=dims, acc_shape=(8, 128), deps=deps)


def _grouped_matmul(name, a_list, b, *, n_tiles, a_block, b_spec, o_spec, out_shape):
    n_groups = len(a_list)

    def a_spec(g):
        def index(j, i):
            mine = j // 3
            return (jnp.where(mine == g, i, jnp.where(mine < g, 0, n_tiles - 1)), 0)
        return pl.BlockSpec(a_block, index)

    def body(*refs):
        b_ref, o_ref = refs[n_groups], refs[n_groups + 1]
        mine = pl.program_id(0) // 3
        for g in range(n_groups):
            @pl.when(mine == g)
            def _(g=g):
                o_ref[...] = jnp.dot(refs[g][...], b_ref[...], preferred_element_type=F32).astype(o_ref.dtype)

    return pl.pallas_call(
        body, out_shape=out_shape, grid=(3 * n_groups, n_tiles), in_specs=[a_spec(g) for g in range(n_groups)] + [b_spec],
        out_specs=o_spec, name=name, compiler_params=_params())(*a_list, b)


def _qkv_fwd(name, hs, w):
    return _grouped_matmul(name, hs, w, n_tiles=SEQ // TM_MM, a_block=(TM_MM, D_MODEL),
                           b_spec=pl.BlockSpec((D_MODEL, D_MODEL), lambda j, i: (0, j)),
                           o_spec=pl.BlockSpec((TM_MM, D_MODEL), lambda j, i: (i, j)),
                           out_shape=jax.ShapeDtypeStruct((SEQ, 3 * len(hs) * D_MODEL), BF16))


def _qkv_dw(name, hs_t, dqkv):
    return _grouped_matmul(name, hs_t, dqkv, n_tiles=2, a_block=(512, SEQ),
                           b_spec=pl.BlockSpec((None, SEQ, D_MODEL), lambda j, i: (j, 0, 0)),
                           o_spec=pl.BlockSpec((512, D_MODEL), lambda j, i: (i, j)),
                           out_shape=jax.ShapeDtypeStruct((D_MODEL, 3 * len(hs_t) * D_MODEL), BF16))


def _proj_do_sorted(name, d_a, w_o):
    def body(da_ref, w_ref, *refs):
        value = lax.dot_general(da_ref[...], w_ref[...], NT, preferred_element_type=F32)
        refs[0][...] = value
        for j, dil in enumerate(SORTED):
            _sort_tile(refs[-1], value, dil, refs[1 + j])

    tile = pl.BlockSpec((TM, D_MODEL), lambda i: (i, 0))
    shapes = [jax.ShapeDtypeStruct((SEQ, D_MODEL), F32)] + [jax.ShapeDtypeStruct((dil, SEQ // dil, D_MODEL), F32) for dil in SORTED]
    result = pl.pallas_call(
        body, out_shape=tuple(shapes), grid=(SEQ // TM,),
        in_specs=[tile, pl.BlockSpec((D_MODEL, D_MODEL), lambda i: (0, 0))],
        out_specs=tuple([tile] + [_sorted_spec(dil) for dil in SORTED]), scratch_shapes=[_sort_scratch()],
        name=name, compiler_params=_params())(d_a, w_o)
    return [t.reshape(SEQ, D_MODEL) for t in result]


def _qkv_dh(name, dqkv, w, n_chunks, deps):
    tm = TM_MM // 2
    return _matmul(name, dqkv, w, grid=(n_chunks // 3, SEQ // tm, 1),
                   a_spec=pl.BlockSpec((3, tm, D_MODEL), lambda g, i, k: (g, i, 0)),
                   b_spec=pl.BlockSpec((D_MODEL, 3 * D_MODEL), lambda g, i, k: (0, g)),
                   o_spec=pl.BlockSpec((None, tm, D_MODEL), lambda g, i, k: (g, i, 0)),
                   out_shape=jax.ShapeDtypeStruct((n_chunks // 3, SEQ, D_MODEL), F32), dims=NT, acc_shape=(8, 128),
                   deps=deps, inner=3)


def _local_step(x, target, norms, rpb, fetch, emit, deps):
    mix_pre, mix_post, ffn_pre, ffn_post = norms
    slopes = 2.0 ** (-8.0 * jnp.arange(1, N_HEADS + 1, dtype=F32) / N_HEADS)
    rpb_pad = jnp.pad(rpb, ((0, 0), (0, 1), (0, 128 - 31)))
    saved = []

    hs = [_rms_fwd("l0_norm_mix", x, _row(mix_pre, 0), out_dtype=BF16, deps=deps)]
    hs_t0 = None
    for layer in range(2):
        tag = f"l{layer}"
        if layer == 0:
            table = _rpb_table(rpb_pad)
            w_qkv, w_o = fetch("na", [table, hs[0]], [hs[0]])
            qkv = _qkv_fwd(tag + "_qkv", hs, w_qkv)
            o, lse = _na_fwd(qkv, table)
            mixer = (hs, qkv, o, lse, table)
        else:
            w_qkv, w_o = fetch("dil", [saved[0][7]], [hs[0]])
            qkv = _qkv_fwd(tag + "_qkv", hs, w_qkv)
            outs, lses = zip(*[_dil_fwd(g, qkv, slopes * dil) for g, (_, dil) in enumerate(DIL_GROUPS)])
            merged, lse_total = _dil_merge(outs, lses)
            o = merged[0]
            mixer = (hs, qkv, merged, lses, lse_total)
        a = _square(tag + "_proj", o, w_o, NN, F32)
        x1, (h2, h2_t) = _post_norm_fwd(tag + "_post_mix", a, _row(mix_post, layer), x, _row(ffn_pre, layer))
        w_gu, w_down = fetch(f"ffn{layer}", [a], [h2])
        f, hg, hu, act_t = _ffn_fwd(tag + "_ffn", h2, w_gu, w_down)
        transposed = ([hs_t0 if hs_t0 is not None and t is hs[0] else t.T for t in hs], o.astype(BF16).T, h2_t, act_t)
        saved.append((x, mixer, a, x1, transposed, hg, hu, f, w_qkv, w_o, w_gu, w_down))
        if layer == 0:
            x, (h, hs_t0, *views) = _post_norm_fwd(tag + "_post_ffn", f, _row(ffn_post, 0), x1, _row(mix_pre, 1), sorted_too=True)
            hs = [h] + [t.reshape(SEQ, D_MODEL) for t in views]
        else:
            x = _rms_fwd(tag + "_post_ffn", f, _row(ffn_post, layer), res=x1)

    dx, loss = _loss_head("loss_head", x, target)
    d_norm = {k: [None, None] for k in ("mix_pre", "mix_post", "ffn_pre", "ffn_post")}
    d_rpb = None

    d_f, d_norm["ffn_post"][1] = _rms_bwd("b1_post_ffn", saved[1][7], _row(ffn_post, 1), [dx], out_dtype=BF16)
    for layer in (1, 0):
        tag = f"b{layer}"
        x0, mixer, a, x1, (h_t, o_t, h2_t, act_t), hg, hu, f, w_qkv, w_o, w_gu, w_down = saved[layer]
        dgu, d_h2, d_down = _ffn_bwd(tag + "_ffn", d_f, w_gu, w_down, hg, hu, act_t)
        d_gu = _ffn_dgu(tag + "_ffn_dgu", h2_t, dgu)
        sent = emit(f"ffn{layer}", [d_gu, d_down])
        dx1, d_a, d_norm["ffn_pre"][layer], d_norm["mix_post"][layer] = _norm_post_bwd(
            tag + "_norm_ffn", x1, _row(ffn_pre, layer), [d_h2], dx, a, _row(mix_post, layer), deps=sent)
        d_wo = _square(tag + "_proj_dw", o_t, d_a, NN, BF16)
        if layer == 0:
            _, qkv, o, lse, table = mixer
            d_o = _square(tag + "_proj_do", d_a, w_o, NT, BF16)
            dqkv, gp = _na_bwd(qkv, table, d_o, lse)
            d_rpb = _rpb_grad(gp)[:, :15, :31]
            sent = emit("na", [_qkv_dw(tag + "_qkv_dw", h_t, dqkv), d_wo])
            d_h = _qkv_dh(tag + "_qkv_dh", dqkv, w_qkv, 3, sent)
            dx, d_norm["mix_pre"][layer] = _rms_bwd(tag + "_norm_mix", x0, _row(mix_pre, layer), [d_h[0]], res=dx1)
        else:
            _, qkv, merged, lses, lse_total = mixer
            d_o = _proj_do_sorted(tag + "_proj_do", d_a, w_o)
            dqkv = lax.empty((3 * len(DIL_GROUPS), SEQ, D_MODEL), BF16)
            for g, (_, dil) in enumerate(DIL_GROUPS):
                dqkv = _dil_bwd(g, qkv, slopes * dil, d_o[g], merged[g], lses[g], lse_total[g], dqkv)
            sent = emit("dil", [_qkv_dw(tag + "_qkv_dw", h_t, dqkv), d_wo])
            d_h = _qkv_dh(tag + "_qkv_dh", dqkv, w_qkv, 9, sent)
            dx, d_f, d_norm["mix_pre"][1], d_norm["ffn_post"][0] = _norm_post_bwd(
                tag + "_norm_mix", x0, _row(mix_pre, 1), None, dx1, saved[0][7], _row(ffn_post, 0), groups=d_h)

    d_gains = [jnp.concatenate(d_norm[k], axis=0) for k in ("mix_pre", "mix_post", "ffn_pre", "ffn_post")]
    return loss, dx, d_gains, d_rpb


RPB_SIZE = N_HEADS * 15 * 31


def _pack_small(gains, rpb, last=None):
    top = jnp.concatenate(gains, axis=0).reshape(64, 128)
    bottom = jnp.pad(rpb.reshape(-1), (0, 64 * 128 - RPB_SIZE))
    if last is not None:
        bottom = bottom + jnp.pad(last.reshape(1), (64 * 128 - 1, 0))
    return jnp.concatenate([top, bottom.reshape(64, 128)], axis=0)


def _unpack_small(p):
    gains = p[:64].reshape(4, 2, D_MODEL)
    rpb = p[64:].reshape(-1)[:RPB_SIZE].reshape(1, N_HEADS, 15, 31)
    return [gains[i] for i in range(4)], rpb


GROUPS = ("na", "ffn0", "dil", "ffn1")


def kernel(x, norm_mix_pre, norm_mix_post, norm_ffn_pre, norm_ffn_post, na_w_qkv, na_w_o, na_rpb, dil_w_qkv, dil_w_o, ffn_w_gate, ffn_w_up, ffn_w_down, loss_target, m_norm_mix_pre, m_norm_mix_post, m_norm_ffn_pre, m_norm_ffn_post, m_na_w_qkv, m_na_w_o, m_na_rpb, m_dil_w_qkv, m_dil_w_o, m_ffn_w_gate, m_ffn_w_up, m_ffn_w_down, v_norm_mix_pre, v_norm_mix_post, v_norm_ffn_pre, v_norm_ffn_post, v_na_w_qkv, v_na_w_o, v_na_rpb, v_dil_w_qkv, v_dil_w_o, v_ffn_w_gate, v_ffn_w_up, v_ffn_w_down):
    na_cols, dil_cols, o_rows = 3 * D_MODEL // N_DEV, 9 * D_MODEL // N_DEV, D_MODEL // N_DEV
    ff_pad = FF_PAD - FF_SHARD
    me = (4 * lax.axis_index("x") + 2 * lax.axis_index("y") + lax.axis_index("c")).astype(jnp.int32).reshape(1)

    full = {
        "na": [((D_MODEL, 3 * D_MODEL), _columns(na_cols)), ((N_DEV, o_rows, D_MODEL), _leading)],
        "dil": [((D_MODEL, 9 * D_MODEL), _columns(dil_cols)), ((N_DEV, o_rows, D_MODEL), _leading)],
        "ffn0": [((N_DEV, 2, D_MODEL, FF_PAD), _leading), ((N_DEV, FF_PAD, D_MODEL), _leading)],
        "ffn1": [((N_DEV, 2, D_MODEL, FF_PAD), _leading), ((N_DEV, FF_PAD, D_MODEL), _leading)],
    }
    block = {
        "na": [(D_MODEL, na_cols), (o_rows, D_MODEL)], "dil": [(D_MODEL, dil_cols), (o_rows, D_MODEL)],
        "ffn0": [(2, D_MODEL, FF_PAD), (FF_PAD, D_MODEL)], "ffn1": [(2, D_MODEL, FF_PAD), (FF_PAD, D_MODEL)],
    }

    land_shapes = [jax.ShapeDtypeStruct(full[g][t][0], BF16) for g in GROUPS for t in range(2)]
    windows = [full[g][t][1] for g in GROUPS for t in range(2)]
    shards, lands = _prep_weights(me, na_w_qkv, na_w_o, dil_w_qkv, dil_w_o, ffn_w_gate, ffn_w_up, ffn_w_down, land_shapes)
    sems, shards, lands = _gather_start("gather_start", shards, lands, windows, [2] * len(GROUPS))

    def fetch(group, early, late):
        gi = GROUPS.index(group)
        mine = slice(2 * gi, 2 * gi + 2)
        pass_sems, shards_g, lands_g = _gather_pass_on(f"gather_pass_{group}", sems[gi], shards[mine], lands[mine],
                                                       windows[mine], early)
        qkv, o = _gather_wait(f"gather_wait_{group}", sems[gi], pass_sems, shards_g, lands_g, windows[mine], late)
        return (qkv, o.reshape(D_MODEL, D_MODEL)) if group in ("na", "dil") else (qkv, o)

    def grad_source(group, t):
        return _columns(block[group][0][1]) if (group in ("na", "dil") and t == 0) else _leading

    in_flight = {}

    def emit(group, grads):
        if group in ("na", "dil"):
            grads = [grads[0], grads[1].reshape(N_DEV, o_rows, D_MODEL)]
        sets = [(t, grad_source(group, t), t, _by_distance) for t in range(2)]
        landing = [lax.empty((N_DEV - 1,) + block[group][t], BF16) for t in range(2)]
        sems_g, grads, landing, tok = _send_start(f"exchange_start_{group}", grads, landing, [sets])
        in_flight[group] = (sems_g[0], grads, landing, sets)
        return [tok]

    norms = (norm_mix_pre, norm_mix_post, norm_ffn_pre, norm_ffn_post)
    loss, grad_x, d_gains, d_rpb = _local_step(x[0], loss_target[0], norms, na_rpb[0], fetch, emit, [shards[0]])

    landed, sent = {}, {}

    def wait_for(group, after):
        sems_g, grads, landing, sets = in_flight[group]
        sent[group], landed[group] = _send_wait(f"exchange_wait_{group}", sems_g, grads, landing, sets, after)

    for group in ("ffn1", "dil", "ffn0"):
        wait_for(group, [grad_x])

    def one(rows, tile, ncols, columns):
        own = (pl.BlockSpec((tile, ncols), lambda i, me: (i, me[0])) if columns
               else pl.BlockSpec((None, tile, ncols), lambda i, me: (me[0], i, 0)))
        return dict(grid=(rows // tile,), land_specs=[pl.BlockSpec((N_DEV - 1, tile, ncols), lambda i, me: (0, i, 0))],
                    own_specs=[own], p_spec=pl.BlockSpec((None, tile, ncols), lambda i, me: (0, i, 0)))

    def layered(block_shape, index, p_block, n_tiles):
        def specs(lead_size, lead):
            shape = (lead_size,) + block_shape
            return [pl.BlockSpec(shape, lambda l, r, me: index(lead(me), jnp.where(l == 0, r, n_tiles - 1))),
                    pl.BlockSpec(shape, lambda l, r, me: index(lead(me), jnp.where(l == 0, 0, r)))]
        return dict(grid=(2, n_tiles), land_specs=specs(N_DEV - 1, lambda me: 0), own_specs=specs(None, lambda me: me[0]),
                    p_spec=pl.BlockSpec(p_block, lambda l, r, me: (l, r, 0)))

    gu_lands, gu_owns = [landed["ffn0"][0], landed["ffn1"][0]], [sent["ffn0"][0], sent["ffn1"][0]]
    down_lands, down_owns = [landed["ffn0"][1], landed["ffn1"][1]], [sent["ffn0"][1], sent["ffn1"][1]]
    updates = {
        "dil_w_qkv": _adamw("adamw_dil_qkv", me, [landed["dil"][0]], [sent["dil"][0]], dil_w_qkv, m_dil_w_qkv, v_dil_w_qkv,
                            **one(D_MODEL, 128, dil_cols, True)),
        "dil_w_o": _adamw("adamw_dil_o", me, [landed["dil"][1]], [sent["dil"][1]], dil_w_o, m_dil_w_o, v_dil_w_o,
                          **one(o_rows, o_rows, D_MODEL, False)),
        "ffn_w_gate": _adamw("adamw_gate", me, gu_lands, gu_owns, ffn_w_gate, m_ffn_w_gate, v_ffn_w_gate,
                             **layered((None, 128, FF_PAD), lambda lead, r: (lead, 0, r, 0), (None, 128, FF_SHARD), 8)),
        "ffn_w_up": _adamw("adamw_up", me, gu_lands, gu_owns, ffn_w_up, m_ffn_w_up, v_ffn_w_up,
                           **layered((None, 128, FF_PAD), lambda lead, r: (lead, 1, r, 0), (None, 128, FF_SHARD), 8)),
        "ffn_w_down": _adamw("adamw_down", me, down_lands, down_owns, ffn_w_down, m_ffn_w_down, v_ffn_w_down,
                             **layered((176, D_MODEL), lambda lead, r: (lead, r, 0), (None, 176, D_MODEL), 2)),
    }
    done = [u[0] for u in updates.values()]
    small = _all_gather("gather_small", [_pack_small(d_gains, d_rpb, loss)], [jax.ShapeDtypeStruct((N_DEV, 128, 128), F32)],
                        [_leading], deps=done)[0]
    wait_for("na", [small])
    updates["na_w_qkv"] = _adamw("adamw_na_qkv", me, [landed["na"][0]], [sent["na"][0]], na_w_qkv, m_na_w_qkv, v_na_w_qkv,
                                 **one(D_MODEL, 256, na_cols, True))
    updates["na_w_o"] = _adamw("adamw_na_o", me, [landed["na"][1]], [sent["na"][1]], na_w_o, m_na_w_o, v_na_w_o,
                               **one(o_rows, o_rows, D_MODEL, False))
    gains = [norm_mix_pre, norm_mix_post, norm_ffn_pre, norm_ffn_post]
    m_gains = [m_norm_mix_pre, m_norm_mix_post, m_norm_ffn_pre, m_norm_ffn_post]
    v_gains = [v_norm_mix_pre, v_norm_mix_post, v_norm_ffn_pre, v_norm_ffn_post]
    packed = _adamw("adamw_small", me, [small], (), _pack_small(gains, na_rpb)[None], _pack_small(m_gains, m_na_rpb)[None],
                    _pack_small(v_gains, v_na_rpb)[None], grid=(1,),
                    land_specs=[pl.BlockSpec((N_DEV, 128, 128), lambda i, me: (0, 0, 0))], own_specs=[],
                    p_spec=pl.BlockSpec((None, 128, 128), lambda i, me: (0, 0, 0)))
    small_out = [_unpack_small(p[0]) for p in packed]

    order = ["na_w_qkv", "na_w_o", "na_rpb", "dil_w_qkv", "dil_w_o", "ffn_w_gate", "ffn_w_up", "ffn_w_down"]
    result = [packed[0][0, 127, 127], grad_x[None]]
    for kind in range(4):
        gains_k, rpb_k = small_out[kind]
        result += gains_k
        result += [rpb_k if name == "na_rpb" else updates[name][kind] for name in order]
    return tuple(result)
```

```python
import jax
import jax.numpy as jnp
from jax import lax
from jax.experimental import pallas as pl
from jax.experimental.pallas import tpu as pltpu

F32 = jnp.float32
BF16 = jnp.bfloat16
MESH = pl.DeviceIdType.MESH
ANY = pl.BlockSpec(memory_space=pl.ANY)
HBM = pl.BlockSpec(memory_space=pltpu.HBM)
SEM = pl.BlockSpec(memory_space=pltpu.SEMAPHORE)
EFFECT = pltpu.SideEffectType.DATAFLOW_SIDE_EFFECTING

N_DEV = 8
SEQ = 2048
D_MODEL = 1024
N_HEADS = 16
HEAD_DIM = 64
GRID_W = 64
NA_ROWS = 8
SEQ_ROWS = SEQ // GRID_W
DIL_GROUPS = ((128, 1), (512, 4), (2048, 16))
BAND = 128
RADIUS = 64
FF_SHARD = 352
FF_PAD = 384
RMS_EPS = 1e-6
NEG_INF = -1e30
Q_SCALE = HEAD_DIM ** -0.5

ADAM_LR = 0.001
ADAM_B1 = 0.9
ADAM_B2 = 0.999
ADAM_EPS = 1e-08
ADAM_WD = 0.01
ADAM_STEP = 10

VMEM_LIMIT = 56 * 1024 * 1024
TM = 512
TM_ROW = 1024
TM_MM = 1024

NN = (((1,), (0,)), ((), ()))
NT = (((1,), (1,)), ((), ()))
TN = (((0,), (0,)), ((), ()))


def _params():
    return pltpu.CompilerParams(vmem_limit_bytes=VMEM_LIMIT)


def _matmul(name, a, b, *, grid, a_spec, b_spec, o_spec, out_shape, dims, acc_shape, deps=(), inner=1):
    nk = grid[-1]
    kaxis = len(grid) - 1

    def body(a_ref, b_ref, *rest):
        o_ref, acc_ref = rest[-2], rest[-1]
        if inner == 1:
            a_all = a_ref[...].astype(BF16)
        else:
            a_all = jnp.concatenate([a_ref[j].astype(BF16) for j in range(inner)], axis=1)
        part = lax.dot_general(a_all, b_ref[...].astype(BF16), dims, preferred_element_type=F32)
        if nk == 1:
            o_ref[...] = part.astype(o_ref.dtype)
        else:
            k = pl.program_id(kaxis)

            @pl.when(k == 0)
            def _():
                acc_ref[...] = part

            @pl.when(k > 0)
            def _():
                acc_ref[...] += part

            @pl.when(k == nk - 1)
            def _():
                o_ref[...] = acc_ref[...].astype(o_ref.dtype)

    return pl.pallas_call(
        body, out_shape=out_shape, grid=grid, in_specs=[a_spec, b_spec] + [ANY] * len(deps), out_specs=o_spec,
        scratch_shapes=[pltpu.VMEM(acc_shape, F32)], name=name, compiler_params=_params())(a, b, *deps)


SORTED = tuple(d for _, d in DIL_GROUPS if d > 1)
LANE_CHUNKS = D_MODEL // 128


def _sort_scratch(tm=TM):
    return pltpu.VMEM((LANE_CHUNKS, tm, 128), F32)


def _sorted_view(t, dil):
    return t.reshape(dil, SEQ // dil, D_MODEL)


def _sorted_spec(dil, lead=(), tm=TM):
    return pl.BlockSpec((None,) * len(lead) + (dil, tm // dil, D_MODEL), lambda i: tuple(lead) + (0, i, 0))


def _sort_tile(scratch, value, dil, out_ref):
    tm = value.shape[0]
    for c in range(LANE_CHUNKS):
        scratch[c] = value[:, 128 * c:128 * (c + 1)]
    for r in range(dil):
        rows = [scratch.at[c][pl.ds(r, tm // dil, stride=dil), :] for c in range(LANE_CHUNKS)]
        out_ref[r] = jnp.concatenate(rows, axis=1).astype(out_ref.dtype)


def _unsort_tile(scratch, in_ref, dil):
    for r in range(dil):
        value = in_ref[r].astype(F32)
        for c in range(LANE_CHUNKS):
            scratch.at[c][pl.ds(r, value.shape[0], stride=dil), :] = value[:, 128 * c:128 * (c + 1)]
    return jnp.concatenate([scratch[c] for c in range(LANE_CHUNKS)], axis=1)


def _rms_fwd(name, x, g, res=None, out_dtype=F32, deps=(), sorted_too=False):
    tm = TM if sorted_too else TM_ROW
    n_tiles = SEQ // tm
    has_res = res is not None
    n_in = 2 + has_res + len(deps)

    def body(*refs):
        x_ref, g_ref = refs[0], refs[1]
        xv = x_ref[...]
        r = lax.rsqrt(jnp.mean(xv * xv, axis=-1, keepdims=True) + RMS_EPS)
        y = xv * r * g_ref[...]
        if has_res:
            y = refs[2][...] + y
        refs[n_in][...] = y.astype(out_dtype)
        if sorted_too:
            for j, dil in enumerate(SORTED):
                _sort_tile(refs[-1], y, dil, refs[n_in + 1 + j])

    tile = pl.BlockSpec((tm, D_MODEL), lambda i: (i, 0))
    gspec = pl.BlockSpec((1, D_MODEL), lambda i: (0, 0))
    ins = [x, g] + ([res] if has_res else []) + list(deps)
    specs = [tile, gspec] + ([tile] if has_res else []) + [ANY] * len(deps)
    shapes, out_specs = [jax.ShapeDtypeStruct((SEQ, D_MODEL), out_dtype)], [tile]
    if sorted_too:
        shapes += [jax.ShapeDtypeStruct((dil, SEQ // dil, D_MODEL), out_dtype) for dil in SORTED]
        out_specs += [_sorted_spec(dil) for dil in SORTED]
    result = pl.pallas_call(
        body, out_shape=tuple(shapes), grid=(n_tiles,), in_specs=specs, out_specs=tuple(out_specs),
        scratch_shapes=[_sort_scratch()] if sorted_too else [], name=name, compiler_params=_params())(*ins)
    return result if sorted_too else result[0]


def _rms_bwd(name, x, g, dys, res=None, out_dtype=F32, groups=None, deps=()):
    tm = TM if groups is not None else TM_ROW
    n_tiles = SEQ // tm
    n_dy = len(dys) if groups is None else 1 + len(SORTED)
    has_res = res is not None

    def body(*refs):
        x_ref, g_ref = refs[0], refs[1]
        dy_refs = refs[2:2 + n_dy]
        res_ref = refs[2 + n_dy] if has_res else None
        first_out = 2 + n_dy + has_res + len(deps)
        dx_ref, dg_ref, acc_ref = refs[first_out:first_out + 3]
        i = pl.program_id(0)
        xv = x_ref[...]
        r = lax.rsqrt(jnp.mean(xv * xv, axis=-1, keepdims=True) + RMS_EPS)
        xn = xv * r
        dy = dy_refs[0][...].astype(F32)
        for j, extra in enumerate(dy_refs[1:]):
            dy = dy + (extra[...].astype(F32) if groups is None else _unsort_tile(refs[-1], extra, SORTED[j]))
        dyg = dy * g_ref[...]
        dx = r * (dyg - xn * jnp.mean(dyg * xn, axis=-1, keepdims=True))
        if has_res:
            dx = res_ref[...] + dx
        dx_ref[...] = dx.astype(dx_ref.dtype)
        part = jnp.sum((dy * xn).reshape(tm // 8, 8, D_MODEL), axis=0)

        @pl.when(i == 0)
        def _():
            acc_ref[...] = part

        @pl.when(i > 0)
        def _():
            acc_ref[...] += part

        @pl.when(i == n_tiles - 1)
        def _():
            dg_ref[...] = jnp.broadcast_to(jnp.sum(acc_ref[...], axis=0, keepdims=True), (8, D_MODEL))

    tile = pl.BlockSpec((tm, D_MODEL), lambda i: (i, 0))
    gspec = pl.BlockSpec((1, D_MODEL), lambda i: (0, 0))
    if groups is None:
        dy_ins, dy_specs = list(dys), [tile] * n_dy
    else:
        dy_ins = [groups] + [groups.reshape(n_dy, dil, SEQ // dil, D_MODEL) for dil in SORTED]
        dy_specs = [pl.BlockSpec((None, tm, D_MODEL), lambda i: (0, i, 0))]
        dy_specs += [_sorted_spec(dil, lead=(1 + j,)) for j, dil in enumerate(SORTED)]
    ins = [x, g] + dy_ins + ([res] if has_res else []) + list(deps)
    specs = [tile, gspec] + dy_specs + ([tile] if has_res else []) + [ANY] * len(deps)
    dx, dg = pl.pallas_call(
        body, out_shape=(jax.ShapeDtypeStruct((SEQ, D_MODEL), out_dtype), jax.ShapeDtypeStruct((8, D_MODEL), F32)),
        grid=(n_tiles,), in_specs=specs,
        out_specs=(tile, pl.BlockSpec((8, D_MODEL), lambda i: (0, 0))),
        scratch_shapes=[pltpu.VMEM((8, D_MODEL), F32)] + ([_sort_scratch()] if groups is not None else []),
        name=name, compiler_params=_params())(*ins)
    return dx, dg[0:1]


def _rms(xv):
    return lax.rsqrt(jnp.mean(xv * xv, axis=-1, keepdims=True) + RMS_EPS)


def _post_norm_fwd(name, a, g_post, res, g_next, sorted_too=False):
    def body(a_ref, gp_ref, res_ref, gn_ref, x_ref, h_ref, ht_ref, *rest):
        av = a_ref[...]
        xv = res_ref[...] + av * _rms(av) * gp_ref[...]
        x_ref[...] = xv
        y = xv * _rms(xv) * gn_ref[...]
        h_ref[...] = y.astype(h_ref.dtype)
        ht_ref[...] = y.T.astype(ht_ref.dtype)
        if sorted_too:
            for j, dil in enumerate(SORTED):
                _sort_tile(rest[-1], y, dil, rest[j])

    tile = pl.BlockSpec((TM, D_MODEL), lambda i: (i, 0))
    gspec = pl.BlockSpec((1, D_MODEL), lambda i: (0, 0))
    shapes = [jax.ShapeDtypeStruct((SEQ, D_MODEL), F32), jax.ShapeDtypeStruct((SEQ, D_MODEL), BF16),
              jax.ShapeDtypeStruct((D_MODEL, SEQ), BF16)]
    out_specs = [tile, tile, pl.BlockSpec((D_MODEL, TM), lambda i: (0, i))]
    if sorted_too:
        shapes += [jax.ShapeDtypeStruct((dil, SEQ // dil, D_MODEL), BF16) for dil in SORTED]
        out_specs += [_sorted_spec(dil) for dil in SORTED]
    result = pl.pallas_call(
        body, out_shape=tuple(shapes), grid=(SEQ // TM,), in_specs=[tile, gspec, tile, gspec], out_specs=tuple(out_specs),
        scratch_shapes=[_sort_scratch()] if sorted_too else [], name=name, compiler_params=_params())(a, g_post, res, g_next)
    return result[0], list(result[1:])


def _norm_post_bwd(name, x, g_norm, dys, res, a, g_post, groups=None, deps=()):
    n_tiles = SEQ // TM
    n_dy = len(dys) if groups is None else 1 + len(SORTED)

    def body(*refs):
        x_ref, g_ref = refs[0], refs[1]
        dy_refs = refs[2:2 + n_dy]
        res_ref, a_ref, gp_ref = refs[2 + n_dy:5 + n_dy]
        dx_ref, da_ref, dg_ref, dgp_ref, acc_ref, accp_ref = refs[5 + n_dy + len(deps):11 + n_dy + len(deps)]
        i = pl.program_id(0)
        xv = x_ref[...]
        r = _rms(xv)
        xn = xv * r
        dy = dy_refs[0][...].astype(F32)
        for j, extra in enumerate(dy_refs[1:]):
            dy = dy + (extra[...].astype(F32) if groups is None else _unsort_tile(refs[-1], extra, SORTED[j]))
        dyg = dy * g_ref[...]
        dx = res_ref[...] + r * (dyg - xn * jnp.mean(dyg * xn, axis=-1, keepdims=True))
        dx_ref[...] = dx
        av = a_ref[...]
        ra = _rms(av)
        an = av * ra
        dxg = dx * gp_ref[...]
        da_ref[...] = (ra * (dxg - an * jnp.mean(dxg * an, axis=-1, keepdims=True))).astype(da_ref.dtype)
        part = jnp.sum((dy * xn).reshape(TM // 8, 8, D_MODEL), axis=0)
        part_p = jnp.sum((dx * an).reshape(TM // 8, 8, D_MODEL), axis=0)

        @pl.when(i == 0)
        def _():
            acc_ref[...] = part
            accp_ref[...] = part_p

        @pl.when(i > 0)
        def _():
            acc_ref[...] += part
            accp_ref[...] += part_p

        @pl.when(i == n_tiles - 1)
        def _():
            dg_ref[...] = jnp.broadcast_to(jnp.sum(acc_ref[...], axis=0, keepdims=True), (8, D_MODEL))
            dgp_ref[...] = jnp.broadcast_to(jnp.sum(accp_ref[...], axis=0, keepdims=True), (8, D_MODEL))

    tile = pl.BlockSpec((TM, D_MODEL), lambda i: (i, 0))
    gspec = pl.BlockSpec((1, D_MODEL), lambda i: (0, 0))
    gain = pl.BlockSpec((8, D_MODEL), lambda i: (0, 0))
    if groups is None:
        dy_ins, dy_specs = list(dys), [tile] * n_dy
    else:
        dy_ins = [groups] + [groups.reshape(n_dy, dil, SEQ // dil, D_MODEL) for dil in SORTED]
        dy_specs = [pl.BlockSpec((None, TM, D_MODEL), lambda i: (0, i, 0))]
        dy_specs += [_sorted_spec(dil, lead=(1 + j,)) for j, dil in enumerate(SORTED)]
    dx, da, dg, dgp = pl.pallas_call(
        body,
        out_shape=(jax.ShapeDtypeStruct((SEQ, D_MODEL), F32), jax.ShapeDtypeStruct((SEQ, D_MODEL), BF16),
                   jax.ShapeDtypeStruct((8, D_MODEL), F32), jax.ShapeDtypeStruct((8, D_MODEL), F32)),
        grid=(n_tiles,), in_specs=[tile, gspec] + dy_specs + [tile, tile, gspec] + [ANY] * len(deps),
        out_specs=(tile, tile, gain, gain),
        scratch_shapes=[pltpu.VMEM((8, D_MODEL), F32)] * 2 + ([_sort_scratch()] if groups is not None else []),
        name=name, compiler_params=_params())(x, g_norm, *dy_ins, res, a, g_post, *deps)
    return dx, da, dg[0:1], dgp[0:1]


def _loss_head(name, y, target):
    tm = TM_ROW
    n_tiles = SEQ // tm

    def body(y_ref, t_ref, dy_ref, loss_ref, acc_ref):
        i = pl.program_id(0)
        diff = y_ref[...] - t_ref[...]
        dy_ref[...] = diff * (1.0 / D_MODEL)
        part = jnp.sum((diff * diff).reshape(tm // 8, 8, D_MODEL), axis=0)

        @pl.when(i == 0)
        def _():
            acc_ref[...] = part

        @pl.when(i > 0)
        def _():
            acc_ref[...] += part

        @pl.when(i == n_tiles - 1)
        def _():
            loss_ref[...] = jnp.full((8, 128), jnp.sum(acc_ref[...]) * (0.5 / D_MODEL), F32)

    tile = pl.BlockSpec((tm, D_MODEL), lambda i: (i, 0))
    dy, loss = pl.pallas_call(
        body, out_shape=(jax.ShapeDtypeStruct((SEQ, D_MODEL), F32), jax.ShapeDtypeStruct((8, 128), F32)),
        grid=(n_tiles,), in_specs=[tile, tile], out_specs=(tile, pl.BlockSpec((8, 128), lambda i: (0, 0))),
        scratch_shapes=[pltpu.VMEM((8, D_MODEL), F32)], name=name, compiler_params=_params())(y, target)
    return dy, loss[0, 0]


def _row_index(shape):
    return lax.broadcasted_iota(jnp.int32, shape, 0)


def _lane_index(shape):
    return lax.broadcasted_iota(jnp.int32, shape, len(shape) - 1)


def _skew_rows(t, direction):
    q = _row_index(t.shape) & (GRID_W - 1)
    for bit in range(6):
        step = 1 << bit
        shift = step if direction > 0 else 128 - step
        t = jnp.where((q & step) != 0, pltpu.roll(t, shift, 1), t)
    return t


RPB_HEADS = 4


def _rpb_table(rpb_pad):
    rows = RPB_HEADS * 16 * GRID_W

    def body(r_ref, t_ref):
        lane = _lane_index((rows, 128))
        v = pltpu.roll(r_ref[...].reshape(RPB_HEADS * 16, 128), 128 - 15, 1)
        t = _skew_rows(jnp.broadcast_to(v[:, None, :], (RPB_HEADS * 16, GRID_W, 128)).reshape(rows, 128), +1)
        t = jnp.where(lane < GRID_W, t, 0.0)
        below = jnp.concatenate([t[GRID_W:], jnp.zeros((GRID_W, 128), F32)], axis=0)
        first_col = jnp.clip((_row_index((rows, 128)) & (GRID_W - 1)) - 8, 0, GRID_W - 16)
        key_col = lane & (GRID_W - 1)
        in_window = (key_col >= first_col) & (key_col < first_col + 16)
        t_ref[...] = jnp.where(in_window, t + pltpu.roll(below, GRID_W, 1), NEG_INF).reshape(RPB_HEADS, 16, GRID_W, 128)

    return pl.pallas_call(
        body, out_shape=jax.ShapeDtypeStruct((N_HEADS, 16, GRID_W, 128), F32), grid=(N_HEADS // RPB_HEADS,),
        in_specs=[pl.BlockSpec((RPB_HEADS, 16, 128), lambda h: (h, 0, 0))],
        out_specs=pl.BlockSpec((RPB_HEADS, 16, GRID_W, 128), lambda h: (h, 0, 0, 0)),
        name="rpb_table", compiler_params=_params())(rpb_pad)


def _rpb_grad(gp):
    rows = RPB_HEADS * 16 * GRID_W

    def body(g_ref, o_ref):
        lane = _lane_index((rows, 128))
        g = g_ref[...].reshape(rows, 128)
        low = jnp.where(lane < GRID_W, g, 0.0)
        high = pltpu.roll(jnp.where(lane >= GRID_W, g, 0.0), GRID_W, 1)
        above = jnp.concatenate([jnp.zeros((GRID_W, 128), F32), high[:rows - GRID_W]], axis=0)
        diag = jnp.sum(_skew_rows(low + above, -1).reshape(RPB_HEADS * 16, GRID_W, 128), axis=1)
        o_ref[...] = pltpu.roll(diag, 15, 1).reshape(RPB_HEADS, 16, 128)

    return pl.pallas_call(
        body, out_shape=jax.ShapeDtypeStruct((N_HEADS, 16, 128), F32), grid=(N_HEADS // RPB_HEADS,),
        in_specs=[pl.BlockSpec((RPB_HEADS, 16, GRID_W, 128), lambda h: (h, 0, 0, 0))],
        out_specs=pl.BlockSpec((RPB_HEADS, 16, 128), lambda h: (h, 0, 0)),
        name="rpb_grad", compiler_params=_params())(gp)


NA_KEYS = NA_ROWS * GRID_W


def _na_window(i):
    first_row = jnp.clip(i - NA_ROWS // 2, 0, SEQ_ROWS - NA_ROWS)
    return pl.multiple_of(first_row * GRID_W, GRID_W), first_row - i + NA_ROWS - 1


NA_STEP = 32


def _head_masks():
    lane = _lane_index((1, 128))
    return (lane < HEAD_DIM, lane >= HEAD_DIM)


def _stack_heads(t, masks):
    zero = jnp.zeros_like(t)
    return jnp.concatenate([jnp.where(masks[0], t, zero), jnp.where(masks[1], t, zero)], axis=0)


def _unstack_heads(t, masks):
    n = t.shape[0] // 2
    return jnp.where(masks[0], t[:n], t[n:])


def _stack_columns(t):
    return jnp.concatenate([t[:, 0:1], t[:, HEAD_DIM:HEAD_DIM + 1]], axis=0)


def _na_scores(qs, kw, tp_ref, dr0):
    s = lax.dot_general(qs, kw, NT, preferred_element_type=F32)
    bias = jnp.concatenate(
        [jnp.concatenate([tp_ref[a, pl.ds(dr0 + 2 * c, 1), :, :].reshape(GRID_W, 128) for c in range(4)], axis=1)
         for a in range(2)], axis=0)
    return s + bias


def _na_specs():
    q_spec = pl.BlockSpec((NA_STEP * GRID_W, 128), lambda hp, i: (i, hp))
    k_spec = pl.BlockSpec((SEQ, 128), lambda hp, i: (0, 8 + hp))
    v_spec = pl.BlockSpec((SEQ, 128), lambda hp, i: (0, 16 + hp))
    tp_spec = pl.BlockSpec((2, 16, GRID_W, 128), lambda hp, i: (hp, 0, 0, 0))
    return q_spec, k_spec, v_spec, tp_spec


def _na_fwd(qkv, table):
    def body(q_ref, k_ref, v_ref, tp_ref, o_ref, lse_ref):
        masks = _head_masks()
        for r in range(NA_STEP):
            rows = slice(r * GRID_W, (r + 1) * GRID_W)
            start, dr0 = _na_window(pl.program_id(1) * NA_STEP + r)
            kw = k_ref[pl.ds(start, NA_KEYS), :]
            vw = v_ref[pl.ds(start, NA_KEYS), :]
            s = _na_scores(_stack_heads(q_ref[rows, :] * Q_SCALE, masks), kw, tp_ref, dr0)
            m = jnp.max(s, axis=-1, keepdims=True)
            p = jnp.exp(s - m)
            denom = jnp.sum(p, axis=-1, keepdims=True)
            out = jnp.dot(p.astype(BF16), vw, preferred_element_type=F32) / denom
            o_ref[rows, :] = _unstack_heads(out, masks).astype(o_ref.dtype)
            lse_ref[rows, :] = _unstack_heads(jnp.broadcast_to(m + jnp.log(denom), (2 * GRID_W, 128)), masks)

    q_spec, k_spec, v_spec, tp_spec = _na_specs()
    return pl.pallas_call(
        body, out_shape=(jax.ShapeDtypeStruct((SEQ, D_MODEL), BF16), jax.ShapeDtypeStruct((SEQ, D_MODEL), F32)),
        grid=(N_HEADS // 2, SEQ_ROWS // NA_STEP), in_specs=[q_spec, k_spec, v_spec, tp_spec],
        out_specs=(q_spec, q_spec), name="na_fwd", compiler_params=_params())(qkv, qkv, qkv, table)


def _na_bwd(qkv, table, d_out, lse):
    def body(q_ref, k_ref, v_ref, tp_ref, do_ref, lse_ref, dqkv_ref, gp_ref, dk_acc, dv_acc):
        step = pl.program_id(1)

        @pl.when(step == 0)
        def _():
            dk_acc[...] = jnp.zeros_like(dk_acc)
            dv_acc[...] = jnp.zeros_like(dv_acc)
            gp_ref[...] = jnp.zeros_like(gp_ref)

        masks = _head_masks()
        for r in range(NA_STEP):
            rows = slice(r * GRID_W, (r + 1) * GRID_W)
            i = step * NA_STEP + r
            start, dr0 = _na_window(i)
            kw = k_ref[pl.ds(start, NA_KEYS), :]
            vw = v_ref[pl.ds(start, NA_KEYS), :]
            qs = _stack_heads(q_ref[rows, :] * Q_SCALE, masks)
            dos = _stack_heads(do_ref[rows, :], masks)
            p = jnp.exp(_na_scores(qs, kw, tp_ref, dr0) - _stack_columns(lse_ref[rows, :]))
            dp = lax.dot_general(dos, vw, NT, preferred_element_type=F32)
            ds = p * (dp - jnp.sum(p * dp, axis=-1, keepdims=True))
            for a in range(2):
                for c in range(4):
                    gp_ref[a, pl.ds(dr0 + 2 * c, 1), :, :] += (
                        ds[a * GRID_W:(a + 1) * GRID_W, 128 * c:128 * (c + 1)].reshape(1, GRID_W, 128))
            dsb = ds.astype(BF16)
            dq = _unstack_heads(jnp.dot(dsb, kw, preferred_element_type=F32), masks) * Q_SCALE
            dqkv_ref[0, pl.ds(pl.multiple_of(i * GRID_W, GRID_W), GRID_W), :] = dq.astype(dqkv_ref.dtype)
            dk_acc[pl.ds(start, NA_KEYS), :] += lax.dot_general(dsb, qs, TN, preferred_element_type=F32)
            dv_acc[pl.ds(start, NA_KEYS), :] += lax.dot_general(p.astype(BF16), dos, TN, preferred_element_type=F32)

        @pl.when(step == SEQ_ROWS // NA_STEP - 1)
        def _():
            dqkv_ref[1] = dk_acc[...].astype(dqkv_ref.dtype)
            dqkv_ref[2] = dv_acc[...].astype(dqkv_ref.dtype)

    q_spec, k_spec, v_spec, tp_spec = _na_specs()
    return pl.pallas_call(
        body,
        out_shape=(jax.ShapeDtypeStruct((3, SEQ, D_MODEL), BF16), jax.ShapeDtypeStruct((N_HEADS, 16, GRID_W, 128), F32)),
        grid=(N_HEADS // 2, SEQ_ROWS // NA_STEP), in_specs=[q_spec, k_spec, v_spec, tp_spec, q_spec, q_spec],
        out_specs=(pl.BlockSpec((3, SEQ, 128), lambda hp, i: (0, 0, hp)), tp_spec),
        scratch_shapes=[pltpu.VMEM((SEQ, 128), F32), pltpu.VMEM((SEQ, 128), F32)],
        name="na_bwd", compiler_params=_params())(qkv, qkv, qkv, table, d_out, lse)


DIL_STEP = 16


def _dil_geometry(group):
    dil = DIL_GROUPS[group][1]
    sub_len = SEQ // dil
    blocks = sub_len // BAND
    return dil, sub_len, max(blocks // DIL_STEP, 1), max(DIL_STEP // blocks, 1), min(2 * BAND, sub_len)


def _dil_block(step, r, sub_len, subs):
    per_sub = DIL_STEP // subs
    return (r // per_sub) * sub_len, step * per_sub + r % per_sub


def _dil_window(b, sub_len, n_keys):
    if n_keys == sub_len:
        return 0
    return pl.multiple_of(jnp.clip(b * BAND - RADIUS, 0, sub_len - n_keys), RADIUS)


def _dil_bias(b, start, n_keys, slope_ref, hp):
    row = _row_index((2 * BAND, n_keys))
    qpos = b * BAND + (row & (BAND - 1))
    kpos = start + _lane_index((2 * BAND, n_keys))
    dist = jnp.abs(qpos - kpos)
    slope = jnp.where(row < BAND, slope_ref[2 * hp], slope_ref[2 * hp + 1])
    return slope * dist.astype(F32), dist <= RADIUS


def _dil_scores(qs, kw, penalty, valid):
    return jnp.where(valid, lax.dot_general(qs, kw, NT, preferred_element_type=F32) - penalty, NEG_INF)


def _dil_specs(group):
    dil, sub_len, steps, subs, _ = _dil_geometry(group)
    col = group * 24
    rows = DIL_STEP * BAND
    q_spec = pl.BlockSpec((rows, 128), lambda n, hp, b: (n * steps + b, col + hp))
    k_spec = pl.BlockSpec((subs * sub_len, 128), lambda n, hp, b: (n, col + 8 + hp))
    v_spec = pl.BlockSpec((subs * sub_len, 128), lambda n, hp, b: (n, col + 16 + hp))
    tile = pl.BlockSpec((rows, 128), lambda n, hp, b: (n * steps + b, hp))
    smem = pl.BlockSpec(memory_space=pltpu.SMEM)
    return (dil // subs, N_HEADS // 2, steps), q_spec, k_spec, v_spec, tile, smem


def _dil_fwd(group, qkv, slopes):
    _, sub_len, _, subs, n_keys = _dil_geometry(group)

    def body(q_ref, k_ref, v_ref, slope_ref, o_ref, lse_ref):
        hp = pl.program_id(1)
        masks = _head_masks()
        for r in range(DIL_STEP):
            rows = slice(r * BAND, (r + 1) * BAND)
            base, b = _dil_block(pl.program_id(2), r, sub_len, subs)
            start = _dil_window(b, sub_len, n_keys)
            kw = k_ref[pl.ds(base + start, n_keys), :]
            vw = v_ref[pl.ds(base + start, n_keys), :]
            penalty, valid = _dil_bias(b, start, n_keys, slope_ref, hp)
            s = _dil_scores(_stack_heads(q_ref[rows, :] * Q_SCALE, masks), kw, penalty, valid)
            m = jnp.max(s, axis=-1, keepdims=True)
            p = jnp.exp(s - m)
            denom = jnp.sum(p, axis=-1, keepdims=True)
            out = jnp.dot(p.astype(BF16), vw, preferred_element_type=F32) / denom
            o_ref[rows, :] = _unstack_heads(out, masks).astype(o_ref.dtype)
            lse_ref[rows, :] = _unstack_heads(jnp.broadcast_to(m + jnp.log(denom), (2 * BAND, 128)), masks)

    grid, q_spec, k_spec, v_spec, tile, smem = _dil_specs(group)
    return pl.pallas_call(
        body, out_shape=(jax.ShapeDtypeStruct((SEQ, D_MODEL), BF16), jax.ShapeDtypeStruct((SEQ, D_MODEL), F32)),
        grid=grid, in_specs=[q_spec, k_spec, v_spec, smem], out_specs=(tile, tile),
        name=f"dil_fwd_{group}", compiler_params=_params())(qkv, qkv, qkv, slopes)


def _dil_merge(outs, lses):
    n_sorted = len(SORTED)

    def body(*refs):
        o_refs, l_refs = refs[:3], refs[3:6]
        out_refs, lse_refs, scratch = refs[6:7 + n_sorted], refs[7 + n_sorted:8 + 2 * n_sorted], refs[-1]
        os_ = [o_refs[0][...]] + [_unsort_tile(scratch, o_refs[1 + j], dil) for j, dil in enumerate(SORTED)]
        ls = [l_refs[0][...]] + [_unsort_tile(scratch, l_refs[1 + j], dil) for j, dil in enumerate(SORTED)]
        m = jnp.maximum(jnp.maximum(ls[0], ls[1]), ls[2])
        es = [jnp.exp(v - m) for v in ls]
        total = es[0] + es[1] + es[2]
        merged = (es[0] * os_[0] + es[1] * os_[1] + es[2] * os_[2]) / total
        lse = m + jnp.log(total)
        out_refs[0][...] = merged
        lse_refs[0][...] = lse
        for j, dil in enumerate(SORTED):
            _sort_tile(scratch, merged, dil, out_refs[1 + j])
            _sort_tile(scratch, lse, dil, lse_refs[1 + j])

    tm = 256
    tile = pl.BlockSpec((tm, D_MODEL), lambda i: (i, 0))
    specs = [tile] + [_sorted_spec(dil, tm=tm) for dil in SORTED]
    shapes = [jax.ShapeDtypeStruct((SEQ, D_MODEL), F32)] + [jax.ShapeDtypeStruct((dil, SEQ // dil, D_MODEL), F32) for dil in SORTED]
    views = lambda ts: [ts[0]] + [_sorted_view(t, dil) for t, dil in zip(ts[1:], SORTED)]
    result = pl.pallas_call(
        body, out_shape=tuple(shapes * 2), grid=(SEQ // tm,), in_specs=specs * 2, out_specs=tuple(specs * 2),
        scratch_shapes=[_sort_scratch(tm)], name="dil_merge", compiler_params=_params())(*views(outs), *views(lses))
    flat = [t.reshape(SEQ, D_MODEL) for t in result]
    return flat[:1 + n_sorted], flat[1 + n_sorted:]


def _dil_bwd(group, qkv, slopes, d_out, out, lse_group, lse_total, into):
    _, sub_len, steps, subs, n_keys = _dil_geometry(group)

    def body(q_ref, k_ref, v_ref, slope_ref, do_ref, o_ref, lg_ref, lt_ref, into_ref, dqkv_ref, dk_acc, dv_acc):
        hp, step = pl.program_id(1), pl.program_id(2)

        @pl.when(step == 0)
        def _():
            dk_acc[...] = jnp.zeros_like(dk_acc)
            dv_acc[...] = jnp.zeros_like(dv_acc)

        masks = _head_masks()
        for r in range(DIL_STEP):
            rows = slice(r * BAND, (r + 1) * BAND)
            base, b = _dil_block(step, r, sub_len, subs)
            start = _dil_window(b, sub_len, n_keys)
            keys = pl.ds(base + start, n_keys)
            kw = k_ref[keys, :]
            vw = v_ref[keys, :]
            penalty, valid = _dil_bias(b, start, n_keys, slope_ref, hp)
            qs = _stack_heads(q_ref[rows, :] * Q_SCALE, masks)
            lse2 = lg_ref[rows, :]
            weight = jnp.exp(lse2 - lt_ref[rows, :])
            do2 = do_ref[rows, :]
            dogs = _stack_heads((weight * do2).astype(BF16), masks)
            delta = _stack_columns(weight) * jnp.sum(_stack_heads(do2 * o_ref[rows, :], masks), axis=-1, keepdims=True)
            p = jnp.exp(_dil_scores(qs, kw, penalty, valid) - _stack_columns(lse2))
            dp = lax.dot_general(dogs, vw, NT, preferred_element_type=F32)
            dsb = (p * (dp - delta)).astype(BF16)
            dq = _unstack_heads(jnp.dot(dsb, kw, preferred_element_type=F32), masks) * Q_SCALE
            dqkv_ref[0, pl.ds(pl.multiple_of(base + b * BAND, BAND), BAND), :] = dq.astype(dqkv_ref.dtype)
            dk_acc[keys, :] += lax.dot_general(dsb, qs, TN, preferred_element_type=F32)
            dv_acc[keys, :] += lax.dot_general(p.astype(BF16), dogs, TN, preferred_element_type=F32)

        @pl.when(step == steps - 1)
        def _():
            dqkv_ref[1] = dk_acc[...].astype(dqkv_ref.dtype)
            dqkv_ref[2] = dv_acc[...].astype(dqkv_ref.dtype)

    grid, q_spec, k_spec, v_spec, tile, smem = _dil_specs(group)
    return pl.pallas_call(
        body, out_shape=jax.ShapeDtypeStruct(into.shape, into.dtype), grid=grid,
        in_specs=[q_spec, k_spec, v_spec, smem, tile, tile, tile, tile, ANY],
        out_specs=pl.BlockSpec((3, subs * sub_len, 128), lambda n, hp, b: (group, n, hp)),
        scratch_shapes=[pltpu.VMEM((subs * sub_len, 128), F32), pltpu.VMEM((subs * sub_len, 128), F32)],
        input_output_aliases={8: 0}, name=f"dil_bwd_{group}", compiler_params=_params(),
    )(qkv, qkv, qkv, slopes, d_out, out, lse_group, lse_total, into)


def _accumulate_rows(acc_ref, i, first, part):
    rows = pl.ds(pl.multiple_of(i * TM_MM, TM_MM), TM_MM)

    @pl.when(first)
    def _():
        acc_ref[rows, :] = part

    @pl.when(jnp.logical_not(first))
    def _():
        acc_ref[rows, :] += part


def _ffn_specs():
    tile = pl.BlockSpec((TM_MM, D_MODEL), lambda d, i: (i, 0))
    gate = pl.BlockSpec((None, None, D_MODEL, FF_PAD), lambda d, i: (d, 0, 0, 0))
    up = pl.BlockSpec((None, None, D_MODEL, FF_PAD), lambda d, i: (d, 1, 0, 0))
    down = pl.BlockSpec((None, FF_PAD, D_MODEL), lambda d, i: (d, 0, 0))
    hidden = pl.BlockSpec((None, TM_MM, FF_PAD), lambda d, i: (d, i, 0))
    whole = pl.BlockSpec((SEQ, D_MODEL), lambda d, i: (0, 0))
    return tile, gate, up, down, hidden, whole


def _ffn_fwd(name, h, w_gu, w_down):
    def body(h_ref, wg_ref, wu_ref, wd_ref, f_ref, hg_ref, hu_ref, act_t_ref):
        hv = h_ref[...]
        hg = jnp.dot(hv, wg_ref[...], preferred_element_type=F32)
        hu = jnp.dot(hv, wu_ref[...], preferred_element_type=F32)
        act = hg * jax.nn.sigmoid(hg) * hu
        act_t_ref[...] = act.T.astype(act_t_ref.dtype)
        hg_ref[...] = hg.astype(hg_ref.dtype)
        hu_ref[...] = hu.astype(hu_ref.dtype)
        _accumulate_rows(f_ref, pl.program_id(1), pl.program_id(0) == 0,
                         jnp.dot(act.astype(BF16), wd_ref[...], preferred_element_type=F32))

    tile, gate, up, down, hidden, whole = _ffn_specs()
    shape = jax.ShapeDtypeStruct((N_DEV, SEQ, FF_PAD), BF16)
    return pl.pallas_call(
        body, out_shape=(jax.ShapeDtypeStruct((SEQ, D_MODEL), F32), shape, shape, jax.ShapeDtypeStruct((N_DEV, FF_PAD, SEQ), BF16)),
        grid=(N_DEV, SEQ // TM_MM), in_specs=[tile, gate, up, down],
        out_specs=(whole, hidden, hidden, pl.BlockSpec((None, FF_PAD, TM_MM), lambda d, i: (d, 0, i))),
        name=name, compiler_params=_params())(h, w_gu, w_gu, w_down)


def _ffn_dgu(name, h_t, dgu):
    def body(h_ref, dgu_ref, o_ref):
        both = jnp.dot(h_ref[...], jnp.concatenate([dgu_ref[0], dgu_ref[1]], axis=1), preferred_element_type=F32)
        o_ref[0] = both[:, :FF_PAD].astype(o_ref.dtype)
        o_ref[1] = both[:, FF_PAD:].astype(o_ref.dtype)

    return pl.pallas_call(
        body, out_shape=jax.ShapeDtypeStruct((N_DEV, 2, D_MODEL, FF_PAD), BF16), grid=(N_DEV,),
        in_specs=[pl.BlockSpec((D_MODEL, SEQ), lambda d: (0, 0)), pl.BlockSpec((2, None, SEQ, FF_PAD), lambda d: (0, d, 0, 0))],
        out_specs=pl.BlockSpec((None, 2, D_MODEL, FF_PAD), lambda d: (d, 0, 0, 0)),
        name=name, compiler_params=_params())(h_t, dgu)


def _ffn_bwd(name, d_f, w_gu, w_down, hg, hu, act_t):
    n_tiles = SEQ // TM_MM

    def body(df_ref, wg_ref, wu_ref, wd_ref, hg_ref, hu_ref, act_t_ref, dgu_ref, dh_ref, dwd_ref, dwd_acc):
        i = pl.program_id(1)
        d_down = jnp.dot(act_t_ref[...], df_ref[...], preferred_element_type=F32)

        @pl.when(i == 0)
        def _():
            dwd_acc[...] = d_down

        @pl.when(i > 0)
        def _():
            dwd_acc[...] += d_down

        @pl.when(i == n_tiles - 1)
        def _():
            dwd_ref[...] = dwd_acc[...].astype(dwd_ref.dtype)

        dact = lax.dot_general(df_ref[...], wd_ref[...], NT, preferred_element_type=F32)
        hgv = hg_ref[...].astype(F32)
        sig = jax.nn.sigmoid(hgv)
        d_gate = (dact * hu_ref[...].astype(F32) * (sig * (1.0 + hgv * (1.0 - sig)))).astype(BF16)
        d_up = (dact * hgv * sig).astype(BF16)
        dgu_ref[0] = d_gate
        dgu_ref[1] = d_up
        part = lax.dot_general(jnp.concatenate([d_gate, d_up], axis=1), jnp.concatenate([wg_ref[...], wu_ref[...]], axis=1),
                               NT, preferred_element_type=F32)
        _accumulate_rows(dh_ref, pl.program_id(1), pl.program_id(0) == 0, part)

    tile, gate, up, down, hidden, whole = _ffn_specs()
    return pl.pallas_call(
        body, out_shape=(jax.ShapeDtypeStruct((2, N_DEV, SEQ, FF_PAD), BF16), jax.ShapeDtypeStruct((SEQ, D_MODEL), F32),
                         jax.ShapeDtypeStruct((N_DEV, FF_PAD, D_MODEL), BF16)),
        grid=(N_DEV, n_tiles),
        in_specs=[tile, gate, up, down, hidden, hidden, pl.BlockSpec((None, FF_PAD, TM_MM), lambda d, i: (d, 0, i))],
        out_specs=(pl.BlockSpec((2, None, TM_MM, FF_PAD), lambda d, i: (0, d, i, 0)), whole, down),
        scratch_shapes=[pltpu.VMEM((FF_PAD, D_MODEL), F32)],
        name=name, compiler_params=_params())(d_f, w_gu, w_gu, w_down, hg, hu, act_t)


def _position():
    return lax.axis_index("x"), lax.axis_index("y"), lax.axis_index("c")


def _flat(p):
    return 4 * p[0] + 2 * p[1] + p[2]


def _peer(me, k):
    x, y, c = me
    return (1 - x if k & 4 else x, 1 - y if k & 2 else y, 1 - c if k & 1 else c)


def _columns(width):
    return lambda ref, d: ref.at[:, pl.ds(pl.multiple_of(d * width, 128), width)]


def _leading(ref, d):
    return ref.at[d]


def _by_distance(ref, sender, k):
    return ref.at[k - 1]


def _prep_weights(me, na_qkv, na_o, dil_qkv, dil_o, gate, up, down, land_shapes):
    na_cols, dil_cols = na_qkv.shape[-1], dil_qkv.shape[-1]
    o_rows = na_o.shape[1]
    tiles = 4
    rows, rows_o = D_MODEL // tiles, o_rows // tiles

    def body(me_ref, naq, nao, dq, do_, g0, u0, d0, g1, u1, d1, *outs):
        def put(t, index, value):
            outs[t][index] = value
            outs[8 + t][index] = value

        put(0, ..., naq[...].astype(BF16))
        put(1, ..., nao[...].astype(BF16))
        put(4, ..., dq[...].astype(BF16))
        put(5, ..., do_[...].astype(BF16))
        for t, (g, u, d) in ((2, (g0, u0, d0)), (6, (g1, u1, d1))):
            for j, part in enumerate((g, u)):
                put(t, (j, slice(None), slice(0, FF_SHARD)), part[...].astype(BF16))
                put(t, (j, slice(None), slice(FF_SHARD, FF_PAD)), jnp.zeros((rows, FF_PAD - FF_SHARD), BF16))
            put(t + 1, (slice(0, FF_SHARD), slice(None)), d[...].astype(BF16))
            put(t + 1, (slice(FF_SHARD, FF_PAD), slice(None)), jnp.zeros((FF_PAD - FF_SHARD, D_MODEL), BF16))

    def tiled(width):
        return pl.BlockSpec((None, rows, width), lambda i, me: (0, i, 0))

    def layer(l, width):
        return pl.BlockSpec((None, rows, width), lambda i, me: (l, i, 0))

    def whole_layer(l):
        return pl.BlockSpec((None, FF_SHARD, D_MODEL), lambda i, me: (l, 0, 0))

    in_specs = [tiled(na_cols), pl.BlockSpec((None, rows_o, D_MODEL), lambda i, me: (0, i, 0)), tiled(dil_cols),
                pl.BlockSpec((None, rows_o, D_MODEL), lambda i, me: (0, i, 0)),
                layer(0, FF_SHARD), layer(0, FF_SHARD), whole_layer(0), layer(1, FF_SHARD), layer(1, FF_SHARD), whole_layer(1)]
    o_shard = pl.BlockSpec((rows_o, D_MODEL), lambda i, me: (i, 0))
    o_land = pl.BlockSpec((None, rows_o, D_MODEL), lambda i, me: (me[0], i, 0))
    gu_shard = pl.BlockSpec((2, rows, FF_PAD), lambda i, me: (0, i, 0))
    gu_land = pl.BlockSpec((None, 2, rows, FF_PAD), lambda i, me: (me[0], 0, i, 0))
    down_shard = pl.BlockSpec((FF_PAD, D_MODEL), lambda i, me: (0, 0))
    down_land = pl.BlockSpec((None, FF_PAD, D_MODEL), lambda i, me: (me[0], 0, 0))

    def qkv_shard(width):
        return pl.BlockSpec((rows, width), lambda i, me: (i, 0))

    def qkv_land(width):
        return pl.BlockSpec((rows, width), lambda i, me: (i, me[0]))

    shard_specs = [qkv_shard(na_cols), o_shard, gu_shard, down_shard, qkv_shard(dil_cols), o_shard, gu_shard, down_shard]
    land_specs = [qkv_land(na_cols), o_land, gu_land, down_land, qkv_land(dil_cols), o_land, gu_land, down_land]
    shard_shapes = [jax.ShapeDtypeStruct(s, BF16) for s in
                    ((D_MODEL, na_cols), (o_rows, D_MODEL), (2, D_MODEL, FF_PAD), (FF_PAD, D_MODEL),
                     (D_MODEL, dil_cols), (o_rows, D_MODEL), (2, D_MODEL, FF_PAD), (FF_PAD, D_MODEL))]
    result = pl.pallas_call(
        body, out_shape=tuple(shard_shapes + list(land_shapes)),
        grid_spec=pltpu.PrefetchScalarGridSpec(num_scalar_prefetch=1, grid=(tiles,), in_specs=in_specs,
                                               out_specs=tuple(shard_specs + land_specs)),
        name="prep_weights", compiler_params=_params())(me, na_qkv, na_o, dil_qkv, dil_o, gate, up, down, gate, up, down)
    return list(result[:8]), list(result[8:])


def _remote_copies(sets, src_refs, land_refs, send_sems, recv_sems, outgoing):
    me = _position()
    copies = []
    for t, (si, src_of, li, dst_of) in enumerate(sets):
        for k in range(1, N_DEV):
            other = _peer(me, k)
            sender = me if outgoing else other
            copies.append(pltpu.make_async_remote_copy(
                src_ref=src_of(src_refs[si], _flat(other)), dst_ref=dst_of(land_refs[li], _flat(sender), k),
                send_sem=send_sems.at[(N_DEV - 1) * t + k - 1], recv_sem=recv_sems.at[(N_DEV - 1) * t + k - 1],
                device_id=other, device_id_type=MESH))
    return copies


def _send_start(name, srcs, lands, sets_by_group):
    n_src, n_land, n_groups = len(srcs), len(lands), len(sets_by_group)

    def body(*refs):
        src_refs, land_refs = refs[:n_src], refs[n_src:n_src + n_land]
        outs = refs[n_src + n_land:]
        for g, sets in enumerate(sets_by_group):
            for cp in _remote_copies(sets, src_refs, land_refs, outs[2 * g], outs[2 * g + 1], True):
                cp.start()
        outs[-1][...] = jnp.zeros_like(outs[-1])

    sem_shapes = []
    for sets in sets_by_group:
        sem_shapes += [pltpu.SemaphoreType.DMA((len(sets) * (N_DEV - 1),))] * 2
    thru = [pltpu.HBM(a.shape, a.dtype) for a in list(srcs) + list(lands)]
    n_sem = len(sem_shapes)
    result = pl.pallas_call(
        body, out_shape=tuple(sem_shapes + thru + [jax.ShapeDtypeStruct((8, 128), F32)]),
        in_specs=[HBM] * (n_src + n_land),
        out_specs=tuple([SEM] * n_sem + [HBM] * (n_src + n_land) + [pl.BlockSpec(memory_space=pltpu.VMEM)]),
        input_output_aliases={i: n_sem + i for i in range(n_src + n_land)},
        compiler_params=pltpu.CompilerParams(has_side_effects=EFFECT), name=name,
    )(*[pltpu.with_memory_space_constraint(a, pltpu.HBM) for a in list(srcs) + list(lands)])
    sems = [(result[2 * g], result[2 * g + 1]) for g in range(n_groups)]
    return sems, list(result[n_sem:n_sem + n_src]), list(result[n_sem + n_src:n_sem + n_src + n_land]), result[-1]


def _send_wait(name, sems, srcs, lands, sets, after):
    n_src, n_land = len(srcs), len(lands)

    def body(*refs):
        src_refs, land_refs = refs[:n_src], refs[n_src:n_src + n_land]
        send_sems, recv_sems = refs[n_src + n_land], refs[n_src + n_land + 1]
        for cp in _remote_copies(sets, src_refs, land_refs, send_sems, recv_sems, True):
            cp.wait_send()
        for cp in _remote_copies(sets, src_refs, land_refs, send_sems, recv_sems, False):
            cp.wait_recv()

    thru = [pltpu.HBM(a.shape, a.dtype) for a in list(srcs) + list(lands)]
    result = pl.pallas_call(
        body, out_shape=tuple(thru), in_specs=[HBM] * (n_src + n_land) + [SEM, SEM] + [ANY] * len(after),
        out_specs=tuple([HBM] * (n_src + n_land)), input_output_aliases={i: i for i in range(n_src + n_land)},
        compiler_params=pltpu.CompilerParams(has_side_effects=EFFECT), name=name,
    )(*srcs, *lands, sems[0], sems[1], *after)
    return list(result[:n_src]), list(result[n_src:])


DIRECT = (1, 2, 4, 6)
PASSED = DIRECT[1:]


def _hbm_passthrough(body, name, arrays, n_sem_in, sem_out_shapes, extra):
    n, n_out = len(arrays), len(sem_out_shapes)
    return pl.pallas_call(
        body, out_shape=tuple(list(sem_out_shapes) + [pltpu.HBM(a.shape, a.dtype) for a in arrays]),
        in_specs=[HBM] * n + [SEM] * n_sem_in + [ANY] * len(extra), out_specs=tuple([SEM] * n_out + [HBM] * n),
        input_output_aliases={i: n_out + i for i in range(n)},
        compiler_params=pltpu.CompilerParams(has_side_effects=EFFECT), name=name)


def _shard_copy(src_ref, land_ref, window, block, to, send_sem, recv_sem, from_shard):
    dst = window(land_ref, _flat(block))
    return pltpu.make_async_remote_copy(src_ref=src_ref if from_shard else dst, dst_ref=dst, send_sem=send_sem,
                                        recv_sem=recv_sem, device_id=to, device_id_type=MESH)


def _gather_start(name, shards, lands, windows, group_sizes):
    n = len(shards)

    def body(*refs):
        shard_refs, land_refs, outs = refs[:n], refs[n:2 * n], refs[2 * n:]
        me = _position()
        t = 0
        for g, size in enumerate(group_sizes):
            for local in range(size):
                for j, k in enumerate(DIRECT):
                    i = len(DIRECT) * local + j
                    _shard_copy(shard_refs[t], land_refs[t], windows[t], me, _peer(me, k), outs[2 * g].at[i],
                                outs[2 * g + 1].at[i], True).start()
                t += 1

    sem_shapes = [pltpu.SemaphoreType.DMA((len(DIRECT) * size,)) for size in group_sizes for _ in range(2)]
    arrays = [pltpu.with_memory_space_constraint(a, pltpu.HBM) for a in list(shards) + list(lands)]
    result = _hbm_passthrough(body, name, arrays, 0, sem_shapes, ())(*arrays)
    n_sem = len(sem_shapes)
    sems = [(result[2 * g], result[2 * g + 1]) for g in range(len(group_sizes))]
    return sems, list(result[n_sem:n_sem + n]), list(result[n_sem + n:])


def _gather_pass_on(name, sems, shards, lands, windows, after):
    n = len(shards)

    def body(*refs):
        shard_refs, land_refs = refs[:n], refs[n:2 * n]
        recv_sems = refs[2 * n + 1]
        pass_send, pass_recv = refs[2 * n + 2 + len(after)], refs[2 * n + 3 + len(after)]
        me = _position()
        sibling = _peer(me, 1)
        for t in range(n):
            for j, k in enumerate(PASSED):
                sender = _peer(me, k)
                arrived = len(DIRECT) * t + 1 + j
                _shard_copy(shard_refs[t], land_refs[t], windows[t], sender, me, refs[2 * n].at[arrived], recv_sems.at[arrived],
                            True).wait_recv()
                i = len(PASSED) * t + j
                _shard_copy(shard_refs[t], land_refs[t], windows[t], sender, sibling, pass_send.at[i], pass_recv.at[i],
                            False).start()

    sem_shapes = [pltpu.SemaphoreType.DMA((len(PASSED) * n,))] * 2
    result = _hbm_passthrough(body, name, list(shards) + list(lands), 2, sem_shapes, after)(
        *shards, *lands, sems[0], sems[1], *after)
    return (result[0], result[1]), list(result[2:2 + n]), list(result[2 + n:])


def _gather_wait(name, sems, pass_sems, shards, lands, windows, after):
    n = len(shards)

    def body(*refs):
        shard_refs, land_refs = refs[:n], refs[n:2 * n]
        send_sems, recv_sems, pass_send, pass_recv = refs[2 * n:2 * n + 4]
        me = _position()
        sibling = _peer(me, 1)
        for t in range(n):
            for j, k in enumerate(DIRECT):
                i = len(DIRECT) * t + j
                _shard_copy(shard_refs[t], land_refs[t], windows[t], me, _peer(me, k), send_sems.at[i], recv_sems.at[i],
                            True).wait_send()
            _shard_copy(shard_refs[t], land_refs[t], windows[t], sibling, me, send_sems.at[len(DIRECT) * t],
                        recv_sems.at[len(DIRECT) * t], True).wait_recv()
            for j, k in enumerate(PASSED):
                i = len(PASSED) * t + j
                _shard_copy(shard_refs[t], land_refs[t], windows[t], _peer(me, k), sibling, pass_send.at[i], pass_recv.at[i],
                            False).wait_send()
                _shard_copy(shard_refs[t], land_refs[t], windows[t], _peer(sibling, k), me, pass_send.at[i], pass_recv.at[i],
                            False).wait_recv()

    result = _hbm_passthrough(body, name, list(shards) + list(lands), 4, [], after)(
        *shards, *lands, sems[0], sems[1], pass_sems[0], pass_sems[1], *after)
    return list(result[n:])


def _all_gather(name, locals_, out_shapes, windows, deps=()):
    n = len(locals_)

    def body(*refs):
        src_refs, out_refs = refs[:n], refs[n + len(deps):2 * n + len(deps)]
        send_sems, recv_sems, local_sems = refs[2 * n + len(deps):]
        x, y, c = _position()
        me, sibling = (x, y, c), (x, y, 1 - c)
        chips = [(1 - x, y), (x, 1 - y), (1 - x, 1 - y)]

        def copy(t, k, block, to, from_local=False):
            dst = windows[t](out_refs[t], _flat(block))
            return pltpu.make_async_remote_copy(
                src_ref=src_refs[t] if from_local else dst, dst_ref=dst, send_sem=send_sems.at[t, k],
                recv_sem=recv_sems.at[t, k], device_id=to, device_id_type=MESH)

        mine = [pltpu.make_async_copy(src_refs[t], windows[t](out_refs[t], _flat(me)), local_sems.at[t]) for t in range(n)]
        sends = []
        for t in range(n):
            mine[t].start()
            sends.append(copy(t, 0, me, sibling, True))
            sends += [copy(t, 1 + j, me, (*chip, c), True) for j, chip in enumerate(chips)]
        for cp in sends:
            cp.start()
        for t in range(n):
            for j, chip in enumerate(chips):
                copy(t, 1 + j, (*chip, c), me).wait_recv()
                passed = copy(t, 4 + j, (*chip, c), sibling)
                passed.start()
                sends.append(passed)
        for t in range(n):
            copy(t, 0, sibling, me).wait_recv()
            for j, chip in enumerate(chips):
                copy(t, 4 + j, (*chip, 1 - c), me).wait_recv()
        for cp in sends:
            cp.wait_send()
        for cp in mine:
            cp.wait()

    return pl.pallas_call(
        body, out_shape=tuple(out_shapes), in_specs=[ANY] * (n + len(deps)), out_specs=tuple([ANY] * n),
        scratch_shapes=[pltpu.SemaphoreType.DMA((n, 7)), pltpu.SemaphoreType.DMA((n, 7)), pltpu.SemaphoreType.DMA((n,))],
        name=name)(*locals_, *deps)


def _adamw(name, me, lands, owns, w, m, v, *, grid, land_specs, own_specs, p_spec):
    n_land = len(lands)

    def body(me_ref, *refs):
        land_refs, own_refs = refs[:n_land], refs[n_land:n_land + len(owns)]
        w_ref, m_ref, v_ref, g_ref, delta_ref, m_out, v_out = refs[n_land + len(owns):]
        ncols = w_ref.shape[-1]
        sums = []
        for i, land_ref in enumerate(land_refs):
            g = own_refs[i][...].astype(F32) if owns else land_ref[0].astype(F32)
            for s in range(0 if owns else 1, land_ref.shape[0]):
                g = g + land_ref[s].astype(F32)
            sums.append(g[:, :ncols])
        g = sums[0] if n_land == 1 else jnp.where(pl.program_id(0) == 0, sums[0], sums[1])
        m_new = ADAM_B1 * m_ref[...] + (1.0 - ADAM_B1) * g
        v_new = ADAM_B2 * v_ref[...] + (1.0 - ADAM_B2) * jnp.square(g)
        m_hat = m_new / (1.0 - ADAM_B1 ** ADAM_STEP)
        v_hat = v_new / (1.0 - ADAM_B2 ** ADAM_STEP)
        g_ref[...] = g
        delta_ref[...] = -ADAM_LR * (m_hat / (jnp.sqrt(v_hat) + ADAM_EPS) + ADAM_WD * w_ref[...])
        m_out[...] = m_new
        v_out[...] = v_new

    shape = jax.ShapeDtypeStruct(w.shape, F32)
    return pl.pallas_call(
        body, out_shape=(shape,) * 4,
        grid_spec=pltpu.PrefetchScalarGridSpec(
            num_scalar_prefetch=1, grid=grid, in_specs=list(land_specs) + list(own_specs) + [p_spec, p_spec, p_spec],
            out_specs=(p_spec,) * 4),
        name=name, compiler_params=_params())(me, *lands, *owns, w, m, v)


def _row(p, layer):
    return p[layer][None, :]


def _square(name, a, b, dims, out_dtype, deps=()):
    if a.shape == (D_MODEL, SEQ):
        return _matmul(name, a, b, grid=(2, 1), a_spec=pl.BlockSpec((512, SEQ), lambda i, k: (i, 0)),
                       b_spec=pl.BlockSpec((SEQ, D_MODEL), lambda i, k: (0, 0)),
                       o_spec=pl.BlockSpec((512, D_MODEL), lambda i, k: (i, 0)),
                       out_shape=jax.ShapeDtypeStruct((D_MODEL, D_MODEL), out_dtype), dims=NN, acc_shape=(8, 128),
                       deps=deps)
    return _matmul(name, a, b, grid=(SEQ // TM_MM, 1), a_spec=pl.BlockSpec((TM_MM, D_MODEL), lambda i, k: (i, 0)),
                   b_spec=pl.BlockSpec((D_MODEL, D_MODEL), lambda i, k: (0, 0)),
                   o_spec=pl.BlockSpec((TM_MM, D_MODEL), lambda i, k: (i, 0)),
                   out_shape=jax.ShapeDtypeStruct((SEQ, D_MODEL), out_dtype), dims=dims, acc_shape=(8, 128), deps=deps)


def _grouped_matmul(name, a_list, b, *, n_tiles, a_block, b_spec, o_spec, out_shape):
    n_groups = len(a_list)

    def a_spec(g):
        def index(j, i):
            mine = j // 3
            return (jnp.where(mine == g, i, jnp.where(mine < g, 0, n_tiles - 1)), 0)
        return pl.BlockSpec(a_block, index)

    def body(*refs):
        b_ref, o_ref = refs[n_groups], refs[n_groups + 1]
        mine = pl.program_id(0) // 3
        for g in range(n_groups):
            @pl.when(mine == g)
            def _(g=g):
                o_ref[...] = jnp.dot(refs[g][...], b_ref[...], preferred_element_type=F32).astype(o_ref.dtype)

    return pl.pallas_call(
        body, out_shape=out_shape, grid=(3 * n_groups, n_tiles), in_specs=[a_spec(g) for g in range(n_groups)] + [b_spec],
        out_specs=o_spec, name=name, compiler_params=_params())(*a_list, b)


def _qkv_fwd(name, hs, w):
    return _grouped_matmul(name, hs, w, n_tiles=1, a_block=(SEQ, D_MODEL),
                           b_spec=pl.BlockSpec((D_MODEL, D_MODEL), lambda j, i: (0, j)),
                           o_spec=pl.BlockSpec((SEQ, D_MODEL), lambda j, i: (i, j)),
                           out_shape=jax.ShapeDtypeStruct((SEQ, 3 * len(hs) * D_MODEL), BF16))


def _qkv_dw(name, hs_t, dqkv):
    return _grouped_matmul(name, hs_t, dqkv, n_tiles=2, a_block=(512, SEQ),
                           b_spec=pl.BlockSpec((None, SEQ, D_MODEL), lambda j, i: (j, 0, 0)),
                           o_spec=pl.BlockSpec((512, D_MODEL), lambda j, i: (i, j)),
                           out_shape=jax.ShapeDtypeStruct((D_MODEL, 3 * len(hs_t) * D_MODEL), BF16))


def _proj_do_sorted(name, d_a, w_o):
    def body(da_ref, w_ref, *refs):
        value = lax.dot_general(da_ref[...], w_ref[...], NT, preferred_element_type=F32)
        refs[0][...] = value
        for j, dil in enumerate(SORTED):
            _sort_tile(refs[-1], value, dil, refs[1 + j])

    tile = pl.BlockSpec((TM, D_MODEL), lambda i: (i, 0))
    shapes = [jax.ShapeDtypeStruct((SEQ, D_MODEL), F32)] + [jax.ShapeDtypeStruct((dil, SEQ // dil, D_MODEL), F32) for dil in SORTED]
    result = pl.pallas_call(
        body, out_shape=tuple(shapes), grid=(SEQ // TM,),
        in_specs=[tile, pl.BlockSpec((D_MODEL, D_MODEL), lambda i: (0, 0))],
        out_specs=tuple([tile] + [_sorted_spec(dil) for dil in SORTED]), scratch_shapes=[_sort_scratch()],
        name=name, compiler_params=_params())(d_a, w_o)
    return [t.reshape(SEQ, D_MODEL) for t in result]


def _qkv_dh(name, dqkv, w, n_chunks, deps):
    tm = TM_MM
    return _matmul(name, dqkv, w, grid=(n_chunks // 3, SEQ // tm, 1),
                   a_spec=pl.BlockSpec((3, tm, D_MODEL), lambda g, i, k: (g, i, 0)),
                   b_spec=pl.BlockSpec((D_MODEL, 3 * D_MODEL), lambda g, i, k: (0, g)),
                   o_spec=pl.BlockSpec((None, tm, D_MODEL), lambda g, i, k: (g, i, 0)),
                   out_shape=jax.ShapeDtypeStruct((n_chunks // 3, SEQ, D_MODEL), F32), dims=NT, acc_shape=(8, 128),
                   deps=deps, inner=3)


def _local_step(x, target, norms, rpb, fetch, emit, deps):
    mix_pre, mix_post, ffn_pre, ffn_post = norms
    slopes = 2.0 ** (-8.0 * jnp.arange(1, N_HEADS + 1, dtype=F32) / N_HEADS)
    rpb_pad = jnp.pad(rpb, ((0, 0), (0, 1), (0, 128 - 31)))
    saved = []

    hs = [_rms_fwd("l0_norm_mix", x, _row(mix_pre, 0), out_dtype=BF16, deps=deps)]
    hs_t0 = None
    for layer in range(2):
        tag = f"l{layer}"
        if layer == 0:
            table = _rpb_table(rpb_pad)
            w_qkv, w_o = fetch("na", [table, hs[0]], [hs[0]])
            qkv = _qkv_fwd(tag + "_qkv", hs, w_qkv)
            o, lse = _na_fwd(qkv, table)
            mixer = (hs, qkv, o, lse, table)
        else:
            w_qkv, w_o = fetch("dil", [saved[0][7]], [hs[0]])
            qkv = _qkv_fwd(tag + "_qkv", hs, w_qkv)
            outs, lses = zip(*[_dil_fwd(g, qkv, slopes * dil) for g, (_, dil) in enumerate(DIL_GROUPS)])
            merged, lse_total = _dil_merge(outs, lses)
            o = merged[0]
            mixer = (hs, qkv, merged, lses, lse_total)
        a = _square(tag + "_proj", o, w_o, NN, F32)
        x1, (h2, h2_t) = _post_norm_fwd(tag + "_post_mix", a, _row(mix_post, layer), x, _row(ffn_pre, layer))
        w_gu, w_down = fetch(f"ffn{layer}", [a], [h2])
        f, hg, hu, act_t = _ffn_fwd(tag + "_ffn", h2, w_gu, w_down)
        transposed = ([hs_t0 if hs_t0 is not None and t is hs[0] else t.T for t in hs], o.astype(BF16).T, h2_t, act_t)
        saved.append((x, mixer, a, x1, transposed, hg, hu, f, w_qkv, w_o, w_gu, w_down))
        if layer == 0:
            x, (h, hs_t0, *views) = _post_norm_fwd(tag + "_post_ffn", f, _row(ffn_post, 0), x1, _row(mix_pre, 1), sorted_too=True)
            hs = [h] + [t.reshape(SEQ, D_MODEL) for t in views]
        else:
            x = _rms_fwd(tag + "_post_ffn", f, _row(ffn_post, layer), res=x1)

    dx, loss = _loss_head("loss_head", x, target)
    d_norm = {k: [None, None] for k in ("mix_pre", "mix_post", "ffn_pre", "ffn_post")}
    d_rpb = None

    d_f, d_norm["ffn_post"][1] = _rms_bwd("b1_post_ffn", saved[1][7], _row(ffn_post, 1), [dx], out_dtype=BF16)
    for layer in (1, 0):
        tag = f"b{layer}"
        x0, mixer, a, x1, (h_t, o_t, h2_t, act_t), hg, hu, f, w_qkv, w_o, w_gu, w_down = saved[layer]
        dgu, d_h2, d_down = _ffn_bwd(tag + "_ffn", d_f, w_gu, w_down, hg, hu, act_t)
        d_gu = _ffn_dgu(tag + "_ffn_dgu", h2_t, dgu)
        sent = emit(f"ffn{layer}", [d_gu, d_down])
        dx1, d_a, d_norm["ffn_pre"][layer], d_norm["mix_post"][layer] = _norm_post_bwd(
            tag + "_norm_ffn", x1, _row(ffn_pre, layer), [d_h2], dx, a, _row(mix_post, layer), deps=sent)
        d_wo = _square(tag + "_proj_dw", o_t, d_a, NN, BF16)
        if layer == 0:
            _, qkv, o, lse, table = mixer
            d_o = _square(tag + "_proj_do", d_a, w_o, NT, BF16)
            dqkv, gp = _na_bwd(qkv, table, d_o, lse)
            d_rpb = _rpb_grad(gp)[:, :15, :31]
            sent = emit("na", [_qkv_dw(tag + "_qkv_dw", h_t, dqkv), d_wo])
            d_h = _qkv_dh(tag + "_qkv_dh", dqkv, w_qkv, 3, sent)
            dx, d_norm["mix_pre"][layer] = _rms_bwd(tag + "_norm_mix", x0, _row(mix_pre, layer), [d_h[0]], res=dx1)
        else:
            _, qkv, merged, lses, lse_total = mixer
            d_o = _proj_do_sorted(tag + "_proj_do", d_a, w_o)
            dqkv = lax.empty((3 * len(DIL_GROUPS), SEQ, D_MODEL), BF16)
            for g, (_, dil) in enumerate(DIL_GROUPS):
                dqkv = _dil_bwd(g, qkv, slopes * dil, d_o[g], merged[g], lses[g], lse_total[g], dqkv)
            sent = emit("dil", [_qkv_dw(tag + "_qkv_dw", h_t, dqkv), d_wo])
            d_h = _qkv_dh(tag + "_qkv_dh", dqkv, w_qkv, 9, sent)
            dx, d_f, d_norm["mix_pre"][1], d_norm["ffn_post"][0] = _norm_post_bwd(
                tag + "_norm_mix", x0, _row(mix_pre, 1), None, dx1, saved[0][7], _row(ffn_post, 0), groups=d_h)

    d_gains = [jnp.concatenate(d_norm[k], axis=0) for k in ("mix_pre", "mix_post", "ffn_pre", "ffn_post")]
    return loss, dx, d_gains, d_rpb


RPB_SIZE = N_HEADS * 15 * 31


def _pack_small(gains, rpb, last=None):
    top = jnp.concatenate(gains, axis=0).reshape(64, 128)
    bottom = jnp.pad(rpb.reshape(-1), (0, 64 * 128 - RPB_SIZE))
    if last is not None:
        bottom = bottom + jnp.pad(last.reshape(1), (64 * 128 - 1, 0))
    return jnp.concatenate([top, bottom.reshape(64, 128)], axis=0)


def _unpack_small(p):
    gains = p[:64].reshape(4, 2, D_MODEL)
    rpb = p[64:].reshape(-1)[:RPB_SIZE].reshape(1, N_HEADS, 15, 31)
    return [gains[i] for i in range(4)], rpb


GROUPS = ("na", "ffn0", "dil", "ffn1")


def kernel(x, norm_mix_pre, norm_mix_post, norm_ffn_pre, norm_ffn_post, na_w_qkv, na_w_o, na_rpb, dil_w_qkv, dil_w_o, ffn_w_gate, ffn_w_up, ffn_w_down, loss_target, m_norm_mix_pre, m_norm_mix_post, m_norm_ffn_pre, m_norm_ffn_post, m_na_w_qkv, m_na_w_o, m_na_rpb, m_dil_w_qkv, m_dil_w_o, m_ffn_w_gate, m_ffn_w_up, m_ffn_w_down, v_norm_mix_pre, v_norm_mix_post, v_norm_ffn_pre, v_norm_ffn_post, v_na_w_qkv, v_na_w_o, v_na_rpb, v_dil_w_qkv, v_dil_w_o, v_ffn_w_gate, v_ffn_w_up, v_ffn_w_down):
    na_cols, dil_cols, o_rows = 3 * D_MODEL // N_DEV, 9 * D_MODEL // N_DEV, D_MODEL // N_DEV
    ff_pad = FF_PAD - FF_SHARD
    me = (4 * lax.axis_index("x") + 2 * lax.axis_index("y") + lax.axis_index("c")).astype(jnp.int32).reshape(1)

    full = {
        "na": [((D_MODEL, 3 * D_MODEL), _columns(na_cols)), ((N_DEV, o_rows, D_MODEL), _leading)],
        "dil": [((D_MODEL, 9 * D_MODEL), _columns(dil_cols)), ((N_DEV, o_rows, D_MODEL), _leading)],
        "ffn0": [((N_DEV, 2, D_MODEL, FF_PAD), _leading), ((N_DEV, FF_PAD, D_MODEL), _leading)],
        "ffn1": [((N_DEV, 2, D_MODEL, FF_PAD), _leading), ((N_DEV, FF_PAD, D_MODEL), _leading)],
    }
    block = {
        "na": [(D_MODEL, na_cols), (o_rows, D_MODEL)], "dil": [(D_MODEL, dil_cols), (o_rows, D_MODEL)],
        "ffn0": [(2, D_MODEL, FF_PAD), (FF_PAD, D_MODEL)], "ffn1": [(2, D_MODEL, FF_PAD), (FF_PAD, D_MODEL)],
    }

    land_shapes = [jax.ShapeDtypeStruct(full[g][t][0], BF16) for g in GROUPS for t in range(2)]
    windows = [full[g][t][1] for g in GROUPS for t in range(2)]
    shards, lands = _prep_weights(me, na_w_qkv, na_w_o, dil_w_qkv, dil_w_o, ffn_w_gate, ffn_w_up, ffn_w_down, land_shapes)
    sems, shards, lands = _gather_start("gather_start", shards, lands, windows, [2] * len(GROUPS))

    def fetch(group, early, late):
        gi = GROUPS.index(group)
        mine = slice(2 * gi, 2 * gi + 2)
        pass_sems, shards_g, lands_g = _gather_pass_on(f"gather_pass_{group}", sems[gi], shards[mine], lands[mine],
                                                       windows[mine], early)
        qkv, o = _gather_wait(f"gather_wait_{group}", sems[gi], pass_sems, shards_g, lands_g, windows[mine], late)
        return (qkv, o.reshape(D_MODEL, D_MODEL)) if group in ("na", "dil") else (qkv, o)

    def grad_source(group, t):
        return _columns(block[group][0][1]) if (group in ("na", "dil") and t == 0) else _leading

    in_flight = {}

    def emit(group, grads):
        if group in ("na", "dil"):
            grads = [grads[0], grads[1].reshape(N_DEV, o_rows, D_MODEL)]
        sets = [(t, grad_source(group, t), t, _by_distance) for t in range(2)]
        landing = [lax.empty((N_DEV - 1,) + block[group][t], BF16) for t in range(2)]
        sems_g, grads, landing, tok = _send_start(f"exchange_start_{group}", grads, landing, [sets])
        in_flight[group] = (sems_g[0], grads, landing, sets)
        return [tok]

    norms = (norm_mix_pre, norm_mix_post, norm_ffn_pre, norm_ffn_post)
    loss, grad_x, d_gains, d_rpb = _local_step(x[0], loss_target[0], norms, na_rpb[0], fetch, emit, [shards[0]])

    landed, sent = {}, {}

    def wait_for(group, after):
        sems_g, grads, landing, sets = in_flight[group]
        sent[group], landed[group] = _send_wait(f"exchange_wait_{group}", sems_g, grads, landing, sets, after)

    for group in ("ffn1", "dil", "ffn0"):
        wait_for(group, [grad_x])

    def one(rows, tile, ncols, columns):
        own = (pl.BlockSpec((tile, ncols), lambda i, me: (i, me[0])) if columns
               else pl.BlockSpec((None, tile, ncols), lambda i, me: (me[0], i, 0)))
        return dict(grid=(rows // tile,), land_specs=[pl.BlockSpec((N_DEV - 1, tile, ncols), lambda i, me: (0, i, 0))],
                    own_specs=[own], p_spec=pl.BlockSpec((None, tile, ncols), lambda i, me: (0, i, 0)))

    def layered(block_shape, index, p_block, n_tiles):
        def specs(lead_size, lead):
            shape = (lead_size,) + block_shape
            return [pl.BlockSpec(shape, lambda l, r, me: index(lead(me), jnp.where(l == 0, r, n_tiles - 1))),
                    pl.BlockSpec(shape, lambda l, r, me: index(lead(me), jnp.where(l == 0, 0, r)))]
        return dict(grid=(2, n_tiles), land_specs=specs(N_DEV - 1, lambda me: 0), own_specs=specs(None, lambda me: me[0]),
                    p_spec=pl.BlockSpec(p_block, lambda l, r, me: (l, r, 0)))

    gu_lands, gu_owns = [landed["ffn0"][0], landed["ffn1"][0]], [sent["ffn0"][0], sent["ffn1"][0]]
    down_lands, down_owns = [landed["ffn0"][1], landed["ffn1"][1]], [sent["ffn0"][1], sent["ffn1"][1]]
    updates = {
        "dil_w_qkv": _adamw("adamw_dil_qkv", me, [landed["dil"][0]], [sent["dil"][0]], dil_w_qkv, m_dil_w_qkv, v_dil_w_qkv,
                            **one(D_MODEL, 128, dil_cols, True)),
        "dil_w_o": _adamw("adamw_dil_o", me, [landed["dil"][1]], [sent["dil"][1]], dil_w_o, m_dil_w_o, v_dil_w_o,
                          **one(o_rows, o_rows, D_MODEL, False)),
        "ffn_w_gate": _adamw("adamw_gate", me, gu_lands, gu_owns, ffn_w_gate, m_ffn_w_gate, v_ffn_w_gate,
                             **layered((None, 128, FF_PAD), lambda lead, r: (lead, 0, r, 0), (None, 128, FF_SHARD), 8)),
        "ffn_w_up": _adamw("adamw_up", me, gu_lands, gu_owns, ffn_w_up, m_ffn_w_up, v_ffn_w_up,
                           **layered((None, 128, FF_PAD), lambda lead, r: (lead, 1, r, 0), (None, 128, FF_SHARD), 8)),
        "ffn_w_down": _adamw("adamw_down", me, down_lands, down_owns, ffn_w_down, m_ffn_w_down, v_ffn_w_down,
                             **layered((176, D_MODEL), lambda lead, r: (lead, r, 0), (None, 176, D_MODEL), 2)),
    }
    done = [u[0] for u in updates.values()]
    small = _all_gather("gather_small", [_pack_small(d_gains, d_rpb, loss)], [jax.ShapeDtypeStruct((N_DEV, 128, 128), F32)],
                        [_leading], deps=done)[0]
    wait_for("na", [small])
    updates["na_w_qkv"] = _adamw("adamw_na_qkv", me, [landed["na"][0]], [sent["na"][0]], na_w_qkv, m_na_w_qkv, v_na_w_qkv,
                                 **one(D_MODEL, 256, na_cols, True))
    updates["na_w_o"] = _adamw("adamw_na_o", me, [landed["na"][1]], [sent["na"][1]], na_w_o, m_na_w_o, v_na_w_o,
                               **one(o_rows, o_rows, D_MODEL, False))
    gains = [norm_mix_pre, norm_mix_post, norm_ffn_pre, norm_ffn_post]
    m_gains = [m_norm_mix_pre, m_norm_mix_post, m_norm_ffn_pre, m_norm_ffn_post]
    v_gains = [v_norm_mix_pre, v_norm_mix_post, v_norm_ffn_pre, v_norm_ffn_post]
    packed = _adamw("adamw_small", me, [small], (), _pack_small(gains, na_rpb)[None], _pack_small(m_gains, m_na_rpb)[None],
                    _pack_small(v_gains, v_na_rpb)[None], grid=(1,),
                    land_specs=[pl.BlockSpec((N_DEV, 128, 128), lambda i, me: (0, 0, 0))], own_specs=[],
                    p_spec=pl.BlockSpec((None, 128, 128), lambda i, me: (0, 0, 0)))
    small_out = [_unpack_small(p[0]) for p in packed]

    order = ["na_w_qkv", "na_w_o", "na_rpb", "dil_w_qkv", "dil_w_o", "ffn_w_gate", "ffn_w_up", "ffn_w_down"]
    result = [packed[0][0, 127, 127], grad_x[None]]
    for kind in range(4):
        gains_k, rpb_k = small_out[kind]
        result += gains_k
        result += [rpb_k if name == "na_rpb" else updates[name][kind] for name in order]
    return tuple(result)
```

```python
import jax
import jax.numpy as jnp
from jax import lax
from jax.experimental import pallas as pl
from jax.experimental.pallas import tpu as pltpu

F32 = jnp.float32
BF16 = jnp.bfloat16
MESH = pl.DeviceIdType.MESH
ANY = pl.BlockSpec(memory_space=pl.ANY)
HBM = pl.BlockSpec(memory_space=pltpu.HBM)
SEM = pl.BlockSpec(memory_space=pltpu.SEMAPHORE)
EFFECT = pltpu.SideEffectType.DATAFLOW_SIDE_EFFECTING

N_DEV = 8
SEQ = 2048
D_MODEL = 1024
N_HEADS = 16
HEAD_DIM = 64
GRID_W = 64
NA_ROWS = 8
SEQ_ROWS = SEQ // GRID_W
DIL_GROUPS = ((128, 1), (512, 4), (2048, 16))
BAND = 128
RADIUS = 64
FF_SHARD = 352
FF_PAD = 384
RMS_EPS = 1e-6
NEG_INF = -1e30
Q_SCALE = HEAD_DIM ** -0.5

ADAM_LR = 0.001
ADAM_B1 = 0.9
ADAM_B2 = 0.999
ADAM_EPS = 1e-08
ADAM_WD = 0.01
ADAM_STEP = 10

VMEM_LIMIT = 56 * 1024 * 1024
TM = 512
TM_ROW = 1024
TM_MM = 1024

NN = (((1,), (0,)), ((), ()))
NT = (((1,), (1,)), ((), ()))
TN = (((0,), (0,)), ((), ()))


def _params():
    return pltpu.CompilerParams(vmem_limit_bytes=VMEM_LIMIT)


def _matmul(name, a, b, *, grid, a_spec, b_spec, o_spec, out_shape, dims, acc_shape, deps=(), inner=1):
    nk = grid[-1]
    kaxis = len(grid) - 1

    def body(a_ref, b_ref, *rest):
        o_ref, acc_ref = rest[-2], rest[-1]
        if inner == 1:
            a_all = a_ref[...].astype(BF16)
        else:
            a_all = jnp.concatenate([a_ref[j].astype(BF16) for j in range(inner)], axis=1)
        part = lax.dot_general(a_all, b_ref[...].astype(BF16), dims, preferred_element_type=F32)
        if nk == 1:
            o_ref[...] = part.astype(o_ref.dtype)
        else:
            k = pl.program_id(kaxis)

            @pl.when(k == 0)
            def _():
                acc_ref[...] = part

            @pl.when(k > 0)
            def _():
                acc_ref[...] += part

            @pl.when(k == nk - 1)
            def _():
                o_ref[...] = acc_ref[...].astype(o_ref.dtype)

    return pl.pallas_call(
        body, out_shape=out_shape, grid=grid, in_specs=[a_spec, b_spec] + [ANY] * len(deps), out_specs=o_spec,
        scratch_shapes=[pltpu.VMEM(acc_shape, F32)], name=name, compiler_params=_params())(a, b, *deps)


SORTED = tuple(d for _, d in DIL_GROUPS if d > 1)
LANE_CHUNKS = D_MODEL // 128


def _sort_scratch(tm=TM):
    return pltpu.VMEM((LANE_CHUNKS, tm, 128), F32)


def _sorted_view(t, dil):
    return t.reshape(dil, SEQ // dil, D_MODEL)


def _sorted_spec(dil, lead=(), tm=TM):
    return pl.BlockSpec((None,) * len(lead) + (dil, tm // dil, D_MODEL), lambda i: tuple(lead) + (0, i, 0))


def _sort_tile(scratch, value, dil, out_ref):
    tm = value.shape[0]
    for c in range(LANE_CHUNKS):
        scratch[c] = value[:, 128 * c:128 * (c + 1)]
    for r in range(dil):
        rows = [scratch.at[c][pl.ds(r, tm // dil, stride=dil), :] for c in range(LANE_CHUNKS)]
        out_ref[r] = jnp.concatenate(rows, axis=1).astype(out_ref.dtype)


def _unsort_tile(scratch, in_ref, dil):
    for r in range(dil):
        value = in_ref[r].astype(F32)
        for c in range(LANE_CHUNKS):
            scratch.at[c][pl.ds(r, value.shape[0], stride=dil), :] = value[:, 128 * c:128 * (c + 1)]
    return jnp.concatenate([scratch[c] for c in range(LANE_CHUNKS)], axis=1)


def _rms_fwd(name, x, g, res=None, out_dtype=F32, deps=(), sorted_too=False):
    tm = TM if sorted_too else TM_ROW
    n_tiles = SEQ // tm
    has_res = res is not None
    n_in = 2 + has_res + len(deps)

    def body(*refs):
        x_ref, g_ref = refs[0], refs[1]
        xv = x_ref[...]
        r = lax.rsqrt(jnp.mean(xv * xv, axis=-1, keepdims=True) + RMS_EPS)
        y = xv * r * g_ref[...]
        if has_res:
            y = refs[2][...] + y
        refs[n_in][...] = y.astype(out_dtype)
        if sorted_too:
            for j, dil in enumerate(SORTED):
                _sort_tile(refs[-1], y, dil, refs[n_in + 1 + j])

    tile = pl.BlockSpec((tm, D_MODEL), lambda i: (i, 0))
    gspec = pl.BlockSpec((1, D_MODEL), lambda i: (0, 0))
    ins = [x, g] + ([res] if has_res else []) + list(deps)
    specs = [tile, gspec] + ([tile] if has_res else []) + [ANY] * len(deps)
    shapes, out_specs = [jax.ShapeDtypeStruct((SEQ, D_MODEL), out_dtype)], [tile]
    if sorted_too:
        shapes += [jax.ShapeDtypeStruct((dil, SEQ // dil, D_MODEL), out_dtype) for dil in SORTED]
        out_specs += [_sorted_spec(dil) for dil in SORTED]
    result = pl.pallas_call(
        body, out_shape=tuple(shapes), grid=(n_tiles,), in_specs=specs, out_specs=tuple(out_specs),
        scratch_shapes=[_sort_scratch()] if sorted_too else [], name=name, compiler_params=_params())(*ins)
    return result if sorted_too else result[0]


def _rms_bwd(name, x, g, dys, res=None, out_dtype=F32, groups=None, deps=()):
    tm = TM if groups is not None else TM_ROW
    n_tiles = SEQ // tm
    n_dy = len(dys) if groups is None else 1 + len(SORTED)
    has_res = res is not None

    def body(*refs):
        x_ref, g_ref = refs[0], refs[1]
        dy_refs = refs[2:2 + n_dy]
        res_ref = refs[2 + n_dy] if has_res else None
        first_out = 2 + n_dy + has_res + len(deps)
        dx_ref, dg_ref, acc_ref = refs[first_out:first_out + 3]
        i = pl.program_id(0)
        xv = x_ref[...]
        r = lax.rsqrt(jnp.mean(xv * xv, axis=-1, keepdims=True) + RMS_EPS)
        xn = xv * r
        dy = dy_refs[0][...].astype(F32)
        for j, extra in enumerate(dy_refs[1:]):
            dy = dy + (extra[...].astype(F32) if groups is None else _unsort_tile(refs[-1], extra, SORTED[j]))
        dyg = dy * g_ref[...]
        dx = r * (dyg - xn * jnp.mean(dyg * xn, axis=-1, keepdims=True))
        if has_res:
            dx = res_ref[...] + dx
        dx_ref[...] = dx.astype(dx_ref.dtype)
        part = jnp.sum((dy * xn).reshape(tm // 8, 8, D_MODEL), axis=0)

        @pl.when(i == 0)
        def _():
            acc_ref[...] = part

        @pl.when(i > 0)
        def _():
            acc_ref[...] += part

        @pl.when(i == n_tiles - 1)
        def _():
            dg_ref[...] = jnp.broadcast_to(jnp.sum(acc_ref[...], axis=0, keepdims=True), (8, D_MODEL))

    tile = pl.BlockSpec((tm, D_MODEL), lambda i: (i, 0))
    gspec = pl.BlockSpec((1, D_MODEL), lambda i: (0, 0))
    if groups is None:
        dy_ins, dy_specs = list(dys), [tile] * n_dy
    else:
        dy_ins = [groups] + [groups.reshape(n_dy, dil, SEQ // dil, D_MODEL) for dil in SORTED]
        dy_specs = [pl.BlockSpec((None, tm, D_MODEL), lambda i: (0, i, 0))]
        dy_specs += [_sorted_spec(dil, lead=(1 + j,)) for j, dil in enumerate(SORTED)]
    ins = [x, g] + dy_ins + ([res] if has_res else []) + list(deps)
    specs = [tile, gspec] + dy_specs + ([tile] if has_res else []) + [ANY] * len(deps)
    dx, dg = pl.pallas_call(
        body, out_shape=(jax.ShapeDtypeStruct((SEQ, D_MODEL), out_dtype), jax.ShapeDtypeStruct((8, D_MODEL), F32)),
        grid=(n_tiles,), in_specs=specs,
        out_specs=(tile, pl.BlockSpec((8, D_MODEL), lambda i: (0, 0))),
        scratch_shapes=[pltpu.VMEM((8, D_MODEL), F32)] + ([_sort_scratch()] if groups is not None else []),
        name=name, compiler_params=_params())(*ins)
    return dx, dg[0:1]


def _rms(xv):
    return lax.rsqrt(jnp.mean(xv * xv, axis=-1, keepdims=True) + RMS_EPS)


def _post_norm_fwd(name, a, g_post, res, g_next, sorted_too=False):
    def body(a_ref, gp_ref, res_ref, gn_ref, x_ref, h_ref, ht_ref, *rest):
        av = a_ref[...]
        xv = res_ref[...] + av * _rms(av) * gp_ref[...]
        x_ref[...] = xv
        y = xv * _rms(xv) * gn_ref[...]
        h_ref[...] = y.astype(h_ref.dtype)
        ht_ref[...] = y.T.astype(ht_ref.dtype)
        if sorted_too:
            for j, dil in enumerate(SORTED):
                _sort_tile(rest[-1], y, dil, rest[j])

    tile = pl.BlockSpec((TM, D_MODEL), lambda i: (i, 0))
    gspec = pl.BlockSpec((1, D_MODEL), lambda i: (0, 0))
    shapes = [jax.ShapeDtypeStruct((SEQ, D_MODEL), F32), jax.ShapeDtypeStruct((SEQ, D_MODEL), BF16),
              jax.ShapeDtypeStruct((D_MODEL, SEQ), BF16)]
    out_specs = [tile, tile, pl.BlockSpec((D_MODEL, TM), lambda i: (0, i))]
    if sorted_too:
        shapes += [jax.ShapeDtypeStruct((dil, SEQ // dil, D_MODEL), BF16) for dil in SORTED]
        out_specs += [_sorted_spec(dil) for dil in SORTED]
    result = pl.pallas_call(
        body, out_shape=tuple(shapes), grid=(SEQ // TM,), in_specs=[tile, gspec, tile, gspec], out_specs=tuple(out_specs),
        scratch_shapes=[_sort_scratch()] if sorted_too else [], name=name, compiler_params=_params())(a, g_post, res, g_next)
    return result[0], list(result[1:])


def _norm_post_bwd(name, x, g_norm, dys, res, a, g_post, groups=None, deps=()):
    n_tiles = SEQ // TM
    n_dy = len(dys) if groups is None else 1 + len(SORTED)

    def body(*refs):
        x_ref, g_ref = refs[0], refs[1]
        dy_refs = refs[2:2 + n_dy]
        res_ref, a_ref, gp_ref = refs[2 + n_dy:5 + n_dy]
        dx_ref, da_ref, dg_ref, dgp_ref, acc_ref, accp_ref = refs[5 + n_dy + len(deps):11 + n_dy + len(deps)]
        i = pl.program_id(0)
        xv = x_ref[...]
        r = _rms(xv)
        xn = xv * r
        dy = dy_refs[0][...].astype(F32)
        for j, extra in enumerate(dy_refs[1:]):
            dy = dy + (extra[...].astype(F32) if groups is None else _unsort_tile(refs[-1], extra, SORTED[j]))
        dyg = dy * g_ref[...]
        dx = res_ref[...] + r * (dyg - xn * jnp.mean(dyg * xn, axis=-1, keepdims=True))
        dx_ref[...] = dx
        av = a_ref[...]
        ra = _rms(av)
        an = av * ra
        dxg = dx * gp_ref[...]
        da_ref[...] = (ra * (dxg - an * jnp.mean(dxg * an, axis=-1, keepdims=True))).astype(da_ref.dtype)
        part = jnp.sum((dy * xn).reshape(TM // 8, 8, D_MODEL), axis=0)
        part_p = jnp.sum((dx * an).reshape(TM // 8, 8, D_MODEL), axis=0)

        @pl.when(i == 0)
        def _():
            acc_ref[...] = part
            accp_ref[...] = part_p

        @pl.when(i > 0)
        def _():
            acc_ref[...] += part
            accp_ref[...] += part_p

        @pl.when(i == n_tiles - 1)
        def _():
            dg_ref[...] = jnp.broadcast_to(jnp.sum(acc_ref[...], axis=0, keepdims=True), (8, D_MODEL))
            dgp_ref[...] = jnp.broadcast_to(jnp.sum(accp_ref[...], axis=0, keepdims=True), (8, D_MODEL))

    tile = pl.BlockSpec((TM, D_MODEL), lambda i: (i, 0))
    gspec = pl.BlockSpec((1, D_MODEL), lambda i: (0, 0))
    gain = pl.BlockSpec((8, D_MODEL), lambda i: (0, 0))
    if groups is None:
        dy_ins, dy_specs = list(dys), [tile] * n_dy
    else:
        dy_ins = [groups] + [groups.reshape(n_dy, dil, SEQ // dil, D_MODEL) for dil in SORTED]
        dy_specs = [pl.BlockSpec((None, TM, D_MODEL), lambda i: (0, i, 0))]
        dy_specs += [_sorted_spec(dil, lead=(1 + j,)) for j, dil in enumerate(SORTED)]
    dx, da, dg, dgp = pl.pallas_call(
        body,
        out_shape=(jax.ShapeDtypeStruct((SEQ, D_MODEL), F32), jax.ShapeDtypeStruct((SEQ, D_MODEL), BF16),
                   jax.ShapeDtypeStruct((8, D_MODEL), F32), jax.ShapeDtypeStruct((8, D_MODEL), F32)),
        grid=(n_tiles,), in_specs=[tile, gspec] + dy_specs + [tile, tile, gspec] + [ANY] * len(deps),
        out_specs=(tile, tile, gain, gain),
        scratch_shapes=[pltpu.VMEM((8, D_MODEL), F32)] * 2 + ([_sort_scratch()] if groups is not None else []),
        name=name, compiler_params=_params())(x, g_norm, *dy_ins, res, a, g_post, *deps)
    return dx, da, dg[0:1], dgp[0:1]


def _loss_head(name, y, target):
    tm = TM_ROW
    n_tiles = SEQ // tm

    def body(y_ref, t_ref, dy_ref, loss_ref, acc_ref):
        i = pl.program_id(0)
        diff = y_ref[...] - t_ref[...]
        dy_ref[...] = diff * (1.0 / D_MODEL)
        part = jnp.sum((diff * diff).reshape(tm // 8, 8, D_MODEL), axis=0)

        @pl.when(i == 0)
        def _():
            acc_ref[...] = part

        @pl.when(i > 0)
        def _():
            acc_ref[...] += part

        @pl.when(i == n_tiles - 1)
        def _():
            loss_ref[...] = jnp.full((8, 128), jnp.sum(acc_ref[...]) * (0.5 / D_MODEL), F32)

    tile = pl.BlockSpec((tm, D_MODEL), lambda i: (i, 0))
    dy, loss = pl.pallas_call(
        body, out_shape=(jax.ShapeDtypeStruct((SEQ, D_MODEL), F32), jax.ShapeDtypeStruct((8, 128), F32)),
        grid=(n_tiles,), in_specs=[tile, tile], out_specs=(tile, pl.BlockSpec((8, 128), lambda i: (0, 0))),
        scratch_shapes=[pltpu.VMEM((8, D_MODEL), F32)], name=name, compiler_params=_params())(y, target)
    return dy, loss[0, 0]


def _row_index(shape):
    return lax.broadcasted_iota(jnp.int32, shape, 0)


def _lane_index(shape):
    return lax.broadcasted_iota(jnp.int32, shape, len(shape) - 1)


def _skew_rows(t, direction):
    q = _row_index(t.shape) & (GRID_W - 1)
    for bit in range(6):
        step = 1 << bit
        shift = step if direction > 0 else 128 - step
        t = jnp.where((q & step) != 0, pltpu.roll(t, shift, 1), t)
    return t


RPB_HEADS = 4


def _rpb_table(rpb_pad):
    rows = RPB_HEADS * 16 * GRID_W

    def body(r_ref, t_ref):
        lane = _lane_index((rows, 128))
        v = pltpu.roll(r_ref[...].reshape(RPB_HEADS * 16, 128), 128 - 15, 1)
        t = _skew_rows(jnp.broadcast_to(v[:, None, :], (RPB_HEADS * 16, GRID_W, 128)).reshape(rows, 128), +1)
        t = jnp.where(lane < GRID_W, t, 0.0)
        below = jnp.concatenate([t[GRID_W:], jnp.zeros((GRID_W, 128), F32)], axis=0)
        first_col = jnp.clip((_row_index((rows, 128)) & (GRID_W - 1)) - 8, 0, GRID_W - 16)
        key_col = lane & (GRID_W - 1)
        in_window = (key_col >= first_col) & (key_col < first_col + 16)
        t_ref[...] = jnp.where(in_window, t + pltpu.roll(below, GRID_W, 1), NEG_INF).reshape(RPB_HEADS, 16, GRID_W, 128)

    return pl.pallas_call(
        body, out_shape=jax.ShapeDtypeStruct((N_HEADS, 16, GRID_W, 128), F32), grid=(N_HEADS // RPB_HEADS,),
        in_specs=[pl.BlockSpec((RPB_HEADS, 16, 128), lambda h: (h, 0, 0))],
        out_specs=pl.BlockSpec((RPB_HEADS, 16, GRID_W, 128), lambda h: (h, 0, 0, 0)),
        name="rpb_table", compiler_params=_params())(rpb_pad)


def _rpb_grad(gp):
    rows = RPB_HEADS * 16 * GRID_W

    def body(g_ref, o_ref):
        lane = _lane_index((rows, 128))
        g = g_ref[...].reshape(rows, 128)
        low = jnp.where(lane < GRID_W, g, 0.0)
        high = pltpu.roll(jnp.where(lane >= GRID_W, g, 0.0), GRID_W, 1)
        above = jnp.concatenate([jnp.zeros((GRID_W, 128), F32), high[:rows - GRID_W]], axis=0)
        diag = jnp.sum(_skew_rows(low + above, -1).reshape(RPB_HEADS * 16, GRID_W, 128), axis=1)
        o_ref[...] = pltpu.roll(diag, 15, 1).reshape(RPB_HEADS, 16, 128)

    return pl.pallas_call(
        body, out_shape=jax.ShapeDtypeStruct((N_HEADS, 16, 128), F32), grid=(N_HEADS // RPB_HEADS,),
        in_specs=[pl.BlockSpec((RPB_HEADS, 16, GRID_W, 128), lambda h: (h, 0, 0, 0))],
        out_specs=pl.BlockSpec((RPB_HEADS, 16, 128), lambda h: (h, 0, 0)),
        name="rpb_grad", compiler_params=_params())(gp)


NA_KEYS = NA_ROWS * GRID_W


def _na_window(i):
    first_row = jnp.clip(i - NA_ROWS // 2, 0, SEQ_ROWS - NA_ROWS)
    return pl.multiple_of(first_row * GRID_W, GRID_W), first_row - i + NA_ROWS - 1


NA_STEP = 32


def _head_masks():
    lane = _lane_index((1, 128))
    return (lane < HEAD_DIM, lane >= HEAD_DIM)


def _stack_heads(t, masks):
    zero = jnp.zeros_like(t)
    return jnp.concatenate([jnp.where(masks[0], t, zero), jnp.where(masks[1], t, zero)], axis=0)


def _unstack_heads(t, masks):
    n = t.shape[0] // 2
    return jnp.where(masks[0], t[:n], t[n:])


def _stack_columns(t):
    return jnp.concatenate([t[:, 0:1], t[:, HEAD_DIM:HEAD_DIM + 1]], axis=0)


def _na_scores(qs, kw, tp_ref, dr0):
    s = lax.dot_general(qs, kw, NT, preferred_element_type=F32)
    bias = jnp.concatenate(
        [jnp.concatenate([tp_ref[a, pl.ds(dr0 + 2 * c, 1), :, :].reshape(GRID_W, 128) for c in range(4)], axis=1)
         for a in range(2)], axis=0)
    return s + bias


def _na_specs():
    q_spec = pl.BlockSpec((NA_STEP * GRID_W, 128), lambda hp, i: (i, hp))
    k_spec = pl.BlockSpec((SEQ, 128), lambda hp, i: (0, 8 + hp))
    v_spec = pl.BlockSpec((SEQ, 128), lambda hp, i: (0, 16 + hp))
    tp_spec = pl.BlockSpec((2, 16, GRID_W, 128), lambda hp, i: (hp, 0, 0, 0))
    return q_spec, k_spec, v_spec, tp_spec


def _na_fwd(qkv, table):
    def body(q_ref, k_ref, v_ref, tp_ref, o_ref, lse_ref):
        masks = _head_masks()
        for r in range(NA_STEP):
            rows = slice(r * GRID_W, (r + 1) * GRID_W)
            start, dr0 = _na_window(pl.program_id(1) * NA_STEP + r)
            kw = k_ref[pl.ds(start, NA_KEYS), :]
            vw = v_ref[pl.ds(start, NA_KEYS), :]
            s = _na_scores(_stack_heads(q_ref[rows, :] * Q_SCALE, masks), kw, tp_ref, dr0)
            m = jnp.max(s, axis=-1, keepdims=True)
            p = jnp.exp(s - m)
            denom = jnp.sum(p, axis=-1, keepdims=True)
            out = jnp.dot(p.astype(BF16), vw, preferred_element_type=F32) / denom
            o_ref[rows, :] = _unstack_heads(out, masks).astype(o_ref.dtype)
            lse_ref[rows, :] = _unstack_heads(jnp.broadcast_to(m + jnp.log(denom), (2 * GRID_W, 128)), masks)

    q_spec, k_spec, v_spec, tp_spec = _na_specs()
    return pl.pallas_call(
        body, out_shape=(jax.ShapeDtypeStruct((SEQ, D_MODEL), BF16), jax.ShapeDtypeStruct((SEQ, D_MODEL), F32)),
        grid=(N_HEADS // 2, SEQ_ROWS // NA_STEP), in_specs=[q_spec, k_spec, v_spec, tp_spec],
        out_specs=(q_spec, q_spec), name="na_fwd", compiler_params=_params())(qkv, qkv, qkv, table)


def _na_bwd(qkv, table, d_out, lse):
    def body(q_ref, k_ref, v_ref, tp_ref, do_ref, lse_ref, dqkv_ref, gp_ref, dk_acc, dv_acc):
        step = pl.program_id(1)

        @pl.when(step == 0)
        def _():
            dk_acc[...] = jnp.zeros_like(dk_acc)
            dv_acc[...] = jnp.zeros_like(dv_acc)
            gp_ref[...] = jnp.zeros_like(gp_ref)

        masks = _head_masks()
        for r in range(NA_STEP):
            rows = slice(r * GRID_W, (r + 1) * GRID_W)
            i = step * NA_STEP + r
            start, dr0 = _na_window(i)
            kw = k_ref[pl.ds(start, NA_KEYS), :]
            vw = v_ref[pl.ds(start, NA_KEYS), :]
            qs = _stack_heads(q_ref[rows, :] * Q_SCALE, masks)
            dos = _stack_heads(do_ref[rows, :], masks)
            p = jnp.exp(_na_scores(qs, kw, tp_ref, dr0) - _stack_columns(lse_ref[rows, :]))
            dp = lax.dot_general(dos, vw, NT, preferred_element_type=F32)
            ds = p * (dp - jnp.sum(p * dp, axis=-1, keepdims=True))
            for a in range(2):
                for c in range(4):
                    gp_ref[a, pl.ds(dr0 + 2 * c, 1), :, :] += (
                        ds[a * GRID_W:(a + 1) * GRID_W, 128 * c:128 * (c + 1)].reshape(1, GRID_W, 128))
            dsb = ds.astype(BF16)
            dq = _unstack_heads(jnp.dot(dsb, kw, preferred_element_type=F32), masks) * Q_SCALE
            dqkv_ref[0, pl.ds(pl.multiple_of(i * GRID_W, GRID_W), GRID_W), :] = dq.astype(dqkv_ref.dtype)
            dk_acc[pl.ds(start, NA_KEYS), :] += lax.dot_general(dsb, qs, TN, preferred_element_type=F32)
            dv_acc[pl.ds(start, NA_KEYS), :] += lax.dot_general(p.astype(BF16), dos, TN, preferred_element_type=F32)

        @pl.when(step == SEQ_ROWS // NA_STEP - 1)
        def _():
            dqkv_ref[1] = dk_acc[...].astype(dqkv_ref.dtype)
            dqkv_ref[2] = dv_acc[...].astype(dqkv_ref.dtype)

    q_spec, k_spec, v_spec, tp_spec = _na_specs()
    return pl.pallas_call(
        body,
        out_shape=(jax.ShapeDtypeStruct((3, SEQ, D_MODEL), BF16), jax.ShapeDtypeStruct((N_HEADS, 16, GRID_W, 128), F32)),
        grid=(N_HEADS // 2, SEQ_ROWS // NA_STEP), in_specs=[q_spec, k_spec, v_spec, tp_spec, q_spec, q_spec],
        out_specs=(pl.BlockSpec((3, SEQ, 128), lambda hp, i: (0, 0, hp)), tp_spec),
        scratch_shapes=[pltpu.VMEM((SEQ, 128), F32), pltpu.VMEM((SEQ, 128), F32)],
        name="na_bwd", compiler_params=_params())(qkv, qkv, qkv, table, d_out, lse)


DIL_STEP = 16


def _dil_geometry(group):
    dil = DIL_GROUPS[group][1]
    sub_len = SEQ // dil
    blocks = sub_len // BAND
    return dil, sub_len, max(blocks // DIL_STEP, 1), max(DIL_STEP // blocks, 1), min(2 * BAND, sub_len)


def _dil_block(step, r, sub_len, subs):
    per_sub = DIL_STEP // subs
    return (r // per_sub) * sub_len, step * per_sub + r % per_sub


def _dil_window(b, sub_len, n_keys):
    if n_keys == sub_len:
        return 0
    return pl.multiple_of(jnp.clip(b * BAND - RADIUS, 0, sub_len - n_keys), RADIUS)


def _dil_bias(b, start, n_keys, slope_ref, hp):
    row = _row_index((2 * BAND, n_keys))
    qpos = b * BAND + (row & (BAND - 1))
    kpos = start + _lane_index((2 * BAND, n_keys))
    dist = jnp.abs(qpos - kpos)
    slope = jnp.where(row < BAND, slope_ref[2 * hp], slope_ref[2 * hp + 1])
    return slope * dist.astype(F32), dist <= RADIUS


def _dil_scores(qs, kw, penalty, valid):
    return jnp.where(valid, lax.dot_general(qs, kw, NT, preferred_element_type=F32) - penalty, NEG_INF)


def _dil_specs(group):
    dil, sub_len, steps, subs, _ = _dil_geometry(group)
    col = group * 24
    rows = DIL_STEP * BAND
    q_spec = pl.BlockSpec((rows, 128), lambda n, hp, b: (n * steps + b, col + hp))
    k_spec = pl.BlockSpec((subs * sub_len, 128), lambda n, hp, b: (n, col + 8 + hp))
    v_spec = pl.BlockSpec((subs * sub_len, 128), lambda n, hp, b: (n, col + 16 + hp))
    tile = pl.BlockSpec((rows, 128), lambda n, hp, b: (n * steps + b, hp))
    smem = pl.BlockSpec(memory_space=pltpu.SMEM)
    return (dil // subs, N_HEADS // 2, steps), q_spec, k_spec, v_spec, tile, smem


def _dil_fwd(group, qkv, slopes):
    _, sub_len, _, subs, n_keys = _dil_geometry(group)

    def body(q_ref, k_ref, v_ref, slope_ref, o_ref, lse_ref):
        hp = pl.program_id(1)
        masks = _head_masks()
        for r in range(DIL_STEP):
            rows = slice(r * BAND, (r + 1) * BAND)
            base, b = _dil_block(pl.program_id(2), r, sub_len, subs)
            start = _dil_window(b, sub_len, n_keys)
            kw = k_ref[pl.ds(base + start, n_keys), :]
            vw = v_ref[pl.ds(base + start, n_keys), :]
            penalty, valid = _dil_bias(b, start, n_keys, slope_ref, hp)
            s = _dil_scores(_stack_heads(q_ref[rows, :] * Q_SCALE, masks), kw, penalty, valid)
            m = jnp.max(s, axis=-1, keepdims=True)
            p = jnp.exp(s - m)
            denom = jnp.sum(p, axis=-1, keepdims=True)
            out = jnp.dot(p.astype(BF16), vw, preferred_element_type=F32) / denom
            o_ref[rows, :] = _unstack_heads(out, masks).astype(o_ref.dtype)
            lse_ref[rows, :] = _unstack_heads(jnp.broadcast_to(m + jnp.log(denom), (2 * BAND, 128)), masks)

    grid, q_spec, k_spec, v_spec, tile, smem = _dil_specs(group)
    return pl.pallas_call(
        body, out_shape=(jax.ShapeDtypeStruct((SEQ, D_MODEL), BF16), jax.ShapeDtypeStruct((SEQ, D_MODEL), F32)),
        grid=grid, in_specs=[q_spec, k_spec, v_spec, smem], out_specs=(tile, tile),
        name=f"dil_fwd_{group}", compiler_params=_params())(qkv, qkv, qkv, slopes)


def _dil_merge(outs, lses):
    n_sorted = len(SORTED)

    def body(*refs):
        o_refs, l_refs = refs[:3], refs[3:6]
        out_refs, lse_refs, scratch = refs[6:7 + n_sorted], refs[7 + n_sorted:8 + 2 * n_sorted], refs[-1]
        os_ = [o_refs[0][...]] + [_unsort_tile(scratch, o_refs[1 + j], dil) for j, dil in enumerate(SORTED)]
        ls = [l_refs[0][...]] + [_unsort_tile(scratch, l_refs[1 + j], dil) for j, dil in enumerate(SORTED)]
        m = jnp.maximum(jnp.maximum(ls[0], ls[1]), ls[2])
        es = [jnp.exp(v - m) for v in ls]
        total = es[0] + es[1] + es[2]
        merged = (es[0] * os_[0] + es[1] * os_[1] + es[2] * os_[2]) / total
        lse = m + jnp.log(total)
        out_refs[0][...] = merged
        lse_refs[0][...] = lse
        for j, dil in enumerate(SORTED):
            _sort_tile(scratch, merged, dil, out_refs[1 + j])
            _sort_tile(scratch, lse, dil, lse_refs[1 + j])

    tm = 256
    tile = pl.BlockSpec((tm, D_MODEL), lambda i: (i, 0))
    specs = [tile] + [_sorted_spec(dil, tm=tm) for dil in SORTED]
    shapes = [jax.ShapeDtypeStruct((SEQ, D_MODEL), F32)] + [jax.ShapeDtypeStruct((dil, SEQ // dil, D_MODEL), F32) for dil in SORTED]
    views = lambda ts: [ts[0]] + [_sorted_view(t, dil) for t, dil in zip(ts[1:], SORTED)]
    result = pl.pallas_call(
        body, out_shape=tuple(shapes * 2), grid=(SEQ // tm,), in_specs=specs * 2, out_specs=tuple(specs * 2),
        scratch_shapes=[_sort_scratch(tm)], name="dil_merge", compiler_params=_params())(*views(outs), *views(lses))
    flat = [t.reshape(SEQ, D_MODEL) for t in result]
    return flat[:1 + n_sorted], flat[1 + n_sorted:]


def _dil_bwd(group, qkv, slopes, d_out, out, lse_group, lse_total, into):
    _, sub_len, steps, subs, n_keys = _dil_geometry(group)

    def body(q_ref, k_ref, v_ref, slope_ref, do_ref, o_ref, lg_ref, lt_ref, into_ref, dqkv_ref, dk_acc, dv_acc):
        hp, step = pl.program_id(1), pl.program_id(2)

        @pl.when(step == 0)
        def _():
            dk_acc[...] = jnp.zeros_like(dk_acc)
            dv_acc[...] = jnp.zeros_like(dv_acc)

        masks = _head_masks()
        for r in range(DIL_STEP):
            rows = slice(r * BAND, (r + 1) * BAND)
            base, b = _dil_block(step, r, sub_len, subs)
            start = _dil_window(b, sub_len, n_keys)
            keys = pl.ds(base + start, n_keys)
            kw = k_ref[keys, :]
            vw = v_ref[keys, :]
            penalty, valid = _dil_bias(b, start, n_keys, slope_ref, hp)
            qs = _stack_heads(q_ref[rows, :] * Q_SCALE, masks)
            lse2 = lg_ref[rows, :]
            weight = jnp.exp(lse2 - lt_ref[rows, :])
            do2 = do_ref[rows, :]
            dogs = _stack_heads((weight * do2).astype(BF16), masks)
            delta = _stack_columns(weight) * jnp.sum(_stack_heads(do2 * o_ref[rows, :], masks), axis=-1, keepdims=True)
            p = jnp.exp(_dil_scores(qs, kw, penalty, valid) - _stack_columns(lse2))
            dp = lax.dot_general(dogs, vw, NT, preferred_element_type=F32)
            dsb = (p * (dp - delta)).astype(BF16)
            dq = _unstack_heads(jnp.dot(dsb, kw, preferred_element_type=F32), masks) * Q_SCALE
            dqkv_ref[0, pl.ds(pl.multiple_of(base + b * BAND, BAND), BAND), :] = dq.astype(dqkv_ref.dtype)
            dk_acc[keys, :] += lax.dot_general(dsb, qs, TN, preferred_element_type=F32)
            dv_acc[keys, :] += lax.dot_general(p.astype(BF16), dogs, TN, preferred_element_type=F32)

        @pl.when(step == steps - 1)
        def _():
            dqkv_ref[1] = dk_acc[...].astype(dqkv_ref.dtype)
            dqkv_ref[2] = dv_acc[...].astype(dqkv_ref.dtype)

    grid, q_spec, k_spec, v_spec, tile, smem = _dil_specs(group)
    return pl.pallas_call(
        body, out_shape=jax.ShapeDtypeStruct(into.shape, into.dtype), grid=grid,
        in_specs=[q_spec, k_spec, v_spec, smem, tile, tile, tile, tile, ANY],
        out_specs=pl.BlockSpec((3, subs * sub_len, 128), lambda n, hp, b: (group, n, hp)),
        scratch_shapes=[pltpu.VMEM((subs * sub_len, 128), F32), pltpu.VMEM((subs * sub_len, 128), F32)],
        input_output_aliases={8: 0}, name=f"dil_bwd_{group}", compiler_params=_params(),
    )(qkv, qkv, qkv, slopes, d_out, out, lse_group, lse_total, into)


def _accumulate_rows(acc_ref, i, first, part):
    rows = pl.ds(pl.multiple_of(i * TM_MM, TM_MM), TM_MM)

    @pl.when(first)
    def _():
        acc_ref[rows, :] = part

    @pl.when(jnp.logical_not(first))
    def _():
        acc_ref[rows, :] += part


def _ffn_specs():
    tile = pl.BlockSpec((TM_MM, D_MODEL), lambda d, i: (i, 0))
    gate = pl.BlockSpec((None, None, D_MODEL, FF_PAD), lambda d, i: (d, 0, 0, 0))
    up = pl.BlockSpec((None, None, D_MODEL, FF_PAD), lambda d, i: (d, 1, 0, 0))
    down = pl.BlockSpec((None, FF_PAD, D_MODEL), lambda d, i: (d, 0, 0))
    hidden = pl.BlockSpec((None, TM_MM, FF_PAD), lambda d, i: (d, i, 0))
    whole = pl.BlockSpec((SEQ, D_MODEL), lambda d, i: (0, 0))
    return tile, gate, up, down, hidden, whole


def _ffn_fwd(name, h, w_gu, w_down):
    def body(h_ref, wg_ref, wu_ref, wd_ref, f_ref, hg_ref, hu_ref, act_t_ref):
        hv = h_ref[...]
        hg = jnp.dot(hv, wg_ref[...], preferred_element_type=F32)
        hu = jnp.dot(hv, wu_ref[...], preferred_element_type=F32)
        act = hg * jax.nn.sigmoid(hg) * hu
        act_t_ref[...] = act.T.astype(act_t_ref.dtype)
        hg_ref[...] = hg.astype(hg_ref.dtype)
        hu_ref[...] = hu.astype(hu_ref.dtype)
        _accumulate_rows(f_ref, pl.program_id(1), pl.program_id(0) == 0,
                         jnp.dot(act.astype(BF16), wd_ref[...], preferred_element_type=F32))

    tile, gate, up, down, hidden, whole = _ffn_specs()
    shape = jax.ShapeDtypeStruct((N_DEV, SEQ, FF_PAD), BF16)
    return pl.pallas_call(
        body, out_shape=(jax.ShapeDtypeStruct((SEQ, D_MODEL), F32), shape, shape, jax.ShapeDtypeStruct((N_DEV, FF_PAD, SEQ), BF16)),
        grid=(N_DEV, SEQ // TM_MM), in_specs=[tile, gate, up, down],
        out_specs=(whole, hidden, hidden, pl.BlockSpec((None, FF_PAD, TM_MM), lambda d, i: (d, 0, i))),
        name=name, compiler_params=_params())(h, w_gu, w_gu, w_down)


def _ffn_dgu(name, h_t, dgu):
    def body(h_ref, dgu_ref, o_ref):
        both = jnp.dot(h_ref[...], jnp.concatenate([dgu_ref[0], dgu_ref[1]], axis=1), preferred_element_type=F32)
        o_ref[0] = both[:, :FF_PAD].astype(o_ref.dtype)
        o_ref[1] = both[:, FF_PAD:].astype(o_ref.dtype)

    return pl.pallas_call(
        body, out_shape=jax.ShapeDtypeStruct((N_DEV, 2, D_MODEL, FF_PAD), BF16), grid=(N_DEV,),
        in_specs=[pl.BlockSpec((D_MODEL, SEQ), lambda d: (0, 0)), pl.BlockSpec((2, None, SEQ, FF_PAD), lambda d: (0, d, 0, 0))],
        out_specs=pl.BlockSpec((None, 2, D_MODEL, FF_PAD), lambda d: (d, 0, 0, 0)),
        name=name, compiler_params=_params())(h_t, dgu)


def _ffn_bwd(name, d_f, w_gu, w_down, hg, hu, act_t):
    n_tiles = SEQ // TM_MM

    def body(df_ref, wg_ref, wu_ref, wd_ref, hg_ref, hu_ref, act_t_ref, dgu_ref, dh_ref, dwd_ref, dwd_acc):
        i = pl.program_id(1)
        d_down = jnp.dot(act_t_ref[...], df_ref[...], preferred_element_type=F32)

        @pl.when(i == 0)
        def _():
            dwd_acc[...] = d_down

        @pl.when(i > 0)
        def _():
            dwd_acc[...] += d_down

        @pl.when(i == n_tiles - 1)
        def _():
            dwd_ref[...] = dwd_acc[...].astype(dwd_ref.dtype)

        dact = lax.dot_general(df_ref[...], wd_ref[...], NT, preferred_element_type=F32)
        hgv = hg_ref[...].astype(F32)
        sig = jax.nn.sigmoid(hgv)
        d_gate = (dact * hu_ref[...].astype(F32) * (sig * (1.0 + hgv * (1.0 - sig)))).astype(BF16)
        d_up = (dact * hgv * sig).astype(BF16)
        dgu_ref[0] = d_gate
        dgu_ref[1] = d_up
        part = lax.dot_general(jnp.concatenate([d_gate, d_up], axis=1), jnp.concatenate([wg_ref[...], wu_ref[...]], axis=1),
                               NT, preferred_element_type=F32)
        _accumulate_rows(dh_ref, pl.program_id(1), pl.program_id(0) == 0, part)

    tile, gate, up, down, hidden, whole = _ffn_specs()
    return pl.pallas_call(
        body, out_shape=(jax.ShapeDtypeStruct((2, N_DEV, SEQ, FF_PAD), BF16), jax.ShapeDtypeStruct((SEQ, D_MODEL), F32),
                         jax.ShapeDtypeStruct((N_DEV, FF_PAD, D_MODEL), BF16)),
        grid=(N_DEV, n_tiles),
        in_specs=[tile, gate, up, down, hidden, hidden, pl.BlockSpec((None, FF_PAD, TM_MM), lambda d, i: (d, 0, i))],
        out_specs=(pl.BlockSpec((2, None, TM_MM, FF_PAD), lambda d, i: (0, d, i, 0)), whole, down),
        scratch_shapes=[pltpu.VMEM((FF_PAD, D_MODEL), F32)],
        name=name, compiler_params=_params())(d_f, w_gu, w_gu, w_down, hg, hu, act_t)


def _position():
    return lax.axis_index("x"), lax.axis_index("y"), lax.axis_index("c")


def _flat(p):
    return 4 * p[0] + 2 * p[1] + p[2]


def _peer(me, k):
    x, y, c = me
    return (1 - x if k & 4 else x, 1 - y if k & 2 else y, 1 - c if k & 1 else c)


def _columns(width):
    return lambda ref, d: ref.at[:, pl.ds(pl.multiple_of(d * width, 128), width)]


def _leading(ref, d):
    return ref.at[d]


def _by_distance(ref, sender, k):
    return ref.at[k - 1]


def _prep_weights(me, na_qkv, na_o, dil_qkv, dil_o, gate, up, down, land_shapes):
    na_cols, dil_cols = na_qkv.shape[-1], dil_qkv.shape[-1]
    o_rows = na_o.shape[1]
    tiles = 4
    rows, rows_o = D_MODEL // tiles, o_rows // tiles

    def body(me_ref, naq, nao, dq, do_, g0, u0, d0, g1, u1, d1, *outs):
        def put(t, index, value):
            outs[t][index] = value
            outs[8 + t][index] = value

        put(0, ..., naq[...].astype(BF16))
        put(1, ..., nao[...].astype(BF16))
        put(4, ..., dq[...].astype(BF16))
        put(5, ..., do_[...].astype(BF16))
        for t, (g, u, d) in ((2, (g0, u0, d0)), (6, (g1, u1, d1))):
            for j, part in enumerate((g, u)):
                put(t, (j, slice(None), slice(0, FF_SHARD)), part[...].astype(BF16))
                put(t, (j, slice(None), slice(FF_SHARD, FF_PAD)), jnp.zeros((rows, FF_PAD - FF_SHARD), BF16))
            put(t + 1, (slice(0, FF_SHARD), slice(None)), d[...].astype(BF16))
            put(t + 1, (slice(FF_SHARD, FF_PAD), slice(None)), jnp.zeros((FF_PAD - FF_SHARD, D_MODEL), BF16))

    def tiled(width):
        return pl.BlockSpec((None, rows, width), lambda i, me: (0, i, 0))

    def layer(l, width):
        return pl.BlockSpec((None, rows, width), lambda i, me: (l, i, 0))

    def whole_layer(l):
        return pl.BlockSpec((None, FF_SHARD, D_MODEL), lambda i, me: (l, 0, 0))

    in_specs = [tiled(na_cols), pl.BlockSpec((None, rows_o, D_MODEL), lambda i, me: (0, i, 0)), tiled(dil_cols),
                pl.BlockSpec((None, rows_o, D_MODEL), lambda i, me: (0, i, 0)),
                layer(0, FF_SHARD), layer(0, FF_SHARD), whole_layer(0), layer(1, FF_SHARD), layer(1, FF_SHARD), whole_layer(1)]
    o_shard = pl.BlockSpec((rows_o, D_MODEL), lambda i, me: (i, 0))
    o_land = pl.BlockSpec((None, rows_o, D_MODEL), lambda i, me: (me[0], i, 0))
    gu_shard = pl.BlockSpec((2, rows, FF_PAD), lambda i, me: (0, i, 0))
    gu_land = pl.BlockSpec((None, 2, rows, FF_PAD), lambda i, me: (me[0], 0, i, 0))
    down_shard = pl.BlockSpec((FF_PAD, D_MODEL), lambda i, me: (0, 0))
    down_land = pl.BlockSpec((None, FF_PAD, D_MODEL), lambda i, me: (me[0], 0, 0))

    def qkv_shard(width):
        return pl.BlockSpec((rows, width), lambda i, me: (i, 0))

    def qkv_land(width):
        return pl.BlockSpec((rows, width), lambda i, me: (i, me[0]))

    shard_specs = [qkv_shard(na_cols), o_shard, gu_shard, down_shard, qkv_shard(dil_cols), o_shard, gu_shard, down_shard]
    land_specs = [qkv_land(na_cols), o_land, gu_land, down_land, qkv_land(dil_cols), o_land, gu_land, down_land]
    shard_shapes = [jax.ShapeDtypeStruct(s, BF16) for s in
                    ((D_MODEL, na_cols), (o_rows, D_MODEL), (2, D_MODEL, FF_PAD), (FF_PAD, D_MODEL),
                     (D_MODEL, dil_cols), (o_rows, D_MODEL), (2, D_MODEL, FF_PAD), (FF_PAD, D_MODEL))]
    result = pl.pallas_call(
        body, out_shape=tuple(shard_shapes + list(land_shapes)),
        grid_spec=pltpu.PrefetchScalarGridSpec(num_scalar_prefetch=1, grid=(tiles,), in_specs=in_specs,
                                               out_specs=tuple(shard_specs + land_specs)),
        name="prep_weights", compiler_params=_params())(me, na_qkv, na_o, dil_qkv, dil_o, gate, up, down, gate, up, down)
    return list(result[:8]), list(result[8:])


def _remote_copies(sets, src_refs, land_refs, send_sems, recv_sems, outgoing):
    me = _position()
    copies = []
    for t, (si, src_of, li, dst_of) in enumerate(sets):
        for k in range(1, N_DEV):
            other = _peer(me, k)
            sender = me if outgoing else other
            copies.append(pltpu.make_async_remote_copy(
                src_ref=src_of(src_refs[si], _flat(other)), dst_ref=dst_of(land_refs[li], _flat(sender), k),
                send_sem=send_sems.at[(N_DEV - 1) * t + k - 1], recv_sem=recv_sems.at[(N_DEV - 1) * t + k - 1],
                device_id=other, device_id_type=MESH))
    return copies


def _send_start(name, srcs, lands, sets_by_group):
    n_src, n_land, n_groups = len(srcs), len(lands), len(sets_by_group)

    def body(*refs):
        src_refs, land_refs = refs[:n_src], refs[n_src:n_src + n_land]
        outs = refs[n_src + n_land:]
        for g, sets in enumerate(sets_by_group):
            for cp in _remote_copies(sets, src_refs, land_refs, outs[2 * g], outs[2 * g + 1], True):
                cp.start()
        outs[-1][...] = jnp.zeros_like(outs[-1])

    sem_shapes = []
    for sets in sets_by_group:
        sem_shapes += [pltpu.SemaphoreType.DMA((len(sets) * (N_DEV - 1),))] * 2
    thru = [pltpu.HBM(a.shape, a.dtype) for a in list(srcs) + list(lands)]
    n_sem = len(sem_shapes)
    result = pl.pallas_call(
        body, out_shape=tuple(sem_shapes + thru + [jax.ShapeDtypeStruct((8, 128), F32)]),
        in_specs=[HBM] * (n_src + n_land),
        out_specs=tuple([SEM] * n_sem + [HBM] * (n_src + n_land) + [pl.BlockSpec(memory_space=pltpu.VMEM)]),
        input_output_aliases={i: n_sem + i for i in range(n_src + n_land)},
        compiler_params=pltpu.CompilerParams(has_side_effects=EFFECT), name=name,
    )(*[pltpu.with_memory_space_constraint(a, pltpu.HBM) for a in list(srcs) + list(lands)])
    sems = [(result[2 * g], result[2 * g + 1]) for g in range(n_groups)]
    return sems, list(result[n_sem:n_sem + n_src]), list(result[n_sem + n_src:n_sem + n_src + n_land]), result[-1]


def _send_wait(name, sems, srcs, lands, sets, after):
    n_src, n_land = len(srcs), len(lands)

    def body(*refs):
        src_refs, land_refs = refs[:n_src], refs[n_src:n_src + n_land]
        send_sems, recv_sems = refs[n_src + n_land], refs[n_src + n_land + 1]
        for cp in _remote_copies(sets, src_refs, land_refs, send_sems, recv_sems, True):
            cp.wait_send()
        for cp in _remote_copies(sets, src_refs, land_refs, send_sems, recv_sems, False):
            cp.wait_recv()

    thru = [pltpu.HBM(a.shape, a.dtype) for a in list(srcs) + list(lands)]
    result = pl.pallas_call(
        body, out_shape=tuple(thru), in_specs=[HBM] * (n_src + n_land) + [SEM, SEM] + [ANY] * len(after),
        out_specs=tuple([HBM] * (n_src + n_land)), input_output_aliases={i: i for i in range(n_src + n_land)},
        compiler_params=pltpu.CompilerParams(has_side_effects=EFFECT), name=name,
    )(*srcs, *lands, sems[0], sems[1], *after)
    return list(result[:n_src]), list(result[n_src:])


DIRECT = (1, 2, 4, 6)
PASSED = DIRECT[1:]


def _hbm_passthrough(body, name, arrays, n_sem_in, sem_out_shapes, extra):
    n, n_out = len(arrays), len(sem_out_shapes)
    return pl.pallas_call(
        body, out_shape=tuple(list(sem_out_shapes) + [pltpu.HBM(a.shape, a.dtype) for a in arrays]),
        in_specs=[HBM] * n + [SEM] * n_sem_in + [ANY] * len(extra), out_specs=tuple([SEM] * n_out + [HBM] * n),
        input_output_aliases={i: n_out + i for i in range(n)},
        compiler_params=pltpu.CompilerParams(has_side_effects=EFFECT), name=name)


def _shard_copy(src_ref, land_ref, window, block, to, send_sem, recv_sem, from_shard):
    dst = window(land_ref, _flat(block))
    return pltpu.make_async_remote_copy(src_ref=src_ref if from_shard else dst, dst_ref=dst, send_sem=send_sem,
                                        recv_sem=recv_sem, device_id=to, device_id_type=MESH)


def _gather_start(name, shards, lands, windows, group_sizes):
    n = len(shards)

    def body(*refs):
        shard_refs, land_refs, outs = refs[:n], refs[n:2 * n], refs[2 * n:]
        me = _position()
        t = 0
        for g, size in enumerate(group_sizes):
            for local in range(size):
                for j, k in enumerate(DIRECT):
                    i = len(DIRECT) * local + j
                    _shard_copy(shard_refs[t], land_refs[t], windows[t], me, _peer(me, k), outs[2 * g].at[i],
                                outs[2 * g + 1].at[i], True).start()
                t += 1

    sem_shapes = [pltpu.SemaphoreType.DMA((len(DIRECT) * size,)) for size in group_sizes for _ in range(2)]
    arrays = [pltpu.with_memory_space_constraint(a, pltpu.HBM) for a in list(shards) + list(lands)]
    result = _hbm_passthrough(body, name, arrays, 0, sem_shapes, ())(*arrays)
    n_sem = len(sem_shapes)
    sems = [(result[2 * g], result[2 * g + 1]) for g in range(len(group_sizes))]
    return sems, list(result[n_sem:n_sem + n]), list(result[n_sem + n:])


def _gather_pass_on(name, sems, shards, lands, windows, after):
    n = len(shards)

    def body(*refs):
        shard_refs, land_refs = refs[:n], refs[n:2 * n]
        recv_sems = refs[2 * n + 1]
        pass_send, pass_recv = refs[2 * n + 2 + len(after)], refs[2 * n + 3 + len(after)]
        me = _position()
        sibling = _peer(me, 1)
        for t in range(n):
            for j, k in enumerate(PASSED):
                sender = _peer(me, k)
                arrived = len(DIRECT) * t + 1 + j
                _shard_copy(shard_refs[t], land_refs[t], windows[t], sender, me, refs[2 * n].at[arrived], recv_sems.at[arrived],
                            True).wait_recv()
                i = len(PASSED) * t + j
                _shard_copy(shard_refs[t], land_refs[t], windows[t], sender, sibling, pass_send.at[i], pass_recv.at[i],
                            False).start()

    sem_shapes = [pltpu.SemaphoreType.DMA((len(PASSED) * n,))] * 2
    result = _hbm_passthrough(body, name, list(shards) + list(lands), 2, sem_shapes, after)(
        *shards, *lands, sems[0], sems[1], *after)
    return (result[0], result[1]), list(result[2:2 + n]), list(result[2 + n:])


def _gather_wait(name, sems, pass_sems, shards, lands, windows, after):
    n = len(shards)

    def body(*refs):
        shard_refs, land_refs = refs[:n], refs[n:2 * n]
        send_sems, recv_sems, pass_send, pass_recv = refs[2 * n:2 * n + 4]
        me = _position()
        sibling = _peer(me, 1)
        for t in range(n):
            for j, k in enumerate(DIRECT):
                i = len(DIRECT) * t + j
                _shard_copy(shard_refs[t], land_refs[t], windows[t], me, _peer(me, k), send_sems.at[i], recv_sems.at[i],
                            True).wait_send()
            _shard_copy(shard_refs[t], land_refs[t], windows[t], sibling, me, send_sems.at[len(DIRECT) * t],
                        recv_sems.at[len(DIRECT) * t], True).wait_recv()
            for j, k in enumerate(PASSED):
                i = len(PASSED) * t + j
                _shard_copy(shard_refs[t], land_refs[t], windows[t], _peer(me, k), sibling, pass_send.at[i], pass_recv.at[i],
                            False).wait_send()
                _shard_copy(shard_refs[t], land_refs[t], windows[t], _peer(sibling, k), me, pass_send.at[i], pass_recv.at[i],
                            False).wait_recv()

    result = _hbm_passthrough(body, name, list(shards) + list(lands), 4, [], after)(
        *shards, *lands, sems[0], sems[1], pass_sems[0], pass_sems[1], *after)
    return list(result[n:])


def _all_gather(name, locals_, out_shapes, windows, deps=()):
    n = len(locals_)

    def body(*refs):
        src_refs, out_refs = refs[:n], refs[n + len(deps):2 * n + len(deps)]
        send_sems, recv_sems, local_sems = refs[2 * n + len(deps):]
        x, y, c = _position()
        me, sibling = (x, y, c), (x, y, 1 - c)
        chips = [(1 - x, y), (x, 1 - y), (1 - x, 1 - y)]

        def copy(t, k, block, to, from_local=False):
            dst = windows[t](out_refs[t], _flat(block))
            return pltpu.make_async_remote_copy(
                src_ref=src_refs[t] if from_local else dst, dst_ref=dst, send_sem=send_sems.at[t, k],
                recv_sem=recv_sems.at[t, k], device_id=to, device_id_type=MESH)

        mine = [pltpu.make_async_copy(src_refs[t], windows[t](out_refs[t], _flat(me)), local_sems.at[t]) for t in range(n)]
        sends = []
        for t in range(n):
            mine[t].start()
            sends.append(copy(t, 0, me, sibling, True))
            sends += [copy(t, 1 + j, me, (*chip, c), True) for j, chip in enumerate(chips)]
        for cp in sends:
            cp.start()
        for t in range(n):
            for j, chip in enumerate(chips):
                copy(t, 1 + j, (*chip, c), me).wait_recv()
                passed = copy(t, 4 + j, (*chip, c), sibling)
                passed.start()
                sends.append(passed)
        for t in range(n):
            copy(t, 0, sibling, me).wait_recv()
            for j, chip in enumerate(chips):
                copy(t, 4 + j, (*chip, 1 - c), me).wait_recv()
        for cp in sends:
            cp.wait_send()
        for cp in mine:
            cp.wait()

    return pl.pallas_call(
        body, out_shape=tuple(out_shapes), in_specs=[ANY] * (n + len(deps)), out_specs=tuple([ANY] * n),
        scratch_shapes=[pltpu.SemaphoreType.DMA((n, 7)), pltpu.SemaphoreType.DMA((n, 7)), pltpu.SemaphoreType.DMA((n,))],
        name=name)(*locals_, *deps)


def _adamw(name, me, lands, owns, w, m, v, *, grid, land_specs, own_specs, p_spec):
    n_land = len(lands)

    def body(me_ref, *refs):
        land_refs, own_refs = refs[:n_land], refs[n_land:n_land + len(owns)]
        w_ref, m_ref, v_ref, g_ref, delta_ref, m_out, v_out = refs[n_land + len(owns):]
        ncols = w_ref.shape[-1]
        sums = []
        for i, land_ref in enumerate(land_refs):
            g = own_refs[i][...].astype(F32) if owns else land_ref[0].astype(F32)
            for s in range(0 if owns else 1, land_ref.shape[0]):
                g = g + land_ref[s].astype(F32)
            sums.append(g[:, :ncols])
        g = sums[0] if n_land == 1 else jnp.where(pl.program_id(0) == 0, sums[0], sums[1])
        m_new = ADAM_B1 * m_ref[...] + (1.0 - ADAM_B1) * g
        v_new = ADAM_B2 * v_ref[...] + (1.0 - ADAM_B2) * jnp.square(g)
        m_hat = m_new / (1.0 - ADAM_B1 ** ADAM_STEP)
        v_hat = v_new / (1.0 - ADAM_B2 ** ADAM_STEP)
        g_ref[...] = g
        delta_ref[...] = -ADAM_LR * (m_hat / (jnp.sqrt(v_hat) + ADAM_EPS) + ADAM_WD * w_ref[...])
        m_out[...] = m_new
        v_out[...] = v_new

    shape = jax.ShapeDtypeStruct(w.shape, F32)
    return pl.pallas_call(
        body, out_shape=(shape,) * 4,
        grid_spec=pltpu.PrefetchScalarGridSpec(
            num_scalar_prefetch=1, grid=grid, in_specs=list(land_specs) + list(own_specs) + [p_spec, p_spec, p_spec],
            out_specs=(p_spec,) * 4),
        name=name, compiler_params=_params())(me, *lands, *owns, w, m, v)


def _row(p, layer):
    return p[layer][None, :]


def _square(name, a, b, dims, out_dtype, deps=()):
    if a.shape == (D_MODEL, SEQ):
        return _matmul(name, a, b, grid=(2, 1), a_spec=pl.BlockSpec((512, SEQ), lambda i, k: (i, 0)),
                       b_spec=pl.BlockSpec((SEQ, D_MODEL), lambda i, k: (0, 0)),
                       o_spec=pl.BlockSpec((512, D_MODEL), lambda i, k: (i, 0)),
                       out_shape=jax.ShapeDtypeStruct((D_MODEL, D_MODEL), out_dtype), dims=NN, acc_shape=(8, 128),
                       deps=deps)
    return _matmul(name, a, b, grid=(1, 1), a_spec=pl.BlockSpec((SEQ, D_MODEL), lambda i, k: (i, 0)),
                   b_spec=pl.BlockSpec((D_MODEL, D_MODEL), lambda i, k: (0, 0)),
                   o_spec=pl.BlockSpec((SEQ, D_MODEL), lambda i, k: (i, 0)),
                   out_shape=jax.ShapeDtypeStruct((SEQ, D_MODEL), out_dtype), dims=dims, acc_shape=(8, 128), deps=deps)


def _grouped_matmul(name, a_list, b, *, n_tiles, a_block, b_spec, o_spec, out_shape):
    n_groups = len(a_list)

    def a_spec(g):
        def index(j, i):
            mine = j // 3
            return (jnp.where(mine == g, i, jnp.where(mine < g, 0, n_tiles - 1)), 0)
        return pl.BlockSpec(a_block, index)

    def body(*refs):
        b_ref, o_ref = refs[n_groups], refs[n_groups + 1]
        mine = pl.program_id(0) // 3
        for g in range(n_groups):
            @pl.when(mine == g)
            def _(g=g):
                o_ref[...] = jnp.dot(refs[g][...], b_ref[...], preferred_element_type=F32).astype(o_ref.dtype)

    return pl.pallas_call(
        body, out_shape=out_shape, grid=(3 * n_groups, n_tiles), in_specs=[a_spec(g) for g in range(n_groups)] + [b_spec],
        out_specs=o_spec, name=name, compiler_params=_params())(*a_list, b)


def _qkv_fwd(name, hs, w):
    return _grouped_matmul(name, hs, w, n_tiles=1, a_block=(SEQ, D_MODEL),
                           b_spec=pl.BlockSpec((D_MODEL, D_MODEL), lambda j, i: (0, j)),
                           o_spec=pl.BlockSpec((SEQ, D_MODEL), lambda j, i: (i, j)),
                           out_shape=jax.ShapeDtypeStruct((SEQ, 3 * len(hs) * D_MODEL), BF16))


def _qkv_dw(name, hs_t, dqkv):
    return _grouped_matmul(name, hs_t, dqkv, n_tiles=2, a_block=(512, SEQ),
                           b_spec=pl.BlockSpec((None, SEQ, D_MODEL), lambda j, i: (j, 0, 0)),
                           o_spec=pl.BlockSpec((512, D_MODEL), lambda j, i: (i, j)),
                           out_shape=jax.ShapeDtypeStruct((D_MODEL, 3 * len(hs_t) * D_MODEL), BF16))


def _proj_do_sorted(name, d_a, w_o):
    def body(da_ref, w_ref, *refs):
        value = lax.dot_general(da_ref[...], w_ref[...], NT, preferred_element_type=F32)
        refs[0][...] = value
        for j, dil in enumerate(SORTED):
            _sort_tile(refs[-1], value, dil, refs[1 + j])

    tile = pl.BlockSpec((TM, D_MODEL), lambda i: (i, 0))
    shapes = [jax.ShapeDtypeStruct((SEQ, D_MODEL), F32)] + [jax.ShapeDtypeStruct((dil, SEQ // dil, D_MODEL), F32) for dil in SORTED]
    result = pl.pallas_call(
        body, out_shape=tuple(shapes), grid=(SEQ // TM,),
        in_specs=[tile, pl.BlockSpec((D_MODEL, D_MODEL), lambda i: (0, 0))],
        out_specs=tuple([tile] + [_sorted_spec(dil) for dil in SORTED]), scratch_shapes=[_sort_scratch()],
        name=name, compiler_params=_params())(d_a, w_o)
    return [t.reshape(SEQ, D_MODEL) for t in result]


def _qkv_dh(name, dqkv, w, n_chunks, deps):
    tm = TM_MM
    return _matmul(name, dqkv, w, grid=(n_chunks // 3, SEQ // tm, 1),
                   a_spec=pl.BlockSpec((3, tm, D_MODEL), lambda g, i, k: (g, i, 0)),
                   b_spec=pl.BlockSpec((D_MODEL, 3 * D_MODEL), lambda g, i, k: (0, g)),
                   o_spec=pl.BlockSpec((None, tm, D_MODEL), lambda g, i, k: (g, i, 0)),
                   out_shape=jax.ShapeDtypeStruct((n_chunks // 3, SEQ, D_MODEL), F32), dims=NT, acc_shape=(8, 128),
                   deps=deps, inner=3)


def _local_step(x, target, norms, rpb, fetch, emit, deps):
    mix_pre, mix_post, ffn_pre, ffn_post = norms
    slopes = 2.0 ** (-8.0 * jnp.arange(1, N_HEADS + 1, dtype=F32) / N_HEADS)
    rpb_pad = jnp.pad(rpb, ((0, 0), (0, 1), (0, 128 - 31)))
    saved = []

    hs = [_rms_fwd("l0_norm_mix", x, _row(mix_pre, 0), out_dtype=BF16, deps=deps)]
    hs_t0 = None
    for layer in range(2):
        tag = f"l{layer}"
        if layer == 0:
            table = _rpb_table(rpb_pad)
            w_qkv, w_o = fetch("na", [table, hs[0]], [hs[0]])
            qkv = _qkv_fwd(tag + "_qkv", hs, w_qkv)
            o, lse = _na_fwd(qkv, table)
            mixer = (hs, qkv, o, lse, table)
        else:
            w_qkv, w_o = fetch("dil", [saved[0][7]], [hs[0]])
            qkv = _qkv_fwd(tag + "_qkv", hs, w_qkv)
            outs, lses = zip(*[_dil_fwd(g, qkv, slopes * dil) for g, (_, dil) in enumerate(DIL_GROUPS)])
            merged, lse_total = _dil_merge(outs, lses)
            o = merged[0]
            mixer = (hs, qkv, merged, lses, lse_total)
        a = _square(tag + "_proj", o, w_o, NN, F32)
        x1, (h2, h2_t) = _post_norm_fwd(tag + "_post_mix", a, _row(mix_post, layer), x, _row(ffn_pre, layer))
        w_gu, w_down = fetch(f"ffn{layer}", [a], [h2])
        f, hg, hu, act_t = _ffn_fwd(tag + "_ffn", h2, w_gu, w_down)
        transposed = ([hs_t0 if hs_t0 is not None and t is hs[0] else t.T for t in hs], o.astype(BF16).T, h2_t, act_t)
        saved.append((x, mixer, a, x1, transposed, hg, hu, f, w_qkv, w_o, w_gu, w_down))
        if layer == 0:
            x, (h, hs_t0, *views) = _post_norm_fwd(tag + "_post_ffn", f, _row(ffn_post, 0), x1, _row(mix_pre, 1), sorted_too=True)
            hs = [h] + [t.reshape(SEQ, D_MODEL) for t in views]
        else:
            x = _rms_fwd(tag + "_post_ffn", f, _row(ffn_post, layer), res=x1)

    dx, loss = _loss_head("loss_head", x, target)
    d_norm = {k: [None, None] for k in ("mix_pre", "mix_post", "ffn_pre", "ffn_post")}
    d_rpb = None

    d_f, d_norm["ffn_post"][1] = _rms_bwd("b1_post_ffn", saved[1][7], _row(ffn_post, 1), [dx], out_dtype=BF16)
    for layer in (1, 0):
        tag = f"b{layer}"
        x0, mixer, a, x1, (h_t, o_t, h2_t, act_t), hg, hu, f, w_qkv, w_o, w_gu, w_down = saved[layer]
        dgu, d_h2, d_down = _ffn_bwd(tag + "_ffn", d_f, w_gu, w_down, hg, hu, act_t)
        d_gu = _ffn_dgu(tag + "_ffn_dgu", h2_t, dgu)
        sent = emit(f"ffn{layer}", [d_gu, d_down])
        dx1, d_a, d_norm["ffn_pre"][layer], d_norm["mix_post"][layer] = _norm_post_bwd(
            tag + "_norm_ffn", x1, _row(ffn_pre, layer), [d_h2], dx, a, _row(mix_post, layer), deps=sent)
        d_wo = _square(tag + "_proj_dw", o_t, d_a, NN, BF16)
        if layer == 0:
            _, qkv, o, lse, table = mixer
            d_o = _square(tag + "_proj_do", d_a, w_o, NT, BF16)
            dqkv, gp = _na_bwd(qkv, table, d_o, lse)
            d_rpb = _rpb_grad(gp)[:, :15, :31]
            sent = emit("na", [_qkv_dw(tag + "_qkv_dw", h_t, dqkv), d_wo])
            d_h = _qkv_dh(tag + "_qkv_dh", dqkv, w_qkv, 3, sent)
            dx, d_norm["mix_pre"][layer] = _rms_bwd(tag + "_norm_mix", x0, _row(mix_pre, layer), [d_h[0]], res=dx1)
        else:
            _, qkv, merged, lses, lse_total = mixer
            d_o = _proj_do_sorted(tag + "_proj_do", d_a, w_o)
            dqkv = lax.empty((3 * len(DIL_GROUPS), SEQ, D_MODEL), BF16)
            for g, (_, dil) in enumerate(DIL_GROUPS):
                dqkv = _dil_bwd(g, qkv, slopes * dil, d_o[g], merged[g], lses[g], lse_total[g], dqkv)
            sent = emit("dil", [_qkv_dw(tag + "_qkv_dw", h_t, dqkv), d_wo])
            d_h = _qkv_dh(tag + "_qkv_dh", dqkv, w_qkv, 9, sent)
            dx, d_f, d_norm["mix_pre"][1], d_norm["ffn_post"][0] = _norm_post_bwd(
                tag + "_norm_mix", x0, _row(mix_pre, 1), None, dx1, saved[0][7], _row(ffn_post, 0), groups=d_h)

    d_gains = [jnp.concatenate(d_norm[k], axis=0) for k in ("mix_pre", "mix_post", "ffn_pre", "ffn_post")]
    return loss, dx, d_gains, d_rpb


RPB_SIZE = N_HEADS * 15 * 31


def _pack_small(gains, rpb, last=None):
    top = jnp.concatenate(gains, axis=0).reshape(64, 128)
    bottom = jnp.pad(rpb.reshape(-1), (0, 64 * 128 - RPB_SIZE))
    if last is not None:
        bottom = bottom + jnp.pad(last.reshape(1), (64 * 128 - 1, 0))
    return jnp.concatenate([top, bottom.reshape(64, 128)], axis=0)


def _unpack_small(p):
    gains = p[:64].reshape(4, 2, D_MODEL)
    rpb = p[64:].reshape(-1)[:RPB_SIZE].reshape(1, N_HEADS, 15, 31)
    return [gains[i] for i in range(4)], rpb


GROUPS = ("na", "ffn0", "dil", "ffn1")


def kernel(x, norm_mix_pre, norm_mix_post, norm_ffn_pre, norm_ffn_post, na_w_qkv, na_w_o, na_rpb, dil_w_qkv, dil_w_o, ffn_w_gate, ffn_w_up, ffn_w_down, loss_target, m_norm_mix_pre, m_norm_mix_post, m_norm_ffn_pre, m_norm_ffn_post, m_na_w_qkv, m_na_w_o, m_na_rpb, m_dil_w_qkv, m_dil_w_o, m_ffn_w_gate, m_ffn_w_up, m_ffn_w_down, v_norm_mix_pre, v_norm_mix_post, v_norm_ffn_pre, v_norm_ffn_post, v_na_w_qkv, v_na_w_o, v_na_rpb, v_dil_w_qkv, v_dil_w_o, v_ffn_w_gate, v_ffn_w_up, v_ffn_w_down):
    na_cols, dil_cols, o_rows = 3 * D_MODEL // N_DEV, 9 * D_MODEL // N_DEV, D_MODEL // N_DEV
    ff_pad = FF_PAD - FF_SHARD
    me = (4 * lax.axis_index("x") + 2 * lax.axis_index("y") + lax.axis_index("c")).astype(jnp.int32).reshape(1)

    full = {
        "na": [((D_MODEL, 3 * D_MODEL), _columns(na_cols)), ((N_DEV, o_rows, D_MODEL), _leading)],
        "dil": [((D_MODEL, 9 * D_MODEL), _columns(dil_cols)), ((N_DEV, o_rows, D_MODEL), _leading)],
        "ffn0": [((N_DEV, 2, D_MODEL, FF_PAD), _leading), ((N_DEV, FF_PAD, D_MODEL), _leading)],
        "ffn1": [((N_DEV, 2, D_MODEL, FF_PAD), _leading), ((N_DEV, FF_PAD, D_MODEL), _leading)],
    }
    block = {
        "na": [(D_MODEL, na_cols), (o_rows, D_MODEL)], "dil": [(D_MODEL, dil_cols), (o_rows, D_MODEL)],
        "ffn0": [(2, D_MODEL, FF_PAD), (FF_PAD, D_MODEL)], "ffn1": [(2, D_MODEL, FF_PAD), (FF_PAD, D_MODEL)],
    }

    land_shapes = [jax.ShapeDtypeStruct(full[g][t][0], BF16) for g in GROUPS for t in range(2)]
    windows = [full[g][t][1] for g in GROUPS for t in range(2)]
    shards, lands = _prep_weights(me, na_w_qkv, na_w_o, dil_w_qkv, dil_w_o, ffn_w_gate, ffn_w_up, ffn_w_down, land_shapes)
    sems, shards, lands = _gather_start("gather_start", shards, lands, windows, [2] * len(GROUPS))

    def fetch(group, early, late):
        gi = GROUPS.index(group)
        mine = slice(2 * gi, 2 * gi + 2)
        pass_sems, shards_g, lands_g = _gather_pass_on(f"gather_pass_{group}", sems[gi], shards[mine], lands[mine],
                                                       windows[mine], early)
        qkv, o = _gather_wait(f"gather_wait_{group}", sems[gi], pass_sems, shards_g, lands_g, windows[mine], late)
        return (qkv, o.reshape(D_MODEL, D_MODEL)) if group in ("na", "dil") else (qkv, o)

    def grad_source(group, t):
        return _columns(block[group][0][1]) if (group in ("na", "dil") and t == 0) else _leading

    in_flight = {}

    def emit(group, grads):
        if group in ("na", "dil"):
            grads = [grads[0], grads[1].reshape(N_DEV, o_rows, D_MODEL)]
        sets = [(t, grad_source(group, t), t, _by_distance) for t in range(2)]
        landing = [lax.empty((N_DEV - 1,) + block[group][t], BF16) for t in range(2)]
        sems_g, grads, landing, tok = _send_start(f"exchange_start_{group}", grads, landing, [sets])
        in_flight[group] = (sems_g[0], grads, landing, sets)
        return [tok]

    norms = (norm_mix_pre, norm_mix_post, norm_ffn_pre, norm_ffn_post)
    loss, grad_x, d_gains, d_rpb = _local_step(x[0], loss_target[0], norms, na_rpb[0], fetch, emit, [shards[0]])

    landed, sent = {}, {}

    def wait_for(group, after):
        sems_g, grads, landing, sets = in_flight[group]
        sent[group], landed[group] = _send_wait(f"exchange_wait_{group}", sems_g, grads, landing, sets, after)

    for group in ("ffn1", "dil", "ffn0"):
        wait_for(group, [grad_x])

    def one(rows, tile, ncols, columns):
        own = (pl.BlockSpec((tile, ncols), lambda i, me: (i, me[0])) if columns
               else pl.BlockSpec((None, tile, ncols), lambda i, me: (me[0], i, 0)))
        return dict(grid=(rows // tile,), land_specs=[pl.BlockSpec((N_DEV - 1, tile, ncols), lambda i, me: (0, i, 0))],
                    own_specs=[own], p_spec=pl.BlockSpec((None, tile, ncols), lambda i, me: (0, i, 0)))

    def layered(block_shape, index, p_block, n_tiles):
        def specs(lead_size, lead):
            shape = (lead_size,) + block_shape
            return [pl.BlockSpec(shape, lambda l, r, me: index(lead(me), jnp.where(l == 0, r, n_tiles - 1))),
                    pl.BlockSpec(shape, lambda l, r, me: index(lead(me), jnp.where(l == 0, 0, r)))]
        return dict(grid=(2, n_tiles), land_specs=specs(N_DEV - 1, lambda me: 0), own_specs=specs(None, lambda me: me[0]),
                    p_spec=pl.BlockSpec(p_block, lambda l, r, me: (l, r, 0)))

    gu_lands, gu_owns = [landed["ffn0"][0], landed["ffn1"][0]], [sent["ffn0"][0], sent["ffn1"][0]]
    down_lands, down_owns = [landed["ffn0"][1], landed["ffn1"][1]], [sent["ffn0"][1], sent["ffn1"][1]]
    updates = {
        "dil_w_qkv": _adamw("adamw_dil_qkv", me, [landed["dil"][0]], [sent["dil"][0]], dil_w_qkv, m_dil_w_qkv, v_dil_w_qkv,
                            **one(D_MODEL, 128, dil_cols, True)),
        "dil_w_o": _adamw("adamw_dil_o", me, [landed["dil"][1]], [sent["dil"][1]], dil_w_o, m_dil_w_o, v_dil_w_o,
                          **one(o_rows, o_rows, D_MODEL, False)),
        "ffn_w_gate": _adamw("adamw_gate", me, gu_lands, gu_owns, ffn_w_gate, m_ffn_w_gate, v_ffn_w_gate,
                             **layered((None, 128, FF_PAD), lambda lead, r: (lead, 0, r, 0), (None, 128, FF_SHARD), 8)),
        "ffn_w_up": _adamw("adamw_up", me, gu_lands, gu_owns, ffn_w_up, m_ffn_w_up, v_ffn_w_up,
                           **layered((None, 128, FF_PAD), lambda lead, r: (lead, 1, r, 0), (None, 128, FF_SHARD), 8)),
        "ffn_w_down": _adamw("adamw_down", me, down_lands, down_owns, ffn_w_down, m_ffn_w_down, v_ffn_w_down,
                             **layered((176, D_MODEL), lambda lead, r: (lead, r, 0), (None, 176, D_MODEL), 2)),
    }
    done = [u[0] for u in updates.values()]
    small = _all_gather("gather_small", [_pack_small(d_gains, d_rpb, loss)], [jax.ShapeDtypeStruct((N_DEV, 128, 128), F32)],
                        [_leading], deps=done)[0]
    wait_for("na", [small])
    updates["na_w_qkv"] = _adamw("adamw_na_qkv", me, [landed["na"][0]], [sent["na"][0]], na_w_qkv, m_na_w_qkv, v_na_w_qkv,
                                 **one(D_MODEL, 256, na_cols, True))
    updates["na_w_o"] = _adamw("adamw_na_o", me, [landed["na"][1]], [sent["na"][1]], na_w_o, m_na_w_o, v_na_w_o,
                               **one(o_rows, o_rows, D_MODEL, False))
    gains = [norm_mix_pre, norm_mix_post, norm_ffn_pre, norm_ffn_post]
    m_gains = [m_norm_mix_pre, m_norm_mix_post, m_norm_ffn_pre, m_norm_ffn_post]
    v_gains = [v_norm_mix_pre, v_norm_mix_post, v_norm_ffn_pre, v_norm_ffn_post]
    packed = _adamw("adamw_small", me, [small], (), _pack_small(gains, na_rpb)[None], _pack_small(m_gains, m_na_rpb)[None],
                    _pack_small(v_gains, v_na_rpb)[None], grid=(1,),
                    land_specs=[pl.BlockSpec((N_DEV, 128, 128), lambda i, me: (0, 0, 0))], own_specs=[],
                    p_spec=pl.BlockSpec((None, 128, 128), lambda i, me: (0, 0, 0)))
    small_out = [_unpack_small(p[0]) for p in packed]

    order = ["na_w_qkv", "na_w_o", "na_rpb", "dil_w_qkv", "dil_w_o", "ffn_w_gate", "ffn_w_up", "ffn_w_down"]
    result = [packed[0][0, 127, 127], grad_x[None]]
    for kind in range(4):
        gains_k, rpb_k = small_out[kind]
        result += gains_k
        result += [rpb_k if name == "na_rpb" else updates[name][kind] for name in order]
    return tuple(result)
```

```python
import jax
import jax.numpy as jnp
from jax import lax
from jax.experimental import pallas as pl
from jax.experimental.pallas import tpu as pltpu

F32 = jnp.float32
BF16 = jnp.bfloat16
MESH = pl.DeviceIdType.MESH
ANY = pl.BlockSpec(memory_space=pl.ANY)
HBM = pl.BlockSpec(memory_space=pltpu.HBM)
SEM = pl.BlockSpec(memory_space=pltpu.SEMAPHORE)
EFFECT = pltpu.SideEffectType.DATAFLOW_SIDE_EFFECTING

N_DEV = 8
SEQ = 2048
D_MODEL = 1024
N_HEADS = 16
HEAD_DIM = 64
GRID_W = 64
NA_ROWS = 8
SEQ_ROWS = SEQ // GRID_W
DIL_GROUPS = ((128, 1), (512, 4), (2048, 16))
BAND = 128
RADIUS = 64
FF_SHARD = 352
FF_PAD = 384
RMS_EPS = 1e-6
NEG_INF = -1e30
Q_SCALE = HEAD_DIM ** -0.5

ADAM_LR = 0.001
ADAM_B1 = 0.9
ADAM_B2 = 0.999
ADAM_EPS = 1e-08
ADAM_WD = 0.01
ADAM_STEP = 10

VMEM_LIMIT = 56 * 1024 * 1024
TM = 512
TM_ROW = 1024
TM_MM = 1024

NN = (((1,), (0,)), ((), ()))
NT = (((1,), (1,)), ((), ()))
TN = (((0,), (0,)), ((), ()))


def _params():
    return pltpu.CompilerParams(vmem_limit_bytes=VMEM_LIMIT)


def _matmul(name, a, b, *, grid, a_spec, b_spec, o_spec, out_shape, dims, acc_shape, deps=(), inner=1):
    nk = grid[-1]
    kaxis = len(grid) - 1

    def body(a_ref, b_ref, *rest):
        o_ref, acc_ref = rest[-2], rest[-1]
        if inner == 1:
            a_all = a_ref[...].astype(BF16)
        else:
            a_all = jnp.concatenate([a_ref[j].astype(BF16) for j in range(inner)], axis=1)
        part = lax.dot_general(a_all, b_ref[...].astype(BF16), dims, preferred_element_type=F32)
        if nk == 1:
            o_ref[...] = part.astype(o_ref.dtype)
        else:
            k = pl.program_id(kaxis)

            @pl.when(k == 0)
            def _():
                acc_ref[...] = part

            @pl.when(k > 0)
            def _():
                acc_ref[...] += part

            @pl.when(k == nk - 1)
            def _():
                o_ref[...] = acc_ref[...].astype(o_ref.dtype)

    return pl.pallas_call(
        body, out_shape=out_shape, grid=grid, in_specs=[a_spec, b_spec] + [ANY] * len(deps), out_specs=o_spec,
        scratch_shapes=[pltpu.VMEM(acc_shape, F32)], name=name, compiler_params=_params())(a, b, *deps)


SORTED = tuple(d for _, d in DIL_GROUPS if d > 1)
LANE_CHUNKS = D_MODEL // 128


def _sort_scratch(tm=TM):
    return pltpu.VMEM((LANE_CHUNKS, tm, 128), F32)


def _sorted_view(t, dil):
    return t.reshape(dil, SEQ // dil, D_MODEL)


def _sorted_spec(dil, lead=(), tm=TM):
    return pl.BlockSpec((None,) * len(lead) + (dil, tm // dil, D_MODEL), lambda i: tuple(lead) + (0, i, 0))


def _sort_tile(scratch, value, dil, out_ref):
    tm = value.shape[0]
    for c in range(LANE_CHUNKS):
        scratch[c] = value[:, 128 * c:128 * (c + 1)]
    for r in range(dil):
        rows = [scratch.at[c][pl.ds(r, tm // dil, stride=dil), :] for c in range(LANE_CHUNKS)]
        out_ref[r] = jnp.concatenate(rows, axis=1).astype(out_ref.dtype)


def _unsort_tile(scratch, in_ref, dil):
    for r in range(dil):
        value = in_ref[r].astype(F32)
        for c in range(LANE_CHUNKS):
            scratch.at[c][pl.ds(r, value.shape[0], stride=dil), :] = value[:, 128 * c:128 * (c + 1)]
    return jnp.concatenate([scratch[c] for c in range(LANE_CHUNKS)], axis=1)


def _rms_fwd(name, x, g, res=None, out_dtype=F32, deps=(), sorted_too=False):
    tm = TM if sorted_too else TM_ROW
    n_tiles = SEQ // tm
    has_res = res is not None
    n_in = 2 + has_res + len(deps)

    def body(*refs):
        x_ref, g_ref = refs[0], refs[1]
        xv = x_ref[...]
        r = lax.rsqrt(jnp.mean(xv * xv, axis=-1, keepdims=True) + RMS_EPS)
        y = xv * r * g_ref[...]
        if has_res:
            y = refs[2][...] + y
        refs[n_in][...] = y.astype(out_dtype)
        if sorted_too:
            for j, dil in enumerate(SORTED):
                _sort_tile(refs[-1], y, dil, refs[n_in + 1 + j])

    tile = pl.BlockSpec((tm, D_MODEL), lambda i: (i, 0))
    gspec = pl.BlockSpec((1, D_MODEL), lambda i: (0, 0))
    ins = [x, g] + ([res] if has_res else []) + list(deps)
    specs = [tile, gspec] + ([tile] if has_res else []) + [ANY] * len(deps)
    shapes, out_specs = [jax.ShapeDtypeStruct((SEQ, D_MODEL), out_dtype)], [tile]
    if sorted_too:
        shapes += [jax.ShapeDtypeStruct((dil, SEQ // dil, D_MODEL), out_dtype) for dil in SORTED]
        out_specs += [_sorted_spec(dil) for dil in SORTED]
    result = pl.pallas_call(
        body, out_shape=tuple(shapes), grid=(n_tiles,), in_specs=specs, out_specs=tuple(out_specs),
        scratch_shapes=[_sort_scratch()] if sorted_too else [], name=name, compiler_params=_params())(*ins)
    return result if sorted_too else result[0]


def _rms_bwd(name, x, g, dys, res=None, out_dtype=F32, groups=None, deps=()):
    tm = TM if groups is not None else TM_ROW
    n_tiles = SEQ // tm
    n_dy = len(dys) if groups is None else 1 + len(SORTED)
    has_res = res is not None

    def body(*refs):
        x_ref, g_ref = refs[0], refs[1]
        dy_refs = refs[2:2 + n_dy]
        res_ref = refs[2 + n_dy] if has_res else None
        first_out = 2 + n_dy + has_res + len(deps)
        dx_ref, dg_ref, acc_ref = refs[first_out:first_out + 3]
        i = pl.program_id(0)
        xv = x_ref[...]
        r = lax.rsqrt(jnp.mean(xv * xv, axis=-1, keepdims=True) + RMS_EPS)
        xn = xv * r
        dy = dy_refs[0][...].astype(F32)
        for j, extra in enumerate(dy_refs[1:]):
            dy = dy + (extra[...].astype(F32) if groups is None else _unsort_tile(refs[-1], extra, SORTED[j]))
        dyg = dy * g_ref[...]
        dx = r * (dyg - xn * jnp.mean(dyg * xn, axis=-1, keepdims=True))
        if has_res:
            dx = res_ref[...] + dx
        dx_ref[...] = dx.astype(dx_ref.dtype)
        part = jnp.sum((dy * xn).reshape(tm // 8, 8, D_MODEL), axis=0)

        @pl.when(i == 0)
        def _():
            acc_ref[...] = part

        @pl.when(i > 0)
        def _():
            acc_ref[...] += part

        @pl.when(i == n_tiles - 1)
        def _():
            dg_ref[...] = jnp.broadcast_to(jnp.sum(acc_ref[...], axis=0, keepdims=True), (8, D_MODEL))

    tile = pl.BlockSpec((tm, D_MODEL), lambda i: (i, 0))
    gspec = pl.BlockSpec((1, D_MODEL), lambda i: (0, 0))
    if groups is None:
        dy_ins, dy_specs = list(dys), [tile] * n_dy
    else:
        dy_ins = [groups] + [groups.reshape(n_dy, dil, SEQ // dil, D_MODEL) for dil in SORTED]
        dy_specs = [pl.BlockSpec((None, tm, D_MODEL), lambda i: (0, i, 0))]
        dy_specs += [_sorted_spec(dil, lead=(1 + j,)) for j, dil in enumerate(SORTED)]
    ins = [x, g] + dy_ins + ([res] if has_res else []) + list(deps)
    specs = [tile, gspec] + dy_specs + ([tile] if has_res else []) + [ANY] * len(deps)
    dx, dg = pl.pallas_call(
        body, out_shape=(jax.ShapeDtypeStruct((SEQ, D_MODEL), out_dtype), jax.ShapeDtypeStruct((8, D_MODEL), F32)),
        grid=(n_tiles,), in_specs=specs,
        out_specs=(tile, pl.BlockSpec((8, D_MODEL), lambda i: (0, 0))),
        scratch_shapes=[pltpu.VMEM((8, D_MODEL), F32)] + ([_sort_scratch()] if groups is not None else []),
        name=name, compiler_params=_params())(*ins)
    return dx, dg[0:1]


def _rms(xv):
    return lax.rsqrt(jnp.mean(xv * xv, axis=-1, keepdims=True) + RMS_EPS)


def _post_norm_fwd(name, a, g_post, res, g_next, sorted_too=False):
    def body(a_ref, gp_ref, res_ref, gn_ref, x_ref, h_ref, ht_ref, *rest):
        av = a_ref[...]
        xv = res_ref[...] + av * _rms(av) * gp_ref[...]
        x_ref[...] = xv
        y = xv * _rms(xv) * gn_ref[...]
        h_ref[...] = y.astype(h_ref.dtype)
        ht_ref[...] = y.T.astype(ht_ref.dtype)
        if sorted_too:
            for j, dil in enumerate(SORTED):
                _sort_tile(rest[-1], y, dil, rest[j])

    tile = pl.BlockSpec((TM, D_MODEL), lambda i: (i, 0))
    gspec = pl.BlockSpec((1, D_MODEL), lambda i: (0, 0))
    shapes = [jax.ShapeDtypeStruct((SEQ, D_MODEL), F32), jax.ShapeDtypeStruct((SEQ, D_MODEL), BF16),
              jax.ShapeDtypeStruct((D_MODEL, SEQ), BF16)]
    out_specs = [tile, tile, pl.BlockSpec((D_MODEL, TM), lambda i: (0, i))]
    if sorted_too:
        shapes += [jax.ShapeDtypeStruct((dil, SEQ // dil, D_MODEL), BF16) for dil in SORTED]
        out_specs += [_sorted_spec(dil) for dil in SORTED]
    result = pl.pallas_call(
        body, out_shape=tuple(shapes), grid=(SEQ // TM,), in_specs=[tile, gspec, tile, gspec], out_specs=tuple(out_specs),
        scratch_shapes=[_sort_scratch()] if sorted_too else [], name=name, compiler_params=_params())(a, g_post, res, g_next)
    return result[0], list(result[1:])


def _norm_post_bwd(name, x, g_norm, dys, res, a, g_post, groups=None, deps=()):
    n_tiles = SEQ // TM
    n_dy = len(dys) if groups is None else 1 + len(SORTED)

    def body(*refs):
        x_ref, g_ref = refs[0], refs[1]
        dy_refs = refs[2:2 + n_dy]
        res_ref, a_ref, gp_ref = refs[2 + n_dy:5 + n_dy]
        dx_ref, da_ref, dg_ref, dgp_ref, acc_ref, accp_ref = refs[5 + n_dy + len(deps):11 + n_dy + len(deps)]
        i = pl.program_id(0)
        xv = x_ref[...]
        r = _rms(xv)
        xn = xv * r
        dy = dy_refs[0][...].astype(F32)
        for j, extra in enumerate(dy_refs[1:]):
            dy = dy + (extra[...].astype(F32) if groups is None else _unsort_tile(refs[-1], extra, SORTED[j]))
        dyg = dy * g_ref[...]
        dx = res_ref[...] + r * (dyg - xn * jnp.mean(dyg * xn, axis=-1, keepdims=True))
        dx_ref[...] = dx
        av = a_ref[...]
        ra = _rms(av)
        an = av * ra
        dxg = dx * gp_ref[...]
        da_ref[...] = (ra * (dxg - an * jnp.mean(dxg * an, axis=-1, keepdims=True))).astype(da_ref.dtype)
        part = jnp.sum((dy * xn).reshape(TM // 8, 8, D_MODEL), axis=0)
        part_p = jnp.sum((dx * an).reshape(TM // 8, 8, D_MODEL), axis=0)

        @pl.when(i == 0)
        def _():
            acc_ref[...] = part
            accp_ref[...] = part_p

        @pl.when(i > 0)
        def _():
            acc_ref[...] += part
            accp_ref[...] += part_p

        @pl.when(i == n_tiles - 1)
        def _():
            dg_ref[...] = jnp.broadcast_to(jnp.sum(acc_ref[...], axis=0, keepdims=True), (8, D_MODEL))
            dgp_ref[...] = jnp.broadcast_to(jnp.sum(accp_ref[...], axis=0, keepdims=True), (8, D_MODEL))

    tile = pl.BlockSpec((TM, D_MODEL), lambda i: (i, 0))
    gspec = pl.BlockSpec((1, D_MODEL), lambda i: (0, 0))
    gain = pl.BlockSpec((8, D_MODEL), lambda i: (0, 0))
    if groups is None:
        dy_ins, dy_specs = list(dys), [tile] * n_dy
    else:
        dy_ins = [groups] + [groups.reshape(n_dy, dil, SEQ // dil, D_MODEL) for dil in SORTED]
        dy_specs = [pl.BlockSpec((None, TM, D_MODEL), lambda i: (0, i, 0))]
        dy_specs += [_sorted_spec(dil, lead=(1 + j,)) for j, dil in enumerate(SORTED)]
    dx, da, dg, dgp = pl.pallas_call(
        body,
        out_shape=(jax.ShapeDtypeStruct((SEQ, D_MODEL), F32), jax.ShapeDtypeStruct((SEQ, D_MODEL), BF16),
                   jax.ShapeDtypeStruct((8, D_MODEL), F32), jax.ShapeDtypeStruct((8, D_MODEL), F32)),
        grid=(n_tiles,), in_specs=[tile, gspec] + dy_specs + [tile, tile, gspec] + [ANY] * len(deps),
        out_specs=(tile, tile, gain, gain),
        scratch_shapes=[pltpu.VMEM((8, D_MODEL), F32)] * 2 + ([_sort_scratch()] if groups is not None else []),
        name=name, compiler_params=_params())(x, g_norm, *dy_ins, res, a, g_post, *deps)
    return dx, da, dg[0:1], dgp[0:1]


def _loss_head(name, y, target):
    tm = TM_ROW
    n_tiles = SEQ // tm

    def body(y_ref, t_ref, dy_ref, loss_ref, acc_ref):
        i = pl.program_id(0)
        diff = y_ref[...] - t_ref[...]
        dy_ref[...] = diff * (1.0 / D_MODEL)
        part = jnp.sum((diff * diff).reshape(tm // 8, 8, D_MODEL), axis=0)

        @pl.when(i == 0)
        def _():
            acc_ref[...] = part

        @pl.when(i > 0)
        def _():
            acc_ref[...] += part

        @pl.when(i == n_tiles - 1)
        def _():
            loss_ref[...] = jnp.full((8, 128), jnp.sum(acc_ref[...]) * (0.5 / D_MODEL), F32)

    tile = pl.BlockSpec((tm, D_MODEL), lambda i: (i, 0))
    dy, loss = pl.pallas_call(
        body, out_shape=(jax.ShapeDtypeStruct((SEQ, D_MODEL), F32), jax.ShapeDtypeStruct((8, 128), F32)),
        grid=(n_tiles,), in_specs=[tile, tile], out_specs=(tile, pl.BlockSpec((8, 128), lambda i: (0, 0))),
        scratch_shapes=[pltpu.VMEM((8, D_MODEL), F32)], name=name, compiler_params=_params())(y, target)
    return dy, loss[0, 0]


def _row_index(shape):
    return lax.broadcasted_iota(jnp.int32, shape, 0)


def _lane_index(shape):
    return lax.broadcasted_iota(jnp.int32, shape, len(shape) - 1)


def _skew_rows(t, direction):
    q = _row_index(t.shape) & (GRID_W - 1)
    for bit in range(6):
        step = 1 << bit
        shift = step if direction > 0 else 128 - step
        t = jnp.where((q & step) != 0, pltpu.roll(t, shift, 1), t)
    return t


RPB_HEADS = 4


def _rpb_table(rpb_pad):
    rows = RPB_HEADS * 16 * GRID_W

    def body(r_ref, t_ref):
        lane = _lane_index((rows, 128))
        v = pltpu.roll(r_ref[...].reshape(RPB_HEADS * 16, 128), 128 - 15, 1)
        t = _skew_rows(jnp.broadcast_to(v[:, None, :], (RPB_HEADS * 16, GRID_W, 128)).reshape(rows, 128), +1)
        t = jnp.where(lane < GRID_W, t, 0.0)
        below = jnp.concatenate([t[GRID_W:], jnp.zeros((GRID_W, 128), F32)], axis=0)
        first_col = jnp.clip((_row_index((rows, 128)) & (GRID_W - 1)) - 8, 0, GRID_W - 16)
        key_col = lane & (GRID_W - 1)
        in_window = (key_col >= first_col) & (key_col < first_col + 16)
        t_ref[...] = jnp.where(in_window, t + pltpu.roll(below, GRID_W, 1), NEG_INF).reshape(RPB_HEADS, 16, GRID_W, 128)

    return pl.pallas_call(
        body, out_shape=jax.ShapeDtypeStruct((N_HEADS, 16, GRID_W, 128), F32), grid=(N_HEADS // RPB_HEADS,),
        in_specs=[pl.BlockSpec((RPB_HEADS, 16, 128), lambda h: (h, 0, 0))],
        out_specs=pl.BlockSpec((RPB_HEADS, 16, GRID_W, 128), lambda h: (h, 0, 0, 0)),
        name="rpb_table", compiler_params=_params())(rpb_pad)


def _rpb_grad(gp):
    rows = RPB_HEADS * 16 * GRID_W

    def body(g_ref, o_ref):
        lane = _lane_index((rows, 128))
        g = g_ref[...].reshape(rows, 128)
        low = jnp.where(lane < GRID_W, g, 0.0)
        high = pltpu.roll(jnp.where(lane >= GRID_W, g, 0.0), GRID_W, 1)
        above = jnp.concatenate([jnp.zeros((GRID_W, 128), F32), high[:rows - GRID_W]], axis=0)
        diag = jnp.sum(_skew_rows(low + above, -1).reshape(RPB_HEADS * 16, GRID_W, 128), axis=1)
        o_ref[...] = pltpu.roll(diag, 15, 1).reshape(RPB_HEADS, 16, 128)

    return pl.pallas_call(
        body, out_shape=jax.ShapeDtypeStruct((N_HEADS, 16, 128), F32), grid=(N_HEADS // RPB_HEADS,),
        in_specs=[pl.BlockSpec((RPB_HEADS, 16, GRID_W, 128), lambda h: (h, 0, 0, 0))],
        out_specs=pl.BlockSpec((RPB_HEADS, 16, 128), lambda h: (h, 0, 0)),
        name="rpb_grad", compiler_params=_params())(gp)


NA_KEYS = NA_ROWS * GRID_W


def _na_window(i):
    first_row = jnp.clip(i - NA_ROWS // 2, 0, SEQ_ROWS - NA_ROWS)
    return pl.multiple_of(first_row * GRID_W, GRID_W), first_row - i + NA_ROWS - 1


NA_STEP = 32


def _head_masks():
    lane = _lane_index((1, 128))
    return (lane < HEAD_DIM, lane >= HEAD_DIM)


def _stack_heads(t, masks):
    zero = jnp.zeros_like(t)
    return jnp.concatenate([jnp.where(masks[0], t, zero), jnp.where(masks[1], t, zero)], axis=0)


def _unstack_heads(t, masks):
    n = t.shape[0] // 2
    return jnp.where(masks[0], t[:n], t[n:])


def _stack_columns(t):
    return jnp.concatenate([t[:, 0:1], t[:, HEAD_DIM:HEAD_DIM + 1]], axis=0)


def _na_scores(qs, kw, tp_ref, dr0):
    s = lax.dot_general(qs, kw, NT, preferred_element_type=F32)
    bias = jnp.concatenate(
        [jnp.concatenate([tp_ref[a, pl.ds(dr0 + 2 * c, 1), :, :].reshape(GRID_W, 128) for c in range(4)], axis=1)
         for a in range(2)], axis=0)
    return s + bias


def _na_specs():
    q_spec = pl.BlockSpec((NA_STEP * GRID_W, 128), lambda hp, i: (i, hp))
    k_spec = pl.BlockSpec((SEQ, 128), lambda hp, i: (0, 8 + hp))
    v_spec = pl.BlockSpec((SEQ, 128), lambda hp, i: (0, 16 + hp))
    tp_spec = pl.BlockSpec((2, 16, GRID_W, 128), lambda hp, i: (hp, 0, 0, 0))
    return q_spec, k_spec, v_spec, tp_spec


def _na_fwd(qkv, table):
    def body(q_ref, k_ref, v_ref, tp_ref, o_ref, lse_ref):
        masks = _head_masks()
        for r in range(NA_STEP):
            rows = slice(r * GRID_W, (r + 1) * GRID_W)
            start, dr0 = _na_window(pl.program_id(1) * NA_STEP + r)
            kw = k_ref[pl.ds(start, NA_KEYS), :]
            vw = v_ref[pl.ds(start, NA_KEYS), :]
            s = _na_scores(_stack_heads(q_ref[rows, :] * Q_SCALE, masks), kw, tp_ref, dr0)
            m = jnp.max(s, axis=-1, keepdims=True)
            p = jnp.exp(s - m)
            denom = jnp.sum(p, axis=-1, keepdims=True)
            out = jnp.dot(p.astype(BF16), vw, preferred_element_type=F32) / denom
            o_ref[rows, :] = _unstack_heads(out, masks).astype(o_ref.dtype)
            lse_ref[rows, :] = _unstack_heads(jnp.broadcast_to(m + jnp.log(denom), (2 * GRID_W, 128)), masks)

    q_spec, k_spec, v_spec, tp_spec = _na_specs()
    return pl.pallas_call(
        body, out_shape=(jax.ShapeDtypeStruct((SEQ, D_MODEL), BF16), jax.ShapeDtypeStruct((SEQ, D_MODEL), F32)),
        grid=(N_HEADS // 2, SEQ_ROWS // NA_STEP), in_specs=[q_spec, k_spec, v_spec, tp_spec],
        out_specs=(q_spec, q_spec), name="na_fwd", compiler_params=_params())(qkv, qkv, qkv, table)


def _na_bwd(qkv, table, d_out, lse):
    def body(q_ref, k_ref, v_ref, tp_ref, do_ref, lse_ref, dqkv_ref, gp_ref, dk_acc, dv_acc):
        step = pl.program_id(1)

        @pl.when(step == 0)
        def _():
            dk_acc[...] = jnp.zeros_like(dk_acc)
            dv_acc[...] = jnp.zeros_like(dv_acc)
            gp_ref[...] = jnp.zeros_like(gp_ref)

        masks = _head_masks()
        for r in range(NA_STEP):
            rows = slice(r * GRID_W, (r + 1) * GRID_W)
            i = step * NA_STEP + r
            start, dr0 = _na_window(i)
            kw = k_ref[pl.ds(start, NA_KEYS), :]
            vw = v_ref[pl.ds(start, NA_KEYS), :]
            qs = _stack_heads(q_ref[rows, :] * Q_SCALE, masks)
            dos = _stack_heads(do_ref[rows, :], masks)
            p = jnp.exp(_na_scores(qs, kw, tp_ref, dr0) - _stack_columns(lse_ref[rows, :]))
            dp = lax.dot_general(dos, vw, NT, preferred_element_type=F32)
            ds = p * (dp - jnp.sum(p * dp, axis=-1, keepdims=True))
            for a in range(2):
                for c in range(4):
                    gp_ref[a, pl.ds(dr0 + 2 * c, 1), :, :] += (
                        ds[a * GRID_W:(a + 1) * GRID_W, 128 * c:128 * (c + 1)].reshape(1, GRID_W, 128))
            dsb = ds.astype(BF16)
            dq = _unstack_heads(jnp.dot(dsb, kw, preferred_element_type=F32), masks) * Q_SCALE
            dqkv_ref[0, pl.ds(pl.multiple_of(i * GRID_W, GRID_W), GRID_W), :] = dq.astype(dqkv_ref.dtype)
            dk_acc[pl.ds(start, NA_KEYS), :] += lax.dot_general(dsb, qs, TN, preferred_element_type=F32)
            dv_acc[pl.ds(start, NA_KEYS), :] += lax.dot_general(p.astype(BF16), dos, TN, preferred_element_type=F32)

        @pl.when(step == SEQ_ROWS // NA_STEP - 1)
        def _():
            dqkv_ref[1] = dk_acc[...].astype(dqkv_ref.dtype)
            dqkv_ref[2] = dv_acc[...].astype(dqkv_ref.dtype)

    q_spec, k_spec, v_spec, tp_spec = _na_specs()
    return pl.pallas_call(
        body,
        out_shape=(jax.ShapeDtypeStruct((3, SEQ, D_MODEL), BF16), jax.ShapeDtypeStruct((N_HEADS, 16, GRID_W, 128), F32)),
        grid=(N_HEADS // 2, SEQ_ROWS // NA_STEP), in_specs=[q_spec, k_spec, v_spec, tp_spec, q_spec, q_spec],
        out_specs=(pl.BlockSpec((3, SEQ, 128), lambda hp, i: (0, 0, hp)), tp_spec),
        scratch_shapes=[pltpu.VMEM((SEQ, 128), F32), pltpu.VMEM((SEQ, 128), F32)],
        name="na_bwd", compiler_params=_params())(qkv, qkv, qkv, table, d_out, lse)


DIL_STEP = 16


def _dil_geometry(group):
    dil = DIL_GROUPS[group][1]
    sub_len = SEQ // dil
    blocks = sub_len // BAND
    return dil, sub_len, max(blocks // DIL_STEP, 1), max(DIL_STEP // blocks, 1), min(2 * BAND, sub_len)


def _dil_block(step, r, sub_len, subs):
    per_sub = DIL_STEP // subs
    return (r // per_sub) * sub_len, step * per_sub + r % per_sub


def _dil_window(b, sub_len, n_keys):
    if n_keys == sub_len:
        return 0
    return pl.multiple_of(jnp.clip(b * BAND - RADIUS, 0, sub_len - n_keys), RADIUS)


def _dil_bias(b, start, n_keys, slope_ref, hp):
    row = _row_index((2 * BAND, n_keys))
    qpos = b * BAND + (row & (BAND - 1))
    kpos = start + _lane_index((2 * BAND, n_keys))
    dist = jnp.abs(qpos - kpos)
    slope = jnp.where(row < BAND, slope_ref[2 * hp], slope_ref[2 * hp + 1])
    return slope * dist.astype(F32), dist <= RADIUS


def _dil_scores(qs, kw, penalty, valid):
    return jnp.where(valid, lax.dot_general(qs, kw, NT, preferred_element_type=F32) - penalty, NEG_INF)


def _dil_specs(group):
    dil, sub_len, steps, subs, _ = _dil_geometry(group)
    col = group * 24
    rows = DIL_STEP * BAND
    q_spec = pl.BlockSpec((rows, 128), lambda n, hp, b: (n * steps + b, col + hp))
    k_spec = pl.BlockSpec((subs * sub_len, 128), lambda n, hp, b: (n, col + 8 + hp))
    v_spec = pl.BlockSpec((subs * sub_len, 128), lambda n, hp, b: (n, col + 16 + hp))
    tile = pl.BlockSpec((rows, 128), lambda n, hp, b: (n * steps + b, hp))
    smem = pl.BlockSpec(memory_space=pltpu.SMEM)
    return (dil // subs, N_HEADS // 2, steps), q_spec, k_spec, v_spec, tile, smem


def _dil_fwd(group, qkv, slopes):
    _, sub_len, _, subs, n_keys = _dil_geometry(group)

    def body(q_ref, k_ref, v_ref, slope_ref, o_ref, lse_ref):
        hp = pl.program_id(1)
        masks = _head_masks()
        for r in range(DIL_STEP):
            rows = slice(r * BAND, (r + 1) * BAND)
            base, b = _dil_block(pl.program_id(2), r, sub_len, subs)
            start = _dil_window(b, sub_len, n_keys)
            kw = k_ref[pl.ds(base + start, n_keys), :]
            vw = v_ref[pl.ds(base + start, n_keys), :]
            penalty, valid = _dil_bias(b, start, n_keys, slope_ref, hp)
            s = _dil_scores(_stack_heads(q_ref[rows, :] * Q_SCALE, masks), kw, penalty, valid)
            m = jnp.max(s, axis=-1, keepdims=True)
            p = jnp.exp(s - m)
            denom = jnp.sum(p, axis=-1, keepdims=True)
            out = jnp.dot(p.astype(BF16), vw, preferred_element_type=F32) / denom
            o_ref[rows, :] = _unstack_heads(out, masks).astype(o_ref.dtype)
            lse_ref[rows, :] = _unstack_heads(jnp.broadcast_to(m + jnp.log(denom), (2 * BAND, 128)), masks)

    grid, q_spec, k_spec, v_spec, tile, smem = _dil_specs(group)
    return pl.pallas_call(
        body, out_shape=(jax.ShapeDtypeStruct((SEQ, D_MODEL), BF16), jax.ShapeDtypeStruct((SEQ, D_MODEL), F32)),
        grid=grid, in_specs=[q_spec, k_spec, v_spec, smem], out_specs=(tile, tile),
        name=f"dil_fwd_{group}", compiler_params=_params())(qkv, qkv, qkv, slopes)


def _dil_merge(outs, lses):
    n_sorted = len(SORTED)

    def body(*refs):
        o_refs, l_refs = refs[:3], refs[3:6]
        out_refs, lse_refs, scratch = refs[6:7 + n_sorted], refs[7 + n_sorted:8 + 2 * n_sorted], refs[-1]
        os_ = [o_refs[0][...]] + [_unsort_tile(scratch, o_refs[1 + j], dil) for j, dil in enumerate(SORTED)]
        ls = [l_refs[0][...]] + [_unsort_tile(scratch, l_refs[1 + j], dil) for j, dil in enumerate(SORTED)]
        m = jnp.maximum(jnp.maximum(ls[0], ls[1]), ls[2])
        es = [jnp.exp(v - m) for v in ls]
        total = es[0] + es[1] + es[2]
        merged = (es[0] * os_[0] + es[1] * os_[1] + es[2] * os_[2]) / total
        lse = m + jnp.log(total)
        out_refs[0][...] = merged
        lse_refs[0][...] = lse
        for j, dil in enumerate(SORTED):
            _sort_tile(scratch, merged, dil, out_refs[1 + j])
            _sort_tile(scratch, lse, dil, lse_refs[1 + j])

    tm = 256
    tile = pl.BlockSpec((tm, D_MODEL), lambda i: (i, 0))
    specs = [tile] + [_sorted_spec(dil, tm=tm) for dil in SORTED]
    shapes = [jax.ShapeDtypeStruct((SEQ, D_MODEL), F32)] + [jax.ShapeDtypeStruct((dil, SEQ // dil, D_MODEL), F32) for dil in SORTED]
    views = lambda ts: [ts[0]] + [_sorted_view(t, dil) for t, dil in zip(ts[1:], SORTED)]
    result = pl.pallas_call(
        body, out_shape=tuple(shapes * 2), grid=(SEQ // tm,), in_specs=specs * 2, out_specs=tuple(specs * 2),
        scratch_shapes=[_sort_scratch(tm)], name="dil_merge", compiler_params=_params())(*views(outs), *views(lses))
    flat = [t.reshape(SEQ, D_MODEL) for t in result]
    return flat[:1 + n_sorted], flat[1 + n_sorted:]


def _dil_bwd(group, qkv, slopes, d_out, out, lse_group, lse_total, into):
    _, sub_len, steps, subs, n_keys = _dil_geometry(group)

    def body(q_ref, k_ref, v_ref, slope_ref, do_ref, o_ref, lg_ref, lt_ref, into_ref, dqkv_ref, dk_acc, dv_acc):
        hp, step = pl.program_id(1), pl.program_id(2)

        @pl.when(step == 0)
        def _():
            dk_acc[...] = jnp.zeros_like(dk_acc)
            dv_acc[...] = jnp.zeros_like(dv_acc)

        masks = _head_masks()
        for r in range(DIL_STEP):
            rows = slice(r * BAND, (r + 1) * BAND)
            base, b = _dil_block(step, r, sub_len, subs)
            start = _dil_window(b, sub_len, n_keys)
            keys = pl.ds(base + start, n_keys)
            kw = k_ref[keys, :]
            vw = v_ref[keys, :]
            penalty, valid = _dil_bias(b, start, n_keys, slope_ref, hp)
            qs = _stack_heads(q_ref[rows, :] * Q_SCALE, masks)
            lse2 = lg_ref[rows, :]
            weight = jnp.exp(lse2 - lt_ref[rows, :])
            do2 = do_ref[rows, :]
            dogs = _stack_heads((weight * do2).astype(BF16), masks)
            delta = _stack_columns(weight) * jnp.sum(_stack_heads(do2 * o_ref[rows, :], masks), axis=-1, keepdims=True)
            p = jnp.exp(_dil_scores(qs, kw, penalty, valid) - _stack_columns(lse2))
            dp = lax.dot_general(dogs, vw, NT, preferred_element_type=F32)
            dsb = (p * (dp - delta)).astype(BF16)
            dq = _unstack_heads(jnp.dot(dsb, kw, preferred_element_type=F32), masks) * Q_SCALE
            dqkv_ref[0, pl.ds(pl.multiple_of(base + b * BAND, BAND), BAND), :] = dq.astype(dqkv_ref.dtype)
            dk_acc[keys, :] += lax.dot_general(dsb, qs, TN, preferred_element_type=F32)
            dv_acc[keys, :] += lax.dot_general(p.astype(BF16), dogs, TN, preferred_element_type=F32)

        @pl.when(step == steps - 1)
        def _():
            dqkv_ref[1] = dk_acc[...].astype(dqkv_ref.dtype)
            dqkv_ref[2] = dv_acc[...].astype(dqkv_ref.dtype)

    grid, q_spec, k_spec, v_spec, tile, smem = _dil_specs(group)
    return pl.pallas_call(
        body, out_shape=jax.ShapeDtypeStruct(into.shape, into.dtype), grid=grid,
        in_specs=[q_spec, k_spec, v_spec, smem, tile, tile, tile, tile, ANY],
        out_specs=pl.BlockSpec((3, subs * sub_len, 128), lambda n, hp, b: (group, n, hp)),
        scratch_shapes=[pltpu.VMEM((subs * sub_len, 128), F32), pltpu.VMEM((subs * sub_len, 128), F32)],
        input_output_aliases={8: 0}, name=f"dil_bwd_{group}", compiler_params=_params(),
    )(qkv, qkv, qkv, slopes, d_out, out, lse_group, lse_total, into)


def _accumulate_rows(acc_ref, i, first, part):
    rows = pl.ds(pl.multiple_of(i * TM_MM, TM_MM), TM_MM)

    @pl.when(first)
    def _():
        acc_ref[rows, :] = part

    @pl.when(jnp.logical_not(first))
    def _():
        acc_ref[rows, :] += part


def _ffn_specs():
    tile = pl.BlockSpec((TM_MM, D_MODEL), lambda d, i: (i, 0))
    gate = pl.BlockSpec((None, None, D_MODEL, FF_PAD), lambda d, i: (d, 0, 0, 0))
    up = pl.BlockSpec((None, None, D_MODEL, FF_PAD), lambda d, i: (d, 1, 0, 0))
    down = pl.BlockSpec((None, FF_PAD, D_MODEL), lambda d, i: (d, 0, 0))
    hidden = pl.BlockSpec((None, TM_MM, FF_PAD), lambda d, i: (d, i, 0))
    whole = pl.BlockSpec((SEQ, D_MODEL), lambda d, i: (0, 0))
    return tile, gate, up, down, hidden, whole


def _ffn_fwd(name, h, w_gu, w_down):
    def body(h_ref, wg_ref, wu_ref, wd_ref, f_ref, hg_ref, hu_ref, act_t_ref):
        both = jnp.dot(h_ref[...], jnp.concatenate([wg_ref[...], wu_ref[...]], axis=1), preferred_element_type=F32)
        hg, hu = both[:, :FF_PAD], both[:, FF_PAD:]
        act = hg * jax.nn.sigmoid(hg) * hu
        act_t_ref[...] = act.T.astype(act_t_ref.dtype)
        hg_ref[...] = hg.astype(hg_ref.dtype)
        hu_ref[...] = hu.astype(hu_ref.dtype)
        _accumulate_rows(f_ref, pl.program_id(1), pl.program_id(0) == 0,
                         jnp.dot(act.astype(BF16), wd_ref[...], preferred_element_type=F32))

    tile, gate, up, down, hidden, whole = _ffn_specs()
    shape = jax.ShapeDtypeStruct((N_DEV, SEQ, FF_PAD), BF16)
    return pl.pallas_call(
        body, out_shape=(jax.ShapeDtypeStruct((SEQ, D_MODEL), F32), shape, shape, jax.ShapeDtypeStruct((N_DEV, FF_PAD, SEQ), BF16)),
        grid=(N_DEV, SEQ // TM_MM), in_specs=[tile, gate, up, down],
        out_specs=(whole, hidden, hidden, pl.BlockSpec((None, FF_PAD, TM_MM), lambda d, i: (d, 0, i))),
        name=name, compiler_params=_params())(h, w_gu, w_gu, w_down)


def _ffn_dgu(name, h_t, dgu):
    def body(h_ref, dgu_ref, o_ref):
        both = jnp.dot(h_ref[...], jnp.concatenate([dgu_ref[0], dgu_ref[1]], axis=1), preferred_element_type=F32)
        o_ref[0] = both[:, :FF_PAD].astype(o_ref.dtype)
        o_ref[1] = both[:, FF_PAD:].astype(o_ref.dtype)

    return pl.pallas_call(
        body, out_shape=jax.ShapeDtypeStruct((N_DEV, 2, D_MODEL, FF_PAD), BF16), grid=(N_DEV,),
        in_specs=[pl.BlockSpec((D_MODEL, SEQ), lambda d: (0, 0)), pl.BlockSpec((2, None, SEQ, FF_PAD), lambda d: (0, d, 0, 0))],
        out_specs=pl.BlockSpec((None, 2, D_MODEL, FF_PAD), lambda d: (d, 0, 0, 0)),
        name=name, compiler_params=_params())(h_t, dgu)


def _ffn_bwd(name, d_f, w_gu, w_down, hg, hu, act_t):
    n_tiles = SEQ // TM_MM

    def body(df_ref, wg_ref, wu_ref, wd_ref, hg_ref, hu_ref, act_t_ref, dgu_ref, dh_ref, dwd_ref, dwd_acc):
        i = pl.program_id(1)
        d_down = jnp.dot(act_t_ref[...], df_ref[...], preferred_element_type=F32)

        @pl.when(i == 0)
        def _():
            dwd_acc[...] = d_down

        @pl.when(i > 0)
        def _():
            dwd_acc[...] += d_down

        @pl.when(i == n_tiles - 1)
        def _():
            dwd_ref[...] = dwd_acc[...].astype(dwd_ref.dtype)

        dact = lax.dot_general(df_ref[...], wd_ref[...], NT, preferred_element_type=F32)
        hgv = hg_ref[...].astype(F32)
        sig = jax.nn.sigmoid(hgv)
        d_gate = (dact * hu_ref[...].astype(F32) * (sig * (1.0 + hgv * (1.0 - sig)))).astype(BF16)
        d_up = (dact * hgv * sig).astype(BF16)
        dgu_ref[0] = d_gate
        dgu_ref[1] = d_up
        part = lax.dot_general(jnp.concatenate([d_gate, d_up], axis=1), jnp.concatenate([wg_ref[...], wu_ref[...]], axis=1),
                               NT, preferred_element_type=F32)
        _accumulate_rows(dh_ref, pl.program_id(1), pl.program_id(0) == 0, part)

    tile, gate, up, down, hidden, whole = _ffn_specs()
    return pl.pallas_call(
        body, out_shape=(jax.ShapeDtypeStruct((2, N_DEV, SEQ, FF_PAD), BF16), jax.ShapeDtypeStruct((SEQ, D_MODEL), F32),
                         jax.ShapeDtypeStruct((N_DEV, FF_PAD, D_MODEL), BF16)),
        grid=(N_DEV, n_tiles),
        in_specs=[tile, gate, up, down, hidden, hidden, pl.BlockSpec((None, FF_PAD, TM_MM), lambda d, i: (d, 0, i))],
        out_specs=(pl.BlockSpec((2, None, TM_MM, FF_PAD), lambda d, i: (0, d, i, 0)), whole, down),
        scratch_shapes=[pltpu.VMEM((FF_PAD, D_MODEL), F32)],
        name=name, compiler_params=_params())(d_f, w_gu, w_gu, w_down, hg, hu, act_t)


def _position():
    return lax.axis_index("x"), lax.axis_index("y"), lax.axis_index("c")


def _flat(p):
    return 4 * p[0] + 2 * p[1] + p[2]


def _peer(me, k):
    x, y, c = me
    return (1 - x if k & 4 else x, 1 - y if k & 2 else y, 1 - c if k & 1 else c)


def _columns(width):
    return lambda ref, d: ref.at[:, pl.ds(pl.multiple_of(d * width, 128), width)]


def _leading(ref, d):
    return ref.at[d]


def _by_distance(ref, sender, k):
    return ref.at[k - 1]


def _prep_weights(me, na_qkv, na_o, dil_qkv, dil_o, gate, up, down, land_shapes):
    na_cols, dil_cols = na_qkv.shape[-1], dil_qkv.shape[-1]
    o_rows = na_o.shape[1]
    tiles = 4
    rows, rows_o = D_MODEL // tiles, o_rows // tiles

    def body(me_ref, naq, nao, dq, do_, g0, u0, d0, g1, u1, d1, *outs):
        def put(t, index, value):
            outs[t][index] = value
            outs[8 + t][index] = value

        put(0, ..., naq[...].astype(BF16))
        put(1, ..., nao[...].astype(BF16))
        put(4, ..., dq[...].astype(BF16))
        put(5, ..., do_[...].astype(BF16))
        for t, (g, u, d) in ((2, (g0, u0, d0)), (6, (g1, u1, d1))):
            for j, part in enumerate((g, u)):
                put(t, (j, slice(None), slice(0, FF_SHARD)), part[...].astype(BF16))
                put(t, (j, slice(None), slice(FF_SHARD, FF_PAD)), jnp.zeros((rows, FF_PAD - FF_SHARD), BF16))
            put(t + 1, (slice(0, FF_SHARD), slice(None)), d[...].astype(BF16))
            put(t + 1, (slice(FF_SHARD, FF_PAD), slice(None)), jnp.zeros((FF_PAD - FF_SHARD, D_MODEL), BF16))

    def tiled(width):
        return pl.BlockSpec((None, rows, width), lambda i, me: (0, i, 0))

    def layer(l, width):
        return pl.BlockSpec((None, rows, width), lambda i, me: (l, i, 0))

    def whole_layer(l):
        return pl.BlockSpec((None, FF_SHARD, D_MODEL), lambda i, me: (l, 0, 0))

    in_specs = [tiled(na_cols), pl.BlockSpec((None, rows_o, D_MODEL), lambda i, me: (0, i, 0)), tiled(dil_cols),
                pl.BlockSpec((None, rows_o, D_MODEL), lambda i, me: (0, i, 0)),
                layer(0, FF_SHARD), layer(0, FF_SHARD), whole_layer(0), layer(1, FF_SHARD), layer(1, FF_SHARD), whole_layer(1)]
    o_shard = pl.BlockSpec((rows_o, D_MODEL), lambda i, me: (i, 0))
    o_land = pl.BlockSpec((None, rows_o, D_MODEL), lambda i, me: (me[0], i, 0))
    gu_shard = pl.BlockSpec((2, rows, FF_PAD), lambda i, me: (0, i, 0))
    gu_land = pl.BlockSpec((None, 2, rows, FF_PAD), lambda i, me: (me[0], 0, i, 0))
    down_shard = pl.BlockSpec((FF_PAD, D_MODEL), lambda i, me: (0, 0))
    down_land = pl.BlockSpec((None, FF_PAD, D_MODEL), lambda i, me: (me[0], 0, 0))

    def qkv_shard(width):
        return pl.BlockSpec((rows, width), lambda i, me: (i, 0))

    def qkv_land(width):
        return pl.BlockSpec((rows, width), lambda i, me: (i, me[0]))

    shard_specs = [qkv_shard(na_cols), o_shard, gu_shard, down_shard, qkv_shard(dil_cols), o_shard, gu_shard, down_shard]
    land_specs = [qkv_land(na_cols), o_land, gu_land, down_land, qkv_land(dil_cols), o_land, gu_land, down_land]
    shard_shapes = [jax.ShapeDtypeStruct(s, BF16) for s in
                    ((D_MODEL, na_cols), (o_rows, D_MODEL), (2, D_MODEL, FF_PAD), (FF_PAD, D_MODEL),
                     (D_MODEL, dil_cols), (o_rows, D_MODEL), (2, D_MODEL, FF_PAD), (FF_PAD, D_MODEL))]
    result = pl.pallas_call(
        body, out_shape=tuple(shard_shapes + list(land_shapes)),
        grid_spec=pltpu.PrefetchScalarGridSpec(num_scalar_prefetch=1, grid=(tiles,), in_specs=in_specs,
                                               out_specs=tuple(shard_specs + land_specs)),
        name="prep_weights", compiler_params=_params())(me, na_qkv, na_o, dil_qkv, dil_o, gate, up, down, gate, up, down)
    return list(result[:8]), list(result[8:])


def _remote_copies(sets, src_refs, land_refs, send_sems, recv_sems, outgoing):
    me = _position()
    copies = []
    for t, (si, src_of, li, dst_of) in enumerate(sets):
        for k in range(1, N_DEV):
            other = _peer(me, k)
            sender = me if outgoing else other
            copies.append(pltpu.make_async_remote_copy(
                src_ref=src_of(src_refs[si], _flat(other)), dst_ref=dst_of(land_refs[li], _flat(sender), k),
                send_sem=send_sems.at[(N_DEV - 1) * t + k - 1], recv_sem=recv_sems.at[(N_DEV - 1) * t + k - 1],
                device_id=other, device_id_type=MESH))
    return copies


def _send_start(name, srcs, lands, sets_by_group):
    n_src, n_land, n_groups = len(srcs), len(lands), len(sets_by_group)

    def body(*refs):
        src_refs, land_refs = refs[:n_src], refs[n_src:n_src + n_land]
        outs = refs[n_src + n_land:]
        for g, sets in enumerate(sets_by_group):
            for cp in _remote_copies(sets, src_refs, land_refs, outs[2 * g], outs[2 * g + 1], True):
                cp.start()
        outs[-1][...] = jnp.zeros_like(outs[-1])

    sem_shapes = []
    for sets in sets_by_group:
        sem_shapes += [pltpu.SemaphoreType.DMA((len(sets) * (N_DEV - 1),))] * 2
    thru = [pltpu.HBM(a.shape, a.dtype) for a in list(srcs) + list(lands)]
    n_sem = len(sem_shapes)
    result = pl.pallas_call(
        body, out_shape=tuple(sem_shapes + thru + [jax.ShapeDtypeStruct((8, 128), F32)]),
        in_specs=[HBM] * (n_src + n_land),
        out_specs=tuple([SEM] * n_sem + [HBM] * (n_src + n_land) + [pl.BlockSpec(memory_space=pltpu.VMEM)]),
        input_output_aliases={i: n_sem + i for i in range(n_src + n_land)},
        compiler_params=pltpu.CompilerParams(has_side_effects=EFFECT), name=name,
    )(*[pltpu.with_memory_space_constraint(a, pltpu.HBM) for a in list(srcs) + list(lands)])
    sems = [(result[2 * g], result[2 * g + 1]) for g in range(n_groups)]
    return sems, list(result[n_sem:n_sem + n_src]), list(result[n_sem + n_src:n_sem + n_src + n_land]), result[-1]


def _send_wait(name, sems, srcs, lands, sets, after):
    n_src, n_land = len(srcs), len(lands)

    def body(*refs):
        src_refs, land_refs = refs[:n_src], refs[n_src:n_src + n_land]
        send_sems, recv_sems = refs[n_src + n_land], refs[n_src + n_land + 1]
        for cp in _remote_copies(sets, src_refs, land_refs, send_sems, recv_sems, True):
            cp.wait_send()
        for cp in _remote_copies(sets, src_refs, land_refs, send_sems, recv_sems, False):
            cp.wait_recv()

    thru = [pltpu.HBM(a.shape, a.dtype) for a in list(srcs) + list(lands)]
    result = pl.pallas_call(
        body, out_shape=tuple(thru), in_specs=[HBM] * (n_src + n_land) + [SEM, SEM] + [ANY] * len(after),
        out_specs=tuple([HBM] * (n_src + n_land)), input_output_aliases={i: i for i in range(n_src + n_land)},
        compiler_params=pltpu.CompilerParams(has_side_effects=EFFECT), name=name,
    )(*srcs, *lands, sems[0], sems[1], *after)
    return list(result[:n_src]), list(result[n_src:])


DIRECT = (1, 2, 4, 6)
PASSED = DIRECT[1:]


def _hbm_passthrough(body, name, arrays, n_sem_in, sem_out_shapes, extra):
    n, n_out = len(arrays), len(sem_out_shapes)
    return pl.pallas_call(
        body, out_shape=tuple(list(sem_out_shapes) + [pltpu.HBM(a.shape, a.dtype) for a in arrays]),
        in_specs=[HBM] * n + [SEM] * n_sem_in + [ANY] * len(extra), out_specs=tuple([SEM] * n_out + [HBM] * n),
        input_output_aliases={i: n_out + i for i in range(n)},
        compiler_params=pltpu.CompilerParams(has_side_effects=EFFECT), name=name)


def _shard_copy(src_ref, land_ref, window, block, to, send_sem, recv_sem, from_shard):
    dst = window(land_ref, _flat(block))
    return pltpu.make_async_remote_copy(src_ref=src_ref if from_shard else dst, dst_ref=dst, send_sem=send_sem,
                                        recv_sem=recv_sem, device_id=to, device_id_type=MESH)


def _gather_start(name, shards, lands, windows, group_sizes):
    n = len(shards)

    def body(*refs):
        shard_refs, land_refs, outs = refs[:n], refs[n:2 * n], refs[2 * n:]
        me = _position()
        t = 0
        for g, size in enumerate(group_sizes):
            for local in range(size):
                for j, k in enumerate(DIRECT):
                    i = len(DIRECT) * local + j
                    _shard_copy(shard_refs[t], land_refs[t], windows[t], me, _peer(me, k), outs[2 * g].at[i],
                                outs[2 * g + 1].at[i], True).start()
                t += 1

    sem_shapes = [pltpu.SemaphoreType.DMA((len(DIRECT) * size,)) for size in group_sizes for _ in range(2)]
    arrays = [pltpu.with_memory_space_constraint(a, pltpu.HBM) for a in list(shards) + list(lands)]
    result = _hbm_passthrough(body, name, arrays, 0, sem_shapes, ())(*arrays)
    n_sem = len(sem_shapes)
    sems = [(result[2 * g], result[2 * g + 1]) for g in range(len(group_sizes))]
    return sems, list(result[n_sem:n_sem + n]), list(result[n_sem + n:])


def _gather_pass_on(name, sems, shards, lands, windows, after):
    n = len(shards)

    def body(*refs):
        shard_refs, land_refs = refs[:n], refs[n:2 * n]
        recv_sems = refs[2 * n + 1]
        pass_send, pass_recv = refs[2 * n + 2 + len(after)], refs[2 * n + 3 + len(after)]
        me = _position()
        sibling = _peer(me, 1)
        for t in range(n):
            for j, k in enumerate(PASSED):
                sender = _peer(me, k)
                arrived = len(DIRECT) * t + 1 + j
                _shard_copy(shard_refs[t], land_refs[t], windows[t], sender, me, refs[2 * n].at[arrived], recv_sems.at[arrived],
                            True).wait_recv()
                i = len(PASSED) * t + j
                _shard_copy(shard_refs[t], land_refs[t], windows[t], sender, sibling, pass_send.at[i], pass_recv.at[i],
                            False).start()

    sem_shapes = [pltpu.SemaphoreType.DMA((len(PASSED) * n,))] * 2
    result = _hbm_passthrough(body, name, list(shards) + list(lands), 2, sem_shapes, after)(
        *shards, *lands, sems[0], sems[1], *after)
    return (result[0], result[1]), list(result[2:2 + n]), list(result[2 + n:])


def _gather_wait(name, sems, pass_sems, shards, lands, windows, after):
    n = len(shards)

    def body(*refs):
        shard_refs, land_refs = refs[:n], refs[n:2 * n]
        send_sems, recv_sems, pass_send, pass_recv = refs[2 * n:2 * n + 4]
        me = _position()
        sibling = _peer(me, 1)
        for t in range(n):
            for j, k in enumerate(DIRECT):
                i = len(DIRECT) * t + j
                _shard_copy(shard_refs[t], land_refs[t], windows[t], me, _peer(me, k), send_sems.at[i], recv_sems.at[i],
                            True).wait_send()
            _shard_copy(shard_refs[t], land_refs[t], windows[t], sibling, me, send_sems.at[len(DIRECT) * t],
                        recv_sems.at[len(DIRECT) * t], True).wait_recv()
            for j, k in enumerate(PASSED):
                i = len(PASSED) * t + j
                _shard_copy(shard_refs[t], land_refs[t], windows[t], _peer(me, k), sibling, pass_send.at[i], pass_recv.at[i],
                            False).wait_send()
                _shard_copy(shard_refs[t], land_refs[t], windows[t], _peer(sibling, k), me, pass_send.at[i], pass_recv.at[i],
                            False).wait_recv()

    result = _hbm_passthrough(body, name, list(shards) + list(lands), 4, [], after)(
        *shards, *lands, sems[0], sems[1], pass_sems[0], pass_sems[1], *after)
    return list(result[n:])


def _all_gather(name, locals_, out_shapes, windows, deps=()):
    n = len(locals_)

    def body(*refs):
        src_refs, out_refs = refs[:n], refs[n + len(deps):2 * n + len(deps)]
        send_sems, recv_sems, local_sems = refs[2 * n + len(deps):]
        x, y, c = _position()
        me, sibling = (x, y, c), (x, y, 1 - c)
        chips = [(1 - x, y), (x, 1 - y), (1 - x, 1 - y)]

        def copy(t, k, block, to, from_local=False):
            dst = windows[t](out_refs[t], _flat(block))
            return pltpu.make_async_remote_copy(
                src_ref=src_refs[t] if from_local else dst, dst_ref=dst, send_sem=send_sems.at[t, k],
                recv_sem=recv_sems.at[t, k], device_id=to, device_id_type=MESH)

        mine = [pltpu.make_async_copy(src_refs[t], windows[t](out_refs[t], _flat(me)), local_sems.at[t]) for t in range(n)]
        sends = []
        for t in range(n):
            mine[t].start()
            sends.append(copy(t, 0, me, sibling, True))
            sends += [copy(t, 1 + j, me, (*chip, c), True) for j, chip in enumerate(chips)]
        for cp in sends:
            cp.start()
        for t in range(n):
            for j, chip in enumerate(chips):
                copy(t, 1 + j, (*chip, c), me).wait_recv()
                passed = copy(t, 4 + j, (*chip, c), sibling)
                passed.start()
                sends.append(passed)
        for t in range(n):
            copy(t, 0, sibling, me).wait_recv()
            for j, chip in enumerate(chips):
                copy(t, 4 + j, (*chip, 1 - c), me).wait_recv()
        for cp in sends:
            cp.wait_send()
        for cp in mine:
            cp.wait()

    return pl.pallas_call(
        body, out_shape=tuple(out_shapes), in_specs=[ANY] * (n + len(deps)), out_specs=tuple([ANY] * n),
        scratch_shapes=[pltpu.SemaphoreType.DMA((n, 7)), pltpu.SemaphoreType.DMA((n, 7)), pltpu.SemaphoreType.DMA((n,))],
        name=name)(*locals_, *deps)


def _adamw(name, me, lands, owns, w, m, v, *, grid, land_specs, own_specs, p_spec):
    n_land = len(lands)

    def body(me_ref, *refs):
        land_refs, own_refs = refs[:n_land], refs[n_land:n_land + len(owns)]
        w_ref, m_ref, v_ref, g_ref, delta_ref, m_out, v_out = refs[n_land + len(owns):]
        ncols = w_ref.shape[-1]
        sums = []
        for i, land_ref in enumerate(land_refs):
            g = own_refs[i][...].astype(F32) if owns else land_ref[0].astype(F32)
            for s in range(0 if owns else 1, land_ref.shape[0]):
                g = g + land_ref[s].astype(F32)
            sums.append(g[:, :ncols])
        g = sums[0] if n_land == 1 else jnp.where(pl.program_id(0) == 0, sums[0], sums[1])
        m_new = ADAM_B1 * m_ref[...] + (1.0 - ADAM_B1) * g
        v_new = ADAM_B2 * v_ref[...] + (1.0 - ADAM_B2) * jnp.square(g)
        m_hat = m_new / (1.0 - ADAM_B1 ** ADAM_STEP)
        v_hat = v_new / (1.0 - ADAM_B2 ** ADAM_STEP)
        g_ref[...] = g
        delta_ref[...] = -ADAM_LR * (m_hat / (jnp.sqrt(v_hat) + ADAM_EPS) + ADAM_WD * w_ref[...])
        m_out[...] = m_new
        v_out[...] = v_new

    shape = jax.ShapeDtypeStruct(w.shape, F32)
    return pl.pallas_call(
        body, out_shape=(shape,) * 4,
        grid_spec=pltpu.PrefetchScalarGridSpec(
            num_scalar_prefetch=1, grid=grid, in_specs=list(land_specs) + list(own_specs) + [p_spec, p_spec, p_spec],
            out_specs=(p_spec,) * 4),
        name=name, compiler_params=_params())(me, *lands, *owns, w, m, v)


def _row(p, layer):
    return p[layer][None, :]


def _square(name, a, b, dims, out_dtype, deps=()):
    if a.shape == (D_MODEL, SEQ):
        return _matmul(name, a, b, grid=(2, 1), a_spec=pl.BlockSpec((512, SEQ), lambda i, k: (i, 0)),
                       b_spec=pl.BlockSpec((SEQ, D_MODEL), lambda i, k: (0, 0)),
                       o_spec=pl.BlockSpec((512, D_MODEL), lambda i, k: (i, 0)),
                       out_shape=jax.ShapeDtypeStruct((D_MODEL, D_MODEL), out_dtype), dims=NN, acc_shape=(8, 128),
                       deps=deps)
    return _matmul(name, a, b, grid=(SEQ // TM_MM, 1), a_spec=pl.BlockSpec((TM_MM, D_MODEL), lambda i, k: (i, 0)),
                   b_spec=pl.BlockSpec((D_MODEL, D_MODEL), lambda i, k: (0, 0)),
                   o_spec=pl.BlockSpec((TM_MM, D_MODEL), lambda i, k: (i, 0)),
                   out_shape=jax.ShapeDtypeStruct((SEQ, D_MODEL), out_dtype), dims=dims, acc_shape=(8, 128), deps=deps)


def _grouped_matmul(name, a_list, b, *, n_tiles, a_block, b_spec, o_spec, out_shape):
    n_groups = len(a_list)

    def a_spec(g):
        def index(j, i):
            mine = j // 3
            return (jnp.where(mine == g, i, jnp.where(mine < g, 0, n_tiles - 1)), 0)
        return pl.BlockSpec(a_block, index)

    def body(*refs):
        b_ref, o_ref = refs[n_groups], refs[n_groups + 1]
        mine = pl.program_id(0) // 3
        for g in range(n_groups):
            @pl.when(mine == g)
            def _(g=g):
                o_ref[...] = jnp.dot(refs[g][...], b_ref[...], preferred_element_type=F32).astype(o_ref.dtype)

    return pl.pallas_call(
        body, out_shape=out_shape, grid=(3 * n_groups, n_tiles), in_specs=[a_spec(g) for g in range(n_groups)] + [b_spec],
        out_specs=o_spec, name=name, compiler_params=_params())(*a_list, b)


def _qkv_fwd(name, hs, w):
    return _grouped_matmul(name, hs, w, n_tiles=1, a_block=(SEQ, D_MODEL),
                           b_spec=pl.BlockSpec((D_MODEL, D_MODEL), lambda j, i: (0, j)),
                           o_spec=pl.BlockSpec((SEQ, D_MODEL), lambda j, i: (i, j)),
                           out_shape=jax.ShapeDtypeStruct((SEQ, 3 * len(hs) * D_MODEL), BF16))


def _qkv_dw(name, hs_t, dqkv):
    return _grouped_matmul(name, hs_t, dqkv, n_tiles=2, a_block=(512, SEQ),
                           b_spec=pl.BlockSpec((None, SEQ, D_MODEL), lambda j, i: (j, 0, 0)),
                           o_spec=pl.BlockSpec((512, D_MODEL), lambda j, i: (i, j)),
                           out_shape=jax.ShapeDtypeStruct((D_MODEL, 3 * len(hs_t) * D_MODEL), BF16))


def _proj_do_sorted(name, d_a, w_o):
    def body(da_ref, w_ref, *refs):
        value = lax.dot_general(da_ref[...], w_ref[...], NT, preferred_element_type=F32)
        refs[0][...] = value
        for j, dil in enumerate(SORTED):
            _sort_tile(refs[-1], value, dil, refs[1 + j])

    tile = pl.BlockSpec((TM, D_MODEL), lambda i: (i, 0))
    shapes = [jax.ShapeDtypeStruct((SEQ, D_MODEL), F32)] + [jax.ShapeDtypeStruct((dil, SEQ // dil, D_MODEL), F32) for dil in SORTED]
    result = pl.pallas_call(
        body, out_shape=tuple(shapes), grid=(SEQ // TM,),
        in_specs=[tile, pl.BlockSpec((D_MODEL, D_MODEL), lambda i: (0, 0))],
        out_specs=tuple([tile] + [_sorted_spec(dil) for dil in SORTED]), scratch_shapes=[_sort_scratch()],
        name=name, compiler_params=_params())(d_a, w_o)
    return [t.reshape(SEQ, D_MODEL) for t in result]


def _qkv_dh(name, dqkv, w, n_chunks, deps):
    tm = TM_MM
    return _matmul(name, dqkv, w, grid=(n_chunks // 3, SEQ // tm, 1),
                   a_spec=pl.BlockSpec((3, tm, D_MODEL), lambda g, i, k: (g, i, 0)),
                   b_spec=pl.BlockSpec((D_MODEL, 3 * D_MODEL), lambda g, i, k: (0, g)),
                   o_spec=pl.BlockSpec((None, tm, D_MODEL), lambda g, i, k: (g, i, 0)),
                   out_shape=jax.ShapeDtypeStruct((n_chunks // 3, SEQ, D_MODEL), F32), dims=NT, acc_shape=(8, 128),
                   deps=deps, inner=3)


def _local_step(x, target, norms, rpb, fetch, emit, deps):
    mix_pre, mix_post, ffn_pre, ffn_post = norms
    slopes = 2.0 ** (-8.0 * jnp.arange(1, N_HEADS + 1, dtype=F32) / N_HEADS)
    rpb_pad = jnp.pad(rpb, ((0, 0), (0, 1), (0, 128 - 31)))
    saved = []

    hs = [_rms_fwd("l0_norm_mix", x, _row(mix_pre, 0), out_dtype=BF16, deps=deps)]
    hs_t0 = None
    for layer in range(2):
        tag = f"l{layer}"
        if layer == 0:
            table = _rpb_table(rpb_pad)
            w_qkv, w_o = fetch("na", [table, hs[0]], [hs[0]])
            qkv = _qkv_fwd(tag + "_qkv", hs, w_qkv)
            o, lse = _na_fwd(qkv, table)
            mixer = (hs, qkv, o, lse, table)
        else:
            w_qkv, w_o = fetch("dil", [saved[0][7]], [hs[0]])
            qkv = _qkv_fwd(tag + "_qkv", hs, w_qkv)
            outs, lses = zip(*[_dil_fwd(g, qkv, slopes * dil) for g, (_, dil) in enumerate(DIL_GROUPS)])
            merged, lse_total = _dil_merge(outs, lses)
            o = merged[0]
            mixer = (hs, qkv, merged, lses, lse_total)
        a = _square(tag + "_proj", o, w_o, NN, F32)
        x1, (h2, h2_t) = _post_norm_fwd(tag + "_post_mix", a, _row(mix_post, layer), x, _row(ffn_pre, layer))
        w_gu, w_down = fetch(f"ffn{layer}", [a], [h2])
        f, hg, hu, act_t = _ffn_fwd(tag + "_ffn", h2, w_gu, w_down)
        transposed = ([hs_t0 if hs_t0 is not None and t is hs[0] else t.T for t in hs], o.astype(BF16).T, h2_t, act_t)
        saved.append((x, mixer, a, x1, transposed, hg, hu, f, w_qkv, w_o, w_gu, w_down))
        if layer == 0:
            x, (h, hs_t0, *views) = _post_norm_fwd(tag + "_post_ffn", f, _row(ffn_post, 0), x1, _row(mix_pre, 1), sorted_too=True)
            hs = [h] + [t.reshape(SEQ, D_MODEL) for t in views]
        else:
            x = _rms_fwd(tag + "_post_ffn", f, _row(ffn_post, layer), res=x1)

    dx, loss = _loss_head("loss_head", x, target)
    d_norm = {k: [None, None] for k in ("mix_pre", "mix_post", "ffn_pre", "ffn_post")}
    d_rpb = None

    d_f, d_norm["ffn_post"][1] = _rms_bwd("b1_post_ffn", saved[1][7], _row(ffn_post, 1), [dx], out_dtype=BF16)
    for layer in (1, 0):
        tag = f"b{layer}"
        x0, mixer, a, x1, (h_t, o_t, h2_t, act_t), hg, hu, f, w_qkv, w_o, w_gu, w_down = saved[layer]
        dgu, d_h2, d_down = _ffn_bwd(tag + "_ffn", d_f, w_gu, w_down, hg, hu, act_t)
        d_gu = _ffn_dgu(tag + "_ffn_dgu", h2_t, dgu)
        sent = emit(f"ffn{layer}", [d_gu, d_down])
        dx1, d_a, d_norm["ffn_pre"][layer], d_norm["mix_post"][layer] = _norm_post_bwd(
            tag + "_norm_ffn", x1, _row(ffn_pre, layer), [d_h2], dx, a, _row(mix_post, layer), deps=sent)
        d_wo = _square(tag + "_proj_dw", o_t, d_a, NN, BF16)
        if layer == 0:
            _, qkv, o, lse, table = mixer
            d_o = _square(tag + "_proj_do", d_a, w_o, NT, BF16)
            dqkv, gp = _na_bwd(qkv, table, d_o, lse)
            d_rpb = _rpb_grad(gp)[:, :15, :31]
            sent = emit("na", [_qkv_dw(tag + "_qkv_dw", h_t, dqkv), d_wo])
            d_h = _qkv_dh(tag + "_qkv_dh", dqkv, w_qkv, 3, sent)
            dx, d_norm["mix_pre"][layer] = _rms_bwd(tag + "_norm_mix", x0, _row(mix_pre, layer), [d_h[0]], res=dx1)
        else:
            _, qkv, merged, lses, lse_total = mixer
            d_o = _proj_do_sorted(tag + "_proj_do", d_a, w_o)
            dqkv = lax.empty((3 * len(DIL_GROUPS), SEQ, D_MODEL), BF16)
            for g, (_, dil) in enumerate(DIL_GROUPS):
                dqkv = _dil_bwd(g, qkv, slopes * dil, d_o[g], merged[g], lses[g], lse_total[g], dqkv)
            sent = emit("dil", [_qkv_dw(tag + "_qkv_dw", h_t, dqkv), d_wo])
            d_h = _qkv_dh(tag + "_qkv_dh", dqkv, w_qkv, 9, sent)
            dx, d_f, d_norm["mix_pre"][1], d_norm["ffn_post"][0] = _norm_post_bwd(
                tag + "_norm_mix", x0, _row(mix_pre, 1), None, dx1, saved[0][7], _row(ffn_post, 0), groups=d_h)

    d_gains = [jnp.concatenate(d_norm[k], axis=0) for k in ("mix_pre", "mix_post", "ffn_pre", "ffn_post")]
    return loss, dx, d_gains, d_rpb


RPB_SIZE = N_HEADS * 15 * 31


def _pack_small(gains, rpb, last=None):
    top = jnp.concatenate(gains, axis=0).reshape(64, 128)
    bottom = jnp.pad(rpb.reshape(-1), (0, 64 * 128 - RPB_SIZE))
    if last is not None:
        bottom = bottom + jnp.pad(last.reshape(1), (64 * 128 - 1, 0))
    return jnp.concatenate([top, bottom.reshape(64, 128)], axis=0)


def _unpack_small(p):
    gains = p[:64].reshape(4, 2, D_MODEL)
    rpb = p[64:].reshape(-1)[:RPB_SIZE].reshape(1, N_HEADS, 15, 31)
    return [gains[i] for i in range(4)], rpb


GROUPS = ("na", "ffn0", "dil", "ffn1")


def kernel(x, norm_mix_pre, norm_mix_post, norm_ffn_pre, norm_ffn_post, na_w_qkv, na_w_o, na_rpb, dil_w_qkv, dil_w_o, ffn_w_gate, ffn_w_up, ffn_w_down, loss_target, m_norm_mix_pre, m_norm_mix_post, m_norm_ffn_pre, m_norm_ffn_post, m_na_w_qkv, m_na_w_o, m_na_rpb, m_dil_w_qkv, m_dil_w_o, m_ffn_w_gate, m_ffn_w_up, m_ffn_w_down, v_norm_mix_pre, v_norm_mix_post, v_norm_ffn_pre, v_norm_ffn_post, v_na_w_qkv, v_na_w_o, v_na_rpb, v_dil_w_qkv, v_dil_w_o, v_ffn_w_gate, v_ffn_w_up, v_ffn_w_down):
    na_cols, dil_cols, o_rows = 3 * D_MODEL // N_DEV, 9 * D_MODEL // N_DEV, D_MODEL // N_DEV
    ff_pad = FF_PAD - FF_SHARD
    me = (4 * lax.axis_index("x") + 2 * lax.axis_index("y") + lax.axis_index("c")).astype(jnp.int32).reshape(1)

    full = {
        "na": [((D_MODEL, 3 * D_MODEL), _columns(na_cols)), ((N_DEV, o_rows, D_MODEL), _leading)],
        "dil": [((D_MODEL, 9 * D_MODEL), _columns(dil_cols)), ((N_DEV, o_rows, D_MODEL), _leading)],
        "ffn0": [((N_DEV, 2, D_MODEL, FF_PAD), _leading), ((N_DEV, FF_PAD, D_MODEL), _leading)],
        "ffn1": [((N_DEV, 2, D_MODEL, FF_PAD), _leading), ((N_DEV, FF_PAD, D_MODEL), _leading)],
    }
    block = {
        "na": [(D_MODEL, na_cols), (o_rows, D_MODEL)], "dil": [(D_MODEL, dil_cols), (o_rows, D_MODEL)],
        "ffn0": [(2, D_MODEL, FF_PAD), (FF_PAD, D_MODEL)], "ffn1": [(2, D_MODEL, FF_PAD), (FF_PAD, D_MODEL)],
    }

    land_shapes = [jax.ShapeDtypeStruct(full[g][t][0], BF16) for g in GROUPS for t in range(2)]
    windows = [full[g][t][1] for g in GROUPS for t in range(2)]
    shards, lands = _prep_weights(me, na_w_qkv, na_w_o, dil_w_qkv, dil_w_o, ffn_w_gate, ffn_w_up, ffn_w_down, land_shapes)
    sems, shards, lands = _gather_start("gather_start", shards, lands, windows, [2] * len(GROUPS))

    def fetch(group, early, late):
        gi = GROUPS.index(group)
        mine = slice(2 * gi, 2 * gi + 2)
        pass_sems, shards_g, lands_g = _gather_pass_on(f"gather_pass_{group}", sems[gi], shards[mine], lands[mine],
                                                       windows[mine], early)
        qkv, o = _gather_wait(f"gather_wait_{group}", sems[gi], pass_sems, shards_g, lands_g, windows[mine], late)
        return (qkv, o.reshape(D_MODEL, D_MODEL)) if group in ("na", "dil") else (qkv, o)

    def grad_source(group, t):
        return _columns(block[group][0][1]) if (group in ("na", "dil") and t == 0) else _leading

    in_flight = {}

    def emit(group, grads):
        if group in ("na", "dil"):
            grads = [grads[0], grads[1].reshape(N_DEV, o_rows, D_MODEL)]
        sets = [(t, grad_source(group, t), t, _by_distance) for t in range(2)]
        landing = [lax.empty((N_DEV - 1,) + block[group][t], BF16) for t in range(2)]
        sems_g, grads, landing, tok = _send_start(f"exchange_start_{group}", grads, landing, [sets])
        in_flight[group] = (sems_g[0], grads, landing, sets)
        return [tok]

    norms = (norm_mix_pre, norm_mix_post, norm_ffn_pre, norm_ffn_post)
    loss, grad_x, d_gains, d_rpb = _local_step(x[0], loss_target[0], norms, na_rpb[0], fetch, emit, [shards[0]])

    landed, sent = {}, {}

    def wait_for(group, after):
        sems_g, grads, landing, sets = in_flight[group]
        sent[group], landed[group] = _send_wait(f"exchange_wait_{group}", sems_g, grads, landing, sets, after)

    for group in ("ffn1", "dil", "ffn0"):
        wait_for(group, [grad_x])

    def one(rows, tile, ncols, columns):
        own = (pl.BlockSpec((tile, ncols), lambda i, me: (i, me[0])) if columns
               else pl.BlockSpec((None, tile, ncols), lambda i, me: (me[0], i, 0)))
        return dict(grid=(rows // tile,), land_specs=[pl.BlockSpec((N_DEV - 1, tile, ncols), lambda i, me: (0, i, 0))],
                    own_specs=[own], p_spec=pl.BlockSpec((None, tile, ncols), lambda i, me: (0, i, 0)))

    def layered(block_shape, index, p_block, n_tiles):
        def specs(lead_size, lead):
            shape = (lead_size,) + block_shape
            return [pl.BlockSpec(shape, lambda l, r, me: index(lead(me), jnp.where(l == 0, r, n_tiles - 1))),
                    pl.BlockSpec(shape, lambda l, r, me: index(lead(me), jnp.where(l == 0, 0, r)))]
        return dict(grid=(2, n_tiles), land_specs=specs(N_DEV - 1, lambda me: 0), own_specs=specs(None, lambda me: me[0]),
                    p_spec=pl.BlockSpec(p_block, lambda l, r, me: (l, r, 0)))

    gu_lands, gu_owns = [landed["ffn0"][0], landed["ffn1"][0]], [sent["ffn0"][0], sent["ffn1"][0]]
    down_lands, down_owns = [landed["ffn0"][1], landed["ffn1"][1]], [sent["ffn0"][1], sent["ffn1"][1]]
    updates = {
        "dil_w_qkv": _adamw("adamw_dil_qkv", me, [landed["dil"][0]], [sent["dil"][0]], dil_w_qkv, m_dil_w_qkv, v_dil_w_qkv,
                            **one(D_MODEL, 128, dil_cols, True)),
        "dil_w_o": _adamw("adamw_dil_o", me, [landed["dil"][1]], [sent["dil"][1]], dil_w_o, m_dil_w_o, v_dil_w_o,
                          **one(o_rows, o_rows, D_MODEL, False)),
        "ffn_w_gate": _adamw("adamw_gate", me, gu_lands, gu_owns, ffn_w_gate, m_ffn_w_gate, v_ffn_w_gate,
                             **layered((None, 128, FF_PAD), lambda lead, r: (lead, 0, r, 0), (None, 128, FF_SHARD), 8)),
        "ffn_w_up": _adamw("adamw_up", me, gu_lands, gu_owns, ffn_w_up, m_ffn_w_up, v_ffn_w_up,
                           **layered((None, 128, FF_PAD), lambda lead, r: (lead, 1, r, 0), (None, 128, FF_SHARD), 8)),
        "ffn_w_down": _adamw("adamw_down", me, down_lands, down_owns, ffn_w_down, m_ffn_w_down, v_ffn_w_down,
                             **layered((176, D_MODEL), lambda lead, r: (lead, r, 0), (None, 176, D_MODEL), 2)),
    }
    done = [u[0] for u in updates.values()]
    small = _all_gather("gather_small", [_pack_small(d_gains, d_rpb, loss)], [jax.ShapeDtypeStruct((N_DEV, 128, 128), F32)],
                        [_leading], deps=done)[0]
    wait_for("na", [small])
    updates["na_w_qkv"] = _adamw("adamw_na_qkv", me, [landed["na"][0]], [sent["na"][0]], na_w_qkv, m_na_w_qkv, v_na_w_qkv,
                                 **one(D_MODEL, 256, na_cols, True))
    updates["na_w_o"] = _adamw("adamw_na_o", me, [landed["na"][1]], [sent["na"][1]], na_w_o, m_na_w_o, v_na_w_o,
                               **one(o_rows, o_rows, D_MODEL, False))
    gains = [norm_mix_pre, norm_mix_post, norm_ffn_pre, norm_ffn_post]
    m_gains = [m_norm_mix_pre, m_norm_mix_post, m_norm_ffn_pre, m_norm_ffn_post]
    v_gains = [v_norm_mix_pre, v_norm_mix_post, v_norm_ffn_pre, v_norm_ffn_post]
    packed = _adamw("adamw_small", me, [small], (), _pack_small(gains, na_rpb)[None], _pack_small(m_gains, m_na_rpb)[None],
                    _pack_small(v_gains, v_na_rpb)[None], grid=(1,),
                    land_specs=[pl.BlockSpec((N_DEV, 128, 128), lambda i, me: (0, 0, 0))], own_specs=[],
                    p_spec=pl.BlockSpec((None, 128, 128), lambda i, me: (0, 0, 0)))
    small_out = [_unpack_small(p[0]) for p in packed]

    order = ["na_w_qkv", "na_w_o", "na_rpb", "dil_w_qkv", "dil_w_o", "ffn_w_gate", "ffn_w_up", "ffn_w_down"]
    result = [packed[0][0, 127, 127], grad_x[None]]
    for kind in range(4):
        gains_k, rpb_k = small_out[kind]
        result += gains_k
        result += [rpb_k if name == "na_rpb" else updates[name][kind] for name in order]
    return tuple(result)
```
